```python
import math
import jax, jax.numpy as jnp
from jax import lax
import numpy as np

D_MODEL = 1024
BATCH = 8
SEQ = 4096
DEPTH = 1

HGRN_EXPAND = 128
HGRN_HEADS = D_MODEL // HGRN_EXPAND
HGRN_DK = HGRN_EXPAND
HGRN_DV = D_MODEL // HGRN_HEADS
HGRN_K_WIDTH = HGRN_HEADS * HGRN_DK
HGRN_V_WIDTH = HGRN_HEADS * HGRN_DV
CHUNK = 64
CONV_CH = D_MODEL
CONV_K = 31
D_FF = 2816
FFN_RESIDUAL = 0.5
N_MOD = 9
EPS = 1e-6
IN_SIZES = (HGRN_K_WIDTH, HGRN_K_WIDTH, HGRN_V_WIDTH, HGRN_V_WIDTH, 2 * CONV_CH, D_MODEL, D_MODEL)
IN_WIDTH = sum(IN_SIZES)
IN_SPLITS = tuple(int(s) for s in np.cumsum(IN_SIZES)[:-1])

kernel_name = "hybrid_hgrn2_conformer_macaron_adaln"


def rms_norm(x, g):
    xf = x.astype(jnp.float32)
    y = xf * lax.rsqrt(jnp.mean(xf * xf, axis=-1, keepdims=True) + EPS)
    return (y * g.astype(jnp.float32)).astype(x.dtype)


def layer_norm(x, g, b):
    xf = x.astype(jnp.float32)
    mu = jnp.mean(xf, axis=-1, keepdims=True)
    xc = xf - mu
    y = xc * lax.rsqrt(jnp.mean(xc * xc, axis=-1, keepdims=True) + EPS)
    return (y * g.astype(jnp.float32) + b.astype(jnp.float32)).astype(x.dtype)


def swiglu(h, w_in, w_out):
    a, b = jnp.split(h @ w_in, 2, axis=-1)
    return (jax.nn.silu(a) * b) @ w_out


def hgrn2_chunked(q, k, v, logf):
    B, S, H, DK = q.shape
    DV = v.shape[-1]
    nc = S // CHUNK

    def to_chunks(t):
        return t.reshape(B, nc, CHUNK, H, t.shape[-1]).transpose(1, 0, 3, 2, 4)

    causal = jnp.tril(jnp.ones((CHUNK, CHUNK), dtype=bool))

    def step(state, inp):
        qc, kc, vc, gc = inp
        b = jnp.cumsum(gc, axis=-2)
        diff = b[:, :, :, None, :] - b[:, :, None, :, :]
        decay = jnp.exp(jnp.where(causal[:, :, None], diff, -jnp.inf))
        att = jnp.einsum('bhtd,bhsd,bhtsd->bhts', qc, kc, decay)
        o_intra = jnp.einsum('bhts,bhsv->bhtv', att, vc)
        o_inter = jnp.einsum('bhtd,bhdv->bhtv', qc * jnp.exp(b), state)
        b_last = b[:, :, -1, :]
        k_dec = kc * jnp.exp(b_last[:, :, None, :] - b)
        new_state = jnp.exp(b_last)[..., None] * state + jnp.einsum('bhsd,bhsv->bhdv', k_dec, vc)
        return new_state, o_intra + o_inter

    state0 = jnp.zeros((B, H, DK, DV), jnp.float32)
    _, o = lax.scan(step, state0, (to_chunks(q), to_chunks(k), to_chunks(v), to_chunks(logf)))
    return o.transpose(1, 0, 3, 2, 4).reshape(B, S, H, DV)


def causal_depthwise_conv(u, w, b):
    y = lax.conv_general_dilated(
        u, w[:, None, :].astype(u.dtype), window_strides=(1,), padding=[(CONV_K - 1, 0)],
        dimension_numbers=('NWC', 'WIO', 'NWC'), feature_group_count=u.shape[-1])
    return y + b


def token_mixer(h, lb, w_in, hgrn_g, hgrn_w_o, conv_w, conv_b, conv_ln_g, conv_ln_b, conv_w_o, w_out):
    B, S, _ = h.shape
    f32 = jnp.float32
    q, f, i, og, u, ga, gb = jnp.split(h @ w_in, IN_SPLITS, axis=-1)
    q = (jax.nn.silu(q.astype(f32)) * (HGRN_DK ** -0.5)).reshape(B, S, HGRN_HEADS, HGRN_DK)
    fg = lb + (1.0 - lb) * jax.nn.sigmoid(f.astype(f32))
    logf = jnp.log(fg).reshape(B, S, HGRN_HEADS, HGRN_DK)
    k = (1.0 - fg).reshape(B, S, HGRN_HEADS, HGRN_DK)
    v = i.astype(f32).reshape(B, S, HGRN_HEADS, HGRN_DV)
    o = hgrn2_chunked(q, k, v, logf)
    o = o * lax.rsqrt(jnp.mean(o * o, axis=-1, keepdims=True) + EPS)
    o = o * hgrn_g.astype(f32).reshape(HGRN_HEADS, HGRN_DV)
    o = (o.reshape(B, S, HGRN_V_WIDTH) * jax.nn.silu(og.astype(f32))).astype(h.dtype)
    y_a = o @ hgrn_w_o
    ua, ub = jnp.split(u, 2, axis=-1)
    u = ua * jax.nn.sigmoid(ub)
    u = causal_depthwise_conv(u, conv_w, conv_b)
    u = jax.nn.silu(layer_norm(u, conv_ln_g, conv_ln_b))
    y_b = u @ conv_w_o
    merged = jax.nn.sigmoid(ga) * y_a + jax.nn.sigmoid(gb) * y_b
    return merged @ w_out


def _fwd_setup_inputs(seed: int = 0) -> dict:
    key = jax.random.key(seed)
    ks = jax.random.split(key, 24)
    D, L = D_MODEL, DEPTH
    nrm = lambda k, shape, fan_in: jax.random.normal(k, shape, jnp.float32) * (fan_in ** -0.5)
    gain = lambda k, shape: 1.0 + 0.02 * jax.random.normal(k, shape, jnp.float32)
    small = lambda k, shape: 0.02 * jax.random.normal(k, shape, jnp.float32)
    return {
        "x": jax.random.normal(ks[0], (BATCH, SEQ, D), jnp.float32),
        "c": jax.random.normal(ks[1], (BATCH, D), jnp.float32),
        "ada_w": nrm(ks[2], (L, D, N_MOD * D), D),
        "ada_b": small(ks[3], (L, N_MOD * D)),
        "norm_ffn1": gain(ks[4], (L, D)),
        "ffn1_w_in": nrm(ks[5], (L, D, 2 * D_FF), D),
        "ffn1_w_out": nrm(ks[6], (L, D_FF, D), D_FF),
        "norm_mix": gain(ks[7], (L, D)),
        "mix_w_in": nrm(ks[8], (L, D, IN_WIDTH), D),
        "hgrn_lb": 0.1 * jax.random.normal(ks[9], (L + 1, HGRN_K_WIDTH), jnp.float32),
        "hgrn_g": gain(ks[10], (L, HGRN_V_WIDTH)),
        "hgrn_w_o": nrm(ks[11], (L, HGRN_V_WIDTH, D), HGRN_V_WIDTH),
        "conv_w": nrm(ks[12], (L, CONV_K, CONV_CH), CONV_K),
        "conv_b": small(ks[13], (L, CONV_CH)),
        "conv_ln_g": gain(ks[14], (L, CONV_CH)),
        "conv_ln_b": small(ks[15], (L, CONV_CH)),
        "conv_w_o": nrm(ks[16], (L, CONV_CH, D), CONV_CH),
        "mix_w_out": nrm(ks[17], (L, D, D), D),
        "norm_ffn2": gain(ks[18], (L, D)),
        "ffn2_w_in": nrm(ks[19], (L, D, 2 * D_FF), D),
        "ffn2_w_out": nrm(ks[20], (L, D_FF, D), D_FF),
        "norm_final": gain(ks[21], (D,)),
    }


def _fwd_reference(x, c, ada_w, ada_b, norm_ffn1, ffn1_w_in, ffn1_w_out, norm_mix, mix_w_in,
              hgrn_lb, hgrn_g, hgrn_w_o, conv_w, conv_b, conv_ln_g, conv_ln_b, conv_w_o,
              mix_w_out, norm_ffn2, ffn2_w_in, ffn2_w_out, norm_final):
    B = x.shape[0]
    lb_all = jnp.cumsum(jax.nn.softmax(hgrn_lb.astype(jnp.float32), axis=0), axis=0)
    cs = jax.nn.silu(c)
    for l in range(DEPTH):
        mod = (cs @ ada_w[l] + ada_b[l]).reshape(B, N_MOD, D_MODEL)
        sh1, sc1, g1, sh2, sc2, g2, sh3, sc3, g3 = [mod[:, j, None, :] for j in range(N_MOD)]
        h = rms_norm(x, norm_ffn1[l]) * (1.0 + sc1) + sh1
        x = x + FFN_RESIDUAL * g1 * swiglu(h, ffn1_w_in[l], ffn1_w_out[l])
        h = rms_norm(x, norm_mix[l]) * (1.0 + sc2) + sh2
        x = x + g2 * token_mixer(h, lb_all[l], mix_w_in[l], hgrn_g[l], hgrn_w_o[l], conv_w[l],
                                 conv_b[l], conv_ln_g[l], conv_ln_b[l], conv_w_o[l], mix_w_out[l])
        h = rms_norm(x, norm_ffn2[l]) * (1.0 + sc3) + sh3
        x = x + FFN_RESIDUAL * g3 * swiglu(h, ffn2_w_in[l], ffn2_w_out[l])
    return rms_norm(x, norm_final)


import jax as _jax
import jax.numpy as _jnp

TWIN_FORMAT = 'train_step'
FWD_PARAMS = ['x', 'c', 'ada_w', 'ada_b', 'norm_ffn1', 'ffn1_w_in', 'ffn1_w_out', 'norm_mix', 'mix_w_in', 'hgrn_lb', 'hgrn_g', 'hgrn_w_o', 'conv_w', 'conv_b', 'conv_ln_g', 'conv_ln_b', 'conv_w_o', 'mix_w_out', 'norm_ffn2', 'ffn2_w_in', 'ffn2_w_out', 'norm_final']
TWIN_WEIGHTS = ['ada_w', 'ada_b', 'norm_ffn1', 'ffn1_w_in', 'ffn1_w_out', 'norm_mix', 'mix_w_in', 'hgrn_lb', 'hgrn_g', 'hgrn_w_o', 'conv_w', 'conv_b', 'conv_ln_g', 'conv_ln_b', 'conv_w_o', 'mix_w_out', 'norm_ffn2', 'ffn2_w_in', 'ffn2_w_out', 'norm_final']
TWIN_DIFF_INPUT = 'x'
TWIN_INPUTS = ['x', 'c', 'ada_w', 'ada_b', 'norm_ffn1', 'ffn1_w_in', 'ffn1_w_out', 'norm_mix', 'mix_w_in', 'hgrn_lb', 'hgrn_g', 'hgrn_w_o', 'conv_w', 'conv_b', 'conv_ln_g', 'conv_ln_b', 'conv_w_o', 'mix_w_out', 'norm_ffn2', 'ffn2_w_in', 'ffn2_w_out', 'norm_final', 'loss_target', 'm_ada_w', 'm_ada_b', 'm_norm_ffn1', 'm_ffn1_w_in', 'm_ffn1_w_out', 'm_norm_mix', 'm_mix_w_in', 'm_hgrn_lb', 'm_hgrn_g', 'm_hgrn_w_o', 'm_conv_w', 'm_conv_b', 'm_conv_ln_g', 'm_conv_ln_b', 'm_conv_w_o', 'm_mix_w_out', 'm_norm_ffn2', 'm_ffn2_w_in', 'm_ffn2_w_out', 'm_norm_final', 'v_ada_w', 'v_ada_b', 'v_norm_ffn1', 'v_ffn1_w_in', 'v_ffn1_w_out', 'v_norm_mix', 'v_mix_w_in', 'v_hgrn_lb', 'v_hgrn_g', 'v_hgrn_w_o', 'v_conv_w', 'v_conv_b', 'v_conv_ln_g', 'v_conv_ln_b', 'v_conv_w_o', 'v_mix_w_out', 'v_norm_ffn2', 'v_ffn2_w_in', 'v_ffn2_w_out', 'v_norm_final']
TWIN_OUTPUTS = ['loss', 'grad_x', 'grad_ada_w', 'grad_ada_b', 'grad_norm_ffn1', 'grad_ffn1_w_in', 'grad_ffn1_w_out', 'grad_norm_mix', 'grad_mix_w_in', 'grad_hgrn_lb', 'grad_hgrn_g', 'grad_hgrn_w_o', 'grad_conv_w', 'grad_conv_b', 'grad_conv_ln_g', 'grad_conv_ln_b', 'grad_conv_w_o', 'grad_mix_w_out', 'grad_norm_ffn2', 'grad_ffn2_w_in', 'grad_ffn2_w_out', 'grad_norm_final', 'delta_ada_w', 'delta_ada_b', 'delta_norm_ffn1', 'delta_ffn1_w_in', 'delta_ffn1_w_out', 'delta_norm_mix', 'delta_mix_w_in', 'delta_hgrn_lb', 'delta_hgrn_g', 'delta_hgrn_w_o', 'delta_conv_w', 'delta_conv_b', 'delta_conv_ln_g', 'delta_conv_ln_b', 'delta_conv_w_o', 'delta_mix_w_out', 'delta_norm_ffn2', 'delta_ffn2_w_in', 'delta_ffn2_w_out', 'delta_norm_final', 'new_m_ada_w', 'new_m_ada_b', 'new_m_norm_ffn1', 'new_m_ffn1_w_in', 'new_m_ffn1_w_out', 'new_m_norm_mix', 'new_m_mix_w_in', 'new_m_hgrn_lb', 'new_m_hgrn_g', 'new_m_hgrn_w_o', 'new_m_conv_w', 'new_m_conv_b', 'new_m_conv_ln_g', 'new_m_conv_ln_b', 'new_m_conv_w_o', 'new_m_mix_w_out', 'new_m_norm_ffn2', 'new_m_ffn2_w_in', 'new_m_ffn2_w_out', 'new_m_norm_final', 'new_v_ada_w', 'new_v_ada_b', 'new_v_norm_ffn1', 'new_v_ffn1_w_in', 'new_v_ffn1_w_out', 'new_v_norm_mix', 'new_v_mix_w_in', 'new_v_hgrn_lb', 'new_v_hgrn_g', 'new_v_hgrn_w_o', 'new_v_conv_w', 'new_v_conv_b', 'new_v_conv_ln_g', 'new_v_conv_ln_b', 'new_v_conv_w_o', 'new_v_mix_w_out', 'new_v_norm_ffn2', 'new_v_ffn2_w_in', 'new_v_ffn2_w_out', 'new_v_norm_final']
TWIN_LEAF_KINDS = {'loss': 'loss', 'grad_x': 'grad_x', 'grad_ada_w': 'grad_w', 'grad_ada_b': 'grad_w', 'grad_norm_ffn1': 'grad_w', 'grad_ffn1_w_in': 'grad_w', 'grad_ffn1_w_out': 'grad_w', 'grad_norm_mix': 'grad_w', 'grad_mix_w_in': 'grad_w', 'grad_hgrn_lb': 'grad_w', 'grad_hgrn_g': 'grad_w', 'grad_hgrn_w_o': 'grad_w', 'grad_conv_w': 'grad_w', 'grad_conv_b': 'grad_w', 'grad_conv_ln_g': 'grad_w', 'grad_conv_ln_b': 'grad_w', 'grad_conv_w_o': 'grad_w', 'grad_mix_w_out': 'grad_w', 'grad_norm_ffn2': 'grad_w', 'grad_ffn2_w_in': 'grad_w', 'grad_ffn2_w_out': 'grad_w', 'grad_norm_final': 'grad_w', 'delta_ada_w': 'delta_w', 'delta_ada_b': 'delta_w', 'delta_norm_ffn1': 'delta_w', 'delta_ffn1_w_in': 'delta_w', 'delta_ffn1_w_out': 'delta_w', 'delta_norm_mix': 'delta_w', 'delta_mix_w_in': 'delta_w', 'delta_hgrn_lb': 'delta_w', 'delta_hgrn_g': 'delta_w', 'delta_hgrn_w_o': 'delta_w', 'delta_conv_w': 'delta_w', 'delta_conv_b': 'delta_w', 'delta_conv_ln_g': 'delta_w', 'delta_conv_ln_b': 'delta_w', 'delta_conv_w_o': 'delta_w', 'delta_mix_w_out': 'delta_w', 'delta_norm_ffn2': 'delta_w', 'delta_ffn2_w_in': 'delta_w', 'delta_ffn2_w_out': 'delta_w', 'delta_norm_final': 'delta_w', 'new_m_ada_w': 'new_m', 'new_m_ada_b': 'new_m', 'new_m_norm_ffn1': 'new_m', 'new_m_ffn1_w_in': 'new_m', 'new_m_ffn1_w_out': 'new_m', 'new_m_norm_mix': 'new_m', 'new_m_mix_w_in': 'new_m', 'new_m_hgrn_lb': 'new_m', 'new_m_hgrn_g': 'new_m', 'new_m_hgrn_w_o': 'new_m', 'new_m_conv_w': 'new_m', 'new_m_conv_b': 'new_m', 'new_m_conv_ln_g': 'new_m', 'new_m_conv_ln_b': 'new_m', 'new_m_conv_w_o': 'new_m', 'new_m_mix_w_out': 'new_m', 'new_m_norm_ffn2': 'new_m', 'new_m_ffn2_w_in': 'new_m', 'new_m_ffn2_w_out': 'new_m', 'new_m_norm_final': 'new_m', 'new_v_ada_w': 'new_v', 'new_v_ada_b': 'new_v', 'new_v_norm_ffn1': 'new_v', 'new_v_ffn1_w_in': 'new_v', 'new_v_ffn1_w_out': 'new_v', 'new_v_norm_mix': 'new_v', 'new_v_mix_w_in': 'new_v', 'new_v_hgrn_lb': 'new_v', 'new_v_hgrn_g': 'new_v', 'new_v_hgrn_w_o': 'new_v', 'new_v_conv_w': 'new_v', 'new_v_conv_b': 'new_v', 'new_v_conv_ln_g': 'new_v', 'new_v_conv_ln_b': 'new_v', 'new_v_conv_w_o': 'new_v', 'new_v_mix_w_out': 'new_v', 'new_v_norm_ffn2': 'new_v', 'new_v_ffn2_w_in': 'new_v', 'new_v_ffn2_w_out': 'new_v', 'new_v_norm_final': 'new_v'}


def _forward(args):
    return _fwd_reference(*[args[k] for k in FWD_PARAMS])


def _output_shape():
    out = _jax.eval_shape(lambda: _forward(_fwd_setup_inputs(0)))
    return out.shape, out.dtype

N_MICROBATCH = 1
ADAM_LR = 0.001
ADAM_B1 = 0.9
ADAM_B2 = 0.999
ADAM_EPS = 1e-08
ADAM_WD = 0.01
ADAM_STEP = 10
PER_EXAMPLE_BATCH_AXIS = {'x': 0, 'c': 0, 'loss_target': 0}
SHARED_INPUTS = []
_WEIGHT_DTYPES = {'ada_w': _jnp.float32, 'ada_b': _jnp.float32, 'norm_ffn1': _jnp.float32, 'ffn1_w_in': _jnp.float32, 'ffn1_w_out': _jnp.float32, 'norm_mix': _jnp.float32, 'mix_w_in': _jnp.float32, 'hgrn_lb': _jnp.float32, 'hgrn_g': _jnp.float32, 'hgrn_w_o': _jnp.float32, 'conv_w': _jnp.float32, 'conv_b': _jnp.float32, 'conv_ln_g': _jnp.float32, 'conv_ln_b': _jnp.float32, 'conv_w_o': _jnp.float32, 'mix_w_out': _jnp.float32, 'norm_ffn2': _jnp.float32, 'ffn2_w_in': _jnp.float32, 'ffn2_w_out': _jnp.float32, 'norm_final': _jnp.float32}
MOMENT_SCALE = {'ada_w': 4.812922e-02, 'ada_b': 8.218126e-02, 'norm_ffn1': 7.903904e-02, 'ffn1_w_in': 3.569230e-02, 'ffn1_w_out': 5.817271e-02, 'norm_mix': 6.572108e-02, 'mix_w_in': 2.976828e-02, 'hgrn_lb': 3.100873e-03, 'hgrn_g': 4.673280e-02, 'hgrn_w_o': 4.886748e-02, 'conv_w': 3.532283e-02, 'conv_b': 5.189814e-02, 'conv_ln_g': 4.587929e-02, 'conv_ln_b': 3.932067e-02, 'conv_w_o': 3.584705e-02, 'mix_w_out': 5.977485e-02, 'norm_ffn2': 6.882072e-02, 'ffn2_w_in': 3.167631e-02, 'ffn2_w_out': 5.209868e-02, 'norm_final': 3.221430e+01}


def _to_microbatches(a, axis):
    t = _jnp.moveaxis(a, axis, 0)
    t = t.reshape((N_MICROBATCH, t.shape[0] // N_MICROBATCH) + t.shape[1:])
    return _jnp.moveaxis(t, 1, axis + 1)


def setup_inputs(seed: int = 0) -> dict:
    inp = _fwd_setup_inputs(seed)
    key = _jax.random.fold_in(_jax.random.key(seed), 7919)
    shape, _ = _output_shape()
    out = dict(inp)
    out["loss_target"] = _jax.random.normal(_jax.random.fold_in(key, 0), shape, _jnp.float32)
    for i, name in enumerate(TWIN_WEIGHTS):
        w = inp[name].astype(_jnp.float32)
        if MOMENT_SCALE is None:
            s = _jnp.sqrt(_jnp.mean(_jnp.square(w)) + 1e-30)
        else:
            s = MOMENT_SCALE[name]
        km, kv = _jax.random.split(_jax.random.fold_in(key, i + 1))
        out[name] = w
        out["m_" + name] = s * _jax.random.normal(km, w.shape, _jnp.float32)
        out["v_" + name] = (s * s) * _jax.random.uniform(kv, w.shape, _jnp.float32, 0.5, 1.5)
    if N_MICROBATCH > 1:
        for name, axis in PER_EXAMPLE_BATCH_AXIS.items():
            out[name] = _to_microbatches(out[name], axis)
    return {'x': out['x'], 'c': out['c'], 'ada_w': out['ada_w'], 'ada_b': out['ada_b'], 'norm_ffn1': out['norm_ffn1'], 'ffn1_w_in': out['ffn1_w_in'], 'ffn1_w_out': out['ffn1_w_out'], 'norm_mix': out['norm_mix'], 'mix_w_in': out['mix_w_in'], 'hgrn_lb': out['hgrn_lb'], 'hgrn_g': out['hgrn_g'], 'hgrn_w_o': out['hgrn_w_o'], 'conv_w': out['conv_w'], 'conv_b': out['conv_b'], 'conv_ln_g': out['conv_ln_g'], 'conv_ln_b': out['conv_ln_b'], 'conv_w_o': out['conv_w_o'], 'mix_w_out': out['mix_w_out'], 'norm_ffn2': out['norm_ffn2'], 'ffn2_w_in': out['ffn2_w_in'], 'ffn2_w_out': out['ffn2_w_out'], 'norm_final': out['norm_final'], 'loss_target': out['loss_target'], 'm_ada_w': out['m_ada_w'], 'm_ada_b': out['m_ada_b'], 'm_norm_ffn1': out['m_norm_ffn1'], 'm_ffn1_w_in': out['m_ffn1_w_in'], 'm_ffn1_w_out': out['m_ffn1_w_out'], 'm_norm_mix': out['m_norm_mix'], 'm_mix_w_in': out['m_mix_w_in'], 'm_hgrn_lb': out['m_hgrn_lb'], 'm_hgrn_g': out['m_hgrn_g'], 'm_hgrn_w_o': out['m_hgrn_w_o'], 'm_conv_w': out['m_conv_w'], 'm_conv_b': out['m_conv_b'], 'm_conv_ln_g': out['m_conv_ln_g'], 'm_conv_ln_b': out['m_conv_ln_b'], 'm_conv_w_o': out['m_conv_w_o'], 'm_mix_w_out': out['m_mix_w_out'], 'm_norm_ffn2': out['m_norm_ffn2'], 'm_ffn2_w_in': out['m_ffn2_w_in'], 'm_ffn2_w_out': out['m_ffn2_w_out'], 'm_norm_final': out['m_norm_final'], 'v_ada_w': out['v_ada_w'], 'v_ada_b': out['v_ada_b'], 'v_norm_ffn1': out['v_norm_ffn1'], 'v_ffn1_w_in': out['v_ffn1_w_in'], 'v_ffn1_w_out': out['v_ffn1_w_out'], 'v_norm_mix': out['v_norm_mix'], 'v_mix_w_in': out['v_mix_w_in'], 'v_hgrn_lb': out['v_hgrn_lb'], 'v_hgrn_g': out['v_hgrn_g'], 'v_hgrn_w_o': out['v_hgrn_w_o'], 'v_conv_w': out['v_conv_w'], 'v_conv_b': out['v_conv_b'], 'v_conv_ln_g': out['v_conv_ln_g'], 'v_conv_ln_b': out['v_conv_ln_b'], 'v_conv_w_o': out['v_conv_w_o'], 'v_mix_w_out': out['v_mix_w_out'], 'v_norm_ffn2': out['v_norm_ffn2'], 'v_ffn2_w_in': out['v_ffn2_w_in'], 'v_ffn2_w_out': out['v_ffn2_w_out'], 'v_norm_final': out['v_norm_final']}


def _loss(weights, diff, rest, loss_target):
    with _jax.named_scope("forward"):
        args = {**rest, TWIN_DIFF_INPUT: diff, **{k: w.astype(_WEIGHT_DTYPES[k]) for k, w in weights.items()}}
        y = _forward(args)
    with _jax.named_scope("loss_head"):
        err = _jnp.square(y.astype(_jnp.float32) - loss_target)
        return 0.5 * _jnp.sum(_jnp.mean(err, axis=-1)) if err.ndim else 0.5 * err


def _adamw(w, g, m, v):
    m = ADAM_B1 * m + (1.0 - ADAM_B1) * g
    v = ADAM_B2 * v + (1.0 - ADAM_B2) * _jnp.square(g)
    m_hat = m / (1.0 - ADAM_B1 ** ADAM_STEP)
    v_hat = v / (1.0 - ADAM_B2 ** ADAM_STEP)
    delta = -ADAM_LR * (m_hat / (_jnp.sqrt(v_hat) + ADAM_EPS) + ADAM_WD * w)
    return delta, m, v


def reference(x, c, ada_w, ada_b, norm_ffn1, ffn1_w_in, ffn1_w_out, norm_mix, mix_w_in, hgrn_lb, hgrn_g, hgrn_w_o, conv_w, conv_b, conv_ln_g, conv_ln_b, conv_w_o, mix_w_out, norm_ffn2, ffn2_w_in, ffn2_w_out, norm_final, loss_target, m_ada_w, m_ada_b, m_norm_ffn1, m_ffn1_w_in, m_ffn1_w_out, m_norm_mix, m_mix_w_in, m_hgrn_lb, m_hgrn_g, m_hgrn_w_o, m_conv_w, m_conv_b, m_conv_ln_g, m_conv_ln_b, m_conv_w_o, m_mix_w_out, m_norm_ffn2, m_ffn2_w_in, m_ffn2_w_out, m_norm_final, v_ada_w, v_ada_b, v_norm_ffn1, v_ffn1_w_in, v_ffn1_w_out, v_norm_mix, v_mix_w_in, v_hgrn_lb, v_hgrn_g, v_hgrn_w_o, v_conv_w, v_conv_b, v_conv_ln_g, v_conv_ln_b, v_conv_w_o, v_mix_w_out, v_norm_ffn2, v_ffn2_w_in, v_ffn2_w_out, v_norm_final):
    given = dict(x=x, c=c, ada_w=ada_w, ada_b=ada_b, norm_ffn1=norm_ffn1, ffn1_w_in=ffn1_w_in, ffn1_w_out=ffn1_w_out, norm_mix=norm_mix, mix_w_in=mix_w_in, hgrn_lb=hgrn_lb, hgrn_g=hgrn_g, hgrn_w_o=hgrn_w_o, conv_w=conv_w, conv_b=conv_b, conv_ln_g=conv_ln_g, conv_ln_b=conv_ln_b, conv_w_o=conv_w_o, mix_w_out=mix_w_out, norm_ffn2=norm_ffn2, ffn2_w_in=ffn2_w_in, ffn2_w_out=ffn2_w_out, norm_final=norm_final, loss_target=loss_target, m_ada_w=m_ada_w, m_ada_b=m_ada_b, m_norm_ffn1=m_norm_ffn1, m_ffn1_w_in=m_ffn1_w_in, m_ffn1_w_out=m_ffn1_w_out, m_norm_mix=m_norm_mix, m_mix_w_in=m_mix_w_in, m_hgrn_lb=m_hgrn_lb, m_hgrn_g=m_hgrn_g, m_hgrn_w_o=m_hgrn_w_o, m_conv_w=m_conv_w, m_conv_b=m_conv_b, m_conv_ln_g=m_conv_ln_g, m_conv_ln_b=m_conv_ln_b, m_conv_w_o=m_conv_w_o, m_mix_w_out=m_mix_w_out, m_norm_ffn2=m_norm_ffn2, m_ffn2_w_in=m_ffn2_w_in, m_ffn2_w_out=m_ffn2_w_out, m_norm_final=m_norm_final, v_ada_w=v_ada_w, v_ada_b=v_ada_b, v_norm_ffn1=v_norm_ffn1, v_ffn1_w_in=v_ffn1_w_in, v_ffn1_w_out=v_ffn1_w_out, v_norm_mix=v_norm_mix, v_mix_w_in=v_mix_w_in, v_hgrn_lb=v_hgrn_lb, v_hgrn_g=v_hgrn_g, v_hgrn_w_o=v_hgrn_w_o, v_conv_w=v_conv_w, v_conv_b=v_conv_b, v_conv_ln_g=v_conv_ln_g, v_conv_ln_b=v_conv_ln_b, v_conv_w_o=v_conv_w_o, v_mix_w_out=v_mix_w_out, v_norm_ffn2=v_norm_ffn2, v_ffn2_w_in=v_ffn2_w_in, v_ffn2_w_out=v_ffn2_w_out, v_norm_final=v_norm_final)
    weights = {n: given[n] for n in TWIN_WEIGHTS}
    shared = {n: given[n] for n in SHARED_INPUTS}
    per_example = {n: given[n] for n in ['x', 'c']}
    grad_fn = _jax.value_and_grad(_loss, argnums=(0, 1))

    def one_microbatch(ex, loss_target):
        ex = dict(ex)
        diff = ex.pop(TWIN_DIFF_INPUT)
        return grad_fn(weights, diff, {**shared, **ex}, loss_target)

    if N_MICROBATCH == 1:
        loss, (grad_w, grad_x) = one_microbatch(per_example, given["loss_target"])
    else:
        def body(carry, xs):
            loss_sum, grad_sum = carry
            l_k, (gw_k, gx_k) = one_microbatch(xs[0], xs[1])
            with _jax.named_scope("update"):
                return (loss_sum + l_k, _jax.tree.map(_jnp.add, grad_sum, gw_k)), gx_k

        init = (_jnp.zeros((), _jnp.float32), _jax.tree.map(_jnp.zeros_like, weights))
        (loss, grad_w), grad_x = _jax.lax.scan(body, init, (per_example, given["loss_target"]))
    with _jax.named_scope("update"):
        delta_w, new_m, new_v = {}, {}, {}
        for n in TWIN_WEIGHTS:
            delta_w[n], new_m[n], new_v[n] = _adamw(weights[n], grad_w[n], given["m_" + n], given["v_" + n])
    return (loss, grad_x, *[grad_w[n] for n in TWIN_WEIGHTS], *[delta_w[n] for n in TWIN_WEIGHTS],
            *[new_m[n] for n in TWIN_WEIGHTS], *[new_v[n] for n in TWIN_WEIGHTS])
```

```python
import functools

import jax
import jax.numpy as jnp
from jax import lax
from jax.experimental import pallas as pl
from jax.experimental.pallas import tpu as pltpu

F32 = jnp.float32
BF16 = jnp.bfloat16

D = 1024
DFF = 2816
NCHIP = 4
FSH = 2 * DFF // NCHIP
HEADS = 8
DK = 128
CHUNK = 64
CONV_K = 31
HALO = 32
EPS = 1e-6
TB = 256
CB = 512
VMEM_LIMIT = 56 * 1024 * 1024

ADAM_LR = 0.001
ADAM_B1 = 0.9
ADAM_B2 = 0.999
ADAM_EPS = 1e-08
ADAM_WD = 0.01
ADAM_STEP = 10

MESH = pl.DeviceIdType.MESH
ANY = pl.BlockSpec(memory_space=pl.ANY)


def _params(*sem):
    return pltpu.CompilerParams(dimension_semantics=sem, vmem_limit_bytes=VMEM_LIMIT)


def _sigmoid(x):
    return 1.0 / (1.0 + jnp.exp(-x))


def _dsilu(x, sg):
    return sg * (1.0 + x * (1.0 - sg))


def _nt(a, b):
    return lax.dot_general(a, b, (((1,), (1,)), ((), ())), preferred_element_type=F32)


def _tn(a, b):
    return lax.dot_general(a, b, (((0,), (0,)), ((), ())), preferred_element_type=F32)


def _nn(a, b):
    return jnp.dot(a, b, preferred_element_type=F32)


def _colsum(x):
    return jnp.sum(x, axis=0, keepdims=True)


def _rms_fwd(x, gn, sc, sh):
    r = lax.rsqrt(jnp.mean(x * x, axis=-1, keepdims=True) + EPS)
    n = x * r
    h = (n * gn) * (1.0 + sc) + sh
    return r, n, h


def _rms_bwd(dh, r, n, gn, sc, acc_ref):
    acc_ref[0:1, :] += _colsum(dh)
    acc_ref[1:2, :] += _colsum(dh * (n * gn))
    dng = dh * (1.0 + sc)
    acc_ref[3:4, :] += _colsum(dng * n)
    dn = dng * gn
    return r * (dn - n * jnp.mean(dn * n, axis=-1, keepdims=True))


def _ffn_fwd(x, vec, w_in, w_out, name):
    T = x.shape[0]

    def body(x_ref, vec_ref, win_hbm, wout_hbm, xo_ref, h_ref, a_ref, b_ref, s_ref, f_ref, win, wout):
        @pl.when(pl.program_id(0) == 0)
        def _():
            pltpu.sync_copy(win_hbm, win)
            pltpu.sync_copy(wout_hbm, wout)

        x = x_ref[...]
        sh, sc, gate, gn = vec_ref[0:1, :], vec_ref[1:2, :], vec_ref[2:3, :], vec_ref[3:4, :]
        _, _, h = _rms_fwd(x, gn, sc, sh)
        hb = h.astype(BF16)
        h_ref[...] = hb
        f = jnp.zeros((TB, D), F32)
        for j in range(2):
            cols = slice(j * FSH, (j + 1) * FSH)
            a = _nn(hb, win[j])
            b = _nn(hb, win[2 + j])
            s = (a * _sigmoid(a) * b).astype(BF16)
            a_ref[:, cols] = a.astype(BF16)
            b_ref[:, cols] = b.astype(BF16)
            s_ref[:, cols] = s
            f = f + _nn(s, wout[cols, :])
        xo_ref[...] = x + (0.5 * gate) * f
        f_ref[...] = f.astype(BF16)

    row = lambda w: pl.BlockSpec((TB, w), lambda i: (i, 0))
    return pl.pallas_call(
        body, name=name, grid=(T // TB,),
        in_specs=[row(D), pl.BlockSpec((8, D), lambda i: (0, 0)), ANY, ANY],
        out_specs=[row(D), row(D), row(DFF), row(DFF), row(DFF), row(D)],
        out_shape=[jax.ShapeDtypeStruct((T, D), F32), jax.ShapeDtypeStruct((T, D), BF16),
                   jax.ShapeDtypeStruct((T, DFF), BF16), jax.ShapeDtypeStruct((T, DFF), BF16),
                   jax.ShapeDtypeStruct((T, DFF), BF16), jax.ShapeDtypeStruct((T, D), BF16)],
        scratch_shapes=[pltpu.VMEM((NCHIP, D, FSH), BF16), pltpu.VMEM((DFF, D), BF16)],
        compiler_params=_params("arbitrary"),
    )(x, vec, w_in, w_out)


def _ffn_bwd(dxo, x, vec, a, b, f, w_in, w_out, name):
    T = x.shape[0]

    def body(dxo_ref, x_ref, vec_ref, a_ref, b_ref, f_ref, win_hbm, wout_hbm,
             dx_ref, df_ref, dab_ref, acc_ref, win, wout):
        @pl.when(pl.program_id(0) == 0)
        def _():
            pltpu.sync_copy(win_hbm, win)
            pltpu.sync_copy(wout_hbm, wout)
            acc_ref[...] = jnp.zeros_like(acc_ref)

        dxo = dxo_ref[...]
        x = x_ref[...]
        sh, sc, gate, gn = vec_ref[0:1, :], vec_ref[1:2, :], vec_ref[2:3, :], vec_ref[3:4, :]
        r, n, _ = _rms_fwd(x, gn, sc, sh)
        acc_ref[2:3, :] += _colsum(0.5 * f_ref[...].astype(F32) * dxo)
        dfb = ((0.5 * gate) * dxo).astype(BF16)
        df_ref[...] = dfb
        dh = jnp.zeros((TB, D), F32)
        for j in range(2):
            cols = slice(j * FSH, (j + 1) * FSH)
            ds = _nt(dfb, wout[cols, :])
            av = a_ref[:, cols].astype(F32)
            bv = b_ref[:, cols].astype(F32)
            sg = _sigmoid(av)
            da = (ds * bv * _dsilu(av, sg)).astype(BF16)
            db = (ds * (av * sg)).astype(BF16)
            dab_ref[j] = da
            dab_ref[2 + j] = db
            dh = dh + _nt(da, win[j]) + _nt(db, win[2 + j])
        dx_ref[...] = dxo + _rms_bwd(dh, r, n, gn, sc, acc_ref)

    row = lambda w: pl.BlockSpec((TB, w), lambda i: (i, 0))
    vec8 = pl.BlockSpec((8, D), lambda i: (0, 0))
    return pl.pallas_call(
        body, name=name, grid=(T // TB,),
        in_specs=[row(D), row(D), vec8, row(DFF), row(DFF), row(D), ANY, ANY],
        out_specs=[row(D), pl.BlockSpec((None, TB, D), lambda i: (0, i, 0)),
                   pl.BlockSpec((NCHIP, TB, FSH), lambda i: (0, i, 0)), vec8],
        out_shape=[jax.ShapeDtypeStruct((T, D), F32), jax.ShapeDtypeStruct((1, T, D), BF16),
                   jax.ShapeDtypeStruct((NCHIP, T, FSH), BF16), jax.ShapeDtypeStruct((8, D), F32)],
        scratch_shapes=[pltpu.VMEM((NCHIP, D, FSH), BF16), pltpu.VMEM((DFF, D), BF16)],
        compiler_params=_params("arbitrary"),
    )(dxo, x, vec, a, b, f, w_in, w_out)


def _mm_tn(a, b3, hp, name):
    T, M = a.shape
    P, _, N = b3.shape
    tm = M if M <= 1408 else M // 2
    tk = 512

    def body(a_ref, b_ref, o_ref):
        @pl.when(pl.program_id(2) == 0)
        def _():
            o_ref[...] = jnp.zeros_like(o_ref)

        o_ref[...] += _tn(a_ref[...], b_ref[...])

    return pl.pallas_call(
        body, name=name, grid=(P, M // tm, T // tk),
        in_specs=[pl.BlockSpec((tk, tm), lambda p, i, k: (k, i)),
                  pl.BlockSpec((None, tk, N), lambda p, i, k: (p, k, 0))],
        out_specs=pl.BlockSpec((None, tm, N), lambda p, i, k: (p // hp, i, p % hp)),
        out_shape=jax.ShapeDtypeStruct((P // hp, M, hp * N), F32),
        compiler_params=_params("parallel", "parallel", "arbitrary"),
    )(a, b3)


def _mix_proj_fwd(x, vec, w_in):
    T = x.shape[0]

    def body(x_ref, vec_ref, w_hbm, h_ref, qr_ref, g_ref, k_ref, v_ref, og_ref, u_ref, ua_ref, ub_ref,
             sa_ref, sb_ref, w):
        @pl.when(pl.program_id(0) == 0)
        def _():
            pltpu.sync_copy(w_hbm, w)

        x = x_ref[...]
        sh, sc, gn, lb = vec_ref[0:1, :], vec_ref[1:2, :], vec_ref[3:4, :], vec_ref[4:5, :]
        _, _, h = _rms_fwd(x, gn, sc, sh)
        hb = h.astype(BF16)
        h_ref[...] = hb
        p = _nn(hb, w[0])
        qr_ref[...] = p[:, :D].astype(BF16)
        fg = lb + (1.0 - lb) * _sigmoid(p[:, D:])
        g_ref[...] = jnp.log(fg)
        k_ref[...] = (1.0 - fg).astype(BF16)
        p = _nn(hb, w[1])
        v_ref[...] = p[:, :D].astype(BF16)
        og_ref[...] = p[:, D:].astype(BF16)
        p = _nn(hb, w[2])
        ua, ub = p[:, :D], p[:, D:]
        u_ref[...] = ua * _sigmoid(ub)
        ua_ref[...] = ua.astype(BF16)
        ub_ref[...] = ub.astype(BF16)
        p = _nn(hb, w[3])
        sa_ref[...] = _sigmoid(p[:, :D]).astype(BF16)
        sb_ref[...] = _sigmoid(p[:, D:]).astype(BF16)

    row = pl.BlockSpec((TB, D), lambda i: (i, 0))
    bf = jax.ShapeDtypeStruct((T, D), BF16)
    f32 = jax.ShapeDtypeStruct((T, D), F32)
    return pl.pallas_call(
        body, name="mix_proj_fwd", grid=(T // TB,),
        in_specs=[row, pl.BlockSpec((8, D), lambda i: (0, 0)), ANY],
        out_specs=[row] * 11,
        out_shape=[bf, bf, f32, bf, bf, bf, f32, bf, bf, bf, bf],
        scratch_shapes=[pltpu.VMEM((NCHIP, D, 2 * D), BF16)],
        compiler_params=_params("arbitrary"),
    )(x, vec, w_in)


def _mix_proj_bwd(dxo, x, vec, dpa, dpb, dpc, w_in):
    T = x.shape[0]

    def body(dxo_ref, x_ref, vec_ref, dpa_ref, dpb_ref, dpc_ref, w_hbm, dx_ref, acc_ref, w):
        @pl.when(pl.program_id(0) == 0)
        def _():
            pltpu.sync_copy(w_hbm, w)
            acc_ref[...] = jnp.zeros_like(acc_ref)

        x = x_ref[...]
        sh, sc, gn = vec_ref[0:1, :], vec_ref[1:2, :], vec_ref[3:4, :]
        r, n, _ = _rms_fwd(x, gn, sc, sh)
        dh = jnp.zeros((TB, D), F32)
        for p in range(8):
            src = dpa_ref[p] if p < 4 else (dpb_ref[p - 4] if p < 6 else dpc_ref[p - 6])
            dh = dh + _nt(src, w[p // 2, :, (p % 2) * D:(p % 2 + 1) * D])
        dx_ref[...] = dxo_ref[...] + _rms_bwd(dh, r, n, gn, sc, acc_ref)

    row = pl.BlockSpec((TB, D), lambda i: (i, 0))
    vec8 = pl.BlockSpec((8, D), lambda i: (0, 0))
    stack = lambda k: pl.BlockSpec((k, TB, D), lambda i: (0, i, 0))
    return pl.pallas_call(
        body, name="mix_proj_bwd", grid=(T // TB,),
        in_specs=[row, row, vec8, stack(4), stack(2), stack(2), ANY],
        out_specs=[row, vec8],
        out_shape=[jax.ShapeDtypeStruct((T, D), F32), jax.ShapeDtypeStruct((8, D), F32)],
        scratch_shapes=[pltpu.VMEM((NCHIP, D, 2 * D), BF16)],
        compiler_params=_params("arbitrary"),
    )(dxo, x, vec, dpa, dpb, dpc, w_in)


def _tri(lower):
    r = lax.broadcasted_iota(jnp.int32, (CHUNK, CHUNK), 0)
    c = lax.broadcasted_iota(jnp.int32, (CHUNK, CHUNK), 1)
    return (c <= r) if lower else (c >= r)


def _cumsum_rows(mask, g):
    return jnp.dot(mask.astype(F32), g, precision=lax.Precision.HIGHEST, preferred_element_type=F32)


def _hgrn_fwd(qr, g, k, v, og, vec):
    T = qr.shape[0]
    nck = CB // CHUNK

    def body(qr_ref, g_ref, k_ref, v_ref, og_ref, vec_ref, out_ref, o_ref, st_ref, state, bsc):
        @pl.when(pl.program_id(1) == 0)
        def _():
            state[...] = jnp.zeros_like(state)

        low = _tri(True)

        def chunk(c, carry):
            rows = pl.ds(pl.multiple_of(c * CHUNK, CHUNK), CHUNK)
            qv = qr_ref[rows, :].astype(F32)
            q = qv * _sigmoid(qv) * (DK ** -0.5)
            kk = k_ref[rows, :].astype(F32)
            vb = v_ref[rows, :]
            bsc[...] = _cumsum_rows(low, g_ref[rows, :])
            b = bsc[...]
            mid = bsc[CHUNK // 2 - 1:CHUNK // 2, :]
            last = bsc[CHUNK - 1:CHUNK, :]
            st = state[...]
            st_ref[c] = st.astype(BF16)
            qt = (q * jnp.exp(b - mid)).astype(BF16)
            kt = (kk * jnp.exp(mid - b)).astype(BF16)
            att = jnp.where(low, _nt(qt, kt), 0.0).astype(BF16)
            qe = (q * jnp.exp(b)).astype(BF16)
            o_ref[rows, :] = _nn(att, vb) + _nt(qe, st.astype(BF16))
            kd = (kk * jnp.exp(last - b)).astype(BF16)
            state[...] = st * jnp.exp(last) + _tn(vb, kd)
            return carry

        lax.fori_loop(0, nck, chunk, 0)
        o = o_ref[...]
        ogv = og_ref[...].astype(F32)
        rms = lax.rsqrt(jnp.mean(o * o, axis=-1, keepdims=True) + EPS)
        out_ref[...] = (o * rms * vec_ref[5:6, :] * (ogv * _sigmoid(ogv))).astype(BF16)

    blk = pl.BlockSpec((CB, DK), lambda h, i: (i, h))
    return pl.pallas_call(
        body, name="hgrn_fwd", grid=(HEADS, T // CB),
        in_specs=[blk, blk, blk, blk, blk, pl.BlockSpec((8, DK), lambda h, i: (0, h))],
        out_specs=[blk, blk, pl.BlockSpec((None, nck, DK, DK), lambda h, i: (h, i, 0, 0))],
        out_shape=[jax.ShapeDtypeStruct((T, D), BF16), jax.ShapeDtypeStruct((T, D), F32),
                   jax.ShapeDtypeStruct((HEADS, T // CHUNK, DK, DK), BF16)],
        scratch_shapes=[pltpu.VMEM((DK, DK), F32), pltpu.VMEM((CHUNK, DK), F32)],
        compiler_params=_params("parallel", "arbitrary"),
    )(qr, g, k, v, og, vec)


def _hgrn_bwd(dout, og, qr, g, k, v, o, st, vec):
    T = qr.shape[0]
    nck = CB // CHUNK
    nb = T // CB

    def body(dout_ref, og_ref, qr_ref, g_ref, k_ref, v_ref, o_ref, st_ref, vec_ref,
             dp_ref, acc_ref, dstate, do_scr, bsc, dbsc):
        @pl.when(pl.program_id(1) == 0)
        def _():
            dstate[...] = jnp.zeros_like(dstate)
            acc_ref[...] = jnp.zeros_like(acc_ref)

        o = o_ref[...]
        ogv = og_ref[...].astype(F32)
        dout = dout_ref[...].astype(F32)
        hg = vec_ref[5:6, :]
        sgo = _sigmoid(ogv)
        rms = lax.rsqrt(jnp.mean(o * o, axis=-1, keepdims=True) + EPS)
        ohat = o * rms
        dp_ref[3] = (dout * (ohat * hg) * _dsilu(ogv, sgo)).astype(BF16)
        don = dout * (ogv * sgo)
        acc_ref[0:1, :] += _colsum(don * ohat)
        dohat = don * hg
        do_scr[...] = rms * (dohat - ohat * jnp.mean(dohat * ohat, axis=-1, keepdims=True))

        low = _tri(True)
        upp = _tri(False)
        lb = vec_ref[4:5, :]

        def chunk(j, carry):
            c = nck - 1 - j
            rows = pl.ds(pl.multiple_of(c * CHUNK, CHUNK), CHUNK)
            qv = qr_ref[rows, :].astype(F32)
            sgq = _sigmoid(qv)
            q = qv * sgq * (DK ** -0.5)
            kk = k_ref[rows, :].astype(F32)
            vb = v_ref[rows, :]
            gv = g_ref[rows, :]
            do = do_scr[rows, :]
            dob = do.astype(BF16)
            bsc[...] = _cumsum_rows(low, gv)
            b = bsc[...]
            mid = bsc[CHUNK // 2 - 1:CHUNK // 2, :]
            last = bsc[CHUNK - 1:CHUNK, :]
            s0 = st_ref[c]
            ds1 = dstate[...]
            ds1b = ds1.astype(BF16)
            eq = jnp.exp(b - mid)
            ek = jnp.exp(mid - b)
            eb = jnp.exp(b)
            ed = jnp.exp(last - b)
            el = jnp.exp(last)
            qt = q * eq
            kt = kk * ek
            qe = q * eb
            kd = kk * ed
            qtb, ktb, qeb, kdb = qt.astype(BF16), kt.astype(BF16), qe.astype(BF16), kd.astype(BF16)
            att = jnp.where(low, _nt(qtb, ktb), 0.0).astype(BF16)
            datt = jnp.where(low, _nt(dob, vb), 0.0).astype(BF16)
            dv = _tn(att, dob) + _nt(kdb, ds1b)
            dqt = _nn(datt, ktb)
            dkt = _tn(datt, qtb)
            dqe = _nn(dob, s0)
            dkd = _nn(vb, ds1b)
            dq = dqt * eq + dqe * eb
            dk = dkt * ek + dkd * ed
            qt, kt, qe, kd = qtb.astype(F32), ktb.astype(F32), qeb.astype(F32), kdb.astype(F32)
            dkdkd = dkd * kd
            dlast = _colsum(dkdkd) + el * _colsum(ds1 * s0.astype(F32))
            dbsc[...] = dqt * qt - dkt * kt + dqe * qe - dkdkd
            dbsc[CHUNK - 1:CHUNK, :] += dlast
            dg = _cumsum_rows(upp, dbsc[...])
            dstate[...] = ds1 * el + _tn(dob, qeb)
            fg = jnp.exp(gv)
            dfg = dg / fg - dk
            sig = (fg - lb) / (1.0 - lb)
            dlb = _colsum(dfg * (1.0 - sig)) * (lb * (1.0 - lb))
            acc_ref[1:2, :] += dlb
            acc_ref[2:3, :] -= dlb
            dp_ref[0, rows, :] = (dq * (DK ** -0.5) * _dsilu(qv, sgq)).astype(BF16)
            dp_ref[1, rows, :] = (dfg * (1.0 - lb) * sig * (1.0 - sig)).astype(BF16)
            dp_ref[2, rows, :] = dv.astype(BF16)
            return carry

        lax.fori_loop(0, nck, chunk, 0)

    blk = pl.BlockSpec((CB, DK), lambda h, i: (nb - 1 - i, h))
    return pl.pallas_call(
        body, name="hgrn_bwd", grid=(HEADS, nb),
        in_specs=[blk, blk, blk, blk, blk, blk, blk,
                  pl.BlockSpec((None, nck, DK, DK), lambda h, i: (h, nb - 1 - i, 0, 0)),
                  pl.BlockSpec((8, DK), lambda h, i: (0, h))],
        out_specs=[pl.BlockSpec((4, CB, DK), lambda h, i: (0, nb - 1 - i, h)),
                   pl.BlockSpec((8, DK), lambda h, i: (0, h))],
        out_shape=[jax.ShapeDtypeStruct((4, T, D), BF16), jax.ShapeDtypeStruct((8, D), F32)],
        scratch_shapes=[pltpu.VMEM((DK, DK), F32), pltpu.VMEM((CB, DK), F32),
                        pltpu.VMEM((CHUNK, DK), F32), pltpu.VMEM((CHUNK, DK), F32)],
        compiler_params=_params("parallel", "arbitrary"),
    )(dout, og, qr, g, k, v, o, st, vec)


def _ln_fwd(uc, lg, lbias):
    mu = jnp.mean(uc, axis=-1, keepdims=True)
    xc = uc - mu
    rstd = lax.rsqrt(jnp.mean(xc * xc, axis=-1, keepdims=True) + EPS)
    z = xc * rstd
    return rstd, z, z * lg + lbias


def _conv_fwd(u, cw, cvec):
    T = u.shape[0]
    per = TB // HALO

    def body(u_ref, halo_ref, cw_ref, cvec_ref, us_ref, uc_ref, pad):
        i = pl.program_id(0)
        pad[0:HALO, :] = jnp.where(i > 0, halo_ref[...], 0.0)
        pad[HALO:, :] = u_ref[...]
        acc = jnp.zeros((TB, D), F32) + cvec_ref[0:1, :]
        for j in range(CONV_K):
            acc = acc + cw_ref[j:j + 1, :] * pad[HALO - (CONV_K - 1) + j:HALO - (CONV_K - 1) + j + TB, :]
        uc_ref[...] = acc
        _, _, ul = _ln_fwd(acc, cvec_ref[1:2, :], cvec_ref[2:3, :])
        us_ref[...] = (ul * _sigmoid(ul)).astype(BF16)

    row = pl.BlockSpec((TB, D), lambda i: (i, 0))
    return pl.pallas_call(
        body, name="conv_fwd", grid=(T // TB,),
        in_specs=[row, pl.BlockSpec((HALO, D), lambda i: (jnp.maximum(i * per - 1, 0), 0)),
                  pl.BlockSpec((32, D), lambda i: (0, 0)), pl.BlockSpec((8, D), lambda i: (0, 0))],
        out_specs=[row, row],
        out_shape=[jax.ShapeDtypeStruct((T, D), BF16), jax.ShapeDtypeStruct((T, D), F32)],
        scratch_shapes=[pltpu.VMEM((TB + HALO, D), F32)],
        compiler_params=_params("parallel"),
    )(u, u, cw, cvec)


def _conv_bwd_ln(dus, uc, cvec):
    T = uc.shape[0]

    def body(dus_ref, uc_ref, cvec_ref, duc_ref, acc_ref):
        @pl.when(pl.program_id(0) == 0)
        def _():
            acc_ref[...] = jnp.zeros_like(acc_ref)

        lg = cvec_ref[1:2, :]
        rstd, z, ul = _ln_fwd(uc_ref[...], lg, cvec_ref[2:3, :])
        dul = dus_ref[...].astype(F32) * _dsilu(ul, _sigmoid(ul))
        acc_ref[1:2, :] += _colsum(dul * z)
        acc_ref[2:3, :] += _colsum(dul)
        dz = dul * lg
        duc = rstd * (dz - jnp.mean(dz, axis=-1, keepdims=True) - z * jnp.mean(dz * z, axis=-1, keepdims=True))
        acc_ref[0:1, :] += _colsum(duc)
        duc_ref[...] = duc

    row = pl.BlockSpec((TB, D), lambda i: (i, 0))
    vec8 = pl.BlockSpec((8, D), lambda i: (0, 0))
    return pl.pallas_call(
        body, name="conv_bwd_ln", grid=(T // TB,),
        in_specs=[row, row, vec8], out_specs=[row, vec8],
        out_shape=[jax.ShapeDtypeStruct((T, D), F32), jax.ShapeDtypeStruct((8, D), F32)],
        compiler_params=_params("arbitrary"),
    )(dus, uc, cvec)


def _conv_bwd_taps(duc, u, ua, ub, cw):
    T = u.shape[0]
    per = TB // HALO
    nblk = T // TB

    def body(duc_ref, dnext_ref, u_ref, uprev_ref, ua_ref, ub_ref, cw_ref, dp_ref, dcw_ref, upad, dpad):
        i = pl.program_id(0)

        @pl.when(i == 0)
        def _():
            dcw_ref[...] = jnp.zeros_like(dcw_ref)

        upad[0:HALO, :] = jnp.where(i > 0, uprev_ref[...], 0.0)
        upad[HALO:, :] = u_ref[...]
        dpad[0:TB, :] = duc_ref[...]
        dpad[TB:, :] = jnp.where(i < nblk - 1, dnext_ref[...], 0.0)
        duc = duc_ref[...]
        du = jnp.zeros((TB, D), F32)
        for j in range(CONV_K):
            lo = HALO - (CONV_K - 1) + j
            dcw_ref[j:j + 1, :] += _colsum(duc * upad[lo:lo + TB, :])
            du = du + cw_ref[j:j + 1, :] * dpad[CONV_K - 1 - j:CONV_K - 1 - j + TB, :]
        ua = ua_ref[...].astype(F32)
        sg = _sigmoid(ub_ref[...].astype(F32))
        dp_ref[0] = (du * sg).astype(BF16)
        dp_ref[1] = (du * ua * sg * (1.0 - sg)).astype(BF16)

    row = pl.BlockSpec((TB, D), lambda i: (i, 0))
    return pl.pallas_call(
        body, name="conv_bwd_taps", grid=(nblk,),
        in_specs=[row, pl.BlockSpec((HALO, D), lambda i: (jnp.minimum((i + 1) * per, T // HALO - 1), 0)),
                  row, pl.BlockSpec((HALO, D), lambda i: (jnp.maximum(i * per - 1, 0), 0)),
                  row, row, pl.BlockSpec((32, D), lambda i: (0, 0))],
        out_specs=[pl.BlockSpec((2, TB, D), lambda i: (0, i, 0)), pl.BlockSpec((32, D), lambda i: (0, 0))],
        out_shape=[jax.ShapeDtypeStruct((2, T, D), BF16), jax.ShapeDtypeStruct((32, D), F32)],
        scratch_shapes=[pltpu.VMEM((TB + HALO, D), F32), pltpu.VMEM((TB + HALO, D), F32)],
        compiler_params=_params("arbitrary"),
    )(duc, duc, u, u, ua, ub, cw)


def _merge_fwd(x, oa, us, sa, sb, vec, w_ho, w_co, w_mo):
    T = x.shape[0]

    def body(x_ref, oa_ref, us_ref, sa_ref, sb_ref, vec_ref, who_hbm, wco_hbm, wmo_hbm,
             xo_ref, ya_ref, yb_ref, mg_ref, mo_ref, who, wco, wmo):
        @pl.when(pl.program_id(0) == 0)
        def _():
            pltpu.sync_copy(who_hbm, who)
            pltpu.sync_copy(wco_hbm, wco)
            pltpu.sync_copy(wmo_hbm, wmo)

        ya = _nn(oa_ref[...], who[...])
        yb = _nn(us_ref[...], wco[...])
        mg = (sa_ref[...].astype(F32) * ya + sb_ref[...].astype(F32) * yb).astype(BF16)
        mo = _nn(mg, wmo[...])
        xo_ref[...] = x_ref[...] + vec_ref[2:3, :] * mo
        ya_ref[...] = ya.astype(BF16)
        yb_ref[...] = yb.astype(BF16)
        mg_ref[...] = mg
        mo_ref[...] = mo.astype(BF16)

    row = pl.BlockSpec((TB, D), lambda i: (i, 0))
    bf = jax.ShapeDtypeStruct((T, D), BF16)
    wv = pltpu.VMEM((D, D), BF16)
    return pl.pallas_call(
        body, name="merge_fwd", grid=(T // TB,),
        in_specs=[row, row, row, row, row, pl.BlockSpec((8, D), lambda i: (0, 0)), ANY, ANY, ANY],
        out_specs=[row] * 5,
        out_shape=[jax.ShapeDtypeStruct((T, D), F32), bf, bf, bf, bf],
        scratch_shapes=[wv, wv, wv],
        compiler_params=_params("arbitrary"),
    )(x, oa, us, sa, sb, vec, w_ho, w_co, w_mo)


def _merge_bwd(dxo, mo, ya, yb, sa, sb, vec, w_ho, w_co, w_mo):
    T = dxo.shape[0]

    def body(dxo_ref, mo_ref, ya_ref, yb_ref, sa_ref, sb_ref, vec_ref, who_hbm, wco_hbm, wmo_hbm,
             dmo_ref, dya_ref, dyb_ref, doa_ref, dus_ref, dp_ref, acc_ref, who, wco, wmo):
        @pl.when(pl.program_id(0) == 0)
        def _():
            pltpu.sync_copy(who_hbm, who)
            pltpu.sync_copy(wco_hbm, wco)
            pltpu.sync_copy(wmo_hbm, wmo)
            acc_ref[...] = jnp.zeros_like(acc_ref)

        dxo = dxo_ref[...]
        acc_ref[2:3, :] += _colsum(mo_ref[...].astype(F32) * dxo)
        dmo = (vec_ref[2:3, :] * dxo).astype(BF16)
        dmo_ref[...] = dmo
        dmg = _nt(dmo, wmo[...])
        sa = sa_ref[...].astype(F32)
        sb = sb_ref[...].astype(F32)
        dya = (sa * dmg).astype(BF16)
        dyb = (sb * dmg).astype(BF16)
        dya_ref[...] = dya
        dyb_ref[...] = dyb
        dp_ref[0] = (dmg * ya_ref[...].astype(F32) * sa * (1.0 - sa)).astype(BF16)
        dp_ref[1] = (dmg * yb_ref[...].astype(F32) * sb * (1.0 - sb)).astype(BF16)
        doa_ref[...] = _nt(dya, who[...]).astype(BF16)
        dus_ref[...] = _nt(dyb, wco[...]).astype(BF16)

    row = pl.BlockSpec((TB, D), lambda i: (i, 0))
    one = pl.BlockSpec((None, TB, D), lambda i: (0, i, 0))
    vec8 = pl.BlockSpec((8, D), lambda i: (0, 0))
    bf = jax.ShapeDtypeStruct((T, D), BF16)
    bf1 = jax.ShapeDtypeStruct((1, T, D), BF16)
    wv = pltpu.VMEM((D, D), BF16)
    return pl.pallas_call(
        body, name="merge_bwd", grid=(T // TB,),
        in_specs=[row, row, row, row, row, row, vec8, ANY, ANY, ANY],
        out_specs=[one, one, one, row, row, pl.BlockSpec((2, TB, D), lambda i: (0, i, 0)), vec8],
        out_shape=[bf1, bf1, bf1, bf, bf, jax.ShapeDtypeStruct((2, T, D), BF16), jax.ShapeDtypeStruct((8, D), F32)],
        scratch_shapes=[wv, wv, wv],
        compiler_params=_params("arbitrary"),
    )(dxo, mo, ya, yb, sa, sb, vec, w_ho, w_co, w_mo)


def _head(x, tgt, gvec):
    T = x.shape[0]

    def body(x_ref, t_ref, g_ref, dx_ref, acc_ref):
        @pl.when(pl.program_id(0) == 0)
        def _():
            acc_ref[...] = jnp.zeros_like(acc_ref)

        x = x_ref[...]
        gf = g_ref[0:1, :]
        r = lax.rsqrt(jnp.mean(x * x, axis=-1, keepdims=True) + EPS)
        n = x * r
        err = n * gf - t_ref[...]
        acc_ref[1:2, :] += _colsum(err * err)
        dy = err * (1.0 / D)
        acc_ref[0:1, :] += _colsum(dy * n)
        dn = dy * gf
        dx_ref[...] = r * (dn - n * jnp.mean(dn * n, axis=-1, keepdims=True))

    row = pl.BlockSpec((TB, D), lambda i: (i, 0))
    vec8 = pl.BlockSpec((8, D), lambda i: (0, 0))
    return pl.pallas_call(
        body, name="loss_head", grid=(T // TB,),
        in_specs=[row, row, vec8], out_specs=[row, vec8],
        out_shape=[jax.ShapeDtypeStruct((T, D), F32), jax.ShapeDtypeStruct((8, D), F32)],
        compiler_params=_params("arbitrary"),
    )(x, tgt, gvec)


def _pack_rows(parts, total, name):
    def body(*refs):
        out = refs[-1]
        out[...] = jnp.zeros_like(out)
        for ref, (_, src, n, dst) in zip(refs[:-1], parts):
            out[dst:dst + n, :] = ref[src:src + n, :]

    arrs = [p[0] for p in parts]
    return pl.pallas_call(
        body, name=name, in_specs=[pl.BlockSpec(a.shape, lambda: (0, 0)) for a in arrs],
        out_specs=pl.BlockSpec((total, D), lambda: (0, 0)),
        out_shape=jax.ShapeDtypeStruct((total, D), F32),
    )(*arrs)


PACK_ROWS = 56
PACK_AT = {"ada_b": 0, "loss": 9, "norm_ffn1": 10, "norm_mix": 11, "hgrn_g": 12, "conv_b": 13, "conv_ln_g": 14,
           "conv_ln_b": 15, "norm_ffn2": 16, "norm_final": 17, "hgrn_lb": 18, "conv_w": 20}


def _local_step(x, tgt, mod, small, wg):
    lb = jax.nn.sigmoid(small["hgrn_lb"][0:1] - small["hgrn_lb"][1:2])
    vec1 = _pack_rows([(mod, 0, 3, 0), (small["norm_ffn1"], 0, 1, 3)], 8, "pack_vec1")
    vec2 = _pack_rows([(mod, 3, 3, 0), (small["norm_mix"], 0, 1, 3), (lb, 0, 1, 4), (small["hgrn_g"], 0, 1, 5)],
                      8, "pack_vec2")
    vec3 = _pack_rows([(mod, 6, 3, 0), (small["norm_ffn2"], 0, 1, 3)], 8, "pack_vec3")
    cvec = _pack_rows([(small["conv_b"], 0, 1, 0), (small["conv_ln_g"], 0, 1, 1), (small["conv_ln_b"], 0, 1, 2)],
                      8, "pack_cvec")
    cw = small["conv_w"]
    gvec = _pack_rows([(small["norm_final"], 0, 1, 0)], 8, "pack_gvec")

    x1, h1, a1, b1, s1, f1 = _ffn_fwd(x, vec1, wg["ffn1_w_in"], wg["ffn1_w_out"], "ffn1_fwd")
    h2, qr, g, k, v, og, u, ua, ub, sa, sb = _mix_proj_fwd(x1, vec2, wg["mix_w_in"])
    oa, o, st = _hgrn_fwd(qr, g, k, v, og, vec2)
    us, uc = _conv_fwd(u, cw, cvec)
    x2, ya, yb, mg, mo = _merge_fwd(x1, oa, us, sa, sb, vec2, wg["hgrn_w_o"], wg["conv_w_o"], wg["mix_w_out"])
    x3, h3, a3, b3, s3, f3 = _ffn_fwd(x2, vec3, wg["ffn2_w_in"], wg["ffn2_w_out"], "ffn2_fwd")

    dx3, acc_head = _head(x3, tgt, gvec)
    dx2, df3, dab3, acc3 = _ffn_bwd(dx3, x2, vec3, a3, b3, f3, wg["ffn2_w_in"], wg["ffn2_w_out"], "ffn2_bwd")
    big = {}
    big["ffn2_w_out"] = _mm_tn(s3, df3, 1, "ffn2_dwout")
    big["ffn2_w_in"] = _mm_tn(h3, dab3, 1, "ffn2_dwin")
    dmo, dya, dyb, doa, dus, dpc, acc_m = _merge_bwd(dx2, mo, ya, yb, sa, sb, vec2,
                                                     wg["hgrn_w_o"], wg["conv_w_o"], wg["mix_w_out"])
    big["mix_w_out"] = _mm_tn(mg, dmo, 1, "mix_dwout")
    big["hgrn_w_o"] = _mm_tn(oa, dya, 1, "hgrn_dwo")
    big["conv_w_o"] = _mm_tn(us, dyb, 1, "conv_dwo")
    duc, acc_c = _conv_bwd_ln(dus, uc, cvec)
    dpb, dcw = _conv_bwd_taps(duc, u, ua, ub, cw)
    dpa, acc_h = _hgrn_bwd(doa, og, qr, g, k, v, o, st, vec2)
    dx1, acc2 = _mix_proj_bwd(dx2, x1, vec2, dpa, dpb, dpc, wg["mix_w_in"])
    big["mix_w_in"] = jnp.concatenate(
        [_mm_tn(h2, dpa, 2, "mix_dwin_a"), _mm_tn(h2, dpb, 2, "mix_dwin_b"), _mm_tn(h2, dpc, 2, "mix_dwin_c")], axis=0)
    dx0, df1, dab1, acc1 = _ffn_bwd(dx1, x, vec1, a1, b1, f1, wg["ffn1_w_in"], wg["ffn1_w_out"], "ffn1_bwd")
    big["ffn1_w_out"] = _mm_tn(s1, df1, 1, "ffn1_dwout")
    big["ffn1_w_in"] = _mm_tn(h1, dab1, 1, "ffn1_dwin")

    at = PACK_AT
    packed = _pack_rows([
        (acc1, 0, 3, at["ada_b"]), (acc2, 0, 2, at["ada_b"] + 3), (acc_m, 2, 1, at["ada_b"] + 5),
        (acc3, 0, 3, at["ada_b"] + 6), (acc_head, 1, 1, at["loss"]), (acc1, 3, 1, at["norm_ffn1"]),
        (acc2, 3, 1, at["norm_mix"]), (acc_h, 0, 1, at["hgrn_g"]), (acc_c, 0, 3, at["conv_b"]),
        (acc3, 3, 1, at["norm_ffn2"]), (acc_head, 0, 1, at["norm_final"]), (acc_h, 1, 2, at["hgrn_lb"]),
        (dcw, 0, CONV_K, at["conv_w"])], PACK_ROWS, "pack_small_grads")
    return dx0, packed, big


BLOCK_BYTES = 3 * 512 * 1024


def _row_block(rows, cols):
    for br in (512, 352, 256, 176, 128, 64, 32, 16, 8):
        if rows % br == 0 and br * cols * 4 <= BLOCK_BYTES:
            return br
    return rows


def _cast_bf16(w, name):
    R, C = w.shape
    br = _row_block(R, C)

    def body(w_ref, o_ref):
        o_ref[...] = w_ref[...].astype(BF16)

    blk = pl.BlockSpec((br, C), lambda i: (i, 0))
    return pl.pallas_call(
        body, name=name, grid=(R // br,), in_specs=[blk], out_specs=blk,
        out_shape=jax.ShapeDtypeStruct((R, C), BF16), compiler_params=_params("parallel"),
    )(w)


def _adamw(w, g, m, v, name):
    R, C = w.shape
    br = _row_block(R, C)

    def body(w_ref, g_ref, m_ref, v_ref, d_ref, nm_ref, nv_ref):
        gv = g_ref[...]
        nm = ADAM_B1 * m_ref[...] + (1.0 - ADAM_B1) * gv
        nv = ADAM_B2 * v_ref[...] + (1.0 - ADAM_B2) * (gv * gv)
        m_hat = nm / (1.0 - ADAM_B1 ** ADAM_STEP)
        v_hat = nv / (1.0 - ADAM_B2 ** ADAM_STEP)
        d_ref[...] = -ADAM_LR * (m_hat / (jnp.sqrt(v_hat) + ADAM_EPS) + ADAM_WD * w_ref[...])
        nm_ref[...] = nm
        nv_ref[...] = nv

    blk = pl.BlockSpec((br, C), lambda i: (i, 0))
    out = jax.ShapeDtypeStruct((R, C), F32)
    return pl.pallas_call(
        body, name=name, grid=(R // br,), in_specs=[blk] * 4, out_specs=[blk] * 3,
        out_shape=[out, out, out], compiler_params=_params("parallel"),
    )(w, g, m, v)


def _coords():
    return lax.axis_index("x"), lax.axis_index("y"), lax.axis_index("c")


def _flip(v, bit):
    return 1 - v if bit else v


def _allgather8(v, name):
    R, C = v.shape

    def body(v_ref, out_ref, send_sems, recv_sems, local_sem):
        x, y, c = _coords()
        me = 4 * x + 2 * y + c
        mine = pltpu.make_async_copy(v_ref, out_ref.at[me], local_sem)
        mine.start()

        def copy(m, block):
            peer = (_flip(x, m & 4), _flip(y, m & 2), _flip(c, m & 1))
            return pltpu.make_async_remote_copy(
                src_ref=v_ref, dst_ref=out_ref.at[block], send_sem=send_sems.at[m - 1],
                recv_sem=recv_sems.at[m - 1], device_id=peer, device_id_type=MESH)

        sends = [copy(m, me) for m in range(1, 8)]
        for cp in sends:
            cp.start()
        for m in range(1, 8):
            sender = 4 * _flip(x, m & 4) + 2 * _flip(y, m & 2) + _flip(c, m & 1)
            copy(m, sender).wait_recv()
        for cp in sends:
            cp.wait_send()
        mine.wait()

    vm = pl.BlockSpec(memory_space=pltpu.VMEM)
    return pl.pallas_call(
        body, name=name, in_specs=[vm], out_specs=vm,
        out_shape=jax.ShapeDtypeStruct((8, R, C), F32),
        scratch_shapes=[pltpu.SemaphoreType.DMA((7,)), pltpu.SemaphoreType.DMA((7,)), pltpu.SemaphoreType.DMA],
    )(v)


def _chip_exchange(srcs, name, gather):
    n = len(srcs)

    def body(*refs):
        ins, outs = refs[:n], refs[n:2 * n]
        send_sems, recv_sems, local_sems = refs[2 * n:]
        x, y, c = _coords()
        k = 2 * x + y

        def src(t, j):
            return ins[t] if gather else ins[t].at[j]

        local = [pltpu.make_async_copy(src(t, k), outs[t].at[k], local_sems.at[t]) for t in range(n)]
        for cp in local:
            cp.start()

        def copy(t, m, recv_side):
            px, py = _flip(x, m & 2), _flip(y, m & 1)
            pk = 2 * px + py
            return pltpu.make_async_remote_copy(
                src_ref=src(t, pk), dst_ref=outs[t].at[pk if recv_side else k],
                send_sem=send_sems.at[3 * t + m - 1], recv_sem=recv_sems.at[3 * t + m - 1],
                device_id=(px, py, c), device_id_type=MESH)

        sends = [copy(t, m, False) for t in range(n) for m in (1, 2, 3)]
        for cp in sends:
            cp.start()
        for t in range(n):
            for m in (1, 2, 3):
                copy(t, m, True).wait_recv()
        for cp in sends:
            cp.wait_send()
        for cp in local:
            cp.wait()

    def out_shape(a):
        return jax.ShapeDtypeStruct(((NCHIP,) + a.shape) if gather else a.shape, a.dtype)

    return pl.pallas_call(
        body, name=name, in_specs=[ANY] * n, out_specs=[ANY] * n,
        out_shape=[out_shape(a) for a in srcs],
        scratch_shapes=[pltpu.SemaphoreType.DMA((3 * n,)), pltpu.SemaphoreType.DMA((3 * n,)),
                        pltpu.SemaphoreType.DMA((n,))],
    )(*srcs)


def _sibling_send_halves(gs, name):
    n = len(gs)

    def body(*refs):
        ins, outs = refs[:n], refs[n:2 * n]
        send_sems, recv_sems = refs[2 * n:]
        x, y, c = _coords()
        copies = []
        for t in range(n):
            half = ins[t].shape[1] // 2
            rows = pl.ds(pl.multiple_of((1 - c) * half, 8), half)
            copies.append(pltpu.make_async_remote_copy(
                src_ref=ins[t].at[:, rows, :], dst_ref=outs[t], send_sem=send_sems.at[t],
                recv_sem=recv_sems.at[t], device_id=(x, y, 1 - c), device_id_type=MESH))
        for cp in copies:
            cp.start()
        for cp in copies:
            cp.wait_recv()
        for cp in copies:
            cp.wait_send()

    return pl.pallas_call(
        body, name=name, in_specs=[ANY] * n, out_specs=[ANY] * n,
        out_shape=[jax.ShapeDtypeStruct((a.shape[0], a.shape[1] // 2, a.shape[2]), a.dtype) for a in gs],
        scratch_shapes=[pltpu.SemaphoreType.DMA((n,)), pltpu.SemaphoreType.DMA((n,))],
    )(*gs)


def _sibling_join_halves(fs, name):
    n = len(fs)

    def body(*refs):
        ins, outs = refs[:n], refs[n:2 * n]
        send_sems, recv_sems, local_sems = refs[2 * n:]
        x, y, c = _coords()
        local, sends, recvs = [], [], []
        for t in range(n):
            half = ins[t].shape[0]
            mine = pl.ds(pl.multiple_of(c * half, 8), half)
            theirs = pl.ds(pl.multiple_of((1 - c) * half, 8), half)
            local.append(pltpu.make_async_copy(ins[t], outs[t].at[mine, :], local_sems.at[t]))
            for rows, lst in ((mine, sends), (theirs, recvs)):
                lst.append(pltpu.make_async_remote_copy(
                    src_ref=ins[t], dst_ref=outs[t].at[rows, :], send_sem=send_sems.at[t],
                    recv_sem=recv_sems.at[t], device_id=(x, y, 1 - c), device_id_type=MESH))
        for cp in local + sends:
            cp.start()
        for cp in recvs:
            cp.wait_recv()
        for cp in sends:
            cp.wait_send()
        for cp in local:
            cp.wait()

    return pl.pallas_call(
        body, name=name, in_specs=[ANY] * n, out_specs=[ANY] * n,
        out_shape=[jax.ShapeDtypeStruct((2 * a.shape[0], a.shape[1]), a.dtype) for a in fs],
        scratch_shapes=[pltpu.SemaphoreType.DMA((n,)), pltpu.SemaphoreType.DMA((n,)),
                        pltpu.SemaphoreType.DMA((n,))],
    )(*fs)


def _sum_own_half(g, ra, cidx, name):
    _, R, C = g.shape
    half = R // 2
    br = _row_block(half, C)
    nb = half // br

    def body(c_ref, g_ref, ra_ref, o_ref):
        o_ref[...] = (g_ref[...] + ra_ref[...]).astype(BF16)

    return pl.pallas_call(
        body, name=name,
        grid_spec=pltpu.PrefetchScalarGridSpec(
            num_scalar_prefetch=1, grid=(NCHIP, nb),
            in_specs=[pl.BlockSpec((None, br, C), lambda j, i, c: (j, c[0] * nb + i, 0)),
                      pl.BlockSpec((None, br, C), lambda j, i, c: (j, i, 0))],
            out_specs=pl.BlockSpec((None, br, C), lambda j, i, c: (j, i, 0))),
        out_shape=jax.ShapeDtypeStruct((NCHIP, half, C), BF16),
        compiler_params=_params("parallel", "parallel"),
    )(cidx, g, ra)


def _sum_chips(rb, name):
    _, half, C = rb.shape
    br = _row_block(half, C)

    def body(rb_ref, o_ref):
        acc = rb_ref[0].astype(F32)
        for j in range(1, NCHIP):
            acc = acc + rb_ref[j].astype(F32)
        o_ref[...] = acc

    return pl.pallas_call(
        body, name=name, grid=(half // br,),
        in_specs=[pl.BlockSpec((NCHIP, br, C), lambda i: (0, i, 0))],
        out_specs=pl.BlockSpec((br, C), lambda i: (i, 0)),
        out_shape=jax.ShapeDtypeStruct((half, C), F32), compiler_params=_params("parallel"),
    )(rb)


def _sum8(ga, name):
    _, R, C = ga.shape

    def body(g_ref, o_ref):
        acc = g_ref[0]
        for j in range(1, 8):
            acc = acc + g_ref[j]
        o_ref[...] = acc

    return pl.pallas_call(
        body, name=name, in_specs=[pl.BlockSpec((8, R, C), lambda: (0, 0, 0))],
        out_specs=pl.BlockSpec((R, C), lambda: (0, 0)), out_shape=jax.ShapeDtypeStruct((R, C), F32),
    )(ga)


ADA_COLS = 9 * D // NCHIP
ADA_BLK = 256


def _ada_mod(c_all, ada_w, ada_b, kidx):
    def body(k_ref, c_ref, w_ref, b_ref, o_ref):
        cv = c_ref[...]
        cs = cv * _sigmoid(cv)
        o_ref[...] = jnp.dot(cs, w_ref[...], precision=lax.Precision.HIGHEST,
                             preferred_element_type=F32) + b_ref[...]

    nblk = ADA_COLS // ADA_BLK
    return pl.pallas_call(
        body, name="ada_mod",
        grid_spec=pltpu.PrefetchScalarGridSpec(
            num_scalar_prefetch=1, grid=(nblk,),
            in_specs=[pl.BlockSpec((8, D), lambda j, k: (0, 0)),
                      pl.BlockSpec((D, ADA_BLK), lambda j, k: (0, j)),
                      pl.BlockSpec((1, ADA_BLK), lambda j, k: (0, k[0] * nblk + j))],
            out_specs=pl.BlockSpec((8, ADA_BLK), lambda j, k: (0, j))),
        out_shape=jax.ShapeDtypeStruct((8, ADA_COLS), F32),
        compiler_params=_params("parallel"),
    )(kidx, c_all, ada_w, ada_b)


def _ada_grad(c_all_t, dmod_all, kidx):
    def body(k_ref, ct_ref, dm_ref, o_ref):
        cv = ct_ref[...]
        cs = cv * _sigmoid(cv)
        acc = cs[:, 0:1] * dm_ref[0:1, :]
        for b in range(1, 8):
            acc = acc + cs[:, b:b + 1] * dm_ref[b:b + 1, :]
        o_ref[...] = acc

    nblk = ADA_COLS // ADA_BLK
    return pl.pallas_call(
        body, name="ada_grad",
        grid_spec=pltpu.PrefetchScalarGridSpec(
            num_scalar_prefetch=1, grid=(nblk,),
            in_specs=[pl.BlockSpec((D, 8), lambda j, k: (0, 0)),
                      pl.BlockSpec((8, ADA_BLK), lambda j, k: (0, k[0] * nblk + j))],
            out_specs=pl.BlockSpec((D, ADA_BLK), lambda j, k: (0, j))),
        out_shape=jax.ShapeDtypeStruct((D, ADA_COLS), F32),
        compiler_params=_params("parallel"),
    )(kidx, c_all_t, dmod_all)


BIG = ("ffn1_w_in", "ffn1_w_out", "mix_w_in", "hgrn_w_o", "conv_w_o", "mix_w_out", "ffn2_w_in", "ffn2_w_out")
ROW_SHARDED = ("ffn1_w_out", "hgrn_w_o", "conv_w_o", "mix_w_out", "ffn2_w_out")
PACK_LEN = {"ada_b": 9, "hgrn_lb": 2}
WEIGHTS = ("ada_w", "ada_b", "norm_ffn1", "ffn1_w_in", "ffn1_w_out", "norm_mix", "mix_w_in", "hgrn_lb", "hgrn_g",
           "hgrn_w_o", "conv_w", "conv_b", "conv_ln_g", "conv_ln_b", "conv_w_o", "mix_w_out", "norm_ffn2",
           "ffn2_w_in", "ffn2_w_out", "norm_final")
PACKED = ("ada_b", "norm_ffn1", "norm_mix", "hgrn_g", "conv_b", "conv_ln_g", "conv_ln_b", "norm_ffn2",
          "norm_final", "hgrn_lb")


def _pack_params(p, name):
    parts = [(p[n].reshape(PACK_LEN.get(n, 1), D), 0, PACK_LEN.get(n, 1), PACK_AT[n]) for n in PACKED]
    return _pack_rows(parts, PACK_ROWS, name)


def _step(w, m, v, x, c, tgt):
    xi, yi, ci = _coords()
    kidx = (2 * xi + yi).astype(jnp.int32).reshape(1)
    cidx = ci.astype(jnp.int32).reshape(1)
    me = 4 * xi + 2 * yi + ci

    c_all = _allgather8(jnp.broadcast_to(c, (8, D)), "gather_c")[:, 0, :]
    mod_cols = _ada_mod(c_all, w["ada_w"][0], w["ada_b"], kidx)
    mod_all = _allgather8(mod_cols, "gather_mod")
    mod = lax.dynamic_slice(mod_all, (0, me, 0), (8, 1, ADA_COLS))[::2].reshape(9, D)

    shards = [_cast_bf16(w[n][0], "cast_" + n) for n in BIG]
    gathered = _chip_exchange(shards, "gather_weights", True)
    wg = {n: (g.reshape(-1, D) if n in ROW_SHARDED else g) for n, g in zip(BIG, gathered)}

    small = {n: w[n].reshape(-1, D) for n in ("norm_ffn1", "norm_mix", "hgrn_lb", "hgrn_g", "conv_b", "conv_ln_g",
                                              "conv_ln_b", "norm_ffn2", "norm_final")}
    small["conv_w"] = _allgather_conv_w(w["conv_w"][0])
    dx, packed, big = _local_step(x[0], tgt[0], mod, small, wg)

    packed_all = _allgather8(packed, "gather_small_grads")
    gsum = _sum8(packed_all, "sum_small_grads")
    loss = (0.5 / D) * jnp.sum(gsum[PACK_AT["loss"]])
    dmod_all = packed_all[:, 0:9, :].reshape(8, 9 * D)
    grads = {"ada_w": _ada_grad(c_all.T, dmod_all, kidx)}
    grads["conv_w"] = lax.dynamic_slice(gsum, (PACK_AT["conv_w"], kidx[0] * (D // NCHIP)), (CONV_K, D // NCHIP))

    gs = [big[n].reshape(NCHIP, -1, big[n].shape[-1]) for n in BIG]
    ra = _sibling_send_halves(gs, "rs_sibling_halves")
    sa = [_sum_own_half(g, r, cidx, "rs_sum_pair_" + n) for g, r, n in zip(gs, ra, BIG)]
    rb = _chip_exchange(sa, "rs_chip_exchange", False)
    fin = [_sum_chips(r, "rs_sum_chips_" + n) for r, n in zip(rb, BIG)]
    full = _sibling_join_halves(fin, "rs_join_halves")
    for n, g in zip(BIG, full):
        grads[n] = g

    delta, new_m, new_v = {}, {}, {}
    for n in ("ada_w",) + BIG + ("conv_w",):
        shape = w[n].shape
        two = (shape[-2], shape[-1])
        d_, m_, v_ = _adamw(w[n].reshape(two), grads[n], m[n].reshape(two), v[n].reshape(two), "adamw_" + n)
        grads[n], delta[n], new_m[n], new_v[n] = (a.reshape(shape) for a in (grads[n], d_, m_, v_))
    pw, pm, pv = (_pack_params(p, "pack_" + s) for p, s in ((w, "w"), (m, "m"), (v, "v")))
    pd, pnm, pnv = _adamw(pw, gsum, pm, pv, "adamw_small")
    for n in PACKED:
        rows = slice(PACK_AT[n], PACK_AT[n] + PACK_LEN.get(n, 1))
        for dst, src in ((grads, gsum), (delta, pd), (new_m, pnm), (new_v, pnv)):
            dst[n] = src[rows].reshape(w[n].shape)

    outs = [loss, dx[None]]
    for d in (grads, delta, new_m, new_v):
        outs += [d[n] for n in WEIGHTS]
    return tuple(outs)


def _allgather_conv_w(cw):
    padded = jnp.pad(cw, ((0, 32 - CONV_K), (0, 0)))
    parts = _allgather8(padded, "gather_conv_w")
    return jnp.concatenate([parts[2 * j] for j in range(NCHIP)], axis=1)


def kernel(x, c, ada_w, ada_b, norm_ffn1, ffn1_w_in, ffn1_w_out, norm_mix, mix_w_in, hgrn_lb, hgrn_g, hgrn_w_o, conv_w, conv_b, conv_ln_g, conv_ln_b, conv_w_o, mix_w_out, norm_ffn2, ffn2_w_in, ffn2_w_out, norm_final, loss_target, m_ada_w, m_ada_b, m_norm_ffn1, m_ffn1_w_in, m_ffn1_w_out, m_norm_mix, m_mix_w_in, m_hgrn_lb, m_hgrn_g, m_hgrn_w_o, m_conv_w, m_conv_b, m_conv_ln_g, m_conv_ln_b, m_conv_w_o, m_mix_w_out, m_norm_ffn2, m_ffn2_w_in, m_ffn2_w_out, m_norm_final, v_ada_w, v_ada_b, v_norm_ffn1, v_ffn1_w_in, v_ffn1_w_out, v_norm_mix, v_mix_w_in, v_hgrn_lb, v_hgrn_g, v_hgrn_w_o, v_conv_w, v_conv_b, v_conv_ln_g, v_conv_ln_b, v_conv_w_o, v_mix_w_out, v_norm_ffn2, v_ffn2_w_in, v_ffn2_w_out, v_norm_final):
    w = dict(ada_w=ada_w, ada_b=ada_b, norm_ffn1=norm_ffn1, ffn1_w_in=ffn1_w_in, ffn1_w_out=ffn1_w_out,
             norm_mix=norm_mix, mix_w_in=mix_w_in, hgrn_lb=hgrn_lb, hgrn_g=hgrn_g, hgrn_w_o=hgrn_w_o, conv_w=conv_w,
             conv_b=conv_b, conv_ln_g=conv_ln_g, conv_ln_b=conv_ln_b, conv_w_o=conv_w_o, mix_w_out=mix_w_out,
             norm_ffn2=norm_ffn2, ffn2_w_in=ffn2_w_in, ffn2_w_out=ffn2_w_out, norm_final=norm_final)
    m = dict(ada_w=m_ada_w, ada_b=m_ada_b, norm_ffn1=m_norm_ffn1, ffn1_w_in=m_ffn1_w_in, ffn1_w_out=m_ffn1_w_out,
             norm_mix=m_norm_mix, mix_w_in=m_mix_w_in, hgrn_lb=m_hgrn_lb, hgrn_g=m_hgrn_g, hgrn_w_o=m_hgrn_w_o,
             conv_w=m_conv_w, conv_b=m_conv_b, conv_ln_g=m_conv_ln_g, conv_ln_b=m_conv_ln_b, conv_w_o=m_conv_w_o,
             mix_w_out=m_mix_w_out, norm_ffn2=m_norm_ffn2, ffn2_w_in=m_ffn2_w_in, ffn2_w_out=m_ffn2_w_out,
             norm_final=m_norm_final)
    v = dict(ada_w=v_ada_w, ada_b=v_ada_b, norm_ffn1=v_norm_ffn1, ffn1_w_in=v_ffn1_w_in, ffn1_w_out=v_ffn1_w_out,
             norm_mix=v_norm_mix, mix_w_in=v_mix_w_in, hgrn_lb=v_hgrn_lb, hgrn_g=v_hgrn_g, hgrn_w_o=v_hgrn_w_o,
             conv_w=v_conv_w, conv_b=v_conv_b, conv_ln_g=v_conv_ln_g, conv_ln_b=v_conv_ln_b, conv_w_o=v_conv_w_o,
             mix_w_out=v_mix_w_out, norm_ffn2=v_norm_ffn2, ffn2_w_in=v_ffn2_w_in, ffn2_w_out=v_ffn2_w_out,
             norm_final=v_norm_final)
    return _step(w, m, v, x, c, loss_target)
```

```python
import functools

import jax
import jax.numpy as jnp
from jax import lax
from jax.experimental import pallas as pl
from jax.experimental.pallas import tpu as pltpu

F32 = jnp.float32
BF16 = jnp.bfloat16

D = 1024
DFF = 2816
NCHIP = 4
FSH = 2 * DFF // NCHIP
HEADS = 8
DK = 128
CHUNK = 64
CONV_K = 31
HALO = 32
EPS = 1e-6
TB = 256
CB = 512
VMEM_LIMIT = 56 * 1024 * 1024

ADAM_LR = 0.001
ADAM_B1 = 0.9
ADAM_B2 = 0.999
ADAM_EPS = 1e-08
ADAM_WD = 0.01
ADAM_STEP = 10

MESH = pl.DeviceIdType.MESH
ANY = pl.BlockSpec(memory_space=pl.ANY)


def _params(*sem):
    return pltpu.CompilerParams(dimension_semantics=sem, vmem_limit_bytes=VMEM_LIMIT)


def _sigmoid(x):
    return 1.0 / (1.0 + jnp.exp(-x))


def _dsilu(x, sg):
    return sg * (1.0 + x * (1.0 - sg))


def _nt(a, b):
    return lax.dot_general(a, b, (((1,), (1,)), ((), ())), preferred_element_type=F32)


def _tn(a, b):
    return lax.dot_general(a, b, (((0,), (0,)), ((), ())), preferred_element_type=F32)


def _nn(a, b):
    return jnp.dot(a, b, preferred_element_type=F32)


def _colsum(x):
    return jnp.sum(x, axis=0, keepdims=True)


def _rms_fwd(x, gn, sc, sh):
    r = lax.rsqrt(jnp.mean(x * x, axis=-1, keepdims=True) + EPS)
    n = x * r
    h = (n * gn) * (1.0 + sc) + sh
    return r, n, h


def _rms_bwd(dh, r, n, gn, sc, acc_ref):
    acc_ref[0:1, :] += _colsum(dh)
    acc_ref[1:2, :] += _colsum(dh * (n * gn))
    dng = dh * (1.0 + sc)
    acc_ref[3:4, :] += _colsum(dng * n)
    dn = dng * gn
    return r * (dn - n * jnp.mean(dn * n, axis=-1, keepdims=True))


def _ffn_fwd(x, vec, w_in, w_out, name):
    T = x.shape[0]

    def body(x_ref, vec_ref, win_hbm, wout_hbm, xo_ref, h_ref, a_ref, b_ref, s_ref, f_ref, win, wout):
        @pl.when(pl.program_id(0) == 0)
        def _():
            pltpu.sync_copy(win_hbm, win)
            pltpu.sync_copy(wout_hbm, wout)

        x = x_ref[...]
        sh, sc, gate, gn = vec_ref[0:1, :], vec_ref[1:2, :], vec_ref[2:3, :], vec_ref[3:4, :]
        _, _, h = _rms_fwd(x, gn, sc, sh)
        hb = h.astype(BF16)
        h_ref[...] = hb
        f = jnp.zeros((TB, D), F32)
        for j in range(2):
            cols = slice(j * FSH, (j + 1) * FSH)
            a = _nn(hb, win[j])
            b = _nn(hb, win[2 + j])
            s = (a * _sigmoid(a) * b).astype(BF16)
            a_ref[:, cols] = a.astype(BF16)
            b_ref[:, cols] = b.astype(BF16)
            s_ref[:, cols] = s
            f = f + _nn(s, wout[cols, :])
        xo_ref[...] = x + (0.5 * gate) * f
        f_ref[...] = f.astype(BF16)

    row = lambda w: pl.BlockSpec((TB, w), lambda i: (i, 0))
    return pl.pallas_call(
        body, name=name, grid=(T // TB,),
        in_specs=[row(D), pl.BlockSpec((8, D), lambda i: (0, 0)), ANY, ANY],
        out_specs=[row(D), row(D), row(DFF), row(DFF), row(DFF), row(D)],
        out_shape=[jax.ShapeDtypeStruct((T, D), F32), jax.ShapeDtypeStruct((T, D), BF16),
                   jax.ShapeDtypeStruct((T, DFF), BF16), jax.ShapeDtypeStruct((T, DFF), BF16),
                   jax.ShapeDtypeStruct((T, DFF), BF16), jax.ShapeDtypeStruct((T, D), BF16)],
        scratch_shapes=[pltpu.VMEM((NCHIP, D, FSH), BF16), pltpu.VMEM((DFF, D), BF16)],
        compiler_params=_params("arbitrary"),
    )(x, vec, w_in, w_out)


def _ffn_bwd(dxo, x, vec, a, b, f, w_in, w_out, name):
    T = x.shape[0]

    def body(dxo_ref, x_ref, vec_ref, a_ref, b_ref, f_ref, win_hbm, wout_hbm,
             dx_ref, df_ref, dab_ref, acc_ref, win, wout):
        @pl.when(pl.program_id(0) == 0)
        def _():
            pltpu.sync_copy(win_hbm, win)
            pltpu.sync_copy(wout_hbm, wout)
            acc_ref[...] = jnp.zeros_like(acc_ref)

        dxo = dxo_ref[...]
        x = x_ref[...]
        sh, sc, gate, gn = vec_ref[0:1, :], vec_ref[1:2, :], vec_ref[2:3, :], vec_ref[3:4, :]
        r, n, _ = _rms_fwd(x, gn, sc, sh)
        acc_ref[2:3, :] += _colsum(0.5 * f_ref[...].astype(F32) * dxo)
        dfb = ((0.5 * gate) * dxo).astype(BF16)
        df_ref[...] = dfb
        dh = jnp.zeros((TB, D), F32)
        for j in range(2):
            cols = slice(j * FSH, (j + 1) * FSH)
            ds = _nt(dfb, wout[cols, :])
            av = a_ref[:, cols].astype(F32)
            bv = b_ref[:, cols].astype(F32)
            sg = _sigmoid(av)
            da = (ds * bv * _dsilu(av, sg)).astype(BF16)
            db = (ds * (av * sg)).astype(BF16)
            dab_ref[j] = da
            dab_ref[2 + j] = db
            dh = dh + _nt(da, win[j]) + _nt(db, win[2 + j])
        dx_ref[...] = dxo + _rms_bwd(dh, r, n, gn, sc, acc_ref)

    row = lambda w: pl.BlockSpec((TB, w), lambda i: (i, 0))
    vec8 = pl.BlockSpec((8, D), lambda i: (0, 0))
    return pl.pallas_call(
        body, name=name, grid=(T // TB,),
        in_specs=[row(D), row(D), vec8, row(DFF), row(DFF), row(D), ANY, ANY],
        out_specs=[row(D), pl.BlockSpec((None, TB, D), lambda i: (0, i, 0)),
                   pl.BlockSpec((NCHIP, TB, FSH), lambda i: (0, i, 0)), vec8],
        out_shape=[jax.ShapeDtypeStruct((T, D), F32), jax.ShapeDtypeStruct((1, T, D), BF16),
                   jax.ShapeDtypeStruct((NCHIP, T, FSH), BF16), jax.ShapeDtypeStruct((8, D), F32)],
        scratch_shapes=[pltpu.VMEM((NCHIP, D, FSH), BF16), pltpu.VMEM((DFF, D), BF16)],
        compiler_params=_params("arbitrary"),
    )(dxo, x, vec, a, b, f, w_in, w_out)


def _mm_tn(a, b3, hp, name):
    T, M = a.shape
    P, _, N = b3.shape
    tm = M if M <= 1408 else M // 2
    tk = 512

    def body(a_ref, b_ref, o_ref):
        @pl.when(pl.program_id(2) == 0)
        def _():
            o_ref[...] = jnp.zeros_like(o_ref)

        o_ref[...] += _tn(a_ref[...], b_ref[...])

    return pl.pallas_call(
        body, name=name, grid=(P, M // tm, T // tk),
        in_specs=[pl.BlockSpec((tk, tm), lambda p, i, k: (k, i)),
                  pl.BlockSpec((None, tk, N), lambda p, i, k: (p, k, 0))],
        out_specs=pl.BlockSpec((None, tm, N), lambda p, i, k: (p // hp, i, p % hp)),
        out_shape=jax.ShapeDtypeStruct((P // hp, M, hp * N), F32),
        compiler_params=_params("parallel", "parallel", "arbitrary"),
    )(a, b3)


def _mix_proj_fwd(x, vec, w_in):
    T = x.shape[0]

    def body(x_ref, vec_ref, w_hbm, h_ref, qr_ref, g_ref, k_ref, v_ref, og_ref, u_ref, ua_ref, ub_ref,
             sa_ref, sb_ref, w):
        @pl.when(pl.program_id(0) == 0)
        def _():
            pltpu.sync_copy(w_hbm, w)

        x = x_ref[...]
        sh, sc, gn, lb = vec_ref[0:1, :], vec_ref[1:2, :], vec_ref[3:4, :], vec_ref[4:5, :]
        _, _, h = _rms_fwd(x, gn, sc, sh)
        hb = h.astype(BF16)
        h_ref[...] = hb
        p = _nn(hb, w[0])
        qr_ref[...] = p[:, :D].astype(BF16)
        fg = lb + (1.0 - lb) * _sigmoid(p[:, D:])
        g_ref[...] = jnp.log(fg)
        k_ref[...] = (1.0 - fg).astype(BF16)
        p = _nn(hb, w[1])
        v_ref[...] = p[:, :D].astype(BF16)
        og_ref[...] = p[:, D:].astype(BF16)
        p = _nn(hb, w[2])
        ua, ub = p[:, :D], p[:, D:]
        u_ref[...] = ua * _sigmoid(ub)
        ua_ref[...] = ua.astype(BF16)
        ub_ref[...] = ub.astype(BF16)
        p = _nn(hb, w[3])
        sa_ref[...] = _sigmoid(p[:, :D]).astype(BF16)
        sb_ref[...] = _sigmoid(p[:, D:]).astype(BF16)

    row = pl.BlockSpec((TB, D), lambda i: (i, 0))
    bf = jax.ShapeDtypeStruct((T, D), BF16)
    f32 = jax.ShapeDtypeStruct((T, D), F32)
    return pl.pallas_call(
        body, name="mix_proj_fwd", grid=(T // TB,),
        in_specs=[row, pl.BlockSpec((8, D), lambda i: (0, 0)), ANY],
        out_specs=[row] * 11,
        out_shape=[bf, bf, f32, bf, bf, bf, f32, bf, bf, bf, bf],
        scratch_shapes=[pltpu.VMEM((NCHIP, D, 2 * D), BF16)],
        compiler_params=_params("arbitrary"),
    )(x, vec, w_in)


def _mix_proj_bwd(dxo, x, vec, dpa, dpb, dpc, w_in):
    T = x.shape[0]

    def body(dxo_ref, x_ref, vec_ref, dpa_ref, dpb_ref, dpc_ref, w_hbm, dx_ref, acc_ref, w):
        @pl.when(pl.program_id(0) == 0)
        def _():
            pltpu.sync_copy(w_hbm, w)
            acc_ref[...] = jnp.zeros_like(acc_ref)

        x = x_ref[...]
        sh, sc, gn = vec_ref[0:1, :], vec_ref[1:2, :], vec_ref[3:4, :]
        r, n, _ = _rms_fwd(x, gn, sc, sh)
        dh = jnp.zeros((TB, D), F32)
        for p in range(8):
            src = dpa_ref[p] if p < 4 else (dpb_ref[p - 4] if p < 6 else dpc_ref[p - 6])
            dh = dh + _nt(src, w[p // 2, :, (p % 2) * D:(p % 2 + 1) * D])
        dx_ref[...] = dxo_ref[...] + _rms_bwd(dh, r, n, gn, sc, acc_ref)

    row = pl.BlockSpec((TB, D), lambda i: (i, 0))
    vec8 = pl.BlockSpec((8, D), lambda i: (0, 0))
    stack = lambda k: pl.BlockSpec((k, TB, D), lambda i: (0, i, 0))
    return pl.pallas_call(
        body, name="mix_proj_bwd", grid=(T // TB,),
        in_specs=[row, row, vec8, stack(4), stack(2), stack(2), ANY],
        out_specs=[row, vec8],
        out_shape=[jax.ShapeDtypeStruct((T, D), F32), jax.ShapeDtypeStruct((8, D), F32)],
        scratch_shapes=[pltpu.VMEM((NCHIP, D, 2 * D), BF16)],
        compiler_params=_params("arbitrary"),
    )(dxo, x, vec, dpa, dpb, dpc, w_in)


def _tri(lower):
    r = lax.broadcasted_iota(jnp.int32, (CHUNK, CHUNK), 0)
    c = lax.broadcasted_iota(jnp.int32, (CHUNK, CHUNK), 1)
    return (c <= r) if lower else (c >= r)


def _cumsum_rows(mask, g):
    return jnp.dot(mask.astype(F32), g, precision=lax.Precision.HIGHEST, preferred_element_type=F32)


def _hgrn_fwd(qr, g, k, v, og, vec):
    T = qr.shape[0]
    nck = CB // CHUNK

    def body(qr_ref, g_ref, k_ref, v_ref, og_ref, vec_ref, out_ref, o_ref, st_ref, state, bsc):
        @pl.when(pl.program_id(1) == 0)
        def _():
            state[...] = jnp.zeros_like(state)

        low = _tri(True)

        def chunk(c, carry):
            rows = pl.ds(pl.multiple_of(c * CHUNK, CHUNK), CHUNK)
            qv = qr_ref[rows, :].astype(F32)
            q = qv * _sigmoid(qv) * (DK ** -0.5)
            kk = k_ref[rows, :].astype(F32)
            vb = v_ref[rows, :]
            bsc[...] = _cumsum_rows(low, g_ref[rows, :])
            b = bsc[...]
            mid = bsc[CHUNK // 2 - 1:CHUNK // 2, :]
            last = bsc[CHUNK - 1:CHUNK, :]
            st = state[...]
            st_ref[c] = st.astype(BF16)
            qt = (q * jnp.exp(b - mid)).astype(BF16)
            kt = (kk * jnp.exp(mid - b)).astype(BF16)
            att = jnp.where(low, _nt(qt, kt), 0.0).astype(BF16)
            qe = (q * jnp.exp(b)).astype(BF16)
            o_ref[rows, :] = _nn(att, vb) + _nt(qe, st.astype(BF16))
            kd = (kk * jnp.exp(last - b)).astype(BF16)
            state[...] = st * jnp.exp(last) + _tn(vb, kd)
            return carry

        lax.fori_loop(0, nck, chunk, 0)
        o = o_ref[...]
        ogv = og_ref[...].astype(F32)
        rms = lax.rsqrt(jnp.mean(o * o, axis=-1, keepdims=True) + EPS)
        out_ref[...] = (o * rms * vec_ref[5:6, :] * (ogv * _sigmoid(ogv))).astype(BF16)

    blk = pl.BlockSpec((CB, DK), lambda h, i: (i, h))
    return pl.pallas_call(
        body, name="hgrn_fwd", grid=(HEADS, T // CB),
        in_specs=[blk, blk, blk, blk, blk, pl.BlockSpec((8, DK), lambda h, i: (0, h))],
        out_specs=[blk, blk, pl.BlockSpec((None, nck, DK, DK), lambda h, i: (h, i, 0, 0))],
        out_shape=[jax.ShapeDtypeStruct((T, D), BF16), jax.ShapeDtypeStruct((T, D), F32),
                   jax.ShapeDtypeStruct((HEADS, T // CHUNK, DK, DK), BF16)],
        scratch_shapes=[pltpu.VMEM((DK, DK), F32), pltpu.VMEM((CHUNK, DK), F32)],
        compiler_params=_params("parallel", "arbitrary"),
    )(qr, g, k, v, og, vec)


def _hgrn_bwd(dout, og, qr, g, k, v, o, st, vec):
    T = qr.shape[0]
    nck = CB // CHUNK
    nb = T // CB

    def body(dout_ref, og_ref, qr_ref, g_ref, k_ref, v_ref, o_ref, st_ref, vec_ref,
             dp_ref, acc_ref, dstate, do_scr, bsc, dbsc):
        @pl.when(pl.program_id(1) == 0)
        def _():
            dstate[...] = jnp.zeros_like(dstate)
            acc_ref[...] = jnp.zeros_like(acc_ref)

        o = o_ref[...]
        ogv = og_ref[...].astype(F32)
        dout = dout_ref[...].astype(F32)
        hg = vec_ref[5:6, :]
        sgo = _sigmoid(ogv)
        rms = lax.rsqrt(jnp.mean(o * o, axis=-1, keepdims=True) + EPS)
        ohat = o * rms
        dp_ref[3] = (dout * (ohat * hg) * _dsilu(ogv, sgo)).astype(BF16)
        don = dout * (ogv * sgo)
        acc_ref[0:1, :] += _colsum(don * ohat)
        dohat = don * hg
        do_scr[...] = rms * (dohat - ohat * jnp.mean(dohat * ohat, axis=-1, keepdims=True))

        low = _tri(True)
        upp = _tri(False)
        lb = vec_ref[4:5, :]

        def chunk(j, carry):
            c = nck - 1 - j
            rows = pl.ds(pl.multiple_of(c * CHUNK, CHUNK), CHUNK)
            qv = qr_ref[rows, :].astype(F32)
            sgq = _sigmoid(qv)
            q = qv * sgq * (DK ** -0.5)
            kk = k_ref[rows, :].astype(F32)
            vb = v_ref[rows, :]
            gv = g_ref[rows, :]
            do = do_scr[rows, :]
            dob = do.astype(BF16)
            bsc[...] = _cumsum_rows(low, gv)
            b = bsc[...]
            mid = bsc[CHUNK // 2 - 1:CHUNK // 2, :]
            last = bsc[CHUNK - 1:CHUNK, :]
            s0 = st_ref[c]
            ds1 = dstate[...]
            ds1b = ds1.astype(BF16)
            eq = jnp.exp(b - mid)
            ek = jnp.exp(mid - b)
            eb = jnp.exp(b)
            ed = jnp.exp(last - b)
            el = jnp.exp(last)
            qt = q * eq
            kt = kk * ek
            qe = q * eb
            kd = kk * ed
            qtb, ktb, qeb, kdb = qt.astype(BF16), kt.astype(BF16), qe.astype(BF16), kd.astype(BF16)
            att = jnp.where(low, _nt(qtb, ktb), 0.0).astype(BF16)
            datt = jnp.where(low, _nt(dob, vb), 0.0).astype(BF16)
            dv = _tn(att, dob) + _nt(kdb, ds1b)
            dqt = _nn(datt, ktb)
            dkt = _tn(datt, qtb)
            dqe = _nn(dob, s0)
            dkd = _nn(vb, ds1b)
            dq = dqt * eq + dqe * eb
            dk = dkt * ek + dkd * ed
            qt, kt, qe, kd = qtb.astype(F32), ktb.astype(F32), qeb.astype(F32), kdb.astype(F32)
            dkdkd = dkd * kd
            dlast = _colsum(dkdkd) + el * _colsum(ds1 * s0.astype(F32))
            dbsc[...] = dqt * qt - dkt * kt + dqe * qe - dkdkd
            dbsc[CHUNK - 1:CHUNK, :] += dlast
            dg = _cumsum_rows(upp, dbsc[...])
            dstate[...] = ds1 * el + _tn(dob, qeb)
            fg = jnp.exp(gv)
            dfg = dg / fg - dk
            sig = (fg - lb) / (1.0 - lb)
            dlb = _colsum(dfg * (1.0 - sig)) * (lb * (1.0 - lb))
            acc_ref[1:2, :] += dlb
            acc_ref[2:3, :] -= dlb
            dp_ref[0, rows, :] = (dq * (DK ** -0.5) * _dsilu(qv, sgq)).astype(BF16)
            dp_ref[1, rows, :] = (dfg * (1.0 - lb) * sig * (1.0 - sig)).astype(BF16)
            dp_ref[2, rows, :] = dv.astype(BF16)
            return carry

        lax.fori_loop(0, nck, chunk, 0)

    blk = pl.BlockSpec((CB, DK), lambda h, i: (nb - 1 - i, h))
    return pl.pallas_call(
        body, name="hgrn_bwd", grid=(HEADS, nb),
        in_specs=[blk, blk, blk, blk, blk, blk, blk,
                  pl.BlockSpec((None, nck, DK, DK), lambda h, i: (h, nb - 1 - i, 0, 0)),
                  pl.BlockSpec((8, DK), lambda h, i: (0, h))],
        out_specs=[pl.BlockSpec((4, CB, DK), lambda h, i: (0, nb - 1 - i, h)),
                   pl.BlockSpec((8, DK), lambda h, i: (0, h))],
        out_shape=[jax.ShapeDtypeStruct((4, T, D), BF16), jax.ShapeDtypeStruct((8, D), F32)],
        scratch_shapes=[pltpu.VMEM((DK, DK), F32), pltpu.VMEM((CB, DK), F32),
                        pltpu.VMEM((CHUNK, DK), F32), pltpu.VMEM((CHUNK, DK), F32)],
        compiler_params=_params("parallel", "arbitrary"),
    )(dout, og, qr, g, k, v, o, st, vec)


def _ln_fwd(uc, lg, lbias):
    mu = jnp.mean(uc, axis=-1, keepdims=True)
    xc = uc - mu
    rstd = lax.rsqrt(jnp.mean(xc * xc, axis=-1, keepdims=True) + EPS)
    z = xc * rstd
    return rstd, z, z * lg + lbias


def _conv_fwd(u, cw, cvec):
    T = u.shape[0]
    per = TB // HALO

    def body(u_ref, halo_ref, cw_ref, cvec_ref, us_ref, uc_ref, pad):
        i = pl.program_id(0)
        pad[0:HALO, :] = jnp.where(i > 0, halo_ref[...], 0.0)
        pad[HALO:, :] = u_ref[...]
        acc = jnp.zeros((TB, D), F32) + cvec_ref[0:1, :]
        for j in range(CONV_K):
            acc = acc + cw_ref[j:j + 1, :] * pad[HALO - (CONV_K - 1) + j:HALO - (CONV_K - 1) + j + TB, :]
        uc_ref[...] = acc
        _, _, ul = _ln_fwd(acc, cvec_ref[1:2, :], cvec_ref[2:3, :])
        us_ref[...] = (ul * _sigmoid(ul)).astype(BF16)

    row = pl.BlockSpec((TB, D), lambda i: (i, 0))
    return pl.pallas_call(
        body, name="conv_fwd", grid=(T // TB,),
        in_specs=[row, pl.BlockSpec((HALO, D), lambda i: (jnp.maximum(i * per - 1, 0), 0)),
                  pl.BlockSpec((32, D), lambda i: (0, 0)), pl.BlockSpec((8, D), lambda i: (0, 0))],
        out_specs=[row, row],
        out_shape=[jax.ShapeDtypeStruct((T, D), BF16), jax.ShapeDtypeStruct((T, D), F32)],
        scratch_shapes=[pltpu.VMEM((TB + HALO, D), F32)],
        compiler_params=_params("parallel"),
    )(u, u, cw, cvec)


def _conv_bwd_ln(dus, uc, cvec):
    T = uc.shape[0]

    def body(dus_ref, uc_ref, cvec_ref, duc_ref, acc_ref):
        @pl.when(pl.program_id(0) == 0)
        def _():
            acc_ref[...] = jnp.zeros_like(acc_ref)

        lg = cvec_ref[1:2, :]
        rstd, z, ul = _ln_fwd(uc_ref[...], lg, cvec_ref[2:3, :])
        dul = dus_ref[...].astype(F32) * _dsilu(ul, _sigmoid(ul))
        acc_ref[1:2, :] += _colsum(dul * z)
        acc_ref[2:3, :] += _colsum(dul)
        dz = dul * lg
        duc = rstd * (dz - jnp.mean(dz, axis=-1, keepdims=True) - z * jnp.mean(dz * z, axis=-1, keepdims=True))
        acc_ref[0:1, :] += _colsum(duc)
        duc_ref[...] = duc

    row = pl.BlockSpec((TB, D), lambda i: (i, 0))
    vec8 = pl.BlockSpec((8, D), lambda i: (0, 0))
    return pl.pallas_call(
        body, name="conv_bwd_ln", grid=(T // TB,),
        in_specs=[row, row, vec8], out_specs=[row, vec8],
        out_shape=[jax.ShapeDtypeStruct((T, D), F32), jax.ShapeDtypeStruct((8, D), F32)],
        compiler_params=_params("arbitrary"),
    )(dus, uc, cvec)


def _conv_bwd_taps(duc, u, ua, ub, cw):
    T = u.shape[0]
    per = TB // HALO
    nblk = T // TB

    def body(duc_ref, dnext_ref, u_ref, uprev_ref, ua_ref, ub_ref, cw_ref, dp_ref, dcw_ref, upad, dpad):
        i = pl.program_id(0)

        @pl.when(i == 0)
        def _():
            dcw_ref[...] = jnp.zeros_like(dcw_ref)

        upad[0:HALO, :] = jnp.where(i > 0, uprev_ref[...], 0.0)
        upad[HALO:, :] = u_ref[...]
        dpad[0:TB, :] = duc_ref[...]
        dpad[TB:, :] = jnp.where(i < nblk - 1, dnext_ref[...], 0.0)
        duc = duc_ref[...]
        du = jnp.zeros((TB, D), F32)
        for j in range(CONV_K):
            lo = HALO - (CONV_K - 1) + j
            dcw_ref[j:j + 1, :] += _colsum(duc * upad[lo:lo + TB, :])
            du = du + cw_ref[j:j + 1, :] * dpad[CONV_K - 1 - j:CONV_K - 1 - j + TB, :]
        ua = ua_ref[...].astype(F32)
        sg = _sigmoid(ub_ref[...].astype(F32))
        dp_ref[0] = (du * sg).astype(BF16)
        dp_ref[1] = (du * ua * sg * (1.0 - sg)).astype(BF16)

    row = pl.BlockSpec((TB, D), lambda i: (i, 0))
    return pl.pallas_call(
        body, name="conv_bwd_taps", grid=(nblk,),
        in_specs=[row, pl.BlockSpec((HALO, D), lambda i: (jnp.minimum((i + 1) * per, T // HALO - 1), 0)),
                  row, pl.BlockSpec((HALO, D), lambda i: (jnp.maximum(i * per - 1, 0), 0)),
                  row, row, pl.BlockSpec((32, D), lambda i: (0, 0))],
        out_specs=[pl.BlockSpec((2, TB, D), lambda i: (0, i, 0)), pl.BlockSpec((32, D), lambda i: (0, 0))],
        out_shape=[jax.ShapeDtypeStruct((2, T, D), BF16), jax.ShapeDtypeStruct((32, D), F32)],
        scratch_shapes=[pltpu.VMEM((TB + HALO, D), F32), pltpu.VMEM((TB + HALO, D), F32)],
        compiler_params=_params("arbitrary"),
    )(duc, duc, u, u, ua, ub, cw)


def _merge_fwd(x, oa, us, sa, sb, vec, w_ho, w_co, w_mo):
    T = x.shape[0]

    def body(x_ref, oa_ref, us_ref, sa_ref, sb_ref, vec_ref, who_hbm, wco_hbm, wmo_hbm,
             xo_ref, ya_ref, yb_ref, mg_ref, mo_ref, who, wco, wmo):
        @pl.when(pl.program_id(0) == 0)
        def _():
            pltpu.sync_copy(who_hbm, who)
            pltpu.sync_copy(wco_hbm, wco)
            pltpu.sync_copy(wmo_hbm, wmo)

        ya = _nn(oa_ref[...], who[...])
        yb = _nn(us_ref[...], wco[...])
        mg = (sa_ref[...].astype(F32) * ya + sb_ref[...].astype(F32) * yb).astype(BF16)
        mo = _nn(mg, wmo[...])
        xo_ref[...] = x_ref[...] + vec_ref[2:3, :] * mo
        ya_ref[...] = ya.astype(BF16)
        yb_ref[...] = yb.astype(BF16)
        mg_ref[...] = mg
        mo_ref[...] = mo.astype(BF16)

    row = pl.BlockSpec((TB, D), lambda i: (i, 0))
    bf = jax.ShapeDtypeStruct((T, D), BF16)
    wv = pltpu.VMEM((D, D), BF16)
    return pl.pallas_call(
        body, name="merge_fwd", grid=(T // TB,),
        in_specs=[row, row, row, row, row, pl.BlockSpec((8, D), lambda i: (0, 0)), ANY, ANY, ANY],
        out_specs=[row] * 5,
        out_shape=[jax.ShapeDtypeStruct((T, D), F32), bf, bf, bf, bf],
        scratch_shapes=[wv, wv, wv],
        compiler_params=_params("arbitrary"),
    )(x, oa, us, sa, sb, vec, w_ho, w_co, w_mo)


def _merge_bwd(dxo, mo, ya, yb, sa, sb, vec, w_ho, w_co, w_mo):
    T = dxo.shape[0]

    def body(dxo_ref, mo_ref, ya_ref, yb_ref, sa_ref, sb_ref, vec_ref, who_hbm, wco_hbm, wmo_hbm,
             dmo_ref, dya_ref, dyb_ref, doa_ref, dus_ref, dp_ref, acc_ref, who, wco, wmo):
        @pl.when(pl.program_id(0) == 0)
        def _():
            pltpu.sync_copy(who_hbm, who)
            pltpu.sync_copy(wco_hbm, wco)
            pltpu.sync_copy(wmo_hbm, wmo)
            acc_ref[...] = jnp.zeros_like(acc_ref)

        dxo = dxo_ref[...]
        acc_ref[2:3, :] += _colsum(mo_ref[...].astype(F32) * dxo)
        dmo = (vec_ref[2:3, :] * dxo).astype(BF16)
        dmo_ref[...] = dmo
        dmg = _nt(dmo, wmo[...])
        sa = sa_ref[...].astype(F32)
        sb = sb_ref[...].astype(F32)
        dya = (sa * dmg).astype(BF16)
        dyb = (sb * dmg).astype(BF16)
        dya_ref[...] = dya
        dyb_ref[...] = dyb
        dp_ref[0] = (dmg * ya_ref[...].astype(F32) * sa * (1.0 - sa)).astype(BF16)
        dp_ref[1] = (dmg * yb_ref[...].astype(F32) * sb * (1.0 - sb)).astype(BF16)
        doa_ref[...] = _nt(dya, who[...]).astype(BF16)
        dus_ref[...] = _nt(dyb, wco[...]).astype(BF16)

    row = pl.BlockSpec((TB, D), lambda i: (i, 0))
    one = pl.BlockSpec((None, TB, D), lambda i: (0, i, 0))
    vec8 = pl.BlockSpec((8, D), lambda i: (0, 0))
    bf = jax.ShapeDtypeStruct((T, D), BF16)
    bf1 = jax.ShapeDtypeStruct((1, T, D), BF16)
    wv = pltpu.VMEM((D, D), BF16)
    return pl.pallas_call(
        body, name="merge_bwd", grid=(T // TB,),
        in_specs=[row, row, row, row, row, row, vec8, ANY, ANY, ANY],
        out_specs=[one, one, one, row, row, pl.BlockSpec((2, TB, D), lambda i: (0, i, 0)), vec8],
        out_shape=[bf1, bf1, bf1, bf, bf, jax.ShapeDtypeStruct((2, T, D), BF16), jax.ShapeDtypeStruct((8, D), F32)],
        scratch_shapes=[wv, wv, wv],
        compiler_params=_params("arbitrary"),
    )(dxo, mo, ya, yb, sa, sb, vec, w_ho, w_co, w_mo)


def _head(x, tgt, gvec):
    T = x.shape[0]

    def body(x_ref, t_ref, g_ref, dx_ref, acc_ref):
        @pl.when(pl.program_id(0) == 0)
        def _():
            acc_ref[...] = jnp.zeros_like(acc_ref)

        x = x_ref[...]
        gf = g_ref[0:1, :]
        r = lax.rsqrt(jnp.mean(x * x, axis=-1, keepdims=True) + EPS)
        n = x * r
        err = n * gf - t_ref[...]
        acc_ref[1:2, :] += _colsum(err * err)
        dy = err * (1.0 / D)
        acc_ref[0:1, :] += _colsum(dy * n)
        dn = dy * gf
        dx_ref[...] = r * (dn - n * jnp.mean(dn * n, axis=-1, keepdims=True))

    row = pl.BlockSpec((TB, D), lambda i: (i, 0))
    vec8 = pl.BlockSpec((8, D), lambda i: (0, 0))
    return pl.pallas_call(
        body, name="loss_head", grid=(T // TB,),
        in_specs=[row, row, vec8], out_specs=[row, vec8],
        out_shape=[jax.ShapeDtypeStruct((T, D), F32), jax.ShapeDtypeStruct((8, D), F32)],
        compiler_params=_params("arbitrary"),
    )(x, tgt, gvec)


def _pack_rows(parts, total, name):
    def body(*refs):
        out = refs[-1]
        out[...] = jnp.zeros_like(out)
        for ref, (_, src, n, dst) in zip(refs[:-1], parts):
            out[dst:dst + n, :] = ref[src:src + n, :]

    arrs = [p[0] for p in parts]
    return pl.pallas_call(
        body, name=name, in_specs=[pl.BlockSpec(a.shape, lambda: (0, 0)) for a in arrs],
        out_specs=pl.BlockSpec((total, D), lambda: (0, 0)),
        out_shape=jax.ShapeDtypeStruct((total, D), F32),
    )(*arrs)


PACK_ROWS = 56
PACK_AT = {"ada_b": 0, "loss": 9, "norm_ffn1": 10, "norm_mix": 11, "hgrn_g": 12, "conv_b": 13, "conv_ln_g": 14,
           "conv_ln_b": 15, "norm_ffn2": 16, "norm_final": 17, "hgrn_lb": 18, "conv_w": 20}


def _local_step(x, tgt, mod, small, weight):
    lb = jax.nn.sigmoid(small["hgrn_lb"][0:1] - small["hgrn_lb"][1:2])
    vec1 = _pack_rows([(mod, 0, 3, 0), (small["norm_ffn1"], 0, 1, 3)], 8, "pack_vec1")
    vec2 = _pack_rows([(mod, 3, 3, 0), (small["norm_mix"], 0, 1, 3), (lb, 0, 1, 4), (small["hgrn_g"], 0, 1, 5)],
                      8, "pack_vec2")
    vec3 = _pack_rows([(mod, 6, 3, 0), (small["norm_ffn2"], 0, 1, 3)], 8, "pack_vec3")
    cvec = _pack_rows([(small["conv_b"], 0, 1, 0), (small["conv_ln_g"], 0, 1, 1), (small["conv_ln_b"], 0, 1, 2)],
                      8, "pack_cvec")
    cw = small["conv_w"]
    gvec = _pack_rows([(small["norm_final"], 0, 1, 0)], 8, "pack_gvec")

    wg = {n: weight(n, vec1) for n in ("ffn1_w_in", "ffn1_w_out")}
    x1, h1, a1, b1, s1, f1 = _ffn_fwd(x, vec1, wg["ffn1_w_in"], wg["ffn1_w_out"], "ffn1_fwd")
    wg["mix_w_in"] = weight("mix_w_in", x1)
    h2, qr, g, k, v, og, u, ua, ub, sa, sb = _mix_proj_fwd(x1, vec2, wg["mix_w_in"])
    oa, o, st = _hgrn_fwd(qr, g, k, v, og, vec2)
    us, uc = _conv_fwd(u, cw, cvec)
    wg.update({n: weight(n, us) for n in ("hgrn_w_o", "conv_w_o", "mix_w_out")})
    x2, ya, yb, mg, mo = _merge_fwd(x1, oa, us, sa, sb, vec2, wg["hgrn_w_o"], wg["conv_w_o"], wg["mix_w_out"])
    wg.update({n: weight(n, x2) for n in ("ffn2_w_in", "ffn2_w_out")})
    x3, h3, a3, b3, s3, f3 = _ffn_fwd(x2, vec3, wg["ffn2_w_in"], wg["ffn2_w_out"], "ffn2_fwd")

    dx3, acc_head = _head(x3, tgt, gvec)
    dx2, df3, dab3, acc3 = _ffn_bwd(dx3, x2, vec3, a3, b3, f3, wg["ffn2_w_in"], wg["ffn2_w_out"], "ffn2_bwd")
    big = {}
    big["ffn2_w_out"] = _mm_tn(s3, df3, 1, "ffn2_dwout")
    big["ffn2_w_in"] = _mm_tn(h3, dab3, 1, "ffn2_dwin")
    dmo, dya, dyb, doa, dus, dpc, acc_m = _merge_bwd(dx2, mo, ya, yb, sa, sb, vec2,
                                                     wg["hgrn_w_o"], wg["conv_w_o"], wg["mix_w_out"])
    big["mix_w_out"] = _mm_tn(mg, dmo, 1, "mix_dwout")
    big["hgrn_w_o"] = _mm_tn(oa, dya, 1, "hgrn_dwo")
    big["conv_w_o"] = _mm_tn(us, dyb, 1, "conv_dwo")
    duc, acc_c = _conv_bwd_ln(dus, uc, cvec)
    dpb, dcw = _conv_bwd_taps(duc, u, ua, ub, cw)
    dpa, acc_h = _hgrn_bwd(doa, og, qr, g, k, v, o, st, vec2)
    dx1, acc2 = _mix_proj_bwd(dx2, x1, vec2, dpa, dpb, dpc, wg["mix_w_in"])
    big["mix_w_in"] = jnp.concatenate(
        [_mm_tn(h2, dpa, 2, "mix_dwin_a"), _mm_tn(h2, dpb, 2, "mix_dwin_b"), _mm_tn(h2, dpc, 2, "mix_dwin_c")], axis=0)
    dx0, df1, dab1, acc1 = _ffn_bwd(dx1, x, vec1, a1, b1, f1, wg["ffn1_w_in"], wg["ffn1_w_out"], "ffn1_bwd")
    big["ffn1_w_out"] = _mm_tn(s1, df1, 1, "ffn1_dwout")
    big["ffn1_w_in"] = _mm_tn(h1, dab1, 1, "ffn1_dwin")

    at = PACK_AT
    packed = _pack_rows([
        (acc1, 0, 3, at["ada_b"]), (acc2, 0, 2, at["ada_b"] + 3), (acc_m, 2, 1, at["ada_b"] + 5),
        (acc3, 0, 3, at["ada_b"] + 6), (acc_head, 1, 1, at["loss"]), (acc1, 3, 1, at["norm_ffn1"]),
        (acc2, 3, 1, at["norm_mix"]), (acc_h, 0, 1, at["hgrn_g"]), (acc_c, 0, 3, at["conv_b"]),
        (acc3, 3, 1, at["norm_ffn2"]), (acc_head, 0, 1, at["norm_final"]), (acc_h, 1, 2, at["hgrn_lb"]),
        (dcw, 0, CONV_K, at["conv_w"])], PACK_ROWS, "pack_small_grads")
    return dx0, packed, big


BLOCK_BYTES = 3 * 512 * 1024


def _row_block(rows, cols):
    for br in (512, 352, 256, 176, 128, 64, 32, 16, 8):
        if rows % br == 0 and br * cols * 4 <= BLOCK_BYTES:
            return br
    return rows


def _cast_into_slot(w, kc, name):
    R, C = w.shape
    br = _row_block(R, C)

    def body(kc_ref, w_ref, o_ref):
        o_ref[...] = w_ref[...].astype(BF16)

    return pl.pallas_call(
        body, name=name,
        grid_spec=pltpu.PrefetchScalarGridSpec(
            num_scalar_prefetch=1, grid=(R // br,),
            in_specs=[pl.BlockSpec((br, C), lambda i, kc: (i, 0))],
            out_specs=pl.BlockSpec((None, br, C), lambda i, kc: (kc[0], i, 0))),
        out_shape=jax.ShapeDtypeStruct((NCHIP, R, C), BF16), compiler_params=_params("parallel"),
    )(kc, w)


def _adamw(w, g, m, v, name):
    R, C = w.shape
    br = _row_block(R, C)

    def body(w_ref, g_ref, m_ref, v_ref, d_ref, nm_ref, nv_ref):
        gv = g_ref[...]
        nm = ADAM_B1 * m_ref[...] + (1.0 - ADAM_B1) * gv
        nv = ADAM_B2 * v_ref[...] + (1.0 - ADAM_B2) * (gv * gv)
        m_hat = nm / (1.0 - ADAM_B1 ** ADAM_STEP)
        v_hat = nv / (1.0 - ADAM_B2 ** ADAM_STEP)
        d_ref[...] = -ADAM_LR * (m_hat / (jnp.sqrt(v_hat) + ADAM_EPS) + ADAM_WD * w_ref[...])
        nm_ref[...] = nm
        nv_ref[...] = nv

    blk = pl.BlockSpec((br, C), lambda i: (i, 0))
    out = jax.ShapeDtypeStruct((R, C), F32)
    return pl.pallas_call(
        body, name=name, grid=(R // br,), in_specs=[blk] * 4, out_specs=[blk] * 3,
        out_shape=[out, out, out], compiler_params=_params("parallel"),
    )(w, g, m, v)


def _coords():
    return lax.axis_index("x"), lax.axis_index("y"), lax.axis_index("c")


def _flip(v, bit):
    return 1 - v if bit else v


def _allgather8(v, name):
    R, C = v.shape

    def body(v_ref, out_ref, send_sems, recv_sems, local_sem):
        x, y, c = _coords()
        me = 4 * x + 2 * y + c
        mine = pltpu.make_async_copy(v_ref, out_ref.at[me], local_sem)
        mine.start()

        def copy(m, block):
            peer = (_flip(x, m & 4), _flip(y, m & 2), _flip(c, m & 1))
            return pltpu.make_async_remote_copy(
                src_ref=v_ref, dst_ref=out_ref.at[block], send_sem=send_sems.at[m - 1],
                recv_sem=recv_sems.at[m - 1], device_id=peer, device_id_type=MESH)

        sends = [copy(m, me) for m in range(1, 8)]
        for cp in sends:
            cp.start()
        for m in range(1, 8):
            sender = 4 * _flip(x, m & 4) + 2 * _flip(y, m & 2) + _flip(c, m & 1)
            copy(m, sender).wait_recv()
        for cp in sends:
            cp.wait_send()
        mine.wait()

    vm = pl.BlockSpec(memory_space=pltpu.VMEM)
    return pl.pallas_call(
        body, name=name, in_specs=[vm], out_specs=vm,
        out_shape=jax.ShapeDtypeStruct((8, R, C), F32),
        scratch_shapes=[pltpu.SemaphoreType.DMA((7,)), pltpu.SemaphoreType.DMA((7,)), pltpu.SemaphoreType.DMA],
    )(v)


HBM = pl.BlockSpec(memory_space=pltpu.HBM)
SEM = pl.BlockSpec(memory_space=pltpu.SEMAPHORE)
EFFECT = pltpu.SideEffectType.DATAFLOW_SIDE_EFFECTING


def _chip_peer(x, y, m):
    px, py = _flip(x, m & 2), _flip(y, m & 1)
    return px, py, 2 * px + py


def _gather_start(lands, groups):
    n, ng = len(lands), len(groups)

    def body(*refs):
        ins = refs[:n]
        sends, recvs = refs[n:n + ng], refs[n + ng:n + 2 * ng]
        token = refs[n + 2 * ng + n]
        x, y, c = _coords()
        k = 2 * x + y
        for gi, grp in enumerate(groups):
            for j, t in enumerate(grp):
                for m in (1, 2, 3):
                    px, py, _ = _chip_peer(x, y, m)
                    pltpu.make_async_remote_copy(
                        src_ref=ins[t].at[k], dst_ref=ins[t].at[k], send_sem=sends[gi].at[3 * j + m - 1],
                        recv_sem=recvs[gi].at[3 * j + m - 1], device_id=(px, py, c), device_id_type=MESH).start()
        token[...] = jnp.zeros_like(token)

    sems = [pltpu.SemaphoreType.DMA((3 * len(g),)) for g in groups]
    out = pl.pallas_call(
        body, name="gather_weights_start",
        out_shape=sems + sems + [pltpu.HBM(a.shape, a.dtype) for a in lands] + [jax.ShapeDtypeStruct((8, 128), F32)],
        in_specs=[HBM] * n, out_specs=[SEM] * (2 * ng) + [HBM] * n + [pl.BlockSpec(memory_space=pltpu.VMEM)],
        input_output_aliases={t: 2 * ng + t for t in range(n)},
        compiler_params=pltpu.CompilerParams(has_side_effects=EFFECT),
    )(*[pltpu.with_memory_space_constraint(a, pltpu.HBM) for a in lands])
    return out[:ng], out[ng:2 * ng], out[2 * ng:2 * ng + n], out[2 * ng + n]


def _gather_wait(lands, send_sem, recv_sem, after, name):
    n = len(lands)

    def body(*refs):
        ins, send, recv = refs[:n], refs[n], refs[n + 1]
        x, y, c = _coords()
        k = 2 * x + y
        for j in range(n):
            for m in (1, 2, 3):
                px, py, pk = _chip_peer(x, y, m)
                cp = pltpu.make_async_remote_copy(
                    src_ref=ins[j].at[k], dst_ref=ins[j].at[pk], send_sem=send.at[3 * j + m - 1],
                    recv_sem=recv.at[3 * j + m - 1], device_id=(px, py, c), device_id_type=MESH)
                cp.wait_send()
                cp.wait_recv()

    return pl.pallas_call(
        body, name=name, out_shape=[pltpu.HBM(a.shape, a.dtype) for a in lands],
        in_specs=[HBM] * n + [SEM, SEM, ANY], out_specs=[HBM] * n,
        input_output_aliases={j: j for j in range(n)},
        compiler_params=pltpu.CompilerParams(has_side_effects=EFFECT),
    )(*lands, send_sem, recv_sem, after)


def _chip_scatter(srcs, name):
    n = len(srcs)

    def body(*refs):
        ins, outs = refs[:n], refs[n:2 * n]
        send_sems, recv_sems = refs[2 * n:]
        x, y, c = _coords()
        k = 2 * x + y

        def copy(t, m, recv_side):
            px, py, pk = _chip_peer(x, y, m)
            return pltpu.make_async_remote_copy(
                src_ref=ins[t].at[pk], dst_ref=outs[t].at[pk if recv_side else k],
                send_sem=send_sems.at[3 * t + m - 1], recv_sem=recv_sems.at[3 * t + m - 1],
                device_id=(px, py, c), device_id_type=MESH)

        sends = [copy(t, m, False) for t in range(n) for m in (1, 2, 3)]
        for cp in sends:
            cp.start()
        for t in range(n):
            for m in (1, 2, 3):
                copy(t, m, True).wait_recv()
        for cp in sends:
            cp.wait_send()

    return pl.pallas_call(
        body, name=name, in_specs=[ANY] * n, out_specs=[ANY] * n,
        out_shape=[jax.ShapeDtypeStruct(a.shape, a.dtype) for a in srcs],
        scratch_shapes=[pltpu.SemaphoreType.DMA((3 * n,)), pltpu.SemaphoreType.DMA((3 * n,))],
    )(*srcs)


def _sibling_send_halves(gs, name):
    n = len(gs)

    def body(*refs):
        ins, outs = refs[:n], refs[n:2 * n]
        send_sems, recv_sems = refs[2 * n:]
        x, y, c = _coords()
        copies = []
        for t in range(n):
            half = ins[t].shape[1] // 2
            rows = pl.ds(pl.multiple_of((1 - c) * half, 8), half)
            copies.append(pltpu.make_async_remote_copy(
                src_ref=ins[t].at[:, rows, :], dst_ref=outs[t], send_sem=send_sems.at[t],
                recv_sem=recv_sems.at[t], device_id=(x, y, 1 - c), device_id_type=MESH))
        for cp in copies:
            cp.start()
        for cp in copies:
            cp.wait_recv()
        for cp in copies:
            cp.wait_send()

    return pl.pallas_call(
        body, name=name, in_specs=[ANY] * n, out_specs=[ANY] * n,
        out_shape=[jax.ShapeDtypeStruct((a.shape[0], a.shape[1] // 2, a.shape[2]), a.dtype) for a in gs],
        scratch_shapes=[pltpu.SemaphoreType.DMA((n,)), pltpu.SemaphoreType.DMA((n,))],
    )(*gs)


def _sibling_join_halves(fs, name):
    n = len(fs)

    def body(*refs):
        ins = refs[:n]
        send_sems, recv_sems = refs[2 * n:]
        x, y, c = _coords()
        sends, recvs = [], []
        for t in range(n):
            half = ins[t].shape[0] // 2
            mine = pl.ds(pl.multiple_of(c * half, 8), half)
            theirs = pl.ds(pl.multiple_of((1 - c) * half, 8), half)
            for rows, lst in ((mine, sends), (theirs, recvs)):
                lst.append(pltpu.make_async_remote_copy(
                    src_ref=ins[t].at[rows, :], dst_ref=ins[t].at[rows, :], send_sem=send_sems.at[t],
                    recv_sem=recv_sems.at[t], device_id=(x, y, 1 - c), device_id_type=MESH))
        for cp in sends:
            cp.start()
        for cp in recvs:
            cp.wait_recv()
        for cp in sends:
            cp.wait_send()

    return pl.pallas_call(
        body, name=name, in_specs=[ANY] * n, out_specs=[ANY] * n,
        out_shape=[jax.ShapeDtypeStruct(a.shape, a.dtype) for a in fs],
        input_output_aliases={t: t for t in range(n)},
        scratch_shapes=[pltpu.SemaphoreType.DMA((n,)), pltpu.SemaphoreType.DMA((n,))],
    )(*fs)


def _sum_own_half(g, ra, kc, name):
    _, R, C = g.shape
    half = R // 2
    br = _row_block(half, C)
    nb = half // br

    def body(kc_ref, g_ref, ra_ref, o_ref):
        o_ref[...] = (g_ref[...] + ra_ref[...]).astype(BF16)

    return pl.pallas_call(
        body, name=name,
        grid_spec=pltpu.PrefetchScalarGridSpec(
            num_scalar_prefetch=1, grid=(NCHIP, nb),
            in_specs=[pl.BlockSpec((None, br, C), lambda j, i, kc: (j, kc[1] * nb + i, 0)),
                      pl.BlockSpec((None, br, C), lambda j, i, kc: (j, i, 0))],
            out_specs=pl.BlockSpec((None, br, C), lambda j, i, kc: (j, i, 0))),
        out_shape=jax.ShapeDtypeStruct((NCHIP, half, C), BF16),
        compiler_params=_params("parallel", "parallel"),
    )(kc, g, ra)


def _sum_chips(sa, rb, kc, name):
    _, half, C = rb.shape
    br = _row_block(half, C)
    nb = half // br

    def body(kc_ref, own_ref, r1_ref, r2_ref, r3_ref, o_ref):
        acc = own_ref[...].astype(F32) + r1_ref[...].astype(F32)
        o_ref[...] = (acc + r2_ref[...].astype(F32)) + r3_ref[...].astype(F32)

    def slab(m):
        return pl.BlockSpec((None, br, C), lambda i, kc: (kc[0] ^ m, i, 0))

    return pl.pallas_call(
        body, name=name,
        grid_spec=pltpu.PrefetchScalarGridSpec(
            num_scalar_prefetch=1, grid=(nb,),
            in_specs=[slab(0), slab(1), slab(2), slab(3)],
            out_specs=pl.BlockSpec((br, C), lambda i, kc: (kc[1] * nb + i, 0))),
        out_shape=jax.ShapeDtypeStruct((2 * half, C), F32), compiler_params=_params("parallel"),
    )(kc, sa, rb, rb, rb)


def _sum8(ga, name):
    _, R, C = ga.shape

    def body(g_ref, o_ref):
        acc = g_ref[0]
        for j in range(1, 8):
            acc = acc + g_ref[j]
        o_ref[...] = acc

    return pl.pallas_call(
        body, name=name, in_specs=[pl.BlockSpec((8, R, C), lambda: (0, 0, 0))],
        out_specs=pl.BlockSpec((R, C), lambda: (0, 0)), out_shape=jax.ShapeDtypeStruct((R, C), F32),
    )(ga)


ADA_COLS = 9 * D // NCHIP
ADA_BLK = 256


def _ada_mod(c_all, ada_w, ada_b, kidx):
    def body(k_ref, c_ref, w_ref, b_ref, o_ref):
        cv = c_ref[...]
        cs = cv * _sigmoid(cv)
        o_ref[...] = jnp.dot(cs, w_ref[...], precision=lax.Precision.HIGHEST,
                             preferred_element_type=F32) + b_ref[...]

    nblk = ADA_COLS // ADA_BLK
    return pl.pallas_call(
        body, name="ada_mod",
        grid_spec=pltpu.PrefetchScalarGridSpec(
            num_scalar_prefetch=1, grid=(nblk,),
            in_specs=[pl.BlockSpec((8, D), lambda j, k: (0, 0)),
                      pl.BlockSpec((D, ADA_BLK), lambda j, k: (0, j)),
                      pl.BlockSpec((1, ADA_BLK), lambda j, k: (0, k[0] * nblk + j))],
            out_specs=pl.BlockSpec((8, ADA_BLK), lambda j, k: (0, j))),
        out_shape=jax.ShapeDtypeStruct((8, ADA_COLS), F32),
        compiler_params=_params("parallel"),
    )(kidx, c_all, ada_w, ada_b)


def _ada_grad(c_all_t, dmod_all, kidx):
    def body(k_ref, ct_ref, dm_ref, o_ref):
        cv = ct_ref[...]
        cs = cv * _sigmoid(cv)
        acc = cs[:, 0:1] * dm_ref[0:1, :]
        for b in range(1, 8):
            acc = acc + cs[:, b:b + 1] * dm_ref[b:b + 1, :]
        o_ref[...] = acc

    nblk = ADA_COLS // ADA_BLK
    return pl.pallas_call(
        body, name="ada_grad",
        grid_spec=pltpu.PrefetchScalarGridSpec(
            num_scalar_prefetch=1, grid=(nblk,),
            in_specs=[pl.BlockSpec((D, 8), lambda j, k: (0, 0)),
                      pl.BlockSpec((8, ADA_BLK), lambda j, k: (0, k[0] * nblk + j))],
            out_specs=pl.BlockSpec((D, ADA_BLK), lambda j, k: (0, j))),
        out_shape=jax.ShapeDtypeStruct((D, ADA_COLS), F32),
        compiler_params=_params("parallel"),
    )(kidx, c_all_t, dmod_all)


BIG = ("ffn1_w_in", "ffn1_w_out", "mix_w_in", "hgrn_w_o", "conv_w_o", "mix_w_out", "ffn2_w_in", "ffn2_w_out")
ROW_SHARDED = ("ffn1_w_out", "hgrn_w_o", "conv_w_o", "mix_w_out", "ffn2_w_out")
GATHER_GROUPS = ((0, 1), (2,), (3, 4, 5), (6, 7))
PACK_LEN = {"ada_b": 9, "hgrn_lb": 2}
WEIGHTS = ("ada_w", "ada_b", "norm_ffn1", "ffn1_w_in", "ffn1_w_out", "norm_mix", "mix_w_in", "hgrn_lb", "hgrn_g",
           "hgrn_w_o", "conv_w", "conv_b", "conv_ln_g", "conv_ln_b", "conv_w_o", "mix_w_out", "norm_ffn2",
           "ffn2_w_in", "ffn2_w_out", "norm_final")
PACKED = ("ada_b", "norm_ffn1", "norm_mix", "hgrn_g", "conv_b", "conv_ln_g", "conv_ln_b", "norm_ffn2",
          "norm_final", "hgrn_lb")


def _pack_params(p, name):
    parts = [(p[n].reshape(PACK_LEN.get(n, 1), D), 0, PACK_LEN.get(n, 1), PACK_AT[n]) for n in PACKED]
    return _pack_rows(parts, PACK_ROWS, name)


def _step(w, m, v, x, c, tgt):
    xi, yi, ci = _coords()
    kidx = (2 * xi + yi).astype(jnp.int32).reshape(1)
    kc = jnp.stack([2 * xi + yi, ci]).astype(jnp.int32)
    me = 4 * xi + 2 * yi + ci

    lands = [_cast_into_slot(w[n][0], kc, "cast_" + n) for n in BIG]
    sends, recvs, lands, token = _gather_start(lands, GATHER_GROUPS)
    ready = {}

    def weight(name, after):
        t = BIG.index(name)
        if t not in ready:
            gi = [t in grp for grp in GATHER_GROUPS].index(True)
            grp = GATHER_GROUPS[gi]
            outs = _gather_wait([lands[j] for j in grp], sends[gi], recvs[gi], after, "gather_weights_wait%d" % gi)
            ready.update(zip(grp, outs))
        return ready[t].reshape(-1, D) if name in ROW_SHARDED else ready[t]

    c_all = _allgather8(jnp.broadcast_to(c, (8, D)) + token[0:1, 0:1], "gather_c")[:, 0, :]
    mod_cols = _ada_mod(c_all, w["ada_w"][0], w["ada_b"], kidx)
    mod_all = _allgather8(mod_cols, "gather_mod")
    mod = lax.dynamic_slice(mod_all, (0, me, 0), (8, 1, ADA_COLS))[::2].reshape(9, D)

    small = {n: w[n].reshape(-1, D) for n in ("norm_ffn1", "norm_mix", "hgrn_lb", "hgrn_g", "conv_b", "conv_ln_g",
                                              "conv_ln_b", "norm_ffn2", "norm_final")}
    small["conv_w"] = _allgather_conv_w(w["conv_w"][0])
    dx, packed, big = _local_step(x[0], tgt[0], mod, small, weight)

    packed_all = _allgather8(packed, "gather_small_grads")
    gsum = _sum8(packed_all, "sum_small_grads")
    loss = (0.5 / D) * jnp.sum(gsum[PACK_AT["loss"]])
    dmod_all = packed_all[:, 0:9, :].reshape(8, 9 * D)
    grads = {"ada_w": _ada_grad(c_all.T, dmod_all, kidx)}
    grads["conv_w"] = lax.dynamic_slice(gsum, (PACK_AT["conv_w"], kidx[0] * (D // NCHIP)), (CONV_K, D // NCHIP))

    gs = [big[n].reshape(NCHIP, -1, big[n].shape[-1]) for n in BIG]
    ra = _sibling_send_halves(gs, "rs_sibling_halves")
    sa = [_sum_own_half(g, r, kc, "rs_sum_pair_" + n) for g, r, n in zip(gs, ra, BIG)]
    rb = _chip_scatter(sa, "rs_chip_exchange")
    fin = [_sum_chips(s, r, kc, "rs_sum_chips_" + n) for s, r, n in zip(sa, rb, BIG)]
    full = _sibling_join_halves(fin, "rs_join_halves")
    for n, g in zip(BIG, full):
        grads[n] = g

    delta, new_m, new_v = {}, {}, {}
    for n in ("ada_w",) + BIG + ("conv_w",):
        shape = w[n].shape
        two = (shape[-2], shape[-1])
        d_, m_, v_ = _adamw(w[n].reshape(two), grads[n], m[n].reshape(two), v[n].reshape(two), "adamw_" + n)
        grads[n], delta[n], new_m[n], new_v[n] = (a.reshape(shape) for a in (grads[n], d_, m_, v_))
    pw, pm, pv = (_pack_params(p, "pack_" + s) for p, s in ((w, "w"), (m, "m"), (v, "v")))
    pd, pnm, pnv = _adamw(pw, gsum, pm, pv, "adamw_small")
    for n in PACKED:
        rows = slice(PACK_AT[n], PACK_AT[n] + PACK_LEN.get(n, 1))
        for dst, src in ((grads, gsum), (delta, pd), (new_m, pnm), (new_v, pnv)):
            dst[n] = src[rows].reshape(w[n].shape)

    outs = [loss, dx[None]]
    for d in (grads, delta, new_m, new_v):
        outs += [d[n] for n in WEIGHTS]
    return tuple(outs)


def _allgather_conv_w(cw):
    padded = jnp.pad(cw, ((0, 32 - CONV_K), (0, 0)))
    parts = _allgather8(padded, "gather_conv_w")
    return jnp.concatenate([parts[2 * j] for j in range(NCHIP)], axis=1)


def kernel(x, c, ada_w, ada_b, norm_ffn1, ffn1_w_in, ffn1_w_out, norm_mix, mix_w_in, hgrn_lb, hgrn_g, hgrn_w_o, conv_w, conv_b, conv_ln_g, conv_ln_b, conv_w_o, mix_w_out, norm_ffn2, ffn2_w_in, ffn2_w_out, norm_final, loss_target, m_ada_w, m_ada_b, m_norm_ffn1, m_ffn1_w_in, m_ffn1_w_out, m_norm_mix, m_mix_w_in, m_hgrn_lb, m_hgrn_g, m_hgrn_w_o, m_conv_w, m_conv_b, m_conv_ln_g, m_conv_ln_b, m_conv_w_o, m_mix_w_out, m_norm_ffn2, m_ffn2_w_in, m_ffn2_w_out, m_norm_final, v_ada_w, v_ada_b, v_norm_ffn1, v_ffn1_w_in, v_ffn1_w_out, v_norm_mix, v_mix_w_in, v_hgrn_lb, v_hgrn_g, v_hgrn_w_o, v_conv_w, v_conv_b, v_conv_ln_g, v_conv_ln_b, v_conv_w_o, v_mix_w_out, v_norm_ffn2, v_ffn2_w_in, v_ffn2_w_out, v_norm_final):
    w = dict(ada_w=ada_w, ada_b=ada_b, norm_ffn1=norm_ffn1, ffn1_w_in=ffn1_w_in, ffn1_w_out=ffn1_w_out,
             norm_mix=norm_mix, mix_w_in=mix_w_in, hgrn_lb=hgrn_lb, hgrn_g=hgrn_g, hgrn_w_o=hgrn_w_o, conv_w=conv_w,
             conv_b=conv_b, conv_ln_g=conv_ln_g, conv_ln_b=conv_ln_b, conv_w_o=conv_w_o, mix_w_out=mix_w_out,
             norm_ffn2=norm_ffn2, ffn2_w_in=ffn2_w_in, ffn2_w_out=ffn2_w_out, norm_final=norm_final)
    m = dict(ada_w=m_ada_w, ada_b=m_ada_b, norm_ffn1=m_norm_ffn1, ffn1_w_in=m_ffn1_w_in, ffn1_w_out=m_ffn1_w_out,
             norm_mix=m_norm_mix, mix_w_in=m_mix_w_in, hgrn_lb=m_hgrn_lb, hgrn_g=m_hgrn_g, hgrn_w_o=m_hgrn_w_o,
             conv_w=m_conv_w, conv_b=m_conv_b, conv_ln_g=m_conv_ln_g, conv_ln_b=m_conv_ln_b, conv_w_o=m_conv_w_o,
             mix_w_out=m_mix_w_out, norm_ffn2=m_norm_ffn2, ffn2_w_in=m_ffn2_w_in, ffn2_w_out=m_ffn2_w_out,
             norm_final=m_norm_final)
    v = dict(ada_w=v_ada_w, ada_b=v_ada_b, norm_ffn1=v_norm_ffn1, ffn1_w_in=v_ffn1_w_in, ffn1_w_out=v_ffn1_w_out,
             norm_mix=v_norm_mix, mix_w_in=v_mix_w_in, hgrn_lb=v_hgrn_lb, hgrn_g=v_hgrn_g, hgrn_w_o=v_hgrn_w_o,
             conv_w=v_conv_w, conv_b=v_conv_b, conv_ln_g=v_conv_ln_g, conv_ln_b=v_conv_ln_b, conv_w_o=v_conv_w_o,
             mix_w_out=v_mix_w_out, norm_ffn2=v_norm_ffn2, ffn2_w_in=v_ffn2_w_in, ffn2_w_out=v_ffn2_w_out,
             norm_final=v_norm_final)
    return _step(w, m, v, x, c, loss_target)
```

```python
import functools

import jax
import jax.numpy as jnp
from jax import lax
from jax.experimental import pallas as pl
from jax.experimental.pallas import tpu as pltpu

F32 = jnp.float32
BF16 = jnp.bfloat16

D = 1024
DFF = 2816
NCHIP = 4
FSH = 2 * DFF // NCHIP
HEADS = 8
DK = 128
CHUNK = 64
CONV_K = 31
HALO = 32
EPS = 1e-6
TB = 256
CB = 512
VMEM_LIMIT = 56 * 1024 * 1024

ADAM_LR = 0.001
ADAM_B1 = 0.9
ADAM_B2 = 0.999
ADAM_EPS = 1e-08
ADAM_WD = 0.01
ADAM_STEP = 10

MESH = pl.DeviceIdType.MESH
ANY = pl.BlockSpec(memory_space=pl.ANY)


def _params(*sem):
    return pltpu.CompilerParams(dimension_semantics=sem, vmem_limit_bytes=VMEM_LIMIT)


def _sigmoid(x):
    return 1.0 / (1.0 + jnp.exp(-x))


def _dsilu(x, sg):
    return sg * (1.0 + x * (1.0 - sg))


def _nt(a, b):
    return lax.dot_general(a, b, (((1,), (1,)), ((), ())), preferred_element_type=F32)


def _tn(a, b):
    return lax.dot_general(a, b, (((0,), (0,)), ((), ())), preferred_element_type=F32)


def _nn(a, b):
    return jnp.dot(a, b, preferred_element_type=F32)


def _colsum(x):
    return jnp.sum(x, axis=0, keepdims=True)


def _rms_fwd(x, gn, sc, sh):
    r = lax.rsqrt(jnp.mean(x * x, axis=-1, keepdims=True) + EPS)
    n = x * r
    h = (n * gn) * (1.0 + sc) + sh
    return r, n, h


def _rms_bwd(dh, r, n, gn, sc, acc_ref):
    acc_ref[0:1, :] += _colsum(dh)
    acc_ref[1:2, :] += _colsum(dh * (n * gn))
    dng = dh * (1.0 + sc)
    acc_ref[3:4, :] += _colsum(dng * n)
    dn = dng * gn
    return r * (dn - n * jnp.mean(dn * n, axis=-1, keepdims=True))


def _ffn_fwd(x, vec, w_in, w_out, name):
    T = x.shape[0]

    def body(x_ref, vec_ref, win_hbm, wout_hbm, xo_ref, h_ref, a_ref, b_ref, s_ref, f_ref, win, wout):
        @pl.when(pl.program_id(0) == 0)
        def _():
            pltpu.sync_copy(win_hbm, win)
            pltpu.sync_copy(wout_hbm, wout)

        x = x_ref[...]
        sh, sc, gate, gn = vec_ref[0:1, :], vec_ref[1:2, :], vec_ref[2:3, :], vec_ref[3:4, :]
        _, _, h = _rms_fwd(x, gn, sc, sh)
        hb = h.astype(BF16)
        h_ref[...] = hb
        f = jnp.zeros((TB, D), F32)
        for j in range(2):
            cols = slice(j * FSH, (j + 1) * FSH)
            a = _nn(hb, win[j])
            b = _nn(hb, win[2 + j])
            s = (a * _sigmoid(a) * b).astype(BF16)
            a_ref[:, cols] = a.astype(BF16)
            b_ref[:, cols] = b.astype(BF16)
            s_ref[:, cols] = s
            f = f + _nn(s, wout[cols, :])
        xo_ref[...] = x + (0.5 * gate) * f
        f_ref[...] = f.astype(BF16)

    row = lambda w: pl.BlockSpec((TB, w), lambda i: (i, 0))
    return pl.pallas_call(
        body, name=name, grid=(T // TB,),
        in_specs=[row(D), pl.BlockSpec((8, D), lambda i: (0, 0)), ANY, ANY],
        out_specs=[row(D), row(D), row(DFF), row(DFF), row(DFF), row(D)],
        out_shape=[jax.ShapeDtypeStruct((T, D), F32), jax.ShapeDtypeStruct((T, D), BF16),
                   jax.ShapeDtypeStruct((T, DFF), BF16), jax.ShapeDtypeStruct((T, DFF), BF16),
                   jax.ShapeDtypeStruct((T, DFF), BF16), jax.ShapeDtypeStruct((T, D), BF16)],
        scratch_shapes=[pltpu.VMEM((NCHIP, D, FSH), BF16), pltpu.VMEM((DFF, D), BF16)],
        compiler_params=_params("arbitrary"),
    )(x, vec, w_in, w_out)


def _ffn_bwd(dxo, x, vec, a, b, f, w_in, w_out, name):
    T = x.shape[0]

    def body(dxo_ref, x_ref, vec_ref, a_ref, b_ref, f_ref, win_hbm, wout_hbm,
             dx_ref, df_ref, dab_ref, acc_ref, win, wout):
        @pl.when(pl.program_id(0) == 0)
        def _():
            pltpu.sync_copy(win_hbm, win)
            pltpu.sync_copy(wout_hbm, wout)
            acc_ref[...] = jnp.zeros_like(acc_ref)

        dxo = dxo_ref[...]
        x = x_ref[...]
        sh, sc, gate, gn = vec_ref[0:1, :], vec_ref[1:2, :], vec_ref[2:3, :], vec_ref[3:4, :]
        r, n, _ = _rms_fwd(x, gn, sc, sh)
        acc_ref[2:3, :] += _colsum(0.5 * f_ref[...].astype(F32) * dxo)
        dfb = ((0.5 * gate) * dxo).astype(BF16)
        df_ref[...] = dfb
        dh = jnp.zeros((TB, D), F32)
        for j in range(2):
            cols = slice(j * FSH, (j + 1) * FSH)
            ds = _nt(dfb, wout[cols, :])
            av = a_ref[:, cols].astype(F32)
            bv = b_ref[:, cols].astype(F32)
            sg = _sigmoid(av)
            da = (ds * bv * _dsilu(av, sg)).astype(BF16)
            db = (ds * (av * sg)).astype(BF16)
            dab_ref[j] = da
            dab_ref[2 + j] = db
            dh = dh + _nt(da, win[j]) + _nt(db, win[2 + j])
        dx_ref[...] = dxo + _rms_bwd(dh, r, n, gn, sc, acc_ref)

    row = lambda w: pl.BlockSpec((TB, w), lambda i: (i, 0))
    vec8 = pl.BlockSpec((8, D), lambda i: (0, 0))
    return pl.pallas_call(
        body, name=name, grid=(T // TB,),
        in_specs=[row(D), row(D), vec8, row(DFF), row(DFF), row(D), ANY, ANY],
        out_specs=[row(D), pl.BlockSpec((None, TB, D), lambda i: (0, i, 0)),
                   pl.BlockSpec((NCHIP, TB, FSH), lambda i: (0, i, 0)), vec8],
        out_shape=[jax.ShapeDtypeStruct((T, D), F32), jax.ShapeDtypeStruct((1, T, D), BF16),
                   jax.ShapeDtypeStruct((NCHIP, T, FSH), BF16), jax.ShapeDtypeStruct((8, D), F32)],
        scratch_shapes=[pltpu.VMEM((NCHIP, D, FSH), BF16), pltpu.VMEM((DFF, D), BF16)],
        compiler_params=_params("arbitrary"),
    )(dxo, x, vec, a, b, f, w_in, w_out)


def _mm_tn(a, b3, hp, name):
    T, M = a.shape
    P, _, N = b3.shape
    tm = M if M <= 1408 else M // 2
    tk = 512

    def body(a_ref, b_ref, o_ref):
        @pl.when(pl.program_id(2) == 0)
        def _():
            o_ref[...] = jnp.zeros_like(o_ref)

        o_ref[...] += _tn(a_ref[...], b_ref[...])

    return pl.pallas_call(
        body, name=name, grid=(P, M // tm, T // tk),
        in_specs=[pl.BlockSpec((tk, tm), lambda p, i, k: (k, i)),
                  pl.BlockSpec((None, tk, N), lambda p, i, k: (p, k, 0))],
        out_specs=pl.BlockSpec((None, tm, N), lambda p, i, k: (p // hp, i, p % hp)),
        out_shape=jax.ShapeDtypeStruct((P // hp, M, hp * N), F32),
        compiler_params=_params("parallel", "parallel", "arbitrary"),
    )(a, b3)


def _mix_proj_fwd(x, vec, w_in):
    T = x.shape[0]

    def body(x_ref, vec_ref, w_hbm, h_ref, qr_ref, g_ref, k_ref, v_ref, og_ref, u_ref, ua_ref, ub_ref,
             sa_ref, sb_ref, w):
        @pl.when(pl.program_id(0) == 0)
        def _():
            pltpu.sync_copy(w_hbm, w)

        x = x_ref[...]
        sh, sc, gn, lb = vec_ref[0:1, :], vec_ref[1:2, :], vec_ref[3:4, :], vec_ref[4:5, :]
        _, _, h = _rms_fwd(x, gn, sc, sh)
        hb = h.astype(BF16)
        h_ref[...] = hb
        p = _nn(hb, w[0])
        qr_ref[...] = p[:, :D].astype(BF16)
        fg = lb + (1.0 - lb) * _sigmoid(p[:, D:])
        g_ref[...] = jnp.log(fg)
        k_ref[...] = (1.0 - fg).astype(BF16)
        p = _nn(hb, w[1])
        v_ref[...] = p[:, :D].astype(BF16)
        og_ref[...] = p[:, D:].astype(BF16)
        p = _nn(hb, w[2])
        ua, ub = p[:, :D], p[:, D:]
        u_ref[...] = ua * _sigmoid(ub)
        ua_ref[...] = ua.astype(BF16)
        ub_ref[...] = ub.astype(BF16)
        p = _nn(hb, w[3])
        sa_ref[...] = _sigmoid(p[:, :D]).astype(BF16)
        sb_ref[...] = _sigmoid(p[:, D:]).astype(BF16)

    row = pl.BlockSpec((TB, D), lambda i: (i, 0))
    bf = jax.ShapeDtypeStruct((T, D), BF16)
    f32 = jax.ShapeDtypeStruct((T, D), F32)
    return pl.pallas_call(
        body, name="mix_proj_fwd", grid=(T // TB,),
        in_specs=[row, pl.BlockSpec((8, D), lambda i: (0, 0)), ANY],
        out_specs=[row] * 11,
        out_shape=[bf, bf, f32, bf, bf, bf, f32, bf, bf, bf, bf],
        scratch_shapes=[pltpu.VMEM((NCHIP, D, 2 * D), BF16)],
        compiler_params=_params("arbitrary"),
    )(x, vec, w_in)


def _mix_proj_bwd(dxo, x, vec, dpa, dpb, dpc, w_in):
    T = x.shape[0]

    def body(dxo_ref, x_ref, vec_ref, dpa_ref, dpb_ref, dpc_ref, w_hbm, dx_ref, acc_ref, w):
        @pl.when(pl.program_id(0) == 0)
        def _():
            pltpu.sync_copy(w_hbm, w)
            acc_ref[...] = jnp.zeros_like(acc_ref)

        x = x_ref[...]
        sh, sc, gn = vec_ref[0:1, :], vec_ref[1:2, :], vec_ref[3:4, :]
        r, n, _ = _rms_fwd(x, gn, sc, sh)
        dh = jnp.zeros((TB, D), F32)
        for p in range(8):
            src = dpa_ref[p] if p < 4 else (dpb_ref[p - 4] if p < 6 else dpc_ref[p - 6])
            dh = dh + _nt(src, w[p // 2, :, (p % 2) * D:(p % 2 + 1) * D])
        dx_ref[...] = dxo_ref[...] + _rms_bwd(dh, r, n, gn, sc, acc_ref)

    row = pl.BlockSpec((TB, D), lambda i: (i, 0))
    vec8 = pl.BlockSpec((8, D), lambda i: (0, 0))
    stack = lambda k: pl.BlockSpec((k, TB, D), lambda i: (0, i, 0))
    return pl.pallas_call(
        body, name="mix_proj_bwd", grid=(T // TB,),
        in_specs=[row, row, vec8, stack(4), stack(2), stack(2), ANY],
        out_specs=[row, vec8],
        out_shape=[jax.ShapeDtypeStruct((T, D), F32), jax.ShapeDtypeStruct((8, D), F32)],
        scratch_shapes=[pltpu.VMEM((NCHIP, D, 2 * D), BF16)],
        compiler_params=_params("arbitrary"),
    )(dxo, x, vec, dpa, dpb, dpc, w_in)


def _tri(lower):
    r = lax.broadcasted_iota(jnp.int32, (CHUNK, CHUNK), 0)
    c = lax.broadcasted_iota(jnp.int32, (CHUNK, CHUNK), 1)
    return (c <= r) if lower else (c >= r)


def _cumsum_rows(mask, g):
    return jnp.dot(mask.astype(F32), g, precision=lax.Precision.HIGHEST, preferred_element_type=F32)


def _hgrn_fwd(qr, g, k, v, og, vec):
    T = qr.shape[0]
    nck = CB // CHUNK

    def body(qr_ref, g_ref, k_ref, v_ref, og_ref, vec_ref, out_ref, o_ref, st_ref, state, bsc):
        @pl.when(pl.program_id(1) == 0)
        def _():
            state[...] = jnp.zeros_like(state)

        low = _tri(True)

        def chunk(c, carry):
            rows = pl.ds(pl.multiple_of(c * CHUNK, CHUNK), CHUNK)
            qv = qr_ref[rows, :].astype(F32)
            q = qv * _sigmoid(qv) * (DK ** -0.5)
            kk = k_ref[rows, :].astype(F32)
            vb = v_ref[rows, :]
            bsc[...] = _cumsum_rows(low, g_ref[rows, :])
            b = bsc[...]
            mid = bsc[CHUNK // 2 - 1:CHUNK // 2, :]
            last = bsc[CHUNK - 1:CHUNK, :]
            st = state[...]
            st_ref[c] = st.astype(BF16)
            qt = (q * jnp.exp(b - mid)).astype(BF16)
            kt = (kk * jnp.exp(mid - b)).astype(BF16)
            att = jnp.where(low, _nt(qt, kt), 0.0).astype(BF16)
            qe = (q * jnp.exp(b)).astype(BF16)
            o_ref[rows, :] = _nn(att, vb) + _nt(qe, st.astype(BF16))
            kd = (kk * jnp.exp(last - b)).astype(BF16)
            state[...] = st * jnp.exp(last) + _tn(vb, kd)
            return carry

        lax.fori_loop(0, nck, chunk, 0)
        o = o_ref[...]
        ogv = og_ref[...].astype(F32)
        rms = lax.rsqrt(jnp.mean(o * o, axis=-1, keepdims=True) + EPS)
        out_ref[...] = (o * rms * vec_ref[5:6, :] * (ogv * _sigmoid(ogv))).astype(BF16)

    blk = pl.BlockSpec((CB, DK), lambda h, i: (i, h))
    return pl.pallas_call(
        body, name="hgrn_fwd", grid=(HEADS, T // CB),
        in_specs=[blk, blk, blk, blk, blk, pl.BlockSpec((8, DK), lambda h, i: (0, h))],
        out_specs=[blk, blk, pl.BlockSpec((None, nck, DK, DK), lambda h, i: (h, i, 0, 0))],
        out_shape=[jax.ShapeDtypeStruct((T, D), BF16), jax.ShapeDtypeStruct((T, D), F32),
                   jax.ShapeDtypeStruct((HEADS, T // CHUNK, DK, DK), BF16)],
        scratch_shapes=[pltpu.VMEM((DK, DK), F32), pltpu.VMEM((CHUNK, DK), F32)],
        compiler_params=_params("parallel", "arbitrary"),
    )(qr, g, k, v, og, vec)


def _hgrn_bwd(dout, og, qr, g, k, v, o, st, vec):
    T = qr.shape[0]
    nck = CB // CHUNK
    nb = T // CB

    def body(dout_ref, og_ref, qr_ref, g_ref, k_ref, v_ref, o_ref, st_ref, vec_ref,
             dp_ref, acc_ref, dstate, do_scr, bsc, dbsc):
        @pl.when(pl.program_id(1) == 0)
        def _():
            dstate[...] = jnp.zeros_like(dstate)
            acc_ref[...] = jnp.zeros_like(acc_ref)

        o = o_ref[...]
        ogv = og_ref[...].astype(F32)
        dout = dout_ref[...].astype(F32)
        hg = vec_ref[5:6, :]
        sgo = _sigmoid(ogv)
        rms = lax.rsqrt(jnp.mean(o * o, axis=-1, keepdims=True) + EPS)
        ohat = o * rms
        dp_ref[3] = (dout * (ohat * hg) * _dsilu(ogv, sgo)).astype(BF16)
        don = dout * (ogv * sgo)
        acc_ref[0:1, :] += _colsum(don * ohat)
        dohat = don * hg
        do_scr[...] = rms * (dohat - ohat * jnp.mean(dohat * ohat, axis=-1, keepdims=True))

        low = _tri(True)
        upp = _tri(False)
        lb = vec_ref[4:5, :]

        def chunk(j, carry):
            c = nck - 1 - j
            rows = pl.ds(pl.multiple_of(c * CHUNK, CHUNK), CHUNK)
            qv = qr_ref[rows, :].astype(F32)
            sgq = _sigmoid(qv)
            q = qv * sgq * (DK ** -0.5)
            kk = k_ref[rows, :].astype(F32)
            vb = v_ref[rows, :]
            gv = g_ref[rows, :]
            do = do_scr[rows, :]
            dob = do.astype(BF16)
            bsc[...] = _cumsum_rows(low, gv)
            b = bsc[...]
            mid = bsc[CHUNK // 2 - 1:CHUNK // 2, :]
            last = bsc[CHUNK - 1:CHUNK, :]
            s0 = st_ref[c]
            ds1 = dstate[...]
            ds1b = ds1.astype(BF16)
            eq = jnp.exp(b - mid)
            ek = jnp.exp(mid - b)
            eb = jnp.exp(b)
            ed = jnp.exp(last - b)
            el = jnp.exp(last)
            qt = q * eq
            kt = kk * ek
            qe = q * eb
            kd = kk * ed
            qtb, ktb, qeb, kdb = qt.astype(BF16), kt.astype(BF16), qe.astype(BF16), kd.astype(BF16)
            att = jnp.where(low, _nt(qtb, ktb), 0.0).astype(BF16)
            datt = jnp.where(low, _nt(dob, vb), 0.0).astype(BF16)
            dv = _tn(att, dob) + _nt(kdb, ds1b)
            dqt = _nn(datt, ktb)
            dkt = _tn(datt, qtb)
            dqe = _nn(dob, s0)
            dkd = _nn(vb, ds1b)
            dq = dqt * eq + dqe * eb
            dk = dkt * ek + dkd * ed
            qt, kt, qe, kd = qtb.astype(F32), ktb.astype(F32), qeb.astype(F32), kdb.astype(F32)
            dkdkd = dkd * kd
            dlast = _colsum(dkdkd) + el * _colsum(ds1 * s0.astype(F32))
            dbsc[...] = dqt * qt - dkt * kt + dqe * qe - dkdkd
            dbsc[CHUNK - 1:CHUNK, :] += dlast
            dg = _cumsum_rows(upp, dbsc[...])
            dstate[...] = ds1 * el + _tn(dob, qeb)
            fg = jnp.exp(gv)
            dfg = dg / fg - dk
            sig = (fg - lb) / (1.0 - lb)
            dlb = _colsum(dfg * (1.0 - sig)) * (lb * (1.0 - lb))
            acc_ref[1:2, :] += dlb
            acc_ref[2:3, :] -= dlb
            dp_ref[0, rows, :] = (dq * (DK ** -0.5) * _dsilu(qv, sgq)).astype(BF16)
            dp_ref[1, rows, :] = (dfg * (1.0 - lb) * sig * (1.0 - sig)).astype(BF16)
            dp_ref[2, rows, :] = dv.astype(BF16)
            return carry

        lax.fori_loop(0, nck, chunk, 0)

    blk = pl.BlockSpec((CB, DK), lambda h, i: (nb - 1 - i, h))
    return pl.pallas_call(
        body, name="hgrn_bwd", grid=(HEADS, nb),
        in_specs=[blk, blk, blk, blk, blk, blk, blk,
                  pl.BlockSpec((None, nck, DK, DK), lambda h, i: (h, nb - 1 - i, 0, 0)),
                  pl.BlockSpec((8, DK), lambda h, i: (0, h))],
        out_specs=[pl.BlockSpec((4, CB, DK), lambda h, i: (0, nb - 1 - i, h)),
                   pl.BlockSpec((8, DK), lambda h, i: (0, h))],
        out_shape=[jax.ShapeDtypeStruct((4, T, D), BF16), jax.ShapeDtypeStruct((8, D), F32)],
        scratch_shapes=[pltpu.VMEM((DK, DK), F32), pltpu.VMEM((CB, DK), F32),
                        pltpu.VMEM((CHUNK, DK), F32), pltpu.VMEM((CHUNK, DK), F32)],
        compiler_params=_params("parallel", "arbitrary"),
    )(dout, og, qr, g, k, v, o, st, vec)


def _ln_fwd(uc, lg, lbias):
    mu = jnp.mean(uc, axis=-1, keepdims=True)
    xc = uc - mu
    rstd = lax.rsqrt(jnp.mean(xc * xc, axis=-1, keepdims=True) + EPS)
    z = xc * rstd
    return rstd, z, z * lg + lbias


def _conv_fwd(u, cw, cvec):
    T = u.shape[0]
    per = TB // HALO

    def body(u_ref, halo_ref, cw_ref, cvec_ref, us_ref, uc_ref, pad):
        i = pl.program_id(0)
        pad[0:HALO, :] = jnp.where(i > 0, halo_ref[...], 0.0)
        pad[HALO:, :] = u_ref[...]
        acc = jnp.zeros((TB, D), F32) + cvec_ref[0:1, :]
        for j in range(CONV_K):
            acc = acc + cw_ref[j:j + 1, :] * pad[HALO - (CONV_K - 1) + j:HALO - (CONV_K - 1) + j + TB, :]
        uc_ref[...] = acc
        _, _, ul = _ln_fwd(acc, cvec_ref[1:2, :], cvec_ref[2:3, :])
        us_ref[...] = (ul * _sigmoid(ul)).astype(BF16)

    row = pl.BlockSpec((TB, D), lambda i: (i, 0))
    return pl.pallas_call(
        body, name="conv_fwd", grid=(T // TB,),
        in_specs=[row, pl.BlockSpec((HALO, D), lambda i: (jnp.maximum(i * per - 1, 0), 0)),
                  pl.BlockSpec((32, D), lambda i: (0, 0)), pl.BlockSpec((8, D), lambda i: (0, 0))],
        out_specs=[row, row],
        out_shape=[jax.ShapeDtypeStruct((T, D), BF16), jax.ShapeDtypeStruct((T, D), F32)],
        scratch_shapes=[pltpu.VMEM((TB + HALO, D), F32)],
        compiler_params=_params("parallel"),
    )(u, u, cw, cvec)


def _conv_bwd_ln(dus, uc, cvec):
    T = uc.shape[0]

    def body(dus_ref, uc_ref, cvec_ref, duc_ref, acc_ref):
        @pl.when(pl.program_id(0) == 0)
        def _():
            acc_ref[...] = jnp.zeros_like(acc_ref)

        lg = cvec_ref[1:2, :]
        rstd, z, ul = _ln_fwd(uc_ref[...], lg, cvec_ref[2:3, :])
        dul = dus_ref[...].astype(F32) * _dsilu(ul, _sigmoid(ul))
        acc_ref[1:2, :] += _colsum(dul * z)
        acc_ref[2:3, :] += _colsum(dul)
        dz = dul * lg
        duc = rstd * (dz - jnp.mean(dz, axis=-1, keepdims=True) - z * jnp.mean(dz * z, axis=-1, keepdims=True))
        acc_ref[0:1, :] += _colsum(duc)
        duc_ref[...] = duc

    row = pl.BlockSpec((TB, D), lambda i: (i, 0))
    vec8 = pl.BlockSpec((8, D), lambda i: (0, 0))
    return pl.pallas_call(
        body, name="conv_bwd_ln", grid=(T // TB,),
        in_specs=[row, row, vec8], out_specs=[row, vec8],
        out_shape=[jax.ShapeDtypeStruct((T, D), F32), jax.ShapeDtypeStruct((8, D), F32)],
        compiler_params=_params("arbitrary"),
    )(dus, uc, cvec)


def _conv_bwd_taps(duc, u, ua, ub, cw):
    T = u.shape[0]
    per = TB // HALO
    nblk = T // TB

    def body(duc_ref, dnext_ref, u_ref, uprev_ref, ua_ref, ub_ref, cw_ref, dp_ref, dcw_ref, upad, dpad):
        i = pl.program_id(0)

        @pl.when(i == 0)
        def _():
            dcw_ref[...] = jnp.zeros_like(dcw_ref)

        upad[0:HALO, :] = jnp.where(i > 0, uprev_ref[...], 0.0)
        upad[HALO:, :] = u_ref[...]
        dpad[0:TB, :] = duc_ref[...]
        dpad[TB:, :] = jnp.where(i < nblk - 1, dnext_ref[...], 0.0)
        duc = duc_ref[...]
        du = jnp.zeros((TB, D), F32)
        for j in range(CONV_K):
            lo = HALO - (CONV_K - 1) + j
            dcw_ref[j:j + 1, :] += _colsum(duc * upad[lo:lo + TB, :])
            du = du + cw_ref[j:j + 1, :] * dpad[CONV_K - 1 - j:CONV_K - 1 - j + TB, :]
        ua = ua_ref[...].astype(F32)
        sg = _sigmoid(ub_ref[...].astype(F32))
        dp_ref[0] = (du * sg).astype(BF16)
        dp_ref[1] = (du * ua * sg * (1.0 - sg)).astype(BF16)

    row = pl.BlockSpec((TB, D), lambda i: (i, 0))
    return pl.pallas_call(
        body, name="conv_bwd_taps", grid=(nblk,),
        in_specs=[row, pl.BlockSpec((HALO, D), lambda i: (jnp.minimum((i + 1) * per, T // HALO - 1), 0)),
                  row, pl.BlockSpec((HALO, D), lambda i: (jnp.maximum(i * per - 1, 0), 0)),
                  row, row, pl.BlockSpec((32, D), lambda i: (0, 0))],
        out_specs=[pl.BlockSpec((2, TB, D), lambda i: (0, i, 0)), pl.BlockSpec((32, D), lambda i: (0, 0))],
        out_shape=[jax.ShapeDtypeStruct((2, T, D), BF16), jax.ShapeDtypeStruct((32, D), F32)],
        scratch_shapes=[pltpu.VMEM((TB + HALO, D), F32), pltpu.VMEM((TB + HALO, D), F32)],
        compiler_params=_params("arbitrary"),
    )(duc, duc, u, u, ua, ub, cw)


def _merge_fwd(x, oa, us, sa, sb, vec, w_ho, w_co, w_mo):
    T = x.shape[0]

    def body(x_ref, oa_ref, us_ref, sa_ref, sb_ref, vec_ref, who_hbm, wco_hbm, wmo_hbm,
             xo_ref, ya_ref, yb_ref, mg_ref, mo_ref, who, wco, wmo):
        @pl.when(pl.program_id(0) == 0)
        def _():
            pltpu.sync_copy(who_hbm, who)
            pltpu.sync_copy(wco_hbm, wco)
            pltpu.sync_copy(wmo_hbm, wmo)

        ya = _nn(oa_ref[...], who[...])
        yb = _nn(us_ref[...], wco[...])
        mg = (sa_ref[...].astype(F32) * ya + sb_ref[...].astype(F32) * yb).astype(BF16)
        mo = _nn(mg, wmo[...])
        xo_ref[...] = x_ref[...] + vec_ref[2:3, :] * mo
        ya_ref[...] = ya.astype(BF16)
        yb_ref[...] = yb.astype(BF16)
        mg_ref[...] = mg
        mo_ref[...] = mo.astype(BF16)

    row = pl.BlockSpec((TB, D), lambda i: (i, 0))
    bf = jax.ShapeDtypeStruct((T, D), BF16)
    wv = pltpu.VMEM((D, D), BF16)
    return pl.pallas_call(
        body, name="merge_fwd", grid=(T // TB,),
        in_specs=[row, row, row, row, row, pl.BlockSpec((8, D), lambda i: (0, 0)), ANY, ANY, ANY],
        out_specs=[row] * 5,
        out_shape=[jax.ShapeDtypeStruct((T, D), F32), bf, bf, bf, bf],
        scratch_shapes=[wv, wv, wv],
        compiler_params=_params("arbitrary"),
    )(x, oa, us, sa, sb, vec, w_ho, w_co, w_mo)


def _merge_bwd(dxo, mo, ya, yb, sa, sb, vec, w_ho, w_co, w_mo):
    T = dxo.shape[0]

    def body(dxo_ref, mo_ref, ya_ref, yb_ref, sa_ref, sb_ref, vec_ref, who_hbm, wco_hbm, wmo_hbm,
             dmo_ref, dya_ref, dyb_ref, doa_ref, dus_ref, dp_ref, acc_ref, who, wco, wmo):
        @pl.when(pl.program_id(0) == 0)
        def _():
            pltpu.sync_copy(who_hbm, who)
            pltpu.sync_copy(wco_hbm, wco)
            pltpu.sync_copy(wmo_hbm, wmo)
            acc_ref[...] = jnp.zeros_like(acc_ref)

        dxo = dxo_ref[...]
        acc_ref[2:3, :] += _colsum(mo_ref[...].astype(F32) * dxo)
        dmo = (vec_ref[2:3, :] * dxo).astype(BF16)
        dmo_ref[...] = dmo
        dmg = _nt(dmo, wmo[...])
        sa = sa_ref[...].astype(F32)
        sb = sb_ref[...].astype(F32)
        dya = (sa * dmg).astype(BF16)
        dyb = (sb * dmg).astype(BF16)
        dya_ref[...] = dya
        dyb_ref[...] = dyb
        dp_ref[0] = (dmg * ya_ref[...].astype(F32) * sa * (1.0 - sa)).astype(BF16)
        dp_ref[1] = (dmg * yb_ref[...].astype(F32) * sb * (1.0 - sb)).astype(BF16)
        doa_ref[...] = _nt(dya, who[...]).astype(BF16)
        dus_ref[...] = _nt(dyb, wco[...]).astype(BF16)

    row = pl.BlockSpec((TB, D), lambda i: (i, 0))
    one = pl.BlockSpec((None, TB, D), lambda i: (0, i, 0))
    vec8 = pl.BlockSpec((8, D), lambda i: (0, 0))
    bf = jax.ShapeDtypeStruct((T, D), BF16)
    bf1 = jax.ShapeDtypeStruct((1, T, D), BF16)
    wv = pltpu.VMEM((D, D), BF16)
    return pl.pallas_call(
        body, name="merge_bwd", grid=(T // TB,),
        in_specs=[row, row, row, row, row, row, vec8, ANY, ANY, ANY],
        out_specs=[one, one, one, row, row, pl.BlockSpec((2, TB, D), lambda i: (0, i, 0)), vec8],
        out_shape=[bf1, bf1, bf1, bf, bf, jax.ShapeDtypeStruct((2, T, D), BF16), jax.ShapeDtypeStruct((8, D), F32)],
        scratch_shapes=[wv, wv, wv],
        compiler_params=_params("arbitrary"),
    )(dxo, mo, ya, yb, sa, sb, vec, w_ho, w_co, w_mo)


def _head(x, tgt, gvec):
    T = x.shape[0]

    def body(x_ref, t_ref, g_ref, dx_ref, acc_ref):
        @pl.when(pl.program_id(0) == 0)
        def _():
            acc_ref[...] = jnp.zeros_like(acc_ref)

        x = x_ref[...]
        gf = g_ref[0:1, :]
        r = lax.rsqrt(jnp.mean(x * x, axis=-1, keepdims=True) + EPS)
        n = x * r
        err = n * gf - t_ref[...]
        acc_ref[1:2, :] += _colsum(err * err)
        dy = err * (1.0 / D)
        acc_ref[0:1, :] += _colsum(dy * n)
        dn = dy * gf
        dx_ref[...] = r * (dn - n * jnp.mean(dn * n, axis=-1, keepdims=True))

    row = pl.BlockSpec((TB, D), lambda i: (i, 0))
    vec8 = pl.BlockSpec((8, D), lambda i: (0, 0))
    return pl.pallas_call(
        body, name="loss_head", grid=(T // TB,),
        in_specs=[row, row, vec8], out_specs=[row, vec8],
        out_shape=[jax.ShapeDtypeStruct((T, D), F32), jax.ShapeDtypeStruct((8, D), F32)],
        compiler_params=_params("arbitrary"),
    )(x, tgt, gvec)


def _pack_rows(parts, total, name):
    def body(*refs):
        out = refs[-1]
        out[...] = jnp.zeros_like(out)
        for ref, (_, src, n, dst) in zip(refs[:-1], parts):
            out[dst:dst + n, :] = ref[src:src + n, :]

    arrs = [p[0] for p in parts]
    return pl.pallas_call(
        body, name=name, in_specs=[pl.BlockSpec(a.shape, lambda: (0, 0)) for a in arrs],
        out_specs=pl.BlockSpec((total, D), lambda: (0, 0)),
        out_shape=jax.ShapeDtypeStruct((total, D), F32),
    )(*arrs)


PACK_ROWS = 56
PACK_AT = {"ada_b": 0, "loss": 9, "norm_ffn1": 10, "norm_mix": 11, "hgrn_g": 12, "conv_b": 13, "conv_ln_g": 14,
           "conv_ln_b": 15, "norm_ffn2": 16, "norm_final": 17, "hgrn_lb": 18, "conv_w": 20}


def _local_step(x, tgt, mod, small, weight, reduce, reduce_small):
    lb = jax.nn.sigmoid(small["hgrn_lb"][0:1] - small["hgrn_lb"][1:2])
    vec1 = _pack_rows([(mod, 0, 3, 0), (small["norm_ffn1"], 0, 1, 3)], 8, "pack_vec1")
    vec2 = _pack_rows([(mod, 3, 3, 0), (small["norm_mix"], 0, 1, 3), (lb, 0, 1, 4), (small["hgrn_g"], 0, 1, 5)],
                      8, "pack_vec2")
    vec3 = _pack_rows([(mod, 6, 3, 0), (small["norm_ffn2"], 0, 1, 3)], 8, "pack_vec3")
    cvec = _pack_rows([(small["conv_b"], 0, 1, 0), (small["conv_ln_g"], 0, 1, 1), (small["conv_ln_b"], 0, 1, 2)],
                      8, "pack_cvec")
    cw = small["conv_w"]
    gvec = _pack_rows([(small["norm_final"], 0, 1, 0)], 8, "pack_gvec")

    wg = {n: weight(n, vec1) for n in ("ffn1_w_in", "ffn1_w_out")}
    x1, h1, a1, b1, s1, f1 = _ffn_fwd(x, vec1, wg["ffn1_w_in"], wg["ffn1_w_out"], "ffn1_fwd")
    wg["mix_w_in"] = weight("mix_w_in", x1)
    h2, qr, g, k, v, og, u, ua, ub, sa, sb = _mix_proj_fwd(x1, vec2, wg["mix_w_in"])
    oa, o, st = _hgrn_fwd(qr, g, k, v, og, vec2)
    us, uc = _conv_fwd(u, cw, cvec)
    wg.update({n: weight(n, us) for n in ("hgrn_w_o", "conv_w_o", "mix_w_out")})
    x2, ya, yb, mg, mo = _merge_fwd(x1, oa, us, sa, sb, vec2, wg["hgrn_w_o"], wg["conv_w_o"], wg["mix_w_out"])
    wg.update({n: weight(n, x2) for n in ("ffn2_w_in", "ffn2_w_out")})
    x3, h3, a3, b3, s3, f3 = _ffn_fwd(x2, vec3, wg["ffn2_w_in"], wg["ffn2_w_out"], "ffn2_fwd")

    dx3, acc_head = _head(x3, tgt, gvec)
    dx2, df3, dab3, acc3 = _ffn_bwd(dx3, x2, vec3, a3, b3, f3, wg["ffn2_w_in"], wg["ffn2_w_out"], "ffn2_bwd")
    tok = reduce(("ffn2_w_out", "ffn2_w_in"),
                 [_mm_tn(s3, df3, 1, "ffn2_dwout"), _mm_tn(h3, dab3, 1, "ffn2_dwin")])
    vec2b = vec2 + tok[0:1, 0:1]
    dmo, dya, dyb, doa, dus, dpc, acc_m = _merge_bwd(dx2, mo, ya, yb, sa, sb, vec2b,
                                                     wg["hgrn_w_o"], wg["conv_w_o"], wg["mix_w_out"])
    tok = reduce(("mix_w_out", "hgrn_w_o", "conv_w_o"),
                 [_mm_tn(mg, dmo, 1, "mix_dwout"), _mm_tn(oa, dya, 1, "hgrn_dwo"), _mm_tn(us, dyb, 1, "conv_dwo")])
    vec2c = vec2 + tok[0:1, 0:1]
    duc, acc_c = _conv_bwd_ln(dus, uc, cvec)
    dpb, dcw = _conv_bwd_taps(duc, u, ua, ub, cw)
    dpa, acc_h = _hgrn_bwd(doa, og, qr, g, k, v, o, st, vec2c)
    dx1, acc2 = _mix_proj_bwd(dx2, x1, vec2c, dpa, dpb, dpc, wg["mix_w_in"])
    tok = reduce(("mix_w_in",), [jnp.concatenate(
        [_mm_tn(h2, dpa, 2, "mix_dwin_a"), _mm_tn(h2, dpb, 2, "mix_dwin_b"), _mm_tn(h2, dpc, 2, "mix_dwin_c")],
        axis=0)])
    vec1b = vec1 + tok[0:1, 0:1]
    dx0, df1, dab1, acc1 = _ffn_bwd(dx1, x, vec1b, a1, b1, f1, wg["ffn1_w_in"], wg["ffn1_w_out"], "ffn1_bwd")

    at = PACK_AT
    packed = _pack_rows([
        (acc1, 0, 3, at["ada_b"]), (acc2, 0, 2, at["ada_b"] + 3), (acc_m, 2, 1, at["ada_b"] + 5),
        (acc3, 0, 3, at["ada_b"] + 6), (acc_head, 1, 1, at["loss"]), (acc1, 3, 1, at["norm_ffn1"]),
        (acc2, 3, 1, at["norm_mix"]), (acc_h, 0, 1, at["hgrn_g"]), (acc_c, 0, 3, at["conv_b"]),
        (acc3, 3, 1, at["norm_ffn2"]), (acc_head, 0, 1, at["norm_final"]), (acc_h, 1, 2, at["hgrn_lb"]),
        (dcw, 0, CONV_K, at["conv_w"])], PACK_ROWS, "pack_small_grads")
    done = reduce_small(packed)
    reduce(("ffn1_w_out", "ffn1_w_in"), [_mm_tn(s1, df1, 1, "ffn1_dwout"), _mm_tn(h1, dab1, 1, "ffn1_dwin")], done)
    return dx0


BLOCK_BYTES = 3 * 512 * 1024


def _row_block(rows, cols):
    for br in (512, 352, 256, 176, 128, 64, 32, 16, 8):
        if rows % br == 0 and br * cols * 4 <= BLOCK_BYTES:
            return br
    return rows


def _cast_into_slot(w, kc, name):
    R, C = w.shape
    br = _row_block(R, C)

    def body(kc_ref, w_ref, o_ref):
        o_ref[...] = w_ref[...].astype(BF16)

    return pl.pallas_call(
        body, name=name,
        grid_spec=pltpu.PrefetchScalarGridSpec(
            num_scalar_prefetch=1, grid=(R // br,),
            in_specs=[pl.BlockSpec((br, C), lambda i, kc: (i, 0))],
            out_specs=pl.BlockSpec((None, br, C), lambda i, kc: (kc[0], i, 0))),
        out_shape=jax.ShapeDtypeStruct((NCHIP, R, C), BF16), compiler_params=_params("parallel"),
    )(kc, w)


def _adamw(w, g, m, v, name):
    R, C = w.shape
    br = _row_block(R, C)

    def body(w_ref, g_ref, m_ref, v_ref, d_ref, nm_ref, nv_ref):
        gv = g_ref[...]
        nm = ADAM_B1 * m_ref[...] + (1.0 - ADAM_B1) * gv
        nv = ADAM_B2 * v_ref[...] + (1.0 - ADAM_B2) * (gv * gv)
        m_hat = nm / (1.0 - ADAM_B1 ** ADAM_STEP)
        v_hat = nv / (1.0 - ADAM_B2 ** ADAM_STEP)
        d_ref[...] = -ADAM_LR * (m_hat / (jnp.sqrt(v_hat) + ADAM_EPS) + ADAM_WD * w_ref[...])
        nm_ref[...] = nm
        nv_ref[...] = nv

    blk = pl.BlockSpec((br, C), lambda i: (i, 0))
    out = jax.ShapeDtypeStruct((R, C), F32)
    return pl.pallas_call(
        body, name=name, grid=(R // br,), in_specs=[blk] * 4, out_specs=[blk] * 3,
        out_shape=[out, out, out], compiler_params=_params("parallel"),
    )(w, g, m, v)


def _coords():
    return lax.axis_index("x"), lax.axis_index("y"), lax.axis_index("c")


def _flip(v, bit):
    return 1 - v if bit else v


def _allgather8(v, name):
    R, C = v.shape

    def body(v_ref, out_ref, send_sems, recv_sems, local_sem):
        x, y, c = _coords()
        me = 4 * x + 2 * y + c
        mine = pltpu.make_async_copy(v_ref, out_ref.at[me], local_sem)
        mine.start()

        def copy(m, block):
            peer = (_flip(x, m & 4), _flip(y, m & 2), _flip(c, m & 1))
            return pltpu.make_async_remote_copy(
                src_ref=v_ref, dst_ref=out_ref.at[block], send_sem=send_sems.at[m - 1],
                recv_sem=recv_sems.at[m - 1], device_id=peer, device_id_type=MESH)

        sends = [copy(m, me) for m in range(1, 8)]
        for cp in sends:
            cp.start()
        for m in range(1, 8):
            sender = 4 * _flip(x, m & 4) + 2 * _flip(y, m & 2) + _flip(c, m & 1)
            copy(m, sender).wait_recv()
        for cp in sends:
            cp.wait_send()
        mine.wait()

    vm = pl.BlockSpec(memory_space=pltpu.VMEM)
    return pl.pallas_call(
        body, name=name, in_specs=[vm], out_specs=vm,
        out_shape=jax.ShapeDtypeStruct((8, R, C), F32),
        scratch_shapes=[pltpu.SemaphoreType.DMA((7,)), pltpu.SemaphoreType.DMA((7,)), pltpu.SemaphoreType.DMA],
    )(v)


HBM = pl.BlockSpec(memory_space=pltpu.HBM)
SEM = pl.BlockSpec(memory_space=pltpu.SEMAPHORE)
EFFECT = pltpu.SideEffectType.DATAFLOW_SIDE_EFFECTING


def _chip_peer(x, y, m):
    px, py = _flip(x, m & 2), _flip(y, m & 1)
    return px, py, 2 * px + py


def _gather_start(lands, groups, after):
    n, ng = len(lands), len(groups)

    def body(*refs):
        ins = refs[:n]
        sends, recvs = refs[n + 1:n + 1 + ng], refs[n + 1 + ng:n + 1 + 2 * ng]
        token = refs[n + 1 + 2 * ng + n]
        x, y, c = _coords()
        k = 2 * x + y
        for gi, grp in enumerate(groups):
            for j, t in enumerate(grp):
                for m in (1, 2, 3):
                    px, py, _ = _chip_peer(x, y, m)
                    pltpu.make_async_remote_copy(
                        src_ref=ins[t].at[k], dst_ref=ins[t].at[k], send_sem=sends[gi].at[3 * j + m - 1],
                        recv_sem=recvs[gi].at[3 * j + m - 1], device_id=(px, py, c), device_id_type=MESH).start()
        token[...] = jnp.zeros_like(token)

    sems = [pltpu.SemaphoreType.DMA((3 * len(g),)) for g in groups]
    out = pl.pallas_call(
        body, name="gather_weights_start",
        out_shape=sems + sems + [pltpu.HBM(a.shape, a.dtype) for a in lands] + [jax.ShapeDtypeStruct((8, 128), F32)],
        in_specs=[HBM] * n + [ANY], out_specs=[SEM] * (2 * ng) + [HBM] * n + [pl.BlockSpec(memory_space=pltpu.VMEM)],
        input_output_aliases={t: 2 * ng + t for t in range(n)},
        compiler_params=pltpu.CompilerParams(has_side_effects=EFFECT),
    )(*[pltpu.with_memory_space_constraint(a, pltpu.HBM) for a in lands], after)
    return out[:ng], out[ng:2 * ng], out[2 * ng:2 * ng + n], out[2 * ng + n]


def _gather_wait(lands, send_sem, recv_sem, after, name):
    n = len(lands)

    def body(*refs):
        ins, send, recv = refs[:n], refs[n], refs[n + 1]
        x, y, c = _coords()
        k = 2 * x + y
        for j in range(n):
            for m in (1, 2, 3):
                px, py, pk = _chip_peer(x, y, m)
                cp = pltpu.make_async_remote_copy(
                    src_ref=ins[j].at[k], dst_ref=ins[j].at[pk], send_sem=send.at[3 * j + m - 1],
                    recv_sem=recv.at[3 * j + m - 1], device_id=(px, py, c), device_id_type=MESH)
                cp.wait_send()
                cp.wait_recv()

    return pl.pallas_call(
        body, name=name, out_shape=[pltpu.HBM(a.shape, a.dtype) for a in lands],
        in_specs=[HBM] * n + [SEM, SEM, ANY], out_specs=[HBM] * n,
        input_output_aliases={j: j for j in range(n)},
        compiler_params=pltpu.CompilerParams(has_side_effects=EFFECT),
    )(*lands, send_sem, recv_sem, after)


def _scatter_start(srcs, name):
    n = len(srcs)

    def body(*refs):
        ins, lands = refs[:n], refs[n:2 * n]
        send, recv = refs[2 * n], refs[2 * n + 1]
        token = refs[2 * n + 2 + 2 * n]
        x, y, c = _coords()
        k = 2 * x + y
        for t in range(n):
            for m in (1, 2, 3):
                px, py, pk = _chip_peer(x, y, m)
                pltpu.make_async_remote_copy(
                    src_ref=ins[t].at[pk], dst_ref=lands[t].at[k], send_sem=send.at[3 * t + m - 1],
                    recv_sem=recv.at[3 * t + m - 1], device_id=(px, py, c), device_id_type=MESH).start()
        token[...] = jnp.zeros_like(token)

    sem = pltpu.SemaphoreType.DMA((3 * n,))
    hbm = [pltpu.HBM(a.shape, a.dtype) for a in srcs]
    operands = list(srcs) + [lax.empty(a.shape, a.dtype) for a in srcs]
    out = pl.pallas_call(
        body, name=name, out_shape=[sem, sem] + hbm + hbm + [jax.ShapeDtypeStruct((8, 128), F32)],
        in_specs=[HBM] * (2 * n), out_specs=[SEM, SEM] + [HBM] * (2 * n) + [pl.BlockSpec(memory_space=pltpu.VMEM)],
        input_output_aliases={t: 2 + t for t in range(2 * n)},
        compiler_params=pltpu.CompilerParams(has_side_effects=EFFECT),
    )(*[pltpu.with_memory_space_constraint(a, pltpu.HBM) for a in operands])
    return out[0], out[1], out[2:2 + n], out[2 + n:2 + 2 * n], out[2 + 2 * n]


def _scatter_wait(srcs, lands, send_sem, recv_sem, after, name):
    n = len(srcs)

    def body(*refs):
        ins, land = refs[:n], refs[n:2 * n]
        send, recv = refs[2 * n], refs[2 * n + 1]
        x, y, c = _coords()
        for t in range(n):
            for m in (1, 2, 3):
                px, py, pk = _chip_peer(x, y, m)
                cp = pltpu.make_async_remote_copy(
                    src_ref=ins[t].at[pk], dst_ref=land[t].at[pk], send_sem=send.at[3 * t + m - 1],
                    recv_sem=recv.at[3 * t + m - 1], device_id=(px, py, c), device_id_type=MESH)
                cp.wait_send()
                cp.wait_recv()

    hbm = [pltpu.HBM(a.shape, a.dtype) for a in srcs]
    out = pl.pallas_call(
        body, name=name, out_shape=hbm + hbm, in_specs=[HBM] * (2 * n) + [SEM, SEM, ANY], out_specs=[HBM] * (2 * n),
        input_output_aliases={t: t for t in range(2 * n)},
        compiler_params=pltpu.CompilerParams(has_side_effects=EFFECT),
    )(*srcs, *lands, send_sem, recv_sem, after)
    return out[:n], out[n:]


def _sibling_send_halves(gs, after, name):
    n = len(gs)

    def body(*refs):
        ins, outs = refs[:n], refs[n + 1:2 * n + 1]
        send_sems, recv_sems = refs[2 * n + 1:]
        x, y, c = _coords()
        copies = []
        for t in range(n):
            half = ins[t].shape[1] // 2
            rows = pl.ds(pl.multiple_of((1 - c) * half, 8), half)
            copies.append(pltpu.make_async_remote_copy(
                src_ref=ins[t].at[:, rows, :], dst_ref=outs[t], send_sem=send_sems.at[t],
                recv_sem=recv_sems.at[t], device_id=(x, y, 1 - c), device_id_type=MESH))
        for cp in copies:
            cp.start()
        for cp in copies:
            cp.wait_recv()
        for cp in copies:
            cp.wait_send()

    return pl.pallas_call(
        body, name=name, in_specs=[ANY] * (n + 1), out_specs=[ANY] * n,
        out_shape=[jax.ShapeDtypeStruct((a.shape[0], a.shape[1] // 2, a.shape[2]), a.dtype) for a in gs],
        scratch_shapes=[pltpu.SemaphoreType.DMA((n,)), pltpu.SemaphoreType.DMA((n,))],
    )(*gs, after)


def _sibling_join_halves(fs, name):
    n = len(fs)

    def body(*refs):
        ins = refs[:n]
        send_sems, recv_sems = refs[2 * n:]
        x, y, c = _coords()
        sends, recvs = [], []
        for t in range(n):
            half = ins[t].shape[0] // 2
            mine = pl.ds(pl.multiple_of(c * half, 8), half)
            theirs = pl.ds(pl.multiple_of((1 - c) * half, 8), half)
            for rows, lst in ((mine, sends), (theirs, recvs)):
                lst.append(pltpu.make_async_remote_copy(
                    src_ref=ins[t].at[rows, :], dst_ref=ins[t].at[rows, :], send_sem=send_sems.at[t],
                    recv_sem=recv_sems.at[t], device_id=(x, y, 1 - c), device_id_type=MESH))
        for cp in sends:
            cp.start()
        for cp in recvs:
            cp.wait_recv()
        for cp in sends:
            cp.wait_send()

    return pl.pallas_call(
        body, name=name, in_specs=[ANY] * n, out_specs=[ANY] * n,
        out_shape=[jax.ShapeDtypeStruct(a.shape, a.dtype) for a in fs],
        input_output_aliases={t: t for t in range(n)},
        scratch_shapes=[pltpu.SemaphoreType.DMA((n,)), pltpu.SemaphoreType.DMA((n,))],
    )(*fs)


def _sum_own_half(g, ra, kc, name):
    _, R, C = g.shape
    half = R // 2
    br = _row_block(half, C)
    nb = half // br

    def body(kc_ref, g_ref, ra_ref, o_ref):
        o_ref[...] = (g_ref[...] + ra_ref[...]).astype(BF16)

    return pl.pallas_call(
        body, name=name,
        grid_spec=pltpu.PrefetchScalarGridSpec(
            num_scalar_prefetch=1, grid=(NCHIP, nb),
            in_specs=[pl.BlockSpec((None, br, C), lambda j, i, kc: (j, kc[1] * nb + i, 0)),
                      pl.BlockSpec((None, br, C), lambda j, i, kc: (j, i, 0))],
            out_specs=pl.BlockSpec((None, br, C), lambda j, i, kc: (j, i, 0))),
        out_shape=jax.ShapeDtypeStruct((NCHIP, half, C), BF16),
        compiler_params=_params("parallel", "parallel"),
    )(kc, g, ra)


def _sum_chips(sa, rb, kc, name):
    _, half, C = rb.shape
    br = _row_block(half, C)
    nb = half // br

    def body(kc_ref, own_ref, r1_ref, r2_ref, r3_ref, o_ref):
        acc = own_ref[...].astype(F32) + r1_ref[...].astype(F32)
        o_ref[...] = (acc + r2_ref[...].astype(F32)) + r3_ref[...].astype(F32)

    def slab(m):
        return pl.BlockSpec((None, br, C), lambda i, kc: (kc[0] ^ m, i, 0))

    return pl.pallas_call(
        body, name=name,
        grid_spec=pltpu.PrefetchScalarGridSpec(
            num_scalar_prefetch=1, grid=(nb,),
            in_specs=[slab(0), slab(1), slab(2), slab(3)],
            out_specs=pl.BlockSpec((br, C), lambda i, kc: (kc[1] * nb + i, 0))),
        out_shape=jax.ShapeDtypeStruct((2 * half, C), F32), compiler_params=_params("parallel"),
    )(kc, sa, rb, rb, rb)


def _sum8(ga, name):
    _, R, C = ga.shape

    def body(g_ref, o_ref):
        acc = g_ref[0]
        for j in range(1, 8):
            acc = acc + g_ref[j]
        o_ref[...] = acc

    return pl.pallas_call(
        body, name=name, in_specs=[pl.BlockSpec((8, R, C), lambda: (0, 0, 0))],
        out_specs=pl.BlockSpec((R, C), lambda: (0, 0)), out_shape=jax.ShapeDtypeStruct((R, C), F32),
    )(ga)


ADA_COLS = 9 * D // NCHIP
ADA_BLK = 256


def _ada_mod(c_all, ada_w, ada_b, kidx):
    def body(k_ref, c_ref, w_ref, b_ref, o_ref):
        cv = c_ref[...]
        cs = cv * _sigmoid(cv)
        o_ref[...] = jnp.dot(cs, w_ref[...], precision=lax.Precision.HIGHEST,
                             preferred_element_type=F32) + b_ref[...]

    nblk = ADA_COLS // ADA_BLK
    return pl.pallas_call(
        body, name="ada_mod",
        grid_spec=pltpu.PrefetchScalarGridSpec(
            num_scalar_prefetch=1, grid=(nblk,),
            in_specs=[pl.BlockSpec((8, D), lambda j, k: (0, 0)),
                      pl.BlockSpec((D, ADA_BLK), lambda j, k: (0, j)),
                      pl.BlockSpec((1, ADA_BLK), lambda j, k: (0, k[0] * nblk + j))],
            out_specs=pl.BlockSpec((8, ADA_BLK), lambda j, k: (0, j))),
        out_shape=jax.ShapeDtypeStruct((8, ADA_COLS), F32),
        compiler_params=_params("parallel"),
    )(kidx, c_all, ada_w, ada_b)


def _ada_grad(c_all_t, dmod_all, kidx):
    def body(k_ref, ct_ref, dm_ref, o_ref):
        cv = ct_ref[...]
        cs = cv * _sigmoid(cv)
        acc = cs[:, 0:1] * dm_ref[0:1, :]
        for b in range(1, 8):
            acc = acc + cs[:, b:b + 1] * dm_ref[b:b + 1, :]
        o_ref[...] = acc

    nblk = ADA_COLS // ADA_BLK
    return pl.pallas_call(
        body, name="ada_grad",
        grid_spec=pltpu.PrefetchScalarGridSpec(
            num_scalar_prefetch=1, grid=(nblk,),
            in_specs=[pl.BlockSpec((D, 8), lambda j, k: (0, 0)),
                      pl.BlockSpec((8, ADA_BLK), lambda j, k: (0, k[0] * nblk + j))],
            out_specs=pl.BlockSpec((D, ADA_BLK), lambda j, k: (0, j))),
        out_shape=jax.ShapeDtypeStruct((D, ADA_COLS), F32),
        compiler_params=_params("parallel"),
    )(kidx, c_all_t, dmod_all)


BIG = ("ffn1_w_in", "ffn1_w_out", "mix_w_in", "hgrn_w_o", "conv_w_o", "mix_w_out", "ffn2_w_in", "ffn2_w_out")
ROW_SHARDED = ("ffn1_w_out", "hgrn_w_o", "conv_w_o", "mix_w_out", "ffn2_w_out")
GATHER_GROUPS = ((0, 1), (2,), (3, 4, 5), (6, 7))
PACK_LEN = {"ada_b": 9, "hgrn_lb": 2}
WEIGHTS = ("ada_w", "ada_b", "norm_ffn1", "ffn1_w_in", "ffn1_w_out", "norm_mix", "mix_w_in", "hgrn_lb", "hgrn_g",
           "hgrn_w_o", "conv_w", "conv_b", "conv_ln_g", "conv_ln_b", "conv_w_o", "mix_w_out", "norm_ffn2",
           "ffn2_w_in", "ffn2_w_out", "norm_final")
PACKED = ("ada_b", "norm_ffn1", "norm_mix", "hgrn_g", "conv_b", "conv_ln_g", "conv_ln_b", "norm_ffn2",
          "norm_final", "hgrn_lb")


def _pack_params(p, name):
    parts = [(p[n].reshape(PACK_LEN.get(n, 1), D), 0, PACK_LEN.get(n, 1), PACK_AT[n]) for n in PACKED]
    return _pack_rows(parts, PACK_ROWS, name)


def _step(w, m, v, x, c, tgt):
    xi, yi, ci = _coords()
    kidx = (2 * xi + yi).astype(jnp.int32).reshape(1)
    kc = jnp.stack([2 * xi + yi, ci]).astype(jnp.int32)
    me = 4 * xi + 2 * yi + ci

    c_all = _allgather8(jnp.broadcast_to(c, (8, D)), "gather_c")[:, 0, :]
    mod_cols = _ada_mod(c_all, w["ada_w"][0], w["ada_b"], kidx)
    mod_all = _allgather8(mod_cols, "gather_mod")
    mod = lax.dynamic_slice(mod_all, (0, me, 0), (8, 1, ADA_COLS))[::2].reshape(9, D)
    small = {n: w[n].reshape(-1, D) for n in ("norm_ffn1", "norm_mix", "hgrn_lb", "hgrn_g", "conv_b", "conv_ln_g",
                                              "conv_ln_b", "norm_ffn2", "norm_final")}
    small["conv_w"] = _allgather_conv_w(w["conv_w"][0])

    lands = [_cast_into_slot(w[n][0], kc, "cast_" + n) for n in BIG]
    sends, recvs, lands, _ = _gather_start(lands, GATHER_GROUPS, small["conv_w"])
    ready = {}

    def weight(name, after):
        t = BIG.index(name)
        if t not in ready:
            gi = [t in grp for grp in GATHER_GROUPS].index(True)
            grp = GATHER_GROUPS[gi]
            outs = _gather_wait([lands[j] for j in grp], sends[gi], recvs[gi], after, "gather_weights_wait%d" % gi)
            ready.update(zip(grp, outs))
        return ready[t].reshape(-1, D) if name in ROW_SHARDED else ready[t]

    grads, delta, new_m, new_v = {}, {}, {}, {}
    flight = []
    landed = []

    def settle(after):
        names, sa, rb, send, recv = flight.pop()
        sa, rb = _scatter_wait(sa, rb, send, recv, after, "rs_chip_wait_" + names[0])
        landed.append((names, sa, rb))

    def reduce(names, gs, after=None):
        gs = [g.reshape(NCHIP, -1, g.shape[-1]) for g in gs]
        ra = _sibling_send_halves(gs, gs[0] if after is None else after, "rs_sibling_halves_" + names[0])
        sa = [_sum_own_half(g, r, kc, "rs_sum_pair_" + n) for g, r, n in zip(gs, ra, names)]
        if flight:
            settle(sa[0])
        send, recv, sa, rb, tok = _scatter_start(sa, "rs_chip_start_" + names[0])
        flight.append((names, sa, rb, send, recv))
        return tok

    def adamw(n):
        shape = w[n].shape
        two = (shape[-2], shape[-1])
        d_, m_, v_ = _adamw(w[n].reshape(two), grads[n], m[n].reshape(two), v[n].reshape(two), "adamw_" + n)
        grads[n], delta[n], new_m[n], new_v[n] = (a.reshape(shape) for a in (grads[n], d_, m_, v_))
        return m_

    def finish():
        names, sa, rb = landed.pop(0)
        fin = [_sum_chips(s, r, kc, "rs_sum_chips_" + n) for s, r, n in zip(sa, rb, names)]
        full = _sibling_join_halves(fin, "rs_join_halves_" + names[0])
        grads.update(zip(names, full))
        return [adamw(n) for n in names][-1]

    smalls = []

    def reduce_small(packed):
        packed_all = _allgather8(packed, "gather_small_grads")
        smalls.extend([packed_all, _sum8(packed_all, "sum_small_grads")])
        return smalls[1]

    dx = _local_step(x[0], tgt[0], mod, small, weight, reduce, reduce_small)
    packed_all, gsum = smalls
    loss = (0.5 / D) * jnp.sum(gsum[PACK_AT["loss"]])
    dmod_all = packed_all[:, 0:9, :].reshape(8, 9 * D)
    grads["ada_w"] = _ada_grad(c_all.T, dmod_all, kidx)
    grads["conv_w"] = lax.dynamic_slice(gsum, (PACK_AT["conv_w"], kidx[0] * (D // NCHIP)), (CONV_K, D // NCHIP))

    adamw("ada_w")
    adamw("conv_w")
    pw, pm, pv = (_pack_params(p, "pack_" + s) for p, s in ((w, "w"), (m, "m"), (v, "v")))
    pd, pnm, pnv = _adamw(pw, gsum, pm, pv, "adamw_small")
    last = pnv
    while landed:
        last = finish()
    settle(last)
    finish()
    for n in PACKED:
        rows = slice(PACK_AT[n], PACK_AT[n] + PACK_LEN.get(n, 1))
        for dst, src in ((grads, gsum), (delta, pd), (new_m, pnm), (new_v, pnv)):
            dst[n] = src[rows].reshape(w[n].shape)

    outs = [loss, dx[None]]
    for d in (grads, delta, new_m, new_v):
        outs += [d[n] for n in WEIGHTS]
    return tuple(outs)


def _allgather_conv_w(cw):
    padded = jnp.pad(cw, ((0, 32 - CONV_K), (0, 0)))
    parts = _allgather8(padded, "gather_conv_w")
    return jnp.concatenate([parts[2 * j] for j in range(NCHIP)], axis=1)


def kernel(x, c, ada_w, ada_b, norm_ffn1, ffn1_w_in, ffn1_w_out, norm_mix, mix_w_in, hgrn_lb, hgrn_g, hgrn_w_o, conv_w, conv_b, conv_ln_g, conv_ln_b, conv_w_o, mix_w_out, norm_ffn2, ffn2_w_in, ffn2_w_out, norm_final, loss_target, m_ada_w, m_ada_b, m_norm_ffn1, m_ffn1_w_in, m_ffn1_w_out, m_norm_mix, m_mix_w_in, m_hgrn_lb, m_hgrn_g, m_hgrn_w_o, m_conv_w, m_conv_b, m_conv_ln_g, m_conv_ln_b, m_conv_w_o, m_mix_w_out, m_norm_ffn2, m_ffn2_w_in, m_ffn2_w_out, m_norm_final, v_ada_w, v_ada_b, v_norm_ffn1, v_ffn1_w_in, v_ffn1_w_out, v_norm_mix, v_mix_w_in, v_hgrn_lb, v_hgrn_g, v_hgrn_w_o, v_conv_w, v_conv_b, v_conv_ln_g, v_conv_ln_b, v_conv_w_o, v_mix_w_out, v_norm_ffn2, v_ffn2_w_in, v_ffn2_w_out, v_norm_final):
    w = dict(ada_w=ada_w, ada_b=ada_b, norm_ffn1=norm_ffn1, ffn1_w_in=ffn1_w_in, ffn1_w_out=ffn1_w_out,
             norm_mix=norm_mix, mix_w_in=mix_w_in, hgrn_lb=hgrn_lb, hgrn_g=hgrn_g, hgrn_w_o=hgrn_w_o, conv_w=conv_w,
             conv_b=conv_b, conv_ln_g=conv_ln_g, conv_ln_b=conv_ln_b, conv_w_o=conv_w_o, mix_w_out=mix_w_out,
             norm_ffn2=norm_ffn2, ffn2_w_in=ffn2_w_in, ffn2_w_out=ffn2_w_out, norm_final=norm_final)
    m = dict(ada_w=m_ada_w, ada_b=m_ada_b, norm_ffn1=m_norm_ffn1, ffn1_w_in=m_ffn1_w_in, ffn1_w_out=m_ffn1_w_out,
             norm_mix=m_norm_mix, mix_w_in=m_mix_w_in, hgrn_lb=m_hgrn_lb, hgrn_g=m_hgrn_g, hgrn_w_o=m_hgrn_w_o,
             conv_w=m_conv_w, conv_b=m_conv_b, conv_ln_g=m_conv_ln_g, conv_ln_b=m_conv_ln_b, conv_w_o=m_conv_w_o,
             mix_w_out=m_mix_w_out, norm_ffn2=m_norm_ffn2, ffn2_w_in=m_ffn2_w_in, ffn2_w_out=m_ffn2_w_out,
             norm_final=m_norm_final)
    v = dict(ada_w=v_ada_w, ada_b=v_ada_b, norm_ffn1=v_norm_ffn1, ffn1_w_in=v_ffn1_w_in, ffn1_w_out=v_ffn1_w_out,
             norm_mix=v_norm_mix, mix_w_in=v_mix_w_in, hgrn_lb=v_hgrn_lb, hgrn_g=v_hgrn_g, hgrn_w_o=v_hgrn_w_o,
             conv_w=v_conv_w, conv_b=v_conv_b, conv_ln_g=v_conv_ln_g, conv_ln_b=v_conv_ln_b, conv_w_o=v_conv_w_o,
             mix_w_out=v_mix_w_out, norm_ffn2=v_norm_ffn2, ffn2_w_in=v_ffn2_w_in, ffn2_w_out=v_ffn2_w_out,
             norm_final=v_norm_final)
    return _step(w, m, v, x, c, loss_target)
```

```python
import functools

import jax
import jax.numpy as jnp
from jax import lax
from jax.experimental import pallas as pl
from jax.experimental.pallas import tpu as pltpu

F32 = jnp.float32
BF16 = jnp.bfloat16

D = 1024
DFF = 2816
NCHIP = 4
FSH = 2 * DFF // NCHIP
HEADS = 8
DK = 128
CHUNK = 64
CONV_K = 31
HALO = 32
EPS = 1e-6
TB = 256
CB = 512
HGRN_UNROLL = 2
VMEM_LIMIT = 56 * 1024 * 1024

ADAM_LR = 0.001
ADAM_B1 = 0.9
ADAM_B2 = 0.999
ADAM_EPS = 1e-08
ADAM_WD = 0.01
ADAM_STEP = 10

MESH = pl.DeviceIdType.MESH
ANY = pl.BlockSpec(memory_space=pl.ANY)


def _params(*sem):
    return pltpu.CompilerParams(dimension_semantics=sem, vmem_limit_bytes=VMEM_LIMIT)


def _sigmoid(x):
    return 1.0 / (1.0 + jnp.exp(-x))


def _dsilu(x, sg):
    return sg * (1.0 + x * (1.0 - sg))


def _nt(a, b):
    return lax.dot_general(a, b, (((1,), (1,)), ((), ())), preferred_element_type=F32)


def _tn(a, b):
    return lax.dot_general(a, b, (((0,), (0,)), ((), ())), preferred_element_type=F32)


def _nn(a, b):
    return jnp.dot(a, b, preferred_element_type=F32)


def _colsum(x):
    return jnp.sum(x, axis=0, keepdims=True)


def _rms_fwd(x, gn, sc, sh):
    r = lax.rsqrt(jnp.mean(x * x, axis=-1, keepdims=True) + EPS)
    n = x * r
    h = (n * gn) * (1.0 + sc) + sh
    return r, n, h


def _rms_bwd(dh, r, n, gn, sc, acc_ref):
    acc_ref[0:1, :] += _colsum(dh)
    acc_ref[1:2, :] += _colsum(dh * (n * gn))
    dng = dh * (1.0 + sc)
    acc_ref[3:4, :] += _colsum(dng * n)
    dn = dng * gn
    return r * (dn - n * jnp.mean(dn * n, axis=-1, keepdims=True))


def _ffn_fwd(x, vec, w_in, w_out, name):
    T = x.shape[0]

    def body(x_ref, vec_ref, win_hbm, wout_hbm, xo_ref, h_ref, a_ref, b_ref, s_ref, f_ref, win, wout):
        @pl.when(pl.program_id(0) == 0)
        def _():
            pltpu.sync_copy(win_hbm, win)
            pltpu.sync_copy(wout_hbm, wout)

        x = x_ref[...]
        sh, sc, gate, gn = vec_ref[0:1, :], vec_ref[1:2, :], vec_ref[2:3, :], vec_ref[3:4, :]
        _, _, h = _rms_fwd(x, gn, sc, sh)
        hb = h.astype(BF16)
        h_ref[...] = hb
        f = jnp.zeros((TB, D), F32)
        for j in range(2):
            cols = slice(j * FSH, (j + 1) * FSH)
            a = _nn(hb, win[j])
            b = _nn(hb, win[2 + j])
            s = (a * _sigmoid(a) * b).astype(BF16)
            a_ref[:, cols] = a.astype(BF16)
            b_ref[:, cols] = b.astype(BF16)
            s_ref[:, cols] = s
            f = f + _nn(s, wout[cols, :])
        xo_ref[...] = x + (0.5 * gate) * f
        f_ref[...] = f.astype(BF16)

    row = lambda w: pl.BlockSpec((TB, w), lambda i: (i, 0))
    return pl.pallas_call(
        body, name=name, grid=(T // TB,),
        in_specs=[row(D), pl.BlockSpec((8, D), lambda i: (0, 0)), ANY, ANY],
        out_specs=[row(D), row(D), row(DFF), row(DFF), row(DFF), row(D)],
        out_shape=[jax.ShapeDtypeStruct((T, D), F32), jax.ShapeDtypeStruct((T, D), BF16),
                   jax.ShapeDtypeStruct((T, DFF), BF16), jax.ShapeDtypeStruct((T, DFF), BF16),
                   jax.ShapeDtypeStruct((T, DFF), BF16), jax.ShapeDtypeStruct((T, D), BF16)],
        scratch_shapes=[pltpu.VMEM((NCHIP, D, FSH), BF16), pltpu.VMEM((DFF, D), BF16)],
        compiler_params=_params("arbitrary"),
    )(x, vec, w_in, w_out)


def _ffn_bwd(dxo, x, vec, a, b, f, w_in, w_out, name):
    T = x.shape[0]

    def body(dxo_ref, x_ref, vec_ref, a_ref, b_ref, f_ref, win_hbm, wout_hbm,
             dx_ref, df_ref, dab_ref, acc_ref, win, wout):
        @pl.when(pl.program_id(0) == 0)
        def _():
            pltpu.sync_copy(win_hbm, win)
            pltpu.sync_copy(wout_hbm, wout)
            acc_ref[...] = jnp.zeros_like(acc_ref)

        dxo = dxo_ref[...]
        x = x_ref[...]
        sh, sc, gate, gn = vec_ref[0:1, :], vec_ref[1:2, :], vec_ref[2:3, :], vec_ref[3:4, :]
        r, n, _ = _rms_fwd(x, gn, sc, sh)
        acc_ref[2:3, :] += _colsum(0.5 * f_ref[...].astype(F32) * dxo)
        dfb = ((0.5 * gate) * dxo).astype(BF16)
        df_ref[...] = dfb
        dh = jnp.zeros((TB, D), F32)
        for j in range(2):
            cols = slice(j * FSH, (j + 1) * FSH)
            ds = _nt(dfb, wout[cols, :])
            av = a_ref[:, cols].astype(F32)
            bv = b_ref[:, cols].astype(F32)
            sg = _sigmoid(av)
            da = (ds * bv * _dsilu(av, sg)).astype(BF16)
            db = (ds * (av * sg)).astype(BF16)
            dab_ref[j] = da
            dab_ref[2 + j] = db
            dh = dh + _nt(da, win[j]) + _nt(db, win[2 + j])
        dx_ref[...] = dxo + _rms_bwd(dh, r, n, gn, sc, acc_ref)

    row = lambda w: pl.BlockSpec((TB, w), lambda i: (i, 0))
    vec8 = pl.BlockSpec((8, D), lambda i: (0, 0))
    return pl.pallas_call(
        body, name=name, grid=(T // TB,),
        in_specs=[row(D), row(D), vec8, row(DFF), row(DFF), row(D), ANY, ANY],
        out_specs=[row(D), pl.BlockSpec((None, TB, D), lambda i: (0, i, 0)),
                   pl.BlockSpec((NCHIP, TB, FSH), lambda i: (0, i, 0)), vec8],
        out_shape=[jax.ShapeDtypeStruct((T, D), F32), jax.ShapeDtypeStruct((1, T, D), BF16),
                   jax.ShapeDtypeStruct((NCHIP, T, FSH), BF16), jax.ShapeDtypeStruct((8, D), F32)],
        scratch_shapes=[pltpu.VMEM((NCHIP, D, FSH), BF16), pltpu.VMEM((DFF, D), BF16)],
        compiler_params=_params("arbitrary"),
    )(dxo, x, vec, a, b, f, w_in, w_out)


def _mm_tn(a, b3, hp, name, into=None, slab=0, slabs=None):
    T, M = a.shape
    P, _, N = b3.shape
    tm = M if M <= 1408 else M // 2
    tk = 512
    slabs = P // hp if slabs is None else slabs

    def body(a_ref, b_ref, *rest):
        o_ref = rest[-1]

        @pl.when(pl.program_id(2) == 0)
        def _():
            o_ref[...] = jnp.zeros_like(o_ref)

        o_ref[...] += _tn(a_ref[...], b_ref[...])

    return pl.pallas_call(
        body, name=name, grid=(P, M // tm, T // tk),
        in_specs=[pl.BlockSpec((tk, tm), lambda p, i, k: (k, i)),
                  pl.BlockSpec((None, tk, N), lambda p, i, k: (p, k, 0))] + ([] if into is None else [ANY]),
        out_specs=pl.BlockSpec((None, tm, N), lambda p, i, k: (slab + p // hp, i, p % hp)),
        out_shape=jax.ShapeDtypeStruct((slabs, M, hp * N), F32),
        input_output_aliases={} if into is None else {2: 0},
        compiler_params=_params("parallel", "parallel", "arbitrary"),
    )(a, b3, *([] if into is None else [into]))


def _mix_proj_fwd(x, vec, w_in):
    T = x.shape[0]

    def body(x_ref, vec_ref, w_hbm, h_ref, qr_ref, g_ref, k_ref, v_ref, og_ref, u_ref, ua_ref, ub_ref,
             sa_ref, sb_ref, w):
        @pl.when(pl.program_id(0) == 0)
        def _():
            pltpu.sync_copy(w_hbm, w)

        x = x_ref[...]
        sh, sc, gn, lb = vec_ref[0:1, :], vec_ref[1:2, :], vec_ref[3:4, :], vec_ref[4:5, :]
        _, _, h = _rms_fwd(x, gn, sc, sh)
        hb = h.astype(BF16)
        h_ref[...] = hb
        p = _nn(hb, w[0])
        qr_ref[...] = p[:, :D].astype(BF16)
        fg = lb + (1.0 - lb) * _sigmoid(p[:, D:])
        g_ref[...] = jnp.log(fg)
        k_ref[...] = (1.0 - fg).astype(BF16)
        p = _nn(hb, w[1])
        v_ref[...] = p[:, :D].astype(BF16)
        og_ref[...] = p[:, D:].astype(BF16)
        p = _nn(hb, w[2])
        ua, ub = p[:, :D], p[:, D:]
        u_ref[...] = ua * _sigmoid(ub)
        ua_ref[...] = ua.astype(BF16)
        ub_ref[...] = ub.astype(BF16)
        p = _nn(hb, w[3])
        sa_ref[...] = _sigmoid(p[:, :D]).astype(BF16)
        sb_ref[...] = _sigmoid(p[:, D:]).astype(BF16)

    row = pl.BlockSpec((TB, D), lambda i: (i, 0))
    bf = jax.ShapeDtypeStruct((T, D), BF16)
    f32 = jax.ShapeDtypeStruct((T, D), F32)
    return pl.pallas_call(
        body, name="mix_proj_fwd", grid=(T // TB,),
        in_specs=[row, pl.BlockSpec((8, D), lambda i: (0, 0)), ANY],
        out_specs=[row] * 11,
        out_shape=[bf, bf, f32, bf, bf, bf, f32, bf, bf, bf, bf],
        scratch_shapes=[pltpu.VMEM((NCHIP, D, 2 * D), BF16)],
        compiler_params=_params("arbitrary"),
    )(x, vec, w_in)


def _mix_proj_bwd(dxo, x, vec, dpa, dpb, dpc, w_in):
    T = x.shape[0]

    def body(dxo_ref, x_ref, vec_ref, dpa_ref, dpb_ref, dpc_ref, w_hbm, dx_ref, acc_ref, w):
        @pl.when(pl.program_id(0) == 0)
        def _():
            pltpu.sync_copy(w_hbm, w)
            acc_ref[...] = jnp.zeros_like(acc_ref)

        x = x_ref[...]
        sh, sc, gn = vec_ref[0:1, :], vec_ref[1:2, :], vec_ref[3:4, :]
        r, n, _ = _rms_fwd(x, gn, sc, sh)
        dh = jnp.zeros((TB, D), F32)
        for p in range(8):
            src = dpa_ref[p] if p < 4 else (dpb_ref[p - 4] if p < 6 else dpc_ref[p - 6])
            dh = dh + _nt(src, w[p // 2, :, (p % 2) * D:(p % 2 + 1) * D])
        dx_ref[...] = dxo_ref[...] + _rms_bwd(dh, r, n, gn, sc, acc_ref)

    row = pl.BlockSpec((TB, D), lambda i: (i, 0))
    vec8 = pl.BlockSpec((8, D), lambda i: (0, 0))
    stack = lambda k: pl.BlockSpec((k, TB, D), lambda i: (0, i, 0))
    return pl.pallas_call(
        body, name="mix_proj_bwd", grid=(T // TB,),
        in_specs=[row, row, vec8, stack(4), stack(2), stack(2), ANY],
        out_specs=[row, vec8],
        out_shape=[jax.ShapeDtypeStruct((T, D), F32), jax.ShapeDtypeStruct((8, D), F32)],
        scratch_shapes=[pltpu.VMEM((NCHIP, D, 2 * D), BF16)],
        compiler_params=_params("arbitrary"),
    )(dxo, x, vec, dpa, dpb, dpc, w_in)


def _tri(lower):
    r = lax.broadcasted_iota(jnp.int32, (CHUNK, CHUNK), 0)
    c = lax.broadcasted_iota(jnp.int32, (CHUNK, CHUNK), 1)
    return (c <= r) if lower else (c >= r)


def _cumsum_rows(mask, g):
    return jnp.dot(mask.astype(F32), g, precision=lax.Precision.HIGHEST, preferred_element_type=F32)


def _hgrn_fwd(qr, g, k, v, og, vec):
    T = qr.shape[0]
    nck = CB // CHUNK

    def body(qr_ref, g_ref, k_ref, v_ref, og_ref, vec_ref, out_ref, o_ref, st_ref, state):
        @pl.when(pl.program_id(1) == 0)
        def _():
            state[...] = jnp.zeros_like(state)

        low = _tri(True)

        def chunk(c, carry):
            rows = pl.ds(pl.multiple_of(c * CHUNK, CHUNK), CHUNK)
            qv = qr_ref[rows, :].astype(F32)
            q = qv * _sigmoid(qv) * (DK ** -0.5)
            kk = k_ref[rows, :].astype(F32)
            vb = v_ref[rows, :]
            gv = g_ref[rows, :]
            b = _cumsum_rows(low, gv)
            mid = _colsum(gv[0:CHUNK // 2])
            last = _colsum(gv)
            st = state[...]
            st_ref[c] = st.astype(BF16)
            qt = (q * jnp.exp(b - mid)).astype(BF16)
            kt = (kk * jnp.exp(mid - b)).astype(BF16)
            att = jnp.where(low, _nt(qt, kt), 0.0).astype(BF16)
            qe = (q * jnp.exp(b)).astype(BF16)
            o_ref[rows, :] = _nn(att, vb) + _nt(qe, st.astype(BF16))
            kd = (kk * jnp.exp(last - b)).astype(BF16)
            state[...] = st * jnp.exp(last) + _tn(vb, kd)
            return carry

        lax.fori_loop(0, nck, chunk, 0, unroll=HGRN_UNROLL)
        o = o_ref[...]
        ogv = og_ref[...].astype(F32)
        rms = lax.rsqrt(jnp.mean(o * o, axis=-1, keepdims=True) + EPS)
        out_ref[...] = (o * rms * vec_ref[5:6, :] * (ogv * _sigmoid(ogv))).astype(BF16)

    blk = pl.BlockSpec((CB, DK), lambda h, i: (i, h))
    return pl.pallas_call(
        body, name="hgrn_fwd", grid=(HEADS, T // CB),
        in_specs=[blk, blk, blk, blk, blk, pl.BlockSpec((8, DK), lambda h, i: (0, h))],
        out_specs=[blk, blk, pl.BlockSpec((None, nck, DK, DK), lambda h, i: (h, i, 0, 0))],
        out_shape=[jax.ShapeDtypeStruct((T, D), BF16), jax.ShapeDtypeStruct((T, D), F32),
                   jax.ShapeDtypeStruct((HEADS, T // CHUNK, DK, DK), BF16)],
        scratch_shapes=[pltpu.VMEM((DK, DK), F32)],
        compiler_params=_params("parallel", "arbitrary"),
    )(qr, g, k, v, og, vec)


def _hgrn_bwd(dout, og, qr, g, k, v, o, st, vec):
    T = qr.shape[0]
    nck = CB // CHUNK
    nb = T // CB

    def body(dout_ref, og_ref, qr_ref, g_ref, k_ref, v_ref, o_ref, st_ref, vec_ref,
             dp_ref, acc_ref, dstate, do_scr):
        @pl.when(pl.program_id(1) == 0)
        def _():
            dstate[...] = jnp.zeros_like(dstate)
            acc_ref[...] = jnp.zeros_like(acc_ref)

        o = o_ref[...]
        ogv = og_ref[...].astype(F32)
        dout = dout_ref[...].astype(F32)
        hg = vec_ref[5:6, :]
        sgo = _sigmoid(ogv)
        rms = lax.rsqrt(jnp.mean(o * o, axis=-1, keepdims=True) + EPS)
        ohat = o * rms
        dp_ref[3] = (dout * (ohat * hg) * _dsilu(ogv, sgo)).astype(BF16)
        don = dout * (ogv * sgo)
        acc_ref[0:1, :] += _colsum(don * ohat)
        dohat = don * hg
        do_scr[...] = rms * (dohat - ohat * jnp.mean(dohat * ohat, axis=-1, keepdims=True))

        low = _tri(True)
        upp = _tri(False)
        lb = vec_ref[4:5, :]

        def chunk(j, dlb_sum):
            c = nck - 1 - j
            rows = pl.ds(pl.multiple_of(c * CHUNK, CHUNK), CHUNK)
            qv = qr_ref[rows, :].astype(F32)
            sgq = _sigmoid(qv)
            q = qv * sgq * (DK ** -0.5)
            kk = k_ref[rows, :].astype(F32)
            vb = v_ref[rows, :]
            gv = g_ref[rows, :]
            do = do_scr[rows, :]
            dob = do.astype(BF16)
            b = _cumsum_rows(low, gv)
            mid = _colsum(gv[0:CHUNK // 2])
            last = _colsum(gv)
            s0 = st_ref[c]
            ds1 = dstate[...]
            ds1b = ds1.astype(BF16)
            eq = jnp.exp(b - mid)
            ek = jnp.exp(mid - b)
            eb = jnp.exp(b)
            ed = jnp.exp(last - b)
            el = jnp.exp(last)
            qt = q * eq
            kt = kk * ek
            qe = q * eb
            kd = kk * ed
            qtb, ktb, qeb, kdb = qt.astype(BF16), kt.astype(BF16), qe.astype(BF16), kd.astype(BF16)
            att = jnp.where(low, _nt(qtb, ktb), 0.0).astype(BF16)
            datt = jnp.where(low, _nt(dob, vb), 0.0).astype(BF16)
            dv = _tn(att, dob) + _nt(kdb, ds1b)
            dqt = _nn(datt, ktb)
            dkt = _tn(datt, qtb)
            dqe = _nn(dob, s0)
            dkd = _nn(vb, ds1b)
            dq = dqt * eq + dqe * eb
            dk = dkt * ek + dkd * ed
            qt, kt, qe, kd = qtb.astype(F32), ktb.astype(F32), qeb.astype(F32), kdb.astype(F32)
            dkdkd = dkd * kd
            dlast = _colsum(dkdkd) + el * _colsum(ds1 * s0.astype(F32))
            dg = _cumsum_rows(upp, dqt * qt - dkt * kt + dqe * qe - dkdkd) + dlast
            dstate[...] = ds1 * el + _tn(dob, qeb)
            fg = jnp.exp(gv)
            dfg = dg / fg - dk
            sig = (fg - lb) / (1.0 - lb)
            dp_ref[0, rows, :] = (dq * (DK ** -0.5) * _dsilu(qv, sgq)).astype(BF16)
            dp_ref[1, rows, :] = (dfg * (1.0 - lb) * sig * (1.0 - sig)).astype(BF16)
            dp_ref[2, rows, :] = dv.astype(BF16)
            return dlb_sum + _colsum(dfg * (1.0 - sig))

        dlb = lax.fori_loop(0, nck, chunk, jnp.zeros((1, DK), F32), unroll=HGRN_UNROLL)
        dlb = dlb * (lb * (1.0 - lb))
        acc_ref[1:2, :] += dlb
        acc_ref[2:3, :] -= dlb

    blk = pl.BlockSpec((CB, DK), lambda h, i: (nb - 1 - i, h))
    return pl.pallas_call(
        body, name="hgrn_bwd", grid=(HEADS, nb),
        in_specs=[blk, blk, blk, blk, blk, blk, blk,
                  pl.BlockSpec((None, nck, DK, DK), lambda h, i: (h, nb - 1 - i, 0, 0)),
                  pl.BlockSpec((8, DK), lambda h, i: (0, h))],
        out_specs=[pl.BlockSpec((4, CB, DK), lambda h, i: (0, nb - 1 - i, h)),
                   pl.BlockSpec((8, DK), lambda h, i: (0, h))],
        out_shape=[jax.ShapeDtypeStruct((4, T, D), BF16), jax.ShapeDtypeStruct((8, D), F32)],
        scratch_shapes=[pltpu.VMEM((DK, DK), F32), pltpu.VMEM((CB, DK), F32)],
        compiler_params=_params("parallel", "arbitrary"),
    )(dout, og, qr, g, k, v, o, st, vec)


def _ln_fwd(uc, lg, lbias):
    mu = jnp.mean(uc, axis=-1, keepdims=True)
    xc = uc - mu
    rstd = lax.rsqrt(jnp.mean(xc * xc, axis=-1, keepdims=True) + EPS)
    z = xc * rstd
    return rstd, z, z * lg + lbias


def _conv_fwd(u, cw, cvec):
    T = u.shape[0]
    per = TB // HALO

    def body(u_ref, halo_ref, cw_ref, cvec_ref, us_ref, uc_ref, pad):
        i = pl.program_id(0)
        pad[0:HALO, :] = jnp.where(i > 0, halo_ref[...], 0.0)
        pad[HALO:, :] = u_ref[...]
        acc = jnp.zeros((TB, D), F32) + cvec_ref[0:1, :]
        for j in range(CONV_K):
            acc = acc + cw_ref[j:j + 1, :] * pad[HALO - (CONV_K - 1) + j:HALO - (CONV_K - 1) + j + TB, :]
        uc_ref[...] = acc
        _, _, ul = _ln_fwd(acc, cvec_ref[1:2, :], cvec_ref[2:3, :])
        us_ref[...] = (ul * _sigmoid(ul)).astype(BF16)

    row = pl.BlockSpec((TB, D), lambda i: (i, 0))
    return pl.pallas_call(
        body, name="conv_fwd", grid=(T // TB,),
        in_specs=[row, pl.BlockSpec((HALO, D), lambda i: (jnp.maximum(i * per - 1, 0), 0)),
                  pl.BlockSpec((32, D), lambda i: (0, 0)), pl.BlockSpec((8, D), lambda i: (0, 0))],
        out_specs=[row, row],
        out_shape=[jax.ShapeDtypeStruct((T, D), BF16), jax.ShapeDtypeStruct((T, D), F32)],
        scratch_shapes=[pltpu.VMEM((TB + HALO, D), F32)],
        compiler_params=_params("parallel"),
    )(u, u, cw, cvec)


def _conv_bwd_ln(dus, uc, cvec):
    T = uc.shape[0]

    def body(dus_ref, uc_ref, cvec_ref, duc_ref, acc_ref):
        @pl.when(pl.program_id(0) == 0)
        def _():
            acc_ref[...] = jnp.zeros_like(acc_ref)

        lg = cvec_ref[1:2, :]
        rstd, z, ul = _ln_fwd(uc_ref[...], lg, cvec_ref[2:3, :])
        dul = dus_ref[...].astype(F32) * _dsilu(ul, _sigmoid(ul))
        acc_ref[1:2, :] += _colsum(dul * z)
        acc_ref[2:3, :] += _colsum(dul)
        dz = dul * lg
        duc = rstd * (dz - jnp.mean(dz, axis=-1, keepdims=True) - z * jnp.mean(dz * z, axis=-1, keepdims=True))
        acc_ref[0:1, :] += _colsum(duc)
        duc_ref[...] = duc

    row = pl.BlockSpec((TB, D), lambda i: (i, 0))
    vec8 = pl.BlockSpec((8, D), lambda i: (0, 0))
    return pl.pallas_call(
        body, name="conv_bwd_ln", grid=(T // TB,),
        in_specs=[row, row, vec8], out_specs=[row, vec8],
        out_shape=[jax.ShapeDtypeStruct((T, D), F32), jax.ShapeDtypeStruct((8, D), F32)],
        compiler_params=_params("arbitrary"),
    )(dus, uc, cvec)


def _conv_bwd_taps(duc, u, ua, ub, cw):
    T = u.shape[0]
    per = TB // HALO
    nblk = T // TB

    def body(duc_ref, dnext_ref, u_ref, uprev_ref, ua_ref, ub_ref, cw_ref, dp_ref, dcw_ref, upad, dpad):
        i = pl.program_id(0)

        @pl.when(i == 0)
        def _():
            dcw_ref[...] = jnp.zeros_like(dcw_ref)

        upad[0:HALO, :] = jnp.where(i > 0, uprev_ref[...], 0.0)
        upad[HALO:, :] = u_ref[...]
        dpad[0:TB, :] = duc_ref[...]
        dpad[TB:, :] = jnp.where(i < nblk - 1, dnext_ref[...], 0.0)
        duc = duc_ref[...]
        du = jnp.zeros((TB, D), F32)
        for j in range(CONV_K):
            lo = HALO - (CONV_K - 1) + j
            dcw_ref[j:j + 1, :] += _colsum(duc * upad[lo:lo + TB, :])
            du = du + cw_ref[j:j + 1, :] * dpad[CONV_K - 1 - j:CONV_K - 1 - j + TB, :]
        ua = ua_ref[...].astype(F32)
        sg = _sigmoid(ub_ref[...].astype(F32))
        dp_ref[0] = (du * sg).astype(BF16)
        dp_ref[1] = (du * ua * sg * (1.0 - sg)).astype(BF16)

    row = pl.BlockSpec((TB, D), lambda i: (i, 0))
    return pl.pallas_call(
        body, name="conv_bwd_taps", grid=(nblk,),
        in_specs=[row, pl.BlockSpec((HALO, D), lambda i: (jnp.minimum((i + 1) * per, T // HALO - 1), 0)),
                  row, pl.BlockSpec((HALO, D), lambda i: (jnp.maximum(i * per - 1, 0), 0)),
                  row, row, pl.BlockSpec((32, D), lambda i: (0, 0))],
        out_specs=[pl.BlockSpec((2, TB, D), lambda i: (0, i, 0)), pl.BlockSpec((32, D), lambda i: (0, 0))],
        out_shape=[jax.ShapeDtypeStruct((2, T, D), BF16), jax.ShapeDtypeStruct((32, D), F32)],
        scratch_shapes=[pltpu.VMEM((TB + HALO, D), F32), pltpu.VMEM((TB + HALO, D), F32)],
        compiler_params=_params("arbitrary"),
    )(duc, duc, u, u, ua, ub, cw)


def _merge_fwd(x, oa, us, sa, sb, vec, w_ho, w_co, w_mo):
    T = x.shape[0]

    def body(x_ref, oa_ref, us_ref, sa_ref, sb_ref, vec_ref, who_hbm, wco_hbm, wmo_hbm,
             xo_ref, ya_ref, yb_ref, mg_ref, mo_ref, who, wco, wmo):
        @pl.when(pl.program_id(0) == 0)
        def _():
            pltpu.sync_copy(who_hbm, who)
            pltpu.sync_copy(wco_hbm, wco)
            pltpu.sync_copy(wmo_hbm, wmo)

        ya = _nn(oa_ref[...], who[...])
        yb = _nn(us_ref[...], wco[...])
        mg = (sa_ref[...].astype(F32) * ya + sb_ref[...].astype(F32) * yb).astype(BF16)
        mo = _nn(mg, wmo[...])
        xo_ref[...] = x_ref[...] + vec_ref[2:3, :] * mo
        ya_ref[...] = ya.astype(BF16)
        yb_ref[...] = yb.astype(BF16)
        mg_ref[...] = mg
        mo_ref[...] = mo.astype(BF16)

    row = pl.BlockSpec((TB, D), lambda i: (i, 0))
    bf = jax.ShapeDtypeStruct((T, D), BF16)
    wv = pltpu.VMEM((D, D), BF16)
    return pl.pallas_call(
        body, name="merge_fwd", grid=(T // TB,),
        in_specs=[row, row, row, row, row, pl.BlockSpec((8, D), lambda i: (0, 0)), ANY, ANY, ANY],
        out_specs=[row] * 5,
        out_shape=[jax.ShapeDtypeStruct((T, D), F32), bf, bf, bf, bf],
        scratch_shapes=[wv, wv, wv],
        compiler_params=_params("arbitrary"),
    )(x, oa, us, sa, sb, vec, w_ho, w_co, w_mo)


def _merge_bwd(dxo, mo, ya, yb, sa, sb, vec, w_ho, w_co, w_mo):
    T = dxo.shape[0]

    def body(dxo_ref, mo_ref, ya_ref, yb_ref, sa_ref, sb_ref, vec_ref, who_hbm, wco_hbm, wmo_hbm,
             dmo_ref, dya_ref, dyb_ref, doa_ref, dus_ref, dp_ref, acc_ref, who, wco, wmo):
        @pl.when(pl.program_id(0) == 0)
        def _():
            pltpu.sync_copy(who_hbm, who)
            pltpu.sync_copy(wco_hbm, wco)
            pltpu.sync_copy(wmo_hbm, wmo)
            acc_ref[...] = jnp.zeros_like(acc_ref)

        dxo = dxo_ref[...]
        acc_ref[2:3, :] += _colsum(mo_ref[...].astype(F32) * dxo)
        dmo = (vec_ref[2:3, :] * dxo).astype(BF16)
        dmo_ref[...] = dmo
        dmg = _nt(dmo, wmo[...])
        sa = sa_ref[...].astype(F32)
        sb = sb_ref[...].astype(F32)
        dya = (sa * dmg).astype(BF16)
        dyb = (sb * dmg).astype(BF16)
        dya_ref[...] = dya
        dyb_ref[...] = dyb
        dp_ref[0] = (dmg * ya_ref[...].astype(F32) * sa * (1.0 - sa)).astype(BF16)
        dp_ref[1] = (dmg * yb_ref[...].astype(F32) * sb * (1.0 - sb)).astype(BF16)
        doa_ref[...] = _nt(dya, who[...]).astype(BF16)
        dus_ref[...] = _nt(dyb, wco[...]).astype(BF16)

    row = pl.BlockSpec((TB, D), lambda i: (i, 0))
    one = pl.BlockSpec((None, TB, D), lambda i: (0, i, 0))
    vec8 = pl.BlockSpec((8, D), lambda i: (0, 0))
    bf = jax.ShapeDtypeStruct((T, D), BF16)
    bf1 = jax.ShapeDtypeStruct((1, T, D), BF16)
    wv = pltpu.VMEM((D, D), BF16)
    return pl.pallas_call(
        body, name="merge_bwd", grid=(T // TB,),
        in_specs=[row, row, row, row, row, row, vec8, ANY, ANY, ANY],
        out_specs=[one, one, one, row, row, pl.BlockSpec((2, TB, D), lambda i: (0, i, 0)), vec8],
        out_shape=[bf1, bf1, bf1, bf, bf, jax.ShapeDtypeStruct((2, T, D), BF16), jax.ShapeDtypeStruct((8, D), F32)],
        scratch_shapes=[wv, wv, wv],
        compiler_params=_params("arbitrary"),
    )(dxo, mo, ya, yb, sa, sb, vec, w_ho, w_co, w_mo)


def _head(x, tgt, gvec):
    T = x.shape[0]

    def body(x_ref, t_ref, g_ref, dx_ref, acc_ref):
        @pl.when(pl.program_id(0) == 0)
        def _():
            acc_ref[...] = jnp.zeros_like(acc_ref)

        x = x_ref[...]
        gf = g_ref[0:1, :]
        r = lax.rsqrt(jnp.mean(x * x, axis=-1, keepdims=True) + EPS)
        n = x * r
        err = n * gf - t_ref[...]
        acc_ref[1:2, :] += _colsum(err * err)
        dy = err * (1.0 / D)
        acc_ref[0:1, :] += _colsum(dy * n)
        dn = dy * gf
        dx_ref[...] = r * (dn - n * jnp.mean(dn * n, axis=-1, keepdims=True))

    row = pl.BlockSpec((TB, D), lambda i: (i, 0))
    vec8 = pl.BlockSpec((8, D), lambda i: (0, 0))
    return pl.pallas_call(
        body, name="loss_head", grid=(T // TB,),
        in_specs=[row, row, vec8], out_specs=[row, vec8],
        out_shape=[jax.ShapeDtypeStruct((T, D), F32), jax.ShapeDtypeStruct((8, D), F32)],
        compiler_params=_params("arbitrary"),
    )(x, tgt, gvec)


def _pack_rows(parts, total, name):
    def body(*refs):
        out = refs[-1]
        out[...] = jnp.zeros_like(out)
        for ref, (_, src, n, dst) in zip(refs[:-1], parts):
            out[dst:dst + n, :] = ref[src:src + n, :]

    arrs = [p[0] for p in parts]
    return pl.pallas_call(
        body, name=name, in_specs=[pl.BlockSpec(a.shape, lambda: (0, 0)) for a in arrs],
        out_specs=pl.BlockSpec((total, D), lambda: (0, 0)),
        out_shape=jax.ShapeDtypeStruct((total, D), F32),
    )(*arrs)


PACK_ROWS = 56
PACK_AT = {"ada_b": 0, "loss": 9, "norm_ffn1": 10, "norm_mix": 11, "hgrn_g": 12, "conv_b": 13, "conv_ln_g": 14,
           "conv_ln_b": 15, "norm_ffn2": 16, "norm_final": 17, "hgrn_lb": 18, "conv_w": 20}


def _local_step(x, tgt, mod, small, weight, reduce, reduce_small):
    lb = jax.nn.sigmoid(small["hgrn_lb"][0:1] - small["hgrn_lb"][1:2])
    vec1 = _pack_rows([(mod, 0, 3, 0), (small["norm_ffn1"], 0, 1, 3)], 8, "pack_vec1")
    vec2 = _pack_rows([(mod, 3, 3, 0), (small["norm_mix"], 0, 1, 3), (lb, 0, 1, 4), (small["hgrn_g"], 0, 1, 5)],
                      8, "pack_vec2")
    vec3 = _pack_rows([(mod, 6, 3, 0), (small["norm_ffn2"], 0, 1, 3)], 8, "pack_vec3")
    cvec = _pack_rows([(small["conv_b"], 0, 1, 0), (small["conv_ln_g"], 0, 1, 1), (small["conv_ln_b"], 0, 1, 2)],
                      8, "pack_cvec")
    cw = small["conv_w"]
    gvec = _pack_rows([(small["norm_final"], 0, 1, 0)], 8, "pack_gvec")

    wg = {n: weight(n, vec1) for n in ("ffn1_w_in", "ffn1_w_out")}
    x1, h1, a1, b1, s1, f1 = _ffn_fwd(x, vec1, wg["ffn1_w_in"], wg["ffn1_w_out"], "ffn1_fwd")
    wg["mix_w_in"] = weight("mix_w_in", x1)
    h2, qr, g, k, v, og, u, ua, ub, sa, sb = _mix_proj_fwd(x1, vec2, wg["mix_w_in"])
    oa, o, st = _hgrn_fwd(qr, g, k, v, og, vec2)
    us, uc = _conv_fwd(u, cw, cvec)
    wg.update({n: weight(n, us) for n in ("hgrn_w_o", "conv_w_o", "mix_w_out")})
    x2, ya, yb, mg, mo = _merge_fwd(x1, oa, us, sa, sb, vec2, wg["hgrn_w_o"], wg["conv_w_o"], wg["mix_w_out"])
    wg.update({n: weight(n, x2) for n in ("ffn2_w_in", "ffn2_w_out")})
    x3, h3, a3, b3, s3, f3 = _ffn_fwd(x2, vec3, wg["ffn2_w_in"], wg["ffn2_w_out"], "ffn2_fwd")

    dx3, acc_head = _head(x3, tgt, gvec)
    dx2, df3, dab3, acc3 = _ffn_bwd(dx3, x2, vec3, a3, b3, f3, wg["ffn2_w_in"], wg["ffn2_w_out"], "ffn2_bwd")
    tok = reduce(("ffn2_w_out", "ffn2_w_in"),
                 [_mm_tn(s3, df3, 1, "ffn2_dwout"), _mm_tn(h3, dab3, 1, "ffn2_dwin")])
    vec2b = vec2 + tok[0:1, 0:1]
    dmo, dya, dyb, doa, dus, dpc, acc_m = _merge_bwd(dx2, mo, ya, yb, sa, sb, vec2b,
                                                     wg["hgrn_w_o"], wg["conv_w_o"], wg["mix_w_out"])
    tok = reduce(("mix_w_out", "hgrn_w_o", "conv_w_o"),
                 [_mm_tn(mg, dmo, 1, "mix_dwout"), _mm_tn(oa, dya, 1, "hgrn_dwo"), _mm_tn(us, dyb, 1, "conv_dwo")])
    vec2c = vec2 + tok[0:1, 0:1]
    duc, acc_c = _conv_bwd_ln(dus, uc, cvec)
    dpb, dcw = _conv_bwd_taps(duc, u, ua, ub, cw)
    dpa, acc_h = _hgrn_bwd(doa, og, qr, g, k, v, o, st, vec2c)
    dx1, acc2 = _mix_proj_bwd(dx2, x1, vec2c, dpa, dpb, dpc, wg["mix_w_in"])
    gmix = _mm_tn(h2, dpa, 2, "mix_dwin_a", slabs=NCHIP)
    gmix = _mm_tn(h2, dpb, 2, "mix_dwin_b", into=gmix, slab=2, slabs=NCHIP)
    gmix = _mm_tn(h2, dpc, 2, "mix_dwin_c", into=gmix, slab=3, slabs=NCHIP)
    tok = reduce(("mix_w_in",), [gmix])
    vec1b = vec1 + tok[0:1, 0:1]
    dx0, df1, dab1, acc1 = _ffn_bwd(dx1, x, vec1b, a1, b1, f1, wg["ffn1_w_in"], wg["ffn1_w_out"], "ffn1_bwd")

    at = PACK_AT
    packed = _pack_rows([
        (acc1, 0, 3, at["ada_b"]), (acc2, 0, 2, at["ada_b"] + 3), (acc_m, 2, 1, at["ada_b"] + 5),
        (acc3, 0, 3, at["ada_b"] + 6), (acc_head, 1, 1, at["loss"]), (acc1, 3, 1, at["norm_ffn1"]),
        (acc2, 3, 1, at["norm_mix"]), (acc_h, 0, 1, at["hgrn_g"]), (acc_c, 0, 3, at["conv_b"]),
        (acc3, 3, 1, at["norm_ffn2"]), (acc_head, 0, 1, at["norm_final"]), (acc_h, 1, 2, at["hgrn_lb"]),
        (dcw, 0, CONV_K, at["conv_w"])], PACK_ROWS, "pack_small_grads")
    done = reduce_small(packed)
    reduce(("ffn1_w_out", "ffn1_w_in"), [_mm_tn(s1, df1, 1, "ffn1_dwout"), _mm_tn(h1, dab1, 1, "ffn1_dwin")], done)
    return dx0


BLOCK_BYTES = 3 * 512 * 1024


def _row_block(rows, cols):
    for br in (512, 352, 256, 176, 128, 64, 32, 16, 8):
        if rows % br == 0 and br * cols * 4 <= BLOCK_BYTES:
            return br
    return rows


def _cast_into_slot(w, kc, name):
    R, C = w.shape
    br = _row_block(R, C)

    def body(kc_ref, w_ref, o_ref):
        o_ref[...] = w_ref[...].astype(BF16)

    return pl.pallas_call(
        body, name=name,
        grid_spec=pltpu.PrefetchScalarGridSpec(
            num_scalar_prefetch=1, grid=(R // br,),
            in_specs=[pl.BlockSpec((br, C), lambda i, kc: (i, 0))],
            out_specs=pl.BlockSpec((None, br, C), lambda i, kc: (kc[0], i, 0))),
        out_shape=jax.ShapeDtypeStruct((NCHIP, R, C), BF16), compiler_params=_params("parallel"),
    )(kc, w)


def _adamw(w, g, m, v, name, after=None):
    R, C = w.shape
    br = _row_block(R, C)
    extra = [] if after is None else [after]

    def body(w_ref, g_ref, m_ref, v_ref, *rest):
        d_ref, nm_ref, nv_ref = rest[-3:]
        gv = g_ref[...]
        nm = ADAM_B1 * m_ref[...] + (1.0 - ADAM_B1) * gv
        nv = ADAM_B2 * v_ref[...] + (1.0 - ADAM_B2) * (gv * gv)
        m_hat = nm / (1.0 - ADAM_B1 ** ADAM_STEP)
        v_hat = nv / (1.0 - ADAM_B2 ** ADAM_STEP)
        d_ref[...] = -ADAM_LR * (m_hat / (jnp.sqrt(v_hat) + ADAM_EPS) + ADAM_WD * w_ref[...])
        nm_ref[...] = nm
        nv_ref[...] = nv

    blk = pl.BlockSpec((br, C), lambda i: (i, 0))
    out = jax.ShapeDtypeStruct((R, C), F32)
    return pl.pallas_call(
        body, name=name, grid=(R // br,), in_specs=[blk] * 4 + [ANY] * len(extra), out_specs=[blk] * 3,
        out_shape=[out, out, out], compiler_params=_params("parallel"),
    )(w, g, m, v, *extra)


def _coords():
    return lax.axis_index("x"), lax.axis_index("y"), lax.axis_index("c")


def _flip(v, bit):
    return 1 - v if bit else v


def _allgather8(v, name):
    R, C = v.shape

    def body(v_ref, out_ref, send_sems, recv_sems, local_sem):
        x, y, c = _coords()
        me = 4 * x + 2 * y + c
        mine = pltpu.make_async_copy(v_ref, out_ref.at[me], local_sem)
        mine.start()

        def copy(m, block):
            peer = (_flip(x, m & 4), _flip(y, m & 2), _flip(c, m & 1))
            return pltpu.make_async_remote_copy(
                src_ref=v_ref, dst_ref=out_ref.at[block], send_sem=send_sems.at[m - 1],
                recv_sem=recv_sems.at[m - 1], device_id=peer, device_id_type=MESH)

        sends = [copy(m, me) for m in range(1, 8)]
        for cp in sends:
            cp.start()
        for m in range(1, 8):
            sender = 4 * _flip(x, m & 4) + 2 * _flip(y, m & 2) + _flip(c, m & 1)
            copy(m, sender).wait_recv()
        for cp in sends:
            cp.wait_send()
        mine.wait()

    vm = pl.BlockSpec(memory_space=pltpu.VMEM)
    return pl.pallas_call(
        body, name=name, in_specs=[vm], out_specs=vm,
        out_shape=jax.ShapeDtypeStruct((8, R, C), F32),
        scratch_shapes=[pltpu.SemaphoreType.DMA((7,)), pltpu.SemaphoreType.DMA((7,)), pltpu.SemaphoreType.DMA],
    )(v)


HBM = pl.BlockSpec(memory_space=pltpu.HBM)
SEM = pl.BlockSpec(memory_space=pltpu.SEMAPHORE)
EFFECT = pltpu.SideEffectType.DATAFLOW_SIDE_EFFECTING


def _chip_peer(x, y, m):
    px, py = _flip(x, m & 2), _flip(y, m & 1)
    return px, py, 2 * px + py


def _gather_start(lands, groups, after):
    n, ng, na = len(lands), len(groups), len(after)

    def body(*refs):
        ins = refs[:n]
        sends, recvs = refs[n + na:n + na + ng], refs[n + na + ng:n + na + 2 * ng]
        token = refs[n + na + 2 * ng + n]
        x, y, c = _coords()
        k = 2 * x + y
        for gi, grp in enumerate(groups):
            for j, t in enumerate(grp):
                for m in (1, 2, 3):
                    px, py, _ = _chip_peer(x, y, m)
                    pltpu.make_async_remote_copy(
                        src_ref=ins[t].at[k], dst_ref=ins[t].at[k], send_sem=sends[gi].at[3 * j + m - 1],
                        recv_sem=recvs[gi].at[3 * j + m - 1], device_id=(px, py, c), device_id_type=MESH).start()
        token[...] = jnp.zeros_like(token)

    sems = [pltpu.SemaphoreType.DMA((3 * len(g),)) for g in groups]
    out = pl.pallas_call(
        body, name="gather_weights_start",
        out_shape=sems + sems + [pltpu.HBM(a.shape, a.dtype) for a in lands] + [jax.ShapeDtypeStruct((8, 128), F32)],
        in_specs=[HBM] * n + [ANY] * na,
        out_specs=[SEM] * (2 * ng) + [HBM] * n + [pl.BlockSpec(memory_space=pltpu.VMEM)],
        input_output_aliases={t: 2 * ng + t for t in range(n)},
        compiler_params=pltpu.CompilerParams(has_side_effects=EFFECT),
    )(*[pltpu.with_memory_space_constraint(a, pltpu.HBM) for a in lands], *after)
    return out[:ng], out[ng:2 * ng], out[2 * ng:2 * ng + n], out[2 * ng + n]


def _gather_wait(lands, send_sem, recv_sem, after, name):
    n = len(lands)

    def body(*refs):
        ins, send, recv = refs[:n], refs[n], refs[n + 1]
        x, y, c = _coords()
        k = 2 * x + y
        for j in range(n):
            for m in (1, 2, 3):
                px, py, pk = _chip_peer(x, y, m)
                cp = pltpu.make_async_remote_copy(
                    src_ref=ins[j].at[k], dst_ref=ins[j].at[pk], send_sem=send.at[3 * j + m - 1],
                    recv_sem=recv.at[3 * j + m - 1], device_id=(px, py, c), device_id_type=MESH)
                cp.wait_send()
                cp.wait_recv()

    return pl.pallas_call(
        body, name=name, out_shape=[pltpu.HBM(a.shape, a.dtype) for a in lands],
        in_specs=[HBM] * n + [SEM, SEM, ANY], out_specs=[HBM] * n,
        input_output_aliases={j: j for j in range(n)},
        compiler_params=pltpu.CompilerParams(has_side_effects=EFFECT),
    )(*lands, send_sem, recv_sem, after)


def _scatter_start(srcs, name):
    n = len(srcs)

    def body(*refs):
        ins, lands = refs[:n], refs[n:2 * n]
        send, recv = refs[2 * n], refs[2 * n + 1]
        token = refs[2 * n + 2 + 2 * n]
        x, y, c = _coords()
        k = 2 * x + y
        for t in range(n):
            for m in (1, 2, 3):
                px, py, pk = _chip_peer(x, y, m)
                pltpu.make_async_remote_copy(
                    src_ref=ins[t].at[pk], dst_ref=lands[t].at[k], send_sem=send.at[3 * t + m - 1],
                    recv_sem=recv.at[3 * t + m - 1], device_id=(px, py, c), device_id_type=MESH).start()
        token[...] = jnp.zeros_like(token)

    sem = pltpu.SemaphoreType.DMA((3 * n,))
    hbm = [pltpu.HBM(a.shape, a.dtype) for a in srcs]
    operands = list(srcs) + [lax.empty(a.shape, a.dtype) for a in srcs]
    out = pl.pallas_call(
        body, name=name, out_shape=[sem, sem] + hbm + hbm + [jax.ShapeDtypeStruct((8, 128), F32)],
        in_specs=[HBM] * (2 * n), out_specs=[SEM, SEM] + [HBM] * (2 * n) + [pl.BlockSpec(memory_space=pltpu.VMEM)],
        input_output_aliases={t: 2 + t for t in range(2 * n)},
        compiler_params=pltpu.CompilerParams(has_side_effects=EFFECT),
    )(*[pltpu.with_memory_space_constraint(a, pltpu.HBM) for a in operands])
    return out[0], out[1], out[2:2 + n], out[2 + n:2 + 2 * n], out[2 + 2 * n]


def _scatter_wait(srcs, lands, send_sem, recv_sem, after, name):
    n = len(srcs)

    def body(*refs):
        ins, land = refs[:n], refs[n:2 * n]
        send, recv = refs[2 * n], refs[2 * n + 1]
        x, y, c = _coords()
        for t in range(n):
            for m in (1, 2, 3):
                px, py, pk = _chip_peer(x, y, m)
                cp = pltpu.make_async_remote_copy(
                    src_ref=ins[t].at[pk], dst_ref=land[t].at[pk], send_sem=send.at[3 * t + m - 1],
                    recv_sem=recv.at[3 * t + m - 1], device_id=(px, py, c), device_id_type=MESH)
                cp.wait_send()
                cp.wait_recv()

    hbm = [pltpu.HBM(a.shape, a.dtype) for a in srcs]
    out = pl.pallas_call(
        body, name=name, out_shape=hbm + hbm, in_specs=[HBM] * (2 * n) + [SEM, SEM, ANY], out_specs=[HBM] * (2 * n),
        input_output_aliases={t: t for t in range(2 * n)},
        compiler_params=pltpu.CompilerParams(has_side_effects=EFFECT),
    )(*srcs, *lands, send_sem, recv_sem, after)
    return out[:n], out[n:]


def _sibling_send_halves(gs, after, name):
    n = len(gs)

    def body(*refs):
        ins, outs = refs[:n], refs[n + 1:2 * n + 1]
        send_sems, recv_sems = refs[2 * n + 1:]
        x, y, c = _coords()
        copies = []
        for t in range(n):
            half = ins[t].shape[1] // 2
            rows = pl.ds(pl.multiple_of((1 - c) * half, 8), half)
            copies.append(pltpu.make_async_remote_copy(
                src_ref=ins[t].at[:, rows, :], dst_ref=outs[t], send_sem=send_sems.at[t],
                recv_sem=recv_sems.at[t], device_id=(x, y, 1 - c), device_id_type=MESH))
        for cp in copies:
            cp.start()
        for cp in copies:
            cp.wait_recv()
        for cp in copies:
            cp.wait_send()

    return pl.pallas_call(
        body, name=name, in_specs=[ANY] * (n + 1), out_specs=[ANY] * n,
        out_shape=[jax.ShapeDtypeStruct((a.shape[0], a.shape[1] // 2, a.shape[2]), a.dtype) for a in gs],
        scratch_shapes=[pltpu.SemaphoreType.DMA((n,)), pltpu.SemaphoreType.DMA((n,))],
    )(*gs, after)


def _sibling_join_halves(fs, name):
    n = len(fs)

    def body(*refs):
        ins = refs[:n]
        send_sems, recv_sems = refs[2 * n:]
        x, y, c = _coords()
        sends, recvs = [], []
        for t in range(n):
            half = ins[t].shape[0] // 2
            mine = pl.ds(pl.multiple_of(c * half, 8), half)
            theirs = pl.ds(pl.multiple_of((1 - c) * half, 8), half)
            for rows, lst in ((mine, sends), (theirs, recvs)):
                lst.append(pltpu.make_async_remote_copy(
                    src_ref=ins[t].at[rows, :], dst_ref=ins[t].at[rows, :], send_sem=send_sems.at[t],
                    recv_sem=recv_sems.at[t], device_id=(x, y, 1 - c), device_id_type=MESH))
        for cp in sends:
            cp.start()
        for cp in recvs:
            cp.wait_recv()
        for cp in sends:
            cp.wait_send()

    return pl.pallas_call(
        body, name=name, in_specs=[ANY] * n, out_specs=[ANY] * n,
        out_shape=[jax.ShapeDtypeStruct(a.shape, a.dtype) for a in fs],
        input_output_aliases={t: t for t in range(n)},
        scratch_shapes=[pltpu.SemaphoreType.DMA((n,)), pltpu.SemaphoreType.DMA((n,))],
    )(*fs)


def _sum_own_half(g, ra, kc, name):
    _, R, C = g.shape
    half = R // 2
    br = _row_block(half, C)
    nb = half // br

    def body(kc_ref, g_ref, ra_ref, o_ref):
        o_ref[...] = (g_ref[...] + ra_ref[...]).astype(BF16)

    return pl.pallas_call(
        body, name=name,
        grid_spec=pltpu.PrefetchScalarGridSpec(
            num_scalar_prefetch=1, grid=(NCHIP, nb),
            in_specs=[pl.BlockSpec((None, br, C), lambda j, i, kc: (j, kc[1] * nb + i, 0)),
                      pl.BlockSpec((None, br, C), lambda j, i, kc: (j, i, 0))],
            out_specs=pl.BlockSpec((None, br, C), lambda j, i, kc: (j, i, 0))),
        out_shape=jax.ShapeDtypeStruct((NCHIP, half, C), BF16),
        compiler_params=_params("parallel", "parallel"),
    )(kc, g, ra)


def _sum_chips(sa, rb, kc, name, after=None):
    _, half, C = rb.shape
    br = _row_block(half, C)
    nb = half // br
    extra = [] if after is None else [after]

    def body(kc_ref, own_ref, r1_ref, r2_ref, r3_ref, *rest):
        acc = own_ref[...].astype(F32) + r1_ref[...].astype(F32)
        rest[-1][...] = (acc + r2_ref[...].astype(F32)) + r3_ref[...].astype(F32)

    def slab(m):
        return pl.BlockSpec((None, br, C), lambda i, kc: (kc[0] ^ m, i, 0))

    return pl.pallas_call(
        body, name=name,
        grid_spec=pltpu.PrefetchScalarGridSpec(
            num_scalar_prefetch=1, grid=(nb,),
            in_specs=[slab(0), slab(1), slab(2), slab(3)] + [ANY] * len(extra),
            out_specs=pl.BlockSpec((br, C), lambda i, kc: (kc[1] * nb + i, 0))),
        out_shape=jax.ShapeDtypeStruct((2 * half, C), F32), compiler_params=_params("parallel"),
    )(kc, sa, rb, rb, rb, *extra)


def _sum8(ga, name):
    _, R, C = ga.shape

    def body(g_ref, o_ref):
        acc = g_ref[0]
        for j in range(1, 8):
            acc = acc + g_ref[j]
        o_ref[...] = acc

    return pl.pallas_call(
        body, name=name, in_specs=[pl.BlockSpec((8, R, C), lambda: (0, 0, 0))],
        out_specs=pl.BlockSpec((R, C), lambda: (0, 0)), out_shape=jax.ShapeDtypeStruct((R, C), F32),
    )(ga)


ADA_COLS = 9 * D // NCHIP
ADA_BLK = 256


def _ada_mod(c_all, ada_w, ada_b, kidx):
    def body(k_ref, c_ref, w_ref, b_ref, o_ref):
        cv = c_ref[...]
        cs = cv * _sigmoid(cv)
        o_ref[...] = jnp.dot(cs, w_ref[...], precision=lax.Precision.HIGHEST,
                             preferred_element_type=F32) + b_ref[...]

    nblk = ADA_COLS // ADA_BLK
    return pl.pallas_call(
        body, name="ada_mod",
        grid_spec=pltpu.PrefetchScalarGridSpec(
            num_scalar_prefetch=1, grid=(nblk,),
            in_specs=[pl.BlockSpec((8, D), lambda j, k: (0, 0)),
                      pl.BlockSpec((D, ADA_BLK), lambda j, k: (0, j)),
                      pl.BlockSpec((1, ADA_BLK), lambda j, k: (0, k[0] * nblk + j))],
            out_specs=pl.BlockSpec((8, ADA_BLK), lambda j, k: (0, j))),
        out_shape=jax.ShapeDtypeStruct((8, ADA_COLS), F32),
        compiler_params=_params("parallel"),
    )(kidx, c_all, ada_w, ada_b)


def _ada_grad(c_all_t, dmod_all, kidx):
    def body(k_ref, ct_ref, dm_ref, o_ref):
        cv = ct_ref[...]
        cs = cv * _sigmoid(cv)
        acc = cs[:, 0:1] * dm_ref[0:1, :]
        for b in range(1, 8):
            acc = acc + cs[:, b:b + 1] * dm_ref[b:b + 1, :]
        o_ref[...] = acc

    nblk = ADA_COLS // ADA_BLK
    return pl.pallas_call(
        body, name="ada_grad",
        grid_spec=pltpu.PrefetchScalarGridSpec(
            num_scalar_prefetch=1, grid=(nblk,),
            in_specs=[pl.BlockSpec((D, 8), lambda j, k: (0, 0)),
                      pl.BlockSpec((8, ADA_BLK), lambda j, k: (0, k[0] * nblk + j))],
            out_specs=pl.BlockSpec((D, ADA_BLK), lambda j, k: (0, j))),
        out_shape=jax.ShapeDtypeStruct((D, ADA_COLS), F32),
        compiler_params=_params("parallel"),
    )(kidx, c_all_t, dmod_all)


BIG = ("ffn1_w_in", "ffn1_w_out", "mix_w_in", "hgrn_w_o", "conv_w_o", "mix_w_out", "ffn2_w_in", "ffn2_w_out")
ROW_SHARDED = ("ffn1_w_out", "hgrn_w_o", "conv_w_o", "mix_w_out", "ffn2_w_out")
GATHER_GROUPS = ((0, 1), (2,), (3, 4, 5), (6, 7))
PACK_LEN = {"ada_b": 9, "hgrn_lb": 2}
WEIGHTS = ("ada_w", "ada_b", "norm_ffn1", "ffn1_w_in", "ffn1_w_out", "norm_mix", "mix_w_in", "hgrn_lb", "hgrn_g",
           "hgrn_w_o", "conv_w", "conv_b", "conv_ln_g", "conv_ln_b", "conv_w_o", "mix_w_out", "norm_ffn2",
           "ffn2_w_in", "ffn2_w_out", "norm_final")
PACKED = ("ada_b", "norm_ffn1", "norm_mix", "hgrn_g", "conv_b", "conv_ln_g", "conv_ln_b", "norm_ffn2",
          "norm_final", "hgrn_lb")


def _pack_params(p, name):
    parts = [(p[n].reshape(PACK_LEN.get(n, 1), D), 0, PACK_LEN.get(n, 1), PACK_AT[n]) for n in PACKED]
    return _pack_rows(parts, PACK_ROWS, name)


def _step(w, m, v, x, c, tgt):
    xi, yi, ci = _coords()
    kidx = (2 * xi + yi).astype(jnp.int32).reshape(1)
    kc = jnp.stack([2 * xi + yi, ci]).astype(jnp.int32)
    me = 4 * xi + 2 * yi + ci

    c_all = _allgather8(jnp.broadcast_to(c, (8, D)), "gather_c")[:, 0, :]
    mod_cols = _ada_mod(c_all, w["ada_w"][0], w["ada_b"], kidx)
    mod_all = _allgather8(mod_cols, "gather_mod")
    mod = lax.dynamic_slice(mod_all, (0, me, 0), (8, 1, ADA_COLS))[::2].reshape(9, D)
    small = {n: w[n].reshape(-1, D) for n in ("norm_ffn1", "norm_mix", "hgrn_lb", "hgrn_g", "conv_b", "conv_ln_g",
                                              "conv_ln_b", "norm_ffn2", "norm_final")}
    small["conv_w"] = _allgather_conv_w(w["conv_w"][0])

    lands = [_cast_into_slot(w[n][0], kc, "cast_" + n) for n in BIG]
    sends, recvs, lands, _ = _gather_start(lands, GATHER_GROUPS, [mod, small["conv_w"]])
    ready = {}

    def weight(name, after):
        t = BIG.index(name)
        if t not in ready:
            gi = [t in grp for grp in GATHER_GROUPS].index(True)
            grp = GATHER_GROUPS[gi]
            outs = _gather_wait([lands[j] for j in grp], sends[gi], recvs[gi], after, "gather_weights_wait%d" % gi)
            ready.update(zip(grp, outs))
        return ready[t].reshape(-1, D) if name in ROW_SHARDED else ready[t]

    grads, delta, new_m, new_v = {}, {}, {}, {}
    flight = []
    landed = []

    def settle(after):
        names, sa, rb, send, recv = flight.pop()
        sa, rb = _scatter_wait(sa, rb, send, recv, after, "rs_chip_wait_" + names[0])
        landed.append((names, sa, rb))

    def reduce(names, gs, after=None):
        gs = [g.reshape(NCHIP, -1, g.shape[-1]) for g in gs]
        ra = _sibling_send_halves(gs, gs[0] if after is None else after, "rs_sibling_halves_" + names[0])
        sa = [_sum_own_half(g, r, kc, "rs_sum_pair_" + n) for g, r, n in zip(gs, ra, names)]
        if flight:
            settle(sa[0])
        send, recv, sa, rb, tok = _scatter_start(sa, "rs_chip_start_" + names[0])
        flight.append((names, sa, rb, send, recv))
        started.append(tok)
        return tok

    def adamw(n, after=None):
        shape = w[n].shape
        two = (shape[-2], shape[-1])
        d_, m_, v_ = _adamw(w[n].reshape(two), grads[n], m[n].reshape(two), v[n].reshape(two), "adamw_" + n, after)
        grads[n], delta[n], new_m[n], new_v[n] = (a.reshape(shape) for a in (grads[n], d_, m_, v_))
        return m_

    def finish(after=None):
        names, sa, rb = landed.pop(0)
        fin = [_sum_chips(s, r, kc, "rs_sum_chips_" + n, after) for s, r, n in zip(sa, rb, names)]
        full = _sibling_join_halves(fin, "rs_join_halves_" + names[0])
        grads.update(zip(names, full))
        return [adamw(n) for n in names][-1]

    started = []

    smalls = []

    def reduce_small(packed):
        packed_all = _allgather8(packed, "gather_small_grads")
        smalls.extend([packed_all, _sum8(packed_all, "sum_small_grads")])
        return smalls[1]

    dx = _local_step(x[0], tgt[0], mod, small, weight, reduce, reduce_small)
    packed_all, gsum = smalls
    loss = (0.5 / D) * jnp.sum(gsum[PACK_AT["loss"]])
    dmod_all = packed_all[:, 0:9, :].reshape(8, 9 * D)
    grads["ada_w"] = _ada_grad(c_all.T, dmod_all, kidx)
    grads["conv_w"] = lax.dynamic_slice(gsum, (PACK_AT["conv_w"], kidx[0] * (D // NCHIP)), (CONV_K, D // NCHIP))

    tok = started[-1]
    adamw("ada_w", tok)
    adamw("conv_w")
    pw, pm, pv = (_pack_params(p, "pack_" + s) for p, s in ((w, "w"), (m, "m"), (v, "v")))
    pd, pnm, pnv = _adamw(pw, gsum, pm, pv, "adamw_small", tok)
    last = pnv
    while landed:
        last = finish(tok)
    settle(last)
    finish()
    for n in PACKED:
        rows = slice(PACK_AT[n], PACK_AT[n] + PACK_LEN.get(n, 1))
        for dst, src in ((grads, gsum), (delta, pd), (new_m, pnm), (new_v, pnv)):
            dst[n] = src[rows].reshape(w[n].shape)

    outs = [loss, dx[None]]
    for d in (grads, delta, new_m, new_v):
        outs += [d[n] for n in WEIGHTS]
    return tuple(outs)


def _allgather_conv_w(cw):
    padded = jnp.pad(cw, ((0, 32 - CONV_K), (0, 0)))
    parts = _allgather8(padded, "gather_conv_w")
    return jnp.concatenate([parts[2 * j] for j in range(NCHIP)], axis=1)


def kernel(x, c, ada_w, ada_b, norm_ffn1, ffn1_w_in, ffn1_w_out, norm_mix, mix_w_in, hgrn_lb, hgrn_g, hgrn_w_o, conv_w, conv_b, conv_ln_g, conv_ln_b, conv_w_o, mix_w_out, norm_ffn2, ffn2_w_in, ffn2_w_out, norm_final, loss_target, m_ada_w, m_ada_b, m_norm_ffn1, m_ffn1_w_in, m_ffn1_w_out, m_norm_mix, m_mix_w_in, m_hgrn_lb, m_hgrn_g, m_hgrn_w_o, m_conv_w, m_conv_b, m_conv_ln_g, m_conv_ln_b, m_conv_w_o, m_mix_w_out, m_norm_ffn2, m_ffn2_w_in, m_ffn2_w_out, m_norm_final, v_ada_w, v_ada_b, v_norm_ffn1, v_ffn1_w_in, v_ffn1_w_out, v_norm_mix, v_mix_w_in, v_hgrn_lb, v_hgrn_g, v_hgrn_w_o, v_conv_w, v_conv_b, v_conv_ln_g, v_conv_ln_b, v_conv_w_o, v_mix_w_out, v_norm_ffn2, v_ffn2_w_in, v_ffn2_w_out, v_norm_final):
    w = dict(ada_w=ada_w, ada_b=ada_b, norm_ffn1=norm_ffn1, ffn1_w_in=ffn1_w_in, ffn1_w_out=ffn1_w_out,
             norm_mix=norm_mix, mix_w_in=mix_w_in, hgrn_lb=hgrn_lb, hgrn_g=hgrn_g, hgrn_w_o=hgrn_w_o, conv_w=conv_w,
             conv_b=conv_b, conv_ln_g=conv_ln_g, conv_ln_b=conv_ln_b, conv_w_o=conv_w_o, mix_w_out=mix_w_out,
             norm_ffn2=norm_ffn2, ffn2_w_in=ffn2_w_in, ffn2_w_out=ffn2_w_out, norm_final=norm_final)
    m = dict(ada_w=m_ada_w, ada_b=m_ada_b, norm_ffn1=m_norm_ffn1, ffn1_w_in=m_ffn1_w_in, ffn1_w_out=m_ffn1_w_out,
             norm_mix=m_norm_mix, mix_w_in=m_mix_w_in, hgrn_lb=m_hgrn_lb, hgrn_g=m_hgrn_g, hgrn_w_o=m_hgrn_w_o,
             conv_w=m_conv_w, conv_b=m_conv_b, conv_ln_g=m_conv_ln_g, conv_ln_b=m_conv_ln_b, conv_w_o=m_conv_w_o,
             mix_w_out=m_mix_w_out, norm_ffn2=m_norm_ffn2, ffn2_w_in=m_ffn2_w_in, ffn2_w_out=m_ffn2_w_out,
             norm_final=m_norm_final)
    v = dict(ada_w=v_ada_w, ada_b=v_ada_b, norm_ffn1=v_norm_ffn1, ffn1_w_in=v_ffn1_w_in, ffn1_w_out=v_ffn1_w_out,
             norm_mix=v_norm_mix, mix_w_in=v_mix_w_in, hgrn_lb=v_hgrn_lb, hgrn_g=v_hgrn_g, hgrn_w_o=v_hgrn_w_o,
             conv_w=v_conv_w, conv_b=v_conv_b, conv_ln_g=v_conv_ln_g, conv_ln_b=v_conv_ln_b, conv_w_o=v_conv_w_o,
             mix_w_out=v_mix_w_out, norm_ffn2=v_norm_ffn2, ffn2_w_in=v_ffn2_w_in, ffn2_w_out=v_ffn2_w_out,
             norm_final=v_norm_final)
    return _step(w, m, v, x, c, loss_target)
```

```python
import functools

import jax
import jax.numpy as jnp
from jax import lax
from jax.experimental import pallas as pl
from jax.experimental.pallas import tpu as pltpu

F32 = jnp.float32
BF16 = jnp.bfloat16

D = 1024
DFF = 2816
NCHIP = 4
FSH = 2 * DFF // NCHIP
HEADS = 8
DK = 128
CHUNK = 64
CONV_K = 31
HALO = 32
EPS = 1e-6
TB = 256
CB = 512
VMEM_LIMIT = 56 * 1024 * 1024

ADAM_LR = 0.001
ADAM_B1 = 0.9
ADAM_B2 = 0.999
ADAM_EPS = 1e-08
ADAM_WD = 0.01
ADAM_STEP = 10

MESH = pl.DeviceIdType.MESH
ANY = pl.BlockSpec(memory_space=pl.ANY)


def _params(*sem):
    return pltpu.CompilerParams(dimension_semantics=sem, vmem_limit_bytes=VMEM_LIMIT)


def _sigmoid(x):
    return 1.0 / (1.0 + jnp.exp(-x))


def _dsilu(x, sg):
    return sg * (1.0 + x * (1.0 - sg))


def _nt(a, b):
    return lax.dot_general(a, b, (((1,), (1,)), ((), ())), preferred_element_type=F32)


def _tn(a, b):
    return lax.dot_general(a, b, (((0,), (0,)), ((), ())), preferred_element_type=F32)


def _nn(a, b):
    return jnp.dot(a, b, preferred_element_type=F32)


def _colsum(x):
    return jnp.sum(x, axis=0, keepdims=True)


def _rms_fwd(x, gn, sc, sh):
    r = lax.rsqrt(jnp.mean(x * x, axis=-1, keepdims=True) + EPS)
    n = x * r
    h = (n * gn) * (1.0 + sc) + sh
    return r, n, h


def _rms_bwd(dh, r, n, gn, sc, acc_ref):
    acc_ref[0:1, :] += _colsum(dh)
    acc_ref[1:2, :] += _colsum(dh * (n * gn))
    dng = dh * (1.0 + sc)
    acc_ref[3:4, :] += _colsum(dng * n)
    dn = dng * gn
    return r * (dn - n * jnp.mean(dn * n, axis=-1, keepdims=True))


def _ffn_fwd(x, vec, w_in, w_out, name):
    T = x.shape[0]

    def body(x_ref, vec_ref, win_hbm, wout_hbm, xo_ref, h_ref, a_ref, b_ref, s_ref, f_ref, win, wout):
        @pl.when(pl.program_id(0) == 0)
        def _():
            pltpu.sync_copy(win_hbm, win)
            pltpu.sync_copy(wout_hbm, wout)

        x = x_ref[...]
        sh, sc, gate, gn = vec_ref[0:1, :], vec_ref[1:2, :], vec_ref[2:3, :], vec_ref[3:4, :]
        _, _, h = _rms_fwd(x, gn, sc, sh)
        hb = h.astype(BF16)
        h_ref[...] = hb
        f = jnp.zeros((TB, D), F32)
        for j in range(2):
            cols = slice(j * FSH, (j + 1) * FSH)
            a = _nn(hb, win[j])
            b = _nn(hb, win[2 + j])
            s = (a * _sigmoid(a) * b).astype(BF16)
            a_ref[:, cols] = a.astype(BF16)
            b_ref[:, cols] = b.astype(BF16)
            s_ref[:, cols] = s
            f = f + _nn(s, wout[cols, :])
        xo_ref[...] = x + (0.5 * gate) * f
        f_ref[...] = f.astype(BF16)

    row = lambda w: pl.BlockSpec((TB, w), lambda i: (i, 0))
    return pl.pallas_call(
        body, name=name, grid=(T // TB,),
        in_specs=[row(D), pl.BlockSpec((8, D), lambda i: (0, 0)), ANY, ANY],
        out_specs=[row(D), row(D), row(DFF), row(DFF), row(DFF), row(D)],
        out_shape=[jax.ShapeDtypeStruct((T, D), F32), jax.ShapeDtypeStruct((T, D), BF16),
                   jax.ShapeDtypeStruct((T, DFF), BF16), jax.ShapeDtypeStruct((T, DFF), BF16),
                   jax.ShapeDtypeStruct((T, DFF), BF16), jax.ShapeDtypeStruct((T, D), BF16)],
        scratch_shapes=[pltpu.VMEM((NCHIP, D, FSH), BF16), pltpu.VMEM((DFF, D), BF16)],
        compiler_params=_params("arbitrary"),
    )(x, vec, w_in, w_out)


def _ffn_bwd(dxo, x, vec, a, b, f, w_in, w_out, name):
    T = x.shape[0]

    def body(dxo_ref, x_ref, vec_ref, a_ref, b_ref, f_ref, win_hbm, wout_hbm,
             dx_ref, df_ref, dab_ref, acc_ref, win, wout):
        @pl.when(pl.program_id(0) == 0)
        def _():
            pltpu.sync_copy(win_hbm, win)
            pltpu.sync_copy(wout_hbm, wout)
            acc_ref[...] = jnp.zeros_like(acc_ref)

        dxo = dxo_ref[...]
        x = x_ref[...]
        sh, sc, gate, gn = vec_ref[0:1, :], vec_ref[1:2, :], vec_ref[2:3, :], vec_ref[3:4, :]
        r, n, _ = _rms_fwd(x, gn, sc, sh)
        acc_ref[2:3, :] += _colsum(0.5 * f_ref[...].astype(F32) * dxo)
        dfb = ((0.5 * gate) * dxo).astype(BF16)
        df_ref[...] = dfb
        dh = jnp.zeros((TB, D), F32)
        for j in range(2):
            cols = slice(j * FSH, (j + 1) * FSH)
            ds = _nt(dfb, wout[cols, :])
            av = a_ref[:, cols].astype(F32)
            bv = b_ref[:, cols].astype(F32)
            sg = _sigmoid(av)
            da = (ds * bv * _dsilu(av, sg)).astype(BF16)
            db = (ds * (av * sg)).astype(BF16)
            dab_ref[j] = da
            dab_ref[2 + j] = db
            dh = dh + _nt(da, win[j]) + _nt(db, win[2 + j])
        dx_ref[...] = dxo + _rms_bwd(dh, r, n, gn, sc, acc_ref)

    row = lambda w: pl.BlockSpec((TB, w), lambda i: (i, 0))
    vec8 = pl.BlockSpec((8, D), lambda i: (0, 0))
    return pl.pallas_call(
        body, name=name, grid=(T // TB,),
        in_specs=[row(D), row(D), vec8, row(DFF), row(DFF), row(D), ANY, ANY],
        out_specs=[row(D), pl.BlockSpec((None, TB, D), lambda i: (0, i, 0)),
                   pl.BlockSpec((NCHIP, TB, FSH), lambda i: (0, i, 0)), vec8],
        out_shape=[jax.ShapeDtypeStruct((T, D), F32), jax.ShapeDtypeStruct((1, T, D), BF16),
                   jax.ShapeDtypeStruct((NCHIP, T, FSH), BF16), jax.ShapeDtypeStruct((8, D), F32)],
        scratch_shapes=[pltpu.VMEM((NCHIP, D, FSH), BF16), pltpu.VMEM((DFF, D), BF16)],
        compiler_params=_params("arbitrary"),
    )(dxo, x, vec, a, b, f, w_in, w_out)


def _mm_tn(a, b3, hp, name, into=None, slab=0, slabs=None):
    T, M = a.shape
    P, _, N = b3.shape
    tm = M if M <= 1408 else M // 2
    tk = 512
    slabs = P // hp if slabs is None else slabs

    def body(a_ref, b_ref, *rest):
        o_ref = rest[-1]

        @pl.when(pl.program_id(2) == 0)
        def _():
            o_ref[...] = jnp.zeros_like(o_ref)

        o_ref[...] += _tn(a_ref[...], b_ref[...])

    return pl.pallas_call(
        body, name=name, grid=(P, M // tm, T // tk),
        in_specs=[pl.BlockSpec((tk, tm), lambda p, i, k: (k, i)),
                  pl.BlockSpec((None, tk, N), lambda p, i, k: (p, k, 0))] + ([] if into is None else [ANY]),
        out_specs=pl.BlockSpec((None, tm, N), lambda p, i, k: (slab + p // hp, i, p % hp)),
        out_shape=jax.ShapeDtypeStruct((slabs, M, hp * N), F32),
        input_output_aliases={} if into is None else {2: 0},
        compiler_params=_params("parallel", "parallel", "arbitrary"),
    )(a, b3, *([] if into is None else [into]))


def _mix_proj_fwd(x, vec, w_in):
    T = x.shape[0]

    def body(x_ref, vec_ref, w_hbm, h_ref, qr_ref, g_ref, k_ref, v_ref, og_ref, u_ref, ua_ref, ub_ref,
             sa_ref, sb_ref, w):
        @pl.when(pl.program_id(0) == 0)
        def _():
            pltpu.sync_copy(w_hbm, w)

        x = x_ref[...]
        sh, sc, gn, lb = vec_ref[0:1, :], vec_ref[1:2, :], vec_ref[3:4, :], vec_ref[4:5, :]
        _, _, h = _rms_fwd(x, gn, sc, sh)
        hb = h.astype(BF16)
        h_ref[...] = hb
        p = _nn(hb, w[0])
        qr_ref[...] = p[:, :D].astype(BF16)
        fg = lb + (1.0 - lb) * _sigmoid(p[:, D:])
        g_ref[...] = jnp.log(fg)
        k_ref[...] = (1.0 - fg).astype(BF16)
        p = _nn(hb, w[1])
        v_ref[...] = p[:, :D].astype(BF16)
        og_ref[...] = p[:, D:].astype(BF16)
        p = _nn(hb, w[2])
        ua, ub = p[:, :D], p[:, D:]
        u_ref[...] = ua * _sigmoid(ub)
        ua_ref[...] = ua.astype(BF16)
        ub_ref[...] = ub.astype(BF16)
        p = _nn(hb, w[3])
        sa_ref[...] = _sigmoid(p[:, :D]).astype(BF16)
        sb_ref[...] = _sigmoid(p[:, D:]).astype(BF16)

    row = pl.BlockSpec((TB, D), lambda i: (i, 0))
    bf = jax.ShapeDtypeStruct((T, D), BF16)
    f32 = jax.ShapeDtypeStruct((T, D), F32)
    return pl.pallas_call(
        body, name="mix_proj_fwd", grid=(T // TB,),
        in_specs=[row, pl.BlockSpec((8, D), lambda i: (0, 0)), ANY],
        out_specs=[row] * 11,
        out_shape=[bf, bf, f32, bf, bf, bf, f32, bf, bf, bf, bf],
        scratch_shapes=[pltpu.VMEM((NCHIP, D, 2 * D), BF16)],
        compiler_params=_params("arbitrary"),
    )(x, vec, w_in)


def _mix_proj_bwd(dxo, x, vec, dpa, dpb, dpc, w_in):
    T = x.shape[0]

    def body(dxo_ref, x_ref, vec_ref, dpa_ref, dpb_ref, dpc_ref, w_hbm, dx_ref, acc_ref, w):
        @pl.when(pl.program_id(0) == 0)
        def _():
            pltpu.sync_copy(w_hbm, w)
            acc_ref[...] = jnp.zeros_like(acc_ref)

        x = x_ref[...]
        sh, sc, gn = vec_ref[0:1, :], vec_ref[1:2, :], vec_ref[3:4, :]
        r, n, _ = _rms_fwd(x, gn, sc, sh)
        dh = jnp.zeros((TB, D), F32)
        for p in range(8):
            src = dpa_ref[p] if p < 4 else (dpb_ref[p - 4] if p < 6 else dpc_ref[p - 6])
            dh = dh + _nt(src, w[p // 2, :, (p % 2) * D:(p % 2 + 1) * D])
        dx_ref[...] = dxo_ref[...] + _rms_bwd(dh, r, n, gn, sc, acc_ref)

    row = pl.BlockSpec((TB, D), lambda i: (i, 0))
    vec8 = pl.BlockSpec((8, D), lambda i: (0, 0))
    stack = lambda k: pl.BlockSpec((k, TB, D), lambda i: (0, i, 0))
    return pl.pallas_call(
        body, name="mix_proj_bwd", grid=(T // TB,),
        in_specs=[row, row, vec8, stack(4), stack(2), stack(2), ANY],
        out_specs=[row, vec8],
        out_shape=[jax.ShapeDtypeStruct((T, D), F32), jax.ShapeDtypeStruct((8, D), F32)],
        scratch_shapes=[pltpu.VMEM((NCHIP, D, 2 * D), BF16)],
        compiler_params=_params("arbitrary"),
    )(dxo, x, vec, dpa, dpb, dpc, w_in)


def _tri(lower):
    r = lax.broadcasted_iota(jnp.int32, (CHUNK, CHUNK), 0)
    c = lax.broadcasted_iota(jnp.int32, (CHUNK, CHUNK), 1)
    return (c <= r) if lower else (c >= r)


def _cumsum_rows(mask, g):
    return jnp.dot(mask.astype(F32), g, precision=lax.Precision.HIGHEST, preferred_element_type=F32)


def _chunk_decay(low, g, nck):
    bs, mids, lasts = [], [], []
    for c in range(nck):
        gc = g[c * CHUNK:(c + 1) * CHUNK]
        bs.append(_cumsum_rows(low, gc))
        mids.append(_colsum(gc[0:CHUNK // 2]))
        lasts.append(_colsum(gc))
    spread = lambda rows: jnp.concatenate([jnp.broadcast_to(r, (CHUNK, DK)) for r in rows], axis=0)
    return jnp.concatenate(bs, axis=0), spread(mids), spread(lasts), lasts


def _hgrn_fwd(qr, g, k, v, og, vec):
    T = qr.shape[0]
    nck = CB // CHUNK

    def body(qr_ref, g_ref, k_ref, v_ref, og_ref, vec_ref, out_ref, o_ref, st_ref, state):
        @pl.when(pl.program_id(1) == 0)
        def _():
            state[...] = jnp.zeros_like(state)

        low = _tri(True)
        qv = qr_ref[...].astype(F32)
        q = qv * _sigmoid(qv) * (DK ** -0.5)
        kk = k_ref[...].astype(F32)
        vb = v_ref[...]
        b, mid, last, lasts = _chunk_decay(low, g_ref[...], nck)
        qt = (q * jnp.exp(b - mid)).astype(BF16)
        kt = (kk * jnp.exp(mid - b)).astype(BF16)
        qe = (q * jnp.exp(b)).astype(BF16)
        kd = (kk * jnp.exp(last - b)).astype(BF16)
        intra, grow = [], []
        for c in range(nck):
            r = slice(c * CHUNK, (c + 1) * CHUNK)
            att = jnp.where(low, _nt(qt[r], kt[r]), 0.0).astype(BF16)
            intra.append(_nn(att, vb[r]))
            grow.append(_tn(vb[r], kd[r]))
        st = state[...]
        inter = []
        for c in range(nck):
            stb = st.astype(BF16)
            st_ref[c] = stb
            inter.append(_nt(qe[c * CHUNK:(c + 1) * CHUNK], stb))
            st = st * jnp.exp(lasts[c]) + grow[c]
        state[...] = st
        o = jnp.concatenate(intra, axis=0) + jnp.concatenate(inter, axis=0)
        o_ref[...] = o
        ogv = og_ref[...].astype(F32)
        rms = lax.rsqrt(jnp.mean(o * o, axis=-1, keepdims=True) + EPS)
        out_ref[...] = (o * rms * vec_ref[5:6, :] * (ogv * _sigmoid(ogv))).astype(BF16)

    blk = pl.BlockSpec((CB, DK), lambda h, i: (i, h))
    return pl.pallas_call(
        body, name="hgrn_fwd", grid=(HEADS, T // CB),
        in_specs=[blk, blk, blk, blk, blk, pl.BlockSpec((8, DK), lambda h, i: (0, h))],
        out_specs=[blk, blk, pl.BlockSpec((None, nck, DK, DK), lambda h, i: (h, i, 0, 0))],
        out_shape=[jax.ShapeDtypeStruct((T, D), BF16), jax.ShapeDtypeStruct((T, D), F32),
                   jax.ShapeDtypeStruct((HEADS, T // CHUNK, DK, DK), BF16)],
        scratch_shapes=[pltpu.VMEM((DK, DK), F32)],
        compiler_params=_params("parallel", "arbitrary"),
    )(qr, g, k, v, og, vec)


def _hgrn_bwd(dout, og, qr, g, k, v, o, st, vec):
    T = qr.shape[0]
    nck = CB // CHUNK
    nb = T // CB

    def body(dout_ref, og_ref, qr_ref, g_ref, k_ref, v_ref, o_ref, st_ref, vec_ref,
             dp_ref, acc_ref, dstate):
        @pl.when(pl.program_id(1) == 0)
        def _():
            dstate[...] = jnp.zeros_like(dstate)
            acc_ref[...] = jnp.zeros_like(acc_ref)

        o = o_ref[...]
        ogv = og_ref[...].astype(F32)
        dout = dout_ref[...].astype(F32)
        hg = vec_ref[5:6, :]
        sgo = _sigmoid(ogv)
        rms = lax.rsqrt(jnp.mean(o * o, axis=-1, keepdims=True) + EPS)
        ohat = o * rms
        dp_ref[3] = (dout * (ohat * hg) * _dsilu(ogv, sgo)).astype(BF16)
        don = dout * (ogv * sgo)
        acc_ref[0:1, :] += _colsum(don * ohat)
        dohat = don * hg
        dob = (rms * (dohat - ohat * jnp.mean(dohat * ohat, axis=-1, keepdims=True))).astype(BF16)

        low = _tri(True)
        upp = _tri(False)
        lb = vec_ref[4:5, :]
        qv = qr_ref[...].astype(F32)
        sgq = _sigmoid(qv)
        q = qv * sgq * (DK ** -0.5)
        kk = k_ref[...].astype(F32)
        vb = v_ref[...]
        gv = g_ref[...]
        b, mid, last, lasts = _chunk_decay(low, gv, nck)
        eq = jnp.exp(b - mid)
        ek = jnp.exp(mid - b)
        eb = jnp.exp(b)
        ed = jnp.exp(last - b)
        qtb, ktb, qeb, kdb = ((t).astype(BF16) for t in (q * eq, kk * ek, q * eb, kk * ed))
        rows = [slice(c * CHUNK, (c + 1) * CHUNK) for c in range(nck)]

        dv1, dqt, dkt, dqe, grow = [], [], [], [], []
        for c, r in enumerate(rows):
            att = jnp.where(low, _nt(qtb[r], ktb[r]), 0.0).astype(BF16)
            datt = jnp.where(low, _nt(dob[r], vb[r]), 0.0).astype(BF16)
            dv1.append(_tn(att, dob[r]))
            dqt.append(_nn(datt, ktb[r]))
            dkt.append(_tn(datt, qtb[r]))
            dqe.append(_nn(dob[r], st_ref[c]))
            grow.append(_tn(dob[r], qeb[r]))
        ds = dstate[...]
        ds1b, dl_state = [None] * nck, [None] * nck
        for c in reversed(range(nck)):
            el = jnp.exp(lasts[c])
            ds1b[c] = ds.astype(BF16)
            dl_state[c] = el * _colsum(ds * st_ref[c].astype(F32))
            ds = ds * el + grow[c]
        dstate[...] = ds
        dkd = jnp.concatenate([_nn(vb[r], ds1b[c]) for c, r in enumerate(rows)], axis=0)
        dv = jnp.concatenate(dv1, axis=0) + jnp.concatenate([_nt(kdb[r], ds1b[c]) for c, r in enumerate(rows)], axis=0)
        dqt, dkt, dqe = (jnp.concatenate(t, axis=0) for t in (dqt, dkt, dqe))
        dq = dqt * eq + dqe * eb
        dk = dkt * ek + dkd * ed
        dkdkd = dkd * kdb.astype(F32)
        db = dqt * qtb.astype(F32) - dkt * ktb.astype(F32) + dqe * qeb.astype(F32) - dkdkd
        dg = jnp.concatenate([_cumsum_rows(upp, db[r]) + (_colsum(dkdkd[r]) + dl_state[c])
                              for c, r in enumerate(rows)], axis=0)
        fg = jnp.exp(gv)
        dfg = dg / fg - dk
        sig = (fg - lb) / (1.0 - lb)
        dp_ref[0] = (dq * (DK ** -0.5) * _dsilu(qv, sgq)).astype(BF16)
        dp_ref[1] = (dfg * (1.0 - lb) * sig * (1.0 - sig)).astype(BF16)
        dp_ref[2] = dv.astype(BF16)
        dlb = _colsum(dfg * (1.0 - sig)) * (lb * (1.0 - lb))
        acc_ref[1:2, :] += dlb
        acc_ref[2:3, :] -= dlb

    blk = pl.BlockSpec((CB, DK), lambda h, i: (nb - 1 - i, h))
    return pl.pallas_call(
        body, name="hgrn_bwd", grid=(HEADS, nb),
        in_specs=[blk, blk, blk, blk, blk, blk, blk,
                  pl.BlockSpec((None, nck, DK, DK), lambda h, i: (h, nb - 1 - i, 0, 0)),
                  pl.BlockSpec((8, DK), lambda h, i: (0, h))],
        out_specs=[pl.BlockSpec((4, CB, DK), lambda h, i: (0, nb - 1 - i, h)),
                   pl.BlockSpec((8, DK), lambda h, i: (0, h))],
        out_shape=[jax.ShapeDtypeStruct((4, T, D), BF16), jax.ShapeDtypeStruct((8, D), F32)],
        scratch_shapes=[pltpu.VMEM((DK, DK), F32)],
        compiler_params=_params("parallel", "arbitrary"),
    )(dout, og, qr, g, k, v, o, st, vec)


def _ln_fwd(uc, lg, lbias):
    mu = jnp.mean(uc, axis=-1, keepdims=True)
    xc = uc - mu
    rstd = lax.rsqrt(jnp.mean(xc * xc, axis=-1, keepdims=True) + EPS)
    z = xc * rstd
    return rstd, z, z * lg + lbias


LANES = 128
SUBLANES = 8
CONV_ROWS = 128


def _lane_tiles():
    return [slice(l * LANES, (l + 1) * LANES) for l in range(D // LANES)]


def _row_shifts(x):
    n = x.shape[0]
    return [x] + [pltpu.roll(x, n - r, axis=0) for r in range(1, SUBLANES)]


def _shifted_rows(shifted, start, rows=TB):
    a, r = divmod(start, SUBLANES)
    return shifted[r][a * SUBLANES:a * SUBLANES + rows]


def _conv_fwd(u, cw, cvec):
    T = u.shape[0]
    per = TB // HALO

    def body(u_ref, halo_ref, cw_ref, cvec_ref, us_ref, uc_ref, pad):
        i = pl.program_id(0)
        pad[0:HALO, :] = jnp.where(i > 0, halo_ref[...], 0.0)
        pad[HALO:, :] = u_ref[...]
        for lanes in _lane_tiles():
            shifted = _row_shifts(pad[:, lanes])
            acc = jnp.broadcast_to(cvec_ref[0:1, lanes], (TB, LANES))
            for j in range(CONV_K):
                acc = acc + cw_ref[j:j + 1, lanes] * _shifted_rows(shifted, HALO - (CONV_K - 1) + j)
            uc_ref[:, lanes] = acc
        _, _, ul = _ln_fwd(uc_ref[...], cvec_ref[1:2, :], cvec_ref[2:3, :])
        us_ref[...] = (ul * _sigmoid(ul)).astype(BF16)

    row = pl.BlockSpec((TB, D), lambda i: (i, 0))
    return pl.pallas_call(
        body, name="conv_fwd", grid=(T // TB,),
        in_specs=[row, pl.BlockSpec((HALO, D), lambda i: (jnp.maximum(i * per - 1, 0), 0)),
                  pl.BlockSpec((32, D), lambda i: (0, 0)), pl.BlockSpec((8, D), lambda i: (0, 0))],
        out_specs=[row, row],
        out_shape=[jax.ShapeDtypeStruct((T, D), BF16), jax.ShapeDtypeStruct((T, D), F32)],
        scratch_shapes=[pltpu.VMEM((TB + HALO, D), F32)],
        compiler_params=_params("parallel"),
    )(u, u, cw, cvec)


def _conv_bwd_ln(dus, uc, cvec):
    T = uc.shape[0]

    def body(dus_ref, uc_ref, cvec_ref, duc_ref, acc_ref):
        @pl.when(pl.program_id(0) == 0)
        def _():
            acc_ref[...] = jnp.zeros_like(acc_ref)

        lg = cvec_ref[1:2, :]
        rstd, z, ul = _ln_fwd(uc_ref[...], lg, cvec_ref[2:3, :])
        dul = dus_ref[...].astype(F32) * _dsilu(ul, _sigmoid(ul))
        acc_ref[1:2, :] += _colsum(dul * z)
        acc_ref[2:3, :] += _colsum(dul)
        dz = dul * lg
        duc = rstd * (dz - jnp.mean(dz, axis=-1, keepdims=True) - z * jnp.mean(dz * z, axis=-1, keepdims=True))
        acc_ref[0:1, :] += _colsum(duc)
        duc_ref[...] = duc

    row = pl.BlockSpec((TB, D), lambda i: (i, 0))
    vec8 = pl.BlockSpec((8, D), lambda i: (0, 0))
    return pl.pallas_call(
        body, name="conv_bwd_ln", grid=(T // TB,),
        in_specs=[row, row, vec8], out_specs=[row, vec8],
        out_shape=[jax.ShapeDtypeStruct((T, D), F32), jax.ShapeDtypeStruct((8, D), F32)],
        compiler_params=_params("arbitrary"),
    )(dus, uc, cvec)


def _conv_bwd_taps(duc, u, ua, ub, cw):
    T = u.shape[0]
    per = TB // HALO
    nblk = T // TB

    def body(duc_ref, dnext_ref, u_ref, uprev_ref, ua_ref, ub_ref, cw_ref, dp_ref, dcw_ref, upad, dpad, dcw):
        i = pl.program_id(0)

        @pl.when(i == 0)
        def _():
            dcw[...] = jnp.zeros_like(dcw)

        upad[0:HALO, :] = jnp.where(i > 0, uprev_ref[...], 0.0)
        upad[HALO:, :] = u_ref[...]
        dpad[0:TB, :] = duc_ref[...]
        dpad[TB:, :] = jnp.where(i < nblk - 1, dnext_ref[...], 0.0)
        for lanes in _lane_tiles():
            ushift = _row_shifts(upad[:, lanes])
            dshift = _row_shifts(dpad[:, lanes])
            for r0 in range(0, TB, CONV_ROWS):
                rows = slice(r0, r0 + CONV_ROWS)
                duc = duc_ref[rows, lanes]
                du = jnp.zeros((CONV_ROWS, LANES), F32)
                for j in range(CONV_K):
                    prod = duc * _shifted_rows(ushift, r0 + HALO - (CONV_K - 1) + j, CONV_ROWS)
                    dcw[j, :, lanes] += jnp.sum(prod.reshape(CONV_ROWS // SUBLANES, SUBLANES, LANES), axis=0)
                    du = du + cw_ref[j:j + 1, lanes] * _shifted_rows(dshift, r0 + CONV_K - 1 - j, CONV_ROWS)
                ua = ua_ref[rows, lanes].astype(F32)
                sg = _sigmoid(ub_ref[rows, lanes].astype(F32))
                dp_ref[0, rows, lanes] = (du * sg).astype(BF16)
                dp_ref[1, rows, lanes] = (du * ua * sg * (1.0 - sg)).astype(BF16)

        @pl.when(i == nblk - 1)
        def _():
            dcw_ref[...] = jnp.sum(dcw[...], axis=1)

    row = pl.BlockSpec((TB, D), lambda i: (i, 0))
    return pl.pallas_call(
        body, name="conv_bwd_taps", grid=(nblk,),
        in_specs=[row, pl.BlockSpec((HALO, D), lambda i: (jnp.minimum((i + 1) * per, T // HALO - 1), 0)),
                  row, pl.BlockSpec((HALO, D), lambda i: (jnp.maximum(i * per - 1, 0), 0)),
                  row, row, pl.BlockSpec((32, D), lambda i: (0, 0))],
        out_specs=[pl.BlockSpec((2, TB, D), lambda i: (0, i, 0)), pl.BlockSpec((32, D), lambda i: (0, 0))],
        out_shape=[jax.ShapeDtypeStruct((2, T, D), BF16), jax.ShapeDtypeStruct((32, D), F32)],
        scratch_shapes=[pltpu.VMEM((TB + HALO, D), F32), pltpu.VMEM((TB + HALO, D), F32),
                        pltpu.VMEM((32, SUBLANES, D), F32)],
        compiler_params=_params("arbitrary"),
    )(duc, duc, u, u, ua, ub, cw)


def _merge_fwd(x, oa, us, sa, sb, vec, w_ho, w_co, w_mo):
    T = x.shape[0]

    def body(x_ref, oa_ref, us_ref, sa_ref, sb_ref, vec_ref, who_hbm, wco_hbm, wmo_hbm,
             xo_ref, ya_ref, yb_ref, mg_ref, mo_ref, who, wco, wmo):
        @pl.when(pl.program_id(0) == 0)
        def _():
            pltpu.sync_copy(who_hbm, who)
            pltpu.sync_copy(wco_hbm, wco)
            pltpu.sync_copy(wmo_hbm, wmo)

        ya = _nn(oa_ref[...], who[...])
        yb = _nn(us_ref[...], wco[...])
        mg = (sa_ref[...].astype(F32) * ya + sb_ref[...].astype(F32) * yb).astype(BF16)
        mo = _nn(mg, wmo[...])
        xo_ref[...] = x_ref[...] + vec_ref[2:3, :] * mo
        ya_ref[...] = ya.astype(BF16)
        yb_ref[...] = yb.astype(BF16)
        mg_ref[...] = mg
        mo_ref[...] = mo.astype(BF16)

    row = pl.BlockSpec((TB, D), lambda i: (i, 0))
    bf = jax.ShapeDtypeStruct((T, D), BF16)
    wv = pltpu.VMEM((D, D), BF16)
    return pl.pallas_call(
        body, name="merge_fwd", grid=(T // TB,),
        in_specs=[row, row, row, row, row, pl.BlockSpec((8, D), lambda i: (0, 0)), ANY, ANY, ANY],
        out_specs=[row] * 5,
        out_shape=[jax.ShapeDtypeStruct((T, D), F32), bf, bf, bf, bf],
        scratch_shapes=[wv, wv, wv],
        compiler_params=_params("arbitrary"),
    )(x, oa, us, sa, sb, vec, w_ho, w_co, w_mo)


def _merge_bwd(dxo, mo, ya, yb, sa, sb, vec, w_ho, w_co, w_mo):
    T = dxo.shape[0]

    def body(dxo_ref, mo_ref, ya_ref, yb_ref, sa_ref, sb_ref, vec_ref, who_hbm, wco_hbm, wmo_hbm,
             dmo_ref, dya_ref, dyb_ref, doa_ref, dus_ref, dp_ref, acc_ref, who, wco, wmo):
        @pl.when(pl.program_id(0) == 0)
        def _():
            pltpu.sync_copy(who_hbm, who)
            pltpu.sync_copy(wco_hbm, wco)
            pltpu.sync_copy(wmo_hbm, wmo)
            acc_ref[...] = jnp.zeros_like(acc_ref)

        dxo = dxo_ref[...]
        acc_ref[2:3, :] += _colsum(mo_ref[...].astype(F32) * dxo)
        dmo = (vec_ref[2:3, :] * dxo).astype(BF16)
        dmo_ref[...] = dmo
        dmg = _nt(dmo, wmo[...])
        sa = sa_ref[...].astype(F32)
        sb = sb_ref[...].astype(F32)
        dya = (sa * dmg).astype(BF16)
        dyb = (sb * dmg).astype(BF16)
        dya_ref[...] = dya
        dyb_ref[...] = dyb
        dp_ref[0] = (dmg * ya_ref[...].astype(F32) * sa * (1.0 - sa)).astype(BF16)
        dp_ref[1] = (dmg * yb_ref[...].astype(F32) * sb * (1.0 - sb)).astype(BF16)
        doa_ref[...] = _nt(dya, who[...]).astype(BF16)
        dus_ref[...] = _nt(dyb, wco[...]).astype(BF16)

    row = pl.BlockSpec((TB, D), lambda i: (i, 0))
    one = pl.BlockSpec((None, TB, D), lambda i: (0, i, 0))
    vec8 = pl.BlockSpec((8, D), lambda i: (0, 0))
    bf = jax.ShapeDtypeStruct((T, D), BF16)
    bf1 = jax.ShapeDtypeStruct((1, T, D), BF16)
    wv = pltpu.VMEM((D, D), BF16)
    return pl.pallas_call(
        body, name="merge_bwd", grid=(T // TB,),
        in_specs=[row, row, row, row, row, row, vec8, ANY, ANY, ANY],
        out_specs=[one, one, one, row, row, pl.BlockSpec((2, TB, D), lambda i: (0, i, 0)), vec8],
        out_shape=[bf1, bf1, bf1, bf, bf, jax.ShapeDtypeStruct((2, T, D), BF16), jax.ShapeDtypeStruct((8, D), F32)],
        scratch_shapes=[wv, wv, wv],
        compiler_params=_params("arbitrary"),
    )(dxo, mo, ya, yb, sa, sb, vec, w_ho, w_co, w_mo)


def _head(x, tgt, gvec):
    T = x.shape[0]

    def body(x_ref, t_ref, g_ref, dx_ref, acc_ref):
        @pl.when(pl.program_id(0) == 0)
        def _():
            acc_ref[...] = jnp.zeros_like(acc_ref)

        x = x_ref[...]
        gf = g_ref[0:1, :]
        r = lax.rsqrt(jnp.mean(x * x, axis=-1, keepdims=True) + EPS)
        n = x * r
        err = n * gf - t_ref[...]
        acc_ref[1:2, :] += _colsum(err * err)
        dy = err * (1.0 / D)
        acc_ref[0:1, :] += _colsum(dy * n)
        dn = dy * gf
        dx_ref[...] = r * (dn - n * jnp.mean(dn * n, axis=-1, keepdims=True))

    row = pl.BlockSpec((TB, D), lambda i: (i, 0))
    vec8 = pl.BlockSpec((8, D), lambda i: (0, 0))
    return pl.pallas_call(
        body, name="loss_head", grid=(T // TB,),
        in_specs=[row, row, vec8], out_specs=[row, vec8],
        out_shape=[jax.ShapeDtypeStruct((T, D), F32), jax.ShapeDtypeStruct((8, D), F32)],
        compiler_params=_params("arbitrary"),
    )(x, tgt, gvec)


def _pack_rows(parts, total, name):
    def body(*refs):
        out = refs[-1]
        out[...] = jnp.zeros_like(out)
        for ref, (_, src, n, dst) in zip(refs[:-1], parts):
            out[dst:dst + n, :] = ref[src:src + n, :]

    arrs = [p[0] for p in parts]
    return pl.pallas_call(
        body, name=name, in_specs=[pl.BlockSpec(a.shape, lambda: (0, 0)) for a in arrs],
        out_specs=pl.BlockSpec((total, D), lambda: (0, 0)),
        out_shape=jax.ShapeDtypeStruct((total, D), F32),
    )(*arrs)


PACK_ROWS = 56
PACK_AT = {"ada_b": 0, "loss": 9, "norm_ffn1": 10, "norm_mix": 11, "hgrn_g": 12, "conv_b": 13, "conv_ln_g": 14,
           "conv_ln_b": 15, "norm_ffn2": 16, "norm_final": 17, "hgrn_lb": 18, "conv_w": 20}


def _local_step(x, tgt, mod, small, weight, reduce, reduce_small):
    lb = jax.nn.sigmoid(small["hgrn_lb"][0:1] - small["hgrn_lb"][1:2])
    vec1 = _pack_rows([(mod, 0, 3, 0), (small["norm_ffn1"], 0, 1, 3)], 8, "pack_vec1")
    vec2 = _pack_rows([(mod, 3, 3, 0), (small["norm_mix"], 0, 1, 3), (lb, 0, 1, 4), (small["hgrn_g"], 0, 1, 5)],
                      8, "pack_vec2")
    vec3 = _pack_rows([(mod, 6, 3, 0), (small["norm_ffn2"], 0, 1, 3)], 8, "pack_vec3")
    cvec = _pack_rows([(small["conv_b"], 0, 1, 0), (small["conv_ln_g"], 0, 1, 1), (small["conv_ln_b"], 0, 1, 2)],
                      8, "pack_cvec")
    cw = small["conv_w"]
    gvec = _pack_rows([(small["norm_final"], 0, 1, 0)], 8, "pack_gvec")

    wg = {n: weight(n, vec1) for n in ("ffn1_w_in", "ffn1_w_out")}
    x1, h1, a1, b1, s1, f1 = _ffn_fwd(x, vec1, wg["ffn1_w_in"], wg["ffn1_w_out"], "ffn1_fwd")
    wg["mix_w_in"] = weight("mix_w_in", x1)
    h2, qr, g, k, v, og, u, ua, ub, sa, sb = _mix_proj_fwd(x1, vec2, wg["mix_w_in"])
    oa, o, st = _hgrn_fwd(qr, g, k, v, og, vec2)
    us, uc = _conv_fwd(u, cw, cvec)
    wg.update({n: weight(n, us) for n in ("hgrn_w_o", "conv_w_o", "mix_w_out")})
    x2, ya, yb, mg, mo = _merge_fwd(x1, oa, us, sa, sb, vec2, wg["hgrn_w_o"], wg["conv_w_o"], wg["mix_w_out"])
    wg.update({n: weight(n, x2) for n in ("ffn2_w_in", "ffn2_w_out")})
    x3, h3, a3, b3, s3, f3 = _ffn_fwd(x2, vec3, wg["ffn2_w_in"], wg["ffn2_w_out"], "ffn2_fwd")

    dx3, acc_head = _head(x3, tgt, gvec)
    dx2, df3, dab3, acc3 = _ffn_bwd(dx3, x2, vec3, a3, b3, f3, wg["ffn2_w_in"], wg["ffn2_w_out"], "ffn2_bwd")
    tok = reduce(("ffn2_w_out", "ffn2_w_in"),
                 [_mm_tn(s3, df3, 1, "ffn2_dwout"), _mm_tn(h3, dab3, 1, "ffn2_dwin")])
    vec2b = vec2 + tok[0:1, 0:1]
    dmo, dya, dyb, doa, dus, dpc, acc_m = _merge_bwd(dx2, mo, ya, yb, sa, sb, vec2b,
                                                     wg["hgrn_w_o"], wg["conv_w_o"], wg["mix_w_out"])
    tok = reduce(("mix_w_out", "hgrn_w_o", "conv_w_o"),
                 [_mm_tn(mg, dmo, 1, "mix_dwout"), _mm_tn(oa, dya, 1, "hgrn_dwo"), _mm_tn(us, dyb, 1, "conv_dwo")])
    vec2c = vec2 + tok[0:1, 0:1]
    duc, acc_c = _conv_bwd_ln(dus, uc, cvec)
    dpb, dcw = _conv_bwd_taps(duc, u, ua, ub, cw)
    dpa, acc_h = _hgrn_bwd(doa, og, qr, g, k, v, o, st, vec2c)
    dx1, acc2 = _mix_proj_bwd(dx2, x1, vec2c, dpa, dpb, dpc, wg["mix_w_in"])
    gmix = _mm_tn(h2, dpa, 2, "mix_dwin_a", slabs=NCHIP)
    gmix = _mm_tn(h2, dpb, 2, "mix_dwin_b", into=gmix, slab=2, slabs=NCHIP)
    gmix = _mm_tn(h2, dpc, 2, "mix_dwin_c", into=gmix, slab=3, slabs=NCHIP)
    tok = reduce(("mix_w_in",), [gmix])
    vec1b = vec1 + tok[0:1, 0:1]
    dx0, df1, dab1, acc1 = _ffn_bwd(dx1, x, vec1b, a1, b1, f1, wg["ffn1_w_in"], wg["ffn1_w_out"], "ffn1_bwd")

    at = PACK_AT
    packed = _pack_rows([
        (acc1, 0, 3, at["ada_b"]), (acc2, 0, 2, at["ada_b"] + 3), (acc_m, 2, 1, at["ada_b"] + 5),
        (acc3, 0, 3, at["ada_b"] + 6), (acc_head, 1, 1, at["loss"]), (acc1, 3, 1, at["norm_ffn1"]),
        (acc2, 3, 1, at["norm_mix"]), (acc_h, 0, 1, at["hgrn_g"]), (acc_c, 0, 3, at["conv_b"]),
        (acc3, 3, 1, at["norm_ffn2"]), (acc_head, 0, 1, at["norm_final"]), (acc_h, 1, 2, at["hgrn_lb"]),
        (dcw, 0, CONV_K, at["conv_w"])], PACK_ROWS, "pack_small_grads")
    done = reduce_small(packed)
    reduce(("ffn1_w_out", "ffn1_w_in"), [_mm_tn(s1, df1, 1, "ffn1_dwout"), _mm_tn(h1, dab1, 1, "ffn1_dwin")], done)
    return dx0


BLOCK_BYTES = 3 * 512 * 1024


def _row_block(rows, cols):
    for br in (512, 352, 256, 176, 128, 64, 32, 16, 8):
        if rows % br == 0 and br * cols * 4 <= BLOCK_BYTES:
            return br
    return rows


def _cast_into_slot(w, kc, name):
    R, C = w.shape
    br = _row_block(R, C)

    def body(kc_ref, w_ref, o_ref):
        o_ref[...] = w_ref[...].astype(BF16)

    return pl.pallas_call(
        body, name=name,
        grid_spec=pltpu.PrefetchScalarGridSpec(
            num_scalar_prefetch=1, grid=(R // br,),
            in_specs=[pl.BlockSpec((br, C), lambda i, kc: (i, 0))],
            out_specs=pl.BlockSpec((None, br, C), lambda i, kc: (kc[0], i, 0))),
        out_shape=jax.ShapeDtypeStruct((NCHIP, R, C), BF16), compiler_params=_params("parallel"),
    )(kc, w)


def _adamw(w, g, m, v, name, after=None):
    R, C = w.shape
    br = _row_block(R, C)
    extra = [] if after is None else [after]

    def body(w_ref, g_ref, m_ref, v_ref, *rest):
        d_ref, nm_ref, nv_ref = rest[-3:]
        gv = g_ref[...]
        nm = ADAM_B1 * m_ref[...] + (1.0 - ADAM_B1) * gv
        nv = ADAM_B2 * v_ref[...] + (1.0 - ADAM_B2) * (gv * gv)
        m_hat = nm / (1.0 - ADAM_B1 ** ADAM_STEP)
        v_hat = nv / (1.0 - ADAM_B2 ** ADAM_STEP)
        d_ref[...] = -ADAM_LR * (m_hat / (jnp.sqrt(v_hat) + ADAM_EPS) + ADAM_WD * w_ref[...])
        nm_ref[...] = nm
        nv_ref[...] = nv

    blk = pl.BlockSpec((br, C), lambda i: (i, 0))
    out = jax.ShapeDtypeStruct((R, C), F32)
    return pl.pallas_call(
        body, name=name, grid=(R // br,), in_specs=[blk] * 4 + [ANY] * len(extra), out_specs=[blk] * 3,
        out_shape=[out, out, out], compiler_params=_params("parallel"),
    )(w, g, m, v, *extra)


def _coords():
    return lax.axis_index("x"), lax.axis_index("y"), lax.axis_index("c")


def _flip(v, bit):
    return 1 - v if bit else v


def _allgather8(v, name):
    R, C = v.shape

    def body(v_ref, out_ref, send_sems, recv_sems, local_sem):
        x, y, c = _coords()
        me = 4 * x + 2 * y + c
        mine = pltpu.make_async_copy(v_ref, out_ref.at[me], local_sem)
        mine.start()

        def copy(m, block):
            peer = (_flip(x, m & 4), _flip(y, m & 2), _flip(c, m & 1))
            return pltpu.make_async_remote_copy(
                src_ref=v_ref, dst_ref=out_ref.at[block], send_sem=send_sems.at[m - 1],
                recv_sem=recv_sems.at[m - 1], device_id=peer, device_id_type=MESH)

        sends = [copy(m, me) for m in range(1, 8)]
        for cp in sends:
            cp.start()
        for m in range(1, 8):
            sender = 4 * _flip(x, m & 4) + 2 * _flip(y, m & 2) + _flip(c, m & 1)
            copy(m, sender).wait_recv()
        for cp in sends:
            cp.wait_send()
        mine.wait()

    vm = pl.BlockSpec(memory_space=pltpu.VMEM)
    return pl.pallas_call(
        body, name=name, in_specs=[vm], out_specs=vm,
        out_shape=jax.ShapeDtypeStruct((8, R, C), F32),
        scratch_shapes=[pltpu.SemaphoreType.DMA((7,)), pltpu.SemaphoreType.DMA((7,)), pltpu.SemaphoreType.DMA],
    )(v)


HBM = pl.BlockSpec(memory_space=pltpu.HBM)
SEM = pl.BlockSpec(memory_space=pltpu.SEMAPHORE)
EFFECT = pltpu.SideEffectType.DATAFLOW_SIDE_EFFECTING


def _chip_peer(x, y, m):
    px, py = _flip(x, m & 2), _flip(y, m & 1)
    return px, py, 2 * px + py


def _gather_start(lands, groups, after):
    n, ng, na = len(lands), len(groups), len(after)

    def body(*refs):
        ins = refs[:n]
        sends, recvs = refs[n + na:n + na + ng], refs[n + na + ng:n + na + 2 * ng]
        token = refs[n + na + 2 * ng + n]
        x, y, c = _coords()
        k = 2 * x + y
        for gi, grp in enumerate(groups):
            for j, t in enumerate(grp):
                for m in (1, 2, 3):
                    px, py, _ = _chip_peer(x, y, m)
                    pltpu.make_async_remote_copy(
                        src_ref=ins[t].at[k], dst_ref=ins[t].at[k], send_sem=sends[gi].at[3 * j + m - 1],
                        recv_sem=recvs[gi].at[3 * j + m - 1], device_id=(px, py, c), device_id_type=MESH).start()
        token[...] = jnp.zeros_like(token)

    sems = [pltpu.SemaphoreType.DMA((3 * len(g),)) for g in groups]
    out = pl.pallas_call(
        body, name="gather_weights_start",
        out_shape=sems + sems + [pltpu.HBM(a.shape, a.dtype) for a in lands] + [jax.ShapeDtypeStruct((8, 128), F32)],
        in_specs=[HBM] * n + [ANY] * na,
        out_specs=[SEM] * (2 * ng) + [HBM] * n + [pl.BlockSpec(memory_space=pltpu.VMEM)],
        input_output_aliases={t: 2 * ng + t for t in range(n)},
        compiler_params=pltpu.CompilerParams(has_side_effects=EFFECT),
    )(*[pltpu.with_memory_space_constraint(a, pltpu.HBM) for a in lands], *after)
    return out[:ng], out[ng:2 * ng], out[2 * ng:2 * ng + n], out[2 * ng + n]


def _gather_wait(lands, send_sem, recv_sem, after, name):
    n = len(lands)

    def body(*refs):
        ins, send, recv = refs[:n], refs[n], refs[n + 1]
        x, y, c = _coords()
        k = 2 * x + y
        for j in range(n):
            for m in (1, 2, 3):
                px, py, pk = _chip_peer(x, y, m)
                cp = pltpu.make_async_remote_copy(
                    src_ref=ins[j].at[k], dst_ref=ins[j].at[pk], send_sem=send.at[3 * j + m - 1],
                    recv_sem=recv.at[3 * j + m - 1], device_id=(px, py, c), device_id_type=MESH)
                cp.wait_send()
                cp.wait_recv()

    return pl.pallas_call(
        body, name=name, out_shape=[pltpu.HBM(a.shape, a.dtype) for a in lands],
        in_specs=[HBM] * n + [SEM, SEM, ANY], out_specs=[HBM] * n,
        input_output_aliases={j: j for j in range(n)},
        compiler_params=pltpu.CompilerParams(has_side_effects=EFFECT),
    )(*lands, send_sem, recv_sem, after)


def _scatter_start(srcs, name):
    n = len(srcs)

    def body(*refs):
        ins, lands = refs[:n], refs[n:2 * n]
        send, recv = refs[2 * n], refs[2 * n + 1]
        token = refs[2 * n + 2 + 2 * n]
        x, y, c = _coords()
        k = 2 * x + y
        for t in range(n):
            for m in (1, 2, 3):
                px, py, pk = _chip_peer(x, y, m)
                pltpu.make_async_remote_copy(
                    src_ref=ins[t].at[pk], dst_ref=lands[t].at[k], send_sem=send.at[3 * t + m - 1],
                    recv_sem=recv.at[3 * t + m - 1], device_id=(px, py, c), device_id_type=MESH).start()
        token[...] = jnp.zeros_like(token)

    sem = pltpu.SemaphoreType.DMA((3 * n,))
    hbm = [pltpu.HBM(a.shape, a.dtype) for a in srcs]
    operands = list(srcs) + [lax.empty(a.shape, a.dtype) for a in srcs]
    out = pl.pallas_call(
        body, name=name, out_shape=[sem, sem] + hbm + hbm + [jax.ShapeDtypeStruct((8, 128), F32)],
        in_specs=[HBM] * (2 * n), out_specs=[SEM, SEM] + [HBM] * (2 * n) + [pl.BlockSpec(memory_space=pltpu.VMEM)],
        input_output_aliases={t: 2 + t for t in range(2 * n)},
        compiler_params=pltpu.CompilerParams(has_side_effects=EFFECT),
    )(*[pltpu.with_memory_space_constraint(a, pltpu.HBM) for a in operands])
    return out[0], out[1], out[2:2 + n], out[2 + n:2 + 2 * n], out[2 + 2 * n]


def _scatter_wait(srcs, lands, send_sem, recv_sem, after, name):
    n = len(srcs)

    def body(*refs):
        ins, land = refs[:n], refs[n:2 * n]
        send, recv = refs[2 * n], refs[2 * n + 1]
        x, y, c = _coords()
        for t in range(n):
            for m in (1, 2, 3):
                px, py, pk = _chip_peer(x, y, m)
                cp = pltpu.make_async_remote_copy(
                    src_ref=ins[t].at[pk], dst_ref=land[t].at[pk], send_sem=send.at[3 * t + m - 1],
                    recv_sem=recv.at[3 * t + m - 1], device_id=(px, py, c), device_id_type=MESH)
                cp.wait_send()
                cp.wait_recv()

    hbm = [pltpu.HBM(a.shape, a.dtype) for a in srcs]
    out = pl.pallas_call(
        body, name=name, out_shape=hbm + hbm, in_specs=[HBM] * (2 * n) + [SEM, SEM, ANY], out_specs=[HBM] * (2 * n),
        input_output_aliases={t: t for t in range(2 * n)},
        compiler_params=pltpu.CompilerParams(has_side_effects=EFFECT),
    )(*srcs, *lands, send_sem, recv_sem, after)
    return out[:n], out[n:]


def _sibling_send_halves(gs, after, name):
    n = len(gs)

    def body(*refs):
        ins, outs = refs[:n], refs[n + 1:2 * n + 1]
        send_sems, recv_sems = refs[2 * n + 1:]
        x, y, c = _coords()
        copies = []
        for t in range(n):
            half = ins[t].shape[1] // 2
            rows = pl.ds(pl.multiple_of((1 - c) * half, 8), half)
            copies.append(pltpu.make_async_remote_copy(
                src_ref=ins[t].at[:, rows, :], dst_ref=outs[t], send_sem=send_sems.at[t],
                recv_sem=recv_sems.at[t], device_id=(x, y, 1 - c), device_id_type=MESH))
        for cp in copies:
            cp.start()
        for cp in copies:
            cp.wait_recv()
        for cp in copies:
            cp.wait_send()

    return pl.pallas_call(
        body, name=name, in_specs=[ANY] * (n + 1), out_specs=[ANY] * n,
        out_shape=[jax.ShapeDtypeStruct((a.shape[0], a.shape[1] // 2, a.shape[2]), a.dtype) for a in gs],
        scratch_shapes=[pltpu.SemaphoreType.DMA((n,)), pltpu.SemaphoreType.DMA((n,))],
    )(*gs, after)


def _sibling_join_halves(fs, name):
    n = len(fs)

    def body(*refs):
        ins = refs[:n]
        send_sems, recv_sems = refs[2 * n:]
        x, y, c = _coords()
        sends, recvs = [], []
        for t in range(n):
            half = ins[t].shape[0] // 2
            mine = pl.ds(pl.multiple_of(c * half, 8), half)
            theirs = pl.ds(pl.multiple_of((1 - c) * half, 8), half)
            for rows, lst in ((mine, sends), (theirs, recvs)):
                lst.append(pltpu.make_async_remote_copy(
                    src_ref=ins[t].at[rows, :], dst_ref=ins[t].at[rows, :], send_sem=send_sems.at[t],
                    recv_sem=recv_sems.at[t], device_id=(x, y, 1 - c), device_id_type=MESH))
        for cp in sends:
            cp.start()
        for cp in recvs:
            cp.wait_recv()
        for cp in sends:
            cp.wait_send()

    return pl.pallas_call(
        body, name=name, in_specs=[ANY] * n, out_specs=[ANY] * n,
        out_shape=[jax.ShapeDtypeStruct(a.shape, a.dtype) for a in fs],
        input_output_aliases={t: t for t in range(n)},
        scratch_shapes=[pltpu.SemaphoreType.DMA((n,)), pltpu.SemaphoreType.DMA((n,))],
    )(*fs)


def _sum_own_half(g, ra, kc, name):
    _, R, C = g.shape
    half = R // 2
    br = _row_block(half, C)
    nb = half // br

    def body(kc_ref, g_ref, ra_ref, o_ref):
        o_ref[...] = (g_ref[...] + ra_ref[...]).astype(BF16)

    return pl.pallas_call(
        body, name=name,
        grid_spec=pltpu.PrefetchScalarGridSpec(
            num_scalar_prefetch=1, grid=(NCHIP, nb),
            in_specs=[pl.BlockSpec((None, br, C), lambda j, i, kc: (j, kc[1] * nb + i, 0)),
                      pl.BlockSpec((None, br, C), lambda j, i, kc: (j, i, 0))],
            out_specs=pl.BlockSpec((None, br, C), lambda j, i, kc: (j, i, 0))),
        out_shape=jax.ShapeDtypeStruct((NCHIP, half, C), BF16),
        compiler_params=_params("parallel", "parallel"),
    )(kc, g, ra)


def _sum_chips(sa, rb, kc, name, after=None):
    _, half, C = rb.shape
    br = _row_block(half, C)
    nb = half // br
    extra = [] if after is None else [after]

    def body(kc_ref, own_ref, r1_ref, r2_ref, r3_ref, *rest):
        acc = own_ref[...].astype(F32) + r1_ref[...].astype(F32)
        rest[-1][...] = (acc + r2_ref[...].astype(F32)) + r3_ref[...].astype(F32)

    def slab(m):
        return pl.BlockSpec((None, br, C), lambda i, kc: (kc[0] ^ m, i, 0))

    return pl.pallas_call(
        body, name=name,
        grid_spec=pltpu.PrefetchScalarGridSpec(
            num_scalar_prefetch=1, grid=(nb,),
            in_specs=[slab(0), slab(1), slab(2), slab(3)] + [ANY] * len(extra),
            out_specs=pl.BlockSpec((br, C), lambda i, kc: (kc[1] * nb + i, 0))),
        out_shape=jax.ShapeDtypeStruct((2 * half, C), F32), compiler_params=_params("parallel"),
    )(kc, sa, rb, rb, rb, *extra)


def _sum8(ga, name):
    _, R, C = ga.shape

    def body(g_ref, o_ref):
        acc = g_ref[0]
        for j in range(1, 8):
            acc = acc + g_ref[j]
        o_ref[...] = acc

    return pl.pallas_call(
        body, name=name, in_specs=[pl.BlockSpec((8, R, C), lambda: (0, 0, 0))],
        out_specs=pl.BlockSpec((R, C), lambda: (0, 0)), out_shape=jax.ShapeDtypeStruct((R, C), F32),
    )(ga)


ADA_COLS = 9 * D // NCHIP
ADA_BLK = 256


def _ada_mod(c_all, ada_w, ada_b, kidx):
    def body(k_ref, c_ref, w_ref, b_ref, o_ref):
        cv = c_ref[...]
        cs = cv * _sigmoid(cv)
        o_ref[...] = jnp.dot(cs, w_ref[...], precision=lax.Precision.HIGHEST,
                             preferred_element_type=F32) + b_ref[...]

    nblk = ADA_COLS // ADA_BLK
    return pl.pallas_call(
        body, name="ada_mod",
        grid_spec=pltpu.PrefetchScalarGridSpec(
            num_scalar_prefetch=1, grid=(nblk,),
            in_specs=[pl.BlockSpec((8, D), lambda j, k: (0, 0)),
                      pl.BlockSpec((D, ADA_BLK), lambda j, k: (0, j)),
                      pl.BlockSpec((1, ADA_BLK), lambda j, k: (0, k[0] * nblk + j))],
            out_specs=pl.BlockSpec((8, ADA_BLK), lambda j, k: (0, j))),
        out_shape=jax.ShapeDtypeStruct((8, ADA_COLS), F32),
        compiler_params=_params("parallel"),
    )(kidx, c_all, ada_w, ada_b)


def _ada_grad(c_all_t, dmod_all, kidx):
    def body(k_ref, ct_ref, dm_ref, o_ref):
        cv = ct_ref[...]
        cs = cv * _sigmoid(cv)
        acc = cs[:, 0:1] * dm_ref[0:1, :]
        for b in range(1, 8):
            acc = acc + cs[:, b:b + 1] * dm_ref[b:b + 1, :]
        o_ref[...] = acc

    nblk = ADA_COLS // ADA_BLK
    return pl.pallas_call(
        body, name="ada_grad",
        grid_spec=pltpu.PrefetchScalarGridSpec(
            num_scalar_prefetch=1, grid=(nblk,),
            in_specs=[pl.BlockSpec((D, 8), lambda j, k: (0, 0)),
                      pl.BlockSpec((8, ADA_BLK), lambda j, k: (0, k[0] * nblk + j))],
            out_specs=pl.BlockSpec((D, ADA_BLK), lambda j, k: (0, j))),
        out_shape=jax.ShapeDtypeStruct((D, ADA_COLS), F32),
        compiler_params=_params("parallel"),
    )(kidx, c_all_t, dmod_all)


BIG = ("ffn1_w_in", "ffn1_w_out", "mix_w_in", "hgrn_w_o", "conv_w_o", "mix_w_out", "ffn2_w_in", "ffn2_w_out")
ROW_SHARDED = ("ffn1_w_out", "hgrn_w_o", "conv_w_o", "mix_w_out", "ffn2_w_out")
GATHER_GROUPS = ((0, 1), (2,), (3, 4, 5), (6, 7))
PACK_LEN = {"ada_b": 9, "hgrn_lb": 2}
WEIGHTS = ("ada_w", "ada_b", "norm_ffn1", "ffn1_w_in", "ffn1_w_out", "norm_mix", "mix_w_in", "hgrn_lb", "hgrn_g",
           "hgrn_w_o", "conv_w", "conv_b", "conv_ln_g", "conv_ln_b", "conv_w_o", "mix_w_out", "norm_ffn2",
           "ffn2_w_in", "ffn2_w_out", "norm_final")
PACKED = ("ada_b", "norm_ffn1", "norm_mix", "hgrn_g", "conv_b", "conv_ln_g", "conv_ln_b", "norm_ffn2",
          "norm_final", "hgrn_lb")


def _pack_params(p, name):
    parts = [(p[n].reshape(PACK_LEN.get(n, 1), D), 0, PACK_LEN.get(n, 1), PACK_AT[n]) for n in PACKED]
    return _pack_rows(parts, PACK_ROWS, name)


def _step(w, m, v, x, c, tgt):
    xi, yi, ci = _coords()
    kidx = (2 * xi + yi).astype(jnp.int32).reshape(1)
    kc = jnp.stack([2 * xi + yi, ci]).astype(jnp.int32)
    me = 4 * xi + 2 * yi + ci

    c_all = _allgather8(jnp.broadcast_to(c, (8, D)), "gather_c")[:, 0, :]
    mod_cols = _ada_mod(c_all, w["ada_w"][0], w["ada_b"], kidx)
    mod_all = _allgather8(mod_cols, "gather_mod")
    mod = lax.dynamic_slice(mod_all, (0, me, 0), (8, 1, ADA_COLS))[::2].reshape(9, D)
    small = {n: w[n].reshape(-1, D) for n in ("norm_ffn1", "norm_mix", "hgrn_lb", "hgrn_g", "conv_b", "conv_ln_g",
                                              "conv_ln_b", "norm_ffn2", "norm_final")}
    small["conv_w"] = _allgather_conv_w(w["conv_w"][0])

    lands = [_cast_into_slot(w[n][0], kc, "cast_" + n) for n in BIG]
    sends, recvs, lands, _ = _gather_start(lands, GATHER_GROUPS, [mod, small["conv_w"]])
    ready = {}

    def weight(name, after):
        t = BIG.index(name)
        if t not in ready:
            gi = [t in grp for grp in GATHER_GROUPS].index(True)
            grp = GATHER_GROUPS[gi]
            outs = _gather_wait([lands[j] for j in grp], sends[gi], recvs[gi], after, "gather_weights_wait%d" % gi)
            ready.update(zip(grp, outs))
        return ready[t].reshape(-1, D) if name in ROW_SHARDED else ready[t]

    grads, delta, new_m, new_v = {}, {}, {}, {}
    flight = []
    landed = []

    def settle(after):
        names, sa, rb, send, recv = flight.pop()
        sa, rb = _scatter_wait(sa, rb, send, recv, after, "rs_chip_wait_" + names[0])
        landed.append((names, sa, rb))

    def reduce(names, gs, after=None):
        gs = [g.reshape(NCHIP, -1, g.shape[-1]) for g in gs]
        ra = _sibling_send_halves(gs, gs[0] if after is None else after, "rs_sibling_halves_" + names[0])
        sa = [_sum_own_half(g, r, kc, "rs_sum_pair_" + n) for g, r, n in zip(gs, ra, names)]
        if flight:
            settle(sa[0])
        send, recv, sa, rb, tok = _scatter_start(sa, "rs_chip_start_" + names[0])
        flight.append((names, sa, rb, send, recv))
        started.append(tok)
        return tok

    def adamw(n, after=None):
        shape = w[n].shape
        two = (shape[-2], shape[-1])
        d_, m_, v_ = _adamw(w[n].reshape(two), grads[n], m[n].reshape(two), v[n].reshape(two), "adamw_" + n, after)
        grads[n], delta[n], new_m[n], new_v[n] = (a.reshape(shape) for a in (grads[n], d_, m_, v_))
        return m_

    def finish(after=None):
        names, sa, rb = landed.pop(0)
        fin = [_sum_chips(s, r, kc, "rs_sum_chips_" + n, after) for s, r, n in zip(sa, rb, names)]
        full = _sibling_join_halves(fin, "rs_join_halves_" + names[0])
        grads.update(zip(names, full))
        return [adamw(n) for n in names][-1]

    started = []

    smalls = []

    def reduce_small(packed):
        packed_all = _allgather8(packed, "gather_small_grads")
        smalls.extend([packed_all, _sum8(packed_all, "sum_small_grads")])
        return smalls[1]

    dx = _local_step(x[0], tgt[0], mod, small, weight, reduce, reduce_small)
    packed_all, gsum = smalls
    loss = (0.5 / D) * jnp.sum(gsum[PACK_AT["loss"]])
    dmod_all = packed_all[:, 0:9, :].reshape(8, 9 * D)
    grads["ada_w"] = _ada_grad(c_all.T, dmod_all, kidx)
    grads["conv_w"] = lax.dynamic_slice(gsum, (PACK_AT["conv_w"], kidx[0] * (D // NCHIP)), (CONV_K, D // NCHIP))

    tok = started[-1]
    adamw("ada_w", tok)
    adamw("conv_w")
    pw, pm, pv = (_pack_params(p, "pack_" + s) for p, s in ((w, "w"), (m, "m"), (v, "v")))
    pd, pnm, pnv = _adamw(pw, gsum, pm, pv, "adamw_small", tok)
    last = pnv
    while landed:
        last = finish(tok)
    settle(last)
    finish()
    for n in PACKED:
        rows = slice(PACK_AT[n], PACK_AT[n] + PACK_LEN.get(n, 1))
        for dst, src in ((grads, gsum), (delta, pd), (new_m, pnm), (new_v, pnv)):
            dst[n] = src[rows].reshape(w[n].shape)

    outs = [loss, dx[None]]
    for d in (grads, delta, new_m, new_v):
        outs += [d[n] for n in WEIGHTS]
    return tuple(outs)


def _allgather_conv_w(cw):
    padded = jnp.pad(cw, ((0, 32 - CONV_K), (0, 0)))
    parts = _allgather8(padded, "gather_conv_w")
    return jnp.concatenate([parts[2 * j] for j in range(NCHIP)], axis=1)


def kernel(x, c, ada_w, ada_b, norm_ffn1, ffn1_w_in, ffn1_w_out, norm_mix, mix_w_in, hgrn_lb, hgrn_g, hgrn_w_o, conv_w, conv_b, conv_ln_g, conv_ln_b, conv_w_o, mix_w_out, norm_ffn2, ffn2_w_in, ffn2_w_out, norm_final, loss_target, m_ada_w, m_ada_b, m_norm_ffn1, m_ffn1_w_in, m_ffn1_w_out, m_norm_mix, m_mix_w_in, m_hgrn_lb, m_hgrn_g, m_hgrn_w_o, m_conv_w, m_conv_b, m_conv_ln_g, m_conv_ln_b, m_conv_w_o, m_mix_w_out, m_norm_ffn2, m_ffn2_w_in, m_ffn2_w_out, m_norm_final, v_ada_w, v_ada_b, v_norm_ffn1, v_ffn1_w_in, v_ffn1_w_out, v_norm_mix, v_mix_w_in, v_hgrn_lb, v_hgrn_g, v_hgrn_w_o, v_conv_w, v_conv_b, v_conv_ln_g, v_conv_ln_b, v_conv_w_o, v_mix_w_out, v_norm_ffn2, v_ffn2_w_in, v_ffn2_w_out, v_norm_final):
    w = dict(ada_w=ada_w, ada_b=ada_b, norm_ffn1=norm_ffn1, ffn1_w_in=ffn1_w_in, ffn1_w_out=ffn1_w_out,
             norm_mix=norm_mix, mix_w_in=mix_w_in, hgrn_lb=hgrn_lb, hgrn_g=hgrn_g, hgrn_w_o=hgrn_w_o, conv_w=conv_w,
             conv_b=conv_b, conv_ln_g=conv_ln_g, conv_ln_b=conv_ln_b, conv_w_o=conv_w_o, mix_w_out=mix_w_out,
             norm_ffn2=norm_ffn2, ffn2_w_in=ffn2_w_in, ffn2_w_out=ffn2_w_out, norm_final=norm_final)
    m = dict(ada_w=m_ada_w, ada_b=m_ada_b, norm_ffn1=m_norm_ffn1, ffn1_w_in=m_ffn1_w_in, ffn1_w_out=m_ffn1_w_out,
             norm_mix=m_norm_mix, mix_w_in=m_mix_w_in, hgrn_lb=m_hgrn_lb, hgrn_g=m_hgrn_g, hgrn_w_o=m_hgrn_w_o,
             conv_w=m_conv_w, conv_b=m_conv_b, conv_ln_g=m_conv_ln_g, conv_ln_b=m_conv_ln_b, conv_w_o=m_conv_w_o,
             mix_w_out=m_mix_w_out, norm_ffn2=m_norm_ffn2, ffn2_w_in=m_ffn2_w_in, ffn2_w_out=m_ffn2_w_out,
             norm_final=m_norm_final)
    v = dict(ada_w=v_ada_w, ada_b=v_ada_b, norm_ffn1=v_norm_ffn1, ffn1_w_in=v_ffn1_w_in, ffn1_w_out=v_ffn1_w_out,
             norm_mix=v_norm_mix, mix_w_in=v_mix_w_in, hgrn_lb=v_hgrn_lb, hgrn_g=v_hgrn_g, hgrn_w_o=v_hgrn_w_o,
             conv_w=v_conv_w, conv_b=v_conv_b, conv_ln_g=v_conv_ln_g, conv_ln_b=v_conv_ln_b, conv_w_o=v_conv_w_o,
             mix_w_out=v_mix_w_out, norm_ffn2=v_norm_ffn2, ffn2_w_in=v_ffn2_w_in, ffn2_w_out=v_ffn2_w_out,
             norm_final=v_norm_final)
    return _step(w, m, v, x, c, loss_target)
```

```python
import functools

import jax
import jax.numpy as jnp
from jax import lax
from jax.experimental import pallas as pl
from jax.experimental.pallas import tpu as pltpu

F32 = jnp.float32
BF16 = jnp.bfloat16

D = 1024
DFF = 2816
NCHIP = 4
FSH = 2 * DFF // NCHIP
HEADS = 8
DK = 128
CHUNK = 64
CONV_K = 31
HALO = 32
EPS = 1e-6
TB = 256
CB = 512
VMEM_LIMIT = 56 * 1024 * 1024

ADAM_LR = 0.001
ADAM_B1 = 0.9
ADAM_B2 = 0.999
ADAM_EPS = 1e-08
ADAM_WD = 0.01
ADAM_STEP = 10

MESH = pl.DeviceIdType.MESH
ANY = pl.BlockSpec(memory_space=pl.ANY)


def _params(*sem):
    return pltpu.CompilerParams(dimension_semantics=sem, vmem_limit_bytes=VMEM_LIMIT)


def _sigmoid(x):
    return 1.0 / (1.0 + jnp.exp(-x))


def _dsilu(x, sg):
    return sg * (1.0 + x * (1.0 - sg))


def _nt(a, b):
    return lax.dot_general(a, b, (((1,), (1,)), ((), ())), preferred_element_type=F32)


def _tn(a, b):
    return lax.dot_general(a, b, (((0,), (0,)), ((), ())), preferred_element_type=F32)


def _nn(a, b):
    return jnp.dot(a, b, preferred_element_type=F32)


def _colsum(x):
    return jnp.sum(x, axis=0, keepdims=True)


def _rms_fwd(x, gn, sc, sh):
    r = lax.rsqrt(jnp.mean(x * x, axis=-1, keepdims=True) + EPS)
    n = x * r
    h = (n * gn) * (1.0 + sc) + sh
    return r, n, h


def _rms_bwd(dh, r, n, gn, sc, acc_ref):
    acc_ref[0:1, :] += _colsum(dh)
    acc_ref[1:2, :] += _colsum(dh * (n * gn))
    dng = dh * (1.0 + sc)
    acc_ref[3:4, :] += _colsum(dng * n)
    dn = dng * gn
    return r * (dn - n * jnp.mean(dn * n, axis=-1, keepdims=True))


def _ffn_fwd(x, vec, w_in, w_out, name):
    T = x.shape[0]

    def body(x_ref, vec_ref, win_hbm, wout_hbm, xo_ref, h_ref, a_ref, b_ref, s_ref, f_ref, win, wout):
        @pl.when(pl.program_id(0) == 0)
        def _():
            pltpu.sync_copy(win_hbm, win)
            pltpu.sync_copy(wout_hbm, wout)

        x = x_ref[...]
        sh, sc, gate, gn = vec_ref[0:1, :], vec_ref[1:2, :], vec_ref[2:3, :], vec_ref[3:4, :]
        _, _, h = _rms_fwd(x, gn, sc, sh)
        hb = h.astype(BF16)
        h_ref[...] = hb
        f = jnp.zeros((TB, D), F32)
        for j in range(2):
            cols = slice(j * FSH, (j + 1) * FSH)
            a = _nn(hb, win[j])
            b = _nn(hb, win[2 + j])
            s = (a * _sigmoid(a) * b).astype(BF16)
            a_ref[:, cols] = a.astype(BF16)
            b_ref[:, cols] = b.astype(BF16)
            s_ref[:, cols] = s
            f = f + _nn(s, wout[cols, :])
        xo_ref[...] = x + (0.5 * gate) * f
        f_ref[...] = f.astype(BF16)

    row = lambda w: pl.BlockSpec((TB, w), lambda i: (i, 0))
    return pl.pallas_call(
        body, name=name, grid=(T // TB,),
        in_specs=[row(D), pl.BlockSpec((8, D), lambda i: (0, 0)), ANY, ANY],
        out_specs=[row(D), row(D), row(DFF), row(DFF), row(DFF), row(D)],
        out_shape=[jax.ShapeDtypeStruct((T, D), F32), jax.ShapeDtypeStruct((T, D), BF16),
                   jax.ShapeDtypeStruct((T, DFF), BF16), jax.ShapeDtypeStruct((T, DFF), BF16),
                   jax.ShapeDtypeStruct((T, DFF), BF16), jax.ShapeDtypeStruct((T, D), BF16)],
        scratch_shapes=[pltpu.VMEM((NCHIP, D, FSH), BF16), pltpu.VMEM((DFF, D), BF16)],
        compiler_params=_params("arbitrary"),
    )(x, vec, w_in, w_out)


def _ffn_bwd(dxo, x, vec, a, b, f, w_in, w_out, name):
    T = x.shape[0]

    def body(dxo_ref, x_ref, vec_ref, a_ref, b_ref, f_ref, win_hbm, wout_hbm,
             dx_ref, df_ref, dab_ref, acc_ref, win, wout):
        @pl.when(pl.program_id(0) == 0)
        def _():
            pltpu.sync_copy(win_hbm, win)
            pltpu.sync_copy(wout_hbm, wout)
            acc_ref[...] = jnp.zeros_like(acc_ref)

        dxo = dxo_ref[...]
        x = x_ref[...]
        sh, sc, gate, gn = vec_ref[0:1, :], vec_ref[1:2, :], vec_ref[2:3, :], vec_ref[3:4, :]
        r, n, _ = _rms_fwd(x, gn, sc, sh)
        acc_ref[2:3, :] += _colsum(0.5 * f_ref[...].astype(F32) * dxo)
        dfb = ((0.5 * gate) * dxo).astype(BF16)
        df_ref[...] = dfb
        dh = jnp.zeros((TB, D), F32)
        for j in range(2):
            cols = slice(j * FSH, (j + 1) * FSH)
            ds = _nt(dfb, wout[cols, :])
            av = a_ref[:, cols].astype(F32)
            bv = b_ref[:, cols].astype(F32)
            sg = _sigmoid(av)
            da = (ds * bv * _dsilu(av, sg)).astype(BF16)
            db = (ds * (av * sg)).astype(BF16)
            dab_ref[j] = da
            dab_ref[2 + j] = db
            dh = dh + _nt(da, win[j]) + _nt(db, win[2 + j])
        dx_ref[...] = dxo + _rms_bwd(dh, r, n, gn, sc, acc_ref)

    row = lambda w: pl.BlockSpec((TB, w), lambda i: (i, 0))
    vec8 = pl.BlockSpec((8, D), lambda i: (0, 0))
    return pl.pallas_call(
        body, name=name, grid=(T // TB,),
        in_specs=[row(D), row(D), vec8, row(DFF), row(DFF), row(D), ANY, ANY],
        out_specs=[row(D), pl.BlockSpec((None, TB, D), lambda i: (0, i, 0)),
                   pl.BlockSpec((NCHIP, TB, FSH), lambda i: (0, i, 0)), vec8],
        out_shape=[jax.ShapeDtypeStruct((T, D), F32), jax.ShapeDtypeStruct((1, T, D), BF16),
                   jax.ShapeDtypeStruct((NCHIP, T, FSH), BF16), jax.ShapeDtypeStruct((8, D), F32)],
        scratch_shapes=[pltpu.VMEM((NCHIP, D, FSH), BF16), pltpu.VMEM((DFF, D), BF16)],
        compiler_params=_params("arbitrary"),
    )(dxo, x, vec, a, b, f, w_in, w_out)


def _mm_tn(a, b3, hp, kc, shard_rows, name, into=None, slab=0, slabs=None):
    T, M = a.shape
    P, _, N = b3.shape
    tm = M if M <= 1408 else M // 2
    tk = 512
    nk = T // tk
    slabs = P // hp if slabs is None else slabs
    half = shard_rows // 2
    extra = [] if into is None else list(into)

    def body(kc_ref, a_ref, b_ref, *rest):
        o_ref, h_ref = rest[-2:]

        @pl.when(pl.program_id(2) == 0)
        def _():
            o_ref[...] = jnp.zeros_like(o_ref)

        o_ref[...] += _tn(a_ref[...], b_ref[...])

        @pl.when(pl.program_id(2) == nk - 1)
        def _():
            for j in range(tm // shard_rows):
                start = pl.multiple_of(j * shard_rows + (1 - kc_ref[1]) * half, 8)
                h_ref[j * half:(j + 1) * half, :] = o_ref[pl.ds(start, half), :].astype(BF16)

    return pl.pallas_call(
        body, name=name,
        grid_spec=pltpu.PrefetchScalarGridSpec(
            num_scalar_prefetch=1, grid=(P, M // tm, nk),
            in_specs=[pl.BlockSpec((tk, tm), lambda p, i, k, kc: (k, i)),
                      pl.BlockSpec((None, tk, N), lambda p, i, k, kc: (p, k, 0))] + [ANY] * len(extra),
            out_specs=[pl.BlockSpec((None, tm, N), lambda p, i, k, kc: (slab + p // hp, i, p % hp)),
                       pl.BlockSpec((None, tm // 2, N), lambda p, i, k, kc: (slab + p // hp, i, p % hp))]),
        out_shape=[jax.ShapeDtypeStruct((slabs, M, hp * N), F32), jax.ShapeDtypeStruct((slabs, M // 2, hp * N), BF16)],
        input_output_aliases={} if into is None else {3: 0, 4: 1},
        compiler_params=_params("parallel", "parallel", "arbitrary"),
    )(kc, a, b3, *extra)


def _mix_proj_fwd(x, vec, w_in):
    T = x.shape[0]

    def body(x_ref, vec_ref, w_hbm, h_ref, qr_ref, g_ref, k_ref, v_ref, og_ref, u_ref, ua_ref, ub_ref,
             sa_ref, sb_ref, w):
        @pl.when(pl.program_id(0) == 0)
        def _():
            pltpu.sync_copy(w_hbm, w)

        x = x_ref[...]
        sh, sc, gn, lb = vec_ref[0:1, :], vec_ref[1:2, :], vec_ref[3:4, :], vec_ref[4:5, :]
        _, _, h = _rms_fwd(x, gn, sc, sh)
        hb = h.astype(BF16)
        h_ref[...] = hb
        p = _nn(hb, w[0])
        qr_ref[...] = p[:, :D].astype(BF16)
        fg = lb + (1.0 - lb) * _sigmoid(p[:, D:])
        g_ref[...] = jnp.log(fg)
        k_ref[...] = (1.0 - fg).astype(BF16)
        p = _nn(hb, w[1])
        v_ref[...] = p[:, :D].astype(BF16)
        og_ref[...] = p[:, D:].astype(BF16)
        p = _nn(hb, w[2])
        ua, ub = p[:, :D], p[:, D:]
        u_ref[...] = ua * _sigmoid(ub)
        ua_ref[...] = ua.astype(BF16)
        ub_ref[...] = ub.astype(BF16)
        p = _nn(hb, w[3])
        sa_ref[...] = _sigmoid(p[:, :D]).astype(BF16)
        sb_ref[...] = _sigmoid(p[:, D:]).astype(BF16)

    row = pl.BlockSpec((TB, D), lambda i: (i, 0))
    bf = jax.ShapeDtypeStruct((T, D), BF16)
    f32 = jax.ShapeDtypeStruct((T, D), F32)
    return pl.pallas_call(
        body, name="mix_proj_fwd", grid=(T // TB,),
        in_specs=[row, pl.BlockSpec((8, D), lambda i: (0, 0)), ANY],
        out_specs=[row] * 11,
        out_shape=[bf, bf, f32, bf, bf, bf, f32, bf, bf, bf, bf],
        scratch_shapes=[pltpu.VMEM((NCHIP, D, 2 * D), BF16)],
        compiler_params=_params("arbitrary"),
    )(x, vec, w_in)


def _mix_proj_bwd(dxo, x, vec, dpa, dpb, dpc, w_in):
    T = x.shape[0]

    def body(dxo_ref, x_ref, vec_ref, dpa_ref, dpb_ref, dpc_ref, w_hbm, dx_ref, acc_ref, w):
        @pl.when(pl.program_id(0) == 0)
        def _():
            pltpu.sync_copy(w_hbm, w)
            acc_ref[...] = jnp.zeros_like(acc_ref)

        x = x_ref[...]
        sh, sc, gn = vec_ref[0:1, :], vec_ref[1:2, :], vec_ref[3:4, :]
        r, n, _ = _rms_fwd(x, gn, sc, sh)
        dh = jnp.zeros((TB, D), F32)
        for p in range(8):
            src = dpa_ref[p] if p < 4 else (dpb_ref[p - 4] if p < 6 else dpc_ref[p - 6])
            dh = dh + _nt(src, w[p // 2, :, (p % 2) * D:(p % 2 + 1) * D])
        dx_ref[...] = dxo_ref[...] + _rms_bwd(dh, r, n, gn, sc, acc_ref)

    row = pl.BlockSpec((TB, D), lambda i: (i, 0))
    vec8 = pl.BlockSpec((8, D), lambda i: (0, 0))
    stack = lambda k: pl.BlockSpec((k, TB, D), lambda i: (0, i, 0))
    return pl.pallas_call(
        body, name="mix_proj_bwd", grid=(T // TB,),
        in_specs=[row, row, vec8, stack(4), stack(2), stack(2), ANY],
        out_specs=[row, vec8],
        out_shape=[jax.ShapeDtypeStruct((T, D), F32), jax.ShapeDtypeStruct((8, D), F32)],
        scratch_shapes=[pltpu.VMEM((NCHIP, D, 2 * D), BF16)],
        compiler_params=_params("arbitrary"),
    )(dxo, x, vec, dpa, dpb, dpc, w_in)


def _tri(lower):
    r = lax.broadcasted_iota(jnp.int32, (CHUNK, CHUNK), 0)
    c = lax.broadcasted_iota(jnp.int32, (CHUNK, CHUNK), 1)
    return (c <= r) if lower else (c >= r)


def _cumsum_rows(mask, g):
    return jnp.dot(mask.astype(F32), g, precision=lax.Precision.HIGHEST, preferred_element_type=F32)


def _chunk_decay(low, g, nck):
    bs, mids, lasts = [], [], []
    for c in range(nck):
        gc = g[c * CHUNK:(c + 1) * CHUNK]
        bs.append(_cumsum_rows(low, gc))
        mids.append(_colsum(gc[0:CHUNK // 2]))
        lasts.append(_colsum(gc))
    spread = lambda rows: jnp.concatenate([jnp.broadcast_to(r, (CHUNK, DK)) for r in rows], axis=0)
    return jnp.concatenate(bs, axis=0), spread(mids), spread(lasts), lasts


def _hgrn_fwd(qr, g, k, v, og, vec):
    T = qr.shape[0]
    nck = CB // CHUNK

    def body(qr_ref, g_ref, k_ref, v_ref, og_ref, vec_ref, out_ref, o_ref, st_ref, state):
        @pl.when(pl.program_id(1) == 0)
        def _():
            state[...] = jnp.zeros_like(state)

        low = _tri(True)
        qv = qr_ref[...].astype(F32)
        q = qv * _sigmoid(qv) * (DK ** -0.5)
        kk = k_ref[...].astype(F32)
        vb = v_ref[...]
        b, mid, last, lasts = _chunk_decay(low, g_ref[...], nck)
        qt = (q * jnp.exp(b - mid)).astype(BF16)
        kt = (kk * jnp.exp(mid - b)).astype(BF16)
        qe = (q * jnp.exp(b)).astype(BF16)
        kd = (kk * jnp.exp(last - b)).astype(BF16)
        intra, grow = [], []
        for c in range(nck):
            r = slice(c * CHUNK, (c + 1) * CHUNK)
            att = jnp.where(low, _nt(qt[r], kt[r]), 0.0).astype(BF16)
            intra.append(_nn(att, vb[r]))
            grow.append(_tn(vb[r], kd[r]))
        st = state[...]
        inter = []
        for c in range(nck):
            stb = st.astype(BF16)
            st_ref[c] = stb
            inter.append(_nt(qe[c * CHUNK:(c + 1) * CHUNK], stb))
            st = st * jnp.exp(lasts[c]) + grow[c]
        state[...] = st
        o = jnp.concatenate(intra, axis=0) + jnp.concatenate(inter, axis=0)
        o_ref[...] = o
        ogv = og_ref[...].astype(F32)
        rms = lax.rsqrt(jnp.mean(o * o, axis=-1, keepdims=True) + EPS)
        out_ref[...] = (o * rms * vec_ref[5:6, :] * (ogv * _sigmoid(ogv))).astype(BF16)

    blk = pl.BlockSpec((CB, DK), lambda h, i: (i, h))
    return pl.pallas_call(
        body, name="hgrn_fwd", grid=(HEADS, T // CB),
        in_specs=[blk, blk, blk, blk, blk, pl.BlockSpec((8, DK), lambda h, i: (0, h))],
        out_specs=[blk, blk, pl.BlockSpec((None, nck, DK, DK), lambda h, i: (h, i, 0, 0))],
        out_shape=[jax.ShapeDtypeStruct((T, D), BF16), jax.ShapeDtypeStruct((T, D), F32),
                   jax.ShapeDtypeStruct((HEADS, T // CHUNK, DK, DK), BF16)],
        scratch_shapes=[pltpu.VMEM((DK, DK), F32)],
        compiler_params=_params("parallel", "arbitrary"),
    )(qr, g, k, v, og, vec)


def _hgrn_bwd(dout, og, qr, g, k, v, o, st, vec):
    T = qr.shape[0]
    nck = CB // CHUNK
    nb = T // CB

    def body(dout_ref, og_ref, qr_ref, g_ref, k_ref, v_ref, o_ref, st_ref, vec_ref,
             dp_ref, acc_ref, dstate):
        @pl.when(pl.program_id(1) == 0)
        def _():
            dstate[...] = jnp.zeros_like(dstate)
            acc_ref[...] = jnp.zeros_like(acc_ref)

        o = o_ref[...]
        ogv = og_ref[...].astype(F32)
        dout = dout_ref[...].astype(F32)
        hg = vec_ref[5:6, :]
        sgo = _sigmoid(ogv)
        rms = lax.rsqrt(jnp.mean(o * o, axis=-1, keepdims=True) + EPS)
        ohat = o * rms
        dp_ref[3] = (dout * (ohat * hg) * _dsilu(ogv, sgo)).astype(BF16)
        don = dout * (ogv * sgo)
        acc_ref[0:1, :] += _colsum(don * ohat)
        dohat = don * hg
        dob = (rms * (dohat - ohat * jnp.mean(dohat * ohat, axis=-1, keepdims=True))).astype(BF16)

        low = _tri(True)
        upp = _tri(False)
        lb = vec_ref[4:5, :]
        qv = qr_ref[...].astype(F32)
        sgq = _sigmoid(qv)
        q = qv * sgq * (DK ** -0.5)
        kk = k_ref[...].astype(F32)
        vb = v_ref[...]
        gv = g_ref[...]
        b, mid, last, lasts = _chunk_decay(low, gv, nck)
        eq = jnp.exp(b - mid)
        ek = jnp.exp(mid - b)
        eb = jnp.exp(b)
        ed = jnp.exp(last - b)
        qtb, ktb, qeb, kdb = ((t).astype(BF16) for t in (q * eq, kk * ek, q * eb, kk * ed))
        rows = [slice(c * CHUNK, (c + 1) * CHUNK) for c in range(nck)]

        dv1, dqt, dkt, dqe, grow = [], [], [], [], []
        for c, r in enumerate(rows):
            att = jnp.where(low, _nt(qtb[r], ktb[r]), 0.0).astype(BF16)
            datt = jnp.where(low, _nt(dob[r], vb[r]), 0.0).astype(BF16)
            dv1.append(_tn(att, dob[r]))
            dqt.append(_nn(datt, ktb[r]))
            dkt.append(_tn(datt, qtb[r]))
            dqe.append(_nn(dob[r], st_ref[c]))
            grow.append(_tn(dob[r], qeb[r]))
        ds = dstate[...]
        ds1b, dl_state = [None] * nck, [None] * nck
        for c in reversed(range(nck)):
            el = jnp.exp(lasts[c])
            ds1b[c] = ds.astype(BF16)
            dl_state[c] = el * _colsum(ds * st_ref[c].astype(F32))
            ds = ds * el + grow[c]
        dstate[...] = ds
        dkd = jnp.concatenate([_nn(vb[r], ds1b[c]) for c, r in enumerate(rows)], axis=0)
        dv = jnp.concatenate(dv1, axis=0) + jnp.concatenate([_nt(kdb[r], ds1b[c]) for c, r in enumerate(rows)], axis=0)
        dqt, dkt, dqe = (jnp.concatenate(t, axis=0) for t in (dqt, dkt, dqe))
        dq = dqt * eq + dqe * eb
        dk = dkt * ek + dkd * ed
        dkdkd = dkd * kdb.astype(F32)
        db = dqt * qtb.astype(F32) - dkt * ktb.astype(F32) + dqe * qeb.astype(F32) - dkdkd
        dg = jnp.concatenate([_cumsum_rows(upp, db[r]) + (_colsum(dkdkd[r]) + dl_state[c])
                              for c, r in enumerate(rows)], axis=0)
        fg = jnp.exp(gv)
        dfg = dg / fg - dk
        sig = (fg - lb) / (1.0 - lb)
        dp_ref[0] = (dq * (DK ** -0.5) * _dsilu(qv, sgq)).astype(BF16)
        dp_ref[1] = (dfg * (1.0 - lb) * sig * (1.0 - sig)).astype(BF16)
        dp_ref[2] = dv.astype(BF16)
        dlb = _colsum(dfg * (1.0 - sig)) * (lb * (1.0 - lb))
        acc_ref[1:2, :] += dlb
        acc_ref[2:3, :] -= dlb

    blk = pl.BlockSpec((CB, DK), lambda h, i: (nb - 1 - i, h))
    return pl.pallas_call(
        body, name="hgrn_bwd", grid=(HEADS, nb),
        in_specs=[blk, blk, blk, blk, blk, blk, blk,
                  pl.BlockSpec((None, nck, DK, DK), lambda h, i: (h, nb - 1 - i, 0, 0)),
                  pl.BlockSpec((8, DK), lambda h, i: (0, h))],
        out_specs=[pl.BlockSpec((4, CB, DK), lambda h, i: (0, nb - 1 - i, h)),
                   pl.BlockSpec((8, DK), lambda h, i: (0, h))],
        out_shape=[jax.ShapeDtypeStruct((4, T, D), BF16), jax.ShapeDtypeStruct((8, D), F32)],
        scratch_shapes=[pltpu.VMEM((DK, DK), F32)],
        compiler_params=_params("parallel", "arbitrary"),
    )(dout, og, qr, g, k, v, o, st, vec)


def _ln_fwd(uc, lg, lbias):
    mu = jnp.mean(uc, axis=-1, keepdims=True)
    xc = uc - mu
    rstd = lax.rsqrt(jnp.mean(xc * xc, axis=-1, keepdims=True) + EPS)
    z = xc * rstd
    return rstd, z, z * lg + lbias


LANES = 128
SUBLANES = 8
CONV_ROWS = 128


def _lane_tiles():
    return [slice(l * LANES, (l + 1) * LANES) for l in range(D // LANES)]


def _row_shifts(x):
    n = x.shape[0]
    return [x] + [pltpu.roll(x, n - r, axis=0) for r in range(1, SUBLANES)]


def _shifted_rows(shifted, start, rows=TB):
    a, r = divmod(start, SUBLANES)
    return shifted[r][a * SUBLANES:a * SUBLANES + rows]


def _conv_fwd(u, cw, cvec):
    T = u.shape[0]
    per = TB // HALO

    def body(u_ref, halo_ref, cw_ref, cvec_ref, us_ref, uc_ref, pad):
        i = pl.program_id(0)
        pad[0:HALO, :] = jnp.where(i > 0, halo_ref[...], 0.0)
        pad[HALO:, :] = u_ref[...]
        for lanes in _lane_tiles():
            shifted = _row_shifts(pad[:, lanes])
            acc = jnp.broadcast_to(cvec_ref[0:1, lanes], (TB, LANES))
            for j in range(CONV_K):
                acc = acc + cw_ref[j:j + 1, lanes] * _shifted_rows(shifted, HALO - (CONV_K - 1) + j)
            uc_ref[:, lanes] = acc
        _, _, ul = _ln_fwd(uc_ref[...], cvec_ref[1:2, :], cvec_ref[2:3, :])
        us_ref[...] = (ul * _sigmoid(ul)).astype(BF16)

    row = pl.BlockSpec((TB, D), lambda i: (i, 0))
    return pl.pallas_call(
        body, name="conv_fwd", grid=(T // TB,),
        in_specs=[row, pl.BlockSpec((HALO, D), lambda i: (jnp.maximum(i * per - 1, 0), 0)),
                  pl.BlockSpec((32, D), lambda i: (0, 0)), pl.BlockSpec((8, D), lambda i: (0, 0))],
        out_specs=[row, row],
        out_shape=[jax.ShapeDtypeStruct((T, D), BF16), jax.ShapeDtypeStruct((T, D), F32)],
        scratch_shapes=[pltpu.VMEM((TB + HALO, D), F32)],
        compiler_params=_params("parallel"),
    )(u, u, cw, cvec)


def _conv_bwd_ln(dus, uc, cvec):
    T = uc.shape[0]

    def body(dus_ref, uc_ref, cvec_ref, duc_ref, acc_ref):
        @pl.when(pl.program_id(0) == 0)
        def _():
            acc_ref[...] = jnp.zeros_like(acc_ref)

        lg = cvec_ref[1:2, :]
        rstd, z, ul = _ln_fwd(uc_ref[...], lg, cvec_ref[2:3, :])
        dul = dus_ref[...].astype(F32) * _dsilu(ul, _sigmoid(ul))
        acc_ref[1:2, :] += _colsum(dul * z)
        acc_ref[2:3, :] += _colsum(dul)
        dz = dul * lg
        duc = rstd * (dz - jnp.mean(dz, axis=-1, keepdims=True) - z * jnp.mean(dz * z, axis=-1, keepdims=True))
        acc_ref[0:1, :] += _colsum(duc)
        duc_ref[...] = duc

    row = pl.BlockSpec((TB, D), lambda i: (i, 0))
    vec8 = pl.BlockSpec((8, D), lambda i: (0, 0))
    return pl.pallas_call(
        body, name="conv_bwd_ln", grid=(T // TB,),
        in_specs=[row, row, vec8], out_specs=[row, vec8],
        out_shape=[jax.ShapeDtypeStruct((T, D), F32), jax.ShapeDtypeStruct((8, D), F32)],
        compiler_params=_params("arbitrary"),
    )(dus, uc, cvec)


def _conv_bwd_taps(duc, u, ua, ub, cw):
    T = u.shape[0]
    per = TB // HALO
    nblk = T // TB

    def body(duc_ref, dnext_ref, u_ref, uprev_ref, ua_ref, ub_ref, cw_ref, dp_ref, dcw_ref, upad, dpad, dcw):
        i = pl.program_id(0)

        @pl.when(i == 0)
        def _():
            dcw[...] = jnp.zeros_like(dcw)

        upad[0:HALO, :] = jnp.where(i > 0, uprev_ref[...], 0.0)
        upad[HALO:, :] = u_ref[...]
        dpad[0:TB, :] = duc_ref[...]
        dpad[TB:, :] = jnp.where(i < nblk - 1, dnext_ref[...], 0.0)
        for lanes in _lane_tiles():
            ushift = _row_shifts(upad[:, lanes])
            dshift = _row_shifts(dpad[:, lanes])
            for r0 in range(0, TB, CONV_ROWS):
                rows = slice(r0, r0 + CONV_ROWS)
                duc = duc_ref[rows, lanes]
                du = jnp.zeros((CONV_ROWS, LANES), F32)
                for j in range(CONV_K):
                    prod = duc * _shifted_rows(ushift, r0 + HALO - (CONV_K - 1) + j, CONV_ROWS)
                    dcw[j, :, lanes] += jnp.sum(prod.reshape(CONV_ROWS // SUBLANES, SUBLANES, LANES), axis=0)
                    du = du + cw_ref[j:j + 1, lanes] * _shifted_rows(dshift, r0 + CONV_K - 1 - j, CONV_ROWS)
                ua = ua_ref[rows, lanes].astype(F32)
                sg = _sigmoid(ub_ref[rows, lanes].astype(F32))
                dp_ref[0, rows, lanes] = (du * sg).astype(BF16)
                dp_ref[1, rows, lanes] = (du * ua * sg * (1.0 - sg)).astype(BF16)

        @pl.when(i == nblk - 1)
        def _():
            dcw_ref[...] = jnp.sum(dcw[...], axis=1)

    row = pl.BlockSpec((TB, D), lambda i: (i, 0))
    return pl.pallas_call(
        body, name="conv_bwd_taps", grid=(nblk,),
        in_specs=[row, pl.BlockSpec((HALO, D), lambda i: (jnp.minimum((i + 1) * per, T // HALO - 1), 0)),
                  row, pl.BlockSpec((HALO, D), lambda i: (jnp.maximum(i * per - 1, 0), 0)),
                  row, row, pl.BlockSpec((32, D), lambda i: (0, 0))],
        out_specs=[pl.BlockSpec((2, TB, D), lambda i: (0, i, 0)), pl.BlockSpec((32, D), lambda i: (0, 0))],
        out_shape=[jax.ShapeDtypeStruct((2, T, D), BF16), jax.ShapeDtypeStruct((32, D), F32)],
        scratch_shapes=[pltpu.VMEM((TB + HALO, D), F32), pltpu.VMEM((TB + HALO, D), F32),
                        pltpu.VMEM((32, SUBLANES, D), F32)],
        compiler_params=_params("arbitrary"),
    )(duc, duc, u, u, ua, ub, cw)


def _merge_fwd(x, oa, us, sa, sb, vec, w_ho, w_co, w_mo):
    T = x.shape[0]

    def body(x_ref, oa_ref, us_ref, sa_ref, sb_ref, vec_ref, who_hbm, wco_hbm, wmo_hbm,
             xo_ref, ya_ref, yb_ref, mg_ref, mo_ref, who, wco, wmo):
        @pl.when(pl.program_id(0) == 0)
        def _():
            pltpu.sync_copy(who_hbm, who)
            pltpu.sync_copy(wco_hbm, wco)
            pltpu.sync_copy(wmo_hbm, wmo)

        ya = _nn(oa_ref[...], who[...])
        yb = _nn(us_ref[...], wco[...])
        mg = (sa_ref[...].astype(F32) * ya + sb_ref[...].astype(F32) * yb).astype(BF16)
        mo = _nn(mg, wmo[...])
        xo_ref[...] = x_ref[...] + vec_ref[2:3, :] * mo
        ya_ref[...] = ya.astype(BF16)
        yb_ref[...] = yb.astype(BF16)
        mg_ref[...] = mg
        mo_ref[...] = mo.astype(BF16)

    row = pl.BlockSpec((TB, D), lambda i: (i, 0))
    bf = jax.ShapeDtypeStruct((T, D), BF16)
    wv = pltpu.VMEM((D, D), BF16)
    return pl.pallas_call(
        body, name="merge_fwd", grid=(T // TB,),
        in_specs=[row, row, row, row, row, pl.BlockSpec((8, D), lambda i: (0, 0)), ANY, ANY, ANY],
        out_specs=[row] * 5,
        out_shape=[jax.ShapeDtypeStruct((T, D), F32), bf, bf, bf, bf],
        scratch_shapes=[wv, wv, wv],
        compiler_params=_params("arbitrary"),
    )(x, oa, us, sa, sb, vec, w_ho, w_co, w_mo)


def _merge_bwd(dxo, mo, ya, yb, sa, sb, vec, w_ho, w_co, w_mo):
    T = dxo.shape[0]

    def body(dxo_ref, mo_ref, ya_ref, yb_ref, sa_ref, sb_ref, vec_ref, who_hbm, wco_hbm, wmo_hbm,
             dmo_ref, dya_ref, dyb_ref, doa_ref, dus_ref, dp_ref, acc_ref, who, wco, wmo):
        @pl.when(pl.program_id(0) == 0)
        def _():
            pltpu.sync_copy(who_hbm, who)
            pltpu.sync_copy(wco_hbm, wco)
            pltpu.sync_copy(wmo_hbm, wmo)
            acc_ref[...] = jnp.zeros_like(acc_ref)

        dxo = dxo_ref[...]
        acc_ref[2:3, :] += _colsum(mo_ref[...].astype(F32) * dxo)
        dmo = (vec_ref[2:3, :] * dxo).astype(BF16)
        dmo_ref[...] = dmo
        dmg = _nt(dmo, wmo[...])
        sa = sa_ref[...].astype(F32)
        sb = sb_ref[...].astype(F32)
        dya = (sa * dmg).astype(BF16)
        dyb = (sb * dmg).astype(BF16)
        dya_ref[...] = dya
        dyb_ref[...] = dyb
        dp_ref[0] = (dmg * ya_ref[...].astype(F32) * sa * (1.0 - sa)).astype(BF16)
        dp_ref[1] = (dmg * yb_ref[...].astype(F32) * sb * (1.0 - sb)).astype(BF16)
        doa_ref[...] = _nt(dya, who[...]).astype(BF16)
        dus_ref[...] = _nt(dyb, wco[...]).astype(BF16)

    row = pl.BlockSpec((TB, D), lambda i: (i, 0))
    one = pl.BlockSpec((None, TB, D), lambda i: (0, i, 0))
    vec8 = pl.BlockSpec((8, D), lambda i: (0, 0))
    bf = jax.ShapeDtypeStruct((T, D), BF16)
    bf1 = jax.ShapeDtypeStruct((1, T, D), BF16)
    wv = pltpu.VMEM((D, D), BF16)
    return pl.pallas_call(
        body, name="merge_bwd", grid=(T // TB,),
        in_specs=[row, row, row, row, row, row, vec8, ANY, ANY, ANY],
        out_specs=[one, one, one, row, row, pl.BlockSpec((2, TB, D), lambda i: (0, i, 0)), vec8],
        out_shape=[bf1, bf1, bf1, bf, bf, jax.ShapeDtypeStruct((2, T, D), BF16), jax.ShapeDtypeStruct((8, D), F32)],
        scratch_shapes=[wv, wv, wv],
        compiler_params=_params("arbitrary"),
    )(dxo, mo, ya, yb, sa, sb, vec, w_ho, w_co, w_mo)


def _head(x, tgt, gvec):
    T = x.shape[0]

    def body(x_ref, t_ref, g_ref, dx_ref, acc_ref):
        @pl.when(pl.program_id(0) == 0)
        def _():
            acc_ref[...] = jnp.zeros_like(acc_ref)

        x = x_ref[...]
        gf = g_ref[0:1, :]
        r = lax.rsqrt(jnp.mean(x * x, axis=-1, keepdims=True) + EPS)
        n = x * r
        err = n * gf - t_ref[...]
        acc_ref[1:2, :] += _colsum(err * err)
        dy = err * (1.0 / D)
        acc_ref[0:1, :] += _colsum(dy * n)
        dn = dy * gf
        dx_ref[...] = r * (dn - n * jnp.mean(dn * n, axis=-1, keepdims=True))

    row = pl.BlockSpec((TB, D), lambda i: (i, 0))
    vec8 = pl.BlockSpec((8, D), lambda i: (0, 0))
    return pl.pallas_call(
        body, name="loss_head", grid=(T // TB,),
        in_specs=[row, row, vec8], out_specs=[row, vec8],
        out_shape=[jax.ShapeDtypeStruct((T, D), F32), jax.ShapeDtypeStruct((8, D), F32)],
        compiler_params=_params("arbitrary"),
    )(x, tgt, gvec)


def _pack_rows(parts, total, name):
    def body(*refs):
        out = refs[-1]
        out[...] = jnp.zeros_like(out)
        for ref, (_, src, n, dst) in zip(refs[:-1], parts):
            out[dst:dst + n, :] = ref[src:src + n, :]

    arrs = [p[0] for p in parts]
    return pl.pallas_call(
        body, name=name, in_specs=[pl.BlockSpec(a.shape, lambda: (0, 0)) for a in arrs],
        out_specs=pl.BlockSpec((total, D), lambda: (0, 0)),
        out_shape=jax.ShapeDtypeStruct((total, D), F32),
    )(*arrs)


PACK_ROWS = 56
PACK_AT = {"ada_b": 0, "loss": 9, "norm_ffn1": 10, "norm_mix": 11, "hgrn_g": 12, "conv_b": 13, "conv_ln_g": 14,
           "conv_ln_b": 15, "norm_ffn2": 16, "norm_final": 17, "hgrn_lb": 18, "conv_w": 20}


def _local_step(x, tgt, mod, small, kc, weight, reduce, reduce_small):
    lb = jax.nn.sigmoid(small["hgrn_lb"][0:1] - small["hgrn_lb"][1:2])
    vec1 = _pack_rows([(mod, 0, 3, 0), (small["norm_ffn1"], 0, 1, 3)], 8, "pack_vec1")
    vec2 = _pack_rows([(mod, 3, 3, 0), (small["norm_mix"], 0, 1, 3), (lb, 0, 1, 4), (small["hgrn_g"], 0, 1, 5)],
                      8, "pack_vec2")
    vec3 = _pack_rows([(mod, 6, 3, 0), (small["norm_ffn2"], 0, 1, 3)], 8, "pack_vec3")
    cvec = _pack_rows([(small["conv_b"], 0, 1, 0), (small["conv_ln_g"], 0, 1, 1), (small["conv_ln_b"], 0, 1, 2)],
                      8, "pack_cvec")
    cw = small["conv_w"]
    gvec = _pack_rows([(small["norm_final"], 0, 1, 0)], 8, "pack_gvec")

    wg = {n: weight(n, vec1) for n in ("ffn1_w_in", "ffn1_w_out")}
    x1, h1, a1, b1, s1, f1 = _ffn_fwd(x, vec1, wg["ffn1_w_in"], wg["ffn1_w_out"], "ffn1_fwd")
    wg["mix_w_in"] = weight("mix_w_in", x1)
    h2, qr, g, k, v, og, u, ua, ub, sa, sb = _mix_proj_fwd(x1, vec2, wg["mix_w_in"])
    oa, o, st = _hgrn_fwd(qr, g, k, v, og, vec2)
    us, uc = _conv_fwd(u, cw, cvec)
    wg.update({n: weight(n, us) for n in ("hgrn_w_o", "conv_w_o", "mix_w_out")})
    x2, ya, yb, mg, mo = _merge_fwd(x1, oa, us, sa, sb, vec2, wg["hgrn_w_o"], wg["conv_w_o"], wg["mix_w_out"])
    wg.update({n: weight(n, x2) for n in ("ffn2_w_in", "ffn2_w_out")})
    x3, h3, a3, b3, s3, f3 = _ffn_fwd(x2, vec3, wg["ffn2_w_in"], wg["ffn2_w_out"], "ffn2_fwd")

    dx3, acc_head = _head(x3, tgt, gvec)
    dx2, df3, dab3, acc3 = _ffn_bwd(dx3, x2, vec3, a3, b3, f3, wg["ffn2_w_in"], wg["ffn2_w_out"], "ffn2_bwd")
    tok = reduce(("ffn2_w_out", "ffn2_w_in"), [_mm_tn(s3, df3, 1, kc, DFF // NCHIP, "ffn2_dwout"),
                                               _mm_tn(h3, dab3, 1, kc, D, "ffn2_dwin")])
    vec2b = vec2 + tok[0:1, 0:1]
    dmo, dya, dyb, doa, dus, dpc, acc_m = _merge_bwd(dx2, mo, ya, yb, sa, sb, vec2b,
                                                     wg["hgrn_w_o"], wg["conv_w_o"], wg["mix_w_out"])
    tok = reduce(("mix_w_out", "hgrn_w_o", "conv_w_o"),
                 [_mm_tn(mg, dmo, 1, kc, D // NCHIP, "mix_dwout"), _mm_tn(oa, dya, 1, kc, D // NCHIP, "hgrn_dwo"),
                  _mm_tn(us, dyb, 1, kc, D // NCHIP, "conv_dwo")])
    vec2c = vec2 + tok[0:1, 0:1]
    duc, acc_c = _conv_bwd_ln(dus, uc, cvec)
    dpb, dcw = _conv_bwd_taps(duc, u, ua, ub, cw)
    dpa, acc_h = _hgrn_bwd(doa, og, qr, g, k, v, o, st, vec2c)
    dx1, acc2 = _mix_proj_bwd(dx2, x1, vec2c, dpa, dpb, dpc, wg["mix_w_in"])
    gmix = _mm_tn(h2, dpa, 2, kc, D, "mix_dwin_a", slabs=NCHIP)
    gmix = _mm_tn(h2, dpb, 2, kc, D, "mix_dwin_b", into=gmix, slab=2, slabs=NCHIP)
    gmix = _mm_tn(h2, dpc, 2, kc, D, "mix_dwin_c", into=gmix, slab=3, slabs=NCHIP)
    tok = reduce(("mix_w_in",), [gmix])
    vec1b = vec1 + tok[0:1, 0:1]
    dx0, df1, dab1, acc1 = _ffn_bwd(dx1, x, vec1b, a1, b1, f1, wg["ffn1_w_in"], wg["ffn1_w_out"], "ffn1_bwd")

    at = PACK_AT
    packed = _pack_rows([
        (acc1, 0, 3, at["ada_b"]), (acc2, 0, 2, at["ada_b"] + 3), (acc_m, 2, 1, at["ada_b"] + 5),
        (acc3, 0, 3, at["ada_b"] + 6), (acc_head, 1, 1, at["loss"]), (acc1, 3, 1, at["norm_ffn1"]),
        (acc2, 3, 1, at["norm_mix"]), (acc_h, 0, 1, at["hgrn_g"]), (acc_c, 0, 3, at["conv_b"]),
        (acc3, 3, 1, at["norm_ffn2"]), (acc_head, 0, 1, at["norm_final"]), (acc_h, 1, 2, at["hgrn_lb"]),
        (dcw, 0, CONV_K, at["conv_w"])], PACK_ROWS, "pack_small_grads")
    done = reduce_small(packed)
    reduce(("ffn1_w_out", "ffn1_w_in"), [_mm_tn(s1, df1, 1, kc, DFF // NCHIP, "ffn1_dwout"),
                                         _mm_tn(h1, dab1, 1, kc, D, "ffn1_dwin")], done)
    return dx0


BLOCK_BYTES = 3 * 512 * 1024


def _row_block(rows, cols):
    for br in (512, 352, 256, 176, 128, 64, 32, 16, 8):
        if rows % br == 0 and br * cols * 4 <= BLOCK_BYTES:
            return br
    return rows


def _cast_into_slot(w, kc, name):
    R, C = w.shape
    br = _row_block(R, C)

    def body(kc_ref, w_ref, o_ref):
        o_ref[...] = w_ref[...].astype(BF16)

    return pl.pallas_call(
        body, name=name,
        grid_spec=pltpu.PrefetchScalarGridSpec(
            num_scalar_prefetch=1, grid=(R // br,),
            in_specs=[pl.BlockSpec((br, C), lambda i, kc: (i, 0))],
            out_specs=pl.BlockSpec((None, br, C), lambda i, kc: (kc[0], i, 0))),
        out_shape=jax.ShapeDtypeStruct((NCHIP, R, C), BF16), compiler_params=_params("parallel"),
    )(kc, w)


def _adamw(w, g, m, v, name, after=None):
    R, C = w.shape
    br = _row_block(R, C)
    extra = [] if after is None else [after]

    def body(w_ref, g_ref, m_ref, v_ref, *rest):
        d_ref, nm_ref, nv_ref = rest[-3:]
        gv = g_ref[...]
        nm = ADAM_B1 * m_ref[...] + (1.0 - ADAM_B1) * gv
        nv = ADAM_B2 * v_ref[...] + (1.0 - ADAM_B2) * (gv * gv)
        m_hat = nm / (1.0 - ADAM_B1 ** ADAM_STEP)
        v_hat = nv / (1.0 - ADAM_B2 ** ADAM_STEP)
        d_ref[...] = -ADAM_LR * (m_hat / (jnp.sqrt(v_hat) + ADAM_EPS) + ADAM_WD * w_ref[...])
        nm_ref[...] = nm
        nv_ref[...] = nv

    blk = pl.BlockSpec((br, C), lambda i: (i, 0))
    out = jax.ShapeDtypeStruct((R, C), F32)
    return pl.pallas_call(
        body, name=name, grid=(R // br,), in_specs=[blk] * 4 + [ANY] * len(extra), out_specs=[blk] * 3,
        out_shape=[out, out, out], compiler_params=_params("parallel"),
    )(w, g, m, v, *extra)


def _coords():
    return lax.axis_index("x"), lax.axis_index("y"), lax.axis_index("c")


def _flip(v, bit):
    return 1 - v if bit else v


def _allgather8(v, name):
    R, C = v.shape

    def body(v_ref, out_ref, send_sems, recv_sems, local_sem):
        x, y, c = _coords()
        me = 4 * x + 2 * y + c
        mine = pltpu.make_async_copy(v_ref, out_ref.at[me], local_sem)
        mine.start()

        def copy(m, block):
            peer = (_flip(x, m & 4), _flip(y, m & 2), _flip(c, m & 1))
            return pltpu.make_async_remote_copy(
                src_ref=v_ref, dst_ref=out_ref.at[block], send_sem=send_sems.at[m - 1],
                recv_sem=recv_sems.at[m - 1], device_id=peer, device_id_type=MESH)

        sends = [copy(m, me) for m in range(1, 8)]
        for cp in sends:
            cp.start()
        for m in range(1, 8):
            sender = 4 * _flip(x, m & 4) + 2 * _flip(y, m & 2) + _flip(c, m & 1)
            copy(m, sender).wait_recv()
        for cp in sends:
            cp.wait_send()
        mine.wait()

    vm = pl.BlockSpec(memory_space=pltpu.VMEM)
    return pl.pallas_call(
        body, name=name, in_specs=[vm], out_specs=vm,
        out_shape=jax.ShapeDtypeStruct((8, R, C), F32),
        scratch_shapes=[pltpu.SemaphoreType.DMA((7,)), pltpu.SemaphoreType.DMA((7,)), pltpu.SemaphoreType.DMA],
    )(v)


HBM = pl.BlockSpec(memory_space=pltpu.HBM)
SEM = pl.BlockSpec(memory_space=pltpu.SEMAPHORE)
EFFECT = pltpu.SideEffectType.DATAFLOW_SIDE_EFFECTING


def _chip_peer(x, y, m):
    px, py = _flip(x, m & 2), _flip(y, m & 1)
    return px, py, 2 * px + py


def _core_rows(land, c):
    half = land.shape[1] // 2
    return pl.ds(pl.multiple_of(c * half, 16), half)


def _gather_start(lands, groups, halved, after):
    n, ng, na = len(lands), len(groups), len(after)

    def body(*refs):
        ins = refs[:n]
        sends, recvs = refs[n + na:n + na + ng], refs[n + na + ng:n + na + 2 * ng]
        token = refs[n + na + 2 * ng + n]
        x, y, c = _coords()
        k = 2 * x + y
        for gi, grp in enumerate(groups):
            for j, t in enumerate(grp):
                mine = ins[t].at[k, _core_rows(ins[t], c), :] if halved[gi] else ins[t].at[k]
                for m in (1, 2, 3):
                    px, py, _ = _chip_peer(x, y, m)
                    pltpu.make_async_remote_copy(
                        src_ref=mine, dst_ref=mine, send_sem=sends[gi].at[3 * j + m - 1],
                        recv_sem=recvs[gi].at[3 * j + m - 1], device_id=(px, py, c), device_id_type=MESH).start()
        token[...] = jnp.zeros_like(token)

    sems = [pltpu.SemaphoreType.DMA((3 * len(g),)) for g in groups]
    out = pl.pallas_call(
        body, name="gather_weights_start",
        out_shape=sems + sems + [pltpu.HBM(a.shape, a.dtype) for a in lands] + [jax.ShapeDtypeStruct((8, 128), F32)],
        in_specs=[HBM] * n + [ANY] * na,
        out_specs=[SEM] * (2 * ng) + [HBM] * n + [pl.BlockSpec(memory_space=pltpu.VMEM)],
        input_output_aliases={t: 2 * ng + t for t in range(n)},
        compiler_params=pltpu.CompilerParams(has_side_effects=EFFECT),
    )(*[pltpu.with_memory_space_constraint(a, pltpu.HBM) for a in lands], *after)
    return out[:ng], out[ng:2 * ng], out[2 * ng:2 * ng + n], out[2 * ng + n]


def _gather_wait(lands, halved, send_sem, recv_sem, after, name):
    n = len(lands)

    def body(*refs):
        ins, send, recv = refs[:n], refs[n], refs[n + 1]
        x, y, c = _coords()
        k = 2 * x + y
        for j in range(n):
            rows = _core_rows(ins[j], c)
            for m in (1, 2, 3):
                px, py, pk = _chip_peer(x, y, m)
                cp = pltpu.make_async_remote_copy(
                    src_ref=ins[j].at[k, rows, :] if halved else ins[j].at[k],
                    dst_ref=ins[j].at[pk, rows, :] if halved else ins[j].at[pk], send_sem=send.at[3 * j + m - 1],
                    recv_sem=recv.at[3 * j + m - 1], device_id=(px, py, c), device_id_type=MESH)
                cp.wait_send()
                cp.wait_recv()

    return pl.pallas_call(
        body, name=name, out_shape=[pltpu.HBM(a.shape, a.dtype) for a in lands],
        in_specs=[HBM] * n + [SEM, SEM, ANY], out_specs=[HBM] * n,
        input_output_aliases={j: j for j in range(n)},
        compiler_params=pltpu.CompilerParams(has_side_effects=EFFECT),
    )(*lands, send_sem, recv_sem, after)


def _sibling_fill(lands, name):
    n = len(lands)

    def body(*refs):
        ins = refs[:n]
        send_sems, recv_sems = refs[2 * n:]
        x, y, c = _coords()
        sends, recvs = [], []
        for t in range(n):
            for m in (1, 2, 3):
                _, _, pk = _chip_peer(x, y, m)
                for rows, lst in ((_core_rows(ins[t], c), sends), (_core_rows(ins[t], 1 - c), recvs)):
                    lst.append(pltpu.make_async_remote_copy(
                        src_ref=ins[t].at[pk, rows, :], dst_ref=ins[t].at[pk, rows, :],
                        send_sem=send_sems.at[3 * t + m - 1], recv_sem=recv_sems.at[3 * t + m - 1],
                        device_id=(x, y, 1 - c), device_id_type=MESH))
        for cp in sends:
            cp.start()
        for cp in recvs:
            cp.wait_recv()
        for cp in sends:
            cp.wait_send()

    return pl.pallas_call(
        body, name=name, in_specs=[ANY] * n, out_specs=[ANY] * n,
        out_shape=[jax.ShapeDtypeStruct(a.shape, a.dtype) for a in lands],
        input_output_aliases={t: t for t in range(n)},
        scratch_shapes=[pltpu.SemaphoreType.DMA((3 * n,)), pltpu.SemaphoreType.DMA((3 * n,))],
    )(*lands)


def _scatter_start(srcs, name):
    n = len(srcs)

    def body(*refs):
        ins, lands = refs[:n], refs[n:2 * n]
        send, recv = refs[2 * n], refs[2 * n + 1]
        token = refs[2 * n + 2 + 2 * n]
        x, y, c = _coords()
        k = 2 * x + y
        for t in range(n):
            for m in (1, 2, 3):
                px, py, pk = _chip_peer(x, y, m)
                pltpu.make_async_remote_copy(
                    src_ref=ins[t].at[pk], dst_ref=lands[t].at[k], send_sem=send.at[3 * t + m - 1],
                    recv_sem=recv.at[3 * t + m - 1], device_id=(px, py, c), device_id_type=MESH).start()
        token[...] = jnp.zeros_like(token)

    sem = pltpu.SemaphoreType.DMA((3 * n,))
    hbm = [pltpu.HBM(a.shape, a.dtype) for a in srcs]
    operands = list(srcs) + [lax.empty(a.shape, a.dtype) for a in srcs]
    out = pl.pallas_call(
        body, name=name, out_shape=[sem, sem] + hbm + hbm + [jax.ShapeDtypeStruct((8, 128), F32)],
        in_specs=[HBM] * (2 * n), out_specs=[SEM, SEM] + [HBM] * (2 * n) + [pl.BlockSpec(memory_space=pltpu.VMEM)],
        input_output_aliases={t: 2 + t for t in range(2 * n)},
        compiler_params=pltpu.CompilerParams(has_side_effects=EFFECT),
    )(*[pltpu.with_memory_space_constraint(a, pltpu.HBM) for a in operands])
    return out[0], out[1], out[2:2 + n], out[2 + n:2 + 2 * n], out[2 + 2 * n]


def _scatter_wait(srcs, lands, send_sem, recv_sem, after, name):
    n = len(srcs)

    def body(*refs):
        ins, land = refs[:n], refs[n:2 * n]
        send, recv = refs[2 * n], refs[2 * n + 1]
        x, y, c = _coords()
        for t in range(n):
            for m in (1, 2, 3):
                px, py, pk = _chip_peer(x, y, m)
                cp = pltpu.make_async_remote_copy(
                    src_ref=ins[t].at[pk], dst_ref=land[t].at[pk], send_sem=send.at[3 * t + m - 1],
                    recv_sem=recv.at[3 * t + m - 1], device_id=(px, py, c), device_id_type=MESH)
                cp.wait_send()
                cp.wait_recv()

    hbm = [pltpu.HBM(a.shape, a.dtype) for a in srcs]
    out = pl.pallas_call(
        body, name=name, out_shape=hbm + hbm, in_specs=[HBM] * (2 * n) + [SEM, SEM, ANY], out_specs=[HBM] * (2 * n),
        input_output_aliases={t: t for t in range(2 * n)},
        compiler_params=pltpu.CompilerParams(has_side_effects=EFFECT),
    )(*srcs, *lands, send_sem, recv_sem, after)
    return out[:n], out[n:]


def _sibling_swap(hs, after, name):
    n = len(hs)

    def body(*refs):
        ins, outs = refs[:n], refs[n + 1:2 * n + 1]
        send_sems, recv_sems = refs[2 * n + 1:]
        x, y, c = _coords()
        copies = [pltpu.make_async_remote_copy(
            src_ref=ins[t], dst_ref=outs[t], send_sem=send_sems.at[t], recv_sem=recv_sems.at[t],
            device_id=(x, y, 1 - c), device_id_type=MESH) for t in range(n)]
        for cp in copies:
            cp.start()
        for cp in copies:
            cp.wait_recv()
        for cp in copies:
            cp.wait_send()

    return pl.pallas_call(
        body, name=name, in_specs=[ANY] * (n + 1), out_specs=[ANY] * n,
        out_shape=[jax.ShapeDtypeStruct(a.shape, a.dtype) for a in hs],
        scratch_shapes=[pltpu.SemaphoreType.DMA((n,)), pltpu.SemaphoreType.DMA((n,))],
    )(*hs, after)


def _sibling_join_halves(fs, name):
    n = len(fs)

    def body(*refs):
        ins = refs[:n]
        send_sems, recv_sems = refs[2 * n:]
        x, y, c = _coords()
        sends, recvs = [], []
        for t in range(n):
            half = ins[t].shape[0] // 2
            mine = pl.ds(pl.multiple_of(c * half, 8), half)
            theirs = pl.ds(pl.multiple_of((1 - c) * half, 8), half)
            for rows, lst in ((mine, sends), (theirs, recvs)):
                lst.append(pltpu.make_async_remote_copy(
                    src_ref=ins[t].at[rows, :], dst_ref=ins[t].at[rows, :], send_sem=send_sems.at[t],
                    recv_sem=recv_sems.at[t], device_id=(x, y, 1 - c), device_id_type=MESH))
        for cp in sends:
            cp.start()
        for cp in recvs:
            cp.wait_recv()
        for cp in sends:
            cp.wait_send()

    return pl.pallas_call(
        body, name=name, in_specs=[ANY] * n, out_specs=[ANY] * n,
        out_shape=[jax.ShapeDtypeStruct(a.shape, a.dtype) for a in fs],
        input_output_aliases={t: t for t in range(n)},
        scratch_shapes=[pltpu.SemaphoreType.DMA((n,)), pltpu.SemaphoreType.DMA((n,))],
    )(*fs)


def _sum_own_half(g, ra, kc, name):
    _, R, C = g.shape
    half = R // 2
    br = _row_block(half, C)
    nb = half // br

    def body(kc_ref, g_ref, ra_ref, o_ref):
        o_ref[...] = (g_ref[...] + ra_ref[...].astype(F32)).astype(BF16)

    return pl.pallas_call(
        body, name=name,
        grid_spec=pltpu.PrefetchScalarGridSpec(
            num_scalar_prefetch=1, grid=(NCHIP, nb),
            in_specs=[pl.BlockSpec((None, br, C), lambda j, i, kc: (j, kc[1] * nb + i, 0)),
                      pl.BlockSpec((None, br, C), lambda j, i, kc: (j, i, 0))],
            out_specs=pl.BlockSpec((None, br, C), lambda j, i, kc: (j, i, 0))),
        out_shape=jax.ShapeDtypeStruct((NCHIP, half, C), BF16),
        compiler_params=_params("parallel", "parallel"),
    )(kc, g, ra)


def _sum_chips(sa, rb, kc, name, after=None):
    _, half, C = rb.shape
    br = _row_block(half, C)
    nb = half // br
    extra = [] if after is None else [after]

    def body(kc_ref, own_ref, r1_ref, r2_ref, r3_ref, *rest):
        acc = own_ref[...].astype(F32) + r1_ref[...].astype(F32)
        rest[-1][...] = (acc + r2_ref[...].astype(F32)) + r3_ref[...].astype(F32)

    def slab(m):
        return pl.BlockSpec((None, br, C), lambda i, kc: (kc[0] ^ m, i, 0))

    return pl.pallas_call(
        body, name=name,
        grid_spec=pltpu.PrefetchScalarGridSpec(
            num_scalar_prefetch=1, grid=(nb,),
            in_specs=[slab(0), slab(1), slab(2), slab(3)] + [ANY] * len(extra),
            out_specs=pl.BlockSpec((br, C), lambda i, kc: (kc[1] * nb + i, 0))),
        out_shape=jax.ShapeDtypeStruct((2 * half, C), F32), compiler_params=_params("parallel"),
    )(kc, sa, rb, rb, rb, *extra)


def _sum8(ga, name):
    _, R, C = ga.shape

    def body(g_ref, o_ref):
        acc = g_ref[0]
        for j in range(1, 8):
            acc = acc + g_ref[j]
        o_ref[...] = acc

    return pl.pallas_call(
        body, name=name, in_specs=[pl.BlockSpec((8, R, C), lambda: (0, 0, 0))],
        out_specs=pl.BlockSpec((R, C), lambda: (0, 0)), out_shape=jax.ShapeDtypeStruct((R, C), F32),
    )(ga)


ADA_COLS = 9 * D // NCHIP
ADA_BLK = 256


def _ada_mod(c_all, ada_w, ada_b, kidx):
    def body(k_ref, c_ref, w_ref, b_ref, o_ref):
        cv = c_ref[...]
        cs = cv * _sigmoid(cv)
        o_ref[...] = jnp.dot(cs, w_ref[...], precision=lax.Precision.HIGHEST,
                             preferred_element_type=F32) + b_ref[...]

    nblk = ADA_COLS // ADA_BLK
    return pl.pallas_call(
        body, name="ada_mod",
        grid_spec=pltpu.PrefetchScalarGridSpec(
            num_scalar_prefetch=1, grid=(nblk,),
            in_specs=[pl.BlockSpec((8, D), lambda j, k: (0, 0)),
                      pl.BlockSpec((D, ADA_BLK), lambda j, k: (0, j)),
                      pl.BlockSpec((1, ADA_BLK), lambda j, k: (0, k[0] * nblk + j))],
            out_specs=pl.BlockSpec((8, ADA_BLK), lambda j, k: (0, j))),
        out_shape=jax.ShapeDtypeStruct((8, ADA_COLS), F32),
        compiler_params=_params("parallel"),
    )(kidx, c_all, ada_w, ada_b)


def _ada_grad(c_all_t, dmod_all, kidx):
    def body(k_ref, ct_ref, dm_ref, o_ref):
        cv = ct_ref[...]
        cs = cv * _sigmoid(cv)
        acc = cs[:, 0:1] * dm_ref[0:1, :]
        for b in range(1, 8):
            acc = acc + cs[:, b:b + 1] * dm_ref[b:b + 1, :]
        o_ref[...] = acc

    nblk = ADA_COLS // ADA_BLK
    return pl.pallas_call(
        body, name="ada_grad",
        grid_spec=pltpu.PrefetchScalarGridSpec(
            num_scalar_prefetch=1, grid=(nblk,),
            in_specs=[pl.BlockSpec((D, 8), lambda j, k: (0, 0)),
                      pl.BlockSpec((8, ADA_BLK), lambda j, k: (0, k[0] * nblk + j))],
            out_specs=pl.BlockSpec((D, ADA_BLK), lambda j, k: (0, j))),
        out_shape=jax.ShapeDtypeStruct((D, ADA_COLS), F32),
        compiler_params=_params("parallel"),
    )(kidx, c_all_t, dmod_all)


BIG = ("ffn1_w_in", "ffn1_w_out", "mix_w_in", "hgrn_w_o", "conv_w_o", "mix_w_out", "ffn2_w_in", "ffn2_w_out")
ROW_SHARDED = ("ffn1_w_out", "hgrn_w_o", "conv_w_o", "mix_w_out", "ffn2_w_out")
GATHER_GROUPS = ((0, 1), (2,), (3, 4, 5), (6, 7))
GATHER_HALVED = (True, True, False, False)
PACK_LEN = {"ada_b": 9, "hgrn_lb": 2}
WEIGHTS = ("ada_w", "ada_b", "norm_ffn1", "ffn1_w_in", "ffn1_w_out", "norm_mix", "mix_w_in", "hgrn_lb", "hgrn_g",
           "hgrn_w_o", "conv_w", "conv_b", "conv_ln_g", "conv_ln_b", "conv_w_o", "mix_w_out", "norm_ffn2",
           "ffn2_w_in", "ffn2_w_out", "norm_final")
PACKED = ("ada_b", "norm_ffn1", "norm_mix", "hgrn_g", "conv_b", "conv_ln_g", "conv_ln_b", "norm_ffn2",
          "norm_final", "hgrn_lb")


def _pack_params(p, name):
    parts = [(p[n].reshape(PACK_LEN.get(n, 1), D), 0, PACK_LEN.get(n, 1), PACK_AT[n]) for n in PACKED]
    return _pack_rows(parts, PACK_ROWS, name)


def _step(w, m, v, x, c, tgt):
    xi, yi, ci = _coords()
    kidx = (2 * xi + yi).astype(jnp.int32).reshape(1)
    kc = jnp.stack([2 * xi + yi, ci]).astype(jnp.int32)
    me = 4 * xi + 2 * yi + ci

    c_all = _allgather8(jnp.broadcast_to(c, (8, D)), "gather_c")[:, 0, :]
    mod_cols = _ada_mod(c_all, w["ada_w"][0], w["ada_b"], kidx)
    mod_all = _allgather8(mod_cols, "gather_mod")
    mod = lax.dynamic_slice(mod_all, (0, me, 0), (8, 1, ADA_COLS))[::2].reshape(9, D)
    small = {n: w[n].reshape(-1, D) for n in ("norm_ffn1", "norm_mix", "hgrn_lb", "hgrn_g", "conv_b", "conv_ln_g",
                                              "conv_ln_b", "norm_ffn2", "norm_final")}
    small["conv_w"] = _allgather_conv_w(w["conv_w"][0])

    lands = [_cast_into_slot(w[n][0], kc, "cast_" + n) for n in BIG]
    sends, recvs, lands, _ = _gather_start(lands, GATHER_GROUPS, GATHER_HALVED, [mod, small["conv_w"]])
    ready = {}

    def weight(name, after):
        t = BIG.index(name)
        if t not in ready:
            gi = [t in grp for grp in GATHER_GROUPS].index(True)
            grp = GATHER_GROUPS[gi]
            outs = _gather_wait([lands[j] for j in grp], GATHER_HALVED[gi], sends[gi], recvs[gi], after,
                                "gather_weights_wait%d" % gi)
            if GATHER_HALVED[gi]:
                outs = _sibling_fill(outs, "gather_weights_fill%d" % gi)
            ready.update(zip(grp, outs))
        return ready[t].reshape(-1, D) if name in ROW_SHARDED else ready[t]

    grads, delta, new_m, new_v = {}, {}, {}, {}
    flight = []
    landed = []

    def settle(after):
        names, sa, rb, send, recv = flight.pop()
        sa, rb = _scatter_wait(sa, rb, send, recv, after, "rs_chip_wait_" + names[0])
        landed.append((names, sa, rb))

    def reduce(names, pairs, after=None):
        gs = [g.reshape(NCHIP, -1, g.shape[-1]) for g, _ in pairs]
        hs = [h.reshape(NCHIP, -1, h.shape[-1]) for _, h in pairs]
        ra = _sibling_swap(hs, hs[0] if after is None else after, "rs_sibling_halves_" + names[0])
        sa = [_sum_own_half(g, r, kc, "rs_sum_pair_" + n) for g, r, n in zip(gs, ra, names)]
        if flight:
            settle(sa[0])
        send, recv, sa, rb, tok = _scatter_start(sa, "rs_chip_start_" + names[0])
        flight.append((names, sa, rb, send, recv))
        started.append(tok)
        return tok

    def adamw(n, after=None):
        shape = w[n].shape
        two = (shape[-2], shape[-1])
        d_, m_, v_ = _adamw(w[n].reshape(two), grads[n], m[n].reshape(two), v[n].reshape(two), "adamw_" + n, after)
        grads[n], delta[n], new_m[n], new_v[n] = (a.reshape(shape) for a in (grads[n], d_, m_, v_))
        return m_

    def finish(after=None):
        names, sa, rb = landed.pop(0)
        fin = [_sum_chips(s, r, kc, "rs_sum_chips_" + n, after) for s, r, n in zip(sa, rb, names)]
        full = _sibling_join_halves(fin, "rs_join_halves_" + names[0])
        grads.update(zip(names, full))
        return [adamw(n) for n in names][-1]

    started = []

    smalls = []

    def reduce_small(packed):
        packed_all = _allgather8(packed, "gather_small_grads")
        smalls.extend([packed_all, _sum8(packed_all, "sum_small_grads")])
        return smalls[1]

    dx = _local_step(x[0], tgt[0], mod, small, kc, weight, reduce, reduce_small)
    packed_all, gsum = smalls
    loss = (0.5 / D) * jnp.sum(gsum[PACK_AT["loss"]])
    dmod_all = packed_all[:, 0:9, :].reshape(8, 9 * D)
    grads["ada_w"] = _ada_grad(c_all.T, dmod_all, kidx)
    grads["conv_w"] = lax.dynamic_slice(gsum, (PACK_AT["conv_w"], kidx[0] * (D // NCHIP)), (CONV_K, D // NCHIP))

    tok = started[-1]
    adamw("ada_w", tok)
    adamw("conv_w")
    pw, pm, pv = (_pack_params(p, "pack_" + s) for p, s in ((w, "w"), (m, "m"), (v, "v")))
    pd, pnm, pnv = _adamw(pw, gsum, pm, pv, "adamw_small", tok)
    last = pnv
    while landed:
        last = finish(tok)
    settle(last)
    finish()
    for n in PACKED:
        rows = slice(PACK_AT[n], PACK_AT[n] + PACK_LEN.get(n, 1))
        for dst, src in ((grads, gsum), (delta, pd), (new_m, pnm), (new_v, pnv)):
            dst[n] = src[rows].reshape(w[n].shape)

    outs = [loss, dx[None]]
    for d in (grads, delta, new_m, new_v):
        outs += [d[n] for n in WEIGHTS]
    return tuple(outs)


def _allgather_conv_w(cw):
    padded = jnp.pad(cw, ((0, 32 - CONV_K), (0, 0)))
    parts = _allgather8(padded, "gather_conv_w")
    return jnp.concatenate([parts[2 * j] for j in range(NCHIP)], axis=1)


def kernel(x, c, ada_w, ada_b, norm_ffn1, ffn1_w_in, ffn1_w_out, norm_mix, mix_w_in, hgrn_lb, hgrn_g, hgrn_w_o, conv_w, conv_b, conv_ln_g, conv_ln_b, conv_w_o, mix_w_out, norm_ffn2, ffn2_w_in, ffn2_w_out, norm_final, loss_target, m_ada_w, m_ada_b, m_norm_ffn1, m_ffn1_w_in, m_ffn1_w_out, m_norm_mix, m_mix_w_in, m_hgrn_lb, m_hgrn_g, m_hgrn_w_o, m_conv_w, m_conv_b, m_conv_ln_g, m_conv_ln_b, m_conv_w_o, m_mix_w_out, m_norm_ffn2, m_ffn2_w_in, m_ffn2_w_out, m_norm_final, v_ada_w, v_ada_b, v_norm_ffn1, v_ffn1_w_in, v_ffn1_w_out, v_norm_mix, v_mix_w_in, v_hgrn_lb, v_hgrn_g, v_hgrn_w_o, v_conv_w, v_conv_b, v_conv_ln_g, v_conv_ln_b, v_conv_w_o, v_mix_w_out, v_norm_ffn2, v_ffn2_w_in, v_ffn2_w_out, v_norm_final):
    w = dict(ada_w=ada_w, ada_b=ada_b, norm_ffn1=norm_ffn1, ffn1_w_in=ffn1_w_in, ffn1_w_out=ffn1_w_out,
             norm_mix=norm_mix, mix_w_in=mix_w_in, hgrn_lb=hgrn_lb, hgrn_g=hgrn_g, hgrn_w_o=hgrn_w_o, conv_w=conv_w,
             conv_b=conv_b, conv_ln_g=conv_ln_g, conv_ln_b=conv_ln_b, conv_w_o=conv_w_o, mix_w_out=mix_w_out,
             norm_ffn2=norm_ffn2, ffn2_w_in=ffn2_w_in, ffn2_w_out=ffn2_w_out, norm_final=norm_final)
    m = dict(ada_w=m_ada_w, ada_b=m_ada_b, norm_ffn1=m_norm_ffn1, ffn1_w_in=m_ffn1_w_in, ffn1_w_out=m_ffn1_w_out,
             norm_mix=m_norm_mix, mix_w_in=m_mix_w_in, hgrn_lb=m_hgrn_lb, hgrn_g=m_hgrn_g, hgrn_w_o=m_hgrn_w_o,
             conv_w=m_conv_w, conv_b=m_conv_b, conv_ln_g=m_conv_ln_g, conv_ln_b=m_conv_ln_b, conv_w_o=m_conv_w_o,
             mix_w_out=m_mix_w_out, norm_ffn2=m_norm_ffn2, ffn2_w_in=m_ffn2_w_in, ffn2_w_out=m_ffn2_w_out,
             norm_final=m_norm_final)
    v = dict(ada_w=v_ada_w, ada_b=v_ada_b, norm_ffn1=v_norm_ffn1, ffn1_w_in=v_ffn1_w_in, ffn1_w_out=v_ffn1_w_out,
             norm_mix=v_norm_mix, mix_w_in=v_mix_w_in, hgrn_lb=v_hgrn_lb, hgrn_g=v_hgrn_g, hgrn_w_o=v_hgrn_w_o,
             conv_w=v_conv_w, conv_b=v_conv_b, conv_ln_g=v_conv_ln_g, conv_ln_b=v_conv_ln_b, conv_w_o=v_conv_w_o,
             mix_w_out=v_mix_w_out, norm_ffn2=v_norm_ffn2, ffn2_w_in=v_ffn2_w_in, ffn2_w_out=v_ffn2_w_out,
             norm_final=v_norm_final)
    return _step(w, m, v, x, c, loss_target)
```

```python
import functools

import jax
import jax.numpy as jnp
from jax import lax
from jax.experimental import pallas as pl
from jax.experimental.pallas import tpu as pltpu

F32 = jnp.float32
BF16 = jnp.bfloat16

D = 1024
DFF = 2816
NCHIP = 4
FSH = 2 * DFF // NCHIP
HEADS = 8
DK = 128
CHUNK = 64
CONV_K = 31
HALO = 32
EPS = 1e-6
TB = 256
CB = 512
DW_TOKENS = 2048
VMEM_LIMIT = 56 * 1024 * 1024

ADAM_LR = 0.001
ADAM_B1 = 0.9
ADAM_B2 = 0.999
ADAM_EPS = 1e-08
ADAM_WD = 0.01
ADAM_STEP = 10

MESH = pl.DeviceIdType.MESH
ANY = pl.BlockSpec(memory_space=pl.ANY)


def _params(*sem):
    return pltpu.CompilerParams(dimension_semantics=sem, vmem_limit_bytes=VMEM_LIMIT)


def _sigmoid(x):
    return 0.5 * jnp.tanh(0.5 * x) + 0.5


def _dsilu(x, sg):
    return sg * (1.0 + x * (1.0 - sg))


def _nt(a, b):
    return lax.dot_general(a, b, (((1,), (1,)), ((), ())), preferred_element_type=F32)


def _tn(a, b):
    return lax.dot_general(a, b, (((0,), (0,)), ((), ())), preferred_element_type=F32)


def _nn(a, b):
    return jnp.dot(a, b, preferred_element_type=F32)


def _colsum(x):
    return jnp.sum(x, axis=0, keepdims=True)


def _rms_fwd(x, gn, sc, sh):
    r = lax.rsqrt(jnp.mean(x * x, axis=-1, keepdims=True) + EPS)
    n = x * r
    h = (n * gn) * (1.0 + sc) + sh
    return r, n, h


def _rms_bwd(dh, r, n, gn, sc, acc_ref):
    acc_ref[0:1, :] += _colsum(dh)
    acc_ref[1:2, :] += _colsum(dh * (n * gn))
    dng = dh * (1.0 + sc)
    acc_ref[3:4, :] += _colsum(dng * n)
    dn = dng * gn
    return r * (dn - n * jnp.mean(dn * n, axis=-1, keepdims=True))


def _ffn_fwd(x, vec, w_in, w_out, name):
    T = x.shape[0]

    def body(x_ref, vec_ref, win_hbm, wout_hbm, xo_ref, h_ref, a_ref, b_ref, s_ref, f_ref, win, wout):
        @pl.when(pl.program_id(0) == 0)
        def _():
            pltpu.sync_copy(win_hbm, win)
            pltpu.sync_copy(wout_hbm, wout)

        x = x_ref[...]
        sh, sc, gate, gn = vec_ref[0:1, :], vec_ref[1:2, :], vec_ref[2:3, :], vec_ref[3:4, :]
        _, _, h = _rms_fwd(x, gn, sc, sh)
        hb = h.astype(BF16)
        h_ref[...] = hb
        f = jnp.zeros((TB, D), F32)
        for j in range(2):
            cols = slice(j * FSH, (j + 1) * FSH)
            a = _nn(hb, win[j])
            b = _nn(hb, win[2 + j])
            s = (a * _sigmoid(a) * b).astype(BF16)
            a_ref[:, cols] = a.astype(BF16)
            b_ref[:, cols] = b.astype(BF16)
            s_ref[:, cols] = s
            f = f + _nn(s, wout[cols, :])
        xo_ref[...] = x + (0.5 * gate) * f
        f_ref[...] = f.astype(BF16)

    row = lambda w: pl.BlockSpec((TB, w), lambda i: (i, 0))
    return pl.pallas_call(
        body, name=name, grid=(T // TB,),
        in_specs=[row(D), pl.BlockSpec((8, D), lambda i: (0, 0)), ANY, ANY],
        out_specs=[row(D), row(D), row(DFF), row(DFF), row(DFF), row(D)],
        out_shape=[jax.ShapeDtypeStruct((T, D), F32), jax.ShapeDtypeStruct((T, D), BF16),
                   jax.ShapeDtypeStruct((T, DFF), BF16), jax.ShapeDtypeStruct((T, DFF), BF16),
                   jax.ShapeDtypeStruct((T, DFF), BF16), jax.ShapeDtypeStruct((T, D), BF16)],
        scratch_shapes=[pltpu.VMEM((NCHIP, D, FSH), BF16), pltpu.VMEM((DFF, D), BF16)],
        compiler_params=_params("arbitrary"),
    )(x, vec, w_in, w_out)


def _ffn_bwd(dxo, x, vec, a, b, f, w_in, w_out, name):
    T = x.shape[0]

    def body(dxo_ref, x_ref, vec_ref, a_ref, b_ref, f_ref, win_hbm, wout_hbm,
             dx_ref, df_ref, dab_ref, acc_ref, win, wout):
        @pl.when(pl.program_id(0) == 0)
        def _():
            pltpu.sync_copy(win_hbm, win)
            pltpu.sync_copy(wout_hbm, wout)
            acc_ref[...] = jnp.zeros_like(acc_ref)

        dxo = dxo_ref[...]
        x = x_ref[...]
        sh, sc, gate, gn = vec_ref[0:1, :], vec_ref[1:2, :], vec_ref[2:3, :], vec_ref[3:4, :]
        r, n, _ = _rms_fwd(x, gn, sc, sh)
        acc_ref[2:3, :] += _colsum(0.5 * f_ref[...].astype(F32) * dxo)
        dfb = ((0.5 * gate) * dxo).astype(BF16)
        df_ref[...] = dfb
        dh = jnp.zeros((TB, D), F32)
        for j in range(2):
            cols = slice(j * FSH, (j + 1) * FSH)
            ds = _nt(dfb, wout[cols, :])
            av = a_ref[:, cols].astype(F32)
            bv = b_ref[:, cols].astype(F32)
            sg = _sigmoid(av)
            da = (ds * bv * _dsilu(av, sg)).astype(BF16)
            db = (ds * (av * sg)).astype(BF16)
            dab_ref[j] = da
            dab_ref[2 + j] = db
            dh = dh + _nt(da, win[j]) + _nt(db, win[2 + j])
        dx_ref[...] = dxo + _rms_bwd(dh, r, n, gn, sc, acc_ref)

    row = lambda w: pl.BlockSpec((TB, w), lambda i: (i, 0))
    vec8 = pl.BlockSpec((8, D), lambda i: (0, 0))
    return pl.pallas_call(
        body, name=name, grid=(T // TB,),
        in_specs=[row(D), row(D), vec8, row(DFF), row(DFF), row(D), ANY, ANY],
        out_specs=[row(D), pl.BlockSpec((None, TB, D), lambda i: (0, i, 0)),
                   pl.BlockSpec((NCHIP, TB, FSH), lambda i: (0, i, 0)), vec8],
        out_shape=[jax.ShapeDtypeStruct((T, D), F32), jax.ShapeDtypeStruct((1, T, D), BF16),
                   jax.ShapeDtypeStruct((NCHIP, T, FSH), BF16), jax.ShapeDtypeStruct((8, D), F32)],
        scratch_shapes=[pltpu.VMEM((NCHIP, D, FSH), BF16), pltpu.VMEM((DFF, D), BF16)],
        compiler_params=_params("arbitrary"),
    )(dxo, x, vec, a, b, f, w_in, w_out)


def _mm_tn(a, b3, hp, kc, shard_rows, name, into=None, slab=0, slabs=None):
    T, M = a.shape
    P, _, N = b3.shape
    tm = M if M <= 1408 else M // 2
    tk = min(T, DW_TOKENS)
    nk = T // tk
    slabs = P // hp if slabs is None else slabs
    half = shard_rows // 2
    extra = [] if into is None else list(into)

    def body(kc_ref, a_ref, b_ref, *rest):
        o_ref, h_ref = rest[-2:]

        @pl.when(pl.program_id(2) == 0)
        def _():
            o_ref[...] = jnp.zeros_like(o_ref)

        o_ref[...] += _tn(a_ref[...], b_ref[...])

        @pl.when(pl.program_id(2) == nk - 1)
        def _():
            for j in range(tm // shard_rows):
                start = pl.multiple_of(j * shard_rows + (1 - kc_ref[1]) * half, 8)
                h_ref[j * half:(j + 1) * half, :] = o_ref[pl.ds(start, half), :].astype(BF16)

    return pl.pallas_call(
        body, name=name,
        grid_spec=pltpu.PrefetchScalarGridSpec(
            num_scalar_prefetch=1, grid=(P, M // tm, nk),
            in_specs=[pl.BlockSpec((tk, tm), lambda p, i, k, kc: (k, i)),
                      pl.BlockSpec((None, tk, N), lambda p, i, k, kc: (p, k, 0))] + [ANY] * len(extra),
            out_specs=[pl.BlockSpec((None, tm, N), lambda p, i, k, kc: (slab + p // hp, i, p % hp)),
                       pl.BlockSpec((None, tm // 2, N), lambda p, i, k, kc: (slab + p // hp, i, p % hp))]),
        out_shape=[jax.ShapeDtypeStruct((slabs, M, hp * N), F32), jax.ShapeDtypeStruct((slabs, M // 2, hp * N), BF16)],
        input_output_aliases={} if into is None else {3: 0, 4: 1},
        compiler_params=_params("parallel", "parallel", "arbitrary"),
    )(kc, a, b3, *extra)


def _mix_proj_fwd(x, vec, w_in):
    T = x.shape[0]

    def body(x_ref, vec_ref, w_hbm, h_ref, qr_ref, g_ref, k_ref, v_ref, og_ref, u_ref, ua_ref, ub_ref,
             sa_ref, sb_ref, w):
        @pl.when(pl.program_id(0) == 0)
        def _():
            pltpu.sync_copy(w_hbm, w)

        x = x_ref[...]
        sh, sc, gn, lb = vec_ref[0:1, :], vec_ref[1:2, :], vec_ref[3:4, :], vec_ref[4:5, :]
        _, _, h = _rms_fwd(x, gn, sc, sh)
        hb = h.astype(BF16)
        h_ref[...] = hb
        p = _nn(hb, w[0])
        qr_ref[...] = p[:, :D].astype(BF16)
        fg = lb + (1.0 - lb) * _sigmoid(p[:, D:])
        g_ref[...] = jnp.log(fg)
        k_ref[...] = (1.0 - fg).astype(BF16)
        p = _nn(hb, w[1])
        v_ref[...] = p[:, :D].astype(BF16)
        og_ref[...] = p[:, D:].astype(BF16)
        p = _nn(hb, w[2])
        ua, ub = p[:, :D], p[:, D:]
        u_ref[...] = ua * _sigmoid(ub)
        ua_ref[...] = ua.astype(BF16)
        ub_ref[...] = ub.astype(BF16)
        p = _nn(hb, w[3])
        sa_ref[...] = _sigmoid(p[:, :D]).astype(BF16)
        sb_ref[...] = _sigmoid(p[:, D:]).astype(BF16)

    row = pl.BlockSpec((TB, D), lambda i: (i, 0))
    bf = jax.ShapeDtypeStruct((T, D), BF16)
    f32 = jax.ShapeDtypeStruct((T, D), F32)
    return pl.pallas_call(
        body, name="mix_proj_fwd", grid=(T // TB,),
        in_specs=[row, pl.BlockSpec((8, D), lambda i: (0, 0)), ANY],
        out_specs=[row] * 11,
        out_shape=[bf, bf, f32, bf, bf, bf, f32, bf, bf, bf, bf],
        scratch_shapes=[pltpu.VMEM((NCHIP, D, 2 * D), BF16)],
        compiler_params=_params("arbitrary"),
    )(x, vec, w_in)


def _mix_proj_bwd(dxo, x, vec, dpa, dpb, dpc, w_in):
    T = x.shape[0]

    def body(dxo_ref, x_ref, vec_ref, dpa_ref, dpb_ref, dpc_ref, w_hbm, dx_ref, acc_ref, w):
        @pl.when(pl.program_id(0) == 0)
        def _():
            pltpu.sync_copy(w_hbm, w)
            acc_ref[...] = jnp.zeros_like(acc_ref)

        x = x_ref[...]
        sh, sc, gn = vec_ref[0:1, :], vec_ref[1:2, :], vec_ref[3:4, :]
        r, n, _ = _rms_fwd(x, gn, sc, sh)
        dh = jnp.zeros((TB, D), F32)
        for p in range(8):
            src = dpa_ref[p] if p < 4 else (dpb_ref[p - 4] if p < 6 else dpc_ref[p - 6])
            dh = dh + _nt(src, w[p // 2, :, (p % 2) * D:(p % 2 + 1) * D])
        dx_ref[...] = dxo_ref[...] + _rms_bwd(dh, r, n, gn, sc, acc_ref)

    row = pl.BlockSpec((TB, D), lambda i: (i, 0))
    vec8 = pl.BlockSpec((8, D), lambda i: (0, 0))
    stack = lambda k: pl.BlockSpec((k, TB, D), lambda i: (0, i, 0))
    return pl.pallas_call(
        body, name="mix_proj_bwd", grid=(T // TB,),
        in_specs=[row, row, vec8, stack(4), stack(2), stack(2), ANY],
        out_specs=[row, vec8],
        out_shape=[jax.ShapeDtypeStruct((T, D), F32), jax.ShapeDtypeStruct((8, D), F32)],
        scratch_shapes=[pltpu.VMEM((NCHIP, D, 2 * D), BF16)],
        compiler_params=_params("arbitrary"),
    )(dxo, x, vec, dpa, dpb, dpc, w_in)


def _tri(lower):
    r = lax.broadcasted_iota(jnp.int32, (CHUNK, CHUNK), 0)
    c = lax.broadcasted_iota(jnp.int32, (CHUNK, CHUNK), 1)
    return (c <= r) if lower else (c >= r)


def _cumsum_rows(mask, g):
    hi = g.astype(BF16)
    rest = g - hi.astype(F32)
    mid = rest.astype(BF16)
    low = (rest - mid.astype(F32)).astype(BF16)
    n = g.shape[1]
    p = _nn(mask.astype(BF16), jnp.concatenate([hi, mid, low], axis=1))
    return (p[:, 2 * n:] + p[:, n:2 * n]) + p[:, :n]


def _chunk_decay(low, g, nck):
    bs, mids, lasts = [], [], []
    for c in range(nck):
        gc = g[c * CHUNK:(c + 1) * CHUNK]
        bs.append(_cumsum_rows(low, gc))
        mids.append(_colsum(gc[0:CHUNK // 2]))
        lasts.append(_colsum(gc))
    spread = lambda rows: jnp.concatenate([jnp.broadcast_to(r, (CHUNK, DK)) for r in rows], axis=0)
    return jnp.concatenate(bs, axis=0), spread(mids), spread(lasts), lasts


def _hgrn_fwd(qr, g, k, v, og, vec):
    T = qr.shape[0]
    nck = CB // CHUNK

    def body(qr_ref, g_ref, k_ref, v_ref, og_ref, vec_ref, out_ref, o_ref, st_ref, state):
        @pl.when(pl.program_id(1) == 0)
        def _():
            state[...] = jnp.zeros_like(state)

        low = _tri(True)
        qv = qr_ref[...].astype(F32)
        q = qv * _sigmoid(qv) * (DK ** -0.5)
        kk = k_ref[...].astype(F32)
        vb = v_ref[...]
        b, mid, last, lasts = _chunk_decay(low, g_ref[...], nck)
        qt = (q * jnp.exp(b - mid)).astype(BF16)
        kt = (kk * jnp.exp(mid - b)).astype(BF16)
        qe = (q * jnp.exp(b)).astype(BF16)
        kd = (kk * jnp.exp(last - b)).astype(BF16)
        intra, grow = [], []
        for c in range(nck):
            r = slice(c * CHUNK, (c + 1) * CHUNK)
            att = jnp.where(low, _nt(qt[r], kt[r]), 0.0).astype(BF16)
            intra.append(_nn(att, vb[r]))
            grow.append(_tn(vb[r], kd[r]))
        st = state[...]
        inter = []
        for c in range(nck):
            stb = st.astype(BF16)
            st_ref[c] = stb
            inter.append(_nt(qe[c * CHUNK:(c + 1) * CHUNK], stb))
            st = st * jnp.exp(lasts[c]) + grow[c]
        state[...] = st
        o = jnp.concatenate(intra, axis=0) + jnp.concatenate(inter, axis=0)
        o_ref[...] = o
        ogv = og_ref[...].astype(F32)
        rms = lax.rsqrt(jnp.mean(o * o, axis=-1, keepdims=True) + EPS)
        out_ref[...] = (o * rms * vec_ref[5:6, :] * (ogv * _sigmoid(ogv))).astype(BF16)

    blk = pl.BlockSpec((CB, DK), lambda h, i: (i, h))
    return pl.pallas_call(
        body, name="hgrn_fwd", grid=(HEADS, T // CB),
        in_specs=[blk, blk, blk, blk, blk, pl.BlockSpec((8, DK), lambda h, i: (0, h))],
        out_specs=[blk, blk, pl.BlockSpec((None, nck, DK, DK), lambda h, i: (h, i, 0, 0))],
        out_shape=[jax.ShapeDtypeStruct((T, D), BF16), jax.ShapeDtypeStruct((T, D), F32),
                   jax.ShapeDtypeStruct((HEADS, T // CHUNK, DK, DK), BF16)],
        scratch_shapes=[pltpu.VMEM((DK, DK), F32)],
        compiler_params=_params("parallel", "arbitrary"),
    )(qr, g, k, v, og, vec)


def _hgrn_bwd(dout, og, qr, g, k, v, o, st, vec):
    T = qr.shape[0]
    nck = CB // CHUNK
    nb = T // CB

    def body(dout_ref, og_ref, qr_ref, g_ref, k_ref, v_ref, o_ref, st_ref, vec_ref,
             dp_ref, acc_ref, dstate):
        @pl.when(pl.program_id(1) == 0)
        def _():
            dstate[...] = jnp.zeros_like(dstate)
            acc_ref[...] = jnp.zeros_like(acc_ref)

        o = o_ref[...]
        ogv = og_ref[...].astype(F32)
        dout = dout_ref[...].astype(F32)
        hg = vec_ref[5:6, :]
        sgo = _sigmoid(ogv)
        rms = lax.rsqrt(jnp.mean(o * o, axis=-1, keepdims=True) + EPS)
        ohat = o * rms
        dp_ref[3] = (dout * (ohat * hg) * _dsilu(ogv, sgo)).astype(BF16)
        don = dout * (ogv * sgo)
        acc_ref[0:1, :] += _colsum(don * ohat)
        dohat = don * hg
        dob = (rms * (dohat - ohat * jnp.mean(dohat * ohat, axis=-1, keepdims=True))).astype(BF16)

        low = _tri(True)
        upp = _tri(False)
        lb = vec_ref[4:5, :]
        qv = qr_ref[...].astype(F32)
        sgq = _sigmoid(qv)
        q = qv * sgq * (DK ** -0.5)
        kk = k_ref[...].astype(F32)
        vb = v_ref[...]
        gv = g_ref[...]
        b, mid, last, lasts = _chunk_decay(low, gv, nck)
        eq = jnp.exp(b - mid)
        ek = jnp.exp(mid - b)
        eb = jnp.exp(b)
        ed = jnp.exp(last - b)
        qtb, ktb, qeb, kdb = ((t).astype(BF16) for t in (q * eq, kk * ek, q * eb, kk * ed))
        rows = [slice(c * CHUNK, (c + 1) * CHUNK) for c in range(nck)]

        dv1, dqt, dkt, dqe, grow = [], [], [], [], []
        for c, r in enumerate(rows):
            att = jnp.where(low, _nt(qtb[r], ktb[r]), 0.0).astype(BF16)
            datt = jnp.where(low, _nt(dob[r], vb[r]), 0.0).astype(BF16)
            dv1.append(_tn(att, dob[r]))
            dqt.append(_nn(datt, ktb[r]))
            dkt.append(_tn(datt, qtb[r]))
            dqe.append(_nn(dob[r], st_ref[c]))
            grow.append(_tn(dob[r], qeb[r]))
        ds = dstate[...]
        ds1b, dl_state = [None] * nck, [None] * nck
        for c in reversed(range(nck)):
            el = jnp.exp(lasts[c])
            ds1b[c] = ds.astype(BF16)
            dl_state[c] = el * _colsum(ds * st_ref[c].astype(F32))
            ds = ds * el + grow[c]
        dstate[...] = ds
        dkd = jnp.concatenate([_nn(vb[r], ds1b[c]) for c, r in enumerate(rows)], axis=0)
        dv = jnp.concatenate(dv1, axis=0) + jnp.concatenate([_nt(kdb[r], ds1b[c]) for c, r in enumerate(rows)], axis=0)
        dqt, dkt, dqe = (jnp.concatenate(t, axis=0) for t in (dqt, dkt, dqe))
        dq = dqt * eq + dqe * eb
        dk = dkt * ek + dkd * ed
        dkdkd = dkd * kdb.astype(F32)
        db = dqt * qtb.astype(F32) - dkt * ktb.astype(F32) + dqe * qeb.astype(F32) - dkdkd
        dg = jnp.concatenate([_cumsum_rows(upp, db[r]) + (_colsum(dkdkd[r]) + dl_state[c])
                              for c, r in enumerate(rows)], axis=0)
        fg = jnp.exp(gv)
        dfg = dg * jnp.exp(-gv) - dk
        one_m_sig = (1.0 - fg) * (1.0 / (1.0 - lb))
        dp_ref[0] = (dq * (DK ** -0.5) * _dsilu(qv, sgq)).astype(BF16)
        dp_ref[1] = (dfg * (fg - lb) * one_m_sig).astype(BF16)
        dp_ref[2] = dv.astype(BF16)
        dlb = _colsum(dfg * one_m_sig) * (lb * (1.0 - lb))
        acc_ref[1:2, :] += dlb
        acc_ref[2:3, :] -= dlb

    blk = pl.BlockSpec((CB, DK), lambda h, i: (nb - 1 - i, h))
    return pl.pallas_call(
        body, name="hgrn_bwd", grid=(HEADS, nb),
        in_specs=[blk, blk, blk, blk, blk, blk, blk,
                  pl.BlockSpec((None, nck, DK, DK), lambda h, i: (h, nb - 1 - i, 0, 0)),
                  pl.BlockSpec((8, DK), lambda h, i: (0, h))],
        out_specs=[pl.BlockSpec((4, CB, DK), lambda h, i: (0, nb - 1 - i, h)),
                   pl.BlockSpec((8, DK), lambda h, i: (0, h))],
        out_shape=[jax.ShapeDtypeStruct((4, T, D), BF16), jax.ShapeDtypeStruct((8, D), F32)],
        scratch_shapes=[pltpu.VMEM((DK, DK), F32)],
        compiler_params=_params("parallel", "arbitrary"),
    )(dout, og, qr, g, k, v, o, st, vec)


def _ln_fwd(uc, lg, lbias):
    mu = jnp.mean(uc, axis=-1, keepdims=True)
    xc = uc - mu
    rstd = lax.rsqrt(jnp.mean(xc * xc, axis=-1, keepdims=True) + EPS)
    z = xc * rstd
    return rstd, z, z * lg + lbias


LANES = 128
SUBLANES = 8
CONV_ROWS = 128


def _lane_tiles():
    return [slice(l * LANES, (l + 1) * LANES) for l in range(D // LANES)]


def _row_shifts(x):
    n = x.shape[0]
    return [x] + [pltpu.roll(x, n - r, axis=0) for r in range(1, SUBLANES)]


def _shifted_rows(shifted, start, rows=TB):
    a, r = divmod(start, SUBLANES)
    return shifted[r][a * SUBLANES:a * SUBLANES + rows]


def _conv_fwd(u, cw, cvec):
    T = u.shape[0]
    per = TB // HALO

    def body(u_ref, halo_ref, cw_ref, cvec_ref, us_ref, uc_ref, pad):
        i = pl.program_id(0)
        pad[0:HALO, :] = jnp.where(i > 0, halo_ref[...], 0.0)
        pad[HALO:, :] = u_ref[...]
        for lanes in _lane_tiles():
            shifted = _row_shifts(pad[:, lanes])
            acc = jnp.broadcast_to(cvec_ref[0:1, lanes], (TB, LANES))
            for j in range(CONV_K):
                acc = acc + cw_ref[j:j + 1, lanes] * _shifted_rows(shifted, HALO - (CONV_K - 1) + j)
            uc_ref[:, lanes] = acc
        _, _, ul = _ln_fwd(uc_ref[...], cvec_ref[1:2, :], cvec_ref[2:3, :])
        us_ref[...] = (ul * _sigmoid(ul)).astype(BF16)

    row = pl.BlockSpec((TB, D), lambda i: (i, 0))
    return pl.pallas_call(
        body, name="conv_fwd", grid=(T // TB,),
        in_specs=[row, pl.BlockSpec((HALO, D), lambda i: (jnp.maximum(i * per - 1, 0), 0)),
                  pl.BlockSpec((32, D), lambda i: (0, 0)), pl.BlockSpec((8, D), lambda i: (0, 0))],
        out_specs=[row, row],
        out_shape=[jax.ShapeDtypeStruct((T, D), BF16), jax.ShapeDtypeStruct((T, D), F32)],
        scratch_shapes=[pltpu.VMEM((TB + HALO, D), F32)],
        compiler_params=_params("parallel"),
    )(u, u, cw, cvec)


def _conv_bwd_ln(dus, uc, cvec):
    T = uc.shape[0]

    def body(dus_ref, uc_ref, cvec_ref, duc_ref, acc_ref):
        @pl.when(pl.program_id(0) == 0)
        def _():
            acc_ref[...] = jnp.zeros_like(acc_ref)

        lg = cvec_ref[1:2, :]
        rstd, z, ul = _ln_fwd(uc_ref[...], lg, cvec_ref[2:3, :])
        dul = dus_ref[...].astype(F32) * _dsilu(ul, _sigmoid(ul))
        acc_ref[1:2, :] += _colsum(dul * z)
        acc_ref[2:3, :] += _colsum(dul)
        dz = dul * lg
        duc = rstd * (dz - jnp.mean(dz, axis=-1, keepdims=True) - z * jnp.mean(dz * z, axis=-1, keepdims=True))
        acc_ref[0:1, :] += _colsum(duc)
        duc_ref[...] = duc

    row = pl.BlockSpec((TB, D), lambda i: (i, 0))
    vec8 = pl.BlockSpec((8, D), lambda i: (0, 0))
    return pl.pallas_call(
        body, name="conv_bwd_ln", grid=(T // TB,),
        in_specs=[row, row, vec8], out_specs=[row, vec8],
        out_shape=[jax.ShapeDtypeStruct((T, D), F32), jax.ShapeDtypeStruct((8, D), F32)],
        compiler_params=_params("arbitrary"),
    )(dus, uc, cvec)


def _conv_bwd_taps(duc, u, ua, ub, cw):
    T = u.shape[0]
    per = TB // HALO
    nblk = T // TB

    def body(duc_ref, dnext_ref, u_ref, uprev_ref, ua_ref, ub_ref, cw_ref, dp_ref, dcw_ref, upad, dpad, dcw):
        i = pl.program_id(0)

        @pl.when(i == 0)
        def _():
            dcw[...] = jnp.zeros_like(dcw)

        upad[0:HALO, :] = jnp.where(i > 0, uprev_ref[...], 0.0)
        upad[HALO:, :] = u_ref[...]
        dpad[0:TB, :] = duc_ref[...]
        dpad[TB:, :] = jnp.where(i < nblk - 1, dnext_ref[...], 0.0)
        for lanes in _lane_tiles():
            ushift = _row_shifts(upad[:, lanes])
            dshift = _row_shifts(dpad[:, lanes])
            for r0 in range(0, TB, CONV_ROWS):
                rows = slice(r0, r0 + CONV_ROWS)
                duc = duc_ref[rows, lanes]
                du = jnp.zeros((CONV_ROWS, LANES), F32)
                for j in range(CONV_K):
                    prod = duc * _shifted_rows(ushift, r0 + HALO - (CONV_K - 1) + j, CONV_ROWS)
                    dcw[j, :, lanes] += jnp.sum(prod.reshape(CONV_ROWS // SUBLANES, SUBLANES, LANES), axis=0)
                    du = du + cw_ref[j:j + 1, lanes] * _shifted_rows(dshift, r0 + CONV_K - 1 - j, CONV_ROWS)
                ua = ua_ref[rows, lanes].astype(F32)
                sg = _sigmoid(ub_ref[rows, lanes].astype(F32))
                dp_ref[0, rows, lanes] = (du * sg).astype(BF16)
                dp_ref[1, rows, lanes] = (du * ua * sg * (1.0 - sg)).astype(BF16)

        @pl.when(i == nblk - 1)
        def _():
            dcw_ref[...] = jnp.sum(dcw[...], axis=1)

    row = pl.BlockSpec((TB, D), lambda i: (i, 0))
    return pl.pallas_call(
        body, name="conv_bwd_taps", grid=(nblk,),
        in_specs=[row, pl.BlockSpec((HALO, D), lambda i: (jnp.minimum((i + 1) * per, T // HALO - 1), 0)),
                  row, pl.BlockSpec((HALO, D), lambda i: (jnp.maximum(i * per - 1, 0), 0)),
                  row, row, pl.BlockSpec((32, D), lambda i: (0, 0))],
        out_specs=[pl.BlockSpec((2, TB, D), lambda i: (0, i, 0)), pl.BlockSpec((32, D), lambda i: (0, 0))],
        out_shape=[jax.ShapeDtypeStruct((2, T, D), BF16), jax.ShapeDtypeStruct((32, D), F32)],
        scratch_shapes=[pltpu.VMEM((TB + HALO, D), F32), pltpu.VMEM((TB + HALO, D), F32),
                        pltpu.VMEM((32, SUBLANES, D), F32)],
        compiler_params=_params("arbitrary"),
    )(duc, duc, u, u, ua, ub, cw)


def _merge_fwd(x, oa, us, sa, sb, vec, w_ho, w_co, w_mo):
    T = x.shape[0]

    def body(x_ref, oa_ref, us_ref, sa_ref, sb_ref, vec_ref, who_hbm, wco_hbm, wmo_hbm,
             xo_ref, ya_ref, yb_ref, mg_ref, mo_ref, who, wco, wmo):
        @pl.when(pl.program_id(0) == 0)
        def _():
            pltpu.sync_copy(who_hbm, who)
            pltpu.sync_copy(wco_hbm, wco)
            pltpu.sync_copy(wmo_hbm, wmo)

        ya = _nn(oa_ref[...], who[...])
        yb = _nn(us_ref[...], wco[...])
        mg = (sa_ref[...].astype(F32) * ya + sb_ref[...].astype(F32) * yb).astype(BF16)
        mo = _nn(mg, wmo[...])
        xo_ref[...] = x_ref[...] + vec_ref[2:3, :] * mo
        ya_ref[...] = ya.astype(BF16)
        yb_ref[...] = yb.astype(BF16)
        mg_ref[...] = mg
        mo_ref[...] = mo.astype(BF16)

    row = pl.BlockSpec((TB, D), lambda i: (i, 0))
    bf = jax.ShapeDtypeStruct((T, D), BF16)
    wv = pltpu.VMEM((D, D), BF16)
    return pl.pallas_call(
        body, name="merge_fwd", grid=(T // TB,),
        in_specs=[row, row, row, row, row, pl.BlockSpec((8, D), lambda i: (0, 0)), ANY, ANY, ANY],
        out_specs=[row] * 5,
        out_shape=[jax.ShapeDtypeStruct((T, D), F32), bf, bf, bf, bf],
        scratch_shapes=[wv, wv, wv],
        compiler_params=_params("arbitrary"),
    )(x, oa, us, sa, sb, vec, w_ho, w_co, w_mo)


def _merge_bwd(dxo, mo, ya, yb, sa, sb, vec, w_ho, w_co, w_mo):
    T = dxo.shape[0]

    def body(dxo_ref, mo_ref, ya_ref, yb_ref, sa_ref, sb_ref, vec_ref, who_hbm, wco_hbm, wmo_hbm,
             dmo_ref, dya_ref, dyb_ref, doa_ref, dus_ref, dp_ref, acc_ref, who, wco, wmo):
        @pl.when(pl.program_id(0) == 0)
        def _():
            pltpu.sync_copy(who_hbm, who)
            pltpu.sync_copy(wco_hbm, wco)
            pltpu.sync_copy(wmo_hbm, wmo)
            acc_ref[...] = jnp.zeros_like(acc_ref)

        dxo = dxo_ref[...]
        acc_ref[2:3, :] += _colsum(mo_ref[...].astype(F32) * dxo)
        dmo = (vec_ref[2:3, :] * dxo).astype(BF16)
        dmo_ref[...] = dmo
        dmg = _nt(dmo, wmo[...])
        sa = sa_ref[...].astype(F32)
        sb = sb_ref[...].astype(F32)
        dya = (sa * dmg).astype(BF16)
        dyb = (sb * dmg).astype(BF16)
        dya_ref[...] = dya
        dyb_ref[...] = dyb
        dp_ref[0] = (dmg * ya_ref[...].astype(F32) * sa * (1.0 - sa)).astype(BF16)
        dp_ref[1] = (dmg * yb_ref[...].astype(F32) * sb * (1.0 - sb)).astype(BF16)
        doa_ref[...] = _nt(dya, who[...]).astype(BF16)
        dus_ref[...] = _nt(dyb, wco[...]).astype(BF16)

    row = pl.BlockSpec((TB, D), lambda i: (i, 0))
    one = pl.BlockSpec((None, TB, D), lambda i: (0, i, 0))
    vec8 = pl.BlockSpec((8, D), lambda i: (0, 0))
    bf = jax.ShapeDtypeStruct((T, D), BF16)
    bf1 = jax.ShapeDtypeStruct((1, T, D), BF16)
    wv = pltpu.VMEM((D, D), BF16)
    return pl.pallas_call(
        body, name="merge_bwd", grid=(T // TB,),
        in_specs=[row, row, row, row, row, row, vec8, ANY, ANY, ANY],
        out_specs=[one, one, one, row, row, pl.BlockSpec((2, TB, D), lambda i: (0, i, 0)), vec8],
        out_shape=[bf1, bf1, bf1, bf, bf, jax.ShapeDtypeStruct((2, T, D), BF16), jax.ShapeDtypeStruct((8, D), F32)],
        scratch_shapes=[wv, wv, wv],
        compiler_params=_params("arbitrary"),
    )(dxo, mo, ya, yb, sa, sb, vec, w_ho, w_co, w_mo)


def _head(x, tgt, gvec):
    T = x.shape[0]

    def body(x_ref, t_ref, g_ref, dx_ref, acc_ref):
        @pl.when(pl.program_id(0) == 0)
        def _():
            acc_ref[...] = jnp.zeros_like(acc_ref)

        x = x_ref[...]
        gf = g_ref[0:1, :]
        r = lax.rsqrt(jnp.mean(x * x, axis=-1, keepdims=True) + EPS)
        n = x * r
        err = n * gf - t_ref[...]
        acc_ref[1:2, :] += _colsum(err * err)
        dy = err * (1.0 / D)
        acc_ref[0:1, :] += _colsum(dy * n)
        dn = dy * gf
        dx_ref[...] = r * (dn - n * jnp.mean(dn * n, axis=-1, keepdims=True))

    row = pl.BlockSpec((TB, D), lambda i: (i, 0))
    vec8 = pl.BlockSpec((8, D), lambda i: (0, 0))
    return pl.pallas_call(
        body, name="loss_head", grid=(T // TB,),
        in_specs=[row, row, vec8], out_specs=[row, vec8],
        out_shape=[jax.ShapeDtypeStruct((T, D), F32), jax.ShapeDtypeStruct((8, D), F32)],
        compiler_params=_params("arbitrary"),
    )(x, tgt, gvec)


def _pack_rows(parts, total, name):
    def body(*refs):
        out = refs[-1]
        out[...] = jnp.zeros_like(out)
        for ref, (_, src, n, dst) in zip(refs[:-1], parts):
            out[dst:dst + n, :] = ref[src:src + n, :]

    arrs = [p[0] for p in parts]
    return pl.pallas_call(
        body, name=name, in_specs=[pl.BlockSpec(a.shape, lambda: (0, 0)) for a in arrs],
        out_specs=pl.BlockSpec((total, D), lambda: (0, 0)),
        out_shape=jax.ShapeDtypeStruct((total, D), F32),
    )(*arrs)


PACK_ROWS = 56
PACK_AT = {"ada_b": 0, "loss": 9, "norm_ffn1": 10, "norm_mix": 11, "hgrn_g": 12, "conv_b": 13, "conv_ln_g": 14,
           "conv_ln_b": 15, "norm_ffn2": 16, "norm_final": 17, "hgrn_lb": 18, "conv_w": 20}


def _local_step(x, tgt, mod, small, kc, weight, reduce, reduce_small):
    lb = jax.nn.sigmoid(small["hgrn_lb"][0:1] - small["hgrn_lb"][1:2])
    vec1 = _pack_rows([(mod, 0, 3, 0), (small["norm_ffn1"], 0, 1, 3)], 8, "pack_vec1")
    vec2 = _pack_rows([(mod, 3, 3, 0), (small["norm_mix"], 0, 1, 3), (lb, 0, 1, 4), (small["hgrn_g"], 0, 1, 5)],
                      8, "pack_vec2")
    vec3 = _pack_rows([(mod, 6, 3, 0), (small["norm_ffn2"], 0, 1, 3)], 8, "pack_vec3")
    cvec = _pack_rows([(small["conv_b"], 0, 1, 0), (small["conv_ln_g"], 0, 1, 1), (small["conv_ln_b"], 0, 1, 2)],
                      8, "pack_cvec")
    cw = small["conv_w"]
    gvec = _pack_rows([(small["norm_final"], 0, 1, 0)], 8, "pack_gvec")

    wg = {n: weight(n, vec1) for n in ("ffn1_w_in", "ffn1_w_out")}
    x1, h1, a1, b1, s1, f1 = _ffn_fwd(x, vec1, wg["ffn1_w_in"], wg["ffn1_w_out"], "ffn1_fwd")
    wg["mix_w_in"] = weight("mix_w_in", x1)
    h2, qr, g, k, v, og, u, ua, ub, sa, sb = _mix_proj_fwd(x1, vec2, wg["mix_w_in"])
    oa, o, st = _hgrn_fwd(qr, g, k, v, og, vec2)
    us, uc = _conv_fwd(u, cw, cvec)
    wg.update({n: weight(n, us) for n in ("hgrn_w_o", "conv_w_o", "mix_w_out")})
    x2, ya, yb, mg, mo = _merge_fwd(x1, oa, us, sa, sb, vec2, wg["hgrn_w_o"], wg["conv_w_o"], wg["mix_w_out"])
    wg.update({n: weight(n, x2) for n in ("ffn2_w_in", "ffn2_w_out")})
    x3, h3, a3, b3, s3, f3 = _ffn_fwd(x2, vec3, wg["ffn2_w_in"], wg["ffn2_w_out"], "ffn2_fwd")

    dx3, acc_head = _head(x3, tgt, gvec)
    dx2, df3, dab3, acc3 = _ffn_bwd(dx3, x2, vec3, a3, b3, f3, wg["ffn2_w_in"], wg["ffn2_w_out"], "ffn2_bwd")
    tok = reduce(("ffn2_w_out", "ffn2_w_in"), [_mm_tn(s3, df3, 1, kc, DFF // NCHIP, "ffn2_dwout"),
                                               _mm_tn(h3, dab3, 1, kc, D, "ffn2_dwin")])
    vec2b = vec2 + tok[0:1, 0:1]
    dmo, dya, dyb, doa, dus, dpc, acc_m = _merge_bwd(dx2, mo, ya, yb, sa, sb, vec2b,
                                                     wg["hgrn_w_o"], wg["conv_w_o"], wg["mix_w_out"])
    tok = reduce(("mix_w_out", "hgrn_w_o", "conv_w_o"),
                 [_mm_tn(mg, dmo, 1, kc, D // NCHIP, "mix_dwout"), _mm_tn(oa, dya, 1, kc, D // NCHIP, "hgrn_dwo"),
                  _mm_tn(us, dyb, 1, kc, D // NCHIP, "conv_dwo")])
    vec2c = vec2 + tok[0:1, 0:1]
    duc, acc_c = _conv_bwd_ln(dus, uc, cvec)
    dpb, dcw = _conv_bwd_taps(duc, u, ua, ub, cw)
    dpa, acc_h = _hgrn_bwd(doa, og, qr, g, k, v, o, st, vec2c)
    dx1, acc2 = _mix_proj_bwd(dx2, x1, vec2c, dpa, dpb, dpc, wg["mix_w_in"])
    gmix = _mm_tn(h2, dpa, 2, kc, D, "mix_dwin_a", slabs=NCHIP)
    gmix = _mm_tn(h2, dpb, 2, kc, D, "mix_dwin_b", into=gmix, slab=2, slabs=NCHIP)
    gmix = _mm_tn(h2, dpc, 2, kc, D, "mix_dwin_c", into=gmix, slab=3, slabs=NCHIP)
    tok = reduce(("mix_w_in",), [gmix])
    vec1b = vec1 + tok[0:1, 0:1]
    dx0, df1, dab1, acc1 = _ffn_bwd(dx1, x, vec1b, a1, b1, f1, wg["ffn1_w_in"], wg["ffn1_w_out"], "ffn1_bwd")

    at = PACK_AT
    packed = _pack_rows([
        (acc1, 0, 3, at["ada_b"]), (acc2, 0, 2, at["ada_b"] + 3), (acc_m, 2, 1, at["ada_b"] + 5),
        (acc3, 0, 3, at["ada_b"] + 6), (acc_head, 1, 1, at["loss"]), (acc1, 3, 1, at["norm_ffn1"]),
        (acc2, 3, 1, at["norm_mix"]), (acc_h, 0, 1, at["hgrn_g"]), (acc_c, 0, 3, at["conv_b"]),
        (acc3, 3, 1, at["norm_ffn2"]), (acc_head, 0, 1, at["norm_final"]), (acc_h, 1, 2, at["hgrn_lb"]),
        (dcw, 0, CONV_K, at["conv_w"])], PACK_ROWS, "pack_small_grads")
    done = reduce_small(packed)
    reduce(("ffn1_w_out", "ffn1_w_in"), [_mm_tn(s1, df1, 1, kc, DFF // NCHIP, "ffn1_dwout"),
                                         _mm_tn(h1, dab1, 1, kc, D, "ffn1_dwin")], done)
    return dx0


BLOCK_BYTES = 3 * 512 * 1024


def _row_block(rows, cols):
    for br in (512, 352, 256, 176, 128, 64, 32, 16, 8):
        if rows % br == 0 and br * cols * 4 <= BLOCK_BYTES:
            return br
    return rows


def _cast_into_slot(w, kc, name):
    R, C = w.shape
    br = _row_block(R, C)

    def body(kc_ref, w_ref, o_ref):
        o_ref[...] = w_ref[...].astype(BF16)

    return pl.pallas_call(
        body, name=name,
        grid_spec=pltpu.PrefetchScalarGridSpec(
            num_scalar_prefetch=1, grid=(R // br,),
            in_specs=[pl.BlockSpec((br, C), lambda i, kc: (i, 0))],
            out_specs=pl.BlockSpec((None, br, C), lambda i, kc: (kc[0], i, 0))),
        out_shape=jax.ShapeDtypeStruct((NCHIP, R, C), BF16), compiler_params=_params("parallel"),
    )(kc, w)


def _adamw(w, g, m, v, name, after=None):
    R, C = w.shape
    br = _row_block(R, C)
    extra = [] if after is None else [after]

    def body(w_ref, g_ref, m_ref, v_ref, *rest):
        d_ref, nm_ref, nv_ref = rest[-3:]
        gv = g_ref[...]
        nm = ADAM_B1 * m_ref[...] + (1.0 - ADAM_B1) * gv
        nv = ADAM_B2 * v_ref[...] + (1.0 - ADAM_B2) * (gv * gv)
        m_hat = nm / (1.0 - ADAM_B1 ** ADAM_STEP)
        v_hat = nv / (1.0 - ADAM_B2 ** ADAM_STEP)
        d_ref[...] = -ADAM_LR * (m_hat / (jnp.sqrt(v_hat) + ADAM_EPS) + ADAM_WD * w_ref[...])
        nm_ref[...] = nm
        nv_ref[...] = nv

    blk = pl.BlockSpec((br, C), lambda i: (i, 0))
    out = jax.ShapeDtypeStruct((R, C), F32)
    return pl.pallas_call(
        body, name=name, grid=(R // br,), in_specs=[blk] * 4 + [ANY] * len(extra), out_specs=[blk] * 3,
        out_shape=[out, out, out], compiler_params=_params("parallel"),
    )(w, g, m, v, *extra)


def _coords():
    return lax.axis_index("x"), lax.axis_index("y"), lax.axis_index("c")


def _flip(v, bit):
    return 1 - v if bit else v


def _allgather8(v, name):
    R, C = v.shape

    def body(v_ref, out_ref, send_sems, recv_sems, local_sem):
        x, y, c = _coords()
        me = 4 * x + 2 * y + c
        mine = pltpu.make_async_copy(v_ref, out_ref.at[me], local_sem)
        mine.start()

        def copy(m, block):
            peer = (_flip(x, m & 4), _flip(y, m & 2), _flip(c, m & 1))
            return pltpu.make_async_remote_copy(
                src_ref=v_ref, dst_ref=out_ref.at[block], send_sem=send_sems.at[m - 1],
                recv_sem=recv_sems.at[m - 1], device_id=peer, device_id_type=MESH)

        sends = [copy(m, me) for m in range(1, 8)]
        for cp in sends:
            cp.start()
        for m in range(1, 8):
            sender = 4 * _flip(x, m & 4) + 2 * _flip(y, m & 2) + _flip(c, m & 1)
            copy(m, sender).wait_recv()
        for cp in sends:
            cp.wait_send()
        mine.wait()

    vm = pl.BlockSpec(memory_space=pltpu.VMEM)
    return pl.pallas_call(
        body, name=name, in_specs=[vm], out_specs=vm,
        out_shape=jax.ShapeDtypeStruct((8, R, C), F32),
        scratch_shapes=[pltpu.SemaphoreType.DMA((7,)), pltpu.SemaphoreType.DMA((7,)), pltpu.SemaphoreType.DMA],
    )(v)


HBM = pl.BlockSpec(memory_space=pltpu.HBM)
SEM = pl.BlockSpec(memory_space=pltpu.SEMAPHORE)
EFFECT = pltpu.SideEffectType.DATAFLOW_SIDE_EFFECTING


def _chip_peer(x, y, m):
    px, py = _flip(x, m & 2), _flip(y, m & 1)
    return px, py, 2 * px + py


def _core_rows(land, c):
    half = land.shape[1] // 2
    return pl.ds(pl.multiple_of(c * half, 16), half)


def _gather_start(lands, groups, halved, after):
    n, ng, na = len(lands), len(groups), len(after)

    def body(*refs):
        ins = refs[:n]
        sends, recvs = refs[n + na:n + na + ng], refs[n + na + ng:n + na + 2 * ng]
        token = refs[n + na + 2 * ng + n]
        x, y, c = _coords()
        k = 2 * x + y
        for gi, grp in enumerate(groups):
            for j, t in enumerate(grp):
                mine = ins[t].at[k, _core_rows(ins[t], c), :] if halved[gi] else ins[t].at[k]
                for m in (1, 2, 3):
                    px, py, _ = _chip_peer(x, y, m)
                    pltpu.make_async_remote_copy(
                        src_ref=mine, dst_ref=mine, send_sem=sends[gi].at[3 * j + m - 1],
                        recv_sem=recvs[gi].at[3 * j + m - 1], device_id=(px, py, c), device_id_type=MESH).start()
        token[...] = jnp.zeros_like(token)

    sems = [pltpu.SemaphoreType.DMA((3 * len(g),)) for g in groups]
    out = pl.pallas_call(
        body, name="gather_weights_start",
        out_shape=sems + sems + [pltpu.HBM(a.shape, a.dtype) for a in lands] + [jax.ShapeDtypeStruct((8, 128), F32)],
        in_specs=[HBM] * n + [ANY] * na,
        out_specs=[SEM] * (2 * ng) + [HBM] * n + [pl.BlockSpec(memory_space=pltpu.VMEM)],
        input_output_aliases={t: 2 * ng + t for t in range(n)},
        compiler_params=pltpu.CompilerParams(has_side_effects=EFFECT),
    )(*[pltpu.with_memory_space_constraint(a, pltpu.HBM) for a in lands], *after)
    return out[:ng], out[ng:2 * ng], out[2 * ng:2 * ng + n], out[2 * ng + n]


def _gather_wait(lands, halved, send_sem, recv_sem, after, name):
    n = len(lands)

    def body(*refs):
        ins, send, recv = refs[:n], refs[n], refs[n + 1]
        x, y, c = _coords()
        k = 2 * x + y
        for j in range(n):
            rows = _core_rows(ins[j], c)
            for m in (1, 2, 3):
                px, py, pk = _chip_peer(x, y, m)
                cp = pltpu.make_async_remote_copy(
                    src_ref=ins[j].at[k, rows, :] if halved else ins[j].at[k],
                    dst_ref=ins[j].at[pk, rows, :] if halved else ins[j].at[pk], send_sem=send.at[3 * j + m - 1],
                    recv_sem=recv.at[3 * j + m - 1], device_id=(px, py, c), device_id_type=MESH)
                cp.wait_send()
                cp.wait_recv()

    return pl.pallas_call(
        body, name=name, out_shape=[pltpu.HBM(a.shape, a.dtype) for a in lands],
        in_specs=[HBM] * n + [SEM, SEM, ANY], out_specs=[HBM] * n,
        input_output_aliases={j: j for j in range(n)},
        compiler_params=pltpu.CompilerParams(has_side_effects=EFFECT),
    )(*lands, send_sem, recv_sem, after)


def _sibling_fill(lands, name):
    n = len(lands)

    def body(*refs):
        ins = refs[:n]
        send_sems, recv_sems = refs[2 * n:]
        x, y, c = _coords()
        sends, recvs = [], []
        for t in range(n):
            for m in (1, 2, 3):
                _, _, pk = _chip_peer(x, y, m)
                for rows, lst in ((_core_rows(ins[t], c), sends), (_core_rows(ins[t], 1 - c), recvs)):
                    lst.append(pltpu.make_async_remote_copy(
                        src_ref=ins[t].at[pk, rows, :], dst_ref=ins[t].at[pk, rows, :],
                        send_sem=send_sems.at[3 * t + m - 1], recv_sem=recv_sems.at[3 * t + m - 1],
                        device_id=(x, y, 1 - c), device_id_type=MESH))
        for cp in sends:
            cp.start()
        for cp in recvs:
            cp.wait_recv()
        for cp in sends:
            cp.wait_send()

    return pl.pallas_call(
        body, name=name, in_specs=[ANY] * n, out_specs=[ANY] * n,
        out_shape=[jax.ShapeDtypeStruct(a.shape, a.dtype) for a in lands],
        input_output_aliases={t: t for t in range(n)},
        scratch_shapes=[pltpu.SemaphoreType.DMA((3 * n,)), pltpu.SemaphoreType.DMA((3 * n,))],
    )(*lands)


def _scatter_start(srcs, name):
    n = len(srcs)

    def body(*refs):
        ins, lands = refs[:n], refs[n:2 * n]
        send, recv = refs[2 * n], refs[2 * n + 1]
        token = refs[2 * n + 2 + 2 * n]
        x, y, c = _coords()
        k = 2 * x + y
        for t in range(n):
            for m in (1, 2, 3):
                px, py, pk = _chip_peer(x, y, m)
                pltpu.make_async_remote_copy(
                    src_ref=ins[t].at[pk], dst_ref=lands[t].at[k], send_sem=send.at[3 * t + m - 1],
                    recv_sem=recv.at[3 * t + m - 1], device_id=(px, py, c), device_id_type=MESH).start()
        token[...] = jnp.zeros_like(token)

    sem = pltpu.SemaphoreType.DMA((3 * n,))
    hbm = [pltpu.HBM(a.shape, a.dtype) for a in srcs]
    operands = list(srcs) + [lax.empty(a.shape, a.dtype) for a in srcs]
    out = pl.pallas_call(
        body, name=name, out_shape=[sem, sem] + hbm + hbm + [jax.ShapeDtypeStruct((8, 128), F32)],
        in_specs=[HBM] * (2 * n), out_specs=[SEM, SEM] + [HBM] * (2 * n) + [pl.BlockSpec(memory_space=pltpu.VMEM)],
        input_output_aliases={t: 2 + t for t in range(2 * n)},
        compiler_params=pltpu.CompilerParams(has_side_effects=EFFECT),
    )(*[pltpu.with_memory_space_constraint(a, pltpu.HBM) for a in operands])
    return out[0], out[1], out[2:2 + n], out[2 + n:2 + 2 * n], out[2 + 2 * n]


def _scatter_wait(srcs, lands, send_sem, recv_sem, after, name):
    n = len(srcs)

    def body(*refs):
        ins, land = refs[:n], refs[n:2 * n]
        send, recv = refs[2 * n], refs[2 * n + 1]
        x, y, c = _coords()
        for t in range(n):
            for m in (1, 2, 3):
                px, py, pk = _chip_peer(x, y, m)
                cp = pltpu.make_async_remote_copy(
                    src_ref=ins[t].at[pk], dst_ref=land[t].at[pk], send_sem=send.at[3 * t + m - 1],
                    recv_sem=recv.at[3 * t + m - 1], device_id=(px, py, c), device_id_type=MESH)
                cp.wait_send()
                cp.wait_recv()

    hbm = [pltpu.HBM(a.shape, a.dtype) for a in srcs]
    out = pl.pallas_call(
        body, name=name, out_shape=hbm + hbm, in_specs=[HBM] * (2 * n) + [SEM, SEM, ANY], out_specs=[HBM] * (2 * n),
        input_output_aliases={t: t for t in range(2 * n)},
        compiler_params=pltpu.CompilerParams(has_side_effects=EFFECT),
    )(*srcs, *lands, send_sem, recv_sem, after)
    return out[:n], out[n:]


def _sibling_swap(hs, after, name):
    n = len(hs)

    def body(*refs):
        ins, outs = refs[:n], refs[n + 1:2 * n + 1]
        send_sems, recv_sems = refs[2 * n + 1:]
        x, y, c = _coords()
        copies = [pltpu.make_async_remote_copy(
            src_ref=ins[t], dst_ref=outs[t], send_sem=send_sems.at[t], recv_sem=recv_sems.at[t],
            device_id=(x, y, 1 - c), device_id_type=MESH) for t in range(n)]
        for cp in copies:
            cp.start()
        for cp in copies:
            cp.wait_recv()
        for cp in copies:
            cp.wait_send()

    return pl.pallas_call(
        body, name=name, in_specs=[ANY] * (n + 1), out_specs=[ANY] * n,
        out_shape=[jax.ShapeDtypeStruct(a.shape, a.dtype) for a in hs],
        scratch_shapes=[pltpu.SemaphoreType.DMA((n,)), pltpu.SemaphoreType.DMA((n,))],
    )(*hs, after)


def _sibling_join_halves(fs, name):
    n = len(fs)

    def body(*refs):
        ins = refs[:n]
        send_sems, recv_sems = refs[2 * n:]
        x, y, c = _coords()
        sends, recvs = [], []
        for t in range(n):
            half = ins[t].shape[0] // 2
            mine = pl.ds(pl.multiple_of(c * half, 8), half)
            theirs = pl.ds(pl.multiple_of((1 - c) * half, 8), half)
            for rows, lst in ((mine, sends), (theirs, recvs)):
                lst.append(pltpu.make_async_remote_copy(
                    src_ref=ins[t].at[rows, :], dst_ref=ins[t].at[rows, :], send_sem=send_sems.at[t],
                    recv_sem=recv_sems.at[t], device_id=(x, y, 1 - c), device_id_type=MESH))
        for cp in sends:
            cp.start()
        for cp in recvs:
            cp.wait_recv()
        for cp in sends:
            cp.wait_send()

    return pl.pallas_call(
        body, name=name, in_specs=[ANY] * n, out_specs=[ANY] * n,
        out_shape=[jax.ShapeDtypeStruct(a.shape, a.dtype) for a in fs],
        input_output_aliases={t: t for t in range(n)},
        scratch_shapes=[pltpu.SemaphoreType.DMA((n,)), pltpu.SemaphoreType.DMA((n,))],
    )(*fs)


def _sum_own_half(g, ra, kc, name):
    _, R, C = g.shape
    half = R // 2
    br = _row_block(half, C)
    nb = half // br

    def body(kc_ref, g_ref, ra_ref, o_ref):
        o_ref[...] = (g_ref[...] + ra_ref[...].astype(F32)).astype(BF16)

    return pl.pallas_call(
        body, name=name,
        grid_spec=pltpu.PrefetchScalarGridSpec(
            num_scalar_prefetch=1, grid=(NCHIP, nb),
            in_specs=[pl.BlockSpec((None, br, C), lambda j, i, kc: (j, kc[1] * nb + i, 0)),
                      pl.BlockSpec((None, br, C), lambda j, i, kc: (j, i, 0))],
            out_specs=pl.BlockSpec((None, br, C), lambda j, i, kc: (j, i, 0))),
        out_shape=jax.ShapeDtypeStruct((NCHIP, half, C), BF16),
        compiler_params=_params("parallel", "parallel"),
    )(kc, g, ra)


def _sum_chips(sa, rb, kc, name, after=None):
    _, half, C = rb.shape
    br = _row_block(half, C)
    nb = half // br
    extra = [] if after is None else [after]

    def body(kc_ref, own_ref, r1_ref, r2_ref, r3_ref, *rest):
        acc = own_ref[...].astype(F32) + r1_ref[...].astype(F32)
        rest[-1][...] = (acc + r2_ref[...].astype(F32)) + r3_ref[...].astype(F32)

    def slab(m):
        return pl.BlockSpec((None, br, C), lambda i, kc: (kc[0] ^ m, i, 0))

    return pl.pallas_call(
        body, name=name,
        grid_spec=pltpu.PrefetchScalarGridSpec(
            num_scalar_prefetch=1, grid=(nb,),
            in_specs=[slab(0), slab(1), slab(2), slab(3)] + [ANY] * len(extra),
            out_specs=pl.BlockSpec((br, C), lambda i, kc: (kc[1] * nb + i, 0))),
        out_shape=jax.ShapeDtypeStruct((2 * half, C), F32), compiler_params=_params("parallel"),
    )(kc, sa, rb, rb, rb, *extra)


def _sum8(ga, name):
    _, R, C = ga.shape

    def body(g_ref, o_ref):
        acc = g_ref[0]
        for j in range(1, 8):
            acc = acc + g_ref[j]
        o_ref[...] = acc

    return pl.pallas_call(
        body, name=name, in_specs=[pl.BlockSpec((8, R, C), lambda: (0, 0, 0))],
        out_specs=pl.BlockSpec((R, C), lambda: (0, 0)), out_shape=jax.ShapeDtypeStruct((R, C), F32),
    )(ga)


ADA_COLS = 9 * D // NCHIP
ADA_BLK = 256


def _ada_mod(c_all, ada_w, ada_b, kidx):
    def body(k_ref, c_ref, w_ref, b_ref, o_ref):
        cv = c_ref[...]
        cs = cv * _sigmoid(cv)
        o_ref[...] = jnp.dot(cs, w_ref[...], precision=lax.Precision.HIGHEST,
                             preferred_element_type=F32) + b_ref[...]

    nblk = ADA_COLS // ADA_BLK
    return pl.pallas_call(
        body, name="ada_mod",
        grid_spec=pltpu.PrefetchScalarGridSpec(
            num_scalar_prefetch=1, grid=(nblk,),
            in_specs=[pl.BlockSpec((8, D), lambda j, k: (0, 0)),
                      pl.BlockSpec((D, ADA_BLK), lambda j, k: (0, j)),
                      pl.BlockSpec((1, ADA_BLK), lambda j, k: (0, k[0] * nblk + j))],
            out_specs=pl.BlockSpec((8, ADA_BLK), lambda j, k: (0, j))),
        out_shape=jax.ShapeDtypeStruct((8, ADA_COLS), F32),
        compiler_params=_params("parallel"),
    )(kidx, c_all, ada_w, ada_b)


def _ada_grad(c_all_t, dmod_all, kidx):
    def body(k_ref, ct_ref, dm_ref, o_ref):
        cv = ct_ref[...]
        cs = cv * _sigmoid(cv)
        acc = cs[:, 0:1] * dm_ref[0:1, :]
        for b in range(1, 8):
            acc = acc + cs[:, b:b + 1] * dm_ref[b:b + 1, :]
        o_ref[...] = acc

    nblk = ADA_COLS // ADA_BLK
    return pl.pallas_call(
        body, name="ada_grad",
        grid_spec=pltpu.PrefetchScalarGridSpec(
            num_scalar_prefetch=1, grid=(nblk,),
            in_specs=[pl.BlockSpec((D, 8), lambda j, k: (0, 0)),
                      pl.BlockSpec((8, ADA_BLK), lambda j, k: (0, k[0] * nblk + j))],
            out_specs=pl.BlockSpec((D, ADA_BLK), lambda j, k: (0, j))),
        out_shape=jax.ShapeDtypeStruct((D, ADA_COLS), F32),
        compiler_params=_params("parallel"),
    )(kidx, c_all_t, dmod_all)


BIG = ("ffn1_w_in", "ffn1_w_out", "mix_w_in", "hgrn_w_o", "conv_w_o", "mix_w_out", "ffn2_w_in", "ffn2_w_out")
ROW_SHARDED = ("ffn1_w_out", "hgrn_w_o", "conv_w_o", "mix_w_out", "ffn2_w_out")
GATHER_GROUPS = ((0, 1), (2,), (3, 4, 5), (6, 7))
GATHER_HALVED = (True, True, False, False)
PACK_LEN = {"ada_b": 9, "hgrn_lb": 2}
WEIGHTS = ("ada_w", "ada_b", "norm_ffn1", "ffn1_w_in", "ffn1_w_out", "norm_mix", "mix_w_in", "hgrn_lb", "hgrn_g",
           "hgrn_w_o", "conv_w", "conv_b", "conv_ln_g", "conv_ln_b", "conv_w_o", "mix_w_out", "norm_ffn2",
           "ffn2_w_in", "ffn2_w_out", "norm_final")
PACKED = ("ada_b", "norm_ffn1", "norm_mix", "hgrn_g", "conv_b", "conv_ln_g", "conv_ln_b", "norm_ffn2",
          "norm_final", "hgrn_lb")


def _pack_params(p, name):
    parts = [(p[n].reshape(PACK_LEN.get(n, 1), D), 0, PACK_LEN.get(n, 1), PACK_AT[n]) for n in PACKED]
    return _pack_rows(parts, PACK_ROWS, name)


def _step(w, m, v, x, c, tgt):
    xi, yi, ci = _coords()
    kidx = (2 * xi + yi).astype(jnp.int32).reshape(1)
    kc = jnp.stack([2 * xi + yi, ci]).astype(jnp.int32)
    me = 4 * xi + 2 * yi + ci

    c_all = _allgather8(jnp.broadcast_to(c, (8, D)), "gather_c")[:, 0, :]
    mod_cols = _ada_mod(c_all, w["ada_w"][0], w["ada_b"], kidx)
    mod_all = _allgather8(mod_cols, "gather_mod")
    mod = lax.dynamic_slice(mod_all, (0, me, 0), (8, 1, ADA_COLS))[::2].reshape(9, D)
    small = {n: w[n].reshape(-1, D) for n in ("norm_ffn1", "norm_mix", "hgrn_lb", "hgrn_g", "conv_b", "conv_ln_g",
                                              "conv_ln_b", "norm_ffn2", "norm_final")}
    small["conv_w"] = _allgather_conv_w(w["conv_w"][0])

    lands = [_cast_into_slot(w[n][0], kc, "cast_" + n) for n in BIG]
    sends, recvs, lands, _ = _gather_start(lands, GATHER_GROUPS, GATHER_HALVED, [mod, small["conv_w"]])
    ready = {}

    def weight(name, after):
        t = BIG.index(name)
        if t not in ready:
            gi = [t in grp for grp in GATHER_GROUPS].index(True)
            grp = GATHER_GROUPS[gi]
            outs = _gather_wait([lands[j] for j in grp], GATHER_HALVED[gi], sends[gi], recvs[gi], after,
                                "gather_weights_wait%d" % gi)
            if GATHER_HALVED[gi]:
                outs = _sibling_fill(outs, "gather_weights_fill%d" % gi)
            ready.update(zip(grp, outs))
        return ready[t].reshape(-1, D) if name in ROW_SHARDED else ready[t]

    grads, delta, new_m, new_v = {}, {}, {}, {}
    flight = []
    landed = []

    def settle(after):
        names, sa, rb, send, recv = flight.pop()
        sa, rb = _scatter_wait(sa, rb, send, recv, after, "rs_chip_wait_" + names[0])
        landed.append((names, sa, rb))

    def reduce(names, pairs, after=None):
        gs = [g.reshape(NCHIP, -1, g.shape[-1]) for g, _ in pairs]
        hs = [h.reshape(NCHIP, -1, h.shape[-1]) for _, h in pairs]
        ra = _sibling_swap(hs, hs[0] if after is None else after, "rs_sibling_halves_" + names[0])
        sa = [_sum_own_half(g, r, kc, "rs_sum_pair_" + n) for g, r, n in zip(gs, ra, names)]
        if flight:
            settle(sa[0])
        send, recv, sa, rb, tok = _scatter_start(sa, "rs_chip_start_" + names[0])
        flight.append((names, sa, rb, send, recv))
        started.append(tok)
        return tok

    def adamw(n, after=None):
        shape = w[n].shape
        two = (shape[-2], shape[-1])
        d_, m_, v_ = _adamw(w[n].reshape(two), grads[n], m[n].reshape(two), v[n].reshape(two), "adamw_" + n, after)
        grads[n], delta[n], new_m[n], new_v[n] = (a.reshape(shape) for a in (grads[n], d_, m_, v_))
        return m_

    def finish(after=None):
        names, sa, rb = landed.pop(0)
        fin = [_sum_chips(s, r, kc, "rs_sum_chips_" + n, after) for s, r, n in zip(sa, rb, names)]
        full = _sibling_join_halves(fin, "rs_join_halves_" + names[0])
        grads.update(zip(names, full))
        return [adamw(n) for n in names][-1]

    started = []

    smalls = []

    def reduce_small(packed):
        packed_all = _allgather8(packed, "gather_small_grads")
        smalls.extend([packed_all, _sum8(packed_all, "sum_small_grads")])
        return smalls[1]

    dx = _local_step(x[0], tgt[0], mod, small, kc, weight, reduce, reduce_small)
    packed_all, gsum = smalls
    loss = (0.5 / D) * jnp.sum(gsum[PACK_AT["loss"]])
    dmod_all = packed_all[:, 0:9, :].reshape(8, 9 * D)
    grads["ada_w"] = _ada_grad(c_all.T, dmod_all, kidx)
    grads["conv_w"] = lax.dynamic_slice(gsum, (PACK_AT["conv_w"], kidx[0] * (D // NCHIP)), (CONV_K, D // NCHIP))

    tok = started[-1]
    adamw("ada_w", tok)
    adamw("conv_w")
    pw, pm, pv = (_pack_params(p, "pack_" + s) for p, s in ((w, "w"), (m, "m"), (v, "v")))
    pd, pnm, pnv = _adamw(pw, gsum, pm, pv, "adamw_small", tok)
    last = pnv
    while landed:
        last = finish(tok)
    settle(last)
    finish()
    for n in PACKED:
        rows = slice(PACK_AT[n], PACK_AT[n] + PACK_LEN.get(n, 1))
        for dst, src in ((grads, gsum), (delta, pd), (new_m, pnm), (new_v, pnv)):
            dst[n] = src[rows].reshape(w[n].shape)

    outs = [loss, dx[None]]
    for d in (grads, delta, new_m, new_v):
        outs += [d[n] for n in WEIGHTS]
    return tuple(outs)


def _allgather_conv_w(cw):
    padded = jnp.pad(cw, ((0, 32 - CONV_K), (0, 0)))
    parts = _allgather8(padded, "gather_conv_w")
    return jnp.concatenate([parts[2 * j] for j in range(NCHIP)], axis=1)


def kernel(x, c, ada_w, ada_b, norm_ffn1, ffn1_w_in, ffn1_w_out, norm_mix, mix_w_in, hgrn_lb, hgrn_g, hgrn_w_o, conv_w, conv_b, conv_ln_g, conv_ln_b, conv_w_o, mix_w_out, norm_ffn2, ffn2_w_in, ffn2_w_out, norm_final, loss_target, m_ada_w, m_ada_b, m_norm_ffn1, m_ffn1_w_in, m_ffn1_w_out, m_norm_mix, m_mix_w_in, m_hgrn_lb, m_hgrn_g, m_hgrn_w_o, m_conv_w, m_conv_b, m_conv_ln_g, m_conv_ln_b, m_conv_w_o, m_mix_w_out, m_norm_ffn2, m_ffn2_w_in, m_ffn2_w_out, m_norm_final, v_ada_w, v_ada_b, v_norm_ffn1, v_ffn1_w_in, v_ffn1_w_out, v_norm_mix, v_mix_w_in, v_hgrn_lb, v_hgrn_g, v_hgrn_w_o, v_conv_w, v_conv_b, v_conv_ln_g, v_conv_ln_b, v_conv_w_o, v_mix_w_out, v_norm_ffn2, v_ffn2_w_in, v_ffn2_w_out, v_norm_final):
    w = dict(ada_w=ada_w, ada_b=ada_b, norm_ffn1=norm_ffn1, ffn1_w_in=ffn1_w_in, ffn1_w_out=ffn1_w_out,
             norm_mix=norm_mix, mix_w_in=mix_w_in, hgrn_lb=hgrn_lb, hgrn_g=hgrn_g, hgrn_w_o=hgrn_w_o, conv_w=conv_w,
             conv_b=conv_b, conv_ln_g=conv_ln_g, conv_ln_b=conv_ln_b, conv_w_o=conv_w_o, mix_w_out=mix_w_out,
             norm_ffn2=norm_ffn2, ffn2_w_in=ffn2_w_in, ffn2_w_out=ffn2_w_out, norm_final=norm_final)
    m = dict(ada_w=m_ada_w, ada_b=m_ada_b, norm_ffn1=m_norm_ffn1, ffn1_w_in=m_ffn1_w_in, ffn1_w_out=m_ffn1_w_out,
             norm_mix=m_norm_mix, mix_w_in=m_mix_w_in, hgrn_lb=m_hgrn_lb, hgrn_g=m_hgrn_g, hgrn_w_o=m_hgrn_w_o,
             conv_w=m_conv_w, conv_b=m_conv_b, conv_ln_g=m_conv_ln_g, conv_ln_b=m_conv_ln_b, conv_w_o=m_conv_w_o,
             mix_w_out=m_mix_w_out, norm_ffn2=m_norm_ffn2, ffn2_w_in=m_ffn2_w_in, ffn2_w_out=m_ffn2_w_out,
             norm_final=m_norm_final)
    v = dict(ada_w=v_ada_w, ada_b=v_ada_b, norm_ffn1=v_norm_ffn1, ffn1_w_in=v_ffn1_w_in, ffn1_w_out=v_ffn1_w_out,
             norm_mix=v_norm_mix, mix_w_in=v_mix_w_in, hgrn_lb=v_hgrn_lb, hgrn_g=v_hgrn_g, hgrn_w_o=v_hgrn_w_o,
             conv_w=v_conv_w, conv_b=v_conv_b, conv_ln_g=v_conv_ln_g, conv_ln_b=v_conv_ln_b, conv_w_o=v_conv_w_o,
             mix_w_out=v_mix_w_out, norm_ffn2=v_norm_ffn2, ffn2_w_in=v_ffn2_w_in, ffn2_w_out=v_ffn2_w_out,
             norm_final=v_norm_final)
    return _step(w, m, v, x, c, loss_target)
```

```python
import functools

import jax
import jax.numpy as jnp
from jax import lax
from jax.experimental import pallas as pl
from jax.experimental.pallas import tpu as pltpu

F32 = jnp.float32
BF16 = jnp.bfloat16

D = 1024
DFF = 2816
NCHIP = 4
FSH = 2 * DFF // NCHIP
HEADS = 8
DK = 128
CHUNK = 64
CONV_K = 31
HALO = 32
EPS = 1e-6
TB = 256
CB = 512
DW_TOKENS = 2048
VMEM_LIMIT = 56 * 1024 * 1024

ADAM_LR = 0.001
ADAM_B1 = 0.9
ADAM_B2 = 0.999
ADAM_EPS = 1e-08
ADAM_WD = 0.01
ADAM_STEP = 10

MESH = pl.DeviceIdType.MESH
ANY = pl.BlockSpec(memory_space=pl.ANY)


def _params(*sem):
    return pltpu.CompilerParams(dimension_semantics=sem, vmem_limit_bytes=VMEM_LIMIT)


def _sigmoid(x):
    return 0.5 * jnp.tanh(0.5 * x) + 0.5


def _dsilu(x, sg):
    return sg * (1.0 + x * (1.0 - sg))


def _nt(a, b):
    return lax.dot_general(a, b, (((1,), (1,)), ((), ())), preferred_element_type=F32)


def _tn(a, b):
    return lax.dot_general(a, b, (((0,), (0,)), ((), ())), preferred_element_type=F32)


def _nn(a, b):
    return jnp.dot(a, b, preferred_element_type=F32)


def _colsum(x):
    return jnp.sum(x, axis=0, keepdims=True)


def _rms_fwd(x, gn, sc, sh):
    r = lax.rsqrt(jnp.mean(x * x, axis=-1, keepdims=True) + EPS)
    n = x * r
    h = (n * gn) * (1.0 + sc) + sh
    return r, n, h


def _rms_bwd(dh, r, n, gn, sc, acc_ref):
    acc_ref[0:1, :] += _colsum(dh)
    acc_ref[1:2, :] += _colsum(dh * (n * gn))
    dng = dh * (1.0 + sc)
    acc_ref[3:4, :] += _colsum(dng * n)
    dn = dng * gn
    return r * (dn - n * jnp.mean(dn * n, axis=-1, keepdims=True))


def _ffn_fwd(x, vec, w_in, w_out, name):
    T = x.shape[0]

    def body(x_ref, vec_ref, win_hbm, wout_hbm, xo_ref, h_ref, a_ref, b_ref, s_ref, f_ref, win, wout):
        @pl.when(pl.program_id(0) == 0)
        def _():
            pltpu.sync_copy(win_hbm, win)
            pltpu.sync_copy(wout_hbm, wout)

        x = x_ref[...]
        sh, sc, gate, gn = vec_ref[0:1, :], vec_ref[1:2, :], vec_ref[2:3, :], vec_ref[3:4, :]
        _, _, h = _rms_fwd(x, gn, sc, sh)
        hb = h.astype(BF16)
        h_ref[...] = hb
        f = jnp.zeros((TB, D), F32)
        for j in range(2):
            cols = slice(j * FSH, (j + 1) * FSH)
            a = _nn(hb, win[j])
            b = _nn(hb, win[2 + j])
            s = (a * _sigmoid(a) * b).astype(BF16)
            a_ref[:, cols] = a.astype(BF16)
            b_ref[:, cols] = b.astype(BF16)
            s_ref[:, cols] = s
            f = f + _nn(s, wout[cols, :])
        xo_ref[...] = x + (0.5 * gate) * f
        f_ref[...] = f.astype(BF16)

    row = lambda w: pl.BlockSpec((TB, w), lambda i: (i, 0))
    return pl.pallas_call(
        body, name=name, grid=(T // TB,),
        in_specs=[row(D), pl.BlockSpec((8, D), lambda i: (0, 0)), ANY, ANY],
        out_specs=[row(D), row(D), row(DFF), row(DFF), row(DFF), row(D)],
        out_shape=[jax.ShapeDtypeStruct((T, D), F32), jax.ShapeDtypeStruct((T, D), BF16),
                   jax.ShapeDtypeStruct((T, DFF), BF16), jax.ShapeDtypeStruct((T, DFF), BF16),
                   jax.ShapeDtypeStruct((T, DFF), BF16), jax.ShapeDtypeStruct((T, D), BF16)],
        scratch_shapes=[pltpu.VMEM((NCHIP, D, FSH), BF16), pltpu.VMEM((DFF, D), BF16)],
        compiler_params=_params("arbitrary"),
    )(x, vec, w_in, w_out)


def _ffn_bwd(dxo, x, vec, a, b, f, w_in, w_out, name):
    T = x.shape[0]

    def body(dxo_ref, x_ref, vec_ref, a_ref, b_ref, f_ref, win_hbm, wout_hbm,
             dx_ref, df_ref, dab_ref, acc_ref, win, wout):
        @pl.when(pl.program_id(0) == 0)
        def _():
            pltpu.sync_copy(win_hbm, win)
            pltpu.sync_copy(wout_hbm, wout)
            acc_ref[...] = jnp.zeros_like(acc_ref)

        dxo = dxo_ref[...]
        x = x_ref[...]
        sh, sc, gate, gn = vec_ref[0:1, :], vec_ref[1:2, :], vec_ref[2:3, :], vec_ref[3:4, :]
        r, n, _ = _rms_fwd(x, gn, sc, sh)
        acc_ref[2:3, :] += _colsum(0.5 * f_ref[...].astype(F32) * dxo)
        dfb = ((0.5 * gate) * dxo).astype(BF16)
        df_ref[...] = dfb
        dh = jnp.zeros((TB, D), F32)
        for j in range(2):
            cols = slice(j * FSH, (j + 1) * FSH)
            ds = _nt(dfb, wout[cols, :])
            av = a_ref[:, cols].astype(F32)
            bv = b_ref[:, cols].astype(F32)
            sg = _sigmoid(av)
            da = (ds * bv * _dsilu(av, sg)).astype(BF16)
            db = (ds * (av * sg)).astype(BF16)
            dab_ref[j] = da
            dab_ref[2 + j] = db
            dh = dh + _nt(da, win[j]) + _nt(db, win[2 + j])
        dx_ref[...] = dxo + _rms_bwd(dh, r, n, gn, sc, acc_ref)

    row = lambda w: pl.BlockSpec((TB, w), lambda i: (i, 0))
    vec8 = pl.BlockSpec((8, D), lambda i: (0, 0))
    return pl.pallas_call(
        body, name=name, grid=(T // TB,),
        in_specs=[row(D), row(D), vec8, row(DFF), row(DFF), row(D), ANY, ANY],
        out_specs=[row(D), pl.BlockSpec((None, TB, D), lambda i: (0, i, 0)),
                   pl.BlockSpec((NCHIP, TB, FSH), lambda i: (0, i, 0)), vec8],
        out_shape=[jax.ShapeDtypeStruct((T, D), F32), jax.ShapeDtypeStruct((1, T, D), BF16),
                   jax.ShapeDtypeStruct((NCHIP, T, FSH), BF16), jax.ShapeDtypeStruct((8, D), F32)],
        scratch_shapes=[pltpu.VMEM((NCHIP, D, FSH), BF16), pltpu.VMEM((DFF, D), BF16)],
        compiler_params=_params("arbitrary"),
    )(dxo, x, vec, a, b, f, w_in, w_out)


def _mm_tn(a, b3, hp, kc, shard_rows, name, into=None, slab=0, slabs=None):
    T, M = a.shape
    P, _, N = b3.shape
    tm = M if M <= 1408 else M // 2
    tk = min(T, DW_TOKENS)
    nk = T // tk
    ni = M // tm
    slabs = P // hp if slabs is None else slabs
    half = shard_rows // 2
    extra = [] if into is None else list(into)

    def body(kc_ref, a_ref, b_ref, *rest):
        o_ref, ra_ref, hbuf, send_sems, recv_sem = rest[-5:]
        p, i, k = pl.program_id(0), pl.program_id(1), pl.program_id(2)
        x, y, c = _coords()
        step = p * ni + i
        slot = step % 2

        def send(p_, i_, slot_):
            dst = ra_ref.at[slab + p_ // hp, pl.ds(pl.multiple_of(i_ * (tm // 2), 8), tm // 2),
                            pl.ds(pl.multiple_of((p_ % hp) * N, LANES), N)]
            return pltpu.make_async_remote_copy(
                src_ref=hbuf.at[slot_], dst_ref=dst, send_sem=send_sems.at[slot_], recv_sem=recv_sem,
                device_id=(x, y, 1 - c), device_id_type=MESH)

        @pl.when(k == 0)
        def _():
            o_ref[...] = jnp.zeros_like(o_ref)

        o_ref[...] += _tn(a_ref[...], b_ref[...])

        @pl.when(k == nk - 1)
        def _():
            @pl.when(step >= 2)
            def _():
                send(p, i, slot).wait_send()

            for j in range(tm // shard_rows):
                start = pl.multiple_of(j * shard_rows + (1 - kc_ref[1]) * half, 8)
                hbuf[slot, j * half:(j + 1) * half, :] = o_ref[pl.ds(start, half), :].astype(BF16)
            send(p, i, slot).start()

        @pl.when((step == P * ni - 1) & (k == nk - 1))
        def _():
            for s in range(min(2, P * ni)):
                send(p, i, (step - s) % 2).wait_send()
            mine = ra_ref.at[slab:slab + P // hp]
            pltpu.make_async_remote_copy(src_ref=mine, dst_ref=mine, send_sem=send_sems.at[0], recv_sem=recv_sem,
                                         device_id=(x, y, 1 - c), device_id_type=MESH).wait_recv()

    return pl.pallas_call(
        body, name=name,
        grid_spec=pltpu.PrefetchScalarGridSpec(
            num_scalar_prefetch=1, grid=(P, ni, nk),
            in_specs=[pl.BlockSpec((tk, tm), lambda p, i, k, kc: (k, i)),
                      pl.BlockSpec((None, tk, N), lambda p, i, k, kc: (p, k, 0))] + [ANY] * len(extra),
            out_specs=[pl.BlockSpec((None, tm, N), lambda p, i, k, kc: (slab + p // hp, i, p % hp)), ANY],
            scratch_shapes=[pltpu.VMEM((2, tm // 2, N), BF16), pltpu.SemaphoreType.DMA((2,)),
                            pltpu.SemaphoreType.DMA]),
        out_shape=[jax.ShapeDtypeStruct((slabs, M, hp * N), F32), jax.ShapeDtypeStruct((slabs, M // 2, hp * N), BF16)],
        input_output_aliases={} if into is None else {3: 0, 4: 1},
        compiler_params=_params("arbitrary", "arbitrary", "arbitrary"),
    )(kc, a, b3, *extra)


def _mix_proj_fwd(x, vec, w_in):
    T = x.shape[0]

    def body(x_ref, vec_ref, w_hbm, h_ref, qr_ref, g_ref, k_ref, v_ref, og_ref, u_ref, ua_ref, ub_ref,
             sa_ref, sb_ref, w):
        @pl.when(pl.program_id(0) == 0)
        def _():
            pltpu.sync_copy(w_hbm, w)

        x = x_ref[...]
        sh, sc, gn, lb = vec_ref[0:1, :], vec_ref[1:2, :], vec_ref[3:4, :], vec_ref[4:5, :]
        _, _, h = _rms_fwd(x, gn, sc, sh)
        hb = h.astype(BF16)
        h_ref[...] = hb
        p = _nn(hb, w[0])
        qr_ref[...] = p[:, :D].astype(BF16)
        fg = lb + (1.0 - lb) * _sigmoid(p[:, D:])
        g_ref[...] = jnp.log(fg)
        k_ref[...] = (1.0 - fg).astype(BF16)
        p = _nn(hb, w[1])
        v_ref[...] = p[:, :D].astype(BF16)
        og_ref[...] = p[:, D:].astype(BF16)
        p = _nn(hb, w[2])
        ua, ub = p[:, :D], p[:, D:]
        u_ref[...] = ua * _sigmoid(ub)
        ua_ref[...] = ua.astype(BF16)
        ub_ref[...] = ub.astype(BF16)
        p = _nn(hb, w[3])
        sa_ref[...] = _sigmoid(p[:, :D]).astype(BF16)
        sb_ref[...] = _sigmoid(p[:, D:]).astype(BF16)

    row = pl.BlockSpec((TB, D), lambda i: (i, 0))
    bf = jax.ShapeDtypeStruct((T, D), BF16)
    f32 = jax.ShapeDtypeStruct((T, D), F32)
    return pl.pallas_call(
        body, name="mix_proj_fwd", grid=(T // TB,),
        in_specs=[row, pl.BlockSpec((8, D), lambda i: (0, 0)), ANY],
        out_specs=[row] * 11,
        out_shape=[bf, bf, f32, bf, bf, bf, f32, bf, bf, bf, bf],
        scratch_shapes=[pltpu.VMEM((NCHIP, D, 2 * D), BF16)],
        compiler_params=_params("arbitrary"),
    )(x, vec, w_in)


def _mix_proj_bwd(dxo, x, vec, dpa, dpb, dpc, w_in):
    T = x.shape[0]

    def body(dxo_ref, x_ref, vec_ref, dpa_ref, dpb_ref, dpc_ref, w_hbm, dx_ref, acc_ref, w):
        @pl.when(pl.program_id(0) == 0)
        def _():
            pltpu.sync_copy(w_hbm, w)
            acc_ref[...] = jnp.zeros_like(acc_ref)

        x = x_ref[...]
        sh, sc, gn = vec_ref[0:1, :], vec_ref[1:2, :], vec_ref[3:4, :]
        r, n, _ = _rms_fwd(x, gn, sc, sh)
        dh = jnp.zeros((TB, D), F32)
        for p in range(8):
            src = dpa_ref[p] if p < 4 else (dpb_ref[p - 4] if p < 6 else dpc_ref[p - 6])
            dh = dh + _nt(src, w[p // 2, :, (p % 2) * D:(p % 2 + 1) * D])
        dx_ref[...] = dxo_ref[...] + _rms_bwd(dh, r, n, gn, sc, acc_ref)

    row = pl.BlockSpec((TB, D), lambda i: (i, 0))
    vec8 = pl.BlockSpec((8, D), lambda i: (0, 0))
    stack = lambda k: pl.BlockSpec((k, TB, D), lambda i: (0, i, 0))
    return pl.pallas_call(
        body, name="mix_proj_bwd", grid=(T // TB,),
        in_specs=[row, row, vec8, stack(4), stack(2), stack(2), ANY],
        out_specs=[row, vec8],
        out_shape=[jax.ShapeDtypeStruct((T, D), F32), jax.ShapeDtypeStruct((8, D), F32)],
        scratch_shapes=[pltpu.VMEM((NCHIP, D, 2 * D), BF16)],
        compiler_params=_params("arbitrary"),
    )(dxo, x, vec, dpa, dpb, dpc, w_in)


def _tri(lower):
    r = lax.broadcasted_iota(jnp.int32, (CHUNK, CHUNK), 0)
    c = lax.broadcasted_iota(jnp.int32, (CHUNK, CHUNK), 1)
    return (c <= r) if lower else (c >= r)


def _cumsum_rows(mask, g):
    hi = g.astype(BF16)
    rest = g - hi.astype(F32)
    mid = rest.astype(BF16)
    low = (rest - mid.astype(F32)).astype(BF16)
    n = g.shape[1]
    p = _nn(mask.astype(BF16), jnp.concatenate([hi, mid, low], axis=1))
    return (p[:, 2 * n:] + p[:, n:2 * n]) + p[:, :n]


def _chunk_decay(low, g, nck):
    bs, mids, lasts = [], [], []
    for c in range(nck):
        gc = g[c * CHUNK:(c + 1) * CHUNK]
        bs.append(_cumsum_rows(low, gc))
        mids.append(_colsum(gc[0:CHUNK // 2]))
        lasts.append(_colsum(gc))
    spread = lambda rows: jnp.concatenate([jnp.broadcast_to(r, (CHUNK, DK)) for r in rows], axis=0)
    return jnp.concatenate(bs, axis=0), spread(mids), spread(lasts), lasts


def _hgrn_fwd(qr, g, k, v, og, vec):
    T = qr.shape[0]
    nck = CB // CHUNK

    def body(qr_ref, g_ref, k_ref, v_ref, og_ref, vec_ref, out_ref, o_ref, st_ref, state):
        @pl.when(pl.program_id(1) == 0)
        def _():
            state[...] = jnp.zeros_like(state)

        low = _tri(True)
        qv = qr_ref[...].astype(F32)
        q = qv * _sigmoid(qv) * (DK ** -0.5)
        kk = k_ref[...].astype(F32)
        vb = v_ref[...]
        b, mid, last, lasts = _chunk_decay(low, g_ref[...], nck)
        qt = (q * jnp.exp(b - mid)).astype(BF16)
        kt = (kk * jnp.exp(mid - b)).astype(BF16)
        qe = (q * jnp.exp(b)).astype(BF16)
        kd = (kk * jnp.exp(last - b)).astype(BF16)
        intra, grow = [], []
        for c in range(nck):
            r = slice(c * CHUNK, (c + 1) * CHUNK)
            att = jnp.where(low, _nt(qt[r], kt[r]), 0.0).astype(BF16)
            intra.append(_nn(att, vb[r]))
            grow.append(_tn(vb[r], kd[r]))
        st = state[...]
        inter = []
        for c in range(nck):
            stb = st.astype(BF16)
            st_ref[c] = stb
            inter.append(_nt(qe[c * CHUNK:(c + 1) * CHUNK], stb))
            st = st * jnp.exp(lasts[c]) + grow[c]
        state[...] = st
        o = jnp.concatenate(intra, axis=0) + jnp.concatenate(inter, axis=0)
        o_ref[...] = o
        ogv = og_ref[...].astype(F32)
        rms = lax.rsqrt(jnp.mean(o * o, axis=-1, keepdims=True) + EPS)
        out_ref[...] = (o * rms * vec_ref[5:6, :] * (ogv * _sigmoid(ogv))).astype(BF16)

    blk = pl.BlockSpec((CB, DK), lambda h, i: (i, h))
    return pl.pallas_call(
        body, name="hgrn_fwd", grid=(HEADS, T // CB),
        in_specs=[blk, blk, blk, blk, blk, pl.BlockSpec((8, DK), lambda h, i: (0, h))],
        out_specs=[blk, blk, pl.BlockSpec((None, nck, DK, DK), lambda h, i: (h, i, 0, 0))],
        out_shape=[jax.ShapeDtypeStruct((T, D), BF16), jax.ShapeDtypeStruct((T, D), F32),
                   jax.ShapeDtypeStruct((HEADS, T // CHUNK, DK, DK), BF16)],
        scratch_shapes=[pltpu.VMEM((DK, DK), F32)],
        compiler_params=_params("parallel", "arbitrary"),
    )(qr, g, k, v, og, vec)


def _hgrn_bwd(dout, og, qr, g, k, v, o, st, vec):
    T = qr.shape[0]
    nck = CB // CHUNK
    nb = T // CB

    def body(dout_ref, og_ref, qr_ref, g_ref, k_ref, v_ref, o_ref, st_ref, vec_ref,
             dp_ref, acc_ref, dstate):
        @pl.when(pl.program_id(1) == 0)
        def _():
            dstate[...] = jnp.zeros_like(dstate)
            acc_ref[...] = jnp.zeros_like(acc_ref)

        o = o_ref[...]
        ogv = og_ref[...].astype(F32)
        dout = dout_ref[...].astype(F32)
        hg = vec_ref[5:6, :]
        sgo = _sigmoid(ogv)
        rms = lax.rsqrt(jnp.mean(o * o, axis=-1, keepdims=True) + EPS)
        ohat = o * rms
        dp_ref[3] = (dout * (ohat * hg) * _dsilu(ogv, sgo)).astype(BF16)
        don = dout * (ogv * sgo)
        acc_ref[0:1, :] += _colsum(don * ohat)
        dohat = don * hg
        dob = (rms * (dohat - ohat * jnp.mean(dohat * ohat, axis=-1, keepdims=True))).astype(BF16)

        low = _tri(True)
        upp = _tri(False)
        lb = vec_ref[4:5, :]
        qv = qr_ref[...].astype(F32)
        sgq = _sigmoid(qv)
        q = qv * sgq * (DK ** -0.5)
        kk = k_ref[...].astype(F32)
        vb = v_ref[...]
        gv = g_ref[...]
        b, mid, last, lasts = _chunk_decay(low, gv, nck)
        eq = jnp.exp(b - mid)
        ek = jnp.exp(mid - b)
        eb = jnp.exp(b)
        ed = jnp.exp(last - b)
        qtb, ktb, qeb, kdb = ((t).astype(BF16) for t in (q * eq, kk * ek, q * eb, kk * ed))
        rows = [slice(c * CHUNK, (c + 1) * CHUNK) for c in range(nck)]

        dv1, dqt, dkt, dqe, grow = [], [], [], [], []
        for c, r in enumerate(rows):
            att = jnp.where(low, _nt(qtb[r], ktb[r]), 0.0).astype(BF16)
            datt = jnp.where(low, _nt(dob[r], vb[r]), 0.0).astype(BF16)
            dv1.append(_tn(att, dob[r]))
            dqt.append(_nn(datt, ktb[r]))
            dkt.append(_tn(datt, qtb[r]))
            dqe.append(_nn(dob[r], st_ref[c]))
            grow.append(_tn(dob[r], qeb[r]))
        ds = dstate[...]
        ds1b, dl_state = [None] * nck, [None] * nck
        for c in reversed(range(nck)):
            el = jnp.exp(lasts[c])
            ds1b[c] = ds.astype(BF16)
            dl_state[c] = el * _colsum(ds * st_ref[c].astype(F32))
            ds = ds * el + grow[c]
        dstate[...] = ds
        dkd = jnp.concatenate([_nn(vb[r], ds1b[c]) for c, r in enumerate(rows)], axis=0)
        dv = jnp.concatenate(dv1, axis=0) + jnp.concatenate([_nt(kdb[r], ds1b[c]) for c, r in enumerate(rows)], axis=0)
        dqt, dkt, dqe = (jnp.concatenate(t, axis=0) for t in (dqt, dkt, dqe))
        dq = dqt * eq + dqe * eb
        dk = dkt * ek + dkd * ed
        dkdkd = dkd * kdb.astype(F32)
        db = dqt * qtb.astype(F32) - dkt * ktb.astype(F32) + dqe * qeb.astype(F32) - dkdkd
        dg = jnp.concatenate([_cumsum_rows(upp, db[r]) + (_colsum(dkdkd[r]) + dl_state[c])
                              for c, r in enumerate(rows)], axis=0)
        fg = jnp.exp(gv)
        dfg = dg * jnp.exp(-gv) - dk
        one_m_sig = (1.0 - fg) * (1.0 / (1.0 - lb))
        dp_ref[0] = (dq * (DK ** -0.5) * _dsilu(qv, sgq)).astype(BF16)
        dp_ref[1] = (dfg * (fg - lb) * one_m_sig).astype(BF16)
        dp_ref[2] = dv.astype(BF16)
        dlb = _colsum(dfg * one_m_sig) * (lb * (1.0 - lb))
        acc_ref[1:2, :] += dlb
        acc_ref[2:3, :] -= dlb

    blk = pl.BlockSpec((CB, DK), lambda h, i: (nb - 1 - i, h))
    return pl.pallas_call(
        body, name="hgrn_bwd", grid=(HEADS, nb),
        in_specs=[blk, blk, blk, blk, blk, blk, blk,
                  pl.BlockSpec((None, nck, DK, DK), lambda h, i: (h, nb - 1 - i, 0, 0)),
                  pl.BlockSpec((8, DK), lambda h, i: (0, h))],
        out_specs=[pl.BlockSpec((4, CB, DK), lambda h, i: (0, nb - 1 - i, h)),
                   pl.BlockSpec((8, DK), lambda h, i: (0, h))],
        out_shape=[jax.ShapeDtypeStruct((4, T, D), BF16), jax.ShapeDtypeStruct((8, D), F32)],
        scratch_shapes=[pltpu.VMEM((DK, DK), F32)],
        compiler_params=_params("parallel", "arbitrary"),
    )(dout, og, qr, g, k, v, o, st, vec)


def _ln_fwd(uc, lg, lbias):
    mu = jnp.mean(uc, axis=-1, keepdims=True)
    xc = uc - mu
    rstd = lax.rsqrt(jnp.mean(xc * xc, axis=-1, keepdims=True) + EPS)
    z = xc * rstd
    return rstd, z, z * lg + lbias


LANES = 128
SUBLANES = 8
CONV_ROWS = 128


def _lane_tiles():
    return [slice(l * LANES, (l + 1) * LANES) for l in range(D // LANES)]


def _row_shifts(x):
    n = x.shape[0]
    return [x] + [pltpu.roll(x, n - r, axis=0) for r in range(1, SUBLANES)]


def _shifted_rows(shifted, start, rows=TB):
    a, r = divmod(start, SUBLANES)
    return shifted[r][a * SUBLANES:a * SUBLANES + rows]


def _conv_fwd(u, cw, cvec):
    T = u.shape[0]
    per = TB // HALO

    def body(u_ref, halo_ref, cw_ref, cvec_ref, us_ref, uc_ref, pad):
        i = pl.program_id(0)
        pad[0:HALO, :] = jnp.where(i > 0, halo_ref[...], 0.0)
        pad[HALO:, :] = u_ref[...]
        for lanes in _lane_tiles():
            shifted = _row_shifts(pad[:, lanes])
            acc = jnp.broadcast_to(cvec_ref[0:1, lanes], (TB, LANES))
            for j in range(CONV_K):
                acc = acc + cw_ref[j:j + 1, lanes] * _shifted_rows(shifted, HALO - (CONV_K - 1) + j)
            uc_ref[:, lanes] = acc
        _, _, ul = _ln_fwd(uc_ref[...], cvec_ref[1:2, :], cvec_ref[2:3, :])
        us_ref[...] = (ul * _sigmoid(ul)).astype(BF16)

    row = pl.BlockSpec((TB, D), lambda i: (i, 0))
    return pl.pallas_call(
        body, name="conv_fwd", grid=(T // TB,),
        in_specs=[row, pl.BlockSpec((HALO, D), lambda i: (jnp.maximum(i * per - 1, 0), 0)),
                  pl.BlockSpec((32, D), lambda i: (0, 0)), pl.BlockSpec((8, D), lambda i: (0, 0))],
        out_specs=[row, row],
        out_shape=[jax.ShapeDtypeStruct((T, D), BF16), jax.ShapeDtypeStruct((T, D), F32)],
        scratch_shapes=[pltpu.VMEM((TB + HALO, D), F32)],
        compiler_params=_params("parallel"),
    )(u, u, cw, cvec)


def _conv_bwd_ln(dus, uc, cvec):
    T = uc.shape[0]

    def body(dus_ref, uc_ref, cvec_ref, duc_ref, acc_ref):
        @pl.when(pl.program_id(0) == 0)
        def _():
            acc_ref[...] = jnp.zeros_like(acc_ref)

        lg = cvec_ref[1:2, :]
        rstd, z, ul = _ln_fwd(uc_ref[...], lg, cvec_ref[2:3, :])
        dul = dus_ref[...].astype(F32) * _dsilu(ul, _sigmoid(ul))
        acc_ref[1:2, :] += _colsum(dul * z)
        acc_ref[2:3, :] += _colsum(dul)
        dz = dul * lg
        duc = rstd * (dz - jnp.mean(dz, axis=-1, keepdims=True) - z * jnp.mean(dz * z, axis=-1, keepdims=True))
        acc_ref[0:1, :] += _colsum(duc)
        duc_ref[...] = duc

    row = pl.BlockSpec((TB, D), lambda i: (i, 0))
    vec8 = pl.BlockSpec((8, D), lambda i: (0, 0))
    return pl.pallas_call(
        body, name="conv_bwd_ln", grid=(T // TB,),
        in_specs=[row, row, vec8], out_specs=[row, vec8],
        out_shape=[jax.ShapeDtypeStruct((T, D), F32), jax.ShapeDtypeStruct((8, D), F32)],
        compiler_params=_params("arbitrary"),
    )(dus, uc, cvec)


def _conv_bwd_taps(duc, u, ua, ub, cw):
    T = u.shape[0]
    per = TB // HALO
    nblk = T // TB

    def body(duc_ref, dnext_ref, u_ref, uprev_ref, ua_ref, ub_ref, cw_ref, dp_ref, dcw_ref, upad, dpad, dcw):
        i = pl.program_id(0)

        @pl.when(i == 0)
        def _():
            dcw[...] = jnp.zeros_like(dcw)

        upad[0:HALO, :] = jnp.where(i > 0, uprev_ref[...], 0.0)
        upad[HALO:, :] = u_ref[...]
        dpad[0:TB, :] = duc_ref[...]
        dpad[TB:, :] = jnp.where(i < nblk - 1, dnext_ref[...], 0.0)
        for lanes in _lane_tiles():
            ushift = _row_shifts(upad[:, lanes])
            dshift = _row_shifts(dpad[:, lanes])
            for r0 in range(0, TB, CONV_ROWS):
                rows = slice(r0, r0 + CONV_ROWS)
                duc = duc_ref[rows, lanes]
                du = jnp.zeros((CONV_ROWS, LANES), F32)
                for j in range(CONV_K):
                    prod = duc * _shifted_rows(ushift, r0 + HALO - (CONV_K - 1) + j, CONV_ROWS)
                    dcw[j, :, lanes] += jnp.sum(prod.reshape(CONV_ROWS // SUBLANES, SUBLANES, LANES), axis=0)
                    du = du + cw_ref[j:j + 1, lanes] * _shifted_rows(dshift, r0 + CONV_K - 1 - j, CONV_ROWS)
                ua = ua_ref[rows, lanes].astype(F32)
                sg = _sigmoid(ub_ref[rows, lanes].astype(F32))
                dp_ref[0, rows, lanes] = (du * sg).astype(BF16)
                dp_ref[1, rows, lanes] = (du * ua * sg * (1.0 - sg)).astype(BF16)

        @pl.when(i == nblk - 1)
        def _():
            dcw_ref[...] = jnp.sum(dcw[...], axis=1)

    row = pl.BlockSpec((TB, D), lambda i: (i, 0))
    return pl.pallas_call(
        body, name="conv_bwd_taps", grid=(nblk,),
        in_specs=[row, pl.BlockSpec((HALO, D), lambda i: (jnp.minimum((i + 1) * per, T // HALO - 1), 0)),
                  row, pl.BlockSpec((HALO, D), lambda i: (jnp.maximum(i * per - 1, 0), 0)),
                  row, row, pl.BlockSpec((32, D), lambda i: (0, 0))],
        out_specs=[pl.BlockSpec((2, TB, D), lambda i: (0, i, 0)), pl.BlockSpec((32, D), lambda i: (0, 0))],
        out_shape=[jax.ShapeDtypeStruct((2, T, D), BF16), jax.ShapeDtypeStruct((32, D), F32)],
        scratch_shapes=[pltpu.VMEM((TB + HALO, D), F32), pltpu.VMEM((TB + HALO, D), F32),
                        pltpu.VMEM((32, SUBLANES, D), F32)],
        compiler_params=_params("arbitrary"),
    )(duc, duc, u, u, ua, ub, cw)


def _merge_fwd(x, oa, us, sa, sb, vec, w_ho, w_co, w_mo):
    T = x.shape[0]

    def body(x_ref, oa_ref, us_ref, sa_ref, sb_ref, vec_ref, who_hbm, wco_hbm, wmo_hbm,
             xo_ref, ya_ref, yb_ref, mg_ref, mo_ref, who, wco, wmo):
        @pl.when(pl.program_id(0) == 0)
        def _():
            pltpu.sync_copy(who_hbm, who)
            pltpu.sync_copy(wco_hbm, wco)
            pltpu.sync_copy(wmo_hbm, wmo)

        ya = _nn(oa_ref[...], who[...])
        yb = _nn(us_ref[...], wco[...])
        mg = (sa_ref[...].astype(F32) * ya + sb_ref[...].astype(F32) * yb).astype(BF16)
        mo = _nn(mg, wmo[...])
        xo_ref[...] = x_ref[...] + vec_ref[2:3, :] * mo
        ya_ref[...] = ya.astype(BF16)
        yb_ref[...] = yb.astype(BF16)
        mg_ref[...] = mg
        mo_ref[...] = mo.astype(BF16)

    row = pl.BlockSpec((TB, D), lambda i: (i, 0))
    bf = jax.ShapeDtypeStruct((T, D), BF16)
    wv = pltpu.VMEM((D, D), BF16)
    return pl.pallas_call(
        body, name="merge_fwd", grid=(T // TB,),
        in_specs=[row, row, row, row, row, pl.BlockSpec((8, D), lambda i: (0, 0)), ANY, ANY, ANY],
        out_specs=[row] * 5,
        out_shape=[jax.ShapeDtypeStruct((T, D), F32), bf, bf, bf, bf],
        scratch_shapes=[wv, wv, wv],
        compiler_params=_params("arbitrary"),
    )(x, oa, us, sa, sb, vec, w_ho, w_co, w_mo)


def _merge_bwd(dxo, mo, ya, yb, sa, sb, vec, w_ho, w_co, w_mo):
    T = dxo.shape[0]

    def body(dxo_ref, mo_ref, ya_ref, yb_ref, sa_ref, sb_ref, vec_ref, who_hbm, wco_hbm, wmo_hbm,
             dmo_ref, dya_ref, dyb_ref, doa_ref, dus_ref, dp_ref, acc_ref, who, wco, wmo):
        @pl.when(pl.program_id(0) == 0)
        def _():
            pltpu.sync_copy(who_hbm, who)
            pltpu.sync_copy(wco_hbm, wco)
            pltpu.sync_copy(wmo_hbm, wmo)
            acc_ref[...] = jnp.zeros_like(acc_ref)

        dxo = dxo_ref[...]
        acc_ref[2:3, :] += _colsum(mo_ref[...].astype(F32) * dxo)
        dmo = (vec_ref[2:3, :] * dxo).astype(BF16)
        dmo_ref[...] = dmo
        dmg = _nt(dmo, wmo[...])
        sa = sa_ref[...].astype(F32)
        sb = sb_ref[...].astype(F32)
        dya = (sa * dmg).astype(BF16)
        dyb = (sb * dmg).astype(BF16)
        dya_ref[...] = dya
        dyb_ref[...] = dyb
        dp_ref[0] = (dmg * ya_ref[...].astype(F32) * sa * (1.0 - sa)).astype(BF16)
        dp_ref[1] = (dmg * yb_ref[...].astype(F32) * sb * (1.0 - sb)).astype(BF16)
        doa_ref[...] = _nt(dya, who[...]).astype(BF16)
        dus_ref[...] = _nt(dyb, wco[...]).astype(BF16)

    row = pl.BlockSpec((TB, D), lambda i: (i, 0))
    one = pl.BlockSpec((None, TB, D), lambda i: (0, i, 0))
    vec8 = pl.BlockSpec((8, D), lambda i: (0, 0))
    bf = jax.ShapeDtypeStruct((T, D), BF16)
    bf1 = jax.ShapeDtypeStruct((1, T, D), BF16)
    wv = pltpu.VMEM((D, D), BF16)
    return pl.pallas_call(
        body, name="merge_bwd", grid=(T // TB,),
        in_specs=[row, row, row, row, row, row, vec8, ANY, ANY, ANY],
        out_specs=[one, one, one, row, row, pl.BlockSpec((2, TB, D), lambda i: (0, i, 0)), vec8],
        out_shape=[bf1, bf1, bf1, bf, bf, jax.ShapeDtypeStruct((2, T, D), BF16), jax.ShapeDtypeStruct((8, D), F32)],
        scratch_shapes=[wv, wv, wv],
        compiler_params=_params("arbitrary"),
    )(dxo, mo, ya, yb, sa, sb, vec, w_ho, w_co, w_mo)


def _head(x, tgt, gvec):
    T = x.shape[0]

    def body(x_ref, t_ref, g_ref, dx_ref, acc_ref):
        @pl.when(pl.program_id(0) == 0)
        def _():
            acc_ref[...] = jnp.zeros_like(acc_ref)

        x = x_ref[...]
        gf = g_ref[0:1, :]
        r = lax.rsqrt(jnp.mean(x * x, axis=-1, keepdims=True) + EPS)
        n = x * r
        err = n * gf - t_ref[...]
        acc_ref[1:2, :] += _colsum(err * err)
        dy = err * (1.0 / D)
        acc_ref[0:1, :] += _colsum(dy * n)
        dn = dy * gf
        dx_ref[...] = r * (dn - n * jnp.mean(dn * n, axis=-1, keepdims=True))

    row = pl.BlockSpec((TB, D), lambda i: (i, 0))
    vec8 = pl.BlockSpec((8, D), lambda i: (0, 0))
    return pl.pallas_call(
        body, name="loss_head", grid=(T // TB,),
        in_specs=[row, row, vec8], out_specs=[row, vec8],
        out_shape=[jax.ShapeDtypeStruct((T, D), F32), jax.ShapeDtypeStruct((8, D), F32)],
        compiler_params=_params("arbitrary"),
    )(x, tgt, gvec)


def _pack_rows(parts, total, name):
    def body(*refs):
        out = refs[-1]
        out[...] = jnp.zeros_like(out)
        for ref, (_, src, n, dst) in zip(refs[:-1], parts):
            out[dst:dst + n, :] = ref[src:src + n, :]

    arrs = [p[0] for p in parts]
    return pl.pallas_call(
        body, name=name, in_specs=[pl.BlockSpec(a.shape, lambda: (0, 0)) for a in arrs],
        out_specs=pl.BlockSpec((total, D), lambda: (0, 0)),
        out_shape=jax.ShapeDtypeStruct((total, D), F32),
    )(*arrs)


PACK_ROWS = 56
PACK_AT = {"ada_b": 0, "loss": 9, "norm_ffn1": 10, "norm_mix": 11, "hgrn_g": 12, "conv_b": 13, "conv_ln_g": 14,
           "conv_ln_b": 15, "norm_ffn2": 16, "norm_final": 17, "hgrn_lb": 18, "conv_w": 20}


def _local_step(x, tgt, mod, small, kc, weight, reduce, reduce_small):
    lb = jax.nn.sigmoid(small["hgrn_lb"][0:1] - small["hgrn_lb"][1:2])
    vec1 = _pack_rows([(mod, 0, 3, 0), (small["norm_ffn1"], 0, 1, 3)], 8, "pack_vec1")
    vec2 = _pack_rows([(mod, 3, 3, 0), (small["norm_mix"], 0, 1, 3), (lb, 0, 1, 4), (small["hgrn_g"], 0, 1, 5)],
                      8, "pack_vec2")
    vec3 = _pack_rows([(mod, 6, 3, 0), (small["norm_ffn2"], 0, 1, 3)], 8, "pack_vec3")
    cvec = _pack_rows([(small["conv_b"], 0, 1, 0), (small["conv_ln_g"], 0, 1, 1), (small["conv_ln_b"], 0, 1, 2)],
                      8, "pack_cvec")
    cw = small["conv_w"]
    gvec = _pack_rows([(small["norm_final"], 0, 1, 0)], 8, "pack_gvec")

    wg = {n: weight(n, vec1) for n in ("ffn1_w_in", "ffn1_w_out")}
    x1, h1, a1, b1, s1, f1 = _ffn_fwd(x, vec1, wg["ffn1_w_in"], wg["ffn1_w_out"], "ffn1_fwd")
    wg["mix_w_in"] = weight("mix_w_in", x1)
    h2, qr, g, k, v, og, u, ua, ub, sa, sb = _mix_proj_fwd(x1, vec2, wg["mix_w_in"])
    oa, o, st = _hgrn_fwd(qr, g, k, v, og, vec2)
    us, uc = _conv_fwd(u, cw, cvec)
    wg.update({n: weight(n, us) for n in ("hgrn_w_o", "conv_w_o", "mix_w_out")})
    x2, ya, yb, mg, mo = _merge_fwd(x1, oa, us, sa, sb, vec2, wg["hgrn_w_o"], wg["conv_w_o"], wg["mix_w_out"])
    wg.update({n: weight(n, x2) for n in ("ffn2_w_in", "ffn2_w_out")})
    x3, h3, a3, b3, s3, f3 = _ffn_fwd(x2, vec3, wg["ffn2_w_in"], wg["ffn2_w_out"], "ffn2_fwd")

    dx3, acc_head = _head(x3, tgt, gvec)
    dx2, df3, dab3, acc3 = _ffn_bwd(dx3, x2, vec3, a3, b3, f3, wg["ffn2_w_in"], wg["ffn2_w_out"], "ffn2_bwd")
    tok = reduce(("ffn2_w_out", "ffn2_w_in"), [_mm_tn(s3, df3, 1, kc, DFF // NCHIP, "ffn2_dwout"),
                                               _mm_tn(h3, dab3, 1, kc, D, "ffn2_dwin")])
    vec2b = vec2 + tok[0:1, 0:1]
    dmo, dya, dyb, doa, dus, dpc, acc_m = _merge_bwd(dx2, mo, ya, yb, sa, sb, vec2b,
                                                     wg["hgrn_w_o"], wg["conv_w_o"], wg["mix_w_out"])
    tok = reduce(("mix_w_out", "hgrn_w_o", "conv_w_o"),
                 [_mm_tn(mg, dmo, 1, kc, D // NCHIP, "mix_dwout"), _mm_tn(oa, dya, 1, kc, D // NCHIP, "hgrn_dwo"),
                  _mm_tn(us, dyb, 1, kc, D // NCHIP, "conv_dwo")])
    vec2c = vec2 + tok[0:1, 0:1]
    duc, acc_c = _conv_bwd_ln(dus, uc, cvec)
    dpb, dcw = _conv_bwd_taps(duc, u, ua, ub, cw)
    dpa, acc_h = _hgrn_bwd(doa, og, qr, g, k, v, o, st, vec2c)
    dx1, acc2 = _mix_proj_bwd(dx2, x1, vec2c, dpa, dpb, dpc, wg["mix_w_in"])
    gmix = _mm_tn(h2, dpa, 2, kc, D, "mix_dwin_a", slabs=NCHIP)
    gmix = _mm_tn(h2, dpb, 2, kc, D, "mix_dwin_b", into=gmix, slab=2, slabs=NCHIP)
    gmix = _mm_tn(h2, dpc, 2, kc, D, "mix_dwin_c", into=gmix, slab=3, slabs=NCHIP)
    tok = reduce(("mix_w_in",), [gmix])
    vec1b = vec1 + tok[0:1, 0:1]
    dx0, df1, dab1, acc1 = _ffn_bwd(dx1, x, vec1b, a1, b1, f1, wg["ffn1_w_in"], wg["ffn1_w_out"], "ffn1_bwd")

    at = PACK_AT
    packed = _pack_rows([
        (acc1, 0, 3, at["ada_b"]), (acc2, 0, 2, at["ada_b"] + 3), (acc_m, 2, 1, at["ada_b"] + 5),
        (acc3, 0, 3, at["ada_b"] + 6), (acc_head, 1, 1, at["loss"]), (acc1, 3, 1, at["norm_ffn1"]),
        (acc2, 3, 1, at["norm_mix"]), (acc_h, 0, 1, at["hgrn_g"]), (acc_c, 0, 3, at["conv_b"]),
        (acc3, 3, 1, at["norm_ffn2"]), (acc_head, 0, 1, at["norm_final"]), (acc_h, 1, 2, at["hgrn_lb"]),
        (dcw, 0, CONV_K, at["conv_w"])], PACK_ROWS, "pack_small_grads")
    done = reduce_small(packed)
    reduce(("ffn1_w_out", "ffn1_w_in"), [_mm_tn(s1, df1, 1, kc, DFF // NCHIP, "ffn1_dwout"),
                                         _mm_tn(h1, dab1, 1, kc, D, "ffn1_dwin")], done)
    return dx0


BLOCK_BYTES = 3 * 512 * 1024


def _row_block(rows, cols):
    for br in (512, 352, 256, 176, 128, 64, 32, 16, 8):
        if rows % br == 0 and br * cols * 4 <= BLOCK_BYTES:
            return br
    return rows


def _cast_into_slot(w, kc, name):
    R, C = w.shape
    br = _row_block(R, C)

    def body(kc_ref, w_ref, o_ref):
        o_ref[...] = w_ref[...].astype(BF16)

    return pl.pallas_call(
        body, name=name,
        grid_spec=pltpu.PrefetchScalarGridSpec(
            num_scalar_prefetch=1, grid=(R // br,),
            in_specs=[pl.BlockSpec((br, C), lambda i, kc: (i, 0))],
            out_specs=pl.BlockSpec((None, br, C), lambda i, kc: (kc[0], i, 0))),
        out_shape=jax.ShapeDtypeStruct((NCHIP, R, C), BF16), compiler_params=_params("parallel"),
    )(kc, w)


def _adamw(w, g, m, v, name, after=None):
    R, C = w.shape
    br = _row_block(R, C)
    extra = [] if after is None else [after]

    def body(w_ref, g_ref, m_ref, v_ref, *rest):
        d_ref, nm_ref, nv_ref = rest[-3:]
        gv = g_ref[...]
        nm = ADAM_B1 * m_ref[...] + (1.0 - ADAM_B1) * gv
        nv = ADAM_B2 * v_ref[...] + (1.0 - ADAM_B2) * (gv * gv)
        m_hat = nm / (1.0 - ADAM_B1 ** ADAM_STEP)
        v_hat = nv / (1.0 - ADAM_B2 ** ADAM_STEP)
        d_ref[...] = -ADAM_LR * (m_hat / (jnp.sqrt(v_hat) + ADAM_EPS) + ADAM_WD * w_ref[...])
        nm_ref[...] = nm
        nv_ref[...] = nv

    blk = pl.BlockSpec((br, C), lambda i: (i, 0))
    out = jax.ShapeDtypeStruct((R, C), F32)
    return pl.pallas_call(
        body, name=name, grid=(R // br,), in_specs=[blk] * 4 + [ANY] * len(extra), out_specs=[blk] * 3,
        out_shape=[out, out, out], compiler_params=_params("parallel"),
    )(w, g, m, v, *extra)


def _coords():
    return lax.axis_index("x"), lax.axis_index("y"), lax.axis_index("c")


def _flip(v, bit):
    return 1 - v if bit else v


def _allgather8(v, name):
    R, C = v.shape

    def body(v_ref, out_ref, send_sems, recv_sems, local_sem):
        x, y, c = _coords()
        me = 4 * x + 2 * y + c
        mine = pltpu.make_async_copy(v_ref, out_ref.at[me], local_sem)
        mine.start()

        def copy(m, block):
            peer = (_flip(x, m & 4), _flip(y, m & 2), _flip(c, m & 1))
            return pltpu.make_async_remote_copy(
                src_ref=v_ref, dst_ref=out_ref.at[block], send_sem=send_sems.at[m - 1],
                recv_sem=recv_sems.at[m - 1], device_id=peer, device_id_type=MESH)

        sends = [copy(m, me) for m in range(1, 8)]
        for cp in sends:
            cp.start()
        for m in range(1, 8):
            sender = 4 * _flip(x, m & 4) + 2 * _flip(y, m & 2) + _flip(c, m & 1)
            copy(m, sender).wait_recv()
        for cp in sends:
            cp.wait_send()
        mine.wait()

    vm = pl.BlockSpec(memory_space=pltpu.VMEM)
    return pl.pallas_call(
        body, name=name, in_specs=[vm], out_specs=vm,
        out_shape=jax.ShapeDtypeStruct((8, R, C), F32),
        scratch_shapes=[pltpu.SemaphoreType.DMA((7,)), pltpu.SemaphoreType.DMA((7,)), pltpu.SemaphoreType.DMA],
    )(v)


HBM = pl.BlockSpec(memory_space=pltpu.HBM)
SEM = pl.BlockSpec(memory_space=pltpu.SEMAPHORE)
EFFECT = pltpu.SideEffectType.DATAFLOW_SIDE_EFFECTING


def _chip_peer(x, y, m):
    px, py = _flip(x, m & 2), _flip(y, m & 1)
    return px, py, 2 * px + py


def _core_rows(land, c):
    half = land.shape[1] // 2
    return pl.ds(pl.multiple_of(c * half, 16), half)


def _gather_start(lands, groups, halved, after):
    n, ng, na = len(lands), len(groups), len(after)

    def body(*refs):
        ins = refs[:n]
        sends, recvs = refs[n + na:n + na + ng], refs[n + na + ng:n + na + 2 * ng]
        token = refs[n + na + 2 * ng + n]
        x, y, c = _coords()
        k = 2 * x + y
        for gi, grp in enumerate(groups):
            for j, t in enumerate(grp):
                mine = ins[t].at[k, _core_rows(ins[t], c), :] if halved[gi] else ins[t].at[k]
                for m in (1, 2, 3):
                    px, py, _ = _chip_peer(x, y, m)
                    pltpu.make_async_remote_copy(
                        src_ref=mine, dst_ref=mine, send_sem=sends[gi].at[3 * j + m - 1],
                        recv_sem=recvs[gi].at[3 * j + m - 1], device_id=(px, py, c), device_id_type=MESH).start()
        token[...] = jnp.zeros_like(token)

    sems = [pltpu.SemaphoreType.DMA((3 * len(g),)) for g in groups]
    out = pl.pallas_call(
        body, name="gather_weights_start",
        out_shape=sems + sems + [pltpu.HBM(a.shape, a.dtype) for a in lands] + [jax.ShapeDtypeStruct((8, 128), F32)],
        in_specs=[HBM] * n + [ANY] * na,
        out_specs=[SEM] * (2 * ng) + [HBM] * n + [pl.BlockSpec(memory_space=pltpu.VMEM)],
        input_output_aliases={t: 2 * ng + t for t in range(n)},
        compiler_params=pltpu.CompilerParams(has_side_effects=EFFECT),
    )(*[pltpu.with_memory_space_constraint(a, pltpu.HBM) for a in lands], *after)
    return out[:ng], out[ng:2 * ng], out[2 * ng:2 * ng + n], out[2 * ng + n]


def _gather_wait(lands, halved, send_sem, recv_sem, after, name):
    n = len(lands)

    def body(*refs):
        ins, send, recv = refs[:n], refs[n], refs[n + 1]
        x, y, c = _coords()
        k = 2 * x + y
        for j in range(n):
            rows = _core_rows(ins[j], c)
            for m in (1, 2, 3):
                px, py, pk = _chip_peer(x, y, m)
                cp = pltpu.make_async_remote_copy(
                    src_ref=ins[j].at[k, rows, :] if halved else ins[j].at[k],
                    dst_ref=ins[j].at[pk, rows, :] if halved else ins[j].at[pk], send_sem=send.at[3 * j + m - 1],
                    recv_sem=recv.at[3 * j + m - 1], device_id=(px, py, c), device_id_type=MESH)
                cp.wait_send()
                cp.wait_recv()

    return pl.pallas_call(
        body, name=name, out_shape=[pltpu.HBM(a.shape, a.dtype) for a in lands],
        in_specs=[HBM] * n + [SEM, SEM, ANY], out_specs=[HBM] * n,
        input_output_aliases={j: j for j in range(n)},
        compiler_params=pltpu.CompilerParams(has_side_effects=EFFECT),
    )(*lands, send_sem, recv_sem, after)


def _sibling_fill(lands, name):
    n = len(lands)

    def body(*refs):
        ins = refs[:n]
        send_sems, recv_sems = refs[2 * n:]
        x, y, c = _coords()
        sends, recvs = [], []
        for t in range(n):
            for m in (1, 2, 3):
                _, _, pk = _chip_peer(x, y, m)
                for rows, lst in ((_core_rows(ins[t], c), sends), (_core_rows(ins[t], 1 - c), recvs)):
                    lst.append(pltpu.make_async_remote_copy(
                        src_ref=ins[t].at[pk, rows, :], dst_ref=ins[t].at[pk, rows, :],
                        send_sem=send_sems.at[3 * t + m - 1], recv_sem=recv_sems.at[3 * t + m - 1],
                        device_id=(x, y, 1 - c), device_id_type=MESH))
        for cp in sends:
            cp.start()
        for cp in recvs:
            cp.wait_recv()
        for cp in sends:
            cp.wait_send()

    return pl.pallas_call(
        body, name=name, in_specs=[ANY] * n, out_specs=[ANY] * n,
        out_shape=[jax.ShapeDtypeStruct(a.shape, a.dtype) for a in lands],
        input_output_aliases={t: t for t in range(n)},
        scratch_shapes=[pltpu.SemaphoreType.DMA((3 * n,)), pltpu.SemaphoreType.DMA((3 * n,))],
    )(*lands)


def _scatter_start(srcs, name, after=()):
    n, na = len(srcs), len(after)

    def body(*refs):
        ins, lands = refs[:n], refs[n:2 * n]
        send, recv = refs[2 * n + na], refs[2 * n + na + 1]
        token = refs[2 * n + na + 2 + 2 * n]
        x, y, c = _coords()
        k = 2 * x + y
        for t in range(n):
            for m in (1, 2, 3):
                px, py, pk = _chip_peer(x, y, m)
                pltpu.make_async_remote_copy(
                    src_ref=ins[t].at[pk], dst_ref=lands[t].at[k], send_sem=send.at[3 * t + m - 1],
                    recv_sem=recv.at[3 * t + m - 1], device_id=(px, py, c), device_id_type=MESH).start()
        token[...] = jnp.zeros_like(token)

    sem = pltpu.SemaphoreType.DMA((3 * n,))
    hbm = [pltpu.HBM(a.shape, a.dtype) for a in srcs]
    operands = list(srcs) + [lax.empty(a.shape, a.dtype) for a in srcs]
    out = pl.pallas_call(
        body, name=name, out_shape=[sem, sem] + hbm + hbm + [jax.ShapeDtypeStruct((8, 128), F32)],
        in_specs=[HBM] * (2 * n) + [ANY] * na,
        out_specs=[SEM, SEM] + [HBM] * (2 * n) + [pl.BlockSpec(memory_space=pltpu.VMEM)],
        input_output_aliases={t: 2 + t for t in range(2 * n)},
        compiler_params=pltpu.CompilerParams(has_side_effects=EFFECT),
    )(*[pltpu.with_memory_space_constraint(a, pltpu.HBM) for a in operands], *after)
    return out[0], out[1], out[2:2 + n], out[2 + n:2 + 2 * n], out[2 + 2 * n]


def _scatter_wait(srcs, lands, send_sem, recv_sem, after, name):
    n = len(srcs)

    def body(*refs):
        ins, land = refs[:n], refs[n:2 * n]
        send, recv = refs[2 * n], refs[2 * n + 1]
        x, y, c = _coords()
        for t in range(n):
            for m in (1, 2, 3):
                px, py, pk = _chip_peer(x, y, m)
                cp = pltpu.make_async_remote_copy(
                    src_ref=ins[t].at[pk], dst_ref=land[t].at[pk], send_sem=send.at[3 * t + m - 1],
                    recv_sem=recv.at[3 * t + m - 1], device_id=(px, py, c), device_id_type=MESH)
                cp.wait_send()
                cp.wait_recv()

    hbm = [pltpu.HBM(a.shape, a.dtype) for a in srcs]
    out = pl.pallas_call(
        body, name=name, out_shape=hbm + hbm, in_specs=[HBM] * (2 * n) + [SEM, SEM, ANY], out_specs=[HBM] * (2 * n),
        input_output_aliases={t: t for t in range(2 * n)},
        compiler_params=pltpu.CompilerParams(has_side_effects=EFFECT),
    )(*srcs, *lands, send_sem, recv_sem, after)
    return out[:n], out[n:]


def _sum_own_half(g, ra, kc, name):
    _, R, C = g.shape
    half = R // 2
    br = _row_block(half, C)
    nb = half // br

    def body(kc_ref, g_ref, ra_ref, o_ref):
        o_ref[...] = (g_ref[...] + ra_ref[...].astype(F32)).astype(BF16)

    return pl.pallas_call(
        body, name=name,
        grid_spec=pltpu.PrefetchScalarGridSpec(
            num_scalar_prefetch=1, grid=(NCHIP, nb),
            in_specs=[pl.BlockSpec((None, br, C), lambda j, i, kc: (j, kc[1] * nb + i, 0)),
                      pl.BlockSpec((None, br, C), lambda j, i, kc: (j, i, 0))],
            out_specs=pl.BlockSpec((None, br, C), lambda j, i, kc: (j, i, 0))),
        out_shape=jax.ShapeDtypeStruct((NCHIP, half, C), BF16),
        compiler_params=_params("parallel", "parallel"),
    )(kc, g, ra)


def _sum_chips(sa, rb, kc, name, after=None):
    _, half, C = rb.shape
    br = _row_block(half, C)
    nb = half // br
    extra = [] if after is None else [after]

    def body(kc_ref, own_ref, r1_ref, r2_ref, r3_ref, *rest):
        out, obuf, local_sems, send_sems, recv_sem = rest[-5:]
        i = pl.program_id(0)
        slot = i % 2
        x, y, c = _coords()

        def copies(i_, slot_):
            rows = out.at[pl.ds(pl.multiple_of((c * nb + i_) * br, 8), br), :]
            return (pltpu.make_async_copy(obuf.at[slot_], rows, local_sems.at[slot_]),
                    pltpu.make_async_remote_copy(src_ref=obuf.at[slot_], dst_ref=rows, send_sem=send_sems.at[slot_],
                                                 recv_sem=recv_sem, device_id=(x, y, 1 - c), device_id_type=MESH))

        @pl.when(i >= 2)
        def _():
            here, there = copies(i, slot)
            here.wait()
            there.wait_send()

        acc = own_ref[...].astype(F32) + r1_ref[...].astype(F32)
        obuf[slot] = (acc + r2_ref[...].astype(F32)) + r3_ref[...].astype(F32)
        here, there = copies(i, slot)
        here.start()
        there.start()

        @pl.when(i == nb - 1)
        def _():
            for s in range(min(2, nb)):
                here, there = copies(i, (i - s) % 2)
                here.wait()
                there.wait_send()
            theirs = out.at[pl.ds(pl.multiple_of((1 - c) * half, 8), half), :]
            pltpu.make_async_remote_copy(src_ref=theirs, dst_ref=theirs, send_sem=send_sems.at[0], recv_sem=recv_sem,
                                         device_id=(x, y, 1 - c), device_id_type=MESH).wait_recv()

    def slab(m):
        return pl.BlockSpec((None, br, C), lambda i, kc: (kc[0] ^ m, i, 0))

    return pl.pallas_call(
        body, name=name,
        grid_spec=pltpu.PrefetchScalarGridSpec(
            num_scalar_prefetch=1, grid=(nb,),
            in_specs=[slab(0), slab(1), slab(2), slab(3)] + [ANY] * len(extra),
            out_specs=ANY,
            scratch_shapes=[pltpu.VMEM((2, br, C), F32), pltpu.SemaphoreType.DMA((2,)), pltpu.SemaphoreType.DMA((2,)),
                            pltpu.SemaphoreType.DMA]),
        out_shape=jax.ShapeDtypeStruct((2 * half, C), F32), compiler_params=_params("arbitrary"),
    )(kc, sa, rb, rb, rb, *extra)


def _sum8(ga, name):
    _, R, C = ga.shape

    def body(g_ref, o_ref):
        acc = g_ref[0]
        for j in range(1, 8):
            acc = acc + g_ref[j]
        o_ref[...] = acc

    return pl.pallas_call(
        body, name=name, in_specs=[pl.BlockSpec((8, R, C), lambda: (0, 0, 0))],
        out_specs=pl.BlockSpec((R, C), lambda: (0, 0)), out_shape=jax.ShapeDtypeStruct((R, C), F32),
    )(ga)


ADA_COLS = 9 * D // NCHIP
ADA_BLK = 256


def _ada_mod(c_all, ada_w, ada_b, kidx):
    def body(k_ref, c_ref, w_ref, b_ref, o_ref):
        cv = c_ref[...]
        cs = cv * _sigmoid(cv)
        o_ref[...] = jnp.dot(cs, w_ref[...], precision=lax.Precision.HIGHEST,
                             preferred_element_type=F32) + b_ref[...]

    nblk = ADA_COLS // ADA_BLK
    return pl.pallas_call(
        body, name="ada_mod",
        grid_spec=pltpu.PrefetchScalarGridSpec(
            num_scalar_prefetch=1, grid=(nblk,),
            in_specs=[pl.BlockSpec((8, D), lambda j, k: (0, 0)),
                      pl.BlockSpec((D, ADA_BLK), lambda j, k: (0, j)),
                      pl.BlockSpec((1, ADA_BLK), lambda j, k: (0, k[0] * nblk + j))],
            out_specs=pl.BlockSpec((8, ADA_BLK), lambda j, k: (0, j))),
        out_shape=jax.ShapeDtypeStruct((8, ADA_COLS), F32),
        compiler_params=_params("parallel"),
    )(kidx, c_all, ada_w, ada_b)


def _ada_grad(c_all_t, dmod_all, kidx):
    def body(k_ref, ct_ref, dm_ref, o_ref):
        cv = ct_ref[...]
        cs = cv * _sigmoid(cv)
        acc = cs[:, 0:1] * dm_ref[0:1, :]
        for b in range(1, 8):
            acc = acc + cs[:, b:b + 1] * dm_ref[b:b + 1, :]
        o_ref[...] = acc

    nblk = ADA_COLS // ADA_BLK
    return pl.pallas_call(
        body, name="ada_grad",
        grid_spec=pltpu.PrefetchScalarGridSpec(
            num_scalar_prefetch=1, grid=(nblk,),
            in_specs=[pl.BlockSpec((D, 8), lambda j, k: (0, 0)),
                      pl.BlockSpec((8, ADA_BLK), lambda j, k: (0, k[0] * nblk + j))],
            out_specs=pl.BlockSpec((D, ADA_BLK), lambda j, k: (0, j))),
        out_shape=jax.ShapeDtypeStruct((D, ADA_COLS), F32),
        compiler_params=_params("parallel"),
    )(kidx, c_all_t, dmod_all)


BIG = ("ffn1_w_in", "ffn1_w_out", "mix_w_in", "hgrn_w_o", "conv_w_o", "mix_w_out", "ffn2_w_in", "ffn2_w_out")
ROW_SHARDED = ("ffn1_w_out", "hgrn_w_o", "conv_w_o", "mix_w_out", "ffn2_w_out")
GATHER_GROUPS = ((0, 1), (2,), (3, 4, 5), (6, 7))
GATHER_HALVED = (True, True, False, False)
PACK_LEN = {"ada_b": 9, "hgrn_lb": 2}
WEIGHTS = ("ada_w", "ada_b", "norm_ffn1", "ffn1_w_in", "ffn1_w_out", "norm_mix", "mix_w_in", "hgrn_lb", "hgrn_g",
           "hgrn_w_o", "conv_w", "conv_b", "conv_ln_g", "conv_ln_b", "conv_w_o", "mix_w_out", "norm_ffn2",
           "ffn2_w_in", "ffn2_w_out", "norm_final")
PACKED = ("ada_b", "norm_ffn1", "norm_mix", "hgrn_g", "conv_b", "conv_ln_g", "conv_ln_b", "norm_ffn2",
          "norm_final", "hgrn_lb")


def _pack_params(p, name):
    parts = [(p[n].reshape(PACK_LEN.get(n, 1), D), 0, PACK_LEN.get(n, 1), PACK_AT[n]) for n in PACKED]
    return _pack_rows(parts, PACK_ROWS, name)


def _step(w, m, v, x, c, tgt):
    xi, yi, ci = _coords()
    kidx = (2 * xi + yi).astype(jnp.int32).reshape(1)
    kc = jnp.stack([2 * xi + yi, ci]).astype(jnp.int32)
    me = 4 * xi + 2 * yi + ci

    c_all = _allgather8(jnp.broadcast_to(c, (8, D)), "gather_c")[:, 0, :]
    mod_cols = _ada_mod(c_all, w["ada_w"][0], w["ada_b"], kidx)
    mod_all = _allgather8(mod_cols, "gather_mod")
    mod = lax.dynamic_slice(mod_all, (0, me, 0), (8, 1, ADA_COLS))[::2].reshape(9, D)
    small = {n: w[n].reshape(-1, D) for n in ("norm_ffn1", "norm_mix", "hgrn_lb", "hgrn_g", "conv_b", "conv_ln_g",
                                              "conv_ln_b", "norm_ffn2", "norm_final")}
    small["conv_w"] = _allgather_conv_w(w["conv_w"][0])

    lands = [_cast_into_slot(w[n][0], kc, "cast_" + n) for n in BIG]
    sends, recvs, lands, _ = _gather_start(lands, GATHER_GROUPS, GATHER_HALVED, [mod, small["conv_w"]])
    ready = {}

    def weight(name, after):
        t = BIG.index(name)
        if t not in ready:
            gi = [t in grp for grp in GATHER_GROUPS].index(True)
            grp = GATHER_GROUPS[gi]
            outs = _gather_wait([lands[j] for j in grp], GATHER_HALVED[gi], sends[gi], recvs[gi], after,
                                "gather_weights_wait%d" % gi)
            if GATHER_HALVED[gi]:
                outs = _sibling_fill(outs, "gather_weights_fill%d" % gi)
            ready.update(zip(grp, outs))
        return ready[t].reshape(-1, D) if name in ROW_SHARDED else ready[t]

    grads, delta, new_m, new_v = {}, {}, {}, {}
    flight = []
    landed = []

    def settle(after):
        names, sa, rb, send, recv = flight.pop()
        sa, rb = _scatter_wait(sa, rb, send, recv, after, "rs_chip_wait_" + names[0])
        landed.append((names, sa, rb))

    def reduce(names, pairs, after=None):
        gs = [g.reshape(NCHIP, -1, g.shape[-1]) for g, _ in pairs]
        ra = [r.reshape(NCHIP, -1, r.shape[-1]) for _, r in pairs]
        sa = [_sum_own_half(g, r, kc, "rs_sum_pair_" + n) for g, r, n in zip(gs, ra, names)]
        if flight:
            settle(sa[0])
        send, recv, sa, rb, tok = _scatter_start(sa, "rs_chip_start_" + names[0], () if after is None else (after,))
        flight.append((names, sa, rb, send, recv))
        started.append(tok)
        return tok

    def adamw(n, after=None):
        shape = w[n].shape
        two = (shape[-2], shape[-1])
        d_, m_, v_ = _adamw(w[n].reshape(two), grads[n], m[n].reshape(two), v[n].reshape(two), "adamw_" + n, after)
        grads[n], delta[n], new_m[n], new_v[n] = (a.reshape(shape) for a in (grads[n], d_, m_, v_))
        return m_

    def finish(after=None):
        names, sa, rb = landed.pop(0)
        full = [_sum_chips(s, r, kc, "rs_sum_chips_" + n, after) for s, r, n in zip(sa, rb, names)]
        grads.update(zip(names, full))
        return [adamw(n) for n in names][-1]

    started = []

    smalls = []

    def reduce_small(packed):
        packed_all = _allgather8(packed, "gather_small_grads")
        smalls.extend([packed_all, _sum8(packed_all, "sum_small_grads")])
        return smalls[1]

    dx = _local_step(x[0], tgt[0], mod, small, kc, weight, reduce, reduce_small)
    packed_all, gsum = smalls
    loss = (0.5 / D) * jnp.sum(gsum[PACK_AT["loss"]])
    dmod_all = packed_all[:, 0:9, :].reshape(8, 9 * D)
    grads["ada_w"] = _ada_grad(c_all.T, dmod_all, kidx)
    grads["conv_w"] = lax.dynamic_slice(gsum, (PACK_AT["conv_w"], kidx[0] * (D // NCHIP)), (CONV_K, D // NCHIP))

    tok = started[-1]
    adamw("ada_w", tok)
    adamw("conv_w")
    pw, pm, pv = (_pack_params(p, "pack_" + s) for p, s in ((w, "w"), (m, "m"), (v, "v")))
    pd, pnm, pnv = _adamw(pw, gsum, pm, pv, "adamw_small", tok)
    last = pnv
    while landed:
        last = finish(tok)
    settle(last)
    finish()
    for n in PACKED:
        rows = slice(PACK_AT[n], PACK_AT[n] + PACK_LEN.get(n, 1))
        for dst, src in ((grads, gsum), (delta, pd), (new_m, pnm), (new_v, pnv)):
            dst[n] = src[rows].reshape(w[n].shape)

    outs = [loss, dx[None]]
    for d in (grads, delta, new_m, new_v):
        outs += [d[n] for n in WEIGHTS]
    return tuple(outs)


def _allgather_conv_w(cw):
    padded = jnp.pad(cw, ((0, 32 - CONV_K), (0, 0)))
    parts = _allgather8(padded, "gather_conv_w")
    return jnp.concatenate([parts[2 * j] for j in range(NCHIP)], axis=1)


def kernel(x, c, ada_w, ada_b, norm_ffn1, ffn1_w_in, ffn1_w_out, norm_mix, mix_w_in, hgrn_lb, hgrn_g, hgrn_w_o, conv_w, conv_b, conv_ln_g, conv_ln_b, conv_w_o, mix_w_out, norm_ffn2, ffn2_w_in, ffn2_w_out, norm_final, loss_target, m_ada_w, m_ada_b, m_norm_ffn1, m_ffn1_w_in, m_ffn1_w_out, m_norm_mix, m_mix_w_in, m_hgrn_lb, m_hgrn_g, m_hgrn_w_o, m_conv_w, m_conv_b, m_conv_ln_g, m_conv_ln_b, m_conv_w_o, m_mix_w_out, m_norm_ffn2, m_ffn2_w_in, m_ffn2_w_out, m_norm_final, v_ada_w, v_ada_b, v_norm_ffn1, v_ffn1_w_in, v_ffn1_w_out, v_norm_mix, v_mix_w_in, v_hgrn_lb, v_hgrn_g, v_hgrn_w_o, v_conv_w, v_conv_b, v_conv_ln_g, v_conv_ln_b, v_conv_w_o, v_mix_w_out, v_norm_ffn2, v_ffn2_w_in, v_ffn2_w_out, v_norm_final):
    w = dict(ada_w=ada_w, ada_b=ada_b, norm_ffn1=norm_ffn1, ffn1_w_in=ffn1_w_in, ffn1_w_out=ffn1_w_out,
             norm_mix=norm_mix, mix_w_in=mix_w_in, hgrn_lb=hgrn_lb, hgrn_g=hgrn_g, hgrn_w_o=hgrn_w_o, conv_w=conv_w,
             conv_b=conv_b, conv_ln_g=conv_ln_g, conv_ln_b=conv_ln_b, conv_w_o=conv_w_o, mix_w_out=mix_w_out,
             norm_ffn2=norm_ffn2, ffn2_w_in=ffn2_w_in, ffn2_w_out=ffn2_w_out, norm_final=norm_final)
    m = dict(ada_w=m_ada_w, ada_b=m_ada_b, norm_ffn1=m_norm_ffn1, ffn1_w_in=m_ffn1_w_in, ffn1_w_out=m_ffn1_w_out,
             norm_mix=m_norm_mix, mix_w_in=m_mix_w_in, hgrn_lb=m_hgrn_lb, hgrn_g=m_hgrn_g, hgrn_w_o=m_hgrn_w_o,
             conv_w=m_conv_w, conv_b=m_conv_b, conv_ln_g=m_conv_ln_g, conv_ln_b=m_conv_ln_b, conv_w_o=m_conv_w_o,
             mix_w_out=m_mix_w_out, norm_ffn2=m_norm_ffn2, ffn2_w_in=m_ffn2_w_in, ffn2_w_out=m_ffn2_w_out,
             norm_final=m_norm_final)
    v = dict(ada_w=v_ada_w, ada_b=v_ada_b, norm_ffn1=v_norm_ffn1, ffn1_w_in=v_ffn1_w_in, ffn1_w_out=v_ffn1_w_out,
             norm_mix=v_norm_mix, mix_w_in=v_mix_w_in, hgrn_lb=v_hgrn_lb, hgrn_g=v_hgrn_g, hgrn_w_o=v_hgrn_w_o,
             conv_w=v_conv_w, conv_b=v_conv_b, conv_ln_g=v_conv_ln_g, conv_ln_b=v_conv_ln_b, conv_w_o=v_conv_w_o,
             mix_w_out=v_mix_w_out, norm_ffn2=v_norm_ffn2, ffn2_w_in=v_ffn2_w_in, ffn2_w_out=v_ffn2_w_out,
             norm_final=v_norm_final)
    return _step(w, m, v, x, c, loss_target)
```

```python
import functools

import jax
import jax.numpy as jnp
from jax import lax
from jax.experimental import pallas as pl
from jax.experimental.pallas import tpu as pltpu

F32 = jnp.float32
BF16 = jnp.bfloat16

D = 1024
DFF = 2816
NCHIP = 4
FSH = 2 * DFF // NCHIP
HEADS = 8
DK = 128
CHUNK = 64
CONV_K = 31
HALO = 32
EPS = 1e-6
TB = 256
CB = 512
DW_TOKENS = 2048
VMEM_LIMIT = 56 * 1024 * 1024

ADAM_LR = 0.001
ADAM_B1 = 0.9
ADAM_B2 = 0.999
ADAM_EPS = 1e-08
ADAM_WD = 0.01
ADAM_STEP = 10

MESH = pl.DeviceIdType.MESH
ANY = pl.BlockSpec(memory_space=pl.ANY)


def _params(*sem):
    return pltpu.CompilerParams(dimension_semantics=sem, vmem_limit_bytes=VMEM_LIMIT)


def _sigmoid(x):
    return 0.5 * jnp.tanh(0.5 * x) + 0.5


def _dsilu(x, sg):
    return sg * (1.0 + x * (1.0 - sg))


def _nt(a, b):
    return lax.dot_general(a, b, (((1,), (1,)), ((), ())), preferred_element_type=F32)


def _tn(a, b):
    return lax.dot_general(a, b, (((0,), (0,)), ((), ())), preferred_element_type=F32)


def _nn(a, b):
    return jnp.dot(a, b, preferred_element_type=F32)


def _colsum(x):
    return jnp.sum(x, axis=0, keepdims=True)


def _rms_fwd(x, gn, sc, sh):
    r = lax.rsqrt(jnp.mean(x * x, axis=-1, keepdims=True) + EPS)
    n = x * r
    h = (n * gn) * (1.0 + sc) + sh
    return r, n, h


def _rms_bwd(dh, r, n, gn, sc, acc_ref):
    acc_ref[0:1, :] += _colsum(dh)
    acc_ref[1:2, :] += _colsum(dh * (n * gn))
    dng = dh * (1.0 + sc)
    acc_ref[3:4, :] += _colsum(dng * n)
    dn = dng * gn
    return r * (dn - n * jnp.mean(dn * n, axis=-1, keepdims=True))


def _loss_head(x, tgt, gf, acc_ref):
    r = lax.rsqrt(jnp.mean(x * x, axis=-1, keepdims=True) + EPS)
    n = x * r
    err = n * gf - tgt
    acc_ref[1:2, :] += _colsum(err * err)
    dy = err * (1.0 / D)
    acc_ref[0:1, :] += _colsum(dy * n)
    dn = dy * gf
    return r * (dn - n * jnp.mean(dn * n, axis=-1, keepdims=True))


def _ffn_fwd(x, vec, w_in, w_out, name, head=None):
    T = x.shape[0]
    nh = 0 if head is None else 2

    def body(x_ref, vec_ref, *rest):
        win_hbm, wout_hbm = rest[nh:nh + 2]
        xo_ref, h_ref, a_ref, b_ref, s_ref, f_ref = rest[nh + 2:nh + 8]
        win, wout = rest[-2:]

        @pl.when(pl.program_id(0) == 0)
        def _():
            pltpu.sync_copy(win_hbm, win)
            pltpu.sync_copy(wout_hbm, wout)
            if head is not None:
                rest[nh + 8][...] = jnp.zeros((8, D), F32)

        x = x_ref[...]
        sh, sc, gate, gn = vec_ref[0:1, :], vec_ref[1:2, :], vec_ref[2:3, :], vec_ref[3:4, :]
        _, _, h = _rms_fwd(x, gn, sc, sh)
        hb = h.astype(BF16)
        h_ref[...] = hb
        f = jnp.zeros((TB, D), F32)
        for j in range(2):
            cols = slice(j * FSH, (j + 1) * FSH)
            a = _nn(hb, win[j])
            b = _nn(hb, win[2 + j])
            s = (a * _sigmoid(a) * b).astype(BF16)
            a_ref[:, cols] = a.astype(BF16)
            b_ref[:, cols] = b.astype(BF16)
            s_ref[:, cols] = s
            f = f + _nn(s, wout[cols, :])
        xo = x + (0.5 * gate) * f
        f_ref[...] = f.astype(BF16)
        if head is None:
            xo_ref[...] = xo
        else:
            xo_ref[...] = _loss_head(xo, rest[0][...], rest[1][0:1, :], rest[nh + 8])

    row = lambda w: pl.BlockSpec((TB, w), lambda i: (i, 0))
    vec8 = pl.BlockSpec((8, D), lambda i: (0, 0))
    acc = [] if head is None else [jax.ShapeDtypeStruct((8, D), F32)]
    return pl.pallas_call(
        body, name=name, grid=(T // TB,),
        in_specs=[row(D), vec8] + ([] if head is None else [row(D), vec8]) + [ANY, ANY],
        out_specs=[row(D), row(D), row(DFF), row(DFF), row(DFF), row(D)] + [vec8] * len(acc),
        out_shape=[jax.ShapeDtypeStruct((T, D), F32), jax.ShapeDtypeStruct((T, D), BF16),
                   jax.ShapeDtypeStruct((T, DFF), BF16), jax.ShapeDtypeStruct((T, DFF), BF16),
                   jax.ShapeDtypeStruct((T, DFF), BF16), jax.ShapeDtypeStruct((T, D), BF16)] + acc,
        scratch_shapes=[pltpu.VMEM((NCHIP, D, FSH), BF16), pltpu.VMEM((DFF, D), BF16)],
        compiler_params=_params("arbitrary"),
    )(x, vec, *([] if head is None else list(head)), w_in, w_out)


def _ffn_bwd(dxo, x, vec, a, b, f, w_in, w_out, name):
    T = x.shape[0]

    def body(dxo_ref, x_ref, vec_ref, a_ref, b_ref, f_ref, win_hbm, wout_hbm,
             dx_ref, df_ref, dab_ref, acc_ref, win, wout):
        @pl.when(pl.program_id(0) == 0)
        def _():
            pltpu.sync_copy(win_hbm, win)
            pltpu.sync_copy(wout_hbm, wout)
            acc_ref[...] = jnp.zeros_like(acc_ref)

        dxo = dxo_ref[...]
        x = x_ref[...]
        sh, sc, gate, gn = vec_ref[0:1, :], vec_ref[1:2, :], vec_ref[2:3, :], vec_ref[3:4, :]
        r, n, _ = _rms_fwd(x, gn, sc, sh)
        acc_ref[2:3, :] += _colsum(0.5 * f_ref[...].astype(F32) * dxo)
        dfb = ((0.5 * gate) * dxo).astype(BF16)
        df_ref[...] = dfb
        dh = jnp.zeros((TB, D), F32)
        for j in range(2):
            cols = slice(j * FSH, (j + 1) * FSH)
            ds = _nt(dfb, wout[cols, :])
            av = a_ref[:, cols].astype(F32)
            bv = b_ref[:, cols].astype(F32)
            sg = _sigmoid(av)
            da = (ds * bv * _dsilu(av, sg)).astype(BF16)
            db = (ds * (av * sg)).astype(BF16)
            dab_ref[j] = da
            dab_ref[2 + j] = db
            dh = dh + _nt(da, win[j]) + _nt(db, win[2 + j])
        dx_ref[...] = dxo + _rms_bwd(dh, r, n, gn, sc, acc_ref)

    row = lambda w: pl.BlockSpec((TB, w), lambda i: (i, 0))
    vec8 = pl.BlockSpec((8, D), lambda i: (0, 0))
    return pl.pallas_call(
        body, name=name, grid=(T // TB,),
        in_specs=[row(D), row(D), vec8, row(DFF), row(DFF), row(D), ANY, ANY],
        out_specs=[row(D), pl.BlockSpec((None, TB, D), lambda i: (0, i, 0)),
                   pl.BlockSpec((NCHIP, TB, FSH), lambda i: (0, i, 0)), vec8],
        out_shape=[jax.ShapeDtypeStruct((T, D), F32), jax.ShapeDtypeStruct((1, T, D), BF16),
                   jax.ShapeDtypeStruct((NCHIP, T, FSH), BF16), jax.ShapeDtypeStruct((8, D), F32)],
        scratch_shapes=[pltpu.VMEM((NCHIP, D, FSH), BF16), pltpu.VMEM((DFF, D), BF16)],
        compiler_params=_params("arbitrary"),
    )(dxo, x, vec, a, b, f, w_in, w_out)


def _mm_tn(a, b3, hp, kc, shard_rows, name, into=None, slab=0, slabs=None):
    T, M = a.shape
    P, _, N = b3.shape
    tm = M if M <= 1408 else M // 2
    tk = min(T, DW_TOKENS)
    nk = T // tk
    ni = M // tm
    slabs = P // hp if slabs is None else slabs
    half = shard_rows // 2
    extra = [] if into is None else list(into)

    def body(kc_ref, a_ref, b_ref, *rest):
        o_ref, ra_ref, hbuf, send_sems, recv_sem = rest[-5:]
        p, i, k = pl.program_id(0), pl.program_id(1), pl.program_id(2)
        x, y, c = _coords()
        step = p * ni + i
        slot = step % 2

        def send(p_, i_, slot_):
            dst = ra_ref.at[slab + p_ // hp, pl.ds(pl.multiple_of(i_ * (tm // 2), 8), tm // 2),
                            pl.ds(pl.multiple_of((p_ % hp) * N, LANES), N)]
            return pltpu.make_async_remote_copy(
                src_ref=hbuf.at[slot_], dst_ref=dst, send_sem=send_sems.at[slot_], recv_sem=recv_sem,
                device_id=(x, y, 1 - c), device_id_type=MESH)

        @pl.when(k == 0)
        def _():
            o_ref[...] = jnp.zeros_like(o_ref)

        o_ref[...] += _tn(a_ref[...], b_ref[...])

        @pl.when(k == nk - 1)
        def _():
            @pl.when(step >= 2)
            def _():
                send(p, i, slot).wait_send()

            for j in range(tm // shard_rows):
                start = pl.multiple_of(j * shard_rows + (1 - kc_ref[1]) * half, 8)
                hbuf[slot, j * half:(j + 1) * half, :] = o_ref[pl.ds(start, half), :].astype(BF16)
            send(p, i, slot).start()

        @pl.when((step == P * ni - 1) & (k == nk - 1))
        def _():
            for s in range(min(2, P * ni)):
                send(p, i, (step - s) % 2).wait_send()
            mine = ra_ref.at[slab:slab + P // hp]
            pltpu.make_async_remote_copy(src_ref=mine, dst_ref=mine, send_sem=send_sems.at[0], recv_sem=recv_sem,
                                         device_id=(x, y, 1 - c), device_id_type=MESH).wait_recv()

    return pl.pallas_call(
        body, name=name,
        grid_spec=pltpu.PrefetchScalarGridSpec(
            num_scalar_prefetch=1, grid=(P, ni, nk),
            in_specs=[pl.BlockSpec((tk, tm), lambda p, i, k, kc: (k, i)),
                      pl.BlockSpec((None, tk, N), lambda p, i, k, kc: (p, k, 0))] + [ANY] * len(extra),
            out_specs=[pl.BlockSpec((None, tm, N), lambda p, i, k, kc: (slab + p // hp, i, p % hp)), ANY],
            scratch_shapes=[pltpu.VMEM((2, tm // 2, N), BF16), pltpu.SemaphoreType.DMA((2,)),
                            pltpu.SemaphoreType.DMA]),
        out_shape=[jax.ShapeDtypeStruct((slabs, M, hp * N), F32), jax.ShapeDtypeStruct((slabs, M // 2, hp * N), BF16)],
        input_output_aliases={} if into is None else {3: 0, 4: 1},
        compiler_params=_params("arbitrary", "arbitrary", "arbitrary"),
    )(kc, a, b3, *extra)


def _mix_proj_fwd(x, vec, w_in):
    T = x.shape[0]

    def body(x_ref, vec_ref, w_hbm, h_ref, qr_ref, g_ref, k_ref, v_ref, og_ref, u_ref, ua_ref, ub_ref,
             sa_ref, sb_ref, w):
        @pl.when(pl.program_id(0) == 0)
        def _():
            pltpu.sync_copy(w_hbm, w)

        x = x_ref[...]
        sh, sc, gn, lb = vec_ref[0:1, :], vec_ref[1:2, :], vec_ref[3:4, :], vec_ref[4:5, :]
        _, _, h = _rms_fwd(x, gn, sc, sh)
        hb = h.astype(BF16)
        h_ref[...] = hb
        p = _nn(hb, w[0])
        qr_ref[...] = p[:, :D].astype(BF16)
        fg = lb + (1.0 - lb) * _sigmoid(p[:, D:])
        g_ref[...] = jnp.log(fg)
        k_ref[...] = (1.0 - fg).astype(BF16)
        p = _nn(hb, w[1])
        v_ref[...] = p[:, :D].astype(BF16)
        og_ref[...] = p[:, D:].astype(BF16)
        p = _nn(hb, w[2])
        ua, ub = p[:, :D], p[:, D:]
        u_ref[...] = ua * _sigmoid(ub)
        ua_ref[...] = ua.astype(BF16)
        ub_ref[...] = ub.astype(BF16)
        p = _nn(hb, w[3])
        sa_ref[...] = _sigmoid(p[:, :D]).astype(BF16)
        sb_ref[...] = _sigmoid(p[:, D:]).astype(BF16)

    row = pl.BlockSpec((TB, D), lambda i: (i, 0))
    bf = jax.ShapeDtypeStruct((T, D), BF16)
    f32 = jax.ShapeDtypeStruct((T, D), F32)
    return pl.pallas_call(
        body, name="mix_proj_fwd", grid=(T // TB,),
        in_specs=[row, pl.BlockSpec((8, D), lambda i: (0, 0)), ANY],
        out_specs=[row] * 11,
        out_shape=[bf, bf, f32, bf, bf, bf, f32, bf, bf, bf, bf],
        scratch_shapes=[pltpu.VMEM((NCHIP, D, 2 * D), BF16)],
        compiler_params=_params("arbitrary"),
    )(x, vec, w_in)


def _mix_proj_bwd(dxo, x, vec, dpa, dpb, dpc, w_in):
    T = x.shape[0]

    def body(dxo_ref, x_ref, vec_ref, dpa_ref, dpb_ref, dpc_ref, w_hbm, dx_ref, acc_ref, w):
        @pl.when(pl.program_id(0) == 0)
        def _():
            pltpu.sync_copy(w_hbm, w)
            acc_ref[...] = jnp.zeros_like(acc_ref)

        x = x_ref[...]
        sh, sc, gn = vec_ref[0:1, :], vec_ref[1:2, :], vec_ref[3:4, :]
        r, n, _ = _rms_fwd(x, gn, sc, sh)
        dh = jnp.zeros((TB, D), F32)
        for p in range(8):
            src = dpa_ref[p] if p < 4 else (dpb_ref[p - 4] if p < 6 else dpc_ref[p - 6])
            dh = dh + _nt(src, w[p // 2, :, (p % 2) * D:(p % 2 + 1) * D])
        dx_ref[...] = dxo_ref[...] + _rms_bwd(dh, r, n, gn, sc, acc_ref)

    row = pl.BlockSpec((TB, D), lambda i: (i, 0))
    vec8 = pl.BlockSpec((8, D), lambda i: (0, 0))
    stack = lambda k: pl.BlockSpec((k, TB, D), lambda i: (0, i, 0))
    return pl.pallas_call(
        body, name="mix_proj_bwd", grid=(T // TB,),
        in_specs=[row, row, vec8, stack(4), stack(2), stack(2), ANY],
        out_specs=[row, vec8],
        out_shape=[jax.ShapeDtypeStruct((T, D), F32), jax.ShapeDtypeStruct((8, D), F32)],
        scratch_shapes=[pltpu.VMEM((NCHIP, D, 2 * D), BF16)],
        compiler_params=_params("arbitrary"),
    )(dxo, x, vec, dpa, dpb, dpc, w_in)


def _tri(lower):
    r = lax.broadcasted_iota(jnp.int32, (CHUNK, CHUNK), 0)
    c = lax.broadcasted_iota(jnp.int32, (CHUNK, CHUNK), 1)
    return (c <= r) if lower else (c >= r)


def _cumsum_rows(mask, g):
    hi = g.astype(BF16)
    rest = g - hi.astype(F32)
    mid = rest.astype(BF16)
    low = (rest - mid.astype(F32)).astype(BF16)
    n = g.shape[1]
    p = _nn(mask.astype(BF16), jnp.concatenate([hi, mid, low], axis=1))
    return (p[:, 2 * n:] + p[:, n:2 * n]) + p[:, :n]


def _chunk_decay(low, g, nck):
    bs, mids, lasts = [], [], []
    for c in range(nck):
        gc = g[c * CHUNK:(c + 1) * CHUNK]
        bs.append(_cumsum_rows(low, gc))
        mids.append(_colsum(gc[0:CHUNK // 2]))
        lasts.append(_colsum(gc))
    spread = lambda rows: jnp.concatenate([jnp.broadcast_to(r, (CHUNK, DK)) for r in rows], axis=0)
    return jnp.concatenate(bs, axis=0), spread(mids), spread(lasts), lasts


def _hgrn_fwd(qr, g, k, v, og, vec):
    T = qr.shape[0]
    nck = CB // CHUNK

    def body(qr_ref, g_ref, k_ref, v_ref, og_ref, vec_ref, out_ref, o_ref, st_ref, state):
        @pl.when(pl.program_id(1) == 0)
        def _():
            state[...] = jnp.zeros_like(state)

        low = _tri(True)
        qv = qr_ref[...].astype(F32)
        q = qv * _sigmoid(qv) * (DK ** -0.5)
        kk = k_ref[...].astype(F32)
        vb = v_ref[...]
        b, mid, last, lasts = _chunk_decay(low, g_ref[...], nck)
        qt = (q * jnp.exp(b - mid)).astype(BF16)
        kt = (kk * jnp.exp(mid - b)).astype(BF16)
        qe = (q * jnp.exp(b)).astype(BF16)
        kd = (kk * jnp.exp(last - b)).astype(BF16)
        intra, grow = [], []
        for c in range(nck):
            r = slice(c * CHUNK, (c + 1) * CHUNK)
            att = jnp.where(low, _nt(qt[r], kt[r]), 0.0).astype(BF16)
            intra.append(_nn(att, vb[r]))
            grow.append(_tn(vb[r], kd[r]))
        st = state[...]
        inter = []
        for c in range(nck):
            stb = st.astype(BF16)
            st_ref[c] = stb
            inter.append(_nt(qe[c * CHUNK:(c + 1) * CHUNK], stb))
            st = st * jnp.exp(lasts[c]) + grow[c]
        state[...] = st
        o = jnp.concatenate(intra, axis=0) + jnp.concatenate(inter, axis=0)
        o_ref[...] = o
        ogv = og_ref[...].astype(F32)
        rms = lax.rsqrt(jnp.mean(o * o, axis=-1, keepdims=True) + EPS)
        out_ref[...] = (o * rms * vec_ref[5:6, :] * (ogv * _sigmoid(ogv))).astype(BF16)

    blk = pl.BlockSpec((CB, DK), lambda h, i: (i, h))
    return pl.pallas_call(
        body, name="hgrn_fwd", grid=(HEADS, T // CB),
        in_specs=[blk, blk, blk, blk, blk, pl.BlockSpec((8, DK), lambda h, i: (0, h))],
        out_specs=[blk, blk, pl.BlockSpec((None, nck, DK, DK), lambda h, i: (h, i, 0, 0))],
        out_shape=[jax.ShapeDtypeStruct((T, D), BF16), jax.ShapeDtypeStruct((T, D), F32),
                   jax.ShapeDtypeStruct((HEADS, T // CHUNK, DK, DK), BF16)],
        scratch_shapes=[pltpu.VMEM((DK, DK), F32)],
        compiler_params=_params("parallel", "arbitrary"),
    )(qr, g, k, v, og, vec)


def _hgrn_bwd(dout, og, qr, g, k, v, o, st, vec):
    T = qr.shape[0]
    nck = CB // CHUNK
    nb = T // CB

    def body(dout_ref, og_ref, qr_ref, g_ref, k_ref, v_ref, o_ref, st_ref, vec_ref,
             dp_ref, acc_ref, dstate):
        @pl.when(pl.program_id(1) == 0)
        def _():
            dstate[...] = jnp.zeros_like(dstate)
            acc_ref[...] = jnp.zeros_like(acc_ref)

        o = o_ref[...]
        ogv = og_ref[...].astype(F32)
        dout = dout_ref[...].astype(F32)
        hg = vec_ref[5:6, :]
        sgo = _sigmoid(ogv)
        rms = lax.rsqrt(jnp.mean(o * o, axis=-1, keepdims=True) + EPS)
        ohat = o * rms
        dp_ref[3] = (dout * (ohat * hg) * _dsilu(ogv, sgo)).astype(BF16)
        don = dout * (ogv * sgo)
        acc_ref[0:1, :] += _colsum(don * ohat)
        dohat = don * hg
        dob = (rms * (dohat - ohat * jnp.mean(dohat * ohat, axis=-1, keepdims=True))).astype(BF16)

        low = _tri(True)
        upp = _tri(False)
        lb = vec_ref[4:5, :]
        qv = qr_ref[...].astype(F32)
        sgq = _sigmoid(qv)
        q = qv * sgq * (DK ** -0.5)
        kk = k_ref[...].astype(F32)
        vb = v_ref[...]
        gv = g_ref[...]
        b, mid, last, lasts = _chunk_decay(low, gv, nck)
        eq = jnp.exp(b - mid)
        ek = jnp.exp(mid - b)
        eb = jnp.exp(b)
        ed = jnp.exp(last - b)
        qtb, ktb, qeb, kdb = ((t).astype(BF16) for t in (q * eq, kk * ek, q * eb, kk * ed))
        rows = [slice(c * CHUNK, (c + 1) * CHUNK) for c in range(nck)]

        dv1, dqt, dkt, dqe, grow = [], [], [], [], []
        for c, r in enumerate(rows):
            att = jnp.where(low, _nt(qtb[r], ktb[r]), 0.0).astype(BF16)
            datt = jnp.where(low, _nt(dob[r], vb[r]), 0.0).astype(BF16)
            dv1.append(_tn(att, dob[r]))
            dqt.append(_nn(datt, ktb[r]))
            dkt.append(_tn(datt, qtb[r]))
            dqe.append(_nn(dob[r], st_ref[c]))
            grow.append(_tn(dob[r], qeb[r]))
        ds = dstate[...]
        ds1b, dl_state = [None] * nck, [None] * nck
        for c in reversed(range(nck)):
            el = jnp.exp(lasts[c])
            ds1b[c] = ds.astype(BF16)
            dl_state[c] = el * _colsum(ds * st_ref[c].astype(F32))
            ds = ds * el + grow[c]
        dstate[...] = ds
        dkd = jnp.concatenate([_nn(vb[r], ds1b[c]) for c, r in enumerate(rows)], axis=0)
        dv = jnp.concatenate(dv1, axis=0) + jnp.concatenate([_nt(kdb[r], ds1b[c]) for c, r in enumerate(rows)], axis=0)
        dqt, dkt, dqe = (jnp.concatenate(t, axis=0) for t in (dqt, dkt, dqe))
        dq = dqt * eq + dqe * eb
        dk = dkt * ek + dkd * ed
        dkdkd = dkd * kdb.astype(F32)
        db = dqt * qtb.astype(F32) - dkt * ktb.astype(F32) + dqe * qeb.astype(F32) - dkdkd
        dg = jnp.concatenate([_cumsum_rows(upp, db[r]) + (_colsum(dkdkd[r]) + dl_state[c])
                              for c, r in enumerate(rows)], axis=0)
        fg = jnp.exp(gv)
        dfg = dg * jnp.exp(-gv) - dk
        one_m_sig = (1.0 - fg) * (1.0 / (1.0 - lb))
        dp_ref[0] = (dq * (DK ** -0.5) * _dsilu(qv, sgq)).astype(BF16)
        dp_ref[1] = (dfg * (fg - lb) * one_m_sig).astype(BF16)
        dp_ref[2] = dv.astype(BF16)
        dlb = _colsum(dfg * one_m_sig) * (lb * (1.0 - lb))
        acc_ref[1:2, :] += dlb
        acc_ref[2:3, :] -= dlb

    blk = pl.BlockSpec((CB, DK), lambda h, i: (nb - 1 - i, h))
    return pl.pallas_call(
        body, name="hgrn_bwd", grid=(HEADS, nb),
        in_specs=[blk, blk, blk, blk, blk, blk, blk,
                  pl.BlockSpec((None, nck, DK, DK), lambda h, i: (h, nb - 1 - i, 0, 0)),
                  pl.BlockSpec((8, DK), lambda h, i: (0, h))],
        out_specs=[pl.BlockSpec((4, CB, DK), lambda h, i: (0, nb - 1 - i, h)),
                   pl.BlockSpec((8, DK), lambda h, i: (0, h))],
        out_shape=[jax.ShapeDtypeStruct((4, T, D), BF16), jax.ShapeDtypeStruct((8, D), F32)],
        scratch_shapes=[pltpu.VMEM((DK, DK), F32)],
        compiler_params=_params("parallel", "arbitrary"),
    )(dout, og, qr, g, k, v, o, st, vec)


def _ln_fwd(uc, lg, lbias):
    mu = jnp.mean(uc, axis=-1, keepdims=True)
    xc = uc - mu
    rstd = lax.rsqrt(jnp.mean(xc * xc, axis=-1, keepdims=True) + EPS)
    z = xc * rstd
    return rstd, z, z * lg + lbias


LANES = 128
SUBLANES = 8
CONV_ROWS = 128


def _lane_tiles():
    return [slice(l * LANES, (l + 1) * LANES) for l in range(D // LANES)]


def _row_shifts(x):
    n = x.shape[0]
    return [x] + [pltpu.roll(x, n - r, axis=0) for r in range(1, SUBLANES)]


def _shifted_rows(shifted, start, rows=TB):
    a, r = divmod(start, SUBLANES)
    return shifted[r][a * SUBLANES:a * SUBLANES + rows]


def _conv_fwd(u, cw, cvec):
    T = u.shape[0]
    per = TB // HALO

    def body(u_ref, halo_ref, cw_ref, cvec_ref, us_ref, uc_ref, pad):
        i = pl.program_id(0)
        pad[0:HALO, :] = jnp.where(i > 0, halo_ref[...], 0.0)
        pad[HALO:, :] = u_ref[...]
        for lanes in _lane_tiles():
            shifted = _row_shifts(pad[:, lanes])
            acc = jnp.broadcast_to(cvec_ref[0:1, lanes], (TB, LANES))
            for j in range(CONV_K):
                acc = acc + cw_ref[j:j + 1, lanes] * _shifted_rows(shifted, HALO - (CONV_K - 1) + j)
            uc_ref[:, lanes] = acc
        _, _, ul = _ln_fwd(uc_ref[...], cvec_ref[1:2, :], cvec_ref[2:3, :])
        us_ref[...] = (ul * _sigmoid(ul)).astype(BF16)

    row = pl.BlockSpec((TB, D), lambda i: (i, 0))
    return pl.pallas_call(
        body, name="conv_fwd", grid=(T // TB,),
        in_specs=[row, pl.BlockSpec((HALO, D), lambda i: (jnp.maximum(i * per - 1, 0), 0)),
                  pl.BlockSpec((32, D), lambda i: (0, 0)), pl.BlockSpec((8, D), lambda i: (0, 0))],
        out_specs=[row, row],
        out_shape=[jax.ShapeDtypeStruct((T, D), BF16), jax.ShapeDtypeStruct((T, D), F32)],
        scratch_shapes=[pltpu.VMEM((TB + HALO, D), F32)],
        compiler_params=_params("parallel"),
    )(u, u, cw, cvec)


def _conv_bwd_taps(duc, u, ua, ub, cw):
    T = u.shape[0]
    per = TB // HALO
    nblk = T // TB

    def body(duc_ref, dnext_ref, u_ref, uprev_ref, ua_ref, ub_ref, cw_ref, dp_ref, dcw_ref, upad, dpad, dcw):
        i = pl.program_id(0)

        @pl.when(i == 0)
        def _():
            dcw[...] = jnp.zeros_like(dcw)

        upad[0:HALO, :] = jnp.where(i > 0, uprev_ref[...], 0.0)
        upad[HALO:, :] = u_ref[...]
        dpad[0:TB, :] = duc_ref[...]
        dpad[TB:, :] = jnp.where(i < nblk - 1, dnext_ref[...], 0.0)
        for lanes in _lane_tiles():
            ushift = _row_shifts(upad[:, lanes])
            dshift = _row_shifts(dpad[:, lanes])
            for r0 in range(0, TB, CONV_ROWS):
                rows = slice(r0, r0 + CONV_ROWS)
                duc = duc_ref[rows, lanes]
                du = jnp.zeros((CONV_ROWS, LANES), F32)
                for j in range(CONV_K):
                    prod = duc * _shifted_rows(ushift, r0 + HALO - (CONV_K - 1) + j, CONV_ROWS)
                    dcw[j, :, lanes] += jnp.sum(prod.reshape(CONV_ROWS // SUBLANES, SUBLANES, LANES), axis=0)
                    du = du + cw_ref[j:j + 1, lanes] * _shifted_rows(dshift, r0 + CONV_K - 1 - j, CONV_ROWS)
                ua = ua_ref[rows, lanes].astype(F32)
                sg = _sigmoid(ub_ref[rows, lanes].astype(F32))
                dp_ref[0, rows, lanes] = (du * sg).astype(BF16)
                dp_ref[1, rows, lanes] = (du * ua * sg * (1.0 - sg)).astype(BF16)

        @pl.when(i == nblk - 1)
        def _():
            dcw_ref[...] = jnp.sum(dcw[...], axis=1)

    row = pl.BlockSpec((TB, D), lambda i: (i, 0))
    return pl.pallas_call(
        body, name="conv_bwd_taps", grid=(nblk,),
        in_specs=[row, pl.BlockSpec((HALO, D), lambda i: (jnp.minimum((i + 1) * per, T // HALO - 1), 0)),
                  row, pl.BlockSpec((HALO, D), lambda i: (jnp.maximum(i * per - 1, 0), 0)),
                  row, row, pl.BlockSpec((32, D), lambda i: (0, 0))],
        out_specs=[pl.BlockSpec((2, TB, D), lambda i: (0, i, 0)), pl.BlockSpec((32, D), lambda i: (0, 0))],
        out_shape=[jax.ShapeDtypeStruct((2, T, D), BF16), jax.ShapeDtypeStruct((32, D), F32)],
        scratch_shapes=[pltpu.VMEM((TB + HALO, D), F32), pltpu.VMEM((TB + HALO, D), F32),
                        pltpu.VMEM((32, SUBLANES, D), F32)],
        compiler_params=_params("arbitrary"),
    )(duc, duc, u, u, ua, ub, cw)


def _merge_fwd(x, oa, us, sa, sb, vec, w_ho, w_co, w_mo):
    T = x.shape[0]

    def body(x_ref, oa_ref, us_ref, sa_ref, sb_ref, vec_ref, who_hbm, wco_hbm, wmo_hbm,
             xo_ref, ya_ref, yb_ref, mg_ref, mo_ref, who, wco, wmo):
        @pl.when(pl.program_id(0) == 0)
        def _():
            pltpu.sync_copy(who_hbm, who)
            pltpu.sync_copy(wco_hbm, wco)
            pltpu.sync_copy(wmo_hbm, wmo)

        ya = _nn(oa_ref[...], who[...])
        yb = _nn(us_ref[...], wco[...])
        mg = (sa_ref[...].astype(F32) * ya + sb_ref[...].astype(F32) * yb).astype(BF16)
        mo = _nn(mg, wmo[...])
        xo_ref[...] = x_ref[...] + vec_ref[2:3, :] * mo
        ya_ref[...] = ya.astype(BF16)
        yb_ref[...] = yb.astype(BF16)
        mg_ref[...] = mg
        mo_ref[...] = mo.astype(BF16)

    row = pl.BlockSpec((TB, D), lambda i: (i, 0))
    bf = jax.ShapeDtypeStruct((T, D), BF16)
    wv = pltpu.VMEM((D, D), BF16)
    return pl.pallas_call(
        body, name="merge_fwd", grid=(T // TB,),
        in_specs=[row, row, row, row, row, pl.BlockSpec((8, D), lambda i: (0, 0)), ANY, ANY, ANY],
        out_specs=[row] * 5,
        out_shape=[jax.ShapeDtypeStruct((T, D), F32), bf, bf, bf, bf],
        scratch_shapes=[wv, wv, wv],
        compiler_params=_params("arbitrary"),
    )(x, oa, us, sa, sb, vec, w_ho, w_co, w_mo)


def _merge_bwd(dxo, mo, ya, yb, sa, sb, uc, vec, cvec, w_ho, w_co, w_mo):
    T = dxo.shape[0]

    def body(dxo_ref, mo_ref, ya_ref, yb_ref, sa_ref, sb_ref, uc_ref, vec_ref, cvec_ref, who_hbm, wco_hbm, wmo_hbm,
             dmo_ref, dya_ref, dyb_ref, doa_ref, duc_ref, dp_ref, acc_ref, cacc_ref, who, wco, wmo):
        @pl.when(pl.program_id(0) == 0)
        def _():
            pltpu.sync_copy(who_hbm, who)
            pltpu.sync_copy(wco_hbm, wco)
            pltpu.sync_copy(wmo_hbm, wmo)
            acc_ref[...] = jnp.zeros_like(acc_ref)
            cacc_ref[...] = jnp.zeros_like(cacc_ref)

        dxo = dxo_ref[...]
        acc_ref[2:3, :] += _colsum(mo_ref[...].astype(F32) * dxo)
        dmo = (vec_ref[2:3, :] * dxo).astype(BF16)
        dmo_ref[...] = dmo
        dmg = _nt(dmo, wmo[...])
        sa = sa_ref[...].astype(F32)
        sb = sb_ref[...].astype(F32)
        dya = (sa * dmg).astype(BF16)
        dyb = (sb * dmg).astype(BF16)
        dya_ref[...] = dya
        dyb_ref[...] = dyb
        dp_ref[0] = (dmg * ya_ref[...].astype(F32) * sa * (1.0 - sa)).astype(BF16)
        dp_ref[1] = (dmg * yb_ref[...].astype(F32) * sb * (1.0 - sb)).astype(BF16)
        doa_ref[...] = _nt(dya, who[...]).astype(BF16)
        dus = _nt(dyb, wco[...])
        lg = cvec_ref[1:2, :]
        rstd, z, ul = _ln_fwd(uc_ref[...], lg, cvec_ref[2:3, :])
        dul = dus * _dsilu(ul, _sigmoid(ul))
        cacc_ref[1:2, :] += _colsum(dul * z)
        cacc_ref[2:3, :] += _colsum(dul)
        dz = dul * lg
        duc = rstd * (dz - jnp.mean(dz, axis=-1, keepdims=True) - z * jnp.mean(dz * z, axis=-1, keepdims=True))
        cacc_ref[0:1, :] += _colsum(duc)
        duc_ref[...] = duc

    row = pl.BlockSpec((TB, D), lambda i: (i, 0))
    one = pl.BlockSpec((None, TB, D), lambda i: (0, i, 0))
    vec8 = pl.BlockSpec((8, D), lambda i: (0, 0))
    bf = jax.ShapeDtypeStruct((T, D), BF16)
    bf1 = jax.ShapeDtypeStruct((1, T, D), BF16)
    acc = jax.ShapeDtypeStruct((8, D), F32)
    wv = pltpu.VMEM((D, D), BF16)
    return pl.pallas_call(
        body, name="merge_bwd", grid=(T // TB,),
        in_specs=[row, row, row, row, row, row, row, vec8, vec8, ANY, ANY, ANY],
        out_specs=[one, one, one, row, row, pl.BlockSpec((2, TB, D), lambda i: (0, i, 0)), vec8, vec8],
        out_shape=[bf1, bf1, bf1, bf, jax.ShapeDtypeStruct((T, D), F32), jax.ShapeDtypeStruct((2, T, D), BF16), acc, acc],
        scratch_shapes=[wv, wv, wv],
        compiler_params=_params("arbitrary"),
    )(dxo, mo, ya, yb, sa, sb, uc, vec, cvec, w_ho, w_co, w_mo)


def _pack_rows(parts, total, name):
    def body(*refs):
        out = refs[-1]
        out[...] = jnp.zeros_like(out)
        for ref, (_, src, n, dst) in zip(refs[:-1], parts):
            out[dst:dst + n, :] = ref[src:src + n, :]

    arrs = [p[0] for p in parts]
    return pl.pallas_call(
        body, name=name, in_specs=[pl.BlockSpec(a.shape, lambda: (0, 0)) for a in arrs],
        out_specs=pl.BlockSpec((total, D), lambda: (0, 0)),
        out_shape=jax.ShapeDtypeStruct((total, D), F32),
    )(*arrs)


PACK_ROWS = 56
PACK_AT = {"ada_b": 0, "loss": 9, "norm_ffn1": 10, "norm_mix": 11, "hgrn_g": 12, "conv_b": 13, "conv_ln_g": 14,
           "conv_ln_b": 15, "norm_ffn2": 16, "norm_final": 17, "hgrn_lb": 18, "conv_w": 20}


def _local_step(x, tgt, mod, small, kc, weight, reduce, reduce_small):
    lb = jax.nn.sigmoid(small["hgrn_lb"][0:1] - small["hgrn_lb"][1:2])
    vec1 = _pack_rows([(mod, 0, 3, 0), (small["norm_ffn1"], 0, 1, 3)], 8, "pack_vec1")
    vec2 = _pack_rows([(mod, 3, 3, 0), (small["norm_mix"], 0, 1, 3), (lb, 0, 1, 4), (small["hgrn_g"], 0, 1, 5)],
                      8, "pack_vec2")
    vec3 = _pack_rows([(mod, 6, 3, 0), (small["norm_ffn2"], 0, 1, 3)], 8, "pack_vec3")
    cvec = _pack_rows([(small["conv_b"], 0, 1, 0), (small["conv_ln_g"], 0, 1, 1), (small["conv_ln_b"], 0, 1, 2)],
                      8, "pack_cvec")
    cw = small["conv_w"]
    gvec = _pack_rows([(small["norm_final"], 0, 1, 0)], 8, "pack_gvec")

    wg = {n: weight(n, vec1) for n in ("ffn1_w_in", "ffn1_w_out")}
    x1, h1, a1, b1, s1, f1 = _ffn_fwd(x, vec1, wg["ffn1_w_in"], wg["ffn1_w_out"], "ffn1_fwd")
    wg["mix_w_in"] = weight("mix_w_in", x1)
    h2, qr, g, k, v, og, u, ua, ub, sa, sb = _mix_proj_fwd(x1, vec2, wg["mix_w_in"])
    oa, o, st = _hgrn_fwd(qr, g, k, v, og, vec2)
    us, uc = _conv_fwd(u, cw, cvec)
    wg.update({n: weight(n, us) for n in ("hgrn_w_o", "conv_w_o", "mix_w_out")})
    x2, ya, yb, mg, mo = _merge_fwd(x1, oa, us, sa, sb, vec2, wg["hgrn_w_o"], wg["conv_w_o"], wg["mix_w_out"])
    wg.update({n: weight(n, x2) for n in ("ffn2_w_in", "ffn2_w_out")})
    dx3, h3, a3, b3, s3, f3, acc_head = _ffn_fwd(x2, vec3, wg["ffn2_w_in"], wg["ffn2_w_out"], "ffn2_fwd",
                                                 head=(tgt, gvec))

    dx2, df3, dab3, acc3 = _ffn_bwd(dx3, x2, vec3, a3, b3, f3, wg["ffn2_w_in"], wg["ffn2_w_out"], "ffn2_bwd")
    tok = reduce(("ffn2_w_out", "ffn2_w_in"), [_mm_tn(s3, df3, 1, kc, DFF // NCHIP, "ffn2_dwout"),
                                               _mm_tn(h3, dab3, 1, kc, D, "ffn2_dwin")])
    vec2b = vec2 + tok[0:1, 0:1]
    dmo, dya, dyb, doa, duc, dpc, acc_m, acc_c = _merge_bwd(dx2, mo, ya, yb, sa, sb, uc, vec2b, cvec,
                                                            wg["hgrn_w_o"], wg["conv_w_o"], wg["mix_w_out"])
    tok = reduce(("mix_w_out", "hgrn_w_o", "conv_w_o"),
                 [_mm_tn(mg, dmo, 1, kc, D // NCHIP, "mix_dwout"), _mm_tn(oa, dya, 1, kc, D // NCHIP, "hgrn_dwo"),
                  _mm_tn(us, dyb, 1, kc, D // NCHIP, "conv_dwo")])
    vec2c = vec2 + tok[0:1, 0:1]
    dpb, dcw = _conv_bwd_taps(duc, u, ua, ub, cw)
    dpa, acc_h = _hgrn_bwd(doa, og, qr, g, k, v, o, st, vec2c)
    dx1, acc2 = _mix_proj_bwd(dx2, x1, vec2c, dpa, dpb, dpc, wg["mix_w_in"])
    gmix = _mm_tn(h2, dpa, 2, kc, D, "mix_dwin_a", slabs=NCHIP)
    gmix = _mm_tn(h2, dpb, 2, kc, D, "mix_dwin_b", into=gmix, slab=2, slabs=NCHIP)
    gmix = _mm_tn(h2, dpc, 2, kc, D, "mix_dwin_c", into=gmix, slab=3, slabs=NCHIP)
    tok = reduce(("mix_w_in",), [gmix])
    vec1b = vec1 + tok[0:1, 0:1]
    dx0, df1, dab1, acc1 = _ffn_bwd(dx1, x, vec1b, a1, b1, f1, wg["ffn1_w_in"], wg["ffn1_w_out"], "ffn1_bwd")

    at = PACK_AT
    packed = _pack_rows([
        (acc1, 0, 3, at["ada_b"]), (acc2, 0, 2, at["ada_b"] + 3), (acc_m, 2, 1, at["ada_b"] + 5),
        (acc3, 0, 3, at["ada_b"] + 6), (acc_head, 1, 1, at["loss"]), (acc1, 3, 1, at["norm_ffn1"]),
        (acc2, 3, 1, at["norm_mix"]), (acc_h, 0, 1, at["hgrn_g"]), (acc_c, 0, 3, at["conv_b"]),
        (acc3, 3, 1, at["norm_ffn2"]), (acc_head, 0, 1, at["norm_final"]), (acc_h, 1, 2, at["hgrn_lb"]),
        (dcw, 0, CONV_K, at["conv_w"])], PACK_ROWS, "pack_small_grads")
    done = reduce_small(packed)
    reduce(("ffn1_w_out", "ffn1_w_in"), [_mm_tn(s1, df1, 1, kc, DFF // NCHIP, "ffn1_dwout"),
                                         _mm_tn(h1, dab1, 1, kc, D, "ffn1_dwin")], done)
    return dx0


BLOCK_BYTES = 3 * 512 * 1024


def _row_block(rows, cols):
    for br in (512, 352, 256, 176, 128, 64, 32, 16, 8):
        if rows % br == 0 and br * cols * 4 <= BLOCK_BYTES:
            return br
    return rows


def _cast_into_slot(w, kc, name, after):
    R, C = w.shape
    br = _row_block(R, C)

    def body(kc_ref, w_ref, after_ref, o_ref):
        o_ref[...] = w_ref[...].astype(BF16)

    return pl.pallas_call(
        body, name=name,
        grid_spec=pltpu.PrefetchScalarGridSpec(
            num_scalar_prefetch=1, grid=(R // br,),
            in_specs=[pl.BlockSpec((br, C), lambda i, kc: (i, 0)), ANY],
            out_specs=pl.BlockSpec((None, br, C), lambda i, kc: (kc[0], i, 0))),
        out_shape=jax.ShapeDtypeStruct((NCHIP, R, C), BF16), compiler_params=_params("parallel"),
    )(kc, w, after)


def _adamw(w, g, m, v, name, after=None, copy_grad=False):
    R, C = w.shape
    br = _row_block(R, C)
    extra = [] if after is None else [after]
    nout = 4 if copy_grad else 3

    def body(w_ref, g_ref, m_ref, v_ref, *rest):
        d_ref, nm_ref, nv_ref = rest[-nout:][:3]
        gv = g_ref[...]
        if copy_grad:
            rest[-1][...] = gv
        nm = ADAM_B1 * m_ref[...] + (1.0 - ADAM_B1) * gv
        nv = ADAM_B2 * v_ref[...] + (1.0 - ADAM_B2) * (gv * gv)
        m_hat = nm / (1.0 - ADAM_B1 ** ADAM_STEP)
        v_hat = nv / (1.0 - ADAM_B2 ** ADAM_STEP)
        d_ref[...] = -ADAM_LR * (m_hat / (jnp.sqrt(v_hat) + ADAM_EPS) + ADAM_WD * w_ref[...])
        nm_ref[...] = nm
        nv_ref[...] = nv

    blk = pl.BlockSpec((br, C), lambda i: (i, 0))
    out = jax.ShapeDtypeStruct((R, C), F32)
    return pl.pallas_call(
        body, name=name, grid=(R // br,), in_specs=[blk] * 4 + [ANY] * len(extra), out_specs=[blk] * nout,
        out_shape=[out] * nout, compiler_params=_params("parallel"),
    )(w, g, m, v, *extra)


def _coords():
    return lax.axis_index("x"), lax.axis_index("y"), lax.axis_index("c")


def _flip(v, bit):
    return 1 - v if bit else v


def _allgather8(v, name):
    R, C = v.shape

    def body(v_ref, out_ref, send_sems, recv_sems, local_sem):
        x, y, c = _coords()
        me = 4 * x + 2 * y + c
        mine = pltpu.make_async_copy(v_ref, out_ref.at[me], local_sem)
        mine.start()

        def copy(m, block):
            peer = (_flip(x, m & 4), _flip(y, m & 2), _flip(c, m & 1))
            return pltpu.make_async_remote_copy(
                src_ref=v_ref, dst_ref=out_ref.at[block], send_sem=send_sems.at[m - 1],
                recv_sem=recv_sems.at[m - 1], device_id=peer, device_id_type=MESH)

        sends = [copy(m, me) for m in range(1, 8)]
        for cp in sends:
            cp.start()
        for m in range(1, 8):
            sender = 4 * _flip(x, m & 4) + 2 * _flip(y, m & 2) + _flip(c, m & 1)
            copy(m, sender).wait_recv()
        for cp in sends:
            cp.wait_send()
        mine.wait()

    vm = pl.BlockSpec(memory_space=pltpu.VMEM)
    return pl.pallas_call(
        body, name=name, in_specs=[vm], out_specs=vm,
        out_shape=jax.ShapeDtypeStruct((8, R, C), F32),
        scratch_shapes=[pltpu.SemaphoreType.DMA((7,)), pltpu.SemaphoreType.DMA((7,)), pltpu.SemaphoreType.DMA],
    )(v)


HBM = pl.BlockSpec(memory_space=pltpu.HBM)
SEM = pl.BlockSpec(memory_space=pltpu.SEMAPHORE)
EFFECT = pltpu.SideEffectType.DATAFLOW_SIDE_EFFECTING


def _chip_peer(x, y, m):
    px, py = _flip(x, m & 2), _flip(y, m & 1)
    return px, py, 2 * px + py


def _core_rows(land, c):
    half = land.shape[1] // 2
    return pl.ds(pl.multiple_of(c * half, 16), half)


def _gather_start(lands, groups, halved, after, name):
    n, ng, na = len(lands), len(groups), len(after)

    def body(*refs):
        ins = refs[:n]
        sends, recvs = refs[n + na:n + na + ng], refs[n + na + ng:n + na + 2 * ng]
        token = refs[n + na + 2 * ng + n]
        x, y, c = _coords()
        k = 2 * x + y
        for gi, grp in enumerate(groups):
            for j, t in enumerate(grp):
                mine = ins[t].at[k, _core_rows(ins[t], c), :] if halved[gi] else ins[t].at[k]
                for m in (1, 2, 3):
                    px, py, _ = _chip_peer(x, y, m)
                    pltpu.make_async_remote_copy(
                        src_ref=mine, dst_ref=mine, send_sem=sends[gi].at[3 * j + m - 1],
                        recv_sem=recvs[gi].at[3 * j + m - 1], device_id=(px, py, c), device_id_type=MESH).start()
        token[...] = jnp.zeros_like(token)

    sems = [pltpu.SemaphoreType.DMA((3 * len(g),)) for g in groups]
    out = pl.pallas_call(
        body, name=name,
        out_shape=sems + sems + [pltpu.HBM(a.shape, a.dtype) for a in lands] + [jax.ShapeDtypeStruct((8, 128), F32)],
        in_specs=[HBM] * n + [ANY] * na,
        out_specs=[SEM] * (2 * ng) + [HBM] * n + [pl.BlockSpec(memory_space=pltpu.VMEM)],
        input_output_aliases={t: 2 * ng + t for t in range(n)},
        compiler_params=pltpu.CompilerParams(has_side_effects=EFFECT),
    )(*[pltpu.with_memory_space_constraint(a, pltpu.HBM) for a in lands], *after)
    return out[:ng], out[ng:2 * ng], out[2 * ng:2 * ng + n], out[2 * ng + n]


def _gather_wait(lands, halved, send_sem, recv_sem, after, name):
    n = len(lands)

    def body(*refs):
        ins, send, recv = refs[:n], refs[n], refs[n + 1]
        x, y, c = _coords()
        k = 2 * x + y
        for j in range(n):
            rows = _core_rows(ins[j], c)
            for m in (1, 2, 3):
                px, py, pk = _chip_peer(x, y, m)
                cp = pltpu.make_async_remote_copy(
                    src_ref=ins[j].at[k, rows, :] if halved else ins[j].at[k],
                    dst_ref=ins[j].at[pk, rows, :] if halved else ins[j].at[pk], send_sem=send.at[3 * j + m - 1],
                    recv_sem=recv.at[3 * j + m - 1], device_id=(px, py, c), device_id_type=MESH)
                cp.wait_send()
                cp.wait_recv()

    return pl.pallas_call(
        body, name=name, out_shape=[pltpu.HBM(a.shape, a.dtype) for a in lands],
        in_specs=[HBM] * n + [SEM, SEM, ANY], out_specs=[HBM] * n,
        input_output_aliases={j: j for j in range(n)},
        compiler_params=pltpu.CompilerParams(has_side_effects=EFFECT),
    )(*lands, send_sem, recv_sem, after)


def _sibling_fill(lands, name):
    n = len(lands)

    def body(*refs):
        ins = refs[:n]
        send_sems, recv_sems = refs[2 * n:]
        x, y, c = _coords()
        sends, recvs = [], []
        for t in range(n):
            for m in (1, 2, 3):
                _, _, pk = _chip_peer(x, y, m)
                for rows, lst in ((_core_rows(ins[t], c), sends), (_core_rows(ins[t], 1 - c), recvs)):
                    lst.append(pltpu.make_async_remote_copy(
                        src_ref=ins[t].at[pk, rows, :], dst_ref=ins[t].at[pk, rows, :],
                        send_sem=send_sems.at[3 * t + m - 1], recv_sem=recv_sems.at[3 * t + m - 1],
                        device_id=(x, y, 1 - c), device_id_type=MESH))
        for cp in sends:
            cp.start()
        for cp in recvs:
            cp.wait_recv()
        for cp in sends:
            cp.wait_send()

    return pl.pallas_call(
        body, name=name, in_specs=[ANY] * n, out_specs=[ANY] * n,
        out_shape=[jax.ShapeDtypeStruct(a.shape, a.dtype) for a in lands],
        input_output_aliases={t: t for t in range(n)},
        scratch_shapes=[pltpu.SemaphoreType.DMA((3 * n,)), pltpu.SemaphoreType.DMA((3 * n,))],
    )(*lands)


def _scatter_start(srcs, name, after=()):
    n, na = len(srcs), len(after)

    def body(*refs):
        ins, lands = refs[:n], refs[n:2 * n]
        send, recv = refs[2 * n + na], refs[2 * n + na + 1]
        token = refs[2 * n + na + 2 + 2 * n]
        x, y, c = _coords()
        k = 2 * x + y
        for t in range(n):
            for m in (1, 2, 3):
                px, py, pk = _chip_peer(x, y, m)
                pltpu.make_async_remote_copy(
                    src_ref=ins[t].at[pk], dst_ref=lands[t].at[k], send_sem=send.at[3 * t + m - 1],
                    recv_sem=recv.at[3 * t + m - 1], device_id=(px, py, c), device_id_type=MESH).start()
        token[...] = jnp.zeros_like(token)

    sem = pltpu.SemaphoreType.DMA((3 * n,))
    hbm = [pltpu.HBM(a.shape, a.dtype) for a in srcs]
    operands = list(srcs) + [lax.empty(a.shape, a.dtype) for a in srcs]
    out = pl.pallas_call(
        body, name=name, out_shape=[sem, sem] + hbm + hbm + [jax.ShapeDtypeStruct((8, 128), F32)],
        in_specs=[HBM] * (2 * n) + [ANY] * na,
        out_specs=[SEM, SEM] + [HBM] * (2 * n) + [pl.BlockSpec(memory_space=pltpu.VMEM)],
        input_output_aliases={t: 2 + t for t in range(2 * n)},
        compiler_params=pltpu.CompilerParams(has_side_effects=EFFECT),
    )(*[pltpu.with_memory_space_constraint(a, pltpu.HBM) for a in operands], *after)
    return out[0], out[1], out[2:2 + n], out[2 + n:2 + 2 * n], out[2 + 2 * n]


def _scatter_wait(srcs, lands, send_sem, recv_sem, after, name):
    n = len(srcs)

    def body(*refs):
        ins, land = refs[:n], refs[n:2 * n]
        send, recv = refs[2 * n], refs[2 * n + 1]
        x, y, c = _coords()
        for t in range(n):
            for m in (1, 2, 3):
                px, py, pk = _chip_peer(x, y, m)
                cp = pltpu.make_async_remote_copy(
                    src_ref=ins[t].at[pk], dst_ref=land[t].at[pk], send_sem=send.at[3 * t + m - 1],
                    recv_sem=recv.at[3 * t + m - 1], device_id=(px, py, c), device_id_type=MESH)
                cp.wait_send()
                cp.wait_recv()

    hbm = [pltpu.HBM(a.shape, a.dtype) for a in srcs]
    out = pl.pallas_call(
        body, name=name, out_shape=hbm + hbm, in_specs=[HBM] * (2 * n) + [SEM, SEM, ANY], out_specs=[HBM] * (2 * n),
        input_output_aliases={t: t for t in range(2 * n)},
        compiler_params=pltpu.CompilerParams(has_side_effects=EFFECT),
    )(*srcs, *lands, send_sem, recv_sem, after)
    return out[:n], out[n:]


def _sum_own_half(g, ra, kc, name):
    _, R, C = g.shape
    half = R // 2
    br = _row_block(half, C)
    nb = half // br

    def body(kc_ref, g_ref, ra_ref, o_ref):
        o_ref[...] = (g_ref[...] + ra_ref[...].astype(F32)).astype(BF16)

    return pl.pallas_call(
        body, name=name,
        grid_spec=pltpu.PrefetchScalarGridSpec(
            num_scalar_prefetch=1, grid=(NCHIP, nb),
            in_specs=[pl.BlockSpec((None, br, C), lambda j, i, kc: (j, kc[1] * nb + i, 0)),
                      pl.BlockSpec((None, br, C), lambda j, i, kc: (j, i, 0))],
            out_specs=pl.BlockSpec((None, br, C), lambda j, i, kc: (j, i, 0))),
        out_shape=jax.ShapeDtypeStruct((NCHIP, half, C), BF16),
        compiler_params=_params("parallel", "parallel"),
    )(kc, g, ra)


def _sum_chips(sa, rb, kc, name, after=None):
    _, half, C = rb.shape
    br = _row_block(half, C)
    nb = half // br
    extra = [] if after is None else [after]

    def body(kc_ref, own_ref, r1_ref, r2_ref, r3_ref, *rest):
        out, obuf, local_sems, send_sems, recv_sem = rest[-5:]
        i = pl.program_id(0)
        slot = i % 2
        x, y, c = _coords()

        def copies(i_, slot_):
            rows = out.at[pl.ds(pl.multiple_of((c * nb + i_) * br, 8), br), :]
            return (pltpu.make_async_copy(obuf.at[slot_], rows, local_sems.at[slot_]),
                    pltpu.make_async_remote_copy(src_ref=obuf.at[slot_], dst_ref=rows, send_sem=send_sems.at[slot_],
                                                 recv_sem=recv_sem, device_id=(x, y, 1 - c), device_id_type=MESH))

        @pl.when(i >= 2)
        def _():
            here, there = copies(i, slot)
            here.wait()
            there.wait_send()

        acc = own_ref[...].astype(F32) + r1_ref[...].astype(F32)
        obuf[slot] = (acc + r2_ref[...].astype(F32)) + r3_ref[...].astype(F32)
        here, there = copies(i, slot)
        here.start()
        there.start()

        @pl.when(i == nb - 1)
        def _():
            for s in range(min(2, nb)):
                here, there = copies(i, (i - s) % 2)
                here.wait()
                there.wait_send()
            theirs = out.at[pl.ds(pl.multiple_of((1 - c) * half, 8), half), :]
            pltpu.make_async_remote_copy(src_ref=theirs, dst_ref=theirs, send_sem=send_sems.at[0], recv_sem=recv_sem,
                                         device_id=(x, y, 1 - c), device_id_type=MESH).wait_recv()

    def slab(m):
        return pl.BlockSpec((None, br, C), lambda i, kc: (kc[0] ^ m, i, 0))

    return pl.pallas_call(
        body, name=name,
        grid_spec=pltpu.PrefetchScalarGridSpec(
            num_scalar_prefetch=1, grid=(nb,),
            in_specs=[slab(0), slab(1), slab(2), slab(3)] + [ANY] * len(extra),
            out_specs=ANY,
            scratch_shapes=[pltpu.VMEM((2, br, C), F32), pltpu.SemaphoreType.DMA((2,)), pltpu.SemaphoreType.DMA((2,)),
                            pltpu.SemaphoreType.DMA]),
        out_shape=jax.ShapeDtypeStruct((2 * half, C), F32), compiler_params=_params("arbitrary"),
    )(kc, sa, rb, rb, rb, *extra)


def _sum8(ga, name):
    _, R, C = ga.shape

    def body(g_ref, o_ref):
        acc = g_ref[0]
        for j in range(1, 8):
            acc = acc + g_ref[j]
        o_ref[...] = acc

    return pl.pallas_call(
        body, name=name, in_specs=[pl.BlockSpec((8, R, C), lambda: (0, 0, 0))],
        out_specs=pl.BlockSpec((R, C), lambda: (0, 0)), out_shape=jax.ShapeDtypeStruct((R, C), F32),
    )(ga)


ADA_COLS = 9 * D // NCHIP
ADA_BLK = 256


def _ada_mod(c_all, ada_w, ada_b, kidx):
    def body(k_ref, c_ref, w_ref, b_ref, o_ref):
        cv = c_ref[...]
        cs = cv * _sigmoid(cv)
        o_ref[...] = jnp.dot(cs, w_ref[...], precision=lax.Precision.HIGHEST,
                             preferred_element_type=F32) + b_ref[...]

    nblk = ADA_COLS // ADA_BLK
    return pl.pallas_call(
        body, name="ada_mod",
        grid_spec=pltpu.PrefetchScalarGridSpec(
            num_scalar_prefetch=1, grid=(nblk,),
            in_specs=[pl.BlockSpec((8, D), lambda j, k: (0, 0)),
                      pl.BlockSpec((D, ADA_BLK), lambda j, k: (0, j)),
                      pl.BlockSpec((1, ADA_BLK), lambda j, k: (0, k[0] * nblk + j))],
            out_specs=pl.BlockSpec((8, ADA_BLK), lambda j, k: (0, j))),
        out_shape=jax.ShapeDtypeStruct((8, ADA_COLS), F32),
        compiler_params=_params("parallel"),
    )(kidx, c_all, ada_w, ada_b)


def _ada_grad(c_all_t, dmod_all, kidx):
    def body(k_ref, ct_ref, dm_ref, o_ref):
        cv = ct_ref[...]
        cs = cv * _sigmoid(cv)
        acc = cs[:, 0:1] * dm_ref[0:1, :]
        for b in range(1, 8):
            acc = acc + cs[:, b:b + 1] * dm_ref[b:b + 1, :]
        o_ref[...] = acc

    nblk = ADA_COLS // ADA_BLK
    return pl.pallas_call(
        body, name="ada_grad",
        grid_spec=pltpu.PrefetchScalarGridSpec(
            num_scalar_prefetch=1, grid=(nblk,),
            in_specs=[pl.BlockSpec((D, 8), lambda j, k: (0, 0)),
                      pl.BlockSpec((8, ADA_BLK), lambda j, k: (0, k[0] * nblk + j))],
            out_specs=pl.BlockSpec((D, ADA_BLK), lambda j, k: (0, j))),
        out_shape=jax.ShapeDtypeStruct((D, ADA_COLS), F32),
        compiler_params=_params("parallel"),
    )(kidx, c_all_t, dmod_all)


BIG = ("ffn1_w_in", "ffn1_w_out", "mix_w_in", "hgrn_w_o", "conv_w_o", "mix_w_out", "ffn2_w_in", "ffn2_w_out")
ROW_SHARDED = ("ffn1_w_out", "hgrn_w_o", "conv_w_o", "mix_w_out", "ffn2_w_out")
GATHER_GROUPS = ((0, 1), (2,), (3, 4, 5), (6, 7))
GATHER_HALVED = (True, True, False, False)
GATHER_STARTS = ((0, 1), (2, 3))
PACK_LEN = {"ada_b": 9, "hgrn_lb": 2}
WEIGHTS = ("ada_w", "ada_b", "norm_ffn1", "ffn1_w_in", "ffn1_w_out", "norm_mix", "mix_w_in", "hgrn_lb", "hgrn_g",
           "hgrn_w_o", "conv_w", "conv_b", "conv_ln_g", "conv_ln_b", "conv_w_o", "mix_w_out", "norm_ffn2",
           "ffn2_w_in", "ffn2_w_out", "norm_final")
PACKED = ("ada_b", "norm_ffn1", "norm_mix", "hgrn_g", "conv_b", "conv_ln_g", "conv_ln_b", "norm_ffn2",
          "norm_final", "hgrn_lb")


def _pack_params(p, name):
    parts = [(p[n].reshape(PACK_LEN.get(n, 1), D), 0, PACK_LEN.get(n, 1), PACK_AT[n]) for n in PACKED]
    return _pack_rows(parts, PACK_ROWS, name)


def _step(w, m, v, x, c, tgt):
    xi, yi, ci = _coords()
    kidx = (2 * xi + yi).astype(jnp.int32).reshape(1)
    kc = jnp.stack([2 * xi + yi, ci]).astype(jnp.int32)
    me = 4 * xi + 2 * yi + ci

    first = jnp.zeros((40, D), F32).at[0:CONV_K, 0:D // NCHIP].set(w["conv_w"][0]).at[32:33].set(c)
    first_all = _allgather8(first, "gather_c_conv_w")
    c_all = first_all[:, 32, :]
    mod_cols = _ada_mod(c_all, w["ada_w"][0], w["ada_b"], kidx)
    mod_all = _allgather8(mod_cols, "gather_mod")
    mod = lax.dynamic_slice(mod_all, (0, me, 0), (8, 1, ADA_COLS))[::2].reshape(9, D)
    small = {n: w[n].reshape(-1, D) for n in ("norm_ffn1", "norm_mix", "hgrn_lb", "hgrn_g", "conv_b", "conv_ln_g",
                                              "conv_ln_b", "norm_ffn2", "norm_final")}
    small["conv_w"] = jnp.concatenate([first_all[2 * j, 0:32, 0:D // NCHIP] for j in range(NCHIP)], axis=1)

    lands, sends, recvs = [], [], []
    after = mod
    for part in GATHER_STARTS:
        tensors = [t for gi in part for t in GATHER_GROUPS[gi]]
        cast = [_cast_into_slot(w[BIG[t]][0], kc, "cast_" + BIG[t], after) for t in tensors]
        groups = [tuple(tensors.index(t) for t in GATHER_GROUPS[gi]) for gi in part]
        s, r, thru, after = _gather_start(cast, groups, [GATHER_HALVED[gi] for gi in part], [after],
                                          "gather_weights_start%d" % part[0])
        lands, sends, recvs = lands + list(thru), sends + list(s), recvs + list(r)
    started_all = after
    ready = {}

    def weight(name, after):
        t = BIG.index(name)
        if t not in ready:
            gi = [t in grp for grp in GATHER_GROUPS].index(True)
            grp = GATHER_GROUPS[gi]
            outs = _gather_wait([lands[j] for j in grp], GATHER_HALVED[gi], sends[gi], recvs[gi],
                                started_all if gi == 0 else after, "gather_weights_wait%d" % gi)
            if GATHER_HALVED[gi]:
                outs = _sibling_fill(outs, "gather_weights_fill%d" % gi)
            ready.update(zip(grp, outs))
        return ready[t].reshape(-1, D) if name in ROW_SHARDED else ready[t]

    grads, delta, new_m, new_v = {}, {}, {}, {}
    flight = []
    landed = []

    def settle(after):
        names, sa, rb, send, recv = flight.pop()
        sa, rb = _scatter_wait(sa, rb, send, recv, after, "rs_chip_wait_" + names[0])
        landed.append((names, sa, rb))

    def reduce(names, pairs, after=None):
        gs = [g.reshape(NCHIP, -1, g.shape[-1]) for g, _ in pairs]
        ra = [r.reshape(NCHIP, -1, r.shape[-1]) for _, r in pairs]
        sa = [_sum_own_half(g, r, kc, "rs_sum_pair_" + n) for g, r, n in zip(gs, ra, names)]
        if flight:
            settle(sa[0])
        send, recv, sa, rb, tok = _scatter_start(sa, "rs_chip_start_" + names[0], () if after is None else (after,))
        flight.append((names, sa, rb, send, recv))
        started.append(tok)
        return tok

    def adamw(n, after=None):
        shape = w[n].shape
        two = (shape[-2], shape[-1])
        out = _adamw(w[n].reshape(two), grads[n], m[n].reshape(two), v[n].reshape(two), "adamw_" + n, after,
                     copy_grad=n in BIG)
        g_ = out[3] if n in BIG else grads[n]
        grads[n], delta[n], new_m[n], new_v[n] = (a.reshape(shape) for a in (g_, out[0], out[1], out[2]))
        return out[1]

    def finish(after=None):
        names, sa, rb = landed.pop(0)
        full = [_sum_chips(s, r, kc, "rs_sum_chips_" + n, after) for s, r, n in zip(sa, rb, names)]
        grads.update(zip(names, full))
        return [adamw(n) for n in names][-1]

    started = []

    smalls = []

    def reduce_small(packed):
        packed_all = _allgather8(packed, "gather_small_grads")
        smalls.extend([packed_all, _sum8(packed_all, "sum_small_grads")])
        return smalls[1]

    dx = _local_step(x[0], tgt[0], mod, small, kc, weight, reduce, reduce_small)
    packed_all, gsum = smalls
    loss = (0.5 / D) * jnp.sum(gsum[PACK_AT["loss"]])
    dmod_all = packed_all[:, 0:9, :].reshape(8, 9 * D)
    grads["ada_w"] = _ada_grad(c_all.T, dmod_all, kidx)
    grads["conv_w"] = lax.dynamic_slice(gsum, (PACK_AT["conv_w"], kidx[0] * (D // NCHIP)), (CONV_K, D // NCHIP))

    tok = started[-1]
    adamw("ada_w", tok)
    adamw("conv_w")
    pw, pm, pv = (_pack_params(p, "pack_" + s) for p, s in ((w, "w"), (m, "m"), (v, "v")))
    pd, pnm, pnv = _adamw(pw, gsum, pm, pv, "adamw_small", tok)
    last = pnv
    while landed:
        last = finish(tok)
    settle(last)
    finish()
    for n in PACKED:
        rows = slice(PACK_AT[n], PACK_AT[n] + PACK_LEN.get(n, 1))
        for dst, src in ((grads, gsum), (delta, pd), (new_m, pnm), (new_v, pnv)):
            dst[n] = src[rows].reshape(w[n].shape)

    outs = [loss, dx[None]]
    for d in (grads, delta, new_m, new_v):
        outs += [d[n] for n in WEIGHTS]
    return tuple(outs)


def kernel(x, c, ada_w, ada_b, norm_ffn1, ffn1_w_in, ffn1_w_out, norm_mix, mix_w_in, hgrn_lb, hgrn_g, hgrn_w_o, conv_w, conv_b, conv_ln_g, conv_ln_b, conv_w_o, mix_w_out, norm_ffn2, ffn2_w_in, ffn2_w_out, norm_final, loss_target, m_ada_w, m_ada_b, m_norm_ffn1, m_ffn1_w_in, m_ffn1_w_out, m_norm_mix, m_mix_w_in, m_hgrn_lb, m_hgrn_g, m_hgrn_w_o, m_conv_w, m_conv_b, m_conv_ln_g, m_conv_ln_b, m_conv_w_o, m_mix_w_out, m_norm_ffn2, m_ffn2_w_in, m_ffn2_w_out, m_norm_final, v_ada_w, v_ada_b, v_norm_ffn1, v_ffn1_w_in, v_ffn1_w_out, v_norm_mix, v_mix_w_in, v_hgrn_lb, v_hgrn_g, v_hgrn_w_o, v_conv_w, v_conv_b, v_conv_ln_g, v_conv_ln_b, v_conv_w_o, v_mix_w_out, v_norm_ffn2, v_ffn2_w_in, v_ffn2_w_out, v_norm_final):
    w = dict(ada_w=ada_w, ada_b=ada_b, norm_ffn1=norm_ffn1, ffn1_w_in=ffn1_w_in, ffn1_w_out=ffn1_w_out,
             norm_mix=norm_mix, mix_w_in=mix_w_in, hgrn_lb=hgrn_lb, hgrn_g=hgrn_g, hgrn_w_o=hgrn_w_o, conv_w=conv_w,
             conv_b=conv_b, conv_ln_g=conv_ln_g, conv_ln_b=conv_ln_b, conv_w_o=conv_w_o, mix_w_out=mix_w_out,
             norm_ffn2=norm_ffn2, ffn2_w_in=ffn2_w_in, ffn2_w_out=ffn2_w_out, norm_final=norm_final)
    m = dict(ada_w=m_ada_w, ada_b=m_ada_b, norm_ffn1=m_norm_ffn1, ffn1_w_in=m_ffn1_w_in, ffn1_w_out=m_ffn1_w_out,
             norm_mix=m_norm_mix, mix_w_in=m_mix_w_in, hgrn_lb=m_hgrn_lb, hgrn_g=m_hgrn_g, hgrn_w_o=m_hgrn_w_o,
             conv_w=m_conv_w, conv_b=m_conv_b, conv_ln_g=m_conv_ln_g, conv_ln_b=m_conv_ln_b, conv_w_o=m_conv_w_o,
             mix_w_out=m_mix_w_out, norm_ffn2=m_norm_ffn2, ffn2_w_in=m_ffn2_w_in, ffn2_w_out=m_ffn2_w_out,
             norm_final=m_norm_final)
    v = dict(ada_w=v_ada_w, ada_b=v_ada_b, norm_ffn1=v_norm_ffn1, ffn1_w_in=v_ffn1_w_in, ffn1_w_out=v_ffn1_w_out,
             norm_mix=v_norm_mix, mix_w_in=v_mix_w_in, hgrn_lb=v_hgrn_lb, hgrn_g=v_hgrn_g, hgrn_w_o=v_hgrn_w_o,
             conv_w=v_conv_w, conv_b=v_conv_b, conv_ln_g=v_conv_ln_g, conv_ln_b=v_conv_ln_b, conv_w_o=v_conv_w_o,
             mix_w_out=v_mix_w_out, norm_ffn2=v_norm_ffn2, ffn2_w_in=v_ffn2_w_in, ffn2_w_out=v_ffn2_w_out,
             norm_final=v_norm_final)
    return _step(w, m, v, x, c, loss_target)
```

```python
import functools

import jax
import jax.numpy as jnp
from jax import lax
from jax.experimental import pallas as pl
from jax.experimental.pallas import tpu as pltpu

F32 = jnp.float32
BF16 = jnp.bfloat16

D = 1024
DFF = 2816
NCHIP = 4
FSH = 2 * DFF // NCHIP
HEADS = 8
DK = 128
CHUNK = 64
CONV_K = 31
HALO = 32
EPS = 1e-6
TB = 256
CB = 1024
DW_TOKENS = 2048
VMEM_LIMIT = 56 * 1024 * 1024

ADAM_LR = 0.001
ADAM_B1 = 0.9
ADAM_B2 = 0.999
ADAM_EPS = 1e-08
ADAM_WD = 0.01
ADAM_STEP = 10

MESH = pl.DeviceIdType.MESH
ANY = pl.BlockSpec(memory_space=pl.ANY)


def _params(*sem):
    return pltpu.CompilerParams(dimension_semantics=sem, vmem_limit_bytes=VMEM_LIMIT)


def _sigmoid(x):
    return 0.5 * jnp.tanh(0.5 * x) + 0.5


def _dsilu(x, sg):
    return sg * (1.0 + x * (1.0 - sg))


def _nt(a, b):
    return lax.dot_general(a, b, (((1,), (1,)), ((), ())), preferred_element_type=F32)


def _tn(a, b):
    return lax.dot_general(a, b, (((0,), (0,)), ((), ())), preferred_element_type=F32)


def _nn(a, b):
    return jnp.dot(a, b, preferred_element_type=F32)


def _colsum(x):
    return jnp.sum(x, axis=0, keepdims=True)


def _rms_fwd(x, gn, sc, sh):
    r = lax.rsqrt(jnp.mean(x * x, axis=-1, keepdims=True) + EPS)
    n = x * r
    h = (n * gn) * (1.0 + sc) + sh
    return r, n, h


def _rms_bwd(dh, r, n, gn, sc, acc_ref):
    acc_ref[0:1, :] += _colsum(dh)
    acc_ref[1:2, :] += _colsum(dh * (n * gn))
    dng = dh * (1.0 + sc)
    acc_ref[3:4, :] += _colsum(dng * n)
    dn = dng * gn
    return r * (dn - n * jnp.mean(dn * n, axis=-1, keepdims=True))


def _loss_head(x, tgt, gf, acc_ref):
    r = lax.rsqrt(jnp.mean(x * x, axis=-1, keepdims=True) + EPS)
    n = x * r
    err = n * gf - tgt
    acc_ref[1:2, :] += _colsum(err * err)
    dy = err * (1.0 / D)
    acc_ref[0:1, :] += _colsum(dy * n)
    dn = dy * gf
    return r * (dn - n * jnp.mean(dn * n, axis=-1, keepdims=True))


def _ffn_fwd(x, vec, w_in, w_out, name, head=None):
    T = x.shape[0]
    nh = 0 if head is None else 2

    def body(x_ref, vec_ref, *rest):
        win_hbm, wout_hbm = rest[nh:nh + 2]
        xo_ref, h_ref, a_ref, b_ref, s_ref, f_ref = rest[nh + 2:nh + 8]
        win, wout = rest[-2:]

        @pl.when(pl.program_id(0) == 0)
        def _():
            pltpu.sync_copy(win_hbm, win)
            pltpu.sync_copy(wout_hbm, wout)
            if head is not None:
                rest[nh + 8][...] = jnp.zeros((8, D), F32)

        x = x_ref[...]
        sh, sc, gate, gn = vec_ref[0:1, :], vec_ref[1:2, :], vec_ref[2:3, :], vec_ref[3:4, :]
        _, _, h = _rms_fwd(x, gn, sc, sh)
        hb = h.astype(BF16)
        h_ref[...] = hb
        f = jnp.zeros((TB, D), F32)
        for j in range(2):
            cols = slice(j * FSH, (j + 1) * FSH)
            a = _nn(hb, win[j])
            b = _nn(hb, win[2 + j])
            s = (a * _sigmoid(a) * b).astype(BF16)
            a_ref[:, cols] = a.astype(BF16)
            b_ref[:, cols] = b.astype(BF16)
            s_ref[:, cols] = s
            f = f + _nn(s, wout[cols, :])
        xo = x + (0.5 * gate) * f
        f_ref[...] = f.astype(BF16)
        if head is None:
            xo_ref[...] = xo
        else:
            xo_ref[...] = _loss_head(xo, rest[0][...], rest[1][0:1, :], rest[nh + 8])

    row = lambda w: pl.BlockSpec((TB, w), lambda i: (i, 0))
    vec8 = pl.BlockSpec((8, D), lambda i: (0, 0))
    acc = [] if head is None else [jax.ShapeDtypeStruct((8, D), F32)]
    return pl.pallas_call(
        body, name=name, grid=(T // TB,),
        in_specs=[row(D), vec8] + ([] if head is None else [row(D), vec8]) + [ANY, ANY],
        out_specs=[row(D), row(D), row(DFF), row(DFF), row(DFF), row(D)] + [vec8] * len(acc),
        out_shape=[jax.ShapeDtypeStruct((T, D), F32), jax.ShapeDtypeStruct((T, D), BF16),
                   jax.ShapeDtypeStruct((T, DFF), BF16), jax.ShapeDtypeStruct((T, DFF), BF16),
                   jax.ShapeDtypeStruct((T, DFF), BF16), jax.ShapeDtypeStruct((T, D), BF16)] + acc,
        scratch_shapes=[pltpu.VMEM((NCHIP, D, FSH), BF16), pltpu.VMEM((DFF, D), BF16)],
        compiler_params=_params("arbitrary"),
    )(x, vec, *([] if head is None else list(head)), w_in, w_out)


def _ffn_bwd(dxo, x, vec, a, b, f, w_in, w_out, name):
    T = x.shape[0]

    def body(dxo_ref, x_ref, vec_ref, a_ref, b_ref, f_ref, win_hbm, wout_hbm,
             dx_ref, df_ref, dab_ref, acc_ref, win, wout):
        @pl.when(pl.program_id(0) == 0)
        def _():
            pltpu.sync_copy(win_hbm, win)
            pltpu.sync_copy(wout_hbm, wout)
            acc_ref[...] = jnp.zeros_like(acc_ref)

        dxo = dxo_ref[...]
        x = x_ref[...]
        sh, sc, gate, gn = vec_ref[0:1, :], vec_ref[1:2, :], vec_ref[2:3, :], vec_ref[3:4, :]
        r, n, _ = _rms_fwd(x, gn, sc, sh)
        acc_ref[2:3, :] += _colsum(0.5 * f_ref[...].astype(F32) * dxo)
        dfb = ((0.5 * gate) * dxo).astype(BF16)
        df_ref[...] = dfb
        dh = jnp.zeros((TB, D), F32)
        for j in range(2):
            cols = slice(j * FSH, (j + 1) * FSH)
            ds = _nt(dfb, wout[cols, :])
            av = a_ref[:, cols].astype(F32)
            bv = b_ref[:, cols].astype(F32)
            sg = _sigmoid(av)
            da = (ds * bv * _dsilu(av, sg)).astype(BF16)
            db = (ds * (av * sg)).astype(BF16)
            dab_ref[j] = da
            dab_ref[2 + j] = db
            dh = dh + _nt(da, win[j]) + _nt(db, win[2 + j])
        dx_ref[...] = dxo + _rms_bwd(dh, r, n, gn, sc, acc_ref)

    row = lambda w: pl.BlockSpec((TB, w), lambda i: (i, 0))
    vec8 = pl.BlockSpec((8, D), lambda i: (0, 0))
    return pl.pallas_call(
        body, name=name, grid=(T // TB,),
        in_specs=[row(D), row(D), vec8, row(DFF), row(DFF), row(D), ANY, ANY],
        out_specs=[row(D), pl.BlockSpec((None, TB, D), lambda i: (0, i, 0)),
                   pl.BlockSpec((NCHIP, TB, FSH), lambda i: (0, i, 0)), vec8],
        out_shape=[jax.ShapeDtypeStruct((T, D), F32), jax.ShapeDtypeStruct((1, T, D), BF16),
                   jax.ShapeDtypeStruct((NCHIP, T, FSH), BF16), jax.ShapeDtypeStruct((8, D), F32)],
        scratch_shapes=[pltpu.VMEM((NCHIP, D, FSH), BF16), pltpu.VMEM((DFF, D), BF16)],
        compiler_params=_params("arbitrary"),
    )(dxo, x, vec, a, b, f, w_in, w_out)


def _mm_tn(a, b3, hp, kc, shard_rows, name, into=None, slab=0, slabs=None):
    T, M = a.shape
    P, _, N = b3.shape
    tm = M if M <= 1408 else M // 2
    tk = min(T, DW_TOKENS)
    nk = T // tk
    ni = M // tm
    slabs = P // hp if slabs is None else slabs
    half = shard_rows // 2
    extra = [] if into is None else list(into)

    def body(kc_ref, a_ref, b_ref, *rest):
        o_ref, ra_ref, hbuf, send_sems, recv_sem = rest[-5:]
        p, i, k = pl.program_id(0), pl.program_id(1), pl.program_id(2)
        x, y, c = _coords()
        step = p * ni + i
        slot = step % 2

        def send(p_, i_, slot_):
            dst = ra_ref.at[slab + p_ // hp, pl.ds(pl.multiple_of(i_ * (tm // 2), 8), tm // 2),
                            pl.ds(pl.multiple_of((p_ % hp) * N, LANES), N)]
            return pltpu.make_async_remote_copy(
                src_ref=hbuf.at[slot_], dst_ref=dst, send_sem=send_sems.at[slot_], recv_sem=recv_sem,
                device_id=(x, y, 1 - c), device_id_type=MESH)

        @pl.when(k == 0)
        def _():
            o_ref[...] = jnp.zeros_like(o_ref)

        o_ref[...] += _tn(a_ref[...], b_ref[...])

        @pl.when(k == nk - 1)
        def _():
            @pl.when(step >= 2)
            def _():
                send(p, i, slot).wait_send()

            for j in range(tm // shard_rows):
                start = pl.multiple_of(j * shard_rows + (1 - kc_ref[1]) * half, 8)
                hbuf[slot, j * half:(j + 1) * half, :] = o_ref[pl.ds(start, half), :].astype(BF16)
            send(p, i, slot).start()

        @pl.when((step == P * ni - 1) & (k == nk - 1))
        def _():
            for s in range(min(2, P * ni)):
                send(p, i, (step - s) % 2).wait_send()
            mine = ra_ref.at[slab:slab + P // hp]
            pltpu.make_async_remote_copy(src_ref=mine, dst_ref=mine, send_sem=send_sems.at[0], recv_sem=recv_sem,
                                         device_id=(x, y, 1 - c), device_id_type=MESH).wait_recv()

    return pl.pallas_call(
        body, name=name,
        grid_spec=pltpu.PrefetchScalarGridSpec(
            num_scalar_prefetch=1, grid=(P, ni, nk),
            in_specs=[pl.BlockSpec((tk, tm), lambda p, i, k, kc: (k, i)),
                      pl.BlockSpec((None, tk, N), lambda p, i, k, kc: (p, k, 0))] + [ANY] * len(extra),
            out_specs=[pl.BlockSpec((None, tm, N), lambda p, i, k, kc: (slab + p // hp, i, p % hp)), ANY],
            scratch_shapes=[pltpu.VMEM((2, tm // 2, N), BF16), pltpu.SemaphoreType.DMA((2,)),
                            pltpu.SemaphoreType.DMA]),
        out_shape=[jax.ShapeDtypeStruct((slabs, M, hp * N), F32), jax.ShapeDtypeStruct((slabs, M // 2, hp * N), BF16)],
        input_output_aliases={} if into is None else {3: 0, 4: 1},
        compiler_params=_params("arbitrary", "arbitrary", "arbitrary"),
    )(kc, a, b3, *extra)


def _mix_proj_fwd(x, vec, w_in):
    T = x.shape[0]

    def body(x_ref, vec_ref, w_hbm, h_ref, qr_ref, g_ref, k_ref, v_ref, og_ref, u_ref, ua_ref, ub_ref,
             sa_ref, sb_ref, w):
        @pl.when(pl.program_id(0) == 0)
        def _():
            pltpu.sync_copy(w_hbm, w)

        x = x_ref[...]
        sh, sc, gn, lb = vec_ref[0:1, :], vec_ref[1:2, :], vec_ref[3:4, :], vec_ref[4:5, :]
        _, _, h = _rms_fwd(x, gn, sc, sh)
        hb = h.astype(BF16)
        h_ref[...] = hb
        p = _nn(hb, w[0])
        qr_ref[...] = p[:, :D].astype(BF16)
        fg = lb + (1.0 - lb) * _sigmoid(p[:, D:])
        g_ref[...] = jnp.log(fg)
        k_ref[...] = (1.0 - fg).astype(BF16)
        p = _nn(hb, w[1])
        v_ref[...] = p[:, :D].astype(BF16)
        og_ref[...] = p[:, D:].astype(BF16)
        p = _nn(hb, w[2])
        ua, ub = p[:, :D], p[:, D:]
        u_ref[...] = ua * _sigmoid(ub)
        ua_ref[...] = ua.astype(BF16)
        ub_ref[...] = ub.astype(BF16)
        p = _nn(hb, w[3])
        sa_ref[...] = _sigmoid(p[:, :D]).astype(BF16)
        sb_ref[...] = _sigmoid(p[:, D:]).astype(BF16)

    row = pl.BlockSpec((TB, D), lambda i: (i, 0))
    bf = jax.ShapeDtypeStruct((T, D), BF16)
    f32 = jax.ShapeDtypeStruct((T, D), F32)
    return pl.pallas_call(
        body, name="mix_proj_fwd", grid=(T // TB,),
        in_specs=[row, pl.BlockSpec((8, D), lambda i: (0, 0)), ANY],
        out_specs=[row] * 11,
        out_shape=[bf, bf, f32, bf, bf, bf, f32, bf, bf, bf, bf],
        scratch_shapes=[pltpu.VMEM((NCHIP, D, 2 * D), BF16)],
        compiler_params=_params("arbitrary"),
    )(x, vec, w_in)


def _mix_proj_bwd(dxo, x, vec, dpa, dpb, dpc, w_in):
    T = x.shape[0]

    def body(dxo_ref, x_ref, vec_ref, dpa_ref, dpb_ref, dpc_ref, w_hbm, dx_ref, acc_ref, w):
        @pl.when(pl.program_id(0) == 0)
        def _():
            pltpu.sync_copy(w_hbm, w)
            acc_ref[...] = jnp.zeros_like(acc_ref)

        x = x_ref[...]
        sh, sc, gn = vec_ref[0:1, :], vec_ref[1:2, :], vec_ref[3:4, :]
        r, n, _ = _rms_fwd(x, gn, sc, sh)
        dh = jnp.zeros((TB, D), F32)
        for p in range(8):
            src = dpa_ref[p] if p < 4 else (dpb_ref[p - 4] if p < 6 else dpc_ref[p - 6])
            dh = dh + _nt(src, w[p // 2, :, (p % 2) * D:(p % 2 + 1) * D])
        dx_ref[...] = dxo_ref[...] + _rms_bwd(dh, r, n, gn, sc, acc_ref)

    row = pl.BlockSpec((TB, D), lambda i: (i, 0))
    vec8 = pl.BlockSpec((8, D), lambda i: (0, 0))
    stack = lambda k: pl.BlockSpec((k, TB, D), lambda i: (0, i, 0))
    return pl.pallas_call(
        body, name="mix_proj_bwd", grid=(T // TB,),
        in_specs=[row, row, vec8, stack(4), stack(2), stack(2), ANY],
        out_specs=[row, vec8],
        out_shape=[jax.ShapeDtypeStruct((T, D), F32), jax.ShapeDtypeStruct((8, D), F32)],
        scratch_shapes=[pltpu.VMEM((NCHIP, D, 2 * D), BF16)],
        compiler_params=_params("arbitrary"),
    )(dxo, x, vec, dpa, dpb, dpc, w_in)


def _tri(lower):
    r = lax.broadcasted_iota(jnp.int32, (CHUNK, CHUNK), 0)
    c = lax.broadcasted_iota(jnp.int32, (CHUNK, CHUNK), 1)
    return (c <= r) if lower else (c >= r)


def _cumsum_rows(mask, g):
    hi = g.astype(BF16)
    rest = g - hi.astype(F32)
    mid = rest.astype(BF16)
    low = (rest - mid.astype(F32)).astype(BF16)
    n = g.shape[1]
    p = _nn(mask.astype(BF16), jnp.concatenate([hi, mid, low], axis=1))
    return (p[:, 2 * n:] + p[:, n:2 * n]) + p[:, :n]


def _chunk_decay(low, g, nck):
    bs, mids, lasts = [], [], []
    for c in range(nck):
        gc = g[c * CHUNK:(c + 1) * CHUNK]
        bs.append(_cumsum_rows(low, gc))
        mids.append(_colsum(gc[0:CHUNK // 2]))
        lasts.append(_colsum(gc))
    spread = lambda rows: jnp.concatenate([jnp.broadcast_to(r, (CHUNK, DK)) for r in rows], axis=0)
    return jnp.concatenate(bs, axis=0), spread(mids), spread(lasts), lasts


def _hgrn_fwd(qr, g, k, v, og, vec):
    T = qr.shape[0]
    nck = CB // CHUNK

    def body(qr_ref, g_ref, k_ref, v_ref, og_ref, vec_ref, out_ref, o_ref, st_ref, state):
        @pl.when(pl.program_id(1) == 0)
        def _():
            state[...] = jnp.zeros_like(state)

        low = _tri(True)
        qv = qr_ref[...].astype(F32)
        q = qv * _sigmoid(qv) * (DK ** -0.5)
        kk = k_ref[...].astype(F32)
        vb = v_ref[...]
        b, mid, last, lasts = _chunk_decay(low, g_ref[...], nck)
        qt = (q * jnp.exp(b - mid)).astype(BF16)
        kt = (kk * jnp.exp(mid - b)).astype(BF16)
        qe = (q * jnp.exp(b)).astype(BF16)
        kd = (kk * jnp.exp(last - b)).astype(BF16)
        intra, grow = [], []
        for c in range(nck):
            r = slice(c * CHUNK, (c + 1) * CHUNK)
            att = jnp.where(low, _nt(qt[r], kt[r]), 0.0).astype(BF16)
            intra.append(_nn(att, vb[r]))
            grow.append(_tn(vb[r], kd[r]))
        st = state[...]
        inter = []
        for c in range(nck):
            stb = st.astype(BF16)
            st_ref[c] = stb
            inter.append(_nt(qe[c * CHUNK:(c + 1) * CHUNK], stb))
            st = st * jnp.exp(lasts[c]) + grow[c]
        state[...] = st
        o = jnp.concatenate(intra, axis=0) + jnp.concatenate(inter, axis=0)
        o_ref[...] = o
        ogv = og_ref[...].astype(F32)
        rms = lax.rsqrt(jnp.mean(o * o, axis=-1, keepdims=True) + EPS)
        out_ref[...] = (o * rms * vec_ref[5:6, :] * (ogv * _sigmoid(ogv))).astype(BF16)

    blk = pl.BlockSpec((CB, DK), lambda h, i: (i, h))
    return pl.pallas_call(
        body, name="hgrn_fwd", grid=(HEADS, T // CB),
        in_specs=[blk, blk, blk, blk, blk, pl.BlockSpec((8, DK), lambda h, i: (0, h))],
        out_specs=[blk, blk, pl.BlockSpec((None, nck, DK, DK), lambda h, i: (h, i, 0, 0))],
        out_shape=[jax.ShapeDtypeStruct((T, D), BF16), jax.ShapeDtypeStruct((T, D), F32),
                   jax.ShapeDtypeStruct((HEADS, T // CHUNK, DK, DK), BF16)],
        scratch_shapes=[pltpu.VMEM((DK, DK), F32)],
        compiler_params=_params("parallel", "arbitrary"),
    )(qr, g, k, v, og, vec)


def _hgrn_bwd(dout, og, qr, g, k, v, o, st, vec):
    T = qr.shape[0]
    nck = CB // CHUNK
    nb = T // CB

    def body(dout_ref, og_ref, qr_ref, g_ref, k_ref, v_ref, o_ref, st_ref, vec_ref,
             dp_ref, acc_ref, dstate):
        @pl.when(pl.program_id(1) == 0)
        def _():
            dstate[...] = jnp.zeros_like(dstate)
            acc_ref[...] = jnp.zeros_like(acc_ref)

        o = o_ref[...]
        ogv = og_ref[...].astype(F32)
        dout = dout_ref[...].astype(F32)
        hg = vec_ref[5:6, :]
        sgo = _sigmoid(ogv)
        rms = lax.rsqrt(jnp.mean(o * o, axis=-1, keepdims=True) + EPS)
        ohat = o * rms
        dp_ref[3] = (dout * (ohat * hg) * _dsilu(ogv, sgo)).astype(BF16)
        don = dout * (ogv * sgo)
        acc_ref[0:1, :] += _colsum(don * ohat)
        dohat = don * hg
        dob = (rms * (dohat - ohat * jnp.mean(dohat * ohat, axis=-1, keepdims=True))).astype(BF16)

        low = _tri(True)
        upp = _tri(False)
        lb = vec_ref[4:5, :]
        qv = qr_ref[...].astype(F32)
        sgq = _sigmoid(qv)
        q = qv * sgq * (DK ** -0.5)
        kk = k_ref[...].astype(F32)
        vb = v_ref[...]
        gv = g_ref[...]
        b, mid, last, lasts = _chunk_decay(low, gv, nck)
        eq = jnp.exp(b - mid)
        ek = jnp.exp(mid - b)
        eb = jnp.exp(b)
        ed = jnp.exp(last - b)
        qtb, ktb, qeb, kdb = ((t).astype(BF16) for t in (q * eq, kk * ek, q * eb, kk * ed))
        rows = [slice(c * CHUNK, (c + 1) * CHUNK) for c in range(nck)]

        dv1, dqt, dkt, dqe, grow = [], [], [], [], []
        for c, r in enumerate(rows):
            att = jnp.where(low, _nt(qtb[r], ktb[r]), 0.0).astype(BF16)
            datt = jnp.where(low, _nt(dob[r], vb[r]), 0.0).astype(BF16)
            dv1.append(_tn(att, dob[r]))
            dqt.append(_nn(datt, ktb[r]))
            dkt.append(_tn(datt, qtb[r]))
            dqe.append(_nn(dob[r], st_ref[c]))
            grow.append(_tn(dob[r], qeb[r]))
        ds = dstate[...]
        ds1b, dl_state = [None] * nck, [None] * nck
        for c in reversed(range(nck)):
            el = jnp.exp(lasts[c])
            ds1b[c] = ds.astype(BF16)
            dl_state[c] = el * _colsum(ds * st_ref[c].astype(F32))
            ds = ds * el + grow[c]
        dstate[...] = ds
        dkd = jnp.concatenate([_nn(vb[r], ds1b[c]) for c, r in enumerate(rows)], axis=0)
        dv = jnp.concatenate(dv1, axis=0) + jnp.concatenate([_nt(kdb[r], ds1b[c]) for c, r in enumerate(rows)], axis=0)
        dqt, dkt, dqe = (jnp.concatenate(t, axis=0) for t in (dqt, dkt, dqe))
        dq = dqt * eq + dqe * eb
        dk = dkt * ek + dkd * ed
        dkdkd = dkd * kdb.astype(F32)
        db = dqt * qtb.astype(F32) - dkt * ktb.astype(F32) + dqe * qeb.astype(F32) - dkdkd
        dg = jnp.concatenate([_cumsum_rows(upp, db[r]) + (_colsum(dkdkd[r]) + dl_state[c])
                              for c, r in enumerate(rows)], axis=0)
        fg = jnp.exp(gv)
        dfg = dg * jnp.exp(-gv) - dk
        one_m_sig = (1.0 - fg) * (1.0 / (1.0 - lb))
        dp_ref[0] = (dq * (DK ** -0.5) * _dsilu(qv, sgq)).astype(BF16)
        dp_ref[1] = (dfg * (fg - lb) * one_m_sig).astype(BF16)
        dp_ref[2] = dv.astype(BF16)
        dlb = _colsum(dfg * one_m_sig) * (lb * (1.0 - lb))
        acc_ref[1:2, :] += dlb
        acc_ref[2:3, :] -= dlb

    blk = pl.BlockSpec((CB, DK), lambda h, i: (nb - 1 - i, h))
    return pl.pallas_call(
        body, name="hgrn_bwd", grid=(HEADS, nb),
        in_specs=[blk, blk, blk, blk, blk, blk, blk,
                  pl.BlockSpec((None, nck, DK, DK), lambda h, i: (h, nb - 1 - i, 0, 0)),
                  pl.BlockSpec((8, DK), lambda h, i: (0, h))],
        out_specs=[pl.BlockSpec((4, CB, DK), lambda h, i: (0, nb - 1 - i, h)),
                   pl.BlockSpec((8, DK), lambda h, i: (0, h))],
        out_shape=[jax.ShapeDtypeStruct((4, T, D), BF16), jax.ShapeDtypeStruct((8, D), F32)],
        scratch_shapes=[pltpu.VMEM((DK, DK), F32)],
        compiler_params=_params("parallel", "arbitrary"),
    )(dout, og, qr, g, k, v, o, st, vec)


def _ln_fwd(uc, lg, lbias):
    mu = jnp.mean(uc, axis=-1, keepdims=True)
    xc = uc - mu
    rstd = lax.rsqrt(jnp.mean(xc * xc, axis=-1, keepdims=True) + EPS)
    z = xc * rstd
    return rstd, z, z * lg + lbias


LANES = 128
SUBLANES = 8
CONV_ROWS = 64


def _lane_tiles():
    return [slice(l * LANES, (l + 1) * LANES) for l in range(D // LANES)]


def _row_shifts(x):
    n = x.shape[0]
    return [x] + [pltpu.roll(x, n - r, axis=0) for r in range(1, SUBLANES)]


TAPS_PAST = tuple(HALO - (CONV_K - 1) + j for j in range(CONV_K))
TAPS_AHEAD = tuple(CONV_K - 1 - j for j in range(CONV_K))


def _tap_windows(shifted, starts, r0, rows):
    for r in range(SUBLANES):
        taps = [(j, s // SUBLANES) for j, s in enumerate(starts) if s % SUBLANES == r]
        if not taps:
            continue
        lo = min(a for _, a in taps)
        hi = max(a for _, a in taps)
        span = shifted[r][r0 + lo * SUBLANES:r0 + hi * SUBLANES + rows]
        for j, a in taps:
            yield j, span[(a - lo) * SUBLANES:(a - lo) * SUBLANES + rows]


def _conv_fwd(u, cw, cvec):
    T = u.shape[0]
    per = TB // HALO

    def body(u_ref, halo_ref, cw_ref, cvec_ref, us_ref, uc_ref, pad):
        i = pl.program_id(0)
        pad[0:HALO, :] = jnp.where(i > 0, halo_ref[...], 0.0)
        pad[HALO:, :] = u_ref[...]
        for lanes in _lane_tiles():
            shifted = _row_shifts(pad[:, lanes])
            for r0 in range(0, TB, CONV_ROWS):
                acc = jnp.broadcast_to(cvec_ref[0:1, lanes], (CONV_ROWS, LANES))
                for j, window in _tap_windows(shifted, TAPS_PAST, r0, CONV_ROWS):
                    acc = acc + cw_ref[j:j + 1, lanes] * window
                uc_ref[r0:r0 + CONV_ROWS, lanes] = acc
        _, _, ul = _ln_fwd(uc_ref[...], cvec_ref[1:2, :], cvec_ref[2:3, :])
        us_ref[...] = (ul * _sigmoid(ul)).astype(BF16)

    row = pl.BlockSpec((TB, D), lambda i: (i, 0))
    return pl.pallas_call(
        body, name="conv_fwd", grid=(T // TB,),
        in_specs=[row, pl.BlockSpec((HALO, D), lambda i: (jnp.maximum(i * per - 1, 0), 0)),
                  pl.BlockSpec((32, D), lambda i: (0, 0)), pl.BlockSpec((8, D), lambda i: (0, 0))],
        out_specs=[row, row],
        out_shape=[jax.ShapeDtypeStruct((T, D), BF16), jax.ShapeDtypeStruct((T, D), F32)],
        scratch_shapes=[pltpu.VMEM((TB + HALO, D), F32)],
        compiler_params=_params("parallel"),
    )(u, u, cw, cvec)


def _conv_bwd_taps(duc, u, ua, ub, cw):
    T = u.shape[0]
    per = TB // HALO
    nblk = T // TB

    def body(duc_ref, dnext_ref, u_ref, uprev_ref, ua_ref, ub_ref, cw_ref, dp_ref, dcw_ref, upad, dpad, dcw):
        i = pl.program_id(0)

        @pl.when(i == 0)
        def _():
            dcw[...] = jnp.zeros_like(dcw)

        upad[0:HALO, :] = jnp.where(i > 0, uprev_ref[...], 0.0)
        upad[HALO:, :] = u_ref[...]
        dpad[0:TB, :] = duc_ref[...]
        dpad[TB:, :] = jnp.where(i < nblk - 1, dnext_ref[...], 0.0)
        for lanes in _lane_tiles():
            ushift = _row_shifts(upad[:, lanes])
            dshift = _row_shifts(dpad[:, lanes])
            for r0 in range(0, TB, CONV_ROWS):
                rows = slice(r0, r0 + CONV_ROWS)
                duc = duc_ref[rows, lanes]
                for j, window in _tap_windows(ushift, TAPS_PAST, r0, CONV_ROWS):
                    prod = duc * window
                    dcw[j, :, lanes] += jnp.sum(prod.reshape(CONV_ROWS // SUBLANES, SUBLANES, LANES), axis=0)
                du = jnp.zeros((CONV_ROWS, LANES), F32)
                for j, window in _tap_windows(dshift, TAPS_AHEAD, r0, CONV_ROWS):
                    du = du + cw_ref[j:j + 1, lanes] * window
                ua = ua_ref[rows, lanes].astype(F32)
                sg = _sigmoid(ub_ref[rows, lanes].astype(F32))
                dp_ref[0, rows, lanes] = (du * sg).astype(BF16)
                dp_ref[1, rows, lanes] = (du * ua * sg * (1.0 - sg)).astype(BF16)

        @pl.when(i == nblk - 1)
        def _():
            dcw_ref[...] = jnp.sum(dcw[...], axis=1)

    row = pl.BlockSpec((TB, D), lambda i: (i, 0))
    return pl.pallas_call(
        body, name="conv_bwd_taps", grid=(nblk,),
        in_specs=[row, pl.BlockSpec((HALO, D), lambda i: (jnp.minimum((i + 1) * per, T // HALO - 1), 0)),
                  row, pl.BlockSpec((HALO, D), lambda i: (jnp.maximum(i * per - 1, 0), 0)),
                  row, row, pl.BlockSpec((32, D), lambda i: (0, 0))],
        out_specs=[pl.BlockSpec((2, TB, D), lambda i: (0, i, 0)), pl.BlockSpec((32, D), lambda i: (0, 0))],
        out_shape=[jax.ShapeDtypeStruct((2, T, D), BF16), jax.ShapeDtypeStruct((32, D), F32)],
        scratch_shapes=[pltpu.VMEM((TB + HALO, D), F32), pltpu.VMEM((TB + HALO, D), F32),
                        pltpu.VMEM((32, SUBLANES, D), F32)],
        compiler_params=_params("arbitrary"),
    )(duc, duc, u, u, ua, ub, cw)


def _merge_fwd(x, oa, us, sa, sb, vec, w_ho, w_co, w_mo):
    T = x.shape[0]

    def body(x_ref, oa_ref, us_ref, sa_ref, sb_ref, vec_ref, who_hbm, wco_hbm, wmo_hbm,
             xo_ref, ya_ref, yb_ref, mg_ref, mo_ref, who, wco, wmo):
        @pl.when(pl.program_id(0) == 0)
        def _():
            pltpu.sync_copy(who_hbm, who)
            pltpu.sync_copy(wco_hbm, wco)
            pltpu.sync_copy(wmo_hbm, wmo)

        ya = _nn(oa_ref[...], who[...])
        yb = _nn(us_ref[...], wco[...])
        mg = (sa_ref[...].astype(F32) * ya + sb_ref[...].astype(F32) * yb).astype(BF16)
        mo = _nn(mg, wmo[...])
        xo_ref[...] = x_ref[...] + vec_ref[2:3, :] * mo
        ya_ref[...] = ya.astype(BF16)
        yb_ref[...] = yb.astype(BF16)
        mg_ref[...] = mg
        mo_ref[...] = mo.astype(BF16)

    row = pl.BlockSpec((TB, D), lambda i: (i, 0))
    bf = jax.ShapeDtypeStruct((T, D), BF16)
    wv = pltpu.VMEM((D, D), BF16)
    return pl.pallas_call(
        body, name="merge_fwd", grid=(T // TB,),
        in_specs=[row, row, row, row, row, pl.BlockSpec((8, D), lambda i: (0, 0)), ANY, ANY, ANY],
        out_specs=[row] * 5,
        out_shape=[jax.ShapeDtypeStruct((T, D), F32), bf, bf, bf, bf],
        scratch_shapes=[wv, wv, wv],
        compiler_params=_params("arbitrary"),
    )(x, oa, us, sa, sb, vec, w_ho, w_co, w_mo)


def _merge_bwd(dxo, mo, ya, yb, sa, sb, uc, vec, cvec, w_ho, w_co, w_mo):
    T = dxo.shape[0]

    def body(dxo_ref, mo_ref, ya_ref, yb_ref, sa_ref, sb_ref, uc_ref, vec_ref, cvec_ref, who_hbm, wco_hbm, wmo_hbm,
             dmo_ref, dya_ref, dyb_ref, doa_ref, duc_ref, dp_ref, acc_ref, cacc_ref, who, wco, wmo):
        @pl.when(pl.program_id(0) == 0)
        def _():
            pltpu.sync_copy(who_hbm, who)
            pltpu.sync_copy(wco_hbm, wco)
            pltpu.sync_copy(wmo_hbm, wmo)
            acc_ref[...] = jnp.zeros_like(acc_ref)
            cacc_ref[...] = jnp.zeros_like(cacc_ref)

        dxo = dxo_ref[...]
        acc_ref[2:3, :] += _colsum(mo_ref[...].astype(F32) * dxo)
        dmo = (vec_ref[2:3, :] * dxo).astype(BF16)
        dmo_ref[...] = dmo
        dmg = _nt(dmo, wmo[...])
        sa = sa_ref[...].astype(F32)
        sb = sb_ref[...].astype(F32)
        dya = (sa * dmg).astype(BF16)
        dyb = (sb * dmg).astype(BF16)
        dya_ref[...] = dya
        dyb_ref[...] = dyb
        dp_ref[0] = (dmg * ya_ref[...].astype(F32) * sa * (1.0 - sa)).astype(BF16)
        dp_ref[1] = (dmg * yb_ref[...].astype(F32) * sb * (1.0 - sb)).astype(BF16)
        doa_ref[...] = _nt(dya, who[...]).astype(BF16)
        dus = _nt(dyb, wco[...])
        lg = cvec_ref[1:2, :]
        rstd, z, ul = _ln_fwd(uc_ref[...], lg, cvec_ref[2:3, :])
        dul = dus * _dsilu(ul, _sigmoid(ul))
        cacc_ref[1:2, :] += _colsum(dul * z)
        cacc_ref[2:3, :] += _colsum(dul)
        dz = dul * lg
        duc = rstd * (dz - jnp.mean(dz, axis=-1, keepdims=True) - z * jnp.mean(dz * z, axis=-1, keepdims=True))
        cacc_ref[0:1, :] += _colsum(duc)
        duc_ref[...] = duc

    row = pl.BlockSpec((TB, D), lambda i: (i, 0))
    one = pl.BlockSpec((None, TB, D), lambda i: (0, i, 0))
    vec8 = pl.BlockSpec((8, D), lambda i: (0, 0))
    bf = jax.ShapeDtypeStruct((T, D), BF16)
    bf1 = jax.ShapeDtypeStruct((1, T, D), BF16)
    acc = jax.ShapeDtypeStruct((8, D), F32)
    wv = pltpu.VMEM((D, D), BF16)
    return pl.pallas_call(
        body, name="merge_bwd", grid=(T // TB,),
        in_specs=[row, row, row, row, row, row, row, vec8, vec8, ANY, ANY, ANY],
        out_specs=[one, one, one, row, row, pl.BlockSpec((2, TB, D), lambda i: (0, i, 0)), vec8, vec8],
        out_shape=[bf1, bf1, bf1, bf, jax.ShapeDtypeStruct((T, D), F32), jax.ShapeDtypeStruct((2, T, D), BF16), acc, acc],
        scratch_shapes=[wv, wv, wv],
        compiler_params=_params("arbitrary"),
    )(dxo, mo, ya, yb, sa, sb, uc, vec, cvec, w_ho, w_co, w_mo)


def _pack_rows(parts, total, name):
    def body(*refs):
        out = refs[-1]
        out[...] = jnp.zeros_like(out)
        for ref, (_, src, n, dst) in zip(refs[:-1], parts):
            out[dst:dst + n, :] = ref[src:src + n, :]

    arrs = [p[0] for p in parts]
    return pl.pallas_call(
        body, name=name, in_specs=[pl.BlockSpec(a.shape, lambda: (0, 0)) for a in arrs],
        out_specs=pl.BlockSpec((total, D), lambda: (0, 0)),
        out_shape=jax.ShapeDtypeStruct((total, D), F32),
    )(*arrs)


PACK_ROWS = 56
PACK_AT = {"ada_b": 0, "loss": 9, "norm_ffn1": 10, "norm_mix": 11, "hgrn_g": 12, "conv_b": 13, "conv_ln_g": 14,
           "conv_ln_b": 15, "norm_ffn2": 16, "norm_final": 17, "hgrn_lb": 18, "conv_w": 20}


def _local_step(x, tgt, mod, small, kc, weight, reduce, reduce_small):
    lb = jax.nn.sigmoid(small["hgrn_lb"][0:1] - small["hgrn_lb"][1:2])
    vec1 = _pack_rows([(mod, 0, 3, 0), (small["norm_ffn1"], 0, 1, 3)], 8, "pack_vec1")
    vec2 = _pack_rows([(mod, 3, 3, 0), (small["norm_mix"], 0, 1, 3), (lb, 0, 1, 4), (small["hgrn_g"], 0, 1, 5)],
                      8, "pack_vec2")
    vec3 = _pack_rows([(mod, 6, 3, 0), (small["norm_ffn2"], 0, 1, 3)], 8, "pack_vec3")
    cvec = _pack_rows([(small["conv_b"], 0, 1, 0), (small["conv_ln_g"], 0, 1, 1), (small["conv_ln_b"], 0, 1, 2)],
                      8, "pack_cvec")
    cw = small["conv_w"]
    gvec = _pack_rows([(small["norm_final"], 0, 1, 0)], 8, "pack_gvec")

    wg = {n: weight(n, vec1) for n in ("ffn1_w_in", "ffn1_w_out")}
    x1, h1, a1, b1, s1, f1 = _ffn_fwd(x, vec1, wg["ffn1_w_in"], wg["ffn1_w_out"], "ffn1_fwd")
    wg["mix_w_in"] = weight("mix_w_in", x1)
    h2, qr, g, k, v, og, u, ua, ub, sa, sb = _mix_proj_fwd(x1, vec2, wg["mix_w_in"])
    oa, o, st = _hgrn_fwd(qr, g, k, v, og, vec2)
    us, uc = _conv_fwd(u, cw, cvec)
    wg.update({n: weight(n, us) for n in ("hgrn_w_o", "conv_w_o", "mix_w_out")})
    x2, ya, yb, mg, mo = _merge_fwd(x1, oa, us, sa, sb, vec2, wg["hgrn_w_o"], wg["conv_w_o"], wg["mix_w_out"])
    wg.update({n: weight(n, x2) for n in ("ffn2_w_in", "ffn2_w_out")})
    dx3, h3, a3, b3, s3, f3, acc_head = _ffn_fwd(x2, vec3, wg["ffn2_w_in"], wg["ffn2_w_out"], "ffn2_fwd",
                                                 head=(tgt, gvec))

    dx2, df3, dab3, acc3 = _ffn_bwd(dx3, x2, vec3, a3, b3, f3, wg["ffn2_w_in"], wg["ffn2_w_out"], "ffn2_bwd")
    tok = reduce(("ffn2_w_out", "ffn2_w_in"), [_mm_tn(s3, df3, 1, kc, DFF // NCHIP, "ffn2_dwout"),
                                               _mm_tn(h3, dab3, 1, kc, D, "ffn2_dwin")])
    vec2b = vec2 + tok[0:1, 0:1]
    dmo, dya, dyb, doa, duc, dpc, acc_m, acc_c = _merge_bwd(dx2, mo, ya, yb, sa, sb, uc, vec2b, cvec,
                                                            wg["hgrn_w_o"], wg["conv_w_o"], wg["mix_w_out"])
    tok = reduce(("mix_w_out", "hgrn_w_o", "conv_w_o"),
                 [_mm_tn(mg, dmo, 1, kc, D // NCHIP, "mix_dwout"), _mm_tn(oa, dya, 1, kc, D // NCHIP, "hgrn_dwo"),
                  _mm_tn(us, dyb, 1, kc, D // NCHIP, "conv_dwo")])
    vec2c = vec2 + tok[0:1, 0:1]
    dpb, dcw = _conv_bwd_taps(duc, u, ua, ub, cw)
    dpa, acc_h = _hgrn_bwd(doa, og, qr, g, k, v, o, st, vec2c)
    dx1, acc2 = _mix_proj_bwd(dx2, x1, vec2c, dpa, dpb, dpc, wg["mix_w_in"])
    gmix = _mm_tn(h2, dpa, 2, kc, D, "mix_dwin_a", slabs=NCHIP)
    gmix = _mm_tn(h2, dpb, 2, kc, D, "mix_dwin_b", into=gmix, slab=2, slabs=NCHIP)
    gmix = _mm_tn(h2, dpc, 2, kc, D, "mix_dwin_c", into=gmix, slab=3, slabs=NCHIP)
    tok = reduce(("mix_w_in",), [gmix])
    vec1b = vec1 + tok[0:1, 0:1]
    dx0, df1, dab1, acc1 = _ffn_bwd(dx1, x, vec1b, a1, b1, f1, wg["ffn1_w_in"], wg["ffn1_w_out"], "ffn1_bwd")

    at = PACK_AT
    packed = _pack_rows([
        (acc1, 0, 3, at["ada_b"]), (acc2, 0, 2, at["ada_b"] + 3), (acc_m, 2, 1, at["ada_b"] + 5),
        (acc3, 0, 3, at["ada_b"] + 6), (acc_head, 1, 1, at["loss"]), (acc1, 3, 1, at["norm_ffn1"]),
        (acc2, 3, 1, at["norm_mix"]), (acc_h, 0, 1, at["hgrn_g"]), (acc_c, 0, 3, at["conv_b"]),
        (acc3, 3, 1, at["norm_ffn2"]), (acc_head, 0, 1, at["norm_final"]), (acc_h, 1, 2, at["hgrn_lb"]),
        (dcw, 0, CONV_K, at["conv_w"])], PACK_ROWS, "pack_small_grads")
    done = reduce_small(packed)
    reduce(("ffn1_w_out", "ffn1_w_in"), [_mm_tn(s1, df1, 1, kc, DFF // NCHIP, "ffn1_dwout"),
                                         _mm_tn(h1, dab1, 1, kc, D, "ffn1_dwin")], done)
    return dx0


BLOCK_BYTES = 3 * 512 * 1024


def _row_block(rows, cols):
    for br in (512, 352, 256, 176, 128, 64, 32, 16, 8):
        if rows % br == 0 and br * cols * 4 <= BLOCK_BYTES:
            return br
    return rows


def _cast_into_slot(w, kc, name, after):
    R, C = w.shape
    br = _row_block(R, C)

    def body(kc_ref, w_ref, after_ref, o_ref):
        o_ref[...] = w_ref[...].astype(BF16)

    return pl.pallas_call(
        body, name=name,
        grid_spec=pltpu.PrefetchScalarGridSpec(
            num_scalar_prefetch=1, grid=(R // br,),
            in_specs=[pl.BlockSpec((br, C), lambda i, kc: (i, 0)), ANY],
            out_specs=pl.BlockSpec((None, br, C), lambda i, kc: (kc[0], i, 0))),
        out_shape=jax.ShapeDtypeStruct((NCHIP, R, C), BF16), compiler_params=_params("parallel"),
    )(kc, w, after)


def _adamw(w, g, m, v, name, after=None, copy_grad=False):
    R, C = w.shape
    br = _row_block(R, C)
    extra = [] if after is None else [after]
    nout = 4 if copy_grad else 3

    def body(w_ref, g_ref, m_ref, v_ref, *rest):
        d_ref, nm_ref, nv_ref = rest[-nout:][:3]
        gv = g_ref[...]
        if copy_grad:
            rest[-1][...] = gv
        nm = ADAM_B1 * m_ref[...] + (1.0 - ADAM_B1) * gv
        nv = ADAM_B2 * v_ref[...] + (1.0 - ADAM_B2) * (gv * gv)
        m_hat = nm / (1.0 - ADAM_B1 ** ADAM_STEP)
        v_hat = nv / (1.0 - ADAM_B2 ** ADAM_STEP)
        d_ref[...] = -ADAM_LR * (m_hat / (jnp.sqrt(v_hat) + ADAM_EPS) + ADAM_WD * w_ref[...])
        nm_ref[...] = nm
        nv_ref[...] = nv

    blk = pl.BlockSpec((br, C), lambda i: (i, 0))
    out = jax.ShapeDtypeStruct((R, C), F32)
    return pl.pallas_call(
        body, name=name, grid=(R // br,), in_specs=[blk] * 4 + [ANY] * len(extra), out_specs=[blk] * nout,
        out_shape=[out] * nout, compiler_params=_params("parallel"),
    )(w, g, m, v, *extra)


def _coords():
    return lax.axis_index("x"), lax.axis_index("y"), lax.axis_index("c")


def _flip(v, bit):
    return 1 - v if bit else v


def _allgather8(v, name):
    R, C = v.shape

    def body(v_ref, out_ref, send_sems, recv_sems, local_sem):
        x, y, c = _coords()
        me = 4 * x + 2 * y + c
        mine = pltpu.make_async_copy(v_ref, out_ref.at[me], local_sem)
        mine.start()

        def copy(m, block):
            peer = (_flip(x, m & 4), _flip(y, m & 2), _flip(c, m & 1))
            return pltpu.make_async_remote_copy(
                src_ref=v_ref, dst_ref=out_ref.at[block], send_sem=send_sems.at[m - 1],
                recv_sem=recv_sems.at[m - 1], device_id=peer, device_id_type=MESH)

        sends = [copy(m, me) for m in range(1, 8)]
        for cp in sends:
            cp.start()
        for m in range(1, 8):
            sender = 4 * _flip(x, m & 4) + 2 * _flip(y, m & 2) + _flip(c, m & 1)
            copy(m, sender).wait_recv()
        for cp in sends:
            cp.wait_send()
        mine.wait()

    vm = pl.BlockSpec(memory_space=pltpu.VMEM)
    return pl.pallas_call(
        body, name=name, in_specs=[vm], out_specs=vm,
        out_shape=jax.ShapeDtypeStruct((8, R, C), F32),
        scratch_shapes=[pltpu.SemaphoreType.DMA((7,)), pltpu.SemaphoreType.DMA((7,)), pltpu.SemaphoreType.DMA],
    )(v)


HBM = pl.BlockSpec(memory_space=pltpu.HBM)
SEM = pl.BlockSpec(memory_space=pltpu.SEMAPHORE)
EFFECT = pltpu.SideEffectType.DATAFLOW_SIDE_EFFECTING


def _chip_peer(x, y, m):
    px, py = _flip(x, m & 2), _flip(y, m & 1)
    return px, py, 2 * px + py


def _core_rows(land, c):
    half = land.shape[1] // 2
    return pl.ds(pl.multiple_of(c * half, 16), half)


def _gather_start(lands, groups, halved, after, name):
    n, ng, na = len(lands), len(groups), len(after)

    def body(*refs):
        ins = refs[:n]
        sends, recvs = refs[n + na:n + na + ng], refs[n + na + ng:n + na + 2 * ng]
        token = refs[n + na + 2 * ng + n]
        x, y, c = _coords()
        k = 2 * x + y
        for gi, grp in enumerate(groups):
            for j, t in enumerate(grp):
                mine = ins[t].at[k, _core_rows(ins[t], c), :] if halved[gi] else ins[t].at[k]
                for m in (1, 2, 3):
                    px, py, _ = _chip_peer(x, y, m)
                    pltpu.make_async_remote_copy(
                        src_ref=mine, dst_ref=mine, send_sem=sends[gi].at[3 * j + m - 1],
                        recv_sem=recvs[gi].at[3 * j + m - 1], device_id=(px, py, c), device_id_type=MESH).start()
        token[...] = jnp.zeros_like(token)

    sems = [pltpu.SemaphoreType.DMA((3 * len(g),)) for g in groups]
    out = pl.pallas_call(
        body, name=name,
        out_shape=sems + sems + [pltpu.HBM(a.shape, a.dtype) for a in lands] + [jax.ShapeDtypeStruct((8, 128), F32)],
        in_specs=[HBM] * n + [ANY] * na,
        out_specs=[SEM] * (2 * ng) + [HBM] * n + [pl.BlockSpec(memory_space=pltpu.VMEM)],
        input_output_aliases={t: 2 * ng + t for t in range(n)},
        compiler_params=pltpu.CompilerParams(has_side_effects=EFFECT),
    )(*[pltpu.with_memory_space_constraint(a, pltpu.HBM) for a in lands], *after)
    return out[:ng], out[ng:2 * ng], out[2 * ng:2 * ng + n], out[2 * ng + n]


def _gather_wait(lands, halved, send_sem, recv_sem, after, name):
    n = len(lands)

    def body(*refs):
        ins, send, recv = refs[:n], refs[n], refs[n + 1]
        x, y, c = _coords()
        k = 2 * x + y
        for j in range(n):
            rows = _core_rows(ins[j], c)
            for m in (1, 2, 3):
                px, py, pk = _chip_peer(x, y, m)
                cp = pltpu.make_async_remote_copy(
                    src_ref=ins[j].at[k, rows, :] if halved else ins[j].at[k],
                    dst_ref=ins[j].at[pk, rows, :] if halved else ins[j].at[pk], send_sem=send.at[3 * j + m - 1],
                    recv_sem=recv.at[3 * j + m - 1], device_id=(px, py, c), device_id_type=MESH)
                cp.wait_send()
                cp.wait_recv()

    return pl.pallas_call(
        body, name=name, out_shape=[pltpu.HBM(a.shape, a.dtype) for a in lands],
        in_specs=[HBM] * n + [SEM, SEM, ANY], out_specs=[HBM] * n,
        input_output_aliases={j: j for j in range(n)},
        compiler_params=pltpu.CompilerParams(has_side_effects=EFFECT),
    )(*lands, send_sem, recv_sem, after)


def _sibling_fill(lands, name):
    n = len(lands)

    def body(*refs):
        ins = refs[:n]
        send_sems, recv_sems = refs[2 * n:]
        x, y, c = _coords()
        sends, recvs = [], []
        for t in range(n):
            for m in (1, 2, 3):
                _, _, pk = _chip_peer(x, y, m)
                for rows, lst in ((_core_rows(ins[t], c), sends), (_core_rows(ins[t], 1 - c), recvs)):
                    lst.append(pltpu.make_async_remote_copy(
                        src_ref=ins[t].at[pk, rows, :], dst_ref=ins[t].at[pk, rows, :],
                        send_sem=send_sems.at[3 * t + m - 1], recv_sem=recv_sems.at[3 * t + m - 1],
                        device_id=(x, y, 1 - c), device_id_type=MESH))
        for cp in sends:
            cp.start()
        for cp in recvs:
            cp.wait_recv()
        for cp in sends:
            cp.wait_send()

    return pl.pallas_call(
        body, name=name, in_specs=[ANY] * n, out_specs=[ANY] * n,
        out_shape=[jax.ShapeDtypeStruct(a.shape, a.dtype) for a in lands],
        input_output_aliases={t: t for t in range(n)},
        scratch_shapes=[pltpu.SemaphoreType.DMA((3 * n,)), pltpu.SemaphoreType.DMA((3 * n,))],
    )(*lands)


def _scatter_start(srcs, name, after=()):
    n, na = len(srcs), len(after)

    def body(*refs):
        ins, lands = refs[:n], refs[n:2 * n]
        send, recv = refs[2 * n + na], refs[2 * n + na + 1]
        token = refs[2 * n + na + 2 + 2 * n]
        x, y, c = _coords()
        k = 2 * x + y
        for t in range(n):
            for m in (1, 2, 3):
                px, py, pk = _chip_peer(x, y, m)
                pltpu.make_async_remote_copy(
                    src_ref=ins[t].at[pk], dst_ref=lands[t].at[k], send_sem=send.at[3 * t + m - 1],
                    recv_sem=recv.at[3 * t + m - 1], device_id=(px, py, c), device_id_type=MESH).start()
        token[...] = jnp.zeros_like(token)

    sem = pltpu.SemaphoreType.DMA((3 * n,))
    hbm = [pltpu.HBM(a.shape, a.dtype) for a in srcs]
    operands = list(srcs) + [lax.empty(a.shape, a.dtype) for a in srcs]
    out = pl.pallas_call(
        body, name=name, out_shape=[sem, sem] + hbm + hbm + [jax.ShapeDtypeStruct((8, 128), F32)],
        in_specs=[HBM] * (2 * n) + [ANY] * na,
        out_specs=[SEM, SEM] + [HBM] * (2 * n) + [pl.BlockSpec(memory_space=pltpu.VMEM)],
        input_output_aliases={t: 2 + t for t in range(2 * n)},
        compiler_params=pltpu.CompilerParams(has_side_effects=EFFECT),
    )(*[pltpu.with_memory_space_constraint(a, pltpu.HBM) for a in operands], *after)
    return out[0], out[1], out[2:2 + n], out[2 + n:2 + 2 * n], out[2 + 2 * n]


def _scatter_wait(srcs, lands, send_sem, recv_sem, after, name):
    n = len(srcs)

    def body(*refs):
        ins, land = refs[:n], refs[n:2 * n]
        send, recv = refs[2 * n], refs[2 * n + 1]
        x, y, c = _coords()
        for t in range(n):
            for m in (1, 2, 3):
                px, py, pk = _chip_peer(x, y, m)
                cp = pltpu.make_async_remote_copy(
                    src_ref=ins[t].at[pk], dst_ref=land[t].at[pk], send_sem=send.at[3 * t + m - 1],
                    recv_sem=recv.at[3 * t + m - 1], device_id=(px, py, c), device_id_type=MESH)
                cp.wait_send()
                cp.wait_recv()

    hbm = [pltpu.HBM(a.shape, a.dtype) for a in srcs]
    out = pl.pallas_call(
        body, name=name, out_shape=hbm + hbm, in_specs=[HBM] * (2 * n) + [SEM, SEM, ANY], out_specs=[HBM] * (2 * n),
        input_output_aliases={t: t for t in range(2 * n)},
        compiler_params=pltpu.CompilerParams(has_side_effects=EFFECT),
    )(*srcs, *lands, send_sem, recv_sem, after)
    return out[:n], out[n:]


def _sum_own_half(g, ra, kc, name):
    _, R, C = g.shape
    half = R // 2
    br = _row_block(half, C)
    nb = half // br

    def body(kc_ref, g_ref, ra_ref, o_ref):
        o_ref[...] = (g_ref[...] + ra_ref[...].astype(F32)).astype(BF16)

    return pl.pallas_call(
        body, name=name,
        grid_spec=pltpu.PrefetchScalarGridSpec(
            num_scalar_prefetch=1, grid=(NCHIP, nb),
            in_specs=[pl.BlockSpec((None, br, C), lambda j, i, kc: (j, kc[1] * nb + i, 0)),
                      pl.BlockSpec((None, br, C), lambda j, i, kc: (j, i, 0))],
            out_specs=pl.BlockSpec((None, br, C), lambda j, i, kc: (j, i, 0))),
        out_shape=jax.ShapeDtypeStruct((NCHIP, half, C), BF16),
        compiler_params=_params("parallel", "parallel"),
    )(kc, g, ra)


def _sum_chips(sa, rb, kc, name, after=None):
    _, half, C = rb.shape
    br = _row_block(half, C)
    nb = half // br
    extra = [] if after is None else [after]

    def body(kc_ref, own_ref, r1_ref, r2_ref, r3_ref, *rest):
        out, obuf, local_sems, send_sems, recv_sem = rest[-5:]
        i = pl.program_id(0)
        slot = i % 2
        x, y, c = _coords()

        def copies(i_, slot_):
            rows = out.at[pl.ds(pl.multiple_of((c * nb + i_) * br, 8), br), :]
            return (pltpu.make_async_copy(obuf.at[slot_], rows, local_sems.at[slot_]),
                    pltpu.make_async_remote_copy(src_ref=obuf.at[slot_], dst_ref=rows, send_sem=send_sems.at[slot_],
                                                 recv_sem=recv_sem, device_id=(x, y, 1 - c), device_id_type=MESH))

        @pl.when(i >= 2)
        def _():
            here, there = copies(i, slot)
            here.wait()
            there.wait_send()

        acc = own_ref[...].astype(F32) + r1_ref[...].astype(F32)
        obuf[slot] = (acc + r2_ref[...].astype(F32)) + r3_ref[...].astype(F32)
        here, there = copies(i, slot)
        here.start()
        there.start()

        @pl.when(i == nb - 1)
        def _():
            for s in range(min(2, nb)):
                here, there = copies(i, (i - s) % 2)
                here.wait()
                there.wait_send()
            theirs = out.at[pl.ds(pl.multiple_of((1 - c) * half, 8), half), :]
            pltpu.make_async_remote_copy(src_ref=theirs, dst_ref=theirs, send_sem=send_sems.at[0], recv_sem=recv_sem,
                                         device_id=(x, y, 1 - c), device_id_type=MESH).wait_recv()

    def slab(m):
        return pl.BlockSpec((None, br, C), lambda i, kc: (kc[0] ^ m, i, 0))

    return pl.pallas_call(
        body, name=name,
        grid_spec=pltpu.PrefetchScalarGridSpec(
            num_scalar_prefetch=1, grid=(nb,),
            in_specs=[slab(0), slab(1), slab(2), slab(3)] + [ANY] * len(extra),
            out_specs=ANY,
            scratch_shapes=[pltpu.VMEM((2, br, C), F32), pltpu.SemaphoreType.DMA((2,)), pltpu.SemaphoreType.DMA((2,)),
                            pltpu.SemaphoreType.DMA]),
        out_shape=jax.ShapeDtypeStruct((2 * half, C), F32), compiler_params=_params("arbitrary"),
    )(kc, sa, rb, rb, rb, *extra)


def _sum8(ga, name):
    _, R, C = ga.shape

    def body(g_ref, o_ref):
        acc = g_ref[0]
        for j in range(1, 8):
            acc = acc + g_ref[j]
        o_ref[...] = acc

    return pl.pallas_call(
        body, name=name, in_specs=[pl.BlockSpec((8, R, C), lambda: (0, 0, 0))],
        out_specs=pl.BlockSpec((R, C), lambda: (0, 0)), out_shape=jax.ShapeDtypeStruct((R, C), F32),
    )(ga)


ADA_COLS = 9 * D // NCHIP
ADA_BLK = 256


def _ada_mod(c_all, ada_w, ada_b, kidx):
    def body(k_ref, c_ref, w_ref, b_ref, o_ref):
        cv = c_ref[...]
        cs = cv * _sigmoid(cv)
        o_ref[...] = jnp.dot(cs, w_ref[...], precision=lax.Precision.HIGHEST,
                             preferred_element_type=F32) + b_ref[...]

    nblk = ADA_COLS // ADA_BLK
    return pl.pallas_call(
        body, name="ada_mod",
        grid_spec=pltpu.PrefetchScalarGridSpec(
            num_scalar_prefetch=1, grid=(nblk,),
            in_specs=[pl.BlockSpec((8, D), lambda j, k: (0, 0)),
                      pl.BlockSpec((D, ADA_BLK), lambda j, k: (0, j)),
                      pl.BlockSpec((1, ADA_BLK), lambda j, k: (0, k[0] * nblk + j))],
            out_specs=pl.BlockSpec((8, ADA_BLK), lambda j, k: (0, j))),
        out_shape=jax.ShapeDtypeStruct((8, ADA_COLS), F32),
        compiler_params=_params("parallel"),
    )(kidx, c_all, ada_w, ada_b)


def _ada_grad(c_all_t, dmod_all, kidx):
    def body(k_ref, ct_ref, dm_ref, o_ref):
        cv = ct_ref[...]
        cs = cv * _sigmoid(cv)
        acc = cs[:, 0:1] * dm_ref[0:1, :]
        for b in range(1, 8):
            acc = acc + cs[:, b:b + 1] * dm_ref[b:b + 1, :]
        o_ref[...] = acc

    nblk = ADA_COLS // ADA_BLK
    return pl.pallas_call(
        body, name="ada_grad",
        grid_spec=pltpu.PrefetchScalarGridSpec(
            num_scalar_prefetch=1, grid=(nblk,),
            in_specs=[pl.BlockSpec((D, 8), lambda j, k: (0, 0)),
                      pl.BlockSpec((8, ADA_BLK), lambda j, k: (0, k[0] * nblk + j))],
            out_specs=pl.BlockSpec((D, ADA_BLK), lambda j, k: (0, j))),
        out_shape=jax.ShapeDtypeStruct((D, ADA_COLS), F32),
        compiler_params=_params("parallel"),
    )(kidx, c_all_t, dmod_all)


BIG = ("ffn1_w_in", "ffn1_w_out", "mix_w_in", "hgrn_w_o", "conv_w_o", "mix_w_out", "ffn2_w_in", "ffn2_w_out")
ROW_SHARDED = ("ffn1_w_out", "hgrn_w_o", "conv_w_o", "mix_w_out", "ffn2_w_out")
GATHER_GROUPS = ((0, 1), (2,), (3, 4, 5), (6, 7))
GATHER_HALVED = (True, True, False, False)
GATHER_STARTS = ((0, 1), (2, 3))
PACK_LEN = {"ada_b": 9, "hgrn_lb": 2}
WEIGHTS = ("ada_w", "ada_b", "norm_ffn1", "ffn1_w_in", "ffn1_w_out", "norm_mix", "mix_w_in", "hgrn_lb", "hgrn_g",
           "hgrn_w_o", "conv_w", "conv_b", "conv_ln_g", "conv_ln_b", "conv_w_o", "mix_w_out", "norm_ffn2",
           "ffn2_w_in", "ffn2_w_out", "norm_final")
PACKED = ("ada_b", "norm_ffn1", "norm_mix", "hgrn_g", "conv_b", "conv_ln_g", "conv_ln_b", "norm_ffn2",
          "norm_final", "hgrn_lb")


def _pack_params(p, name):
    parts = [(p[n].reshape(PACK_LEN.get(n, 1), D), 0, PACK_LEN.get(n, 1), PACK_AT[n]) for n in PACKED]
    return _pack_rows(parts, PACK_ROWS, name)


def _step(w, m, v, x, c, tgt):
    xi, yi, ci = _coords()
    kidx = (2 * xi + yi).astype(jnp.int32).reshape(1)
    kc = jnp.stack([2 * xi + yi, ci]).astype(jnp.int32)
    me = 4 * xi + 2 * yi + ci

    first = jnp.zeros((40, D), F32).at[0:CONV_K, 0:D // NCHIP].set(w["conv_w"][0]).at[32:33].set(c)
    first_all = _allgather8(first, "gather_c_conv_w")
    c_all = first_all[:, 32, :]
    mod_cols = _ada_mod(c_all, w["ada_w"][0], w["ada_b"], kidx)
    mod_all = _allgather8(mod_cols, "gather_mod")
    mod = lax.dynamic_slice(mod_all, (0, me, 0), (8, 1, ADA_COLS))[::2].reshape(9, D)
    small = {n: w[n].reshape(-1, D) for n in ("norm_ffn1", "norm_mix", "hgrn_lb", "hgrn_g", "conv_b", "conv_ln_g",
                                              "conv_ln_b", "norm_ffn2", "norm_final")}
    small["conv_w"] = jnp.concatenate([first_all[2 * j, 0:32, 0:D // NCHIP] for j in range(NCHIP)], axis=1)

    lands, sends, recvs = [], [], []
    after = mod
    for part in GATHER_STARTS:
        tensors = [t for gi in part for t in GATHER_GROUPS[gi]]
        cast = [_cast_into_slot(w[BIG[t]][0], kc, "cast_" + BIG[t], after) for t in tensors]
        groups = [tuple(tensors.index(t) for t in GATHER_GROUPS[gi]) for gi in part]
        s, r, thru, after = _gather_start(cast, groups, [GATHER_HALVED[gi] for gi in part], [after],
                                          "gather_weights_start%d" % part[0])
        lands, sends, recvs = lands + list(thru), sends + list(s), recvs + list(r)
    started_all = after
    ready = {}

    def weight(name, after):
        t = BIG.index(name)
        if t not in ready:
            gi = [t in grp for grp in GATHER_GROUPS].index(True)
            grp = GATHER_GROUPS[gi]
            outs = _gather_wait([lands[j] for j in grp], GATHER_HALVED[gi], sends[gi], recvs[gi],
                                started_all if gi == 0 else after, "gather_weights_wait%d" % gi)
            if GATHER_HALVED[gi]:
                outs = _sibling_fill(outs, "gather_weights_fill%d" % gi)
            ready.update(zip(grp, outs))
        return ready[t].reshape(-1, D) if name in ROW_SHARDED else ready[t]

    grads, delta, new_m, new_v = {}, {}, {}, {}
    flight = []
    landed = []

    def settle(after):
        names, sa, rb, send, recv = flight.pop()
        sa, rb = _scatter_wait(sa, rb, send, recv, after, "rs_chip_wait_" + names[0])
        landed.append((names, sa, rb))

    def reduce(names, pairs, after=None):
        gs = [g.reshape(NCHIP, -1, g.shape[-1]) for g, _ in pairs]
        ra = [r.reshape(NCHIP, -1, r.shape[-1]) for _, r in pairs]
        sa = [_sum_own_half(g, r, kc, "rs_sum_pair_" + n) for g, r, n in zip(gs, ra, names)]
        if flight:
            settle(sa[0])
        send, recv, sa, rb, tok = _scatter_start(sa, "rs_chip_start_" + names[0], () if after is None else (after,))
        flight.append((names, sa, rb, send, recv))
        started.append(tok)
        return tok

    def adamw(n, after=None):
        shape = w[n].shape
        two = (shape[-2], shape[-1])
        out = _adamw(w[n].reshape(two), grads[n], m[n].reshape(two), v[n].reshape(two), "adamw_" + n, after,
                     copy_grad=n in BIG)
        g_ = out[3] if n in BIG else grads[n]
        grads[n], delta[n], new_m[n], new_v[n] = (a.reshape(shape) for a in (g_, out[0], out[1], out[2]))
        return out[1]

    def finish(after=None):
        names, sa, rb = landed.pop(0)
        full = [_sum_chips(s, r, kc, "rs_sum_chips_" + n, after) for s, r, n in zip(sa, rb, names)]
        grads.update(zip(names, full))
        return [adamw(n) for n in names][-1]

    started = []

    smalls = []

    def reduce_small(packed):
        packed_all = _allgather8(packed, "gather_small_grads")
        smalls.extend([packed_all, _sum8(packed_all, "sum_small_grads")])
        return smalls[1]

    dx = _local_step(x[0], tgt[0], mod, small, kc, weight, reduce, reduce_small)
    packed_all, gsum = smalls
    loss = (0.5 / D) * jnp.sum(gsum[PACK_AT["loss"]])
    dmod_all = packed_all[:, 0:9, :].reshape(8, 9 * D)
    grads["ada_w"] = _ada_grad(c_all.T, dmod_all, kidx)
    grads["conv_w"] = lax.dynamic_slice(gsum, (PACK_AT["conv_w"], kidx[0] * (D // NCHIP)), (CONV_K, D // NCHIP))

    tok = started[-1]
    adamw("ada_w", tok)
    adamw("conv_w")
    pw, pm, pv = (_pack_params(p, "pack_" + s) for p, s in ((w, "w"), (m, "m"), (v, "v")))
    pd, pnm, pnv = _adamw(pw, gsum, pm, pv, "adamw_small", tok)
    last = pnv
    while landed:
        last = finish(tok)
    settle(last)
    finish()
    for n in PACKED:
        rows = slice(PACK_AT[n], PACK_AT[n] + PACK_LEN.get(n, 1))
        for dst, src in ((grads, gsum), (delta, pd), (new_m, pnm), (new_v, pnv)):
            dst[n] = src[rows].reshape(w[n].shape)

    outs = [loss, dx[None]]
    for d in (grads, delta, new_m, new_v):
        outs += [d[n] for n in WEIGHTS]
    return tuple(outs)


def kernel(x, c, ada_w, ada_b, norm_ffn1, ffn1_w_in, ffn1_w_out, norm_mix, mix_w_in, hgrn_lb, hgrn_g, hgrn_w_o, conv_w, conv_b, conv_ln_g, conv_ln_b, conv_w_o, mix_w_out, norm_ffn2, ffn2_w_in, ffn2_w_out, norm_final, loss_target, m_ada_w, m_ada_b, m_norm_ffn1, m_ffn1_w_in, m_ffn1_w_out, m_norm_mix, m_mix_w_in, m_hgrn_lb, m_hgrn_g, m_hgrn_w_o, m_conv_w, m_conv_b, m_conv_ln_g, m_conv_ln_b, m_conv_w_o, m_mix_w_out, m_norm_ffn2, m_ffn2_w_in, m_ffn2_w_out, m_norm_final, v_ada_w, v_ada_b, v_norm_ffn1, v_ffn1_w_in, v_ffn1_w_out, v_norm_mix, v_mix_w_in, v_hgrn_lb, v_hgrn_g, v_hgrn_w_o, v_conv_w, v_conv_b, v_conv_ln_g, v_conv_ln_b, v_conv_w_o, v_mix_w_out, v_norm_ffn2, v_ffn2_w_in, v_ffn2_w_out, v_norm_final):
    w = dict(ada_w=ada_w, ada_b=ada_b, norm_ffn1=norm_ffn1, ffn1_w_in=ffn1_w_in, ffn1_w_out=ffn1_w_out,
             norm_mix=norm_mix, mix_w_in=mix_w_in, hgrn_lb=hgrn_lb, hgrn_g=hgrn_g, hgrn_w_o=hgrn_w_o, conv_w=conv_w,
             conv_b=conv_b, conv_ln_g=conv_ln_g, conv_ln_b=conv_ln_b, conv_w_o=conv_w_o, mix_w_out=mix_w_out,
             norm_ffn2=norm_ffn2, ffn2_w_in=ffn2_w_in, ffn2_w_out=ffn2_w_out, norm_final=norm_final)
    m = dict(ada_w=m_ada_w, ada_b=m_ada_b, norm_ffn1=m_norm_ffn1, ffn1_w_in=m_ffn1_w_in, ffn1_w_out=m_ffn1_w_out,
             norm_mix=m_norm_mix, mix_w_in=m_mix_w_in, hgrn_lb=m_hgrn_lb, hgrn_g=m_hgrn_g, hgrn_w_o=m_hgrn_w_o,
             conv_w=m_conv_w, conv_b=m_conv_b, conv_ln_g=m_conv_ln_g, conv_ln_b=m_conv_ln_b, conv_w_o=m_conv_w_o,
             mix_w_out=m_mix_w_out, norm_ffn2=m_norm_ffn2, ffn2_w_in=m_ffn2_w_in, ffn2_w_out=m_ffn2_w_out,
             norm_final=m_norm_final)
    v = dict(ada_w=v_ada_w, ada_b=v_ada_b, norm_ffn1=v_norm_ffn1, ffn1_w_in=v_ffn1_w_in, ffn1_w_out=v_ffn1_w_out,
             norm_mix=v_norm_mix, mix_w_in=v_mix_w_in, hgrn_lb=v_hgrn_lb, hgrn_g=v_hgrn_g, hgrn_w_o=v_hgrn_w_o,
             conv_w=v_conv_w, conv_b=v_conv_b, conv_ln_g=v_conv_ln_g, conv_ln_b=v_conv_ln_b, conv_w_o=v_conv_w_o,
             mix_w_out=v_mix_w_out, norm_ffn2=v_norm_ffn2, ffn2_w_in=v_ffn2_w_in, ffn2_w_out=v_ffn2_w_out,
             norm_final=v_norm_final)
    return _step(w, m, v, x, c, loss_target)
```

```python
import functools

import jax
import jax.numpy as jnp
from jax import lax
from jax.experimental import pallas as pl
from jax.experimental.pallas import tpu as pltpu

F32 = jnp.float32
BF16 = jnp.bfloat16

D = 1024
DFF = 2816
NCHIP = 4
FSH = 2 * DFF // NCHIP
HEADS = 8
DK = 128
CHUNK = 64
CONV_K = 31
HALO = 32
EPS = 1e-6
TB = 256
CB = 1024
DW_TOKENS = 2048
VMEM_LIMIT = 56 * 1024 * 1024

ADAM_LR = 0.001
ADAM_B1 = 0.9
ADAM_B2 = 0.999
ADAM_EPS = 1e-08
ADAM_WD = 0.01
ADAM_STEP = 10

MESH = pl.DeviceIdType.MESH
ANY = pl.BlockSpec(memory_space=pl.ANY)


def _params(*sem):
    return pltpu.CompilerParams(dimension_semantics=sem, vmem_limit_bytes=VMEM_LIMIT)


def _sigmoid(x):
    return 0.5 * jnp.tanh(0.5 * x) + 0.5


def _dsilu(x, sg):
    return sg * (1.0 + x * (1.0 - sg))


def _nt(a, b):
    return lax.dot_general(a, b, (((1,), (1,)), ((), ())), preferred_element_type=F32)


def _tn(a, b):
    return lax.dot_general(a, b, (((0,), (0,)), ((), ())), preferred_element_type=F32)


def _nn(a, b):
    return jnp.dot(a, b, preferred_element_type=F32)


def _colsum(x):
    return jnp.sum(x, axis=0, keepdims=True)


def _rms_fwd(x, gn, sc, sh):
    r = lax.rsqrt(jnp.mean(x * x, axis=-1, keepdims=True) + EPS)
    n = x * r
    h = (n * gn) * (1.0 + sc) + sh
    return r, n, h


def _rms_bwd(dh, r, n, gn, sc, acc_ref):
    acc_ref[0:1, :] += _colsum(dh)
    acc_ref[1:2, :] += _colsum(dh * (n * gn))
    dng = dh * (1.0 + sc)
    acc_ref[3:4, :] += _colsum(dng * n)
    dn = dng * gn
    return r * (dn - n * jnp.mean(dn * n, axis=-1, keepdims=True))


def _loss_head(x, tgt, gf, acc_ref):
    r = lax.rsqrt(jnp.mean(x * x, axis=-1, keepdims=True) + EPS)
    n = x * r
    err = n * gf - tgt
    acc_ref[1:2, :] += _colsum(err * err)
    dy = err * (1.0 / D)
    acc_ref[0:1, :] += _colsum(dy * n)
    dn = dy * gf
    return r * (dn - n * jnp.mean(dn * n, axis=-1, keepdims=True))


def _ffn_fwd(x, vec, w_in, w_out, name, head=None):
    T = x.shape[0]
    nh = 0 if head is None else 2

    def body(x_ref, vec_ref, *rest):
        win_hbm, wout_hbm = rest[nh:nh + 2]
        xo_ref, h_ref, a_ref, b_ref, s_ref, f_ref = rest[nh + 2:nh + 8]
        win, wout = rest[-2:]

        @pl.when(pl.program_id(0) == 0)
        def _():
            pltpu.sync_copy(win_hbm, win)
            pltpu.sync_copy(wout_hbm, wout)
            if head is not None:
                rest[nh + 8][...] = jnp.zeros((8, D), F32)

        x = x_ref[...]
        sh, sc, gate, gn = vec_ref[0:1, :], vec_ref[1:2, :], vec_ref[2:3, :], vec_ref[3:4, :]
        _, _, h = _rms_fwd(x, gn, sc, sh)
        hb = h.astype(BF16)
        h_ref[...] = hb
        f = jnp.zeros((TB, D), F32)
        for j in range(2):
            cols = slice(j * FSH, (j + 1) * FSH)
            a = _nn(hb, win[j])
            b = _nn(hb, win[2 + j])
            s = (a * _sigmoid(a) * b).astype(BF16)
            a_ref[:, cols] = a.astype(BF16)
            b_ref[:, cols] = b.astype(BF16)
            s_ref[:, cols] = s
            f = f + _nn(s, wout[cols, :])
        xo = x + (0.5 * gate) * f
        f_ref[...] = f.astype(BF16)
        if head is None:
            xo_ref[...] = xo
        else:
            xo_ref[...] = _loss_head(xo, rest[0][...], rest[1][0:1, :], rest[nh + 8])

    row = lambda w: pl.BlockSpec((TB, w), lambda i: (i, 0))
    vec8 = pl.BlockSpec((8, D), lambda i: (0, 0))
    acc = [] if head is None else [jax.ShapeDtypeStruct((8, D), F32)]
    return pl.pallas_call(
        body, name=name, grid=(T // TB,),
        in_specs=[row(D), vec8] + ([] if head is None else [row(D), vec8]) + [ANY, ANY],
        out_specs=[row(D), row(D), row(DFF), row(DFF), row(DFF), row(D)] + [vec8] * len(acc),
        out_shape=[jax.ShapeDtypeStruct((T, D), F32), jax.ShapeDtypeStruct((T, D), BF16),
                   jax.ShapeDtypeStruct((T, DFF), BF16), jax.ShapeDtypeStruct((T, DFF), BF16),
                   jax.ShapeDtypeStruct((T, DFF), BF16), jax.ShapeDtypeStruct((T, D), BF16)] + acc,
        scratch_shapes=[pltpu.VMEM((NCHIP, D, FSH), BF16), pltpu.VMEM((DFF, D), BF16)],
        compiler_params=_params("arbitrary"),
    )(x, vec, *([] if head is None else list(head)), w_in, w_out)


def _ffn_bwd(dxo, x, vec, a, b, f, w_in, w_out, name):
    T = x.shape[0]

    def body(dxo_ref, x_ref, vec_ref, a_ref, b_ref, f_ref, win_hbm, wout_hbm,
             dx_ref, df_ref, dab_ref, acc_ref, win, wout):
        @pl.when(pl.program_id(0) == 0)
        def _():
            pltpu.sync_copy(win_hbm, win)
            pltpu.sync_copy(wout_hbm, wout)
            acc_ref[...] = jnp.zeros_like(acc_ref)

        dxo = dxo_ref[...]
        x = x_ref[...]
        sh, sc, gate, gn = vec_ref[0:1, :], vec_ref[1:2, :], vec_ref[2:3, :], vec_ref[3:4, :]
        r, n, _ = _rms_fwd(x, gn, sc, sh)
        acc_ref[2:3, :] += _colsum(0.5 * f_ref[...].astype(F32) * dxo)
        dfb = ((0.5 * gate) * dxo).astype(BF16)
        df_ref[...] = dfb
        dh = jnp.zeros((TB, D), F32)
        for j in range(2):
            cols = slice(j * FSH, (j + 1) * FSH)
            ds = _nt(dfb, wout[cols, :])
            av = a_ref[:, cols].astype(F32)
            bv = b_ref[:, cols].astype(F32)
            sg = _sigmoid(av)
            da = (ds * bv * _dsilu(av, sg)).astype(BF16)
            db = (ds * (av * sg)).astype(BF16)
            dab_ref[j] = da
            dab_ref[2 + j] = db
            dh = dh + _nt(da, win[j]) + _nt(db, win[2 + j])
        dx_ref[...] = dxo + _rms_bwd(dh, r, n, gn, sc, acc_ref)

    row = lambda w: pl.BlockSpec((TB, w), lambda i: (i, 0))
    vec8 = pl.BlockSpec((8, D), lambda i: (0, 0))
    return pl.pallas_call(
        body, name=name, grid=(T // TB,),
        in_specs=[row(D), row(D), vec8, row(DFF), row(DFF), row(D), ANY, ANY],
        out_specs=[row(D), pl.BlockSpec((None, TB, D), lambda i: (0, i, 0)),
                   pl.BlockSpec((NCHIP, TB, FSH), lambda i: (0, i, 0)), vec8],
        out_shape=[jax.ShapeDtypeStruct((T, D), F32), jax.ShapeDtypeStruct((1, T, D), BF16),
                   jax.ShapeDtypeStruct((NCHIP, T, FSH), BF16), jax.ShapeDtypeStruct((8, D), F32)],
        scratch_shapes=[pltpu.VMEM((NCHIP, D, FSH), BF16), pltpu.VMEM((DFF, D), BF16)],
        compiler_params=_params("arbitrary"),
    )(dxo, x, vec, a, b, f, w_in, w_out)


def _mm_tn(a, b3, hp, kc, shard_rows, name, into=None, slab=0, slabs=None):
    T, M = a.shape
    P, _, N = b3.shape
    tm = M if M <= 1408 else M // 2
    tk = min(T, DW_TOKENS)
    nk = T // tk
    ni = M // tm
    slabs = P // hp if slabs is None else slabs
    half = shard_rows // 2
    extra = [] if into is None else list(into)

    def body(kc_ref, a_ref, b_ref, *rest):
        o_ref, ra_ref, hbuf, send_sems, recv_sem = rest[-5:]
        p, i, k = pl.program_id(0), pl.program_id(1), pl.program_id(2)
        x, y, c = _coords()
        step = p * ni + i
        slot = step % 2

        def send(p_, i_, slot_):
            dst = ra_ref.at[slab + p_ // hp, pl.ds(pl.multiple_of(i_ * (tm // 2), 8), tm // 2),
                            pl.ds(pl.multiple_of((p_ % hp) * N, LANES), N)]
            return pltpu.make_async_remote_copy(
                src_ref=hbuf.at[slot_], dst_ref=dst, send_sem=send_sems.at[slot_], recv_sem=recv_sem,
                device_id=(x, y, 1 - c), device_id_type=MESH)

        @pl.when(k == 0)
        def _():
            o_ref[...] = jnp.zeros_like(o_ref)

        o_ref[...] += _tn(a_ref[...], b_ref[...])

        @pl.when(k == nk - 1)
        def _():
            @pl.when(step >= 2)
            def _():
                send(p, i, slot).wait_send()

            for j in range(tm // shard_rows):
                start = pl.multiple_of(j * shard_rows + (1 - kc_ref[1]) * half, 8)
                hbuf[slot, j * half:(j + 1) * half, :] = o_ref[pl.ds(start, half), :].astype(BF16)
            send(p, i, slot).start()

        @pl.when((step == P * ni - 1) & (k == nk - 1))
        def _():
            for s in range(min(2, P * ni)):
                send(p, i, (step - s) % 2).wait_send()
            mine = ra_ref.at[slab:slab + P // hp]
            pltpu.make_async_remote_copy(src_ref=mine, dst_ref=mine, send_sem=send_sems.at[0], recv_sem=recv_sem,
                                         device_id=(x, y, 1 - c), device_id_type=MESH).wait_recv()

    return pl.pallas_call(
        body, name=name,
        grid_spec=pltpu.PrefetchScalarGridSpec(
            num_scalar_prefetch=1, grid=(P, ni, nk),
            in_specs=[pl.BlockSpec((tk, tm), lambda p, i, k, kc: (k, i)),
                      pl.BlockSpec((None, tk, N), lambda p, i, k, kc: (p, k, 0))] + [ANY] * len(extra),
            out_specs=[pl.BlockSpec((None, tm, N), lambda p, i, k, kc: (slab + p // hp, i, p % hp)), ANY],
            scratch_shapes=[pltpu.VMEM((2, tm // 2, N), BF16), pltpu.SemaphoreType.DMA((2,)),
                            pltpu.SemaphoreType.DMA]),
        out_shape=[jax.ShapeDtypeStruct((slabs, M, hp * N), F32), jax.ShapeDtypeStruct((slabs, M // 2, hp * N), BF16)],
        input_output_aliases={} if into is None else {3: 0, 4: 1},
        compiler_params=_params("arbitrary", "arbitrary", "arbitrary"),
    )(kc, a, b3, *extra)


def _mix_proj_fwd(x, vec, w_in):
    T = x.shape[0]

    def body(x_ref, vec_ref, w_hbm, h_ref, qr_ref, g_ref, k_ref, v_ref, og_ref, u_ref, ua_ref, ub_ref,
             sa_ref, sb_ref, w):
        @pl.when(pl.program_id(0) == 0)
        def _():
            pltpu.sync_copy(w_hbm, w)

        x = x_ref[...]
        sh, sc, gn, lb = vec_ref[0:1, :], vec_ref[1:2, :], vec_ref[3:4, :], vec_ref[4:5, :]
        _, _, h = _rms_fwd(x, gn, sc, sh)
        hb = h.astype(BF16)
        h_ref[...] = hb
        p = _nn(hb, w[0])
        qr_ref[...] = p[:, :D].astype(BF16)
        fg = lb + (1.0 - lb) * _sigmoid(p[:, D:])
        g_ref[...] = jnp.log(fg)
        k_ref[...] = (1.0 - fg).astype(BF16)
        p = _nn(hb, w[1])
        v_ref[...] = p[:, :D].astype(BF16)
        og_ref[...] = p[:, D:].astype(BF16)
        p = _nn(hb, w[2])
        ua, ub = p[:, :D], p[:, D:]
        u_ref[...] = ua * _sigmoid(ub)
        ua_ref[...] = ua.astype(BF16)
        ub_ref[...] = ub.astype(BF16)
        p = _nn(hb, w[3])
        sa_ref[...] = _sigmoid(p[:, :D]).astype(BF16)
        sb_ref[...] = _sigmoid(p[:, D:]).astype(BF16)

    row = pl.BlockSpec((TB, D), lambda i: (i, 0))
    bf = jax.ShapeDtypeStruct((T, D), BF16)
    f32 = jax.ShapeDtypeStruct((T, D), F32)
    return pl.pallas_call(
        body, name="mix_proj_fwd", grid=(T // TB,),
        in_specs=[row, pl.BlockSpec((8, D), lambda i: (0, 0)), ANY],
        out_specs=[row] * 11,
        out_shape=[bf, bf, f32, bf, bf, bf, f32, bf, bf, bf, bf],
        scratch_shapes=[pltpu.VMEM((NCHIP, D, 2 * D), BF16)],
        compiler_params=_params("arbitrary"),
    )(x, vec, w_in)


def _mix_proj_bwd(dxo, x, vec, dpa, dpb, dpc, w_in):
    T = x.shape[0]

    def body(dxo_ref, x_ref, vec_ref, dpa_ref, dpb_ref, dpc_ref, w_hbm, dx_ref, acc_ref, w):
        @pl.when(pl.program_id(0) == 0)
        def _():
            pltpu.sync_copy(w_hbm, w)
            acc_ref[...] = jnp.zeros_like(acc_ref)

        x = x_ref[...]
        sh, sc, gn = vec_ref[0:1, :], vec_ref[1:2, :], vec_ref[3:4, :]
        r, n, _ = _rms_fwd(x, gn, sc, sh)
        dh = jnp.zeros((TB, D), F32)
        for p in range(8):
            src = dpa_ref[p] if p < 4 else (dpb_ref[p - 4] if p < 6 else dpc_ref[p - 6])
            dh = dh + _nt(src, w[p // 2, :, (p % 2) * D:(p % 2 + 1) * D])
        dx_ref[...] = dxo_ref[...] + _rms_bwd(dh, r, n, gn, sc, acc_ref)

    row = pl.BlockSpec((TB, D), lambda i: (i, 0))
    vec8 = pl.BlockSpec((8, D), lambda i: (0, 0))
    stack = lambda k: pl.BlockSpec((k, TB, D), lambda i: (0, i, 0))
    return pl.pallas_call(
        body, name="mix_proj_bwd", grid=(T // TB,),
        in_specs=[row, row, vec8, stack(4), stack(2), stack(2), ANY],
        out_specs=[row, vec8],
        out_shape=[jax.ShapeDtypeStruct((T, D), F32), jax.ShapeDtypeStruct((8, D), F32)],
        scratch_shapes=[pltpu.VMEM((NCHIP, D, 2 * D), BF16)],
        compiler_params=_params("arbitrary"),
    )(dxo, x, vec, dpa, dpb, dpc, w_in)


def _tri(lower):
    r = lax.broadcasted_iota(jnp.int32, (CHUNK, CHUNK), 0)
    c = lax.broadcasted_iota(jnp.int32, (CHUNK, CHUNK), 1)
    return (c <= r) if lower else (c >= r)


def _cumsum_rows(mask, g):
    hi = g.astype(BF16)
    rest = g - hi.astype(F32)
    mid = rest.astype(BF16)
    low = (rest - mid.astype(F32)).astype(BF16)
    n = g.shape[1]
    p = _nn(mask.astype(BF16), jnp.concatenate([hi, mid, low], axis=1))
    return (p[:, 2 * n:] + p[:, n:2 * n]) + p[:, :n]


def _chunk_decay(low, g, nck):
    bs, mids, lasts = [], [], []
    for c in range(nck):
        gc = g[c * CHUNK:(c + 1) * CHUNK]
        bs.append(_cumsum_rows(low, gc))
        mids.append(_colsum(gc[0:CHUNK // 2]))
        lasts.append(_colsum(gc))
    spread = lambda rows: jnp.concatenate([jnp.broadcast_to(r, (CHUNK, DK)) for r in rows], axis=0)
    return jnp.concatenate(bs, axis=0), spread(mids), spread(lasts), lasts


def _hgrn_fwd(qr, g, k, v, og, vec):
    T = qr.shape[0]
    nck = CB // CHUNK

    def body(qr_ref, g_ref, k_ref, v_ref, og_ref, vec_ref, out_ref, o_ref, st_ref, state):
        @pl.when(pl.program_id(1) == 0)
        def _():
            state[...] = jnp.zeros_like(state)

        low = _tri(True)
        qv = qr_ref[...].astype(F32)
        q = qv * _sigmoid(qv) * (DK ** -0.5)
        kk = k_ref[...].astype(F32)
        vb = v_ref[...]
        b, mid, last, lasts = _chunk_decay(low, g_ref[...], nck)
        qt = (q * jnp.exp(b - mid)).astype(BF16)
        kt = (kk * jnp.exp(mid - b)).astype(BF16)
        qe = (q * jnp.exp(b)).astype(BF16)
        kd = (kk * jnp.exp(last - b)).astype(BF16)
        intra, grow = [], []
        for c in range(nck):
            r = slice(c * CHUNK, (c + 1) * CHUNK)
            att = jnp.where(low, _nt(qt[r], kt[r]), 0.0).astype(BF16)
            intra.append(_nn(att, vb[r]))
            grow.append(_tn(vb[r], kd[r]))
        st = state[...]
        inter = []
        for c in range(nck):
            stb = st.astype(BF16)
            st_ref[c] = stb
            inter.append(_nt(qe[c * CHUNK:(c + 1) * CHUNK], stb))
            st = st * jnp.exp(lasts[c]) + grow[c]
        state[...] = st
        o = jnp.concatenate(intra, axis=0) + jnp.concatenate(inter, axis=0)
        o_ref[...] = o
        ogv = og_ref[...].astype(F32)
        rms = lax.rsqrt(jnp.mean(o * o, axis=-1, keepdims=True) + EPS)
        out_ref[...] = (o * rms * vec_ref[5:6, :] * (ogv * _sigmoid(ogv))).astype(BF16)

    blk = pl.BlockSpec((CB, DK), lambda h, i: (i, h))
    return pl.pallas_call(
        body, name="hgrn_fwd", grid=(HEADS, T // CB),
        in_specs=[blk, blk, blk, blk, blk, pl.BlockSpec((8, DK), lambda h, i: (0, h))],
        out_specs=[blk, blk, pl.BlockSpec((None, nck, DK, DK), lambda h, i: (h, i, 0, 0))],
        out_shape=[jax.ShapeDtypeStruct((T, D), BF16), jax.ShapeDtypeStruct((T, D), F32),
                   jax.ShapeDtypeStruct((HEADS, T // CHUNK, DK, DK), BF16)],
        scratch_shapes=[pltpu.VMEM((DK, DK), F32)],
        compiler_params=_params("parallel", "arbitrary"),
    )(qr, g, k, v, og, vec)


def _hgrn_bwd(dout, og, qr, g, k, v, o, st, vec):
    T = qr.shape[0]
    nck = CB // CHUNK
    nb = T // CB

    def body(dout_ref, og_ref, qr_ref, g_ref, k_ref, v_ref, o_ref, st_ref, vec_ref,
             dp_ref, acc_ref, dstate):
        @pl.when(pl.program_id(1) == 0)
        def _():
            dstate[...] = jnp.zeros_like(dstate)
            acc_ref[...] = jnp.zeros_like(acc_ref)

        o = o_ref[...]
        ogv = og_ref[...].astype(F32)
        dout = dout_ref[...].astype(F32)
        hg = vec_ref[5:6, :]
        sgo = _sigmoid(ogv)
        rms = lax.rsqrt(jnp.mean(o * o, axis=-1, keepdims=True) + EPS)
        ohat = o * rms
        dp_ref[3] = (dout * (ohat * hg) * _dsilu(ogv, sgo)).astype(BF16)
        don = dout * (ogv * sgo)
        acc_ref[0:1, :] += _colsum(don * ohat)
        dohat = don * hg
        dob = (rms * (dohat - ohat * jnp.mean(dohat * ohat, axis=-1, keepdims=True))).astype(BF16)

        low = _tri(True)
        upp = _tri(False)
        lb = vec_ref[4:5, :]
        qv = qr_ref[...].astype(F32)
        sgq = _sigmoid(qv)
        q = qv * sgq * (DK ** -0.5)
        kk = k_ref[...].astype(F32)
        vb = v_ref[...]
        gv = g_ref[...]
        b, mid, last, lasts = _chunk_decay(low, gv, nck)
        eq = jnp.exp(b - mid)
        ek = jnp.exp(mid - b)
        eb = jnp.exp(b)
        ed = jnp.exp(last - b)
        qtb, ktb, qeb, kdb = ((t).astype(BF16) for t in (q * eq, kk * ek, q * eb, kk * ed))
        rows = [slice(c * CHUNK, (c + 1) * CHUNK) for c in range(nck)]

        dv1, dqt, dkt, dqe, grow = [], [], [], [], []
        for c, r in enumerate(rows):
            att = jnp.where(low, _nt(qtb[r], ktb[r]), 0.0).astype(BF16)
            datt = jnp.where(low, _nt(dob[r], vb[r]), 0.0).astype(BF16)
            dv1.append(_tn(att, dob[r]))
            dqt.append(_nn(datt, ktb[r]))
            dkt.append(_tn(datt, qtb[r]))
            dqe.append(_nn(dob[r], st_ref[c]))
            grow.append(_tn(dob[r], qeb[r]))
        ds = dstate[...]
        ds1b, dl_state = [None] * nck, [None] * nck
        for c in reversed(range(nck)):
            el = jnp.exp(lasts[c])
            ds1b[c] = ds.astype(BF16)
            dl_state[c] = el * _colsum(ds * st_ref[c].astype(F32))
            ds = ds * el + grow[c]
        dstate[...] = ds
        dkd = jnp.concatenate([_nn(vb[r], ds1b[c]) for c, r in enumerate(rows)], axis=0)
        dv = jnp.concatenate(dv1, axis=0) + jnp.concatenate([_nt(kdb[r], ds1b[c]) for c, r in enumerate(rows)], axis=0)
        dqt, dkt, dqe = (jnp.concatenate(t, axis=0) for t in (dqt, dkt, dqe))
        dq = dqt * eq + dqe * eb
        dk = dkt * ek + dkd * ed
        dkdkd = dkd * kdb.astype(F32)
        db = dqt * qtb.astype(F32) - dkt * ktb.astype(F32) + dqe * qeb.astype(F32) - dkdkd
        dg = jnp.concatenate([_cumsum_rows(upp, db[r]) + (_colsum(dkdkd[r]) + dl_state[c])
                              for c, r in enumerate(rows)], axis=0)
        fg = jnp.exp(gv)
        dfg = dg * jnp.exp(-gv) - dk
        one_m_sig = (1.0 - fg) * (1.0 / (1.0 - lb))
        dp_ref[0] = (dq * (DK ** -0.5) * _dsilu(qv, sgq)).astype(BF16)
        dp_ref[1] = (dfg * (fg - lb) * one_m_sig).astype(BF16)
        dp_ref[2] = dv.astype(BF16)
        dlb = _colsum(dfg * one_m_sig) * (lb * (1.0 - lb))
        acc_ref[1:2, :] += dlb
        acc_ref[2:3, :] -= dlb

    blk = pl.BlockSpec((CB, DK), lambda h, i: (nb - 1 - i, h))
    return pl.pallas_call(
        body, name="hgrn_bwd", grid=(HEADS, nb),
        in_specs=[blk, blk, blk, blk, blk, blk, blk,
                  pl.BlockSpec((None, nck, DK, DK), lambda h, i: (h, nb - 1 - i, 0, 0)),
                  pl.BlockSpec((8, DK), lambda h, i: (0, h))],
        out_specs=[pl.BlockSpec((4, CB, DK), lambda h, i: (0, nb - 1 - i, h)),
                   pl.BlockSpec((8, DK), lambda h, i: (0, h))],
        out_shape=[jax.ShapeDtypeStruct((4, T, D), BF16), jax.ShapeDtypeStruct((8, D), F32)],
        scratch_shapes=[pltpu.VMEM((DK, DK), F32)],
        compiler_params=_params("parallel", "arbitrary"),
    )(dout, og, qr, g, k, v, o, st, vec)


def _ln_fwd(uc, lg, lbias):
    mu = jnp.mean(uc, axis=-1, keepdims=True)
    xc = uc - mu
    rstd = lax.rsqrt(jnp.mean(xc * xc, axis=-1, keepdims=True) + EPS)
    z = xc * rstd
    return rstd, z, z * lg + lbias


LANES = 128
SUBLANES = 8
CONV_ROWS = 64


def _lane_tiles():
    return [slice(l * LANES, (l + 1) * LANES) for l in range(D // LANES)]


def _row_shifts(x):
    n = x.shape[0]
    return [x] + [pltpu.roll(x, n - r, axis=0) for r in range(1, SUBLANES)]


TAPS_PAST = tuple(HALO - (CONV_K - 1) + j for j in range(CONV_K))
TAPS_AHEAD = tuple(CONV_K - 1 - j for j in range(CONV_K))


def _tap_windows(shifted, starts, r0, rows):
    for r in range(SUBLANES):
        taps = [(j, s // SUBLANES) for j, s in enumerate(starts) if s % SUBLANES == r]
        if not taps:
            continue
        lo = min(a for _, a in taps)
        hi = max(a for _, a in taps)
        span = shifted[r][r0 + lo * SUBLANES:r0 + hi * SUBLANES + rows]
        for j, a in taps:
            yield j, span[(a - lo) * SUBLANES:(a - lo) * SUBLANES + rows]


def _conv_fwd(u, cw, cvec):
    T = u.shape[0]
    per = TB // HALO

    def body(u_ref, halo_ref, cw_ref, cvec_ref, us_ref, uc_ref, pad):
        i = pl.program_id(0)
        pad[0:HALO, :] = jnp.where(i > 0, halo_ref[...], 0.0)
        pad[HALO:, :] = u_ref[...]
        for lanes in _lane_tiles():
            shifted = _row_shifts(pad[:, lanes])
            for r0 in range(0, TB, CONV_ROWS):
                acc = jnp.broadcast_to(cvec_ref[0:1, lanes], (CONV_ROWS, LANES))
                for j, window in _tap_windows(shifted, TAPS_PAST, r0, CONV_ROWS):
                    acc = acc + cw_ref[j:j + 1, lanes] * window
                uc_ref[r0:r0 + CONV_ROWS, lanes] = acc
        _, _, ul = _ln_fwd(uc_ref[...], cvec_ref[1:2, :], cvec_ref[2:3, :])
        us_ref[...] = (ul * _sigmoid(ul)).astype(BF16)

    row = pl.BlockSpec((TB, D), lambda i: (i, 0))
    return pl.pallas_call(
        body, name="conv_fwd", grid=(T // TB,),
        in_specs=[row, pl.BlockSpec((HALO, D), lambda i: (jnp.maximum(i * per - 1, 0), 0)),
                  pl.BlockSpec((32, D), lambda i: (0, 0)), pl.BlockSpec((8, D), lambda i: (0, 0))],
        out_specs=[row, row],
        out_shape=[jax.ShapeDtypeStruct((T, D), BF16), jax.ShapeDtypeStruct((T, D), F32)],
        scratch_shapes=[pltpu.VMEM((TB + HALO, D), F32)],
        compiler_params=_params("parallel"),
    )(u, u, cw, cvec)


def _conv_bwd_taps(duc, u, ua, ub, cw):
    T = u.shape[0]
    per = TB // HALO
    nblk = T // TB

    def body(duc_ref, dnext_ref, u_ref, uprev_ref, ua_ref, ub_ref, cw_ref, dp_ref, dcw_ref, upad, dpad, dcw):
        i = pl.program_id(0)

        @pl.when(i == 0)
        def _():
            dcw[...] = jnp.zeros_like(dcw)

        upad[0:HALO, :] = jnp.where(i > 0, uprev_ref[...], 0.0)
        upad[HALO:, :] = u_ref[...]
        dpad[0:TB, :] = duc_ref[...]
        dpad[TB:, :] = jnp.where(i < nblk - 1, dnext_ref[...], 0.0)
        for lanes in _lane_tiles():
            ushift = _row_shifts(upad[:, lanes])
            dshift = _row_shifts(dpad[:, lanes])
            for r0 in range(0, TB, CONV_ROWS):
                rows = slice(r0, r0 + CONV_ROWS)
                duc = duc_ref[rows, lanes]
                for j, window in _tap_windows(ushift, TAPS_PAST, r0, CONV_ROWS):
                    prod = duc * window
                    dcw[j, :, lanes] += jnp.sum(prod.reshape(CONV_ROWS // SUBLANES, SUBLANES, LANES), axis=0)
                du = jnp.zeros((CONV_ROWS, LANES), F32)
                for j, window in _tap_windows(dshift, TAPS_AHEAD, r0, CONV_ROWS):
                    du = du + cw_ref[j:j + 1, lanes] * window
                ua = ua_ref[rows, lanes].astype(F32)
                sg = _sigmoid(ub_ref[rows, lanes].astype(F32))
                dp_ref[0, rows, lanes] = (du * sg).astype(BF16)
                dp_ref[1, rows, lanes] = (du * ua * sg * (1.0 - sg)).astype(BF16)

        @pl.when(i == nblk - 1)
        def _():
            dcw_ref[...] = jnp.sum(dcw[...], axis=1)

    row = pl.BlockSpec((TB, D), lambda i: (i, 0))
    return pl.pallas_call(
        body, name="conv_bwd_taps", grid=(nblk,),
        in_specs=[row, pl.BlockSpec((HALO, D), lambda i: (jnp.minimum((i + 1) * per, T // HALO - 1), 0)),
                  row, pl.BlockSpec((HALO, D), lambda i: (jnp.maximum(i * per - 1, 0), 0)),
                  row, row, pl.BlockSpec((32, D), lambda i: (0, 0))],
        out_specs=[pl.BlockSpec((2, TB, D), lambda i: (0, i, 0)), pl.BlockSpec((32, D), lambda i: (0, 0))],
        out_shape=[jax.ShapeDtypeStruct((2, T, D), BF16), jax.ShapeDtypeStruct((32, D), F32)],
        scratch_shapes=[pltpu.VMEM((TB + HALO, D), F32), pltpu.VMEM((TB + HALO, D), F32),
                        pltpu.VMEM((32, SUBLANES, D), F32)],
        compiler_params=_params("arbitrary"),
    )(duc, duc, u, u, ua, ub, cw)


def _merge_fwd(x, oa, us, sa, sb, vec, w_ho, w_co, w_mo):
    T = x.shape[0]

    def body(x_ref, oa_ref, us_ref, sa_ref, sb_ref, vec_ref, who_hbm, wco_hbm, wmo_hbm,
             xo_ref, ya_ref, yb_ref, mg_ref, mo_ref, who, wco, wmo):
        @pl.when(pl.program_id(0) == 0)
        def _():
            pltpu.sync_copy(who_hbm, who)
            pltpu.sync_copy(wco_hbm, wco)
            pltpu.sync_copy(wmo_hbm, wmo)

        ya = _nn(oa_ref[...], who[...])
        yb = _nn(us_ref[...], wco[...])
        mg = (sa_ref[...].astype(F32) * ya + sb_ref[...].astype(F32) * yb).astype(BF16)
        mo = _nn(mg, wmo[...])
        xo_ref[...] = x_ref[...] + vec_ref[2:3, :] * mo
        ya_ref[...] = ya.astype(BF16)
        yb_ref[...] = yb.astype(BF16)
        mg_ref[...] = mg
        mo_ref[...] = mo.astype(BF16)

    row = pl.BlockSpec((TB, D), lambda i: (i, 0))
    bf = jax.ShapeDtypeStruct((T, D), BF16)
    wv = pltpu.VMEM((D, D), BF16)
    return pl.pallas_call(
        body, name="merge_fwd", grid=(T // TB,),
        in_specs=[row, row, row, row, row, pl.BlockSpec((8, D), lambda i: (0, 0)), ANY, ANY, ANY],
        out_specs=[row] * 5,
        out_shape=[jax.ShapeDtypeStruct((T, D), F32), bf, bf, bf, bf],
        scratch_shapes=[wv, wv, wv],
        compiler_params=_params("arbitrary"),
    )(x, oa, us, sa, sb, vec, w_ho, w_co, w_mo)


def _merge_bwd(dxo, mo, ya, yb, sa, sb, uc, vec, cvec, w_ho, w_co, w_mo):
    T = dxo.shape[0]

    def body(dxo_ref, mo_ref, ya_ref, yb_ref, sa_ref, sb_ref, uc_ref, vec_ref, cvec_ref, who_hbm, wco_hbm, wmo_hbm,
             dmo_ref, dya_ref, dyb_ref, doa_ref, duc_ref, dp_ref, acc_ref, cacc_ref, who, wco, wmo):
        @pl.when(pl.program_id(0) == 0)
        def _():
            pltpu.sync_copy(who_hbm, who)
            pltpu.sync_copy(wco_hbm, wco)
            pltpu.sync_copy(wmo_hbm, wmo)
            acc_ref[...] = jnp.zeros_like(acc_ref)
            cacc_ref[...] = jnp.zeros_like(cacc_ref)

        dxo = dxo_ref[...]
        acc_ref[2:3, :] += _colsum(mo_ref[...].astype(F32) * dxo)
        dmo = (vec_ref[2:3, :] * dxo).astype(BF16)
        dmo_ref[...] = dmo
        dmg = _nt(dmo, wmo[...])
        sa = sa_ref[...].astype(F32)
        sb = sb_ref[...].astype(F32)
        dya = (sa * dmg).astype(BF16)
        dyb = (sb * dmg).astype(BF16)
        dya_ref[...] = dya
        dyb_ref[...] = dyb
        dp_ref[0] = (dmg * ya_ref[...].astype(F32) * sa * (1.0 - sa)).astype(BF16)
        dp_ref[1] = (dmg * yb_ref[...].astype(F32) * sb * (1.0 - sb)).astype(BF16)
        doa_ref[...] = _nt(dya, who[...]).astype(BF16)
        dus = _nt(dyb, wco[...])
        lg = cvec_ref[1:2, :]
        rstd, z, ul = _ln_fwd(uc_ref[...], lg, cvec_ref[2:3, :])
        dul = dus * _dsilu(ul, _sigmoid(ul))
        cacc_ref[1:2, :] += _colsum(dul * z)
        cacc_ref[2:3, :] += _colsum(dul)
        dz = dul * lg
        duc = rstd * (dz - jnp.mean(dz, axis=-1, keepdims=True) - z * jnp.mean(dz * z, axis=-1, keepdims=True))
        cacc_ref[0:1, :] += _colsum(duc)
        duc_ref[...] = duc

    row = pl.BlockSpec((TB, D), lambda i: (i, 0))
    one = pl.BlockSpec((None, TB, D), lambda i: (0, i, 0))
    vec8 = pl.BlockSpec((8, D), lambda i: (0, 0))
    bf = jax.ShapeDtypeStruct((T, D), BF16)
    bf1 = jax.ShapeDtypeStruct((1, T, D), BF16)
    acc = jax.ShapeDtypeStruct((8, D), F32)
    wv = pltpu.VMEM((D, D), BF16)
    return pl.pallas_call(
        body, name="merge_bwd", grid=(T // TB,),
        in_specs=[row, row, row, row, row, row, row, vec8, vec8, ANY, ANY, ANY],
        out_specs=[one, one, one, row, row, pl.BlockSpec((2, TB, D), lambda i: (0, i, 0)), vec8, vec8],
        out_shape=[bf1, bf1, bf1, bf, jax.ShapeDtypeStruct((T, D), F32), jax.ShapeDtypeStruct((2, T, D), BF16), acc, acc],
        scratch_shapes=[wv, wv, wv],
        compiler_params=_params("arbitrary"),
    )(dxo, mo, ya, yb, sa, sb, uc, vec, cvec, w_ho, w_co, w_mo)


def _pack_rows(parts, total, name, slot=None):
    def body(*refs):
        out = refs[-1]
        out[...] = jnp.zeros_like(out)
        for ref, (_, src, n, dst) in zip(refs[-1 - len(parts):-1], parts):
            out[dst:dst + n, :] = ref[src:src + n, :]

    arrs = [p[0] for p in parts]
    if slot is None:
        return pl.pallas_call(
            body, name=name, in_specs=[pl.BlockSpec(a.shape, lambda: (0, 0)) for a in arrs],
            out_specs=pl.BlockSpec((total, D), lambda: (0, 0)),
            out_shape=jax.ShapeDtypeStruct((total, D), F32),
        )(*arrs)
    return pl.pallas_call(
        body, name=name,
        grid_spec=pltpu.PrefetchScalarGridSpec(
            num_scalar_prefetch=1, grid=(1,),
            in_specs=[pl.BlockSpec(a.shape, lambda i, s: (0, 0)) for a in arrs],
            out_specs=pl.BlockSpec((None, total, D), lambda i, s: (s[0], 0, 0))),
        out_shape=jax.ShapeDtypeStruct((8, total, D), F32),
    )(slot, *arrs)


PACK_ROWS = 56
PACK_AT = {"ada_b": 0, "loss": 9, "norm_ffn1": 10, "norm_mix": 11, "hgrn_g": 12, "conv_b": 13, "conv_ln_g": 14,
           "conv_ln_b": 15, "norm_ffn2": 16, "norm_final": 17, "hgrn_lb": 18, "conv_w": 20}


def _local_step(x, tgt, mod, small, kc, weight, reduce, reduce_small):
    lb = jax.nn.sigmoid(small["hgrn_lb"][0:1] - small["hgrn_lb"][1:2])
    vec1 = _pack_rows([(mod, 0, 3, 0), (small["norm_ffn1"], 0, 1, 3)], 8, "pack_vec1")
    vec2 = _pack_rows([(mod, 3, 3, 0), (small["norm_mix"], 0, 1, 3), (lb, 0, 1, 4), (small["hgrn_g"], 0, 1, 5)],
                      8, "pack_vec2")
    vec3 = _pack_rows([(mod, 6, 3, 0), (small["norm_ffn2"], 0, 1, 3)], 8, "pack_vec3")
    cvec = _pack_rows([(small["conv_b"], 0, 1, 0), (small["conv_ln_g"], 0, 1, 1), (small["conv_ln_b"], 0, 1, 2)],
                      8, "pack_cvec")
    cw = small["conv_w"]
    gvec = _pack_rows([(small["norm_final"], 0, 1, 0)], 8, "pack_gvec")

    wg = {n: weight(n, vec1) for n in ("ffn1_w_in", "ffn1_w_out")}
    x1, h1, a1, b1, s1, f1 = _ffn_fwd(x, vec1, wg["ffn1_w_in"], wg["ffn1_w_out"], "ffn1_fwd")
    wg["mix_w_in"] = weight("mix_w_in", x1)
    h2, qr, g, k, v, og, u, ua, ub, sa, sb = _mix_proj_fwd(x1, vec2, wg["mix_w_in"])
    oa, o, st = _hgrn_fwd(qr, g, k, v, og, vec2)
    us, uc = _conv_fwd(u, cw, cvec)
    wg.update({n: weight(n, us) for n in ("hgrn_w_o", "conv_w_o", "mix_w_out")})
    x2, ya, yb, mg, mo = _merge_fwd(x1, oa, us, sa, sb, vec2, wg["hgrn_w_o"], wg["conv_w_o"], wg["mix_w_out"])
    wg.update({n: weight(n, x2) for n in ("ffn2_w_in", "ffn2_w_out")})
    dx3, h3, a3, b3, s3, f3, acc_head = _ffn_fwd(x2, vec3, wg["ffn2_w_in"], wg["ffn2_w_out"], "ffn2_fwd",
                                                 head=(tgt, gvec))

    dx2, df3, dab3, acc3 = _ffn_bwd(dx3, x2, vec3, a3, b3, f3, wg["ffn2_w_in"], wg["ffn2_w_out"], "ffn2_bwd")
    tok = reduce(("ffn2_w_out", "ffn2_w_in"), [_mm_tn(s3, df3, 1, kc, DFF // NCHIP, "ffn2_dwout"),
                                               _mm_tn(h3, dab3, 1, kc, D, "ffn2_dwin")])
    vec2b = vec2 + tok[0:1, 0:1]
    dmo, dya, dyb, doa, duc, dpc, acc_m, acc_c = _merge_bwd(dx2, mo, ya, yb, sa, sb, uc, vec2b, cvec,
                                                            wg["hgrn_w_o"], wg["conv_w_o"], wg["mix_w_out"])
    tok = reduce(("mix_w_out", "hgrn_w_o", "conv_w_o"),
                 [_mm_tn(mg, dmo, 1, kc, D // NCHIP, "mix_dwout"), _mm_tn(oa, dya, 1, kc, D // NCHIP, "hgrn_dwo"),
                  _mm_tn(us, dyb, 1, kc, D // NCHIP, "conv_dwo")])
    vec2c = vec2 + tok[0:1, 0:1]
    dpb, dcw = _conv_bwd_taps(duc, u, ua, ub, cw)
    dpa, acc_h = _hgrn_bwd(doa, og, qr, g, k, v, o, st, vec2c)
    dx1, acc2 = _mix_proj_bwd(dx2, x1, vec2c, dpa, dpb, dpc, wg["mix_w_in"])
    gmix = _mm_tn(h2, dpa, 2, kc, D, "mix_dwin_a", slabs=NCHIP)
    gmix = _mm_tn(h2, dpb, 2, kc, D, "mix_dwin_b", into=gmix, slab=2, slabs=NCHIP)
    gmix = _mm_tn(h2, dpc, 2, kc, D, "mix_dwin_c", into=gmix, slab=3, slabs=NCHIP)
    tok = reduce(("mix_w_in",), [gmix])
    vec1b = vec1 + tok[0:1, 0:1]
    dx0, df1, dab1, acc1 = _ffn_bwd(dx1, x, vec1b, a1, b1, f1, wg["ffn1_w_in"], wg["ffn1_w_out"], "ffn1_bwd")

    at = PACK_AT
    finish_small = reduce_small([
        (acc1, 0, 3, at["ada_b"]), (acc2, 0, 2, at["ada_b"] + 3), (acc_m, 2, 1, at["ada_b"] + 5),
        (acc3, 0, 3, at["ada_b"] + 6), (acc_head, 1, 1, at["loss"]), (acc1, 3, 1, at["norm_ffn1"]),
        (acc2, 3, 1, at["norm_mix"]), (acc_h, 0, 1, at["hgrn_g"]), (acc_c, 0, 3, at["conv_b"]),
        (acc3, 3, 1, at["norm_ffn2"]), (acc_head, 0, 1, at["norm_final"]), (acc_h, 1, 2, at["hgrn_lb"]),
        (dcw, 0, CONV_K, at["conv_w"])])
    last = [_mm_tn(s1, df1, 1, kc, DFF // NCHIP, "ffn1_dwout"), _mm_tn(h1, dab1, 1, kc, D, "ffn1_dwin")]
    reduce(("ffn1_w_out", "ffn1_w_in"), last, finish_small(last[1][0]))
    return dx0


BLOCK_BYTES = 3 * 512 * 1024


def _row_block(rows, cols):
    for br in (512, 352, 256, 176, 128, 64, 32, 16, 8):
        if rows % br == 0 and br * cols * 4 <= BLOCK_BYTES:
            return br
    return rows


def _cast_into_slot(w, kc, name, after):
    R, C = w.shape
    br = _row_block(R, C)

    def body(kc_ref, w_ref, after_ref, o_ref):
        o_ref[...] = w_ref[...].astype(BF16)

    return pl.pallas_call(
        body, name=name,
        grid_spec=pltpu.PrefetchScalarGridSpec(
            num_scalar_prefetch=1, grid=(R // br,),
            in_specs=[pl.BlockSpec((br, C), lambda i, kc: (i, 0)), ANY],
            out_specs=pl.BlockSpec((None, br, C), lambda i, kc: (kc[0], i, 0))),
        out_shape=jax.ShapeDtypeStruct((NCHIP, R, C), BF16), compiler_params=_params("parallel"),
    )(kc, w, after)


def _adamw(w, g, m, v, name, after=None, copy_grad=False):
    R, C = w.shape
    br = _row_block(R, C)
    extra = [] if after is None else [after]
    nout = 4 if copy_grad else 3

    def body(w_ref, g_ref, m_ref, v_ref, *rest):
        d_ref, nm_ref, nv_ref = rest[-nout:][:3]
        gv = g_ref[...]
        if copy_grad:
            rest[-1][...] = gv
        nm = ADAM_B1 * m_ref[...] + (1.0 - ADAM_B1) * gv
        nv = ADAM_B2 * v_ref[...] + (1.0 - ADAM_B2) * (gv * gv)
        m_hat = nm / (1.0 - ADAM_B1 ** ADAM_STEP)
        v_hat = nv / (1.0 - ADAM_B2 ** ADAM_STEP)
        d_ref[...] = -ADAM_LR * (m_hat / (jnp.sqrt(v_hat) + ADAM_EPS) + ADAM_WD * w_ref[...])
        nm_ref[...] = nm
        nv_ref[...] = nv

    blk = pl.BlockSpec((br, C), lambda i: (i, 0))
    out = jax.ShapeDtypeStruct((R, C), F32)
    return pl.pallas_call(
        body, name=name, grid=(R // br,), in_specs=[blk] * 4 + [ANY] * len(extra), out_specs=[blk] * nout,
        out_shape=[out] * nout, compiler_params=_params("parallel"),
    )(w, g, m, v, *extra)


def _coords():
    return lax.axis_index("x"), lax.axis_index("y"), lax.axis_index("c")


def _flip(v, bit):
    return 1 - v if bit else v


def _allgather8(v, name):
    R, C = v.shape

    def body(v_ref, out_ref, send_sems, recv_sems, local_sem):
        x, y, c = _coords()
        me = 4 * x + 2 * y + c
        mine = pltpu.make_async_copy(v_ref, out_ref.at[me], local_sem)
        mine.start()

        def copy(m, block):
            peer = (_flip(x, m & 4), _flip(y, m & 2), _flip(c, m & 1))
            return pltpu.make_async_remote_copy(
                src_ref=v_ref, dst_ref=out_ref.at[block], send_sem=send_sems.at[m - 1],
                recv_sem=recv_sems.at[m - 1], device_id=peer, device_id_type=MESH)

        sends = [copy(m, me) for m in range(1, 8)]
        for cp in sends:
            cp.start()
        for m in range(1, 8):
            sender = 4 * _flip(x, m & 4) + 2 * _flip(y, m & 2) + _flip(c, m & 1)
            copy(m, sender).wait_recv()
        for cp in sends:
            cp.wait_send()
        mine.wait()

    vm = pl.BlockSpec(memory_space=pltpu.VMEM)
    return pl.pallas_call(
        body, name=name, in_specs=[vm], out_specs=vm,
        out_shape=jax.ShapeDtypeStruct((8, R, C), F32),
        scratch_shapes=[pltpu.SemaphoreType.DMA((7,)), pltpu.SemaphoreType.DMA((7,)), pltpu.SemaphoreType.DMA],
    )(v)


HBM = pl.BlockSpec(memory_space=pltpu.HBM)
SEM = pl.BlockSpec(memory_space=pltpu.SEMAPHORE)
EFFECT = pltpu.SideEffectType.DATAFLOW_SIDE_EFFECTING


def _peer8(x, y, c, m):
    px, py, pc = _flip(x, m & 4), _flip(y, m & 2), _flip(c, m & 1)
    return (px, py, pc), 4 * px + 2 * py + pc


def _allgather8_start(blocks, name):
    def body(b_ref, send, recv, thru):
        x, y, c = _coords()
        me = 4 * x + 2 * y + c
        for m in range(1, 8):
            peer, _ = _peer8(x, y, c, m)
            pltpu.make_async_remote_copy(src_ref=b_ref.at[me], dst_ref=b_ref.at[me], send_sem=send.at[m - 1],
                                         recv_sem=recv.at[m - 1], device_id=peer, device_id_type=MESH).start()

    sem = pltpu.SemaphoreType.DMA((7,))
    return pl.pallas_call(
        body, name=name, out_shape=[sem, sem, pltpu.HBM(blocks.shape, blocks.dtype)],
        in_specs=[HBM], out_specs=[SEM, SEM, HBM], input_output_aliases={0: 2},
        compiler_params=pltpu.CompilerParams(has_side_effects=EFFECT),
    )(pltpu.with_memory_space_constraint(blocks, pltpu.HBM))


def _allgather8_wait(blocks, send_sem, recv_sem, after, name):
    def body(b_ref, send, recv, after_ref, thru):
        x, y, c = _coords()
        me = 4 * x + 2 * y + c
        for m in range(1, 8):
            peer, sender = _peer8(x, y, c, m)
            cp = pltpu.make_async_remote_copy(src_ref=b_ref.at[me], dst_ref=b_ref.at[sender], send_sem=send.at[m - 1],
                                              recv_sem=recv.at[m - 1], device_id=peer, device_id_type=MESH)
            cp.wait_send()
            cp.wait_recv()

    return pl.pallas_call(
        body, name=name, out_shape=pltpu.HBM(blocks.shape, blocks.dtype),
        in_specs=[HBM, SEM, SEM, ANY], out_specs=HBM, input_output_aliases={0: 0},
        compiler_params=pltpu.CompilerParams(has_side_effects=EFFECT),
    )(blocks, send_sem, recv_sem, after)


def _chip_peer(x, y, m):
    px, py = _flip(x, m & 2), _flip(y, m & 1)
    return px, py, 2 * px + py


def _core_rows(land, c):
    half = land.shape[1] // 2
    return pl.ds(pl.multiple_of(c * half, 16), half)


def _gather_start(lands, groups, halved, after, name):
    n, ng, na = len(lands), len(groups), len(after)

    def body(*refs):
        ins = refs[:n]
        sends, recvs = refs[n + na:n + na + ng], refs[n + na + ng:n + na + 2 * ng]
        token = refs[n + na + 2 * ng + n]
        x, y, c = _coords()
        k = 2 * x + y
        for gi, grp in enumerate(groups):
            for j, t in enumerate(grp):
                mine = ins[t].at[k, _core_rows(ins[t], c), :] if halved[gi] else ins[t].at[k]
                for m in (1, 2, 3):
                    px, py, _ = _chip_peer(x, y, m)
                    pltpu.make_async_remote_copy(
                        src_ref=mine, dst_ref=mine, send_sem=sends[gi].at[3 * j + m - 1],
                        recv_sem=recvs[gi].at[3 * j + m - 1], device_id=(px, py, c), device_id_type=MESH).start()
        token[...] = jnp.zeros_like(token)

    sems = [pltpu.SemaphoreType.DMA((3 * len(g),)) for g in groups]
    out = pl.pallas_call(
        body, name=name,
        out_shape=sems + sems + [pltpu.HBM(a.shape, a.dtype) for a in lands] + [jax.ShapeDtypeStruct((8, 128), F32)],
        in_specs=[HBM] * n + [ANY] * na,
        out_specs=[SEM] * (2 * ng) + [HBM] * n + [pl.BlockSpec(memory_space=pltpu.VMEM)],
        input_output_aliases={t: 2 * ng + t for t in range(n)},
        compiler_params=pltpu.CompilerParams(has_side_effects=EFFECT),
    )(*[pltpu.with_memory_space_constraint(a, pltpu.HBM) for a in lands], *after)
    return out[:ng], out[ng:2 * ng], out[2 * ng:2 * ng + n], out[2 * ng + n]


def _gather_wait(lands, halved, send_sem, recv_sem, after, name):
    n = len(lands)

    def body(*refs):
        ins, send, recv = refs[:n], refs[n], refs[n + 1]
        x, y, c = _coords()
        k = 2 * x + y
        for j in range(n):
            rows = _core_rows(ins[j], c)
            for m in (1, 2, 3):
                px, py, pk = _chip_peer(x, y, m)
                cp = pltpu.make_async_remote_copy(
                    src_ref=ins[j].at[k, rows, :] if halved else ins[j].at[k],
                    dst_ref=ins[j].at[pk, rows, :] if halved else ins[j].at[pk], send_sem=send.at[3 * j + m - 1],
                    recv_sem=recv.at[3 * j + m - 1], device_id=(px, py, c), device_id_type=MESH)
                cp.wait_send()
                cp.wait_recv()

    return pl.pallas_call(
        body, name=name, out_shape=[pltpu.HBM(a.shape, a.dtype) for a in lands],
        in_specs=[HBM] * n + [SEM, SEM, ANY], out_specs=[HBM] * n,
        input_output_aliases={j: j for j in range(n)},
        compiler_params=pltpu.CompilerParams(has_side_effects=EFFECT),
    )(*lands, send_sem, recv_sem, after)


def _sibling_fill(lands, name):
    n = len(lands)

    def body(*refs):
        ins = refs[:n]
        send_sems, recv_sems = refs[2 * n:]
        x, y, c = _coords()
        sends, recvs = [], []
        for t in range(n):
            for m in (1, 2, 3):
                _, _, pk = _chip_peer(x, y, m)
                for rows, lst in ((_core_rows(ins[t], c), sends), (_core_rows(ins[t], 1 - c), recvs)):
                    lst.append(pltpu.make_async_remote_copy(
                        src_ref=ins[t].at[pk, rows, :], dst_ref=ins[t].at[pk, rows, :],
                        send_sem=send_sems.at[3 * t + m - 1], recv_sem=recv_sems.at[3 * t + m - 1],
                        device_id=(x, y, 1 - c), device_id_type=MESH))
        for cp in sends:
            cp.start()
        for cp in recvs:
            cp.wait_recv()
        for cp in sends:
            cp.wait_send()

    return pl.pallas_call(
        body, name=name, in_specs=[ANY] * n, out_specs=[ANY] * n,
        out_shape=[jax.ShapeDtypeStruct(a.shape, a.dtype) for a in lands],
        input_output_aliases={t: t for t in range(n)},
        scratch_shapes=[pltpu.SemaphoreType.DMA((3 * n,)), pltpu.SemaphoreType.DMA((3 * n,))],
    )(*lands)


def _scatter_start(srcs, name, after=()):
    n, na = len(srcs), len(after)

    def body(*refs):
        ins, lands = refs[:n], refs[n:2 * n]
        send, recv = refs[2 * n + na], refs[2 * n + na + 1]
        token = refs[2 * n + na + 2 + 2 * n]
        x, y, c = _coords()
        k = 2 * x + y
        for t in range(n):
            for m in (1, 2, 3):
                px, py, pk = _chip_peer(x, y, m)
                pltpu.make_async_remote_copy(
                    src_ref=ins[t].at[pk], dst_ref=lands[t].at[k], send_sem=send.at[3 * t + m - 1],
                    recv_sem=recv.at[3 * t + m - 1], device_id=(px, py, c), device_id_type=MESH).start()
        token[...] = jnp.zeros_like(token)

    sem = pltpu.SemaphoreType.DMA((3 * n,))
    hbm = [pltpu.HBM(a.shape, a.dtype) for a in srcs]
    operands = list(srcs) + [lax.empty(a.shape, a.dtype) for a in srcs]
    out = pl.pallas_call(
        body, name=name, out_shape=[sem, sem] + hbm + hbm + [jax.ShapeDtypeStruct((8, 128), F32)],
        in_specs=[HBM] * (2 * n) + [ANY] * na,
        out_specs=[SEM, SEM] + [HBM] * (2 * n) + [pl.BlockSpec(memory_space=pltpu.VMEM)],
        input_output_aliases={t: 2 + t for t in range(2 * n)},
        compiler_params=pltpu.CompilerParams(has_side_effects=EFFECT),
    )(*[pltpu.with_memory_space_constraint(a, pltpu.HBM) for a in operands], *after)
    return out[0], out[1], out[2:2 + n], out[2 + n:2 + 2 * n], out[2 + 2 * n]


def _scatter_wait(srcs, lands, send_sem, recv_sem, after, name):
    n = len(srcs)

    def body(*refs):
        ins, land = refs[:n], refs[n:2 * n]
        send, recv = refs[2 * n], refs[2 * n + 1]
        x, y, c = _coords()
        for t in range(n):
            for m in (1, 2, 3):
                px, py, pk = _chip_peer(x, y, m)
                cp = pltpu.make_async_remote_copy(
                    src_ref=ins[t].at[pk], dst_ref=land[t].at[pk], send_sem=send.at[3 * t + m - 1],
                    recv_sem=recv.at[3 * t + m - 1], device_id=(px, py, c), device_id_type=MESH)
                cp.wait_send()
                cp.wait_recv()

    hbm = [pltpu.HBM(a.shape, a.dtype) for a in srcs]
    out = pl.pallas_call(
        body, name=name, out_shape=hbm + hbm, in_specs=[HBM] * (2 * n) + [SEM, SEM, ANY], out_specs=[HBM] * (2 * n),
        input_output_aliases={t: t for t in range(2 * n)},
        compiler_params=pltpu.CompilerParams(has_side_effects=EFFECT),
    )(*srcs, *lands, send_sem, recv_sem, after)
    return out[:n], out[n:]


def _sum_own_half(g, ra, kc, name):
    _, R, C = g.shape
    half = R // 2
    br = _row_block(half, C)
    nb = half // br

    def body(kc_ref, g_ref, ra_ref, o_ref):
        o_ref[...] = (g_ref[...] + ra_ref[...].astype(F32)).astype(BF16)

    return pl.pallas_call(
        body, name=name,
        grid_spec=pltpu.PrefetchScalarGridSpec(
            num_scalar_prefetch=1, grid=(NCHIP, nb),
            in_specs=[pl.BlockSpec((None, br, C), lambda j, i, kc: (j, kc[1] * nb + i, 0)),
                      pl.BlockSpec((None, br, C), lambda j, i, kc: (j, i, 0))],
            out_specs=pl.BlockSpec((None, br, C), lambda j, i, kc: (j, i, 0))),
        out_shape=jax.ShapeDtypeStruct((NCHIP, half, C), BF16),
        compiler_params=_params("parallel", "parallel"),
    )(kc, g, ra)


def _sum_chips(sa, rb, kc, name, after=None):
    _, half, C = rb.shape
    br = _row_block(half, C)
    nb = half // br
    extra = [] if after is None else [after]

    def body(kc_ref, own_ref, r1_ref, r2_ref, r3_ref, *rest):
        out, obuf, local_sems, send_sems, recv_sem = rest[-5:]
        i = pl.program_id(0)
        slot = i % 2
        x, y, c = _coords()

        def copies(i_, slot_):
            rows = out.at[pl.ds(pl.multiple_of((c * nb + i_) * br, 8), br), :]
            return (pltpu.make_async_copy(obuf.at[slot_], rows, local_sems.at[slot_]),
                    pltpu.make_async_remote_copy(src_ref=obuf.at[slot_], dst_ref=rows, send_sem=send_sems.at[slot_],
                                                 recv_sem=recv_sem, device_id=(x, y, 1 - c), device_id_type=MESH))

        @pl.when(i >= 2)
        def _():
            here, there = copies(i, slot)
            here.wait()
            there.wait_send()

        acc = own_ref[...].astype(F32) + r1_ref[...].astype(F32)
        obuf[slot] = (acc + r2_ref[...].astype(F32)) + r3_ref[...].astype(F32)
        here, there = copies(i, slot)
        here.start()
        there.start()

        @pl.when(i == nb - 1)
        def _():
            for s in range(min(2, nb)):
                here, there = copies(i, (i - s) % 2)
                here.wait()
                there.wait_send()
            theirs = out.at[pl.ds(pl.multiple_of((1 - c) * half, 8), half), :]
            pltpu.make_async_remote_copy(src_ref=theirs, dst_ref=theirs, send_sem=send_sems.at[0], recv_sem=recv_sem,
                                         device_id=(x, y, 1 - c), device_id_type=MESH).wait_recv()

    def slab(m):
        return pl.BlockSpec((None, br, C), lambda i, kc: (kc[0] ^ m, i, 0))

    return pl.pallas_call(
        body, name=name,
        grid_spec=pltpu.PrefetchScalarGridSpec(
            num_scalar_prefetch=1, grid=(nb,),
            in_specs=[slab(0), slab(1), slab(2), slab(3)] + [ANY] * len(extra),
            out_specs=ANY,
            scratch_shapes=[pltpu.VMEM((2, br, C), F32), pltpu.SemaphoreType.DMA((2,)), pltpu.SemaphoreType.DMA((2,)),
                            pltpu.SemaphoreType.DMA]),
        out_shape=jax.ShapeDtypeStruct((2 * half, C), F32), compiler_params=_params("arbitrary"),
    )(kc, sa, rb, rb, rb, *extra)


def _sum8(ga, name):
    _, R, C = ga.shape

    def body(g_ref, o_ref):
        acc = g_ref[0]
        for j in range(1, 8):
            acc = acc + g_ref[j]
        o_ref[...] = acc

    return pl.pallas_call(
        body, name=name, in_specs=[pl.BlockSpec((8, R, C), lambda: (0, 0, 0))],
        out_specs=pl.BlockSpec((R, C), lambda: (0, 0)), out_shape=jax.ShapeDtypeStruct((R, C), F32),
    )(ga)


ADA_COLS = 9 * D // NCHIP
ADA_BLK = 256


def _ada_mod(c_all, ada_w, ada_b, kidx):
    def body(k_ref, c_ref, w_ref, b_ref, o_ref):
        cv = c_ref[...]
        cs = cv * _sigmoid(cv)
        o_ref[...] = jnp.dot(cs, w_ref[...], precision=lax.Precision.HIGHEST,
                             preferred_element_type=F32) + b_ref[...]

    nblk = ADA_COLS // ADA_BLK
    return pl.pallas_call(
        body, name="ada_mod",
        grid_spec=pltpu.PrefetchScalarGridSpec(
            num_scalar_prefetch=1, grid=(nblk,),
            in_specs=[pl.BlockSpec((8, D), lambda j, k: (0, 0)),
                      pl.BlockSpec((D, ADA_BLK), lambda j, k: (0, j)),
                      pl.BlockSpec((1, ADA_BLK), lambda j, k: (0, k[0] * nblk + j))],
            out_specs=pl.BlockSpec((8, ADA_BLK), lambda j, k: (0, j))),
        out_shape=jax.ShapeDtypeStruct((8, ADA_COLS), F32),
        compiler_params=_params("parallel"),
    )(kidx, c_all, ada_w, ada_b)


def _ada_grad(c_all_t, dmod_all, kidx):
    def body(k_ref, ct_ref, dm_ref, o_ref):
        cv = ct_ref[...]
        cs = cv * _sigmoid(cv)
        acc = cs[:, 0:1] * dm_ref[0:1, :]
        for b in range(1, 8):
            acc = acc + cs[:, b:b + 1] * dm_ref[b:b + 1, :]
        o_ref[...] = acc

    nblk = ADA_COLS // ADA_BLK
    return pl.pallas_call(
        body, name="ada_grad",
        grid_spec=pltpu.PrefetchScalarGridSpec(
            num_scalar_prefetch=1, grid=(nblk,),
            in_specs=[pl.BlockSpec((D, 8), lambda j, k: (0, 0)),
                      pl.BlockSpec((8, ADA_BLK), lambda j, k: (0, k[0] * nblk + j))],
            out_specs=pl.BlockSpec((D, ADA_BLK), lambda j, k: (0, j))),
        out_shape=jax.ShapeDtypeStruct((D, ADA_COLS), F32),
        compiler_params=_params("parallel"),
    )(kidx, c_all_t, dmod_all)


BIG = ("ffn1_w_in", "ffn1_w_out", "mix_w_in", "hgrn_w_o", "conv_w_o", "mix_w_out", "ffn2_w_in", "ffn2_w_out")
ROW_SHARDED = ("ffn1_w_out", "hgrn_w_o", "conv_w_o", "mix_w_out", "ffn2_w_out")
GATHER_GROUPS = ((0, 1), (2,), (3, 4, 5), (6, 7))
GATHER_HALVED = (True, True, False, False)
GATHER_STARTS = ((0, 1), (2, 3))
PACK_LEN = {"ada_b": 9, "hgrn_lb": 2}
WEIGHTS = ("ada_w", "ada_b", "norm_ffn1", "ffn1_w_in", "ffn1_w_out", "norm_mix", "mix_w_in", "hgrn_lb", "hgrn_g",
           "hgrn_w_o", "conv_w", "conv_b", "conv_ln_g", "conv_ln_b", "conv_w_o", "mix_w_out", "norm_ffn2",
           "ffn2_w_in", "ffn2_w_out", "norm_final")
PACKED = ("ada_b", "norm_ffn1", "norm_mix", "hgrn_g", "conv_b", "conv_ln_g", "conv_ln_b", "norm_ffn2",
          "norm_final", "hgrn_lb")


def _pack_params(p, name):
    parts = [(p[n].reshape(PACK_LEN.get(n, 1), D), 0, PACK_LEN.get(n, 1), PACK_AT[n]) for n in PACKED]
    return _pack_rows(parts, PACK_ROWS, name)


def _step(w, m, v, x, c, tgt):
    xi, yi, ci = _coords()
    kidx = (2 * xi + yi).astype(jnp.int32).reshape(1)
    kc = jnp.stack([2 * xi + yi, ci]).astype(jnp.int32)
    me = 4 * xi + 2 * yi + ci

    cq = D // NCHIP
    first = jnp.zeros((40, cq), F32).at[0:CONV_K].set(w["conv_w"][0]).at[32:36].set(c.reshape(NCHIP, cq))
    first_all = _allgather8(first, "gather_c_conv_w")
    c_all = first_all[:, 32:36, :].reshape(8, D)
    mod_cols = _ada_mod(c_all, w["ada_w"][0], w["ada_b"], kidx)
    mod_all = _allgather8(mod_cols, "gather_mod")
    mod = lax.dynamic_slice(mod_all, (0, me, 0), (8, 1, ADA_COLS))[::2].reshape(9, D)
    small = {n: w[n].reshape(-1, D) for n in ("norm_ffn1", "norm_mix", "hgrn_lb", "hgrn_g", "conv_b", "conv_ln_g",
                                              "conv_ln_b", "norm_ffn2", "norm_final")}
    small["conv_w"] = jnp.concatenate([first_all[2 * j, 0:32, :] for j in range(NCHIP)], axis=1)

    lands, sends, recvs = [], [], []
    after = mod
    for part in GATHER_STARTS:
        tensors = [t for gi in part for t in GATHER_GROUPS[gi]]
        cast = [_cast_into_slot(w[BIG[t]][0], kc, "cast_" + BIG[t], after) for t in tensors]
        groups = [tuple(tensors.index(t) for t in GATHER_GROUPS[gi]) for gi in part]
        s, r, thru, after = _gather_start(cast, groups, [GATHER_HALVED[gi] for gi in part], [after],
                                          "gather_weights_start%d" % part[0])
        lands, sends, recvs = lands + list(thru), sends + list(s), recvs + list(r)
    started_all = after
    ready = {}

    def weight(name, after):
        t = BIG.index(name)
        if t not in ready:
            gi = [t in grp for grp in GATHER_GROUPS].index(True)
            grp = GATHER_GROUPS[gi]
            outs = _gather_wait([lands[j] for j in grp], GATHER_HALVED[gi], sends[gi], recvs[gi],
                                started_all if gi == 0 else after, "gather_weights_wait%d" % gi)
            if GATHER_HALVED[gi]:
                outs = _sibling_fill(outs, "gather_weights_fill%d" % gi)
            ready.update(zip(grp, outs))
        return ready[t].reshape(-1, D) if name in ROW_SHARDED else ready[t]

    grads, delta, new_m, new_v = {}, {}, {}, {}
    flight = []
    landed = []

    def settle(after):
        names, sa, rb, send, recv = flight.pop()
        sa, rb = _scatter_wait(sa, rb, send, recv, after, "rs_chip_wait_" + names[0])
        landed.append((names, sa, rb))

    def reduce(names, pairs, after=None):
        gs = [g.reshape(NCHIP, -1, g.shape[-1]) for g, _ in pairs]
        ra = [r.reshape(NCHIP, -1, r.shape[-1]) for _, r in pairs]
        sa = [_sum_own_half(g, r, kc, "rs_sum_pair_" + n) for g, r, n in zip(gs, ra, names)]
        if flight:
            settle(sa[0])
        send, recv, sa, rb, tok = _scatter_start(sa, "rs_chip_start_" + names[0], () if after is None else (after,))
        flight.append((names, sa, rb, send, recv))
        started.append(tok)
        return tok

    def adamw(n, after=None):
        shape = w[n].shape
        two = (shape[-2], shape[-1])
        out = _adamw(w[n].reshape(two), grads[n], m[n].reshape(two), v[n].reshape(two), "adamw_" + n, after,
                     copy_grad=n in BIG)
        g_ = out[3] if n in BIG else grads[n]
        grads[n], delta[n], new_m[n], new_v[n] = (a.reshape(shape) for a in (g_, out[0], out[1], out[2]))
        return out[1]

    def finish(after=None):
        names, sa, rb = landed.pop(0)
        full = [_sum_chips(s, r, kc, "rs_sum_chips_" + n, after) for s, r, n in zip(sa, rb, names)]
        grads.update(zip(names, full))
        return [adamw(n) for n in names][-1]

    started = []

    smalls = []

    def reduce_small(parts):
        blocks = _pack_rows(parts, PACK_ROWS, "pack_small_grads", slot=me.astype(jnp.int32).reshape(1))
        send, recv, blocks = _allgather8_start(blocks, "gather_small_grads_start")

        def finish(after):
            packed_all = _allgather8_wait(blocks, send, recv, after, "gather_small_grads_wait")
            smalls.extend([packed_all, _sum8(packed_all, "sum_small_grads")])
            return smalls[1]

        return finish

    dx = _local_step(x[0], tgt[0], mod, small, kc, weight, reduce, reduce_small)
    packed_all, gsum = smalls
    loss = (0.5 / D) * jnp.sum(gsum[PACK_AT["loss"]])
    dmod_all = packed_all[:, 0:9, :].reshape(8, 9 * D)
    grads["ada_w"] = _ada_grad(c_all.T, dmod_all, kidx)
    grads["conv_w"] = lax.dynamic_slice(gsum, (PACK_AT["conv_w"], kidx[0] * (D // NCHIP)), (CONV_K, D // NCHIP))

    tok = started[-1]
    adamw("ada_w", tok)
    adamw("conv_w")
    pw, pm, pv = (_pack_params(p, "pack_" + s) for p, s in ((w, "w"), (m, "m"), (v, "v")))
    pd, pnm, pnv = _adamw(pw, gsum, pm, pv, "adamw_small", tok)
    last = pnv
    while landed:
        last = finish(tok)
    settle(last)
    finish()
    for n in PACKED:
        rows = slice(PACK_AT[n], PACK_AT[n] + PACK_LEN.get(n, 1))
        for dst, src in ((grads, gsum), (delta, pd), (new_m, pnm), (new_v, pnv)):
            dst[n] = src[rows].reshape(w[n].shape)

    outs = [loss, dx[None]]
    for d in (grads, delta, new_m, new_v):
        outs += [d[n] for n in WEIGHTS]
    return tuple(outs)


def kernel(x, c, ada_w, ada_b, norm_ffn1, ffn1_w_in, ffn1_w_out, norm_mix, mix_w_in, hgrn_lb, hgrn_g, hgrn_w_o, conv_w, conv_b, conv_ln_g, conv_ln_b, conv_w_o, mix_w_out, norm_ffn2, ffn2_w_in, ffn2_w_out, norm_final, loss_target, m_ada_w, m_ada_b, m_norm_ffn1, m_ffn1_w_in, m_ffn1_w_out, m_norm_mix, m_mix_w_in, m_hgrn_lb, m_hgrn_g, m_hgrn_w_o, m_conv_w, m_conv_b, m_conv_ln_g, m_conv_ln_b, m_conv_w_o, m_mix_w_out, m_norm_ffn2, m_ffn2_w_in, m_ffn2_w_out, m_norm_final, v_ada_w, v_ada_b, v_norm_ffn1, v_ffn1_w_in, v_ffn1_w_out, v_norm_mix, v_mix_w_in, v_hgrn_lb, v_hgrn_g, v_hgrn_w_o, v_conv_w, v_conv_b, v_conv_ln_g, v_conv_ln_b, v_conv_w_o, v_mix_w_out, v_norm_ffn2, v_ffn2_w_in, v_ffn2_w_out, v_norm_final):
    w = dict(ada_w=ada_w, ada_b=ada_b, norm_ffn1=norm_ffn1, ffn1_w_in=ffn1_w_in, ffn1_w_out=ffn1_w_out,
             norm_mix=norm_mix, mix_w_in=mix_w_in, hgrn_lb=hgrn_lb, hgrn_g=hgrn_g, hgrn_w_o=hgrn_w_o, conv_w=conv_w,
             conv_b=conv_b, conv_ln_g=conv_ln_g, conv_ln_b=conv_ln_b, conv_w_o=conv_w_o, mix_w_out=mix_w_out,
             norm_ffn2=norm_ffn2, ffn2_w_in=ffn2_w_in, ffn2_w_out=ffn2_w_out, norm_final=norm_final)
    m = dict(ada_w=m_ada_w, ada_b=m_ada_b, norm_ffn1=m_norm_ffn1, ffn1_w_in=m_ffn1_w_in, ffn1_w_out=m_ffn1_w_out,
             norm_mix=m_norm_mix, mix_w_in=m_mix_w_in, hgrn_lb=m_hgrn_lb, hgrn_g=m_hgrn_g, hgrn_w_o=m_hgrn_w_o,
             conv_w=m_conv_w, conv_b=m_conv_b, conv_ln_g=m_conv_ln_g, conv_ln_b=m_conv_ln_b, conv_w_o=m_conv_w_o,
             mix_w_out=m_mix_w_out, norm_ffn2=m_norm_ffn2, ffn2_w_in=m_ffn2_w_in, ffn2_w_out=m_ffn2_w_out,
             norm_final=m_norm_final)
    v = dict(ada_w=v_ada_w, ada_b=v_ada_b, norm_ffn1=v_norm_ffn1, ffn1_w_in=v_ffn1_w_in, ffn1_w_out=v_ffn1_w_out,
             norm_mix=v_norm_mix, mix_w_in=v_mix_w_in, hgrn_lb=v_hgrn_lb, hgrn_g=v_hgrn_g, hgrn_w_o=v_hgrn_w_o,
             conv_w=v_conv_w, conv_b=v_conv_b, conv_ln_g=v_conv_ln_g, conv_ln_b=v_conv_ln_b, conv_w_o=v_conv_w_o,
             mix_w_out=v_mix_w_out, norm_ffn2=v_norm_ffn2, ffn2_w_in=v_ffn2_w_in, ffn2_w_out=v_ffn2_w_out,
             norm_final=v_norm_final)
    return _step(w, m, v, x, c, loss_target)
```

```python
import functools

import jax
import jax.numpy as jnp
from jax import lax
from jax.experimental import pallas as pl
from jax.experimental.pallas import tpu as pltpu

F32 = jnp.float32
BF16 = jnp.bfloat16

D = 1024
DFF = 2816
NCHIP = 4
FSH = 2 * DFF // NCHIP
HEADS = 8
DK = 128
CHUNK = 64
CONV_K = 31
HALO = 32
EPS = 1e-6
TB = 256
CB = 1024
DW_TOKENS = 2048
VMEM_LIMIT = 56 * 1024 * 1024

ADAM_LR = 0.001
ADAM_B1 = 0.9
ADAM_B2 = 0.999
ADAM_EPS = 1e-08
ADAM_WD = 0.01
ADAM_STEP = 10

MESH = pl.DeviceIdType.MESH
ANY = pl.BlockSpec(memory_space=pl.ANY)


def _params(*sem):
    return pltpu.CompilerParams(dimension_semantics=sem, vmem_limit_bytes=VMEM_LIMIT)


def _sigmoid(x):
    return 0.5 * jnp.tanh(0.5 * x) + 0.5


def _dsilu(x, sg):
    return sg * (1.0 + x * (1.0 - sg))


def _nt(a, b):
    return lax.dot_general(a, b, (((1,), (1,)), ((), ())), preferred_element_type=F32)


def _tn(a, b):
    return lax.dot_general(a, b, (((0,), (0,)), ((), ())), preferred_element_type=F32)


def _nn(a, b):
    return jnp.dot(a, b, preferred_element_type=F32)


def _colsum(x):
    return jnp.sum(x, axis=0, keepdims=True)


def _rms_fwd(x, gn, sc, sh):
    r = lax.rsqrt(jnp.mean(x * x, axis=-1, keepdims=True) + EPS)
    n = x * r
    h = (n * gn) * (1.0 + sc) + sh
    return r, n, h


def _rms_bwd(dh, r, n, gn, sc, acc_ref):
    acc_ref[0:1, :] += _colsum(dh)
    acc_ref[1:2, :] += _colsum(dh * (n * gn))
    dng = dh * (1.0 + sc)
    acc_ref[3:4, :] += _colsum(dng * n)
    dn = dng * gn
    return r * (dn - n * jnp.mean(dn * n, axis=-1, keepdims=True))


def _loss_head(x, tgt, gf, acc_ref):
    r = lax.rsqrt(jnp.mean(x * x, axis=-1, keepdims=True) + EPS)
    n = x * r
    err = n * gf - tgt
    acc_ref[1:2, :] += _colsum(err * err)
    dy = err * (1.0 / D)
    acc_ref[0:1, :] += _colsum(dy * n)
    dn = dy * gf
    return r * (dn - n * jnp.mean(dn * n, axis=-1, keepdims=True))


def _ffn_fwd(x, vec, w_in, w_out, name, head=None):
    T = x.shape[0]
    nh = 0 if head is None else 2

    def body(x_ref, vec_ref, *rest):
        win_hbm, wout_hbm = rest[nh:nh + 2]
        xo_ref, h_ref, a_ref, b_ref, s_ref, f_ref = rest[nh + 2:nh + 8]
        win, wout = rest[-2:]

        @pl.when(pl.program_id(0) == 0)
        def _():
            pltpu.sync_copy(win_hbm, win)
            pltpu.sync_copy(wout_hbm, wout)
            if head is not None:
                rest[nh + 8][...] = jnp.zeros((8, D), F32)

        x = x_ref[...]
        sh, sc, gate, gn = vec_ref[0:1, :], vec_ref[1:2, :], vec_ref[2:3, :], vec_ref[3:4, :]
        _, _, h = _rms_fwd(x, gn, sc, sh)
        hb = h.astype(BF16)
        h_ref[...] = hb
        f = jnp.zeros((TB, D), F32)
        for j in range(2):
            cols = slice(j * FSH, (j + 1) * FSH)
            a = _nn(hb, win[j])
            b = _nn(hb, win[2 + j])
            s = (a * _sigmoid(a) * b).astype(BF16)
            a_ref[:, cols] = a.astype(BF16)
            b_ref[:, cols] = b.astype(BF16)
            s_ref[:, cols] = s
            f = f + _nn(s, wout[cols, :])
        xo = x + (0.5 * gate) * f
        f_ref[...] = f.astype(BF16)
        if head is None:
            xo_ref[...] = xo
        else:
            xo_ref[...] = _loss_head(xo, rest[0][...], rest[1][0:1, :], rest[nh + 8])

    row = lambda w: pl.BlockSpec((TB, w), lambda i: (i, 0))
    vec8 = pl.BlockSpec((8, D), lambda i: (0, 0))
    acc = [] if head is None else [jax.ShapeDtypeStruct((8, D), F32)]
    return pl.pallas_call(
        body, name=name, grid=(T // TB,),
        in_specs=[row(D), vec8] + ([] if head is None else [row(D), vec8]) + [ANY, ANY],
        out_specs=[row(D), row(D), row(DFF), row(DFF), row(DFF), row(D)] + [vec8] * len(acc),
        out_shape=[jax.ShapeDtypeStruct((T, D), F32), jax.ShapeDtypeStruct((T, D), BF16),
                   jax.ShapeDtypeStruct((T, DFF), BF16), jax.ShapeDtypeStruct((T, DFF), BF16),
                   jax.ShapeDtypeStruct((T, DFF), BF16), jax.ShapeDtypeStruct((T, D), BF16)] + acc,
        scratch_shapes=[pltpu.VMEM((NCHIP, D, FSH), BF16), pltpu.VMEM((DFF, D), BF16)],
        compiler_params=_params("arbitrary"),
    )(x, vec, *([] if head is None else list(head)), w_in, w_out)


def _ffn_bwd(dxo, x, vec, a, b, f, w_in, w_out, name):
    T = x.shape[0]

    def body(dxo_ref, x_ref, vec_ref, a_ref, b_ref, f_ref, win_hbm, wout_hbm,
             dx_ref, df_ref, dab_ref, acc_ref, win, wout):
        @pl.when(pl.program_id(0) == 0)
        def _():
            pltpu.sync_copy(win_hbm, win)
            pltpu.sync_copy(wout_hbm, wout)
            acc_ref[...] = jnp.zeros_like(acc_ref)

        dxo = dxo_ref[...]
        x = x_ref[...]
        sh, sc, gate, gn = vec_ref[0:1, :], vec_ref[1:2, :], vec_ref[2:3, :], vec_ref[3:4, :]
        r, n, _ = _rms_fwd(x, gn, sc, sh)
        acc_ref[2:3, :] += _colsum(0.5 * f_ref[...].astype(F32) * dxo)
        dfb = ((0.5 * gate) * dxo).astype(BF16)
        df_ref[...] = dfb
        dh = jnp.zeros((TB, D), F32)
        for j in range(2):
            cols = slice(j * FSH, (j + 1) * FSH)
            ds = _nt(dfb, wout[cols, :])
            av = a_ref[:, cols].astype(F32)
            bv = b_ref[:, cols].astype(F32)
            sg = _sigmoid(av)
            da = (ds * bv * _dsilu(av, sg)).astype(BF16)
            db = (ds * (av * sg)).astype(BF16)
            dab_ref[j] = da
            dab_ref[2 + j] = db
            dh = dh + _nt(da, win[j]) + _nt(db, win[2 + j])
        dx_ref[...] = dxo + _rms_bwd(dh, r, n, gn, sc, acc_ref)

    row = lambda w: pl.BlockSpec((TB, w), lambda i: (i, 0))
    vec8 = pl.BlockSpec((8, D), lambda i: (0, 0))
    return pl.pallas_call(
        body, name=name, grid=(T // TB,),
        in_specs=[row(D), row(D), vec8, row(DFF), row(DFF), row(D), ANY, ANY],
        out_specs=[row(D), pl.BlockSpec((None, TB, D), lambda i: (0, i, 0)),
                   pl.BlockSpec((NCHIP, TB, FSH), lambda i: (0, i, 0)), vec8],
        out_shape=[jax.ShapeDtypeStruct((T, D), F32), jax.ShapeDtypeStruct((1, T, D), BF16),
                   jax.ShapeDtypeStruct((NCHIP, T, FSH), BF16), jax.ShapeDtypeStruct((8, D), F32)],
        scratch_shapes=[pltpu.VMEM((NCHIP, D, FSH), BF16), pltpu.VMEM((DFF, D), BF16)],
        compiler_params=_params("arbitrary"),
    )(dxo, x, vec, a, b, f, w_in, w_out)


def _mm_tn(a, b3, hp, kc, shard_rows, name, into=None, slab=0, slabs=None, after=None):
    T, M = a.shape
    P, _, N = b3.shape
    tm = M if M <= 1408 else M // 2
    tk = min(T, DW_TOKENS)
    nk = T // tk
    ni = M // tm
    slabs = P // hp if slabs is None else slabs
    half = shard_rows // 2
    extra = ([] if into is None else list(into)) + ([] if after is None else [after])

    def body(kc_ref, a_ref, b_ref, *rest):
        o_ref, ra_ref, hbuf, send_sems, recv_sem = rest[-5:]
        p, i, k = pl.program_id(0), pl.program_id(1), pl.program_id(2)
        x, y, c = _coords()
        step = p * ni + i
        slot = step % 2

        def send(p_, i_, slot_):
            dst = ra_ref.at[slab + p_ // hp, pl.ds(pl.multiple_of(i_ * (tm // 2), 8), tm // 2),
                            pl.ds(pl.multiple_of((p_ % hp) * N, LANES), N)]
            return pltpu.make_async_remote_copy(
                src_ref=hbuf.at[slot_], dst_ref=dst, send_sem=send_sems.at[slot_], recv_sem=recv_sem,
                device_id=(x, y, 1 - c), device_id_type=MESH)

        @pl.when(k == 0)
        def _():
            o_ref[...] = jnp.zeros_like(o_ref)

        o_ref[...] += _tn(a_ref[...], b_ref[...])

        @pl.when(k == nk - 1)
        def _():
            @pl.when(step >= 2)
            def _():
                send(p, i, slot).wait_send()

            for j in range(tm // shard_rows):
                start = pl.multiple_of(j * shard_rows + (1 - kc_ref[1]) * half, 8)
                hbuf[slot, j * half:(j + 1) * half, :] = o_ref[pl.ds(start, half), :].astype(BF16)
            send(p, i, slot).start()

        @pl.when((step == P * ni - 1) & (k == nk - 1))
        def _():
            for s in range(min(2, P * ni)):
                send(p, i, (step - s) % 2).wait_send()
            mine = ra_ref.at[slab:slab + P // hp]
            pltpu.make_async_remote_copy(src_ref=mine, dst_ref=mine, send_sem=send_sems.at[0], recv_sem=recv_sem,
                                         device_id=(x, y, 1 - c), device_id_type=MESH).wait_recv()

    return pl.pallas_call(
        body, name=name,
        grid_spec=pltpu.PrefetchScalarGridSpec(
            num_scalar_prefetch=1, grid=(P, ni, nk),
            in_specs=[pl.BlockSpec((tk, tm), lambda p, i, k, kc: (k, i)),
                      pl.BlockSpec((None, tk, N), lambda p, i, k, kc: (p, k, 0))] + [ANY] * len(extra),
            out_specs=[pl.BlockSpec((None, tm, N), lambda p, i, k, kc: (slab + p // hp, i, p % hp)), ANY],
            scratch_shapes=[pltpu.VMEM((2, tm // 2, N), BF16), pltpu.SemaphoreType.DMA((2,)),
                            pltpu.SemaphoreType.DMA]),
        out_shape=[jax.ShapeDtypeStruct((slabs, M, hp * N), F32), jax.ShapeDtypeStruct((slabs, M // 2, hp * N), BF16)],
        input_output_aliases={} if into is None else {3: 0, 4: 1},
        compiler_params=_params("arbitrary", "arbitrary", "arbitrary"),
    )(kc, a, b3, *extra)


def _mix_proj_fwd(x, vec, w_in):
    T = x.shape[0]

    def body(x_ref, vec_ref, w_hbm, h_ref, qr_ref, g_ref, k_ref, v_ref, og_ref, u_ref, ua_ref, ub_ref,
             sa_ref, sb_ref, w):
        @pl.when(pl.program_id(0) == 0)
        def _():
            pltpu.sync_copy(w_hbm, w)

        x = x_ref[...]
        sh, sc, gn, lb = vec_ref[0:1, :], vec_ref[1:2, :], vec_ref[3:4, :], vec_ref[4:5, :]
        _, _, h = _rms_fwd(x, gn, sc, sh)
        hb = h.astype(BF16)
        h_ref[...] = hb
        p = _nn(hb, w[0])
        qr_ref[...] = p[:, :D].astype(BF16)
        fg = lb + (1.0 - lb) * _sigmoid(p[:, D:])
        g_ref[...] = jnp.log(fg)
        k_ref[...] = (1.0 - fg).astype(BF16)
        p = _nn(hb, w[1])
        v_ref[...] = p[:, :D].astype(BF16)
        og_ref[...] = p[:, D:].astype(BF16)
        p = _nn(hb, w[2])
        ua, ub = p[:, :D], p[:, D:]
        u_ref[...] = ua * _sigmoid(ub)
        ua_ref[...] = ua.astype(BF16)
        ub_ref[...] = ub.astype(BF16)
        p = _nn(hb, w[3])
        sa_ref[...] = _sigmoid(p[:, :D]).astype(BF16)
        sb_ref[...] = _sigmoid(p[:, D:]).astype(BF16)

    row = pl.BlockSpec((TB, D), lambda i: (i, 0))
    bf = jax.ShapeDtypeStruct((T, D), BF16)
    f32 = jax.ShapeDtypeStruct((T, D), F32)
    return pl.pallas_call(
        body, name="mix_proj_fwd", grid=(T // TB,),
        in_specs=[row, pl.BlockSpec((8, D), lambda i: (0, 0)), ANY],
        out_specs=[row] * 11,
        out_shape=[bf, bf, f32, bf, bf, bf, f32, bf, bf, bf, bf],
        scratch_shapes=[pltpu.VMEM((NCHIP, D, 2 * D), BF16)],
        compiler_params=_params("arbitrary"),
    )(x, vec, w_in)


def _mix_proj_bwd(dxo, x, vec, dpa, dpb, dpc, w_in):
    T = x.shape[0]

    def body(dxo_ref, x_ref, vec_ref, dpa_ref, dpb_ref, dpc_ref, w_hbm, dx_ref, acc_ref, w):
        @pl.when(pl.program_id(0) == 0)
        def _():
            pltpu.sync_copy(w_hbm, w)
            acc_ref[...] = jnp.zeros_like(acc_ref)

        x = x_ref[...]
        sh, sc, gn = vec_ref[0:1, :], vec_ref[1:2, :], vec_ref[3:4, :]
        r, n, _ = _rms_fwd(x, gn, sc, sh)
        dh = jnp.zeros((TB, D), F32)
        for p in range(8):
            src = dpa_ref[p] if p < 4 else (dpb_ref[p - 4] if p < 6 else dpc_ref[p - 6])
            dh = dh + _nt(src, w[p // 2, :, (p % 2) * D:(p % 2 + 1) * D])
        dx_ref[...] = dxo_ref[...] + _rms_bwd(dh, r, n, gn, sc, acc_ref)

    row = pl.BlockSpec((TB, D), lambda i: (i, 0))
    vec8 = pl.BlockSpec((8, D), lambda i: (0, 0))
    stack = lambda k: pl.BlockSpec((k, TB, D), lambda i: (0, i, 0))
    return pl.pallas_call(
        body, name="mix_proj_bwd", grid=(T // TB,),
        in_specs=[row, row, vec8, stack(4), stack(2), stack(2), ANY],
        out_specs=[row, vec8],
        out_shape=[jax.ShapeDtypeStruct((T, D), F32), jax.ShapeDtypeStruct((8, D), F32)],
        scratch_shapes=[pltpu.VMEM((NCHIP, D, 2 * D), BF16)],
        compiler_params=_params("arbitrary"),
    )(dxo, x, vec, dpa, dpb, dpc, w_in)


def _tri(lower):
    r = lax.broadcasted_iota(jnp.int32, (CHUNK, CHUNK), 0)
    c = lax.broadcasted_iota(jnp.int32, (CHUNK, CHUNK), 1)
    return (c <= r) if lower else (c >= r)


def _cumsum_rows(mask, g):
    hi = g.astype(BF16)
    rest = g - hi.astype(F32)
    mid = rest.astype(BF16)
    low = (rest - mid.astype(F32)).astype(BF16)
    n = g.shape[1]
    p = _nn(mask.astype(BF16), jnp.concatenate([hi, mid, low], axis=1))
    return (p[:, 2 * n:] + p[:, n:2 * n]) + p[:, :n]


def _chunk_decay(low, g, nck):
    bs, mids, lasts = [], [], []
    for c in range(nck):
        gc = g[c * CHUNK:(c + 1) * CHUNK]
        bs.append(_cumsum_rows(low, gc))
        mids.append(_colsum(gc[0:CHUNK // 2]))
        lasts.append(_colsum(gc))
    spread = lambda rows: jnp.concatenate([jnp.broadcast_to(r, (CHUNK, DK)) for r in rows], axis=0)
    return jnp.concatenate(bs, axis=0), spread(mids), spread(lasts), lasts


def _hgrn_fwd(qr, g, k, v, og, vec):
    T = qr.shape[0]
    nck = CB // CHUNK

    def body(qr_ref, g_ref, k_ref, v_ref, og_ref, vec_ref, out_ref, o_ref, st_ref, state):
        @pl.when(pl.program_id(1) == 0)
        def _():
            state[...] = jnp.zeros_like(state)

        low = _tri(True)
        qv = qr_ref[...].astype(F32)
        q = qv * _sigmoid(qv) * (DK ** -0.5)
        kk = k_ref[...].astype(F32)
        vb = v_ref[...]
        b, mid, last, lasts = _chunk_decay(low, g_ref[...], nck)
        qt = (q * jnp.exp(b - mid)).astype(BF16)
        kt = (kk * jnp.exp(mid - b)).astype(BF16)
        qe = (q * jnp.exp(b)).astype(BF16)
        kd = (kk * jnp.exp(last - b)).astype(BF16)
        intra, grow = [], []
        for c in range(nck):
            r = slice(c * CHUNK, (c + 1) * CHUNK)
            att = jnp.where(low, _nt(qt[r], kt[r]), 0.0).astype(BF16)
            intra.append(_nn(att, vb[r]))
            grow.append(_tn(vb[r], kd[r]))
        st = state[...]
        inter = []
        for c in range(nck):
            stb = st.astype(BF16)
            st_ref[c] = stb
            inter.append(_nt(qe[c * CHUNK:(c + 1) * CHUNK], stb))
            st = st * jnp.exp(lasts[c]) + grow[c]
        state[...] = st
        o = jnp.concatenate(intra, axis=0) + jnp.concatenate(inter, axis=0)
        o_ref[...] = o
        ogv = og_ref[...].astype(F32)
        rms = lax.rsqrt(jnp.mean(o * o, axis=-1, keepdims=True) + EPS)
        out_ref[...] = (o * rms * vec_ref[5:6, :] * (ogv * _sigmoid(ogv))).astype(BF16)

    blk = pl.BlockSpec((CB, DK), lambda h, i: (i, h))
    return pl.pallas_call(
        body, name="hgrn_fwd", grid=(HEADS, T // CB),
        in_specs=[blk, blk, blk, blk, blk, pl.BlockSpec((8, DK), lambda h, i: (0, h))],
        out_specs=[blk, blk, pl.BlockSpec((None, nck, DK, DK), lambda h, i: (h, i, 0, 0))],
        out_shape=[jax.ShapeDtypeStruct((T, D), BF16), jax.ShapeDtypeStruct((T, D), F32),
                   jax.ShapeDtypeStruct((HEADS, T // CHUNK, DK, DK), BF16)],
        scratch_shapes=[pltpu.VMEM((DK, DK), F32)],
        compiler_params=_params("parallel", "arbitrary"),
    )(qr, g, k, v, og, vec)


def _hgrn_bwd(dout, og, qr, g, k, v, o, st, vec):
    T = qr.shape[0]
    nck = CB // CHUNK
    nb = T // CB

    def body(dout_ref, og_ref, qr_ref, g_ref, k_ref, v_ref, o_ref, st_ref, vec_ref,
             dp_ref, acc_ref, dstate):
        @pl.when(pl.program_id(1) == 0)
        def _():
            dstate[...] = jnp.zeros_like(dstate)
            acc_ref[...] = jnp.zeros_like(acc_ref)

        o = o_ref[...]
        ogv = og_ref[...].astype(F32)
        dout = dout_ref[...].astype(F32)
        hg = vec_ref[5:6, :]
        sgo = _sigmoid(ogv)
        rms = lax.rsqrt(jnp.mean(o * o, axis=-1, keepdims=True) + EPS)
        ohat = o * rms
        dp_ref[3] = (dout * (ohat * hg) * _dsilu(ogv, sgo)).astype(BF16)
        don = dout * (ogv * sgo)
        acc_ref[0:1, :] += _colsum(don * ohat)
        dohat = don * hg
        dob = (rms * (dohat - ohat * jnp.mean(dohat * ohat, axis=-1, keepdims=True))).astype(BF16)

        low = _tri(True)
        upp = _tri(False)
        lb = vec_ref[4:5, :]
        qv = qr_ref[...].astype(F32)
        sgq = _sigmoid(qv)
        q = qv * sgq * (DK ** -0.5)
        kk = k_ref[...].astype(F32)
        vb = v_ref[...]
        gv = g_ref[...]
        b, mid, last, lasts = _chunk_decay(low, gv, nck)
        eq = jnp.exp(b - mid)
        ek = jnp.exp(mid - b)
        eb = jnp.exp(b)
        ed = jnp.exp(last - b)
        qtb, ktb, qeb, kdb = ((t).astype(BF16) for t in (q * eq, kk * ek, q * eb, kk * ed))
        rows = [slice(c * CHUNK, (c + 1) * CHUNK) for c in range(nck)]

        dv1, dqt, dkt, dqe, grow = [], [], [], [], []
        for c, r in enumerate(rows):
            att = jnp.where(low, _nt(qtb[r], ktb[r]), 0.0).astype(BF16)
            datt = jnp.where(low, _nt(dob[r], vb[r]), 0.0).astype(BF16)
            dv1.append(_tn(att, dob[r]))
            dqt.append(_nn(datt, ktb[r]))
            dkt.append(_tn(datt, qtb[r]))
            dqe.append(_nn(dob[r], st_ref[c]))
            grow.append(_tn(dob[r], qeb[r]))
        ds = dstate[...]
        ds1b, dl_state = [None] * nck, [None] * nck
        for c in reversed(range(nck)):
            el = jnp.exp(lasts[c])
            ds1b[c] = ds.astype(BF16)
            dl_state[c] = el * _colsum(ds * st_ref[c].astype(F32))
            ds = ds * el + grow[c]
        dstate[...] = ds
        dkd = jnp.concatenate([_nn(vb[r], ds1b[c]) for c, r in enumerate(rows)], axis=0)
        dv = jnp.concatenate(dv1, axis=0) + jnp.concatenate([_nt(kdb[r], ds1b[c]) for c, r in enumerate(rows)], axis=0)
        dqt, dkt, dqe = (jnp.concatenate(t, axis=0) for t in (dqt, dkt, dqe))
        dq = dqt * eq + dqe * eb
        dk = dkt * ek + dkd * ed
        dkdkd = dkd * kdb.astype(F32)
        db = dqt * qtb.astype(F32) - dkt * ktb.astype(F32) + dqe * qeb.astype(F32) - dkdkd
        dg = jnp.concatenate([_cumsum_rows(upp, db[r]) + (_colsum(dkdkd[r]) + dl_state[c])
                              for c, r in enumerate(rows)], axis=0)
        fg = jnp.exp(gv)
        dfg = dg * jnp.exp(-gv) - dk
        one_m_sig = (1.0 - fg) * (1.0 / (1.0 - lb))
        dp_ref[0] = (dq * (DK ** -0.5) * _dsilu(qv, sgq)).astype(BF16)
        dp_ref[1] = (dfg * (fg - lb) * one_m_sig).astype(BF16)
        dp_ref[2] = dv.astype(BF16)
        dlb = _colsum(dfg * one_m_sig) * (lb * (1.0 - lb))
        acc_ref[1:2, :] += dlb
        acc_ref[2:3, :] -= dlb

    blk = pl.BlockSpec((CB, DK), lambda h, i: (nb - 1 - i, h))
    return pl.pallas_call(
        body, name="hgrn_bwd", grid=(HEADS, nb),
        in_specs=[blk, blk, blk, blk, blk, blk, blk,
                  pl.BlockSpec((None, nck, DK, DK), lambda h, i: (h, nb - 1 - i, 0, 0)),
                  pl.BlockSpec((8, DK), lambda h, i: (0, h))],
        out_specs=[pl.BlockSpec((4, CB, DK), lambda h, i: (0, nb - 1 - i, h)),
                   pl.BlockSpec((8, DK), lambda h, i: (0, h))],
        out_shape=[jax.ShapeDtypeStruct((4, T, D), BF16), jax.ShapeDtypeStruct((8, D), F32)],
        scratch_shapes=[pltpu.VMEM((DK, DK), F32)],
        compiler_params=_params("parallel", "arbitrary"),
    )(dout, og, qr, g, k, v, o, st, vec)


def _ln_fwd(uc, lg, lbias):
    mu = jnp.mean(uc, axis=-1, keepdims=True)
    xc = uc - mu
    rstd = lax.rsqrt(jnp.mean(xc * xc, axis=-1, keepdims=True) + EPS)
    z = xc * rstd
    return rstd, z, z * lg + lbias


LANES = 128
SUBLANES = 8
CONV_ROWS = 64


def _lane_tiles():
    return [slice(l * LANES, (l + 1) * LANES) for l in range(D // LANES)]


def _row_shifts(x):
    n = x.shape[0]
    return [x] + [pltpu.roll(x, n - r, axis=0) for r in range(1, SUBLANES)]


TAPS_PAST = tuple(HALO - (CONV_K - 1) + j for j in range(CONV_K))
TAPS_AHEAD = tuple(CONV_K - 1 - j for j in range(CONV_K))


def _tap_windows(shifted, starts, r0, rows):
    for r in range(SUBLANES):
        taps = [(j, s // SUBLANES) for j, s in enumerate(starts) if s % SUBLANES == r]
        if not taps:
            continue
        lo = min(a for _, a in taps)
        hi = max(a for _, a in taps)
        span = shifted[r][r0 + lo * SUBLANES:r0 + hi * SUBLANES + rows]
        for j, a in taps:
            yield j, span[(a - lo) * SUBLANES:(a - lo) * SUBLANES + rows]


def _conv_fwd(u, cw, cvec):
    T = u.shape[0]
    per = TB // HALO

    def body(u_ref, halo_ref, cw_ref, cvec_ref, us_ref, uc_ref, pad):
        i = pl.program_id(0)
        pad[0:HALO, :] = jnp.where(i > 0, halo_ref[...], 0.0)
        pad[HALO:, :] = u_ref[...]
        for lanes in _lane_tiles():
            shifted = _row_shifts(pad[:, lanes])
            for r0 in range(0, TB, CONV_ROWS):
                acc = jnp.broadcast_to(cvec_ref[0:1, lanes], (CONV_ROWS, LANES))
                for j, window in _tap_windows(shifted, TAPS_PAST, r0, CONV_ROWS):
                    acc = acc + cw_ref[j:j + 1, lanes] * window
                uc_ref[r0:r0 + CONV_ROWS, lanes] = acc
        _, _, ul = _ln_fwd(uc_ref[...], cvec_ref[1:2, :], cvec_ref[2:3, :])
        us_ref[...] = (ul * _sigmoid(ul)).astype(BF16)

    row = pl.BlockSpec((TB, D), lambda i: (i, 0))
    return pl.pallas_call(
        body, name="conv_fwd", grid=(T // TB,),
        in_specs=[row, pl.BlockSpec((HALO, D), lambda i: (jnp.maximum(i * per - 1, 0), 0)),
                  pl.BlockSpec((32, D), lambda i: (0, 0)), pl.BlockSpec((8, D), lambda i: (0, 0))],
        out_specs=[row, row],
        out_shape=[jax.ShapeDtypeStruct((T, D), BF16), jax.ShapeDtypeStruct((T, D), F32)],
        scratch_shapes=[pltpu.VMEM((TB + HALO, D), F32)],
        compiler_params=_params("parallel"),
    )(u, u, cw, cvec)


def _conv_bwd_taps(duc, u, ua, ub, cw):
    T = u.shape[0]
    per = TB // HALO
    nblk = T // TB

    def body(duc_ref, dnext_ref, u_ref, uprev_ref, ua_ref, ub_ref, cw_ref, dp_ref, dcw_ref, upad, dpad, dcw):
        i = pl.program_id(0)

        @pl.when(i == 0)
        def _():
            dcw[...] = jnp.zeros_like(dcw)

        upad[0:HALO, :] = jnp.where(i > 0, uprev_ref[...], 0.0)
        upad[HALO:, :] = u_ref[...]
        dpad[0:TB, :] = duc_ref[...]
        dpad[TB:, :] = jnp.where(i < nblk - 1, dnext_ref[...], 0.0)
        for lanes in _lane_tiles():
            ushift = _row_shifts(upad[:, lanes])
            dshift = _row_shifts(dpad[:, lanes])
            for r0 in range(0, TB, CONV_ROWS):
                rows = slice(r0, r0 + CONV_ROWS)
                duc = duc_ref[rows, lanes]
                for j, window in _tap_windows(ushift, TAPS_PAST, r0, CONV_ROWS):
                    prod = duc * window
                    dcw[j, :, lanes] += jnp.sum(prod.reshape(CONV_ROWS // SUBLANES, SUBLANES, LANES), axis=0)
                du = jnp.zeros((CONV_ROWS, LANES), F32)
                for j, window in _tap_windows(dshift, TAPS_AHEAD, r0, CONV_ROWS):
                    du = du + cw_ref[j:j + 1, lanes] * window
                ua = ua_ref[rows, lanes].astype(F32)
                sg = _sigmoid(ub_ref[rows, lanes].astype(F32))
                dp_ref[0, rows, lanes] = (du * sg).astype(BF16)
                dp_ref[1, rows, lanes] = (du * ua * sg * (1.0 - sg)).astype(BF16)

        @pl.when(i == nblk - 1)
        def _():
            dcw_ref[...] = jnp.sum(dcw[...], axis=1)

    row = pl.BlockSpec((TB, D), lambda i: (i, 0))
    return pl.pallas_call(
        body, name="conv_bwd_taps", grid=(nblk,),
        in_specs=[row, pl.BlockSpec((HALO, D), lambda i: (jnp.minimum((i + 1) * per, T // HALO - 1), 0)),
                  row, pl.BlockSpec((HALO, D), lambda i: (jnp.maximum(i * per - 1, 0), 0)),
                  row, row, pl.BlockSpec((32, D), lambda i: (0, 0))],
        out_specs=[pl.BlockSpec((2, TB, D), lambda i: (0, i, 0)), pl.BlockSpec((32, D), lambda i: (0, 0))],
        out_shape=[jax.ShapeDtypeStruct((2, T, D), BF16), jax.ShapeDtypeStruct((32, D), F32)],
        scratch_shapes=[pltpu.VMEM((TB + HALO, D), F32), pltpu.VMEM((TB + HALO, D), F32),
                        pltpu.VMEM((32, SUBLANES, D), F32)],
        compiler_params=_params("arbitrary"),
    )(duc, duc, u, u, ua, ub, cw)


def _merge_fwd(x, oa, us, sa, sb, vec, w_ho, w_co, w_mo):
    T = x.shape[0]

    def body(x_ref, oa_ref, us_ref, sa_ref, sb_ref, vec_ref, who_hbm, wco_hbm, wmo_hbm,
             xo_ref, ya_ref, yb_ref, mg_ref, mo_ref, who, wco, wmo):
        @pl.when(pl.program_id(0) == 0)
        def _():
            pltpu.sync_copy(who_hbm, who)
            pltpu.sync_copy(wco_hbm, wco)
            pltpu.sync_copy(wmo_hbm, wmo)

        ya = _nn(oa_ref[...], who[...])
        yb = _nn(us_ref[...], wco[...])
        mg = (sa_ref[...].astype(F32) * ya + sb_ref[...].astype(F32) * yb).astype(BF16)
        mo = _nn(mg, wmo[...])
        xo_ref[...] = x_ref[...] + vec_ref[2:3, :] * mo
        ya_ref[...] = ya.astype(BF16)
        yb_ref[...] = yb.astype(BF16)
        mg_ref[...] = mg
        mo_ref[...] = mo.astype(BF16)

    row = pl.BlockSpec((TB, D), lambda i: (i, 0))
    bf = jax.ShapeDtypeStruct((T, D), BF16)
    wv = pltpu.VMEM((D, D), BF16)
    return pl.pallas_call(
        body, name="merge_fwd", grid=(T // TB,),
        in_specs=[row, row, row, row, row, pl.BlockSpec((8, D), lambda i: (0, 0)), ANY, ANY, ANY],
        out_specs=[row] * 5,
        out_shape=[jax.ShapeDtypeStruct((T, D), F32), bf, bf, bf, bf],
        scratch_shapes=[wv, wv, wv],
        compiler_params=_params("arbitrary"),
    )(x, oa, us, sa, sb, vec, w_ho, w_co, w_mo)


def _merge_bwd(dxo, mo, ya, yb, sa, sb, uc, vec, cvec, w_ho, w_co, w_mo):
    T = dxo.shape[0]

    def body(dxo_ref, mo_ref, ya_ref, yb_ref, sa_ref, sb_ref, uc_ref, vec_ref, cvec_ref, who_hbm, wco_hbm, wmo_hbm,
             dmo_ref, dya_ref, dyb_ref, doa_ref, duc_ref, dp_ref, acc_ref, cacc_ref, who, wco, wmo):
        @pl.when(pl.program_id(0) == 0)
        def _():
            pltpu.sync_copy(who_hbm, who)
            pltpu.sync_copy(wco_hbm, wco)
            pltpu.sync_copy(wmo_hbm, wmo)
            acc_ref[...] = jnp.zeros_like(acc_ref)
            cacc_ref[...] = jnp.zeros_like(cacc_ref)

        dxo = dxo_ref[...]
        acc_ref[2:3, :] += _colsum(mo_ref[...].astype(F32) * dxo)
        dmo = (vec_ref[2:3, :] * dxo).astype(BF16)
        dmo_ref[...] = dmo
        dmg = _nt(dmo, wmo[...])
        sa = sa_ref[...].astype(F32)
        sb = sb_ref[...].astype(F32)
        dya = (sa * dmg).astype(BF16)
        dyb = (sb * dmg).astype(BF16)
        dya_ref[...] = dya
        dyb_ref[...] = dyb
        dp_ref[0] = (dmg * ya_ref[...].astype(F32) * sa * (1.0 - sa)).astype(BF16)
        dp_ref[1] = (dmg * yb_ref[...].astype(F32) * sb * (1.0 - sb)).astype(BF16)
        doa_ref[...] = _nt(dya, who[...]).astype(BF16)
        dus = _nt(dyb, wco[...])
        lg = cvec_ref[1:2, :]
        rstd, z, ul = _ln_fwd(uc_ref[...], lg, cvec_ref[2:3, :])
        dul = dus * _dsilu(ul, _sigmoid(ul))
        cacc_ref[1:2, :] += _colsum(dul * z)
        cacc_ref[2:3, :] += _colsum(dul)
        dz = dul * lg
        duc = rstd * (dz - jnp.mean(dz, axis=-1, keepdims=True) - z * jnp.mean(dz * z, axis=-1, keepdims=True))
        cacc_ref[0:1, :] += _colsum(duc)
        duc_ref[...] = duc

    row = pl.BlockSpec((TB, D), lambda i: (i, 0))
    one = pl.BlockSpec((None, TB, D), lambda i: (0, i, 0))
    vec8 = pl.BlockSpec((8, D), lambda i: (0, 0))
    bf = jax.ShapeDtypeStruct((T, D), BF16)
    bf1 = jax.ShapeDtypeStruct((1, T, D), BF16)
    acc = jax.ShapeDtypeStruct((8, D), F32)
    wv = pltpu.VMEM((D, D), BF16)
    return pl.pallas_call(
        body, name="merge_bwd", grid=(T // TB,),
        in_specs=[row, row, row, row, row, row, row, vec8, vec8, ANY, ANY, ANY],
        out_specs=[one, one, one, row, row, pl.BlockSpec((2, TB, D), lambda i: (0, i, 0)), vec8, vec8],
        out_shape=[bf1, bf1, bf1, bf, jax.ShapeDtypeStruct((T, D), F32), jax.ShapeDtypeStruct((2, T, D), BF16), acc, acc],
        scratch_shapes=[wv, wv, wv],
        compiler_params=_params("arbitrary"),
    )(dxo, mo, ya, yb, sa, sb, uc, vec, cvec, w_ho, w_co, w_mo)


def _pack_rows(parts, total, name, slot=None):
    def body(*refs):
        out = refs[-1]
        out[...] = jnp.zeros_like(out)
        for ref, (_, src, n, dst) in zip(refs[-1 - len(parts):-1], parts):
            out[dst:dst + n, :] = ref[src:src + n, :]

    arrs = [p[0] for p in parts]
    if slot is None:
        return pl.pallas_call(
            body, name=name, in_specs=[pl.BlockSpec(a.shape, lambda: (0, 0)) for a in arrs],
            out_specs=pl.BlockSpec((total, D), lambda: (0, 0)),
            out_shape=jax.ShapeDtypeStruct((total, D), F32),
        )(*arrs)
    return pl.pallas_call(
        body, name=name,
        grid_spec=pltpu.PrefetchScalarGridSpec(
            num_scalar_prefetch=1, grid=(1,),
            in_specs=[pl.BlockSpec(a.shape, lambda i, s: (0, 0)) for a in arrs],
            out_specs=pl.BlockSpec((None, total, D), lambda i, s: (s[0], 0, 0))),
        out_shape=jax.ShapeDtypeStruct((8, total, D), F32),
    )(slot, *arrs)


PACK_ROWS = 56
PACK_AT = {"ada_b": 0, "loss": 9, "norm_ffn1": 10, "norm_mix": 11, "hgrn_g": 12, "conv_b": 13, "conv_ln_g": 14,
           "conv_ln_b": 15, "norm_ffn2": 16, "norm_final": 17, "hgrn_lb": 18, "conv_w": 20}


def _local_step(x, tgt, mod, small, kc, weight, reduce, reduce_small):
    lb = jax.nn.sigmoid(small["hgrn_lb"][0:1] - small["hgrn_lb"][1:2])
    vec1 = _pack_rows([(mod, 0, 3, 0), (small["norm_ffn1"], 0, 1, 3)], 8, "pack_vec1")
    vec2 = _pack_rows([(mod, 3, 3, 0), (small["norm_mix"], 0, 1, 3), (lb, 0, 1, 4), (small["hgrn_g"], 0, 1, 5)],
                      8, "pack_vec2")
    vec3 = _pack_rows([(mod, 6, 3, 0), (small["norm_ffn2"], 0, 1, 3)], 8, "pack_vec3")
    cvec = _pack_rows([(small["conv_b"], 0, 1, 0), (small["conv_ln_g"], 0, 1, 1), (small["conv_ln_b"], 0, 1, 2)],
                      8, "pack_cvec")
    cw = small["conv_w"]
    gvec = _pack_rows([(small["norm_final"], 0, 1, 0)], 8, "pack_gvec")

    wg = {n: weight(n, vec1) for n in ("ffn1_w_in", "ffn1_w_out")}
    x1, h1, a1, b1, s1, f1 = _ffn_fwd(x, vec1, wg["ffn1_w_in"], wg["ffn1_w_out"], "ffn1_fwd")
    wg["mix_w_in"] = weight("mix_w_in", x1)
    h2, qr, g, k, v, og, u, ua, ub, sa, sb = _mix_proj_fwd(x1, vec2, wg["mix_w_in"])
    oa, o, st = _hgrn_fwd(qr, g, k, v, og, vec2)
    us, uc = _conv_fwd(u, cw, cvec)
    wg.update({n: weight(n, us) for n in ("hgrn_w_o", "conv_w_o", "mix_w_out")})
    x2, ya, yb, mg, mo = _merge_fwd(x1, oa, us, sa, sb, vec2, wg["hgrn_w_o"], wg["conv_w_o"], wg["mix_w_out"])
    wg.update({n: weight(n, x2) for n in ("ffn2_w_in", "ffn2_w_out")})
    dx3, h3, a3, b3, s3, f3, acc_head = _ffn_fwd(x2, vec3, wg["ffn2_w_in"], wg["ffn2_w_out"], "ffn2_fwd",
                                                 head=(tgt, gvec))

    dx2, df3, dab3, acc3 = _ffn_bwd(dx3, x2, vec3, a3, b3, f3, wg["ffn2_w_in"], wg["ffn2_w_out"], "ffn2_bwd")
    tok = reduce(("ffn2_w_out", "ffn2_w_in"), [_mm_tn(s3, df3, 1, kc, DFF // NCHIP, "ffn2_dwout"),
                                               _mm_tn(h3, dab3, 1, kc, D, "ffn2_dwin")])
    vec2b = vec2 + tok[0:1, 0:1]
    dmo, dya, dyb, doa, duc, dpc, acc_m, acc_c = _merge_bwd(dx2, mo, ya, yb, sa, sb, uc, vec2b, cvec,
                                                            wg["hgrn_w_o"], wg["conv_w_o"], wg["mix_w_out"])
    tok = reduce(("mix_w_out", "hgrn_w_o", "conv_w_o"),
                 [_mm_tn(mg, dmo, 1, kc, D // NCHIP, "mix_dwout"), _mm_tn(oa, dya, 1, kc, D // NCHIP, "hgrn_dwo"),
                  _mm_tn(us, dyb, 1, kc, D // NCHIP, "conv_dwo")])
    vec2c = vec2 + tok[0:1, 0:1]
    dpb, dcw = _conv_bwd_taps(duc, u, ua, ub, cw)
    dpa, acc_h = _hgrn_bwd(doa, og, qr, g, k, v, o, st, vec2c)
    dx1, acc2 = _mix_proj_bwd(dx2, x1, vec2c, dpa, dpb, dpc, wg["mix_w_in"])
    gmix = _mm_tn(h2, dpa, 2, kc, D, "mix_dwin_a", slabs=NCHIP)
    gmix = _mm_tn(h2, dpb, 2, kc, D, "mix_dwin_b", into=gmix, slab=2, slabs=NCHIP)
    gmix = _mm_tn(h2, dpc, 2, kc, D, "mix_dwin_c", into=gmix, slab=3, slabs=NCHIP)
    tok = reduce(("mix_w_in",), [gmix])
    vec1b = vec1 + tok[0:1, 0:1]
    dx0, df1, dab1, acc1 = _ffn_bwd(dx1, x, vec1b, a1, b1, f1, wg["ffn1_w_in"], wg["ffn1_w_out"], "ffn1_bwd")

    at = PACK_AT
    finish_small = reduce_small([
        (acc1, 0, 3, at["ada_b"]), (acc2, 0, 2, at["ada_b"] + 3), (acc_m, 2, 1, at["ada_b"] + 5),
        (acc3, 0, 3, at["ada_b"] + 6), (acc_head, 1, 1, at["loss"]), (acc1, 3, 1, at["norm_ffn1"]),
        (acc2, 3, 1, at["norm_mix"]), (acc_h, 0, 1, at["hgrn_g"]), (acc_c, 0, 3, at["conv_b"]),
        (acc3, 3, 1, at["norm_ffn2"]), (acc_head, 0, 1, at["norm_final"]), (acc_h, 1, 2, at["hgrn_lb"]),
        (dcw, 0, CONV_K, at["conv_w"])])
    finish_small, tok = finish_small
    last = [_mm_tn(s1, df1, 1, kc, DFF // NCHIP, "ffn1_dwout", after=tok),
            _mm_tn(h1, dab1, 1, kc, D, "ffn1_dwin", after=tok)]
    reduce(("ffn1_w_out", "ffn1_w_in"), last, finish_small(last[1][0]))
    return dx0


BLOCK_BYTES = 3 * 512 * 1024


def _row_block(rows, cols):
    for br in (512, 352, 256, 176, 128, 64, 32, 16, 8):
        if rows % br == 0 and br * cols * 4 <= BLOCK_BYTES:
            return br
    return rows


def _cast_into_slot(w, kc, name, after):
    R, C = w.shape
    br = _row_block(R, C)

    def body(kc_ref, w_ref, after_ref, o_ref):
        o_ref[...] = w_ref[...].astype(BF16)

    return pl.pallas_call(
        body, name=name,
        grid_spec=pltpu.PrefetchScalarGridSpec(
            num_scalar_prefetch=1, grid=(R // br,),
            in_specs=[pl.BlockSpec((br, C), lambda i, kc: (i, 0)), ANY],
            out_specs=pl.BlockSpec((None, br, C), lambda i, kc: (kc[0], i, 0))),
        out_shape=jax.ShapeDtypeStruct((NCHIP, R, C), BF16), compiler_params=_params("parallel"),
    )(kc, w, after)


def _adamw(w, g, m, v, name, after=None, copy_grad=False):
    R, C = w.shape
    br = _row_block(R, C)
    extra = [] if after is None else [after]
    nout = 4 if copy_grad else 3

    def body(w_ref, g_ref, m_ref, v_ref, *rest):
        d_ref, nm_ref, nv_ref = rest[-nout:][:3]
        gv = g_ref[...]
        if copy_grad:
            rest[-1][...] = gv
        nm = ADAM_B1 * m_ref[...] + (1.0 - ADAM_B1) * gv
        nv = ADAM_B2 * v_ref[...] + (1.0 - ADAM_B2) * (gv * gv)
        m_hat = nm / (1.0 - ADAM_B1 ** ADAM_STEP)
        v_hat = nv / (1.0 - ADAM_B2 ** ADAM_STEP)
        d_ref[...] = -ADAM_LR * (m_hat / (jnp.sqrt(v_hat) + ADAM_EPS) + ADAM_WD * w_ref[...])
        nm_ref[...] = nm
        nv_ref[...] = nv

    blk = pl.BlockSpec((br, C), lambda i: (i, 0))
    out = jax.ShapeDtypeStruct((R, C), F32)
    return pl.pallas_call(
        body, name=name, grid=(R // br,), in_specs=[blk] * 4 + [ANY] * len(extra), out_specs=[blk] * nout,
        out_shape=[out] * nout, compiler_params=_params("parallel"),
    )(w, g, m, v, *extra)


def _coords():
    return lax.axis_index("x"), lax.axis_index("y"), lax.axis_index("c")


def _flip(v, bit):
    return 1 - v if bit else v


def _allgather8(v, name):
    R, C = v.shape

    def body(v_ref, out_ref, send_sems, recv_sems, local_sem):
        x, y, c = _coords()
        me = 4 * x + 2 * y + c
        mine = pltpu.make_async_copy(v_ref, out_ref.at[me], local_sem)
        mine.start()

        def copy(m, block):
            peer = (_flip(x, m & 4), _flip(y, m & 2), _flip(c, m & 1))
            return pltpu.make_async_remote_copy(
                src_ref=v_ref, dst_ref=out_ref.at[block], send_sem=send_sems.at[m - 1],
                recv_sem=recv_sems.at[m - 1], device_id=peer, device_id_type=MESH)

        sends = [copy(m, me) for m in range(1, 8)]
        for cp in sends:
            cp.start()
        for m in range(1, 8):
            sender = 4 * _flip(x, m & 4) + 2 * _flip(y, m & 2) + _flip(c, m & 1)
            copy(m, sender).wait_recv()
        for cp in sends:
            cp.wait_send()
        mine.wait()

    vm = pl.BlockSpec(memory_space=pltpu.VMEM)
    return pl.pallas_call(
        body, name=name, in_specs=[vm], out_specs=vm,
        out_shape=jax.ShapeDtypeStruct((8, R, C), F32),
        scratch_shapes=[pltpu.SemaphoreType.DMA((7,)), pltpu.SemaphoreType.DMA((7,)), pltpu.SemaphoreType.DMA],
    )(v)


HBM = pl.BlockSpec(memory_space=pltpu.HBM)
SEM = pl.BlockSpec(memory_space=pltpu.SEMAPHORE)
EFFECT = pltpu.SideEffectType.DATAFLOW_SIDE_EFFECTING


def _peer8(x, y, c, m):
    px, py, pc = _flip(x, m & 4), _flip(y, m & 2), _flip(c, m & 1)
    return (px, py, pc), 4 * px + 2 * py + pc


def _allgather8_start(blocks, name):
    def body(b_ref, send, recv, thru, token):
        x, y, c = _coords()
        me = 4 * x + 2 * y + c
        for m in range(1, 8):
            peer, _ = _peer8(x, y, c, m)
            pltpu.make_async_remote_copy(src_ref=b_ref.at[me], dst_ref=b_ref.at[me], send_sem=send.at[m - 1],
                                         recv_sem=recv.at[m - 1], device_id=peer, device_id_type=MESH).start()
        token[...] = jnp.zeros_like(token)

    sem = pltpu.SemaphoreType.DMA((7,))
    return pl.pallas_call(
        body, name=name,
        out_shape=[sem, sem, pltpu.HBM(blocks.shape, blocks.dtype), jax.ShapeDtypeStruct((8, 128), F32)],
        in_specs=[HBM], out_specs=[SEM, SEM, HBM, pl.BlockSpec(memory_space=pltpu.VMEM)], input_output_aliases={0: 2},
        compiler_params=pltpu.CompilerParams(has_side_effects=EFFECT),
    )(pltpu.with_memory_space_constraint(blocks, pltpu.HBM))


def _allgather8_wait(blocks, send_sem, recv_sem, after, name):
    def body(b_ref, send, recv, after_ref, thru):
        x, y, c = _coords()
        me = 4 * x + 2 * y + c
        for m in range(1, 8):
            peer, sender = _peer8(x, y, c, m)
            cp = pltpu.make_async_remote_copy(src_ref=b_ref.at[me], dst_ref=b_ref.at[sender], send_sem=send.at[m - 1],
                                              recv_sem=recv.at[m - 1], device_id=peer, device_id_type=MESH)
            cp.wait_send()
            cp.wait_recv()

    return pl.pallas_call(
        body, name=name, out_shape=pltpu.HBM(blocks.shape, blocks.dtype),
        in_specs=[HBM, SEM, SEM, ANY], out_specs=HBM, input_output_aliases={0: 0},
        compiler_params=pltpu.CompilerParams(has_side_effects=EFFECT),
    )(blocks, send_sem, recv_sem, after)


def _chip_peer(x, y, m):
    px, py = _flip(x, m & 2), _flip(y, m & 1)
    return px, py, 2 * px + py


def _core_rows(land, c):
    half = land.shape[1] // 2
    return pl.ds(pl.multiple_of(c * half, 16), half)


def _gather_start(lands, groups, halved, after, name):
    n, ng, na = len(lands), len(groups), len(after)

    def body(*refs):
        ins = refs[:n]
        sends, recvs = refs[n + na:n + na + ng], refs[n + na + ng:n + na + 2 * ng]
        token = refs[n + na + 2 * ng + n]
        x, y, c = _coords()
        k = 2 * x + y
        for gi, grp in enumerate(groups):
            for j, t in enumerate(grp):
                mine = ins[t].at[k, _core_rows(ins[t], c), :] if halved[gi] else ins[t].at[k]
                for m in (1, 2, 3):
                    px, py, _ = _chip_peer(x, y, m)
                    pltpu.make_async_remote_copy(
                        src_ref=mine, dst_ref=mine, send_sem=sends[gi].at[3 * j + m - 1],
                        recv_sem=recvs[gi].at[3 * j + m - 1], device_id=(px, py, c), device_id_type=MESH).start()
        token[...] = jnp.zeros_like(token)

    sems = [pltpu.SemaphoreType.DMA((3 * len(g),)) for g in groups]
    out = pl.pallas_call(
        body, name=name,
        out_shape=sems + sems + [pltpu.HBM(a.shape, a.dtype) for a in lands] + [jax.ShapeDtypeStruct((8, 128), F32)],
        in_specs=[HBM] * n + [ANY] * na,
        out_specs=[SEM] * (2 * ng) + [HBM] * n + [pl.BlockSpec(memory_space=pltpu.VMEM)],
        input_output_aliases={t: 2 * ng + t for t in range(n)},
        compiler_params=pltpu.CompilerParams(has_side_effects=EFFECT),
    )(*[pltpu.with_memory_space_constraint(a, pltpu.HBM) for a in lands], *after)
    return out[:ng], out[ng:2 * ng], out[2 * ng:2 * ng + n], out[2 * ng + n]


def _gather_wait(lands, halved, send_sem, recv_sem, after, name):
    n = len(lands)

    def body(*refs):
        ins, send, recv = refs[:n], refs[n], refs[n + 1]
        x, y, c = _coords()
        k = 2 * x + y
        for j in range(n):
            rows = _core_rows(ins[j], c)
            for m in (1, 2, 3):
                px, py, pk = _chip_peer(x, y, m)
                cp = pltpu.make_async_remote_copy(
                    src_ref=ins[j].at[k, rows, :] if halved else ins[j].at[k],
                    dst_ref=ins[j].at[pk, rows, :] if halved else ins[j].at[pk], send_sem=send.at[3 * j + m - 1],
                    recv_sem=recv.at[3 * j + m - 1], device_id=(px, py, c), device_id_type=MESH)
                cp.wait_send()
                cp.wait_recv()

    return pl.pallas_call(
        body, name=name, out_shape=[pltpu.HBM(a.shape, a.dtype) for a in lands],
        in_specs=[HBM] * n + [SEM, SEM, ANY], out_specs=[HBM] * n,
        input_output_aliases={j: j for j in range(n)},
        compiler_params=pltpu.CompilerParams(has_side_effects=EFFECT),
    )(*lands, send_sem, recv_sem, after)


def _sibling_fill(lands, name):
    n = len(lands)

    def body(*refs):
        ins = refs[:n]
        send_sems, recv_sems = refs[2 * n:]
        x, y, c = _coords()
        sends, recvs = [], []
        for t in range(n):
            for m in (1, 2, 3):
                _, _, pk = _chip_peer(x, y, m)
                for rows, lst in ((_core_rows(ins[t], c), sends), (_core_rows(ins[t], 1 - c), recvs)):
                    lst.append(pltpu.make_async_remote_copy(
                        src_ref=ins[t].at[pk, rows, :], dst_ref=ins[t].at[pk, rows, :],
                        send_sem=send_sems.at[3 * t + m - 1], recv_sem=recv_sems.at[3 * t + m - 1],
                        device_id=(x, y, 1 - c), device_id_type=MESH))
        for cp in sends:
            cp.start()
        for cp in recvs:
            cp.wait_recv()
        for cp in sends:
            cp.wait_send()

    return pl.pallas_call(
        body, name=name, in_specs=[ANY] * n, out_specs=[ANY] * n,
        out_shape=[jax.ShapeDtypeStruct(a.shape, a.dtype) for a in lands],
        input_output_aliases={t: t for t in range(n)},
        scratch_shapes=[pltpu.SemaphoreType.DMA((3 * n,)), pltpu.SemaphoreType.DMA((3 * n,))],
    )(*lands)


def _scatter_start(srcs, name, after=()):
    n, na = len(srcs), len(after)

    def body(*refs):
        ins, lands = refs[:n], refs[n:2 * n]
        send, recv = refs[2 * n + na], refs[2 * n + na + 1]
        token = refs[2 * n + na + 2 + 2 * n]
        x, y, c = _coords()
        k = 2 * x + y
        for t in range(n):
            for m in (1, 2, 3):
                px, py, pk = _chip_peer(x, y, m)
                pltpu.make_async_remote_copy(
                    src_ref=ins[t].at[pk], dst_ref=lands[t].at[k], send_sem=send.at[3 * t + m - 1],
                    recv_sem=recv.at[3 * t + m - 1], device_id=(px, py, c), device_id_type=MESH).start()
        token[...] = jnp.zeros_like(token)

    sem = pltpu.SemaphoreType.DMA((3 * n,))
    hbm = [pltpu.HBM(a.shape, a.dtype) for a in srcs]
    operands = list(srcs) + [lax.empty(a.shape, a.dtype) for a in srcs]
    out = pl.pallas_call(
        body, name=name, out_shape=[sem, sem] + hbm + hbm + [jax.ShapeDtypeStruct((8, 128), F32)],
        in_specs=[HBM] * (2 * n) + [ANY] * na,
        out_specs=[SEM, SEM] + [HBM] * (2 * n) + [pl.BlockSpec(memory_space=pltpu.VMEM)],
        input_output_aliases={t: 2 + t for t in range(2 * n)},
        compiler_params=pltpu.CompilerParams(has_side_effects=EFFECT),
    )(*[pltpu.with_memory_space_constraint(a, pltpu.HBM) for a in operands], *after)
    return out[0], out[1], out[2:2 + n], out[2 + n:2 + 2 * n], out[2 + 2 * n]


def _scatter_wait(srcs, lands, send_sem, recv_sem, after, name):
    n = len(srcs)

    def body(*refs):
        ins, land = refs[:n], refs[n:2 * n]
        send, recv = refs[2 * n], refs[2 * n + 1]
        x, y, c = _coords()
        for t in range(n):
            for m in (1, 2, 3):
                px, py, pk = _chip_peer(x, y, m)
                cp = pltpu.make_async_remote_copy(
                    src_ref=ins[t].at[pk], dst_ref=land[t].at[pk], send_sem=send.at[3 * t + m - 1],
                    recv_sem=recv.at[3 * t + m - 1], device_id=(px, py, c), device_id_type=MESH)
                cp.wait_send()
                cp.wait_recv()

    hbm = [pltpu.HBM(a.shape, a.dtype) for a in srcs]
    out = pl.pallas_call(
        body, name=name, out_shape=hbm + hbm, in_specs=[HBM] * (2 * n) + [SEM, SEM, ANY], out_specs=[HBM] * (2 * n),
        input_output_aliases={t: t for t in range(2 * n)},
        compiler_params=pltpu.CompilerParams(has_side_effects=EFFECT),
    )(*srcs, *lands, send_sem, recv_sem, after)
    return out[:n], out[n:]


def _sum_own_half(g, ra, kc, name):
    _, R, C = g.shape
    half = R // 2
    br = _row_block(half, C)
    nb = half // br

    def body(kc_ref, g_ref, ra_ref, o_ref):
        o_ref[...] = (g_ref[...] + ra_ref[...].astype(F32)).astype(BF16)

    return pl.pallas_call(
        body, name=name,
        grid_spec=pltpu.PrefetchScalarGridSpec(
            num_scalar_prefetch=1, grid=(NCHIP, nb),
            in_specs=[pl.BlockSpec((None, br, C), lambda j, i, kc: (j, kc[1] * nb + i, 0)),
                      pl.BlockSpec((None, br, C), lambda j, i, kc: (j, i, 0))],
            out_specs=pl.BlockSpec((None, br, C), lambda j, i, kc: (j, i, 0))),
        out_shape=jax.ShapeDtypeStruct((NCHIP, half, C), BF16),
        compiler_params=_params("parallel", "parallel"),
    )(kc, g, ra)


def _sum_chips(sa, rb, kc, name, after=None):
    _, half, C = rb.shape
    br = _row_block(half, C)
    nb = half // br
    extra = [] if after is None else [after]

    def body(kc_ref, own_ref, r1_ref, r2_ref, r3_ref, *rest):
        out, obuf, local_sems, send_sems, recv_sem = rest[-5:]
        i = pl.program_id(0)
        slot = i % 2
        x, y, c = _coords()

        def copies(i_, slot_):
            rows = out.at[pl.ds(pl.multiple_of((c * nb + i_) * br, 8), br), :]
            return (pltpu.make_async_copy(obuf.at[slot_], rows, local_sems.at[slot_]),
                    pltpu.make_async_remote_copy(src_ref=obuf.at[slot_], dst_ref=rows, send_sem=send_sems.at[slot_],
                                                 recv_sem=recv_sem, device_id=(x, y, 1 - c), device_id_type=MESH))

        @pl.when(i >= 2)
        def _():
            here, there = copies(i, slot)
            here.wait()
            there.wait_send()

        acc = own_ref[...].astype(F32) + r1_ref[...].astype(F32)
        obuf[slot] = (acc + r2_ref[...].astype(F32)) + r3_ref[...].astype(F32)
        here, there = copies(i, slot)
        here.start()
        there.start()

        @pl.when(i == nb - 1)
        def _():
            for s in range(min(2, nb)):
                here, there = copies(i, (i - s) % 2)
                here.wait()
                there.wait_send()
            theirs = out.at[pl.ds(pl.multiple_of((1 - c) * half, 8), half), :]
            pltpu.make_async_remote_copy(src_ref=theirs, dst_ref=theirs, send_sem=send_sems.at[0], recv_sem=recv_sem,
                                         device_id=(x, y, 1 - c), device_id_type=MESH).wait_recv()

    def slab(m):
        return pl.BlockSpec((None, br, C), lambda i, kc: (kc[0] ^ m, i, 0))

    return pl.pallas_call(
        body, name=name,
        grid_spec=pltpu.PrefetchScalarGridSpec(
            num_scalar_prefetch=1, grid=(nb,),
            in_specs=[slab(0), slab(1), slab(2), slab(3)] + [ANY] * len(extra),
            out_specs=ANY,
            scratch_shapes=[pltpu.VMEM((2, br, C), F32), pltpu.SemaphoreType.DMA((2,)), pltpu.SemaphoreType.DMA((2,)),
                            pltpu.SemaphoreType.DMA]),
        out_shape=jax.ShapeDtypeStruct((2 * half, C), F32), compiler_params=_params("arbitrary"),
    )(kc, sa, rb, rb, rb, *extra)


def _sum8(ga, name):
    _, R, C = ga.shape

    def body(g_ref, o_ref):
        acc = g_ref[0]
        for j in range(1, 8):
            acc = acc + g_ref[j]
        o_ref[...] = acc

    return pl.pallas_call(
        body, name=name, in_specs=[pl.BlockSpec((8, R, C), lambda: (0, 0, 0))],
        out_specs=pl.BlockSpec((R, C), lambda: (0, 0)), out_shape=jax.ShapeDtypeStruct((R, C), F32),
    )(ga)


ADA_COLS = 9 * D // NCHIP
ADA_BLK = 256


def _ada_mod(c_all, ada_w, ada_b, kidx):
    def body(k_ref, c_ref, w_ref, b_ref, o_ref):
        cv = c_ref[...]
        cs = cv * _sigmoid(cv)
        o_ref[...] = jnp.dot(cs, w_ref[...], precision=lax.Precision.HIGHEST,
                             preferred_element_type=F32) + b_ref[...]

    nblk = ADA_COLS // ADA_BLK
    return pl.pallas_call(
        body, name="ada_mod",
        grid_spec=pltpu.PrefetchScalarGridSpec(
            num_scalar_prefetch=1, grid=(nblk,),
            in_specs=[pl.BlockSpec((8, D), lambda j, k: (0, 0)),
                      pl.BlockSpec((D, ADA_BLK), lambda j, k: (0, j)),
                      pl.BlockSpec((1, ADA_BLK), lambda j, k: (0, k[0] * nblk + j))],
            out_specs=pl.BlockSpec((8, ADA_BLK), lambda j, k: (0, j))),
        out_shape=jax.ShapeDtypeStruct((8, ADA_COLS), F32),
        compiler_params=_params("parallel"),
    )(kidx, c_all, ada_w, ada_b)


def _ada_grad(c_all_t, dmod_all, kidx):
    def body(k_ref, ct_ref, dm_ref, o_ref):
        cv = ct_ref[...]
        cs = cv * _sigmoid(cv)
        acc = cs[:, 0:1] * dm_ref[0:1, :]
        for b in range(1, 8):
            acc = acc + cs[:, b:b + 1] * dm_ref[b:b + 1, :]
        o_ref[...] = acc

    nblk = ADA_COLS // ADA_BLK
    return pl.pallas_call(
        body, name="ada_grad",
        grid_spec=pltpu.PrefetchScalarGridSpec(
            num_scalar_prefetch=1, grid=(nblk,),
            in_specs=[pl.BlockSpec((D, 8), lambda j, k: (0, 0)),
                      pl.BlockSpec((8, ADA_BLK), lambda j, k: (0, k[0] * nblk + j))],
            out_specs=pl.BlockSpec((D, ADA_BLK), lambda j, k: (0, j))),
        out_shape=jax.ShapeDtypeStruct((D, ADA_COLS), F32),
        compiler_params=_params("parallel"),
    )(kidx, c_all_t, dmod_all)


BIG = ("ffn1_w_in", "ffn1_w_out", "mix_w_in", "hgrn_w_o", "conv_w_o", "mix_w_out", "ffn2_w_in", "ffn2_w_out")
ROW_SHARDED = ("ffn1_w_out", "hgrn_w_o", "conv_w_o", "mix_w_out", "ffn2_w_out")
GATHER_GROUPS = ((0, 1), (2,), (3, 4, 5), (6, 7))
GATHER_HALVED = (True, True, False, False)
GATHER_STARTS = ((0, 1), (2, 3))
PACK_LEN = {"ada_b": 9, "hgrn_lb": 2}
WEIGHTS = ("ada_w", "ada_b", "norm_ffn1", "ffn1_w_in", "ffn1_w_out", "norm_mix", "mix_w_in", "hgrn_lb", "hgrn_g",
           "hgrn_w_o", "conv_w", "conv_b", "conv_ln_g", "conv_ln_b", "conv_w_o", "mix_w_out", "norm_ffn2",
           "ffn2_w_in", "ffn2_w_out", "norm_final")
PACKED = ("ada_b", "norm_ffn1", "norm_mix", "hgrn_g", "conv_b", "conv_ln_g", "conv_ln_b", "norm_ffn2",
          "norm_final", "hgrn_lb")


def _pack_params(p, name):
    parts = [(p[n].reshape(PACK_LEN.get(n, 1), D), 0, PACK_LEN.get(n, 1), PACK_AT[n]) for n in PACKED]
    return _pack_rows(parts, PACK_ROWS, name)


def _step(w, m, v, x, c, tgt):
    xi, yi, ci = _coords()
    kidx = (2 * xi + yi).astype(jnp.int32).reshape(1)
    kc = jnp.stack([2 * xi + yi, ci]).astype(jnp.int32)
    me = 4 * xi + 2 * yi + ci

    cq = D // NCHIP
    first = jnp.zeros((40, cq), F32).at[0:CONV_K].set(w["conv_w"][0]).at[32:36].set(c.reshape(NCHIP, cq))
    first_all = _allgather8(first, "gather_c_conv_w")
    c_all = first_all[:, 32:36, :].reshape(8, D)
    mod_cols = _ada_mod(c_all, w["ada_w"][0], w["ada_b"], kidx)
    mod_all = _allgather8(mod_cols, "gather_mod")
    mod = lax.dynamic_slice(mod_all, (0, me, 0), (8, 1, ADA_COLS))[::2].reshape(9, D)
    small = {n: w[n].reshape(-1, D) for n in ("norm_ffn1", "norm_mix", "hgrn_lb", "hgrn_g", "conv_b", "conv_ln_g",
                                              "conv_ln_b", "norm_ffn2", "norm_final")}
    small["conv_w"] = jnp.concatenate([first_all[2 * j, 0:32, :] for j in range(NCHIP)], axis=1)

    lands, sends, recvs = [], [], []
    after = mod
    for part in GATHER_STARTS:
        tensors = [t for gi in part for t in GATHER_GROUPS[gi]]
        cast = [_cast_into_slot(w[BIG[t]][0], kc, "cast_" + BIG[t], after) for t in tensors]
        groups = [tuple(tensors.index(t) for t in GATHER_GROUPS[gi]) for gi in part]
        s, r, thru, after = _gather_start(cast, groups, [GATHER_HALVED[gi] for gi in part], [after],
                                          "gather_weights_start%d" % part[0])
        lands, sends, recvs = lands + list(thru), sends + list(s), recvs + list(r)
    started_all = after
    ready = {}

    def weight(name, after):
        t = BIG.index(name)
        if t not in ready:
            gi = [t in grp for grp in GATHER_GROUPS].index(True)
            grp = GATHER_GROUPS[gi]
            outs = _gather_wait([lands[j] for j in grp], GATHER_HALVED[gi], sends[gi], recvs[gi],
                                started_all if gi == 0 else after, "gather_weights_wait%d" % gi)
            if GATHER_HALVED[gi]:
                outs = _sibling_fill(outs, "gather_weights_fill%d" % gi)
            ready.update(zip(grp, outs))
        return ready[t].reshape(-1, D) if name in ROW_SHARDED else ready[t]

    grads, delta, new_m, new_v = {}, {}, {}, {}
    flight = []
    landed = []

    def settle(after):
        names, sa, rb, send, recv = flight.pop()
        sa, rb = _scatter_wait(sa, rb, send, recv, after, "rs_chip_wait_" + names[0])
        landed.append((names, sa, rb))

    def reduce(names, pairs, after=None):
        gs = [g.reshape(NCHIP, -1, g.shape[-1]) for g, _ in pairs]
        ra = [r.reshape(NCHIP, -1, r.shape[-1]) for _, r in pairs]
        sa = [_sum_own_half(g, r, kc, "rs_sum_pair_" + n) for g, r, n in zip(gs, ra, names)]
        if flight:
            settle(sa[0])
        send, recv, sa, rb, tok = _scatter_start(sa, "rs_chip_start_" + names[0], () if after is None else (after,))
        flight.append((names, sa, rb, send, recv))
        started.append(tok)
        return tok

    def adamw(n, after=None):
        shape = w[n].shape
        two = (shape[-2], shape[-1])
        out = _adamw(w[n].reshape(two), grads[n], m[n].reshape(two), v[n].reshape(two), "adamw_" + n, after,
                     copy_grad=n in BIG)
        g_ = out[3] if n in BIG else grads[n]
        grads[n], delta[n], new_m[n], new_v[n] = (a.reshape(shape) for a in (g_, out[0], out[1], out[2]))
        return out[1]

    def finish(after=None):
        names, sa, rb = landed.pop(0)
        full = [_sum_chips(s, r, kc, "rs_sum_chips_" + n, after) for s, r, n in zip(sa, rb, names)]
        grads.update(zip(names, full))
        return [adamw(n) for n in names][-1]

    started = []

    smalls = []

    def reduce_small(parts):
        blocks = _pack_rows(parts, PACK_ROWS, "pack_small_grads", slot=me.astype(jnp.int32).reshape(1))
        send, recv, blocks, tok = _allgather8_start(blocks, "gather_small_grads_start")

        def finish(after):
            packed_all = _allgather8_wait(blocks, send, recv, after, "gather_small_grads_wait")
            smalls.extend([packed_all, _sum8(packed_all, "sum_small_grads")])
            return smalls[1]

        return finish, tok

    dx = _local_step(x[0], tgt[0], mod, small, kc, weight, reduce, reduce_small)
    packed_all, gsum = smalls
    loss = (0.5 / D) * jnp.sum(gsum[PACK_AT["loss"]])
    dmod_all = packed_all[:, 0:9, :].reshape(8, 9 * D)
    grads["ada_w"] = _ada_grad(c_all.T, dmod_all, kidx)
    grads["conv_w"] = lax.dynamic_slice(gsum, (PACK_AT["conv_w"], kidx[0] * (D // NCHIP)), (CONV_K, D // NCHIP))

    tok = started[-1]
    adamw("ada_w", tok)
    adamw("conv_w")
    pw, pm, pv = (_pack_params(p, "pack_" + s) for p, s in ((w, "w"), (m, "m"), (v, "v")))
    pd, pnm, pnv = _adamw(pw, gsum, pm, pv, "adamw_small", tok)
    last = pnv
    while landed:
        last = finish(tok)
    settle(last)
    finish()
    for n in PACKED:
        rows = slice(PACK_AT[n], PACK_AT[n] + PACK_LEN.get(n, 1))
        for dst, src in ((grads, gsum), (delta, pd), (new_m, pnm), (new_v, pnv)):
            dst[n] = src[rows].reshape(w[n].shape)

    outs = [loss, dx[None]]
    for d in (grads, delta, new_m, new_v):
        outs += [d[n] for n in WEIGHTS]
    return tuple(outs)


def kernel(x, c, ada_w, ada_b, norm_ffn1, ffn1_w_in, ffn1_w_out, norm_mix, mix_w_in, hgrn_lb, hgrn_g, hgrn_w_o, conv_w, conv_b, conv_ln_g, conv_ln_b, conv_w_o, mix_w_out, norm_ffn2, ffn2_w_in, ffn2_w_out, norm_final, loss_target, m_ada_w, m_ada_b, m_norm_ffn1, m_ffn1_w_in, m_ffn1_w_out, m_norm_mix, m_mix_w_in, m_hgrn_lb, m_hgrn_g, m_hgrn_w_o, m_conv_w, m_conv_b, m_conv_ln_g, m_conv_ln_b, m_conv_w_o, m_mix_w_out, m_norm_ffn2, m_ffn2_w_in, m_ffn2_w_out, m_norm_final, v_ada_w, v_ada_b, v_norm_ffn1, v_ffn1_w_in, v_ffn1_w_out, v_norm_mix, v_mix_w_in, v_hgrn_lb, v_hgrn_g, v_hgrn_w_o, v_conv_w, v_conv_b, v_conv_ln_g, v_conv_ln_b, v_conv_w_o, v_mix_w_out, v_norm_ffn2, v_ffn2_w_in, v_ffn2_w_out, v_norm_final):
    w = dict(ada_w=ada_w, ada_b=ada_b, norm_ffn1=norm_ffn1, ffn1_w_in=ffn1_w_in, ffn1_w_out=ffn1_w_out,
             norm_mix=norm_mix, mix_w_in=mix_w_in, hgrn_lb=hgrn_lb, hgrn_g=hgrn_g, hgrn_w_o=hgrn_w_o, conv_w=conv_w,
             conv_b=conv_b, conv_ln_g=conv_ln_g, conv_ln_b=conv_ln_b, conv_w_o=conv_w_o, mix_w_out=mix_w_out,
             norm_ffn2=norm_ffn2, ffn2_w_in=ffn2_w_in, ffn2_w_out=ffn2_w_out, norm_final=norm_final)
    m = dict(ada_w=m_ada_w, ada_b=m_ada_b, norm_ffn1=m_norm_ffn1, ffn1_w_in=m_ffn1_w_in, ffn1_w_out=m_ffn1_w_out,
             norm_mix=m_norm_mix, mix_w_in=m_mix_w_in, hgrn_lb=m_hgrn_lb, hgrn_g=m_hgrn_g, hgrn_w_o=m_hgrn_w_o,
             conv_w=m_conv_w, conv_b=m_conv_b, conv_ln_g=m_conv_ln_g, conv_ln_b=m_conv_ln_b, conv_w_o=m_conv_w_o,
             mix_w_out=m_mix_w_out, norm_ffn2=m_norm_ffn2, ffn2_w_in=m_ffn2_w_in, ffn2_w_out=m_ffn2_w_out,
             norm_final=m_norm_final)
    v = dict(ada_w=v_ada_w, ada_b=v_ada_b, norm_ffn1=v_norm_ffn1, ffn1_w_in=v_ffn1_w_in, ffn1_w_out=v_ffn1_w_out,
             norm_mix=v_norm_mix, mix_w_in=v_mix_w_in, hgrn_lb=v_hgrn_lb, hgrn_g=v_hgrn_g, hgrn_w_o=v_hgrn_w_o,
             conv_w=v_conv_w, conv_b=v_conv_b, conv_ln_g=v_conv_ln_g, conv_ln_b=v_conv_ln_b, conv_w_o=v_conv_w_o,
             mix_w_out=v_mix_w_out, norm_ffn2=v_norm_ffn2, ffn2_w_in=v_ffn2_w_in, ffn2_w_out=v_ffn2_w_out,
             norm_final=v_norm_final)
    return _step(w, m, v, x, c, loss_target)
```

```python
import jax
import jax.numpy as jnp
from jax import lax
from jax.experimental import pallas as pl
from jax.experimental.pallas import tpu as pltpu

F32 = jnp.float32
BF16 = jnp.bfloat16

D = 1024
DFF = 2816
NCHIP = 4
FSH = 2 * DFF // NCHIP
HEADS = 8
DK = 128
CHUNK = 64
CONV_K = 31
HALO = 32
EPS = 1e-6
TB = 256
CB = 1024
DW_TOKENS = 2048
VMEM_LIMIT = 56 * 1024 * 1024

ADAM_LR = 0.001
ADAM_B1 = 0.9
ADAM_B2 = 0.999
ADAM_EPS = 1e-08
ADAM_WD = 0.01
ADAM_STEP = 10

MESH = pl.DeviceIdType.MESH
ANY = pl.BlockSpec(memory_space=pl.ANY)


def _params(*sem):
    return pltpu.CompilerParams(dimension_semantics=sem, vmem_limit_bytes=VMEM_LIMIT)


def _sigmoid(x):
    return 0.5 * jnp.tanh(0.5 * x) + 0.5


def _dsilu(x, sg):
    return sg * (1.0 + x * (1.0 - sg))


def _nt(a, b):
    return lax.dot_general(a, b, (((1,), (1,)), ((), ())), preferred_element_type=F32)


def _tn(a, b):
    return lax.dot_general(a, b, (((0,), (0,)), ((), ())), preferred_element_type=F32)


def _nn(a, b):
    return jnp.dot(a, b, preferred_element_type=F32)


def _colsum(x):
    return jnp.sum(x, axis=0, keepdims=True)


def _rms_fwd(x, gn, sc, sh):
    r = lax.rsqrt(jnp.mean(x * x, axis=-1, keepdims=True) + EPS)
    n = x * r
    h = (n * gn) * (1.0 + sc) + sh
    return r, n, h


def _rms_bwd(dh, r, n, gn, sc, acc_ref):
    acc_ref[0:1, :] += _colsum(dh)
    acc_ref[1:2, :] += _colsum(dh * (n * gn))
    dng = dh * (1.0 + sc)
    acc_ref[3:4, :] += _colsum(dng * n)
    dn = dng * gn
    return r * (dn - n * jnp.mean(dn * n, axis=-1, keepdims=True))


def _loss_head(x, tgt, gf, acc_ref):
    r = lax.rsqrt(jnp.mean(x * x, axis=-1, keepdims=True) + EPS)
    n = x * r
    err = n * gf - tgt
    acc_ref[1:2, :] += _colsum(err * err)
    dy = err * (1.0 / D)
    acc_ref[0:1, :] += _colsum(dy * n)
    dn = dy * gf
    return r * (dn - n * jnp.mean(dn * n, axis=-1, keepdims=True))


def _ffn_fwd(x, vec, w_in, w_out, name, head=None):
    T = x.shape[0]
    nh = 0 if head is None else 2

    def body(x_ref, vec_ref, *rest):
        win_hbm, wout_hbm = rest[nh:nh + 2]
        xo_ref, h_ref, a_ref, b_ref, s_ref, f_ref = rest[nh + 2:nh + 8]
        win, wout = rest[-2:]

        @pl.when(pl.program_id(0) == 0)
        def _():
            pltpu.sync_copy(win_hbm, win)
            pltpu.sync_copy(wout_hbm, wout)
            if head is not None:
                rest[nh + 8][...] = jnp.zeros((8, D), F32)

        x = x_ref[...]
        sh, sc, gate, gn = vec_ref[0:1, :], vec_ref[1:2, :], vec_ref[2:3, :], vec_ref[3:4, :]
        _, _, h = _rms_fwd(x, gn, sc, sh)
        hb = h.astype(BF16)
        h_ref[...] = hb
        f = jnp.zeros((TB, D), F32)
        for j in range(2):
            cols = slice(j * FSH, (j + 1) * FSH)
            a = _nn(hb, win[j])
            b = _nn(hb, win[2 + j])
            s = (a * _sigmoid(a) * b).astype(BF16)
            a_ref[:, cols] = a.astype(BF16)
            b_ref[:, cols] = b.astype(BF16)
            s_ref[:, cols] = s
            f = f + _nn(s, wout[cols, :])
        xo = x + (0.5 * gate) * f
        f_ref[...] = f.astype(BF16)
        if head is None:
            xo_ref[...] = xo
        else:
            xo_ref[...] = _loss_head(xo, rest[0][...], rest[1][0:1, :], rest[nh + 8])

    row = lambda w: pl.BlockSpec((TB, w), lambda i: (i, 0))
    vec8 = pl.BlockSpec((8, D), lambda i: (0, 0))
    acc = [] if head is None else [jax.ShapeDtypeStruct((8, D), F32)]
    return pl.pallas_call(
        body, name=name, grid=(T // TB,),
        in_specs=[row(D), vec8] + ([] if head is None else [row(D), vec8]) + [ANY, ANY],
        out_specs=[row(D), row(D), row(DFF), row(DFF), row(DFF), row(D)] + [vec8] * len(acc),
        out_shape=[jax.ShapeDtypeStruct((T, D), F32), jax.ShapeDtypeStruct((T, D), BF16),
                   jax.ShapeDtypeStruct((T, DFF), BF16), jax.ShapeDtypeStruct((T, DFF), BF16),
                   jax.ShapeDtypeStruct((T, DFF), BF16), jax.ShapeDtypeStruct((T, D), BF16)] + acc,
        scratch_shapes=[pltpu.VMEM((NCHIP, D, FSH), BF16), pltpu.VMEM((DFF, D), BF16)],
        compiler_params=_params("arbitrary"),
    )(x, vec, *([] if head is None else list(head)), w_in, w_out)


def _ffn_bwd(dxo, x, vec, a, b, f, w_in, w_out, name):
    T = x.shape[0]

    def body(dxo_ref, x_ref, vec_ref, a_ref, b_ref, f_ref, win_hbm, wout_hbm,
             dx_ref, df_ref, dab_ref, acc_ref, win, wout):
        @pl.when(pl.program_id(0) == 0)
        def _():
            pltpu.sync_copy(win_hbm, win)
            pltpu.sync_copy(wout_hbm, wout)
            acc_ref[...] = jnp.zeros_like(acc_ref)

        dxo = dxo_ref[...]
        x = x_ref[...]
        sh, sc, gate, gn = vec_ref[0:1, :], vec_ref[1:2, :], vec_ref[2:3, :], vec_ref[3:4, :]
        r, n, _ = _rms_fwd(x, gn, sc, sh)
        acc_ref[2:3, :] += _colsum(0.5 * f_ref[...].astype(F32) * dxo)
        dfb = ((0.5 * gate) * dxo).astype(BF16)
        df_ref[...] = dfb
        dh = jnp.zeros((TB, D), F32)
        for j in range(2):
            cols = slice(j * FSH, (j + 1) * FSH)
            ds = _nt(dfb, wout[cols, :])
            av = a_ref[:, cols].astype(F32)
            bv = b_ref[:, cols].astype(F32)
            sg = _sigmoid(av)
            da = (ds * bv * _dsilu(av, sg)).astype(BF16)
            db = (ds * (av * sg)).astype(BF16)
            dab_ref[j] = da
            dab_ref[2 + j] = db
            dh = dh + _nt(da, win[j]) + _nt(db, win[2 + j])
        dx_ref[...] = dxo + _rms_bwd(dh, r, n, gn, sc, acc_ref)

    row = lambda w: pl.BlockSpec((TB, w), lambda i: (i, 0))
    vec8 = pl.BlockSpec((8, D), lambda i: (0, 0))
    return pl.pallas_call(
        body, name=name, grid=(T // TB,),
        in_specs=[row(D), row(D), vec8, row(DFF), row(DFF), row(D), ANY, ANY],
        out_specs=[row(D), pl.BlockSpec((None, TB, D), lambda i: (0, i, 0)),
                   pl.BlockSpec((NCHIP, TB, FSH), lambda i: (0, i, 0)), vec8],
        out_shape=[jax.ShapeDtypeStruct((T, D), F32), jax.ShapeDtypeStruct((1, T, D), BF16),
                   jax.ShapeDtypeStruct((NCHIP, T, FSH), BF16), jax.ShapeDtypeStruct((8, D), F32)],
        scratch_shapes=[pltpu.VMEM((NCHIP, D, FSH), BF16), pltpu.VMEM((DFF, D), BF16)],
        compiler_params=_params("arbitrary"),
    )(dxo, x, vec, a, b, f, w_in, w_out)


def _mm_tn(a, b3, hp, kc, shard_rows, name, into=None, slab=0, slabs=None, after=None):
    T, M = a.shape
    P, _, N = b3.shape
    tm = M if M <= 1408 else M // 2
    tk = min(T, DW_TOKENS if P * (M // tm) > 1 else DW_TOKENS // 2)
    nk = T // tk
    ni = M // tm
    slabs = P // hp if slabs is None else slabs
    half = shard_rows // 2
    extra = ([] if into is None else list(into)) + ([] if after is None else [after])

    def body(kc_ref, a_ref, b_ref, *rest):
        o_ref, ra_ref, hbuf, send_sems, recv_sem = rest[-5:]
        p, i, k = pl.program_id(0), pl.program_id(1), pl.program_id(2)
        x, y, c = _coords()
        step = p * ni + i
        slot = step % 2

        def send(p_, i_, slot_):
            dst = ra_ref.at[slab + p_ // hp, pl.ds(pl.multiple_of(i_ * (tm // 2), 8), tm // 2),
                            pl.ds(pl.multiple_of((p_ % hp) * N, LANES), N)]
            return pltpu.make_async_remote_copy(
                src_ref=hbuf.at[slot_], dst_ref=dst, send_sem=send_sems.at[slot_], recv_sem=recv_sem,
                device_id=(x, y, 1 - c), device_id_type=MESH)

        @pl.when(k == 0)
        def _():
            o_ref[...] = jnp.zeros_like(o_ref)

        o_ref[...] += _tn(a_ref[...], b_ref[...])

        @pl.when(k == nk - 1)
        def _():
            @pl.when(step >= 2)
            def _():
                send(p, i, slot).wait_send()

            for j in range(tm // shard_rows):
                start = pl.multiple_of(j * shard_rows + (1 - kc_ref[1]) * half, 8)
                hbuf[slot, j * half:(j + 1) * half, :] = o_ref[pl.ds(start, half), :].astype(BF16)
            send(p, i, slot).start()

        @pl.when((step == P * ni - 1) & (k == nk - 1))
        def _():
            for s in range(min(2, P * ni)):
                send(p, i, (step - s) % 2).wait_send()
            mine = ra_ref.at[slab:slab + P // hp]
            pltpu.make_async_remote_copy(src_ref=mine, dst_ref=mine, send_sem=send_sems.at[0], recv_sem=recv_sem,
                                         device_id=(x, y, 1 - c), device_id_type=MESH).wait_recv()

    return pl.pallas_call(
        body, name=name,
        grid_spec=pltpu.PrefetchScalarGridSpec(
            num_scalar_prefetch=1, grid=(P, ni, nk),
            in_specs=[pl.BlockSpec((tk, tm), lambda p, i, k, kc: (k, i)),
                      pl.BlockSpec((None, tk, N), lambda p, i, k, kc: (p, k, 0))] + [ANY] * len(extra),
            out_specs=[pl.BlockSpec((None, tm, N), lambda p, i, k, kc: (slab + p // hp, i, p % hp)), ANY],
            scratch_shapes=[pltpu.VMEM((2, tm // 2, N), BF16), pltpu.SemaphoreType.DMA((2,)),
                            pltpu.SemaphoreType.DMA]),
        out_shape=[jax.ShapeDtypeStruct((slabs, M, hp * N), F32), jax.ShapeDtypeStruct((slabs, M // 2, hp * N), BF16)],
        input_output_aliases={} if into is None else {3: 0, 4: 1},
        compiler_params=_params("arbitrary", "arbitrary", "arbitrary"),
    )(kc, a, b3, *extra)


def _mix_proj_fwd(x, vec, w_in):
    T = x.shape[0]

    def body(x_ref, vec_ref, w_hbm, h_ref, qr_ref, g_ref, k_ref, v_ref, og_ref, u_ref, ua_ref, ub_ref,
             sa_ref, sb_ref, w):
        @pl.when(pl.program_id(0) == 0)
        def _():
            pltpu.sync_copy(w_hbm, w)

        x = x_ref[...]
        sh, sc, gn, lb = vec_ref[0:1, :], vec_ref[1:2, :], vec_ref[3:4, :], vec_ref[4:5, :]
        _, _, h = _rms_fwd(x, gn, sc, sh)
        hb = h.astype(BF16)
        h_ref[...] = hb
        p = _nn(hb, w[0])
        qr_ref[...] = p[:, :D].astype(BF16)
        fg = lb + (1.0 - lb) * _sigmoid(p[:, D:])
        g_ref[...] = jnp.log(fg)
        k_ref[...] = (1.0 - fg).astype(BF16)
        p = _nn(hb, w[1])
        v_ref[...] = p[:, :D].astype(BF16)
        og_ref[...] = p[:, D:].astype(BF16)
        p = _nn(hb, w[2])
        ua, ub = p[:, :D], p[:, D:]
        u_ref[...] = ua * _sigmoid(ub)
        ua_ref[...] = ua.astype(BF16)
        ub_ref[...] = ub.astype(BF16)
        p = _nn(hb, w[3])
        sa_ref[...] = _sigmoid(p[:, :D]).astype(BF16)
        sb_ref[...] = _sigmoid(p[:, D:]).astype(BF16)

    row = pl.BlockSpec((TB, D), lambda i: (i, 0))
    bf = jax.ShapeDtypeStruct((T, D), BF16)
    f32 = jax.ShapeDtypeStruct((T, D), F32)
    return pl.pallas_call(
        body, name="mix_proj_fwd", grid=(T // TB,),
        in_specs=[row, pl.BlockSpec((8, D), lambda i: (0, 0)), ANY],
        out_specs=[row] * 11,
        out_shape=[bf, bf, f32, bf, bf, bf, f32, bf, bf, bf, bf],
        scratch_shapes=[pltpu.VMEM((NCHIP, D, 2 * D), BF16)],
        compiler_params=_params("arbitrary"),
    )(x, vec, w_in)


def _mix_proj_bwd(dxo, x, vec, dpa, dpb, dpc, w_in):
    T = x.shape[0]

    def body(dxo_ref, x_ref, vec_ref, dpa_ref, dpb_ref, dpc_ref, w_hbm, dx_ref, acc_ref, w):
        @pl.when(pl.program_id(0) == 0)
        def _():
            pltpu.sync_copy(w_hbm, w)
            acc_ref[...] = jnp.zeros_like(acc_ref)

        x = x_ref[...]
        sh, sc, gn = vec_ref[0:1, :], vec_ref[1:2, :], vec_ref[3:4, :]
        r, n, _ = _rms_fwd(x, gn, sc, sh)
        dh = jnp.zeros((TB, D), F32)
        for p in range(8):
            src = dpa_ref[p] if p < 4 else (dpb_ref[p - 4] if p < 6 else dpc_ref[p - 6])
            dh = dh + _nt(src, w[p // 2, :, (p % 2) * D:(p % 2 + 1) * D])
        dx_ref[...] = dxo_ref[...] + _rms_bwd(dh, r, n, gn, sc, acc_ref)

    row = pl.BlockSpec((TB, D), lambda i: (i, 0))
    vec8 = pl.BlockSpec((8, D), lambda i: (0, 0))
    stack = lambda k: pl.BlockSpec((k, TB, D), lambda i: (0, i, 0))
    return pl.pallas_call(
        body, name="mix_proj_bwd", grid=(T // TB,),
        in_specs=[row, row, vec8, stack(4), stack(2), stack(2), ANY],
        out_specs=[row, vec8],
        out_shape=[jax.ShapeDtypeStruct((T, D), F32), jax.ShapeDtypeStruct((8, D), F32)],
        scratch_shapes=[pltpu.VMEM((NCHIP, D, 2 * D), BF16)],
        compiler_params=_params("arbitrary"),
    )(dxo, x, vec, dpa, dpb, dpc, w_in)


def _tri(lower):
    r = lax.broadcasted_iota(jnp.int32, (CHUNK, CHUNK), 0)
    c = lax.broadcasted_iota(jnp.int32, (CHUNK, CHUNK), 1)
    return (c <= r) if lower else (c >= r)


def _cumsum_rows(mask, g):
    hi = g.astype(BF16)
    rest = g - hi.astype(F32)
    mid = rest.astype(BF16)
    low = (rest - mid.astype(F32)).astype(BF16)
    n = g.shape[1]
    p = _nn(mask.astype(BF16), jnp.concatenate([hi, mid, low], axis=1))
    return (p[:, 2 * n:] + p[:, n:2 * n]) + p[:, :n]


def _chunk_decay(low, g, nck):
    bs, mids, lasts = [], [], []
    for c in range(nck):
        gc = g[c * CHUNK:(c + 1) * CHUNK]
        bs.append(_cumsum_rows(low, gc))
        mids.append(_colsum(gc[0:CHUNK // 2]))
        lasts.append(_colsum(gc))
    spread = lambda rows: jnp.concatenate([jnp.broadcast_to(r, (CHUNK, DK)) for r in rows], axis=0)
    return jnp.concatenate(bs, axis=0), spread(mids), spread(lasts), lasts


def _hgrn_fwd(qr, g, k, v, og, vec):
    T = qr.shape[0]
    nck = CB // CHUNK

    def body(qr_ref, g_ref, k_ref, v_ref, og_ref, vec_ref, out_ref, o_ref, st_ref, state):
        @pl.when(pl.program_id(1) == 0)
        def _():
            state[...] = jnp.zeros_like(state)

        low = _tri(True)
        qv = qr_ref[...].astype(F32)
        q = qv * _sigmoid(qv) * (DK ** -0.5)
        kk = k_ref[...].astype(F32)
        vb = v_ref[...]
        b, mid, last, lasts = _chunk_decay(low, g_ref[...], nck)
        qt = (q * jnp.exp(b - mid)).astype(BF16)
        kt = (kk * jnp.exp(mid - b)).astype(BF16)
        qe = (q * jnp.exp(b)).astype(BF16)
        kd = (kk * jnp.exp(last - b)).astype(BF16)
        intra, grow = [], []
        for c in range(nck):
            r = slice(c * CHUNK, (c + 1) * CHUNK)
            att = jnp.where(low, _nt(qt[r], kt[r]), 0.0).astype(BF16)
            intra.append(_nn(att, vb[r]))
            grow.append(_tn(vb[r], kd[r]))
        st = state[...]
        inter = []
        for c in range(nck):
            stb = st.astype(BF16)
            st_ref[c] = stb
            inter.append(_nt(qe[c * CHUNK:(c + 1) * CHUNK], stb))
            st = st * jnp.exp(lasts[c]) + grow[c]
        state[...] = st
        o = jnp.concatenate(intra, axis=0) + jnp.concatenate(inter, axis=0)
        o_ref[...] = o
        ogv = og_ref[...].astype(F32)
        rms = lax.rsqrt(jnp.mean(o * o, axis=-1, keepdims=True) + EPS)
        out_ref[...] = (o * rms * vec_ref[5:6, :] * (ogv * _sigmoid(ogv))).astype(BF16)

    blk = pl.BlockSpec((CB, DK), lambda h, i: (i, h))
    return pl.pallas_call(
        body, name="hgrn_fwd", grid=(HEADS, T // CB),
        in_specs=[blk, blk, blk, blk, blk, pl.BlockSpec((8, DK), lambda h, i: (0, h))],
        out_specs=[blk, blk, pl.BlockSpec((None, nck, DK, DK), lambda h, i: (h, i, 0, 0))],
        out_shape=[jax.ShapeDtypeStruct((T, D), BF16), jax.ShapeDtypeStruct((T, D), F32),
                   jax.ShapeDtypeStruct((HEADS, T // CHUNK, DK, DK), BF16)],
        scratch_shapes=[pltpu.VMEM((DK, DK), F32)],
        compiler_params=_params("parallel", "arbitrary"),
    )(qr, g, k, v, og, vec)


def _hgrn_bwd(dout, og, qr, g, k, v, o, st, vec):
    T = qr.shape[0]
    nck = CB // CHUNK
    nb = T // CB

    def body(dout_ref, og_ref, qr_ref, g_ref, k_ref, v_ref, o_ref, st_ref, vec_ref,
             dp_ref, acc_ref, dstate):
        @pl.when(pl.program_id(1) == 0)
        def _():
            dstate[...] = jnp.zeros_like(dstate)
            acc_ref[...] = jnp.zeros_like(acc_ref)

        o = o_ref[...]
        ogv = og_ref[...].astype(F32)
        dout = dout_ref[...].astype(F32)
        hg = vec_ref[5:6, :]
        sgo = _sigmoid(ogv)
        rms = lax.rsqrt(jnp.mean(o * o, axis=-1, keepdims=True) + EPS)
        ohat = o * rms
        dp_ref[3] = (dout * (ohat * hg) * _dsilu(ogv, sgo)).astype(BF16)
        don = dout * (ogv * sgo)
        acc_ref[0:1, :] += _colsum(don * ohat)
        dohat = don * hg
        dob = (rms * (dohat - ohat * jnp.mean(dohat * ohat, axis=-1, keepdims=True))).astype(BF16)

        low = _tri(True)
        upp = _tri(False)
        lb = vec_ref[4:5, :]
        qv = qr_ref[...].astype(F32)
        sgq = _sigmoid(qv)
        q = qv * sgq * (DK ** -0.5)
        kk = k_ref[...].astype(F32)
        vb = v_ref[...]
        gv = g_ref[...]
        b, mid, last, lasts = _chunk_decay(low, gv, nck)
        eq = jnp.exp(b - mid)
        ek = jnp.exp(mid - b)
        eb = jnp.exp(b)
        ed = jnp.exp(last - b)
        qtb, ktb, qeb, kdb = ((t).astype(BF16) for t in (q * eq, kk * ek, q * eb, kk * ed))
        rows = [slice(c * CHUNK, (c + 1) * CHUNK) for c in range(nck)]

        dv1, dqt, dkt, dqe, grow = [], [], [], [], []
        for c, r in enumerate(rows):
            att = jnp.where(low, _nt(qtb[r], ktb[r]), 0.0).astype(BF16)
            datt = jnp.where(low, _nt(dob[r], vb[r]), 0.0).astype(BF16)
            dv1.append(_tn(att, dob[r]))
            dqt.append(_nn(datt, ktb[r]))
            dkt.append(_tn(datt, qtb[r]))
            dqe.append(_nn(dob[r], st_ref[c]))
            grow.append(_tn(dob[r], qeb[r]))
        ds = dstate[...]
        ds1b, dl_state = [None] * nck, [None] * nck
        for c in reversed(range(nck)):
            el = jnp.exp(lasts[c])
            ds1b[c] = ds.astype(BF16)
            dl_state[c] = el * _colsum(ds * st_ref[c].astype(F32))
            ds = ds * el + grow[c]
        dstate[...] = ds
        dkd = jnp.concatenate([_nn(vb[r], ds1b[c]) for c, r in enumerate(rows)], axis=0)
        dv = jnp.concatenate(dv1, axis=0) + jnp.concatenate([_nt(kdb[r], ds1b[c]) for c, r in enumerate(rows)], axis=0)
        dqt, dkt, dqe = (jnp.concatenate(t, axis=0) for t in (dqt, dkt, dqe))
        dq = dqt * eq + dqe * eb
        dk = dkt * ek + dkd * ed
        dkdkd = dkd * kdb.astype(F32)
        db = dqt * qtb.astype(F32) - dkt * ktb.astype(F32) + dqe * qeb.astype(F32) - dkdkd
        dg = jnp.concatenate([_cumsum_rows(upp, db[r]) + (_colsum(dkdkd[r]) + dl_state[c])
                              for c, r in enumerate(rows)], axis=0)
        fg = jnp.exp(gv)
        dfg = dg * jnp.exp(-gv) - dk
        one_m_sig = (1.0 - fg) * (1.0 / (1.0 - lb))
        dp_ref[0] = (dq * (DK ** -0.5) * _dsilu(qv, sgq)).astype(BF16)
        dp_ref[1] = (dfg * (fg - lb) * one_m_sig).astype(BF16)
        dp_ref[2] = dv.astype(BF16)
        dlb = _colsum(dfg * one_m_sig) * (lb * (1.0 - lb))
        acc_ref[1:2, :] += dlb
        acc_ref[2:3, :] -= dlb

    blk = pl.BlockSpec((CB, DK), lambda h, i: (nb - 1 - i, h))
    return pl.pallas_call(
        body, name="hgrn_bwd", grid=(HEADS, nb),
        in_specs=[blk, blk, blk, blk, blk, blk, blk,
                  pl.BlockSpec((None, nck, DK, DK), lambda h, i: (h, nb - 1 - i, 0, 0)),
                  pl.BlockSpec((8, DK), lambda h, i: (0, h))],
        out_specs=[pl.BlockSpec((4, CB, DK), lambda h, i: (0, nb - 1 - i, h)),
                   pl.BlockSpec((8, DK), lambda h, i: (0, h))],
        out_shape=[jax.ShapeDtypeStruct((4, T, D), BF16), jax.ShapeDtypeStruct((8, D), F32)],
        scratch_shapes=[pltpu.VMEM((DK, DK), F32)],
        compiler_params=_params("parallel", "arbitrary"),
    )(dout, og, qr, g, k, v, o, st, vec)


def _ln_fwd(uc, lg, lbias):
    mu = jnp.mean(uc, axis=-1, keepdims=True)
    xc = uc - mu
    rstd = lax.rsqrt(jnp.mean(xc * xc, axis=-1, keepdims=True) + EPS)
    z = xc * rstd
    return rstd, z, z * lg + lbias


LANES = 128
SUBLANES = 8
CONV_ROWS = 64


def _lane_tiles():
    return [slice(l * LANES, (l + 1) * LANES) for l in range(D // LANES)]


def _row_shifts(x):
    n = x.shape[0]
    return [x] + [pltpu.roll(x, n - r, axis=0) for r in range(1, SUBLANES)]


TAPS_PAST = tuple(HALO - (CONV_K - 1) + j for j in range(CONV_K))
TAPS_AHEAD = tuple(CONV_K - 1 - j for j in range(CONV_K))


def _tap_windows(shifted, starts, r0, rows):
    for r in range(SUBLANES):
        taps = [(j, s // SUBLANES) for j, s in enumerate(starts) if s % SUBLANES == r]
        if not taps:
            continue
        lo = min(a for _, a in taps)
        hi = max(a for _, a in taps)
        span = shifted[r][r0 + lo * SUBLANES:r0 + hi * SUBLANES + rows]
        for j, a in taps:
            yield j, span[(a - lo) * SUBLANES:(a - lo) * SUBLANES + rows]


def _conv_fwd(u, cw, cvec):
    T = u.shape[0]
    per = TB // HALO

    def body(u_ref, halo_ref, cw_ref, cvec_ref, us_ref, uc_ref, pad):
        i = pl.program_id(0)
        pad[0:HALO, :] = jnp.where(i > 0, halo_ref[...], 0.0)
        pad[HALO:, :] = u_ref[...]
        for lanes in _lane_tiles():
            shifted = _row_shifts(pad[:, lanes])
            taps = cw_ref[:, lanes]
            for r0 in range(0, TB, CONV_ROWS):
                acc = jnp.broadcast_to(cvec_ref[0:1, lanes], (CONV_ROWS, LANES))
                for j, window in _tap_windows(shifted, TAPS_PAST, r0, CONV_ROWS):
                    acc = acc + taps[j:j + 1] * window
                uc_ref[r0:r0 + CONV_ROWS, lanes] = acc
        _, _, ul = _ln_fwd(uc_ref[...], cvec_ref[1:2, :], cvec_ref[2:3, :])
        us_ref[...] = (ul * _sigmoid(ul)).astype(BF16)

    row = pl.BlockSpec((TB, D), lambda i: (i, 0))
    return pl.pallas_call(
        body, name="conv_fwd", grid=(T // TB,),
        in_specs=[row, pl.BlockSpec((HALO, D), lambda i: (jnp.maximum(i * per - 1, 0), 0)),
                  pl.BlockSpec((32, D), lambda i: (0, 0)), pl.BlockSpec((8, D), lambda i: (0, 0))],
        out_specs=[row, row],
        out_shape=[jax.ShapeDtypeStruct((T, D), BF16), jax.ShapeDtypeStruct((T, D), F32)],
        scratch_shapes=[pltpu.VMEM((TB + HALO, D), F32)],
        compiler_params=_params("parallel"),
    )(u, u, cw, cvec)


def _conv_bwd_taps(duc, u, ua, ub, cw):
    T = u.shape[0]
    per = TB // HALO
    nblk = T // TB

    def body(duc_ref, dnext_ref, u_ref, uprev_ref, ua_ref, ub_ref, cw_ref, dp_ref, dcw_ref, upad, dpad, dcw):
        i = pl.program_id(0)

        @pl.when(i == 0)
        def _():
            dcw[...] = jnp.zeros_like(dcw)

        upad[0:HALO, :] = jnp.where(i > 0, uprev_ref[...], 0.0)
        upad[HALO:, :] = u_ref[...]
        dpad[0:TB, :] = duc_ref[...]
        dpad[TB:, :] = jnp.where(i < nblk - 1, dnext_ref[...], 0.0)
        for lanes in _lane_tiles():
            ushift = _row_shifts(upad[:, lanes])
            dshift = _row_shifts(dpad[:, lanes])
            for r0 in range(0, TB, CONV_ROWS):
                rows = slice(r0, r0 + CONV_ROWS)
                duc = duc_ref[rows, lanes]
                for j, window in _tap_windows(ushift, TAPS_PAST, r0, CONV_ROWS):
                    prod = duc * window
                    dcw[j, :, lanes] += jnp.sum(prod.reshape(CONV_ROWS // SUBLANES, SUBLANES, LANES), axis=0)
                du = jnp.zeros((CONV_ROWS, LANES), F32)
                for j, window in _tap_windows(dshift, TAPS_AHEAD, r0, CONV_ROWS):
                    du = du + cw_ref[j:j + 1, lanes] * window
                ua = ua_ref[rows, lanes].astype(F32)
                sg = _sigmoid(ub_ref[rows, lanes].astype(F32))
                dp_ref[0, rows, lanes] = (du * sg).astype(BF16)
                dp_ref[1, rows, lanes] = (du * ua * sg * (1.0 - sg)).astype(BF16)

        @pl.when(i == nblk - 1)
        def _():
            dcw_ref[...] = jnp.sum(dcw[...], axis=1)

    row = pl.BlockSpec((TB, D), lambda i: (i, 0))
    return pl.pallas_call(
        body, name="conv_bwd_taps", grid=(nblk,),
        in_specs=[row, pl.BlockSpec((HALO, D), lambda i: (jnp.minimum((i + 1) * per, T // HALO - 1), 0)),
                  row, pl.BlockSpec((HALO, D), lambda i: (jnp.maximum(i * per - 1, 0), 0)),
                  row, row, pl.BlockSpec((32, D), lambda i: (0, 0))],
        out_specs=[pl.BlockSpec((2, TB, D), lambda i: (0, i, 0)), pl.BlockSpec((32, D), lambda i: (0, 0))],
        out_shape=[jax.ShapeDtypeStruct((2, T, D), BF16), jax.ShapeDtypeStruct((32, D), F32)],
        scratch_shapes=[pltpu.VMEM((TB + HALO, D), F32), pltpu.VMEM((TB + HALO, D), F32),
                        pltpu.VMEM((32, SUBLANES, D), F32)],
        compiler_params=_params("arbitrary"),
    )(duc, duc, u, u, ua, ub, cw)


def _merge_fwd(x, oa, us, sa, sb, vec, w_ho, w_co, w_mo):
    T = x.shape[0]

    def body(x_ref, oa_ref, us_ref, sa_ref, sb_ref, vec_ref, who_hbm, wco_hbm, wmo_hbm,
             xo_ref, ya_ref, yb_ref, mg_ref, mo_ref, who, wco, wmo):
        @pl.when(pl.program_id(0) == 0)
        def _():
            pltpu.sync_copy(who_hbm, who)
            pltpu.sync_copy(wco_hbm, wco)
            pltpu.sync_copy(wmo_hbm, wmo)

        ya = _nn(oa_ref[...], who[...])
        yb = _nn(us_ref[...], wco[...])
        mg = (sa_ref[...].astype(F32) * ya + sb_ref[...].astype(F32) * yb).astype(BF16)
        mo = _nn(mg, wmo[...])
        xo_ref[...] = x_ref[...] + vec_ref[2:3, :] * mo
        ya_ref[...] = ya.astype(BF16)
        yb_ref[...] = yb.astype(BF16)
        mg_ref[...] = mg
        mo_ref[...] = mo.astype(BF16)

    row = pl.BlockSpec((TB, D), lambda i: (i, 0))
    bf = jax.ShapeDtypeStruct((T, D), BF16)
    wv = pltpu.VMEM((D, D), BF16)
    return pl.pallas_call(
        body, name="merge_fwd", grid=(T // TB,),
        in_specs=[row, row, row, row, row, pl.BlockSpec((8, D), lambda i: (0, 0)), ANY, ANY, ANY],
        out_specs=[row] * 5,
        out_shape=[jax.ShapeDtypeStruct((T, D), F32), bf, bf, bf, bf],
        scratch_shapes=[wv, wv, wv],
        compiler_params=_params("arbitrary"),
    )(x, oa, us, sa, sb, vec, w_ho, w_co, w_mo)


def _merge_bwd(dxo, mo, ya, yb, sa, sb, uc, vec, cvec, w_ho, w_co, w_mo):
    T = dxo.shape[0]

    def body(dxo_ref, mo_ref, ya_ref, yb_ref, sa_ref, sb_ref, uc_ref, vec_ref, cvec_ref, who_hbm, wco_hbm, wmo_hbm,
             dmo_ref, dya_ref, dyb_ref, doa_ref, duc_ref, dp_ref, acc_ref, cacc_ref, who, wco, wmo):
        @pl.when(pl.program_id(0) == 0)
        def _():
            pltpu.sync_copy(who_hbm, who)
            pltpu.sync_copy(wco_hbm, wco)
            pltpu.sync_copy(wmo_hbm, wmo)
            acc_ref[...] = jnp.zeros_like(acc_ref)
            cacc_ref[...] = jnp.zeros_like(cacc_ref)

        dxo = dxo_ref[...]
        acc_ref[2:3, :] += _colsum(mo_ref[...].astype(F32) * dxo)
        dmo = (vec_ref[2:3, :] * dxo).astype(BF16)
        dmo_ref[...] = dmo
        dmg = _nt(dmo, wmo[...])
        sa = sa_ref[...].astype(F32)
        sb = sb_ref[...].astype(F32)
        dya = (sa * dmg).astype(BF16)
        dyb = (sb * dmg).astype(BF16)
        dya_ref[...] = dya
        dyb_ref[...] = dyb
        dp_ref[0] = (dmg * ya_ref[...].astype(F32) * sa * (1.0 - sa)).astype(BF16)
        dp_ref[1] = (dmg * yb_ref[...].astype(F32) * sb * (1.0 - sb)).astype(BF16)
        doa_ref[...] = _nt(dya, who[...]).astype(BF16)
        dus = _nt(dyb, wco[...])
        lg = cvec_ref[1:2, :]
        rstd, z, ul = _ln_fwd(uc_ref[...], lg, cvec_ref[2:3, :])
        dul = dus * _dsilu(ul, _sigmoid(ul))
        cacc_ref[1:2, :] += _colsum(dul * z)
        cacc_ref[2:3, :] += _colsum(dul)
        dz = dul * lg
        duc = rstd * (dz - jnp.mean(dz, axis=-1, keepdims=True) - z * jnp.mean(dz * z, axis=-1, keepdims=True))
        cacc_ref[0:1, :] += _colsum(duc)
        duc_ref[...] = duc

    row = pl.BlockSpec((TB, D), lambda i: (i, 0))
    one = pl.BlockSpec((None, TB, D), lambda i: (0, i, 0))
    vec8 = pl.BlockSpec((8, D), lambda i: (0, 0))
    bf = jax.ShapeDtypeStruct((T, D), BF16)
    bf1 = jax.ShapeDtypeStruct((1, T, D), BF16)
    acc = jax.ShapeDtypeStruct((8, D), F32)
    wv = pltpu.VMEM((D, D), BF16)
    return pl.pallas_call(
        body, name="merge_bwd", grid=(T // TB,),
        in_specs=[row, row, row, row, row, row, row, vec8, vec8, ANY, ANY, ANY],
        out_specs=[one, one, one, row, row, pl.BlockSpec((2, TB, D), lambda i: (0, i, 0)), vec8, vec8],
        out_shape=[bf1, bf1, bf1, bf, jax.ShapeDtypeStruct((T, D), F32), jax.ShapeDtypeStruct((2, T, D), BF16), acc, acc],
        scratch_shapes=[wv, wv, wv],
        compiler_params=_params("arbitrary"),
    )(dxo, mo, ya, yb, sa, sb, uc, vec, cvec, w_ho, w_co, w_mo)


def _pack_rows(parts, total, name, slot=None):
    def body(*refs):
        out = refs[-1]
        out[...] = jnp.zeros_like(out)
        for ref, (_, src, n, dst) in zip(refs[-1 - len(parts):-1], parts):
            out[dst:dst + n, :] = ref[src:src + n, :]

    arrs = [p[0] for p in parts]
    if slot is None:
        return pl.pallas_call(
            body, name=name, in_specs=[pl.BlockSpec(a.shape, lambda: (0, 0)) for a in arrs],
            out_specs=pl.BlockSpec((total, D), lambda: (0, 0)),
            out_shape=jax.ShapeDtypeStruct((total, D), F32),
        )(*arrs)
    return pl.pallas_call(
        body, name=name,
        grid_spec=pltpu.PrefetchScalarGridSpec(
            num_scalar_prefetch=1, grid=(1,),
            in_specs=[pl.BlockSpec(a.shape, lambda i, s: (0, 0)) for a in arrs],
            out_specs=pl.BlockSpec((None, total, D), lambda i, s: (s[0], 0, 0))),
        out_shape=jax.ShapeDtypeStruct((8, total, D), F32),
    )(slot, *arrs)


PACK_ROWS = 56
PACK_AT = {"ada_b": 0, "loss": 9, "norm_ffn1": 10, "norm_mix": 11, "hgrn_g": 12, "conv_b": 13, "conv_ln_g": 14,
           "conv_ln_b": 15, "norm_ffn2": 16, "norm_final": 17, "hgrn_lb": 18, "conv_w": 20}


def _local_step(x, tgt, mod, small, kc, weight, reduce, reduce_small):
    lb = jax.nn.sigmoid(small["hgrn_lb"][0:1] - small["hgrn_lb"][1:2])
    vec1 = _pack_rows([(mod, 0, 3, 0), (small["norm_ffn1"], 0, 1, 3)], 8, "pack_vec1")
    vec2 = _pack_rows([(mod, 3, 3, 0), (small["norm_mix"], 0, 1, 3), (lb, 0, 1, 4), (small["hgrn_g"], 0, 1, 5)],
                      8, "pack_vec2")
    vec3 = _pack_rows([(mod, 6, 3, 0), (small["norm_ffn2"], 0, 1, 3)], 8, "pack_vec3")
    cvec = _pack_rows([(small["conv_b"], 0, 1, 0), (small["conv_ln_g"], 0, 1, 1), (small["conv_ln_b"], 0, 1, 2)],
                      8, "pack_cvec")
    cw = small["conv_w"]
    gvec = _pack_rows([(small["norm_final"], 0, 1, 0)], 8, "pack_gvec")

    wg = {n: weight(n, vec1) for n in ("ffn1_w_in", "ffn1_w_out")}
    x1, h1, a1, b1, s1, f1 = _ffn_fwd(x, vec1, wg["ffn1_w_in"], wg["ffn1_w_out"], "ffn1_fwd")
    wg["mix_w_in"] = weight("mix_w_in", x1)
    h2, qr, g, k, v, og, u, ua, ub, sa, sb = _mix_proj_fwd(x1, vec2, wg["mix_w_in"])
    oa, o, st = _hgrn_fwd(qr, g, k, v, og, vec2)
    us, uc = _conv_fwd(u, cw, cvec)
    wg.update({n: weight(n, us) for n in ("hgrn_w_o", "conv_w_o", "mix_w_out")})
    x2, ya, yb, mg, mo = _merge_fwd(x1, oa, us, sa, sb, vec2, wg["hgrn_w_o"], wg["conv_w_o"], wg["mix_w_out"])
    wg.update({n: weight(n, x2) for n in ("ffn2_w_in", "ffn2_w_out")})
    dx3, h3, a3, b3, s3, f3, acc_head = _ffn_fwd(x2, vec3, wg["ffn2_w_in"], wg["ffn2_w_out"], "ffn2_fwd",
                                                 head=(tgt, gvec))

    dx2, df3, dab3, acc3 = _ffn_bwd(dx3, x2, vec3, a3, b3, f3, wg["ffn2_w_in"], wg["ffn2_w_out"], "ffn2_bwd")
    tok = reduce(("ffn2_w_out", "ffn2_w_in"), [_mm_tn(s3, df3, 1, kc, DFF // NCHIP, "ffn2_dwout"),
                                               _mm_tn(h3, dab3, 1, kc, D, "ffn2_dwin")])
    vec2b = vec2 + tok[0:1, 0:1]
    dmo, dya, dyb, doa, duc, dpc, acc_m, acc_c = _merge_bwd(dx2, mo, ya, yb, sa, sb, uc, vec2b, cvec,
                                                            wg["hgrn_w_o"], wg["conv_w_o"], wg["mix_w_out"])
    tok = reduce(("mix_w_out", "hgrn_w_o", "conv_w_o"),
                 [_mm_tn(mg, dmo, 1, kc, D // NCHIP, "mix_dwout"), _mm_tn(oa, dya, 1, kc, D // NCHIP, "hgrn_dwo"),
                  _mm_tn(us, dyb, 1, kc, D // NCHIP, "conv_dwo")])
    vec2c = vec2 + tok[0:1, 0:1]
    dpb, dcw = _conv_bwd_taps(duc, u, ua, ub, cw)
    dpa, acc_h = _hgrn_bwd(doa, og, qr, g, k, v, o, st, vec2c)
    dx1, acc2 = _mix_proj_bwd(dx2, x1, vec2c, dpa, dpb, dpc, wg["mix_w_in"])
    gmix = _mm_tn(h2, dpa, 2, kc, D, "mix_dwin_a", slabs=NCHIP)
    gmix = _mm_tn(h2, dpb, 2, kc, D, "mix_dwin_b", into=gmix, slab=2, slabs=NCHIP)
    gmix = _mm_tn(h2, dpc, 2, kc, D, "mix_dwin_c", into=gmix, slab=3, slabs=NCHIP)
    tok = reduce(("mix_w_in",), [gmix])
    vec1b = vec1 + tok[0:1, 0:1]
    dx0, df1, dab1, acc1 = _ffn_bwd(dx1, x, vec1b, a1, b1, f1, wg["ffn1_w_in"], wg["ffn1_w_out"], "ffn1_bwd")

    at = PACK_AT
    finish_small = reduce_small([
        (acc1, 0, 3, at["ada_b"]), (acc2, 0, 2, at["ada_b"] + 3), (acc_m, 2, 1, at["ada_b"] + 5),
        (acc3, 0, 3, at["ada_b"] + 6), (acc_head, 1, 1, at["loss"]), (acc1, 3, 1, at["norm_ffn1"]),
        (acc2, 3, 1, at["norm_mix"]), (acc_h, 0, 1, at["hgrn_g"]), (acc_c, 0, 3, at["conv_b"]),
        (acc3, 3, 1, at["norm_ffn2"]), (acc_head, 0, 1, at["norm_final"]), (acc_h, 1, 2, at["hgrn_lb"]),
        (dcw, 0, CONV_K, at["conv_w"])])
    finish_small, tok = finish_small
    last = [_mm_tn(s1, df1, 1, kc, DFF // NCHIP, "ffn1_dwout", after=tok),
            _mm_tn(h1, dab1, 1, kc, D, "ffn1_dwin", after=tok)]
    reduce(("ffn1_w_out", "ffn1_w_in"), last, finish_small(last[1][0]))
    return dx0


BLOCK_BYTES = 3 * 512 * 1024


def _row_block(rows, cols):
    for br in (512, 352, 256, 176, 128, 64, 32, 16, 8):
        if rows % br == 0 and br * cols * 4 <= BLOCK_BYTES:
            return br
    return rows


def _cast_into_slot(w, kc, name, after):
    R, C = w.shape
    br = _row_block(R, C)

    def body(kc_ref, w_ref, after_ref, o_ref):
        o_ref[...] = w_ref[...].astype(BF16)

    return pl.pallas_call(
        body, name=name,
        grid_spec=pltpu.PrefetchScalarGridSpec(
            num_scalar_prefetch=1, grid=(R // br,),
            in_specs=[pl.BlockSpec((br, C), lambda i, kc: (i, 0)), ANY],
            out_specs=pl.BlockSpec((None, br, C), lambda i, kc: (kc[0], i, 0))),
        out_shape=jax.ShapeDtypeStruct((NCHIP, R, C), BF16), compiler_params=_params("parallel"),
    )(kc, w, after)


def _adamw(w, g, m, v, name, after=None, copy_grad=False):
    R, C = w.shape
    br = _row_block(R, C)
    extra = [] if after is None else [after]
    nout = 4 if copy_grad else 3

    def body(w_ref, g_ref, m_ref, v_ref, *rest):
        d_ref, nm_ref, nv_ref = rest[-nout:][:3]
        gv = g_ref[...]
        if copy_grad:
            rest[-1][...] = gv
        nm = ADAM_B1 * m_ref[...] + (1.0 - ADAM_B1) * gv
        nv = ADAM_B2 * v_ref[...] + (1.0 - ADAM_B2) * (gv * gv)
        m_hat = nm / (1.0 - ADAM_B1 ** ADAM_STEP)
        v_hat = nv / (1.0 - ADAM_B2 ** ADAM_STEP)
        d_ref[...] = -ADAM_LR * (m_hat / (jnp.sqrt(v_hat) + ADAM_EPS) + ADAM_WD * w_ref[...])
        nm_ref[...] = nm
        nv_ref[...] = nv

    blk = pl.BlockSpec((br, C), lambda i: (i, 0))
    out = jax.ShapeDtypeStruct((R, C), F32)
    return pl.pallas_call(
        body, name=name, grid=(R // br,), in_specs=[blk] * 4 + [ANY] * len(extra), out_specs=[blk] * nout,
        out_shape=[out] * nout, compiler_params=_params("parallel"),
    )(w, g, m, v, *extra)


def _coords():
    return lax.axis_index("x"), lax.axis_index("y"), lax.axis_index("c")


def _flip(v, bit):
    return 1 - v if bit else v


def _allgather8(v, name):
    R, C = v.shape

    def body(v_ref, out_ref, send_sems, recv_sems, local_sem):
        x, y, c = _coords()
        me = 4 * x + 2 * y + c
        mine = pltpu.make_async_copy(v_ref, out_ref.at[me], local_sem)
        mine.start()

        def copy(m, block):
            peer = (_flip(x, m & 4), _flip(y, m & 2), _flip(c, m & 1))
            return pltpu.make_async_remote_copy(
                src_ref=v_ref, dst_ref=out_ref.at[block], send_sem=send_sems.at[m - 1],
                recv_sem=recv_sems.at[m - 1], device_id=peer, device_id_type=MESH)

        sends = [copy(m, me) for m in range(1, 8)]
        for cp in sends:
            cp.start()
        for m in range(1, 8):
            sender = 4 * _flip(x, m & 4) + 2 * _flip(y, m & 2) + _flip(c, m & 1)
            copy(m, sender).wait_recv()
        for cp in sends:
            cp.wait_send()
        mine.wait()

    vm = pl.BlockSpec(memory_space=pltpu.VMEM)
    return pl.pallas_call(
        body, name=name, in_specs=[vm], out_specs=vm,
        out_shape=jax.ShapeDtypeStruct((8, R, C), F32),
        scratch_shapes=[pltpu.SemaphoreType.DMA((7,)), pltpu.SemaphoreType.DMA((7,)), pltpu.SemaphoreType.DMA],
    )(v)


HBM = pl.BlockSpec(memory_space=pltpu.HBM)
SEM = pl.BlockSpec(memory_space=pltpu.SEMAPHORE)
EFFECT = pltpu.SideEffectType.DATAFLOW_SIDE_EFFECTING


def _peer8(x, y, c, m):
    px, py, pc = _flip(x, m & 4), _flip(y, m & 2), _flip(c, m & 1)
    return (px, py, pc), 4 * px + 2 * py + pc


def _allgather8_start(blocks, name):
    def body(b_ref, send, recv, thru, token):
        x, y, c = _coords()
        me = 4 * x + 2 * y + c
        for m in range(1, 8):
            peer, _ = _peer8(x, y, c, m)
            pltpu.make_async_remote_copy(src_ref=b_ref.at[me], dst_ref=b_ref.at[me], send_sem=send.at[m - 1],
                                         recv_sem=recv.at[m - 1], device_id=peer, device_id_type=MESH).start()
        token[...] = jnp.zeros_like(token)

    sem = pltpu.SemaphoreType.DMA((7,))
    return pl.pallas_call(
        body, name=name,
        out_shape=[sem, sem, pltpu.HBM(blocks.shape, blocks.dtype), jax.ShapeDtypeStruct((8, 128), F32)],
        in_specs=[HBM], out_specs=[SEM, SEM, HBM, pl.BlockSpec(memory_space=pltpu.VMEM)], input_output_aliases={0: 2},
        compiler_params=pltpu.CompilerParams(has_side_effects=EFFECT),
    )(pltpu.with_memory_space_constraint(blocks, pltpu.HBM))


def _allgather8_wait(blocks, send_sem, recv_sem, after, name):
    def body(b_ref, send, recv, after_ref, thru):
        x, y, c = _coords()
        me = 4 * x + 2 * y + c
        for m in range(1, 8):
            peer, sender = _peer8(x, y, c, m)
            cp = pltpu.make_async_remote_copy(src_ref=b_ref.at[me], dst_ref=b_ref.at[sender], send_sem=send.at[m - 1],
                                              recv_sem=recv.at[m - 1], device_id=peer, device_id_type=MESH)
            cp.wait_send()
            cp.wait_recv()

    return pl.pallas_call(
        body, name=name, out_shape=pltpu.HBM(blocks.shape, blocks.dtype),
        in_specs=[HBM, SEM, SEM, ANY], out_specs=HBM, input_output_aliases={0: 0},
        compiler_params=pltpu.CompilerParams(has_side_effects=EFFECT),
    )(blocks, send_sem, recv_sem, after)


def _chip_peer(x, y, m):
    px, py = _flip(x, m & 2), _flip(y, m & 1)
    return px, py, 2 * px + py


def _core_rows(land, c):
    half = land.shape[1] // 2
    return pl.ds(pl.multiple_of(c * half, 16), half)


def _gather_start(lands, groups, halved, after, name):
    n, ng, na = len(lands), len(groups), len(after)

    def body(*refs):
        ins = refs[:n]
        sends, recvs = refs[n + na:n + na + ng], refs[n + na + ng:n + na + 2 * ng]
        token = refs[n + na + 2 * ng + n]
        x, y, c = _coords()
        k = 2 * x + y
        for gi, grp in enumerate(groups):
            for j, t in enumerate(grp):
                mine = ins[t].at[k, _core_rows(ins[t], c), :] if halved[gi] else ins[t].at[k]
                for m in (1, 2, 3):
                    px, py, _ = _chip_peer(x, y, m)
                    pltpu.make_async_remote_copy(
                        src_ref=mine, dst_ref=mine, send_sem=sends[gi].at[3 * j + m - 1],
                        recv_sem=recvs[gi].at[3 * j + m - 1], device_id=(px, py, c), device_id_type=MESH).start()
        token[...] = jnp.zeros_like(token)

    sems = [pltpu.SemaphoreType.DMA((3 * len(g),)) for g in groups]
    out = pl.pallas_call(
        body, name=name,
        out_shape=sems + sems + [pltpu.HBM(a.shape, a.dtype) for a in lands] + [jax.ShapeDtypeStruct((8, 128), F32)],
        in_specs=[HBM] * n + [ANY] * na,
        out_specs=[SEM] * (2 * ng) + [HBM] * n + [pl.BlockSpec(memory_space=pltpu.VMEM)],
        input_output_aliases={t: 2 * ng + t for t in range(n)},
        compiler_params=pltpu.CompilerParams(has_side_effects=EFFECT),
    )(*[pltpu.with_memory_space_constraint(a, pltpu.HBM) for a in lands], *after)
    return out[:ng], out[ng:2 * ng], out[2 * ng:2 * ng + n], out[2 * ng + n]


def _gather_wait(lands, halved, send_sem, recv_sem, after, name):
    n = len(lands)

    def body(*refs):
        ins, send, recv = refs[:n], refs[n], refs[n + 1]
        x, y, c = _coords()
        k = 2 * x + y
        for j in range(n):
            rows = _core_rows(ins[j], c)
            for m in (1, 2, 3):
                px, py, pk = _chip_peer(x, y, m)
                cp = pltpu.make_async_remote_copy(
                    src_ref=ins[j].at[k, rows, :] if halved else ins[j].at[k],
                    dst_ref=ins[j].at[pk, rows, :] if halved else ins[j].at[pk], send_sem=send.at[3 * j + m - 1],
                    recv_sem=recv.at[3 * j + m - 1], device_id=(px, py, c), device_id_type=MESH)
                cp.wait_send()
                cp.wait_recv()

    return pl.pallas_call(
        body, name=name, out_shape=[pltpu.HBM(a.shape, a.dtype) for a in lands],
        in_specs=[HBM] * n + [SEM, SEM, ANY], out_specs=[HBM] * n,
        input_output_aliases={j: j for j in range(n)},
        compiler_params=pltpu.CompilerParams(has_side_effects=EFFECT),
    )(*lands, send_sem, recv_sem, after)


def _sibling_fill(lands, name):
    n = len(lands)

    def body(*refs):
        ins = refs[:n]
        send_sems, recv_sems = refs[2 * n:]
        x, y, c = _coords()
        sends, recvs = [], []
        for t in range(n):
            for m in (1, 2, 3):
                _, _, pk = _chip_peer(x, y, m)
                for rows, lst in ((_core_rows(ins[t], c), sends), (_core_rows(ins[t], 1 - c), recvs)):
                    lst.append(pltpu.make_async_remote_copy(
                        src_ref=ins[t].at[pk, rows, :], dst_ref=ins[t].at[pk, rows, :],
                        send_sem=send_sems.at[3 * t + m - 1], recv_sem=recv_sems.at[3 * t + m - 1],
                        device_id=(x, y, 1 - c), device_id_type=MESH))
        for cp in sends:
            cp.start()
        for cp in recvs:
            cp.wait_recv()
        for cp in sends:
            cp.wait_send()

    return pl.pallas_call(
        body, name=name, in_specs=[ANY] * n, out_specs=[ANY] * n,
        out_shape=[jax.ShapeDtypeStruct(a.shape, a.dtype) for a in lands],
        input_output_aliases={t: t for t in range(n)},
        scratch_shapes=[pltpu.SemaphoreType.DMA((3 * n,)), pltpu.SemaphoreType.DMA((3 * n,))],
    )(*lands)


def _scatter_start(srcs, name, after=()):
    n, na = len(srcs), len(after)

    def body(*refs):
        ins, lands = refs[:n], refs[n:2 * n]
        send, recv = refs[2 * n + na], refs[2 * n + na + 1]
        token = refs[2 * n + na + 2 + 2 * n]
        x, y, c = _coords()
        k = 2 * x + y
        for t in range(n):
            for m in (1, 2, 3):
                px, py, pk = _chip_peer(x, y, m)
                pltpu.make_async_remote_copy(
                    src_ref=ins[t].at[pk], dst_ref=lands[t].at[k], send_sem=send.at[3 * t + m - 1],
                    recv_sem=recv.at[3 * t + m - 1], device_id=(px, py, c), device_id_type=MESH).start()
        token[...] = jnp.zeros_like(token)

    sem = pltpu.SemaphoreType.DMA((3 * n,))
    hbm = [pltpu.HBM(a.shape, a.dtype) for a in srcs]
    operands = list(srcs) + [lax.empty(a.shape, a.dtype) for a in srcs]
    out = pl.pallas_call(
        body, name=name, out_shape=[sem, sem] + hbm + hbm + [jax.ShapeDtypeStruct((8, 128), F32)],
        in_specs=[HBM] * (2 * n) + [ANY] * na,
        out_specs=[SEM, SEM] + [HBM] * (2 * n) + [pl.BlockSpec(memory_space=pltpu.VMEM)],
        input_output_aliases={t: 2 + t for t in range(2 * n)},
        compiler_params=pltpu.CompilerParams(has_side_effects=EFFECT),
    )(*[pltpu.with_memory_space_constraint(a, pltpu.HBM) for a in operands], *after)
    return out[0], out[1], out[2:2 + n], out[2 + n:2 + 2 * n], out[2 + 2 * n]


def _scatter_wait(srcs, lands, send_sem, recv_sem, after, name):
    n = len(srcs)

    def body(*refs):
        ins, land = refs[:n], refs[n:2 * n]
        send, recv = refs[2 * n], refs[2 * n + 1]
        x, y, c = _coords()
        for t in range(n):
            for m in (1, 2, 3):
                px, py, pk = _chip_peer(x, y, m)
                cp = pltpu.make_async_remote_copy(
                    src_ref=ins[t].at[pk], dst_ref=land[t].at[pk], send_sem=send.at[3 * t + m - 1],
                    recv_sem=recv.at[3 * t + m - 1], device_id=(px, py, c), device_id_type=MESH)
                cp.wait_send()
                cp.wait_recv()

    hbm = [pltpu.HBM(a.shape, a.dtype) for a in srcs]
    out = pl.pallas_call(
        body, name=name, out_shape=hbm + hbm, in_specs=[HBM] * (2 * n) + [SEM, SEM, ANY], out_specs=[HBM] * (2 * n),
        input_output_aliases={t: t for t in range(2 * n)},
        compiler_params=pltpu.CompilerParams(has_side_effects=EFFECT),
    )(*srcs, *lands, send_sem, recv_sem, after)
    return out[:n], out[n:]


def _sum_own_half(g, ra, kc, name):
    _, R, C = g.shape
    half = R // 2
    br = _row_block(half, C)
    nb = half // br

    def body(kc_ref, g_ref, ra_ref, o_ref):
        o_ref[...] = (g_ref[...] + ra_ref[...].astype(F32)).astype(BF16)

    return pl.pallas_call(
        body, name=name,
        grid_spec=pltpu.PrefetchScalarGridSpec(
            num_scalar_prefetch=1, grid=(NCHIP, nb),
            in_specs=[pl.BlockSpec((None, br, C), lambda j, i, kc: (j, kc[1] * nb + i, 0)),
                      pl.BlockSpec((None, br, C), lambda j, i, kc: (j, i, 0))],
            out_specs=pl.BlockSpec((None, br, C), lambda j, i, kc: (j, i, 0))),
        out_shape=jax.ShapeDtypeStruct((NCHIP, half, C), BF16),
        compiler_params=_params("parallel", "parallel"),
    )(kc, g, ra)


def _sum_chips(sa, rb, kc, name, after=None):
    _, half, C = rb.shape
    br = _row_block(half, C)
    nb = half // br
    extra = [] if after is None else [after]

    def body(kc_ref, own_ref, r1_ref, r2_ref, r3_ref, *rest):
        out, obuf, local_sems, send_sems, recv_sem = rest[-5:]
        i = pl.program_id(0)
        slot = i % 2
        x, y, c = _coords()

        def copies(i_, slot_):
            rows = out.at[pl.ds(pl.multiple_of((c * nb + i_) * br, 8), br), :]
            return (pltpu.make_async_copy(obuf.at[slot_], rows, local_sems.at[slot_]),
                    pltpu.make_async_remote_copy(src_ref=obuf.at[slot_], dst_ref=rows, send_sem=send_sems.at[slot_],
                                                 recv_sem=recv_sem, device_id=(x, y, 1 - c), device_id_type=MESH))

        @pl.when(i >= 2)
        def _():
            here, there = copies(i, slot)
            here.wait()
            there.wait_send()

        acc = own_ref[...].astype(F32) + r1_ref[...].astype(F32)
        obuf[slot] = (acc + r2_ref[...].astype(F32)) + r3_ref[...].astype(F32)
        here, there = copies(i, slot)
        here.start()
        there.start()

        @pl.when(i == nb - 1)
        def _():
            for s in range(min(2, nb)):
                here, there = copies(i, (i - s) % 2)
                here.wait()
                there.wait_send()
            theirs = out.at[pl.ds(pl.multiple_of((1 - c) * half, 8), half), :]
            pltpu.make_async_remote_copy(src_ref=theirs, dst_ref=theirs, send_sem=send_sems.at[0], recv_sem=recv_sem,
                                         device_id=(x, y, 1 - c), device_id_type=MESH).wait_recv()

    def slab(m):
        return pl.BlockSpec((None, br, C), lambda i, kc: (kc[0] ^ m, i, 0))

    return pl.pallas_call(
        body, name=name,
        grid_spec=pltpu.PrefetchScalarGridSpec(
            num_scalar_prefetch=1, grid=(nb,),
            in_specs=[slab(0), slab(1), slab(2), slab(3)] + [ANY] * len(extra),
            out_specs=ANY,
            scratch_shapes=[pltpu.VMEM((2, br, C), F32), pltpu.SemaphoreType.DMA((2,)), pltpu.SemaphoreType.DMA((2,)),
                            pltpu.SemaphoreType.DMA]),
        out_shape=jax.ShapeDtypeStruct((2 * half, C), F32), compiler_params=_params("arbitrary"),
    )(kc, sa, rb, rb, rb, *extra)


def _sum8(ga, name):
    _, R, C = ga.shape

    def body(g_ref, o_ref):
        acc = g_ref[0]
        for j in range(1, 8):
            acc = acc + g_ref[j]
        o_ref[...] = acc

    return pl.pallas_call(
        body, name=name, in_specs=[pl.BlockSpec((8, R, C), lambda: (0, 0, 0))],
        out_specs=pl.BlockSpec((R, C), lambda: (0, 0)), out_shape=jax.ShapeDtypeStruct((R, C), F32),
    )(ga)


ADA_COLS = 9 * D // NCHIP
ADA_BLK = 256


def _ada_mod(c_all, ada_w, ada_b, kidx):
    def body(k_ref, c_ref, w_ref, b_ref, o_ref):
        cv = c_ref[...]
        cs = cv * _sigmoid(cv)
        o_ref[...] = jnp.dot(cs, w_ref[...], precision=lax.Precision.HIGHEST,
                             preferred_element_type=F32) + b_ref[...]

    nblk = ADA_COLS // ADA_BLK
    return pl.pallas_call(
        body, name="ada_mod",
        grid_spec=pltpu.PrefetchScalarGridSpec(
            num_scalar_prefetch=1, grid=(nblk,),
            in_specs=[pl.BlockSpec((8, D), lambda j, k: (0, 0)),
                      pl.BlockSpec((D, ADA_BLK), lambda j, k: (0, j)),
                      pl.BlockSpec((1, ADA_BLK), lambda j, k: (0, k[0] * nblk + j))],
            out_specs=pl.BlockSpec((8, ADA_BLK), lambda j, k: (0, j))),
        out_shape=jax.ShapeDtypeStruct((8, ADA_COLS), F32),
        compiler_params=_params("parallel"),
    )(kidx, c_all, ada_w, ada_b)


def _ada_grad(c_all_t, dmod_all, kidx):
    def body(k_ref, ct_ref, dm_ref, o_ref):
        cv = ct_ref[...]
        cs = cv * _sigmoid(cv)
        acc = cs[:, 0:1] * dm_ref[0:1, :]
        for b in range(1, 8):
            acc = acc + cs[:, b:b + 1] * dm_ref[b:b + 1, :]
        o_ref[...] = acc

    nblk = ADA_COLS // ADA_BLK
    return pl.pallas_call(
        body, name="ada_grad",
        grid_spec=pltpu.PrefetchScalarGridSpec(
            num_scalar_prefetch=1, grid=(nblk,),
            in_specs=[pl.BlockSpec((D, 8), lambda j, k: (0, 0)),
                      pl.BlockSpec((8, ADA_BLK), lambda j, k: (0, k[0] * nblk + j))],
            out_specs=pl.BlockSpec((D, ADA_BLK), lambda j, k: (0, j))),
        out_shape=jax.ShapeDtypeStruct((D, ADA_COLS), F32),
        compiler_params=_params("parallel"),
    )(kidx, c_all_t, dmod_all)


BIG = ("ffn1_w_in", "ffn1_w_out", "mix_w_in", "hgrn_w_o", "conv_w_o", "mix_w_out", "ffn2_w_in", "ffn2_w_out")
ROW_SHARDED = ("ffn1_w_out", "hgrn_w_o", "conv_w_o", "mix_w_out", "ffn2_w_out")
GATHER_GROUPS = ((0, 1), (2,), (3, 4, 5), (6, 7))
GATHER_HALVED = (True, True, False, False)
GATHER_STARTS = ((0, 1), (2, 3))
PACK_LEN = {"ada_b": 9, "hgrn_lb": 2}
WEIGHTS = ("ada_w", "ada_b", "norm_ffn1", "ffn1_w_in", "ffn1_w_out", "norm_mix", "mix_w_in", "hgrn_lb", "hgrn_g",
           "hgrn_w_o", "conv_w", "conv_b", "conv_ln_g", "conv_ln_b", "conv_w_o", "mix_w_out", "norm_ffn2",
           "ffn2_w_in", "ffn2_w_out", "norm_final")
PACKED = ("ada_b", "norm_ffn1", "norm_mix", "hgrn_g", "conv_b", "conv_ln_g", "conv_ln_b", "norm_ffn2",
          "norm_final", "hgrn_lb")


def _pack_params(p, name):
    parts = [(p[n].reshape(PACK_LEN.get(n, 1), D), 0, PACK_LEN.get(n, 1), PACK_AT[n]) for n in PACKED]
    return _pack_rows(parts, PACK_ROWS, name)


def _step(w, m, v, x, c, tgt):
    xi, yi, ci = _coords()
    kidx = (2 * xi + yi).astype(jnp.int32).reshape(1)
    kc = jnp.stack([2 * xi + yi, ci]).astype(jnp.int32)
    me = 4 * xi + 2 * yi + ci

    cq = D // NCHIP
    first = jnp.zeros((40, cq), F32).at[0:CONV_K].set(w["conv_w"][0]).at[32:36].set(c.reshape(NCHIP, cq))
    first_all = _allgather8(first, "gather_c_conv_w")
    c_all = first_all[:, 32:36, :].reshape(8, D)
    mod_cols = _ada_mod(c_all, w["ada_w"][0], w["ada_b"], kidx)
    mod_all = _allgather8(mod_cols, "gather_mod")
    mod = lax.dynamic_slice(mod_all, (0, me, 0), (8, 1, ADA_COLS))[::2].reshape(9, D)
    small = {n: w[n].reshape(-1, D) for n in ("norm_ffn1", "norm_mix", "hgrn_lb", "hgrn_g", "conv_b", "conv_ln_g",
                                              "conv_ln_b", "norm_ffn2", "norm_final")}
    small["conv_w"] = jnp.concatenate([first_all[2 * j, 0:32, :] for j in range(NCHIP)], axis=1)

    lands, sends, recvs = [], [], []
    after = mod
    for part in GATHER_STARTS:
        tensors = [t for gi in part for t in GATHER_GROUPS[gi]]
        cast = [_cast_into_slot(w[BIG[t]][0], kc, "cast_" + BIG[t], after) for t in tensors]
        groups = [tuple(tensors.index(t) for t in GATHER_GROUPS[gi]) for gi in part]
        s, r, thru, after = _gather_start(cast, groups, [GATHER_HALVED[gi] for gi in part], [after],
                                          "gather_weights_start%d" % part[0])
        lands, sends, recvs = lands + list(thru), sends + list(s), recvs + list(r)
    started_all = after
    ready = {}

    def weight(name, after):
        t = BIG.index(name)
        if t not in ready:
            gi = [t in grp for grp in GATHER_GROUPS].index(True)
            grp = GATHER_GROUPS[gi]
            outs = _gather_wait([lands[j] for j in grp], GATHER_HALVED[gi], sends[gi], recvs[gi],
                                started_all if gi == 0 else after, "gather_weights_wait%d" % gi)
            if GATHER_HALVED[gi]:
                outs = _sibling_fill(outs, "gather_weights_fill%d" % gi)
            ready.update(zip(grp, outs))
        return ready[t].reshape(-1, D) if name in ROW_SHARDED else ready[t]

    grads, delta, new_m, new_v = {}, {}, {}, {}
    flight = []
    landed = []

    def settle(after):
        names, sa, rb, send, recv = flight.pop()
        sa, rb = _scatter_wait(sa, rb, send, recv, after, "rs_chip_wait_" + names[0])
        landed.append((names, sa, rb))

    def reduce(names, pairs, after=None):
        gs = [g.reshape(NCHIP, -1, g.shape[-1]) for g, _ in pairs]
        ra = [r.reshape(NCHIP, -1, r.shape[-1]) for _, r in pairs]
        sa = [_sum_own_half(g, r, kc, "rs_sum_pair_" + n) for g, r, n in zip(gs, ra, names)]
        if flight:
            settle(sa[0])
        send, recv, sa, rb, tok = _scatter_start(sa, "rs_chip_start_" + names[0], () if after is None else (after,))
        flight.append((names, sa, rb, send, recv))
        started.append(tok)
        return tok

    def adamw(n, after=None):
        shape = w[n].shape
        two = (shape[-2], shape[-1])
        out = _adamw(w[n].reshape(two), grads[n], m[n].reshape(two), v[n].reshape(two), "adamw_" + n, after,
                     copy_grad=n in BIG)
        g_ = out[3] if n in BIG else grads[n]
        grads[n], delta[n], new_m[n], new_v[n] = (a.reshape(shape) for a in (g_, out[0], out[1], out[2]))
        return out[1]

    def finish(after=None):
        names, sa, rb = landed.pop(0)
        full = [_sum_chips(s, r, kc, "rs_sum_chips_" + n, after) for s, r, n in zip(sa, rb, names)]
        grads.update(zip(names, full))
        return [adamw(n) for n in names][-1]

    started = []

    smalls = []

    def reduce_small(parts):
        blocks = _pack_rows(parts, PACK_ROWS, "pack_small_grads", slot=me.astype(jnp.int32).reshape(1))
        send, recv, blocks, tok = _allgather8_start(blocks, "gather_small_grads_start")

        def finish(after):
            packed_all = _allgather8_wait(blocks, send, recv, after, "gather_small_grads_wait")
            smalls.extend([packed_all, _sum8(packed_all, "sum_small_grads")])
            return smalls[1]

        return finish, tok

    dx = _local_step(x[0], tgt[0], mod, small, kc, weight, reduce, reduce_small)
    packed_all, gsum = smalls
    loss = (0.5 / D) * jnp.sum(gsum[PACK_AT["loss"]])
    dmod_all = packed_all[:, 0:9, :].reshape(8, 9 * D)
    grads["ada_w"] = _ada_grad(c_all.T, dmod_all, kidx)
    grads["conv_w"] = lax.dynamic_slice(gsum, (PACK_AT["conv_w"], kidx[0] * (D // NCHIP)), (CONV_K, D // NCHIP))

    tok = started[-1]
    adamw("ada_w", tok)
    adamw("conv_w")
    pw, pm, pv = (_pack_params(p, "pack_" + s) for p, s in ((w, "w"), (m, "m"), (v, "v")))
    pd, pnm, pnv = _adamw(pw, gsum, pm, pv, "adamw_small", tok)
    last = pnv
    while landed:
        last = finish(tok)
    settle(last)
    finish()
    for n in PACKED:
        rows = slice(PACK_AT[n], PACK_AT[n] + PACK_LEN.get(n, 1))
        for dst, src in ((grads, gsum), (delta, pd), (new_m, pnm), (new_v, pnv)):
            dst[n] = src[rows].reshape(w[n].shape)

    outs = [loss, dx[None]]
    for d in (grads, delta, new_m, new_v):
        outs += [d[n] for n in WEIGHTS]
    return tuple(outs)


def kernel(x, c, ada_w, ada_b, norm_ffn1, ffn1_w_in, ffn1_w_out, norm_mix, mix_w_in, hgrn_lb, hgrn_g, hgrn_w_o, conv_w, conv_b, conv_ln_g, conv_ln_b, conv_w_o, mix_w_out, norm_ffn2, ffn2_w_in, ffn2_w_out, norm_final, loss_target, m_ada_w, m_ada_b, m_norm_ffn1, m_ffn1_w_in, m_ffn1_w_out, m_norm_mix, m_mix_w_in, m_hgrn_lb, m_hgrn_g, m_hgrn_w_o, m_conv_w, m_conv_b, m_conv_ln_g, m_conv_ln_b, m_conv_w_o, m_mix_w_out, m_norm_ffn2, m_ffn2_w_in, m_ffn2_w_out, m_norm_final, v_ada_w, v_ada_b, v_norm_ffn1, v_ffn1_w_in, v_ffn1_w_out, v_norm_mix, v_mix_w_in, v_hgrn_lb, v_hgrn_g, v_hgrn_w_o, v_conv_w, v_conv_b, v_conv_ln_g, v_conv_ln_b, v_conv_w_o, v_mix_w_out, v_norm_ffn2, v_ffn2_w_in, v_ffn2_w_out, v_norm_final):
    w = dict(ada_w=ada_w, ada_b=ada_b, norm_ffn1=norm_ffn1, ffn1_w_in=ffn1_w_in, ffn1_w_out=ffn1_w_out,
             norm_mix=norm_mix, mix_w_in=mix_w_in, hgrn_lb=hgrn_lb, hgrn_g=hgrn_g, hgrn_w_o=hgrn_w_o, conv_w=conv_w,
             conv_b=conv_b, conv_ln_g=conv_ln_g, conv_ln_b=conv_ln_b, conv_w_o=conv_w_o, mix_w_out=mix_w_out,
             norm_ffn2=norm_ffn2, ffn2_w_in=ffn2_w_in, ffn2_w_out=ffn2_w_out, norm_final=norm_final)
    m = dict(ada_w=m_ada_w, ada_b=m_ada_b, norm_ffn1=m_norm_ffn1, ffn1_w_in=m_ffn1_w_in, ffn1_w_out=m_ffn1_w_out,
             norm_mix=m_norm_mix, mix_w_in=m_mix_w_in, hgrn_lb=m_hgrn_lb, hgrn_g=m_hgrn_g, hgrn_w_o=m_hgrn_w_o,
             conv_w=m_conv_w, conv_b=m_conv_b, conv_ln_g=m_conv_ln_g, conv_ln_b=m_conv_ln_b, conv_w_o=m_conv_w_o,
             mix_w_out=m_mix_w_out, norm_ffn2=m_norm_ffn2, ffn2_w_in=m_ffn2_w_in, ffn2_w_out=m_ffn2_w_out,
             norm_final=m_norm_final)
    v = dict(ada_w=v_ada_w, ada_b=v_ada_b, norm_ffn1=v_norm_ffn1, ffn1_w_in=v_ffn1_w_in, ffn1_w_out=v_ffn1_w_out,
             norm_mix=v_norm_mix, mix_w_in=v_mix_w_in, hgrn_lb=v_hgrn_lb, hgrn_g=v_hgrn_g, hgrn_w_o=v_hgrn_w_o,
             conv_w=v_conv_w, conv_b=v_conv_b, conv_ln_g=v_conv_ln_g, conv_ln_b=v_conv_ln_b, conv_w_o=v_conv_w_o,
             mix_w_out=v_mix_w_out, norm_ffn2=v_norm_ffn2, ffn2_w_in=v_ffn2_w_in, ffn2_w_out=v_ffn2_w_out,
             norm_final=v_norm_final)
    return _step(w, m, v, x, c, loss_target)
```

```python
import jax
import jax.numpy as jnp
from jax import lax
from jax.experimental import pallas as pl
from jax.experimental.pallas import tpu as pltpu

F32 = jnp.float32
BF16 = jnp.bfloat16

D = 1024
DFF = 2816
NCHIP = 4
FSH = 2 * DFF // NCHIP
HEADS = 8
DK = 128
CHUNK = 64
CONV_K = 31
HALO = 32
EPS = 1e-6
TB = 256
CB = 1024
DW_TOKENS = 2048
VMEM_LIMIT = 56 * 1024 * 1024

ADAM_LR = 0.001
ADAM_B1 = 0.9
ADAM_B2 = 0.999
ADAM_EPS = 1e-08
ADAM_WD = 0.01
ADAM_STEP = 10

MESH = pl.DeviceIdType.MESH
ANY = pl.BlockSpec(memory_space=pl.ANY)


def _params(*sem):
    return pltpu.CompilerParams(dimension_semantics=sem, vmem_limit_bytes=VMEM_LIMIT)


def _sigmoid(x):
    return 0.5 * jnp.tanh(0.5 * x) + 0.5


def _dsilu(x, sg):
    return sg * (1.0 + x * (1.0 - sg))


def _nt(a, b):
    return lax.dot_general(a, b, (((1,), (1,)), ((), ())), preferred_element_type=F32)


def _tn(a, b):
    return lax.dot_general(a, b, (((0,), (0,)), ((), ())), preferred_element_type=F32)


def _nn(a, b):
    return jnp.dot(a, b, preferred_element_type=F32)


def _colsum(x):
    return jnp.sum(x, axis=0, keepdims=True)


def _rms_fwd(x, gn, sc, sh):
    r = lax.rsqrt(jnp.mean(x * x, axis=-1, keepdims=True) + EPS)
    n = x * r
    h = (n * gn) * (1.0 + sc) + sh
    return r, n, h


def _rms_bwd(dh, r, n, gn, sc, acc_ref):
    acc_ref[0:1, :] += _colsum(dh)
    acc_ref[1:2, :] += _colsum(dh * (n * gn))
    dng = dh * (1.0 + sc)
    acc_ref[3:4, :] += _colsum(dng * n)
    dn = dng * gn
    return r * (dn - n * jnp.mean(dn * n, axis=-1, keepdims=True))


def _loss_head(x, tgt, gf, acc_ref):
    r = lax.rsqrt(jnp.mean(x * x, axis=-1, keepdims=True) + EPS)
    n = x * r
    err = n * gf - tgt
    acc_ref[1:2, :] += _colsum(err * err)
    dy = err * (1.0 / D)
    acc_ref[0:1, :] += _colsum(dy * n)
    dn = dy * gf
    return r * (dn - n * jnp.mean(dn * n, axis=-1, keepdims=True))


def _ffn_fwd(x, vec, w_in, w_out, name, head=None):
    T = x.shape[0]
    nh = 0 if head is None else 2

    def body(x_ref, vec_ref, *rest):
        win_hbm, wout_hbm = rest[nh:nh + 2]
        xo_ref, h_ref, a_ref, b_ref, s_ref, f_ref = rest[nh + 2:nh + 8]
        win, wout = rest[-2:]

        @pl.when(pl.program_id(0) == 0)
        def _():
            pltpu.sync_copy(win_hbm, win)
            pltpu.sync_copy(wout_hbm, wout)
            if head is not None:
                rest[nh + 8][...] = jnp.zeros((8, D), F32)

        x = x_ref[...]
        sh, sc, gate, gn = vec_ref[0:1, :], vec_ref[1:2, :], vec_ref[2:3, :], vec_ref[3:4, :]
        _, _, h = _rms_fwd(x, gn, sc, sh)
        hb = h.astype(BF16)
        h_ref[...] = hb
        f = jnp.zeros((TB, D), F32)
        for j in range(2):
            cols = slice(j * FSH, (j + 1) * FSH)
            a = _nn(hb, win[j])
            b = _nn(hb, win[2 + j])
            s = (a * _sigmoid(a) * b).astype(BF16)
            a_ref[:, cols] = a.astype(BF16)
            b_ref[:, cols] = b.astype(BF16)
            s_ref[:, cols] = s
            f = f + _nn(s, wout[cols, :])
        xo = x + (0.5 * gate) * f
        f_ref[...] = f.astype(BF16)
        if head is None:
            xo_ref[...] = xo
        else:
            xo_ref[...] = _loss_head(xo, rest[0][...], rest[1][0:1, :], rest[nh + 8])

    row = lambda w: pl.BlockSpec((TB, w), lambda i: (i, 0))
    vec8 = pl.BlockSpec((8, D), lambda i: (0, 0))
    acc = [] if head is None else [jax.ShapeDtypeStruct((8, D), F32)]
    return pl.pallas_call(
        body, name=name, grid=(T // TB,),
        in_specs=[row(D), vec8] + ([] if head is None else [row(D), vec8]) + [ANY, ANY],
        out_specs=[row(D), row(D), row(DFF), row(DFF), row(DFF), row(D)] + [vec8] * len(acc),
        out_shape=[jax.ShapeDtypeStruct((T, D), F32), jax.ShapeDtypeStruct((T, D), BF16),
                   jax.ShapeDtypeStruct((T, DFF), BF16), jax.ShapeDtypeStruct((T, DFF), BF16),
                   jax.ShapeDtypeStruct((T, DFF), BF16), jax.ShapeDtypeStruct((T, D), BF16)] + acc,
        scratch_shapes=[pltpu.VMEM((NCHIP, D, FSH), BF16), pltpu.VMEM((DFF, D), BF16)],
        compiler_params=_params("arbitrary"),
    )(x, vec, *([] if head is None else list(head)), w_in, w_out)


def _ffn_bwd(dxo, x, vec, a, b, f, w_in, w_out, name):
    T = x.shape[0]

    def body(dxo_ref, x_ref, vec_ref, a_ref, b_ref, f_ref, win_hbm, wout_hbm,
             dx_ref, df_ref, dab_ref, acc_ref, win, wout):
        @pl.when(pl.program_id(0) == 0)
        def _():
            pltpu.sync_copy(win_hbm, win)
            pltpu.sync_copy(wout_hbm, wout)
            acc_ref[...] = jnp.zeros_like(acc_ref)

        dxo = dxo_ref[...]
        x = x_ref[...]
        sh, sc, gate, gn = vec_ref[0:1, :], vec_ref[1:2, :], vec_ref[2:3, :], vec_ref[3:4, :]
        r, n, _ = _rms_fwd(x, gn, sc, sh)
        acc_ref[2:3, :] += _colsum(0.5 * f_ref[...].astype(F32) * dxo)
        dfb = ((0.5 * gate) * dxo).astype(BF16)
        df_ref[...] = dfb
        dh = jnp.zeros((TB, D), F32)
        for j in range(2):
            cols = slice(j * FSH, (j + 1) * FSH)
            ds = _nt(dfb, wout[cols, :])
            av = a_ref[:, cols].astype(F32)
            bv = b_ref[:, cols].astype(F32)
            sg = _sigmoid(av)
            da = (ds * bv * _dsilu(av, sg)).astype(BF16)
            db = (ds * (av * sg)).astype(BF16)
            dab_ref[j] = da
            dab_ref[2 + j] = db
            dh = dh + _nt(da, win[j]) + _nt(db, win[2 + j])
        dx_ref[...] = dxo + _rms_bwd(dh, r, n, gn, sc, acc_ref)

    row = lambda w: pl.BlockSpec((TB, w), lambda i: (i, 0))
    vec8 = pl.BlockSpec((8, D), lambda i: (0, 0))
    return pl.pallas_call(
        body, name=name, grid=(T // TB,),
        in_specs=[row(D), row(D), vec8, row(DFF), row(DFF), row(D), ANY, ANY],
        out_specs=[row(D), pl.BlockSpec((None, TB, D), lambda i: (0, i, 0)),
                   pl.BlockSpec((NCHIP, TB, FSH), lambda i: (0, i, 0)), vec8],
        out_shape=[jax.ShapeDtypeStruct((T, D), F32), jax.ShapeDtypeStruct((1, T, D), BF16),
                   jax.ShapeDtypeStruct((NCHIP, T, FSH), BF16), jax.ShapeDtypeStruct((8, D), F32)],
        scratch_shapes=[pltpu.VMEM((NCHIP, D, FSH), BF16), pltpu.VMEM((DFF, D), BF16)],
        compiler_params=_params("arbitrary"),
    )(dxo, x, vec, a, b, f, w_in, w_out)


def _mm_tn(a, b3, hp, kc, shard_rows, name, into=None, slab=0, slabs=None, after=None):
    T, M = a.shape
    P, _, N = b3.shape
    tm = M if M <= 1408 else M // 2
    tk = min(T, DW_TOKENS if P * (M // tm) > 1 else DW_TOKENS // 2)
    nk = T // tk
    ni = M // tm
    slabs = P // hp if slabs is None else slabs
    half = shard_rows // 2
    extra = ([] if into is None else list(into)) + ([] if after is None else [after])

    def body(kc_ref, a_ref, b_ref, *rest):
        o_ref, ra_ref, hbuf, send_sems, recv_sem = rest[-5:]
        p, i, k = pl.program_id(0), pl.program_id(1), pl.program_id(2)
        x, y, c = _coords()
        step = p * ni + i
        slot = step % 2

        def send(p_, i_, slot_):
            dst = ra_ref.at[slab + p_ // hp, pl.ds(pl.multiple_of(i_ * (tm // 2), 8), tm // 2),
                            pl.ds(pl.multiple_of((p_ % hp) * N, LANES), N)]
            return pltpu.make_async_remote_copy(
                src_ref=hbuf.at[slot_], dst_ref=dst, send_sem=send_sems.at[slot_], recv_sem=recv_sem,
                device_id=(x, y, 1 - c), device_id_type=MESH)

        @pl.when(k == 0)
        def _():
            o_ref[...] = jnp.zeros_like(o_ref)

        o_ref[...] += _tn(a_ref[...], b_ref[...])

        @pl.when(k == nk - 1)
        def _():
            @pl.when(step >= 2)
            def _():
                send(p, i, slot).wait_send()

            for j in range(tm // shard_rows):
                start = pl.multiple_of(j * shard_rows + (1 - kc_ref[1]) * half, 8)
                hbuf[slot, j * half:(j + 1) * half, :] = o_ref[pl.ds(start, half), :].astype(BF16)
            send(p, i, slot).start()

        @pl.when((step == P * ni - 1) & (k == nk - 1))
        def _():
            for s in range(min(2, P * ni)):
                send(p, i, (step - s) % 2).wait_send()
            mine = ra_ref.at[slab:slab + P // hp]
            pltpu.make_async_remote_copy(src_ref=mine, dst_ref=mine, send_sem=send_sems.at[0], recv_sem=recv_sem,
                                         device_id=(x, y, 1 - c), device_id_type=MESH).wait_recv()

    return pl.pallas_call(
        body, name=name,
        grid_spec=pltpu.PrefetchScalarGridSpec(
            num_scalar_prefetch=1, grid=(P, ni, nk),
            in_specs=[pl.BlockSpec((tk, tm), lambda p, i, k, kc: (k, i)),
                      pl.BlockSpec((None, tk, N), lambda p, i, k, kc: (p, k, 0))] + [ANY] * len(extra),
            out_specs=[pl.BlockSpec((None, tm, N), lambda p, i, k, kc: (slab + p // hp, i, p % hp)), ANY],
            scratch_shapes=[pltpu.VMEM((2, tm // 2, N), BF16), pltpu.SemaphoreType.DMA((2,)),
                            pltpu.SemaphoreType.DMA]),
        out_shape=[jax.ShapeDtypeStruct((slabs, M, hp * N), F32), jax.ShapeDtypeStruct((slabs, M // 2, hp * N), BF16)],
        input_output_aliases={} if into is None else {3: 0, 4: 1},
        compiler_params=_params("arbitrary", "arbitrary", "arbitrary"),
    )(kc, a, b3, *extra)


def _mix_proj_fwd(x, vec, w_in):
    T = x.shape[0]

    def body(x_ref, vec_ref, w_hbm, h_ref, qr_ref, g_ref, k_ref, v_ref, og_ref, u_ref, ua_ref, ub_ref,
             sa_ref, sb_ref, w):
        @pl.when(pl.program_id(0) == 0)
        def _():
            pltpu.sync_copy(w_hbm, w)

        x = x_ref[...]
        sh, sc, gn, lb = vec_ref[0:1, :], vec_ref[1:2, :], vec_ref[3:4, :], vec_ref[4:5, :]
        _, _, h = _rms_fwd(x, gn, sc, sh)
        hb = h.astype(BF16)
        h_ref[...] = hb
        p = _nn(hb, w[0])
        qr_ref[...] = p[:, :D].astype(BF16)
        fg = lb + (1.0 - lb) * _sigmoid(p[:, D:])
        g_ref[...] = jnp.log(fg)
        k_ref[...] = (1.0 - fg).astype(BF16)
        p = _nn(hb, w[1])
        v_ref[...] = p[:, :D].astype(BF16)
        og_ref[...] = p[:, D:].astype(BF16)
        p = _nn(hb, w[2])
        ua, ub = p[:, :D], p[:, D:]
        u_ref[...] = ua * _sigmoid(ub)
        ua_ref[...] = ua.astype(BF16)
        ub_ref[...] = ub.astype(BF16)
        p = _nn(hb, w[3])
        sa_ref[...] = _sigmoid(p[:, :D]).astype(BF16)
        sb_ref[...] = _sigmoid(p[:, D:]).astype(BF16)

    row = pl.BlockSpec((TB, D), lambda i: (i, 0))
    bf = jax.ShapeDtypeStruct((T, D), BF16)
    f32 = jax.ShapeDtypeStruct((T, D), F32)
    return pl.pallas_call(
        body, name="mix_proj_fwd", grid=(T // TB,),
        in_specs=[row, pl.BlockSpec((8, D), lambda i: (0, 0)), ANY],
        out_specs=[row] * 11,
        out_shape=[bf, bf, f32, bf, bf, bf, f32, bf, bf, bf, bf],
        scratch_shapes=[pltpu.VMEM((NCHIP, D, 2 * D), BF16)],
        compiler_params=_params("arbitrary"),
    )(x, vec, w_in)


def _mix_proj_bwd(dxo, x, vec, dpa, dpb, dpc, w_in):
    T = x.shape[0]

    def body(dxo_ref, x_ref, vec_ref, dpa_ref, dpb_ref, dpc_ref, w_hbm, dx_ref, acc_ref, w):
        @pl.when(pl.program_id(0) == 0)
        def _():
            pltpu.sync_copy(w_hbm, w)
            acc_ref[...] = jnp.zeros_like(acc_ref)

        x = x_ref[...]
        sh, sc, gn = vec_ref[0:1, :], vec_ref[1:2, :], vec_ref[3:4, :]
        r, n, _ = _rms_fwd(x, gn, sc, sh)
        dh = jnp.zeros((TB, D), F32)
        for p in range(8):
            src = dpa_ref[p] if p < 4 else (dpb_ref[p - 4] if p < 6 else dpc_ref[p - 6])
            dh = dh + _nt(src, w[p // 2, :, (p % 2) * D:(p % 2 + 1) * D])
        dx_ref[...] = dxo_ref[...] + _rms_bwd(dh, r, n, gn, sc, acc_ref)

    row = pl.BlockSpec((TB, D), lambda i: (i, 0))
    vec8 = pl.BlockSpec((8, D), lambda i: (0, 0))
    stack = lambda k: pl.BlockSpec((k, TB, D), lambda i: (0, i, 0))
    return pl.pallas_call(
        body, name="mix_proj_bwd", grid=(T // TB,),
        in_specs=[row, row, vec8, stack(4), stack(2), stack(2), ANY],
        out_specs=[row, vec8],
        out_shape=[jax.ShapeDtypeStruct((T, D), F32), jax.ShapeDtypeStruct((8, D), F32)],
        scratch_shapes=[pltpu.VMEM((NCHIP, D, 2 * D), BF16)],
        compiler_params=_params("arbitrary"),
    )(dxo, x, vec, dpa, dpb, dpc, w_in)


def _tri(lower):
    r = lax.broadcasted_iota(jnp.int32, (CHUNK, CHUNK), 0)
    c = lax.broadcasted_iota(jnp.int32, (CHUNK, CHUNK), 1)
    return (c <= r) if lower else (c >= r)


def _cumsum_rows(mask, g):
    hi = g.astype(BF16)
    rest = g - hi.astype(F32)
    mid = rest.astype(BF16)
    low = (rest - mid.astype(F32)).astype(BF16)
    n = g.shape[1]
    p = _nn(mask.astype(BF16), jnp.concatenate([hi, mid, low], axis=1))
    return (p[:, 2 * n:] + p[:, n:2 * n]) + p[:, :n]


def _chunk_decay(low, g, nck):
    bs, mids, lasts = [], [], []
    for c in range(nck):
        gc = g[c * CHUNK:(c + 1) * CHUNK]
        bs.append(_cumsum_rows(low, gc))
        mids.append(_colsum(gc[0:CHUNK // 2]))
        lasts.append(_colsum(gc))
    spread = lambda rows: jnp.concatenate([jnp.broadcast_to(r, (CHUNK, DK)) for r in rows], axis=0)
    return jnp.concatenate(bs, axis=0), spread(mids), spread(lasts), lasts


def _hgrn_fwd(qr, g, k, v, og, vec):
    T = qr.shape[0]
    nck = CB // CHUNK

    def body(qr_ref, g_ref, k_ref, v_ref, og_ref, vec_ref, out_ref, o_ref, st_ref, state):
        @pl.when(pl.program_id(1) == 0)
        def _():
            state[...] = jnp.zeros_like(state)

        low = _tri(True)
        qv = qr_ref[...].astype(F32)
        q = qv * _sigmoid(qv) * (DK ** -0.5)
        kk = k_ref[...].astype(F32)
        vb = v_ref[...]
        b, mid, last, lasts = _chunk_decay(low, g_ref[...], nck)
        qt = (q * jnp.exp(b - mid)).astype(BF16)
        kt = (kk * jnp.exp(mid - b)).astype(BF16)
        qe = (q * jnp.exp(b)).astype(BF16)
        kd = (kk * jnp.exp(last - b)).astype(BF16)
        intra, grow = [], []
        for c in range(nck):
            r = slice(c * CHUNK, (c + 1) * CHUNK)
            att = jnp.where(low, _nt(qt[r], kt[r]), 0.0).astype(BF16)
            intra.append(_nn(att, vb[r]))
            grow.append(_tn(vb[r], kd[r]))
        st = state[...]
        inter = []
        for c in range(nck):
            stb = st.astype(BF16)
            st_ref[c] = stb
            inter.append(_nt(qe[c * CHUNK:(c + 1) * CHUNK], stb))
            st = st * jnp.exp(lasts[c]) + grow[c]
        state[...] = st
        o = jnp.concatenate(intra, axis=0) + jnp.concatenate(inter, axis=0)
        o_ref[...] = o
        ogv = og_ref[...].astype(F32)
        rms = lax.rsqrt(jnp.mean(o * o, axis=-1, keepdims=True) + EPS)
        out_ref[...] = (o * rms * vec_ref[5:6, :] * (ogv * _sigmoid(ogv))).astype(BF16)

    blk = pl.BlockSpec((CB, DK), lambda h, i: (i, h))
    return pl.pallas_call(
        body, name="hgrn_fwd", grid=(HEADS, T // CB),
        in_specs=[blk, blk, blk, blk, blk, pl.BlockSpec((8, DK), lambda h, i: (0, h))],
        out_specs=[blk, blk, pl.BlockSpec((None, nck, DK, DK), lambda h, i: (h, i, 0, 0))],
        out_shape=[jax.ShapeDtypeStruct((T, D), BF16), jax.ShapeDtypeStruct((T, D), F32),
                   jax.ShapeDtypeStruct((HEADS, T // CHUNK, DK, DK), BF16)],
        scratch_shapes=[pltpu.VMEM((DK, DK), F32)],
        compiler_params=_params("parallel", "arbitrary"),
    )(qr, g, k, v, og, vec)


def _hgrn_bwd(dout, og, qr, g, k, v, o, st, vec):
    T = qr.shape[0]
    nck = CB // CHUNK
    nb = T // CB

    def body(dout_ref, og_ref, qr_ref, g_ref, k_ref, v_ref, o_ref, st_ref, vec_ref,
             dp_ref, acc_ref, dstate):
        @pl.when(pl.program_id(1) == 0)
        def _():
            dstate[...] = jnp.zeros_like(dstate)
            acc_ref[...] = jnp.zeros_like(acc_ref)

        o = o_ref[...]
        ogv = og_ref[...].astype(F32)
        dout = dout_ref[...].astype(F32)
        hg = vec_ref[5:6, :]
        sgo = _sigmoid(ogv)
        rms = lax.rsqrt(jnp.mean(o * o, axis=-1, keepdims=True) + EPS)
        ohat = o * rms
        dp_ref[3] = (dout * (ohat * hg) * _dsilu(ogv, sgo)).astype(BF16)
        don = dout * (ogv * sgo)
        acc_ref[0:1, :] += _colsum(don * ohat)
        dohat = don * hg
        dob = (rms * (dohat - ohat * jnp.mean(dohat * ohat, axis=-1, keepdims=True))).astype(BF16)

        low = _tri(True)
        upp = _tri(False)
        lb = vec_ref[4:5, :]
        qv = qr_ref[...].astype(F32)
        sgq = _sigmoid(qv)
        q = qv * sgq * (DK ** -0.5)
        kk = k_ref[...].astype(F32)
        vb = v_ref[...]
        gv = g_ref[...]
        b, mid, last, lasts = _chunk_decay(low, gv, nck)
        eq = jnp.exp(b - mid)
        ek = jnp.exp(mid - b)
        eb = jnp.exp(b)
        ed = jnp.exp(last - b)
        qtb, ktb, qeb, kdb = ((t).astype(BF16) for t in (q * eq, kk * ek, q * eb, kk * ed))
        rows = [slice(c * CHUNK, (c + 1) * CHUNK) for c in range(nck)]

        dv1, dqt, dkt, dqe, grow = [], [], [], [], []
        for c, r in enumerate(rows):
            att = jnp.where(low, _nt(qtb[r], ktb[r]), 0.0).astype(BF16)
            datt = jnp.where(low, _nt(dob[r], vb[r]), 0.0).astype(BF16)
            dv1.append(_tn(att, dob[r]))
            dqt.append(_nn(datt, ktb[r]))
            dkt.append(_tn(datt, qtb[r]))
            dqe.append(_nn(dob[r], st_ref[c]))
            grow.append(_tn(dob[r], qeb[r]))
        ds = dstate[...]
        ds1b, dl_state = [None] * nck, [None] * nck
        for c in reversed(range(nck)):
            el = jnp.exp(lasts[c])
            ds1b[c] = ds.astype(BF16)
            dl_state[c] = el * _colsum(ds * st_ref[c].astype(F32))
            ds = ds * el + grow[c]
        dstate[...] = ds
        dkd = jnp.concatenate([_nn(vb[r], ds1b[c]) for c, r in enumerate(rows)], axis=0)
        dv = jnp.concatenate(dv1, axis=0) + jnp.concatenate([_nt(kdb[r], ds1b[c]) for c, r in enumerate(rows)], axis=0)
        dqt, dkt, dqe = (jnp.concatenate(t, axis=0) for t in (dqt, dkt, dqe))
        dq = dqt * eq + dqe * eb
        dk = dkt * ek + dkd * ed
        dkdkd = dkd * kdb.astype(F32)
        db = dqt * qtb.astype(F32) - dkt * ktb.astype(F32) + dqe * qeb.astype(F32) - dkdkd
        dg = jnp.concatenate([_cumsum_rows(upp, db[r]) + (_colsum(dkdkd[r]) + dl_state[c])
                              for c, r in enumerate(rows)], axis=0)
        fg = jnp.exp(gv)
        dfg = dg * jnp.exp(-gv) - dk
        one_m_sig = (1.0 - fg) * (1.0 / (1.0 - lb))
        dp_ref[0] = (dq * (DK ** -0.5) * _dsilu(qv, sgq)).astype(BF16)
        dp_ref[1] = (dfg * (fg - lb) * one_m_sig).astype(BF16)
        dp_ref[2] = dv.astype(BF16)
        dlb = _colsum(dfg * one_m_sig) * (lb * (1.0 - lb))
        acc_ref[1:2, :] += dlb
        acc_ref[2:3, :] -= dlb

    blk = pl.BlockSpec((CB, DK), lambda h, i: (nb - 1 - i, h))
    return pl.pallas_call(
        body, name="hgrn_bwd", grid=(HEADS, nb),
        in_specs=[blk, blk, blk, blk, blk, blk, blk,
                  pl.BlockSpec((None, nck, DK, DK), lambda h, i: (h, nb - 1 - i, 0, 0)),
                  pl.BlockSpec((8, DK), lambda h, i: (0, h))],
        out_specs=[pl.BlockSpec((4, CB, DK), lambda h, i: (0, nb - 1 - i, h)),
                   pl.BlockSpec((8, DK), lambda h, i: (0, h))],
        out_shape=[jax.ShapeDtypeStruct((4, T, D), BF16), jax.ShapeDtypeStruct((8, D), F32)],
        scratch_shapes=[pltpu.VMEM((DK, DK), F32)],
        compiler_params=_params("parallel", "arbitrary"),
    )(dout, og, qr, g, k, v, o, st, vec)


def _ln_fwd(uc, lg, lbias):
    mu = jnp.mean(uc, axis=-1, keepdims=True)
    xc = uc - mu
    rstd = lax.rsqrt(jnp.mean(xc * xc, axis=-1, keepdims=True) + EPS)
    z = xc * rstd
    return rstd, z, z * lg + lbias


LANES = 128
SUBLANES = 8
CONV_ROWS = 64


def _lane_tiles():
    return [slice(l * LANES, (l + 1) * LANES) for l in range(D // LANES)]


def _row_shifts(x):
    n = x.shape[0]
    return [x] + [pltpu.roll(x, n - r, axis=0) for r in range(1, SUBLANES)]


TAPS_PAST = tuple(HALO - (CONV_K - 1) + j for j in range(CONV_K))
TAPS_AHEAD = tuple(CONV_K - 1 - j for j in range(CONV_K))


def _tap_windows(shifted, starts, r0, rows):
    for r in range(SUBLANES):
        taps = [(j, s // SUBLANES) for j, s in enumerate(starts) if s % SUBLANES == r]
        if not taps:
            continue
        lo = min(a for _, a in taps)
        hi = max(a for _, a in taps)
        span = shifted[r][r0 + lo * SUBLANES:r0 + hi * SUBLANES + rows]
        for j, a in taps:
            yield j, span[(a - lo) * SUBLANES:(a - lo) * SUBLANES + rows]


def _conv_fwd(u, cw, cvec):
    T = u.shape[0]
    per = TB // HALO

    def body(u_ref, halo_ref, cw_ref, cvec_ref, us_ref, uc_ref, pad):
        i = pl.program_id(0)
        pad[0:HALO, :] = jnp.where(i > 0, halo_ref[...], 0.0)
        pad[HALO:, :] = u_ref[...]
        for lanes in _lane_tiles():
            shifted = _row_shifts(pad[:, lanes])
            taps = cw_ref[:, lanes]
            for r0 in range(0, TB, CONV_ROWS):
                acc = jnp.broadcast_to(cvec_ref[0:1, lanes], (CONV_ROWS, LANES))
                for j, window in _tap_windows(shifted, TAPS_PAST, r0, CONV_ROWS):
                    acc = acc + taps[j:j + 1] * window
                uc_ref[r0:r0 + CONV_ROWS, lanes] = acc
        _, _, ul = _ln_fwd(uc_ref[...], cvec_ref[1:2, :], cvec_ref[2:3, :])
        us_ref[...] = (ul * _sigmoid(ul)).astype(BF16)

    row = pl.BlockSpec((TB, D), lambda i: (i, 0))
    return pl.pallas_call(
        body, name="conv_fwd", grid=(T // TB,),
        in_specs=[row, pl.BlockSpec((HALO, D), lambda i: (jnp.maximum(i * per - 1, 0), 0)),
                  pl.BlockSpec((32, D), lambda i: (0, 0)), pl.BlockSpec((8, D), lambda i: (0, 0))],
        out_specs=[row, row],
        out_shape=[jax.ShapeDtypeStruct((T, D), BF16), jax.ShapeDtypeStruct((T, D), F32)],
        scratch_shapes=[pltpu.VMEM((TB + HALO, D), F32)],
        compiler_params=_params("parallel"),
    )(u, u, cw, cvec)


def _conv_bwd_taps(duc, u, ua, ub, cw):
    T = u.shape[0]
    per = TB // HALO
    nblk = T // TB

    def body(duc_ref, dnext_ref, u_ref, uprev_ref, ua_ref, ub_ref, cw_ref, dp_ref, dcw_ref, upad, dpad, dcw):
        i = pl.program_id(0)

        @pl.when(i == 0)
        def _():
            dcw[...] = jnp.zeros_like(dcw)

        upad[0:HALO, :] = jnp.where(i > 0, uprev_ref[...], 0.0)
        upad[HALO:, :] = u_ref[...]
        dpad[0:TB, :] = duc_ref[...]
        dpad[TB:, :] = jnp.where(i < nblk - 1, dnext_ref[...], 0.0)
        for lanes in _lane_tiles():
            ushift = _row_shifts(upad[:, lanes])
            dshift = _row_shifts(dpad[:, lanes])
            for r0 in range(0, TB, CONV_ROWS):
                rows = slice(r0, r0 + CONV_ROWS)
                duc = duc_ref[rows, lanes]
                for j, window in _tap_windows(ushift, TAPS_PAST, r0, CONV_ROWS):
                    prod = duc * window
                    dcw[j, :, lanes] += jnp.sum(prod.reshape(CONV_ROWS // SUBLANES, SUBLANES, LANES), axis=0)
                du = jnp.zeros((CONV_ROWS, LANES), F32)
                for j, window in _tap_windows(dshift, TAPS_AHEAD, r0, CONV_ROWS):
                    du = du + cw_ref[j:j + 1, lanes] * window
                ua = ua_ref[rows, lanes].astype(F32)
                sg = _sigmoid(ub_ref[rows, lanes].astype(F32))
                dp_ref[0, rows, lanes] = (du * sg).astype(BF16)
                dp_ref[1, rows, lanes] = (du * ua * sg * (1.0 - sg)).astype(BF16)

        @pl.when(i == nblk - 1)
        def _():
            dcw_ref[...] = jnp.sum(dcw[...], axis=1)

    row = pl.BlockSpec((TB, D), lambda i: (i, 0))
    return pl.pallas_call(
        body, name="conv_bwd_taps", grid=(nblk,),
        in_specs=[row, pl.BlockSpec((HALO, D), lambda i: (jnp.minimum((i + 1) * per, T // HALO - 1), 0)),
                  row, pl.BlockSpec((HALO, D), lambda i: (jnp.maximum(i * per - 1, 0), 0)),
                  row, row, pl.BlockSpec((32, D), lambda i: (0, 0))],
        out_specs=[pl.BlockSpec((2, TB, D), lambda i: (0, i, 0)), pl.BlockSpec((32, D), lambda i: (0, 0))],
        out_shape=[jax.ShapeDtypeStruct((2, T, D), BF16), jax.ShapeDtypeStruct((32, D), F32)],
        scratch_shapes=[pltpu.VMEM((TB + HALO, D), F32), pltpu.VMEM((TB + HALO, D), F32),
                        pltpu.VMEM((32, SUBLANES, D), F32)],
        compiler_params=_params("arbitrary"),
    )(duc, duc, u, u, ua, ub, cw)


def _merge_fwd(x, oa, us, sa, sb, vec, w_ho, w_co, w_mo):
    T = x.shape[0]

    def body(x_ref, oa_ref, us_ref, sa_ref, sb_ref, vec_ref, who_hbm, wco_hbm, wmo_hbm,
             xo_ref, ya_ref, yb_ref, mg_ref, mo_ref, who, wco, wmo):
        @pl.when(pl.program_id(0) == 0)
        def _():
            pltpu.sync_copy(who_hbm, who)
            pltpu.sync_copy(wco_hbm, wco)
            pltpu.sync_copy(wmo_hbm, wmo)

        ya = _nn(oa_ref[...], who[...])
        yb = _nn(us_ref[...], wco[...])
        mg = (sa_ref[...].astype(F32) * ya + sb_ref[...].astype(F32) * yb).astype(BF16)
        mo = _nn(mg, wmo[...])
        xo_ref[...] = x_ref[...] + vec_ref[2:3, :] * mo
        ya_ref[...] = ya.astype(BF16)
        yb_ref[...] = yb.astype(BF16)
        mg_ref[...] = mg
        mo_ref[...] = mo.astype(BF16)

    row = pl.BlockSpec((TB, D), lambda i: (i, 0))
    bf = jax.ShapeDtypeStruct((T, D), BF16)
    wv = pltpu.VMEM((D, D), BF16)
    return pl.pallas_call(
        body, name="merge_fwd", grid=(T // TB,),
        in_specs=[row, row, row, row, row, pl.BlockSpec((8, D), lambda i: (0, 0)), ANY, ANY, ANY],
        out_specs=[row] * 5,
        out_shape=[jax.ShapeDtypeStruct((T, D), F32), bf, bf, bf, bf],
        scratch_shapes=[wv, wv, wv],
        compiler_params=_params("arbitrary"),
    )(x, oa, us, sa, sb, vec, w_ho, w_co, w_mo)


def _merge_bwd(dxo, mo, ya, yb, sa, sb, uc, vec, cvec, w_ho, w_co, w_mo):
    T = dxo.shape[0]

    def body(dxo_ref, mo_ref, ya_ref, yb_ref, sa_ref, sb_ref, uc_ref, vec_ref, cvec_ref, who_hbm, wco_hbm, wmo_hbm,
             dmo_ref, dya_ref, dyb_ref, doa_ref, duc_ref, dp_ref, acc_ref, cacc_ref, who, wco, wmo):
        @pl.when(pl.program_id(0) == 0)
        def _():
            pltpu.sync_copy(who_hbm, who)
            pltpu.sync_copy(wco_hbm, wco)
            pltpu.sync_copy(wmo_hbm, wmo)
            acc_ref[...] = jnp.zeros_like(acc_ref)
            cacc_ref[...] = jnp.zeros_like(cacc_ref)

        dxo = dxo_ref[...]
        acc_ref[2:3, :] += _colsum(mo_ref[...].astype(F32) * dxo)
        dmo = (vec_ref[2:3, :] * dxo).astype(BF16)
        dmo_ref[...] = dmo
        dmg = _nt(dmo, wmo[...])
        sa = sa_ref[...].astype(F32)
        sb = sb_ref[...].astype(F32)
        dya = (sa * dmg).astype(BF16)
        dyb = (sb * dmg).astype(BF16)
        dya_ref[...] = dya
        dyb_ref[...] = dyb
        dp_ref[0] = (dmg * ya_ref[...].astype(F32) * sa * (1.0 - sa)).astype(BF16)
        dp_ref[1] = (dmg * yb_ref[...].astype(F32) * sb * (1.0 - sb)).astype(BF16)
        doa_ref[...] = _nt(dya, who[...]).astype(BF16)
        dus = _nt(dyb, wco[...])
        lg = cvec_ref[1:2, :]
        rstd, z, ul = _ln_fwd(uc_ref[...], lg, cvec_ref[2:3, :])
        dul = dus * _dsilu(ul, _sigmoid(ul))
        cacc_ref[1:2, :] += _colsum(dul * z)
        cacc_ref[2:3, :] += _colsum(dul)
        dz = dul * lg
        duc = rstd * (dz - jnp.mean(dz, axis=-1, keepdims=True) - z * jnp.mean(dz * z, axis=-1, keepdims=True))
        cacc_ref[0:1, :] += _colsum(duc)
        duc_ref[...] = duc

    row = pl.BlockSpec((TB, D), lambda i: (i, 0))
    one = pl.BlockSpec((None, TB, D), lambda i: (0, i, 0))
    vec8 = pl.BlockSpec((8, D), lambda i: (0, 0))
    bf = jax.ShapeDtypeStruct((T, D), BF16)
    bf1 = jax.ShapeDtypeStruct((1, T, D), BF16)
    acc = jax.ShapeDtypeStruct((8, D), F32)
    wv = pltpu.VMEM((D, D), BF16)
    return pl.pallas_call(
        body, name="merge_bwd", grid=(T // TB,),
        in_specs=[row, row, row, row, row, row, row, vec8, vec8, ANY, ANY, ANY],
        out_specs=[one, one, one, row, row, pl.BlockSpec((2, TB, D), lambda i: (0, i, 0)), vec8, vec8],
        out_shape=[bf1, bf1, bf1, bf, jax.ShapeDtypeStruct((T, D), F32), jax.ShapeDtypeStruct((2, T, D), BF16), acc, acc],
        scratch_shapes=[wv, wv, wv],
        compiler_params=_params("arbitrary"),
    )(dxo, mo, ya, yb, sa, sb, uc, vec, cvec, w_ho, w_co, w_mo)


def _pack_rows(parts, total, name, slot=None):
    def body(*refs):
        out = refs[-1]
        out[...] = jnp.zeros_like(out)
        for ref, (_, src, n, dst) in zip(refs[-1 - len(parts):-1], parts):
            out[dst:dst + n, :] = ref[src:src + n, :]

    arrs = [p[0] for p in parts]
    if slot is None:
        return pl.pallas_call(
            body, name=name, in_specs=[pl.BlockSpec(a.shape, lambda: (0, 0)) for a in arrs],
            out_specs=pl.BlockSpec((total, D), lambda: (0, 0)),
            out_shape=jax.ShapeDtypeStruct((total, D), F32),
        )(*arrs)
    return pl.pallas_call(
        body, name=name,
        grid_spec=pltpu.PrefetchScalarGridSpec(
            num_scalar_prefetch=1, grid=(1,),
            in_specs=[pl.BlockSpec(a.shape, lambda i, s: (0, 0)) for a in arrs],
            out_specs=pl.BlockSpec((None, total, D), lambda i, s: (s[0], 0, 0))),
        out_shape=jax.ShapeDtypeStruct((8, total, D), F32),
    )(slot, *arrs)


PACK_ROWS = 56
PACK_AT = {"ada_b": 0, "loss": 9, "norm_ffn1": 10, "norm_mix": 11, "hgrn_g": 12, "conv_b": 13, "conv_ln_g": 14,
           "conv_ln_b": 15, "norm_ffn2": 16, "norm_final": 17, "hgrn_lb": 18, "conv_w": 20}


def _local_step(x, tgt, mod, small, kc, weight, reduce, reduce_small):
    lb = jax.nn.sigmoid(small["hgrn_lb"][0:1] - small["hgrn_lb"][1:2])
    vec1 = _pack_rows([(mod, 0, 3, 0), (small["norm_ffn1"], 0, 1, 3)], 8, "pack_vec1")
    vec2 = _pack_rows([(mod, 3, 3, 0), (small["norm_mix"], 0, 1, 3), (lb, 0, 1, 4), (small["hgrn_g"], 0, 1, 5)],
                      8, "pack_vec2")
    vec3 = _pack_rows([(mod, 6, 3, 0), (small["norm_ffn2"], 0, 1, 3)], 8, "pack_vec3")
    cvec = _pack_rows([(small["conv_b"], 0, 1, 0), (small["conv_ln_g"], 0, 1, 1), (small["conv_ln_b"], 0, 1, 2)],
                      8, "pack_cvec")
    cw = small["conv_w"]
    gvec = _pack_rows([(small["norm_final"], 0, 1, 0)], 8, "pack_gvec")

    wg = {n: weight(n, vec1) for n in ("ffn1_w_in", "ffn1_w_out")}
    x1, h1, a1, b1, s1, f1 = _ffn_fwd(x, vec1, wg["ffn1_w_in"], wg["ffn1_w_out"], "ffn1_fwd")
    wg["mix_w_in"] = weight("mix_w_in", x1)
    h2, qr, g, k, v, og, u, ua, ub, sa, sb = _mix_proj_fwd(x1, vec2, wg["mix_w_in"])
    oa, o, st = _hgrn_fwd(qr, g, k, v, og, vec2)
    us, uc = _conv_fwd(u, cw, cvec)
    wg.update({n: weight(n, us) for n in ("hgrn_w_o", "conv_w_o", "mix_w_out")})
    x2, ya, yb, mg, mo = _merge_fwd(x1, oa, us, sa, sb, vec2, wg["hgrn_w_o"], wg["conv_w_o"], wg["mix_w_out"])
    wg.update({n: weight(n, x2) for n in ("ffn2_w_in", "ffn2_w_out")})
    dx3, h3, a3, b3, s3, f3, acc_head = _ffn_fwd(x2, vec3, wg["ffn2_w_in"], wg["ffn2_w_out"], "ffn2_fwd",
                                                 head=(tgt, gvec))

    dx2, df3, dab3, acc3 = _ffn_bwd(dx3, x2, vec3, a3, b3, f3, wg["ffn2_w_in"], wg["ffn2_w_out"], "ffn2_bwd")
    tok = reduce(("ffn2_w_out", "ffn2_w_in"), [_mm_tn(s3, df3, 1, kc, DFF // NCHIP, "ffn2_dwout"),
                                               _mm_tn(h3, dab3, 1, kc, D, "ffn2_dwin")])
    vec2b = vec2 + tok[0:1, 0:1]
    dmo, dya, dyb, doa, duc, dpc, acc_m, acc_c = _merge_bwd(dx2, mo, ya, yb, sa, sb, uc, vec2b, cvec,
                                                            wg["hgrn_w_o"], wg["conv_w_o"], wg["mix_w_out"])
    tok = reduce(("mix_w_out", "hgrn_w_o", "conv_w_o"),
                 [_mm_tn(mg, dmo, 1, kc, D // NCHIP, "mix_dwout"), _mm_tn(oa, dya, 1, kc, D // NCHIP, "hgrn_dwo"),
                  _mm_tn(us, dyb, 1, kc, D // NCHIP, "conv_dwo")])
    vec2c = vec2 + tok[0:1, 0:1]
    dpb, dcw = _conv_bwd_taps(duc, u, ua, ub, cw)
    dpa, acc_h = _hgrn_bwd(doa, og, qr, g, k, v, o, st, vec2c)
    dx1, acc2 = _mix_proj_bwd(dx2, x1, vec2c, dpa, dpb, dpc, wg["mix_w_in"])
    gmix = _mm_tn(h2, dpa, 2, kc, D, "mix_dwin_a", slabs=NCHIP)
    gmix = _mm_tn(h2, dpb, 2, kc, D, "mix_dwin_b", into=gmix, slab=2, slabs=NCHIP)
    gmix = _mm_tn(h2, dpc, 2, kc, D, "mix_dwin_c", into=gmix, slab=3, slabs=NCHIP)
    tok = reduce(("mix_w_in",), [gmix])
    vec1b = vec1 + tok[0:1, 0:1]
    dx0, df1, dab1, acc1 = _ffn_bwd(dx1, x, vec1b, a1, b1, f1, wg["ffn1_w_in"], wg["ffn1_w_out"], "ffn1_bwd")

    at = PACK_AT
    finish_small = reduce_small([
        (acc1, 0, 3, at["ada_b"]), (acc2, 0, 2, at["ada_b"] + 3), (acc_m, 2, 1, at["ada_b"] + 5),
        (acc3, 0, 3, at["ada_b"] + 6), (acc_head, 1, 1, at["loss"]), (acc1, 3, 1, at["norm_ffn1"]),
        (acc2, 3, 1, at["norm_mix"]), (acc_h, 0, 1, at["hgrn_g"]), (acc_c, 0, 3, at["conv_b"]),
        (acc3, 3, 1, at["norm_ffn2"]), (acc_head, 0, 1, at["norm_final"]), (acc_h, 1, 2, at["hgrn_lb"]),
        (dcw, 0, CONV_K, at["conv_w"])])
    finish_small, tok = finish_small
    last = [_mm_tn(s1, df1, 1, kc, DFF // NCHIP, "ffn1_dwout", after=tok),
            _mm_tn(h1, dab1, 1, kc, D, "ffn1_dwin", after=tok)]
    reduce(("ffn1_w_out", "ffn1_w_in"), last, finish_small(last[1][0]))
    return dx0


BLOCK_BYTES = 3 * 512 * 1024


def _row_block(rows, cols):
    for br in (512, 352, 256, 176, 128, 64, 32, 16, 8):
        if rows % br == 0 and br * cols * 4 <= BLOCK_BYTES:
            return br
    return rows


def _cast_into_slot(w, kc, name, after):
    R, C = w.shape
    br = _row_block(R, C)

    def body(kc_ref, w_ref, after_ref, o_ref):
        o_ref[...] = w_ref[...].astype(BF16)

    return pl.pallas_call(
        body, name=name,
        grid_spec=pltpu.PrefetchScalarGridSpec(
            num_scalar_prefetch=1, grid=(R // br,),
            in_specs=[pl.BlockSpec((br, C), lambda i, kc: (i, 0)), ANY],
            out_specs=pl.BlockSpec((None, br, C), lambda i, kc: (kc[0], i, 0))),
        out_shape=jax.ShapeDtypeStruct((NCHIP, R, C), BF16), compiler_params=_params("parallel"),
    )(kc, w, after)


def _adamw_math(w, g, m, v):
    nm = ADAM_B1 * m + (1.0 - ADAM_B1) * g
    nv = ADAM_B2 * v + (1.0 - ADAM_B2) * (g * g)
    m_hat = nm / (1.0 - ADAM_B1 ** ADAM_STEP)
    v_hat = nv / (1.0 - ADAM_B2 ** ADAM_STEP)
    return -ADAM_LR * (m_hat / (jnp.sqrt(v_hat) + ADAM_EPS) + ADAM_WD * w), nm, nv


def _adamw_rows(params, gsum, rows, after, name):
    n = len(params)

    def body(*refs):
        g_ref = refs[3 * n]
        outs = refs[3 * n + 2:]
        for i, r0 in enumerate(rows):
            w_ref, m_ref, v_ref = refs[3 * i:3 * i + 3]
            g = g_ref[r0:r0 + w_ref.shape[0], :]
            outs[4 * i][...] = g
            outs[4 * i + 1][...], outs[4 * i + 2][...], outs[4 * i + 3][...] = _adamw_math(
                w_ref[...], g, m_ref[...], v_ref[...])

    flat = [a for p in params for a in p]
    full = lambda a: pl.BlockSpec(a.shape, lambda: (0, 0))
    out = pl.pallas_call(
        body, name=name, in_specs=[full(a) for a in flat] + [full(gsum), ANY],
        out_specs=[full(p[0]) for p in params for _ in range(4)],
        out_shape=[jax.ShapeDtypeStruct(p[0].shape, F32) for p in params for _ in range(4)],
    )(*flat, gsum, after)
    return [out[4 * i:4 * i + 4] for i in range(n)]


def _adamw(w, g, m, v, name, after=None, copy_grad=False):
    R, C = w.shape
    br = _row_block(R, C)
    extra = [] if after is None else [after]
    nout = 4 if copy_grad else 3

    def body(w_ref, g_ref, m_ref, v_ref, *rest):
        d_ref, nm_ref, nv_ref = rest[-nout:][:3]
        gv = g_ref[...]
        if copy_grad:
            rest[-1][...] = gv
        d_ref[...], nm_ref[...], nv_ref[...] = _adamw_math(w_ref[...], gv, m_ref[...], v_ref[...])

    blk = pl.BlockSpec((br, C), lambda i: (i, 0))
    out = jax.ShapeDtypeStruct((R, C), F32)
    return pl.pallas_call(
        body, name=name, grid=(R // br,), in_specs=[blk] * 4 + [ANY] * len(extra), out_specs=[blk] * nout,
        out_shape=[out] * nout, compiler_params=_params("parallel"),
    )(w, g, m, v, *extra)


def _coords():
    return lax.axis_index("x"), lax.axis_index("y"), lax.axis_index("c")


def _flip(v, bit):
    return 1 - v if bit else v


def _allgather8(v, name):
    R, C = v.shape

    def body(v_ref, out_ref, send_sems, recv_sems, local_sem):
        x, y, c = _coords()
        me = 4 * x + 2 * y + c
        mine = pltpu.make_async_copy(v_ref, out_ref.at[me], local_sem)
        mine.start()

        def copy(m, block):
            peer = (_flip(x, m & 4), _flip(y, m & 2), _flip(c, m & 1))
            return pltpu.make_async_remote_copy(
                src_ref=v_ref, dst_ref=out_ref.at[block], send_sem=send_sems.at[m - 1],
                recv_sem=recv_sems.at[m - 1], device_id=peer, device_id_type=MESH)

        sends = [copy(m, me) for m in range(1, 8)]
        for cp in sends:
            cp.start()
        for m in range(1, 8):
            sender = 4 * _flip(x, m & 4) + 2 * _flip(y, m & 2) + _flip(c, m & 1)
            copy(m, sender).wait_recv()
        for cp in sends:
            cp.wait_send()
        mine.wait()

    vm = pl.BlockSpec(memory_space=pltpu.VMEM)
    return pl.pallas_call(
        body, name=name, in_specs=[vm], out_specs=vm,
        out_shape=jax.ShapeDtypeStruct((8, R, C), F32),
        scratch_shapes=[pltpu.SemaphoreType.DMA((7,)), pltpu.SemaphoreType.DMA((7,)), pltpu.SemaphoreType.DMA],
    )(v)


HBM = pl.BlockSpec(memory_space=pltpu.HBM)
SEM = pl.BlockSpec(memory_space=pltpu.SEMAPHORE)
EFFECT = pltpu.SideEffectType.DATAFLOW_SIDE_EFFECTING


def _peer8(x, y, c, m):
    px, py, pc = _flip(x, m & 4), _flip(y, m & 2), _flip(c, m & 1)
    return (px, py, pc), 4 * px + 2 * py + pc


def _allgather8_start(blocks, name):
    def body(b_ref, send, recv, thru, token):
        x, y, c = _coords()
        me = 4 * x + 2 * y + c
        for m in range(1, 8):
            peer, _ = _peer8(x, y, c, m)
            pltpu.make_async_remote_copy(src_ref=b_ref.at[me], dst_ref=b_ref.at[me], send_sem=send.at[m - 1],
                                         recv_sem=recv.at[m - 1], device_id=peer, device_id_type=MESH).start()
        token[...] = jnp.zeros_like(token)

    sem = pltpu.SemaphoreType.DMA((7,))
    return pl.pallas_call(
        body, name=name,
        out_shape=[sem, sem, pltpu.HBM(blocks.shape, blocks.dtype), jax.ShapeDtypeStruct((8, 128), F32)],
        in_specs=[HBM], out_specs=[SEM, SEM, HBM, pl.BlockSpec(memory_space=pltpu.VMEM)], input_output_aliases={0: 2},
        compiler_params=pltpu.CompilerParams(has_side_effects=EFFECT),
    )(pltpu.with_memory_space_constraint(blocks, pltpu.HBM))


def _allgather8_wait(blocks, send_sem, recv_sem, after, name):
    def body(b_ref, send, recv, after_ref, thru):
        x, y, c = _coords()
        me = 4 * x + 2 * y + c
        for m in range(1, 8):
            peer, sender = _peer8(x, y, c, m)
            cp = pltpu.make_async_remote_copy(src_ref=b_ref.at[me], dst_ref=b_ref.at[sender], send_sem=send.at[m - 1],
                                              recv_sem=recv.at[m - 1], device_id=peer, device_id_type=MESH)
            cp.wait_send()
            cp.wait_recv()

    return pl.pallas_call(
        body, name=name, out_shape=pltpu.HBM(blocks.shape, blocks.dtype),
        in_specs=[HBM, SEM, SEM, ANY], out_specs=HBM, input_output_aliases={0: 0},
        compiler_params=pltpu.CompilerParams(has_side_effects=EFFECT),
    )(blocks, send_sem, recv_sem, after)


def _chip_peer(x, y, m):
    px, py = _flip(x, m & 2), _flip(y, m & 1)
    return px, py, 2 * px + py


def _core_rows(land, c):
    half = land.shape[1] // 2
    return pl.ds(pl.multiple_of(c * half, 16), half)


def _gather_start(lands, groups, halved, after, name):
    n, ng, na = len(lands), len(groups), len(after)

    def body(*refs):
        ins = refs[:n]
        sends, recvs = refs[n + na:n + na + ng], refs[n + na + ng:n + na + 2 * ng]
        token = refs[n + na + 2 * ng + n]
        x, y, c = _coords()
        k = 2 * x + y
        for gi, grp in enumerate(groups):
            for j, t in enumerate(grp):
                mine = ins[t].at[k, _core_rows(ins[t], c), :] if halved[gi] else ins[t].at[k]
                for m in (1, 2, 3):
                    px, py, _ = _chip_peer(x, y, m)
                    pltpu.make_async_remote_copy(
                        src_ref=mine, dst_ref=mine, send_sem=sends[gi].at[3 * j + m - 1],
                        recv_sem=recvs[gi].at[3 * j + m - 1], device_id=(px, py, c), device_id_type=MESH).start()
        token[...] = jnp.zeros_like(token)

    sems = [pltpu.SemaphoreType.DMA((3 * len(g),)) for g in groups]
    out = pl.pallas_call(
        body, name=name,
        out_shape=sems + sems + [pltpu.HBM(a.shape, a.dtype) for a in lands] + [jax.ShapeDtypeStruct((8, 128), F32)],
        in_specs=[HBM] * n + [ANY] * na,
        out_specs=[SEM] * (2 * ng) + [HBM] * n + [pl.BlockSpec(memory_space=pltpu.VMEM)],
        input_output_aliases={t: 2 * ng + t for t in range(n)},
        compiler_params=pltpu.CompilerParams(has_side_effects=EFFECT),
    )(*[pltpu.with_memory_space_constraint(a, pltpu.HBM) for a in lands], *after)
    return out[:ng], out[ng:2 * ng], out[2 * ng:2 * ng + n], out[2 * ng + n]


def _gather_wait(lands, halved, send_sem, recv_sem, after, name):
    n = len(lands)

    def body(*refs):
        ins, send, recv = refs[:n], refs[n], refs[n + 1]
        x, y, c = _coords()
        k = 2 * x + y
        for j in range(n):
            rows = _core_rows(ins[j], c)
            for m in (1, 2, 3):
                px, py, pk = _chip_peer(x, y, m)
                cp = pltpu.make_async_remote_copy(
                    src_ref=ins[j].at[k, rows, :] if halved else ins[j].at[k],
                    dst_ref=ins[j].at[pk, rows, :] if halved else ins[j].at[pk], send_sem=send.at[3 * j + m - 1],
                    recv_sem=recv.at[3 * j + m - 1], device_id=(px, py, c), device_id_type=MESH)
                cp.wait_send()
                cp.wait_recv()

    return pl.pallas_call(
        body, name=name, out_shape=[pltpu.HBM(a.shape, a.dtype) for a in lands],
        in_specs=[HBM] * n + [SEM, SEM, ANY], out_specs=[HBM] * n,
        input_output_aliases={j: j for j in range(n)},
        compiler_params=pltpu.CompilerParams(has_side_effects=EFFECT),
    )(*lands, send_sem, recv_sem, after)


def _sibling_fill(lands, name):
    n = len(lands)

    def body(*refs):
        ins = refs[:n]
        send_sems, recv_sems = refs[2 * n:]
        x, y, c = _coords()
        sends, recvs = [], []
        for t in range(n):
            for m in (1, 2, 3):
                _, _, pk = _chip_peer(x, y, m)
                for rows, lst in ((_core_rows(ins[t], c), sends), (_core_rows(ins[t], 1 - c), recvs)):
                    lst.append(pltpu.make_async_remote_copy(
                        src_ref=ins[t].at[pk, rows, :], dst_ref=ins[t].at[pk, rows, :],
                        send_sem=send_sems.at[3 * t + m - 1], recv_sem=recv_sems.at[3 * t + m - 1],
                        device_id=(x, y, 1 - c), device_id_type=MESH))
        for cp in sends:
            cp.start()
        for cp in recvs:
            cp.wait_recv()
        for cp in sends:
            cp.wait_send()

    return pl.pallas_call(
        body, name=name, in_specs=[ANY] * n, out_specs=[ANY] * n,
        out_shape=[jax.ShapeDtypeStruct(a.shape, a.dtype) for a in lands],
        input_output_aliases={t: t for t in range(n)},
        scratch_shapes=[pltpu.SemaphoreType.DMA((3 * n,)), pltpu.SemaphoreType.DMA((3 * n,))],
    )(*lands)


def _scatter_start(srcs, name, after=()):
    n, na = len(srcs), len(after)

    def body(*refs):
        ins, lands = refs[:n], refs[n:2 * n]
        send, recv = refs[2 * n + na], refs[2 * n + na + 1]
        token = refs[2 * n + na + 2 + 2 * n]
        x, y, c = _coords()
        k = 2 * x + y
        for t in range(n):
            for m in (1, 2, 3):
                px, py, pk = _chip_peer(x, y, m)
                pltpu.make_async_remote_copy(
                    src_ref=ins[t].at[pk], dst_ref=lands[t].at[k], send_sem=send.at[3 * t + m - 1],
                    recv_sem=recv.at[3 * t + m - 1], device_id=(px, py, c), device_id_type=MESH).start()
        token[...] = jnp.zeros_like(token)

    sem = pltpu.SemaphoreType.DMA((3 * n,))
    hbm = [pltpu.HBM(a.shape, a.dtype) for a in srcs]
    operands = list(srcs) + [lax.empty(a.shape, a.dtype) for a in srcs]
    out = pl.pallas_call(
        body, name=name, out_shape=[sem, sem] + hbm + hbm + [jax.ShapeDtypeStruct((8, 128), F32)],
        in_specs=[HBM] * (2 * n) + [ANY] * na,
        out_specs=[SEM, SEM] + [HBM] * (2 * n) + [pl.BlockSpec(memory_space=pltpu.VMEM)],
        input_output_aliases={t: 2 + t for t in range(2 * n)},
        compiler_params=pltpu.CompilerParams(has_side_effects=EFFECT),
    )(*[pltpu.with_memory_space_constraint(a, pltpu.HBM) for a in operands], *after)
    return out[0], out[1], out[2:2 + n], out[2 + n:2 + 2 * n], out[2 + 2 * n]


def _scatter_wait(srcs, lands, send_sem, recv_sem, after, name):
    n = len(srcs)

    def body(*refs):
        ins, land = refs[:n], refs[n:2 * n]
        send, recv = refs[2 * n], refs[2 * n + 1]
        x, y, c = _coords()
        for t in range(n):
            for m in (1, 2, 3):
                px, py, pk = _chip_peer(x, y, m)
                cp = pltpu.make_async_remote_copy(
                    src_ref=ins[t].at[pk], dst_ref=land[t].at[pk], send_sem=send.at[3 * t + m - 1],
                    recv_sem=recv.at[3 * t + m - 1], device_id=(px, py, c), device_id_type=MESH)
                cp.wait_send()
                cp.wait_recv()

    hbm = [pltpu.HBM(a.shape, a.dtype) for a in srcs]
    out = pl.pallas_call(
        body, name=name, out_shape=hbm + hbm, in_specs=[HBM] * (2 * n) + [SEM, SEM, ANY], out_specs=[HBM] * (2 * n),
        input_output_aliases={t: t for t in range(2 * n)},
        compiler_params=pltpu.CompilerParams(has_side_effects=EFFECT),
    )(*srcs, *lands, send_sem, recv_sem, after)
    return out[:n], out[n:]


def _sum_own_half(g, ra, kc, name):
    _, R, C = g.shape
    half = R // 2
    br = _row_block(half, C)
    nb = half // br

    def body(kc_ref, g_ref, ra_ref, o_ref):
        o_ref[...] = (g_ref[...] + ra_ref[...].astype(F32)).astype(BF16)

    return pl.pallas_call(
        body, name=name,
        grid_spec=pltpu.PrefetchScalarGridSpec(
            num_scalar_prefetch=1, grid=(NCHIP, nb),
            in_specs=[pl.BlockSpec((None, br, C), lambda j, i, kc: (j, kc[1] * nb + i, 0)),
                      pl.BlockSpec((None, br, C), lambda j, i, kc: (j, i, 0))],
            out_specs=pl.BlockSpec((None, br, C), lambda j, i, kc: (j, i, 0))),
        out_shape=jax.ShapeDtypeStruct((NCHIP, half, C), BF16),
        compiler_params=_params("parallel", "parallel"),
    )(kc, g, ra)


def _sum_chips(sa, rb, kc, name, after=None):
    _, half, C = rb.shape
    br = _row_block(half, C)
    nb = half // br
    extra = [] if after is None else [after]

    def body(kc_ref, own_ref, r1_ref, r2_ref, r3_ref, *rest):
        out, obuf, local_sems, send_sems, recv_sem = rest[-5:]
        i = pl.program_id(0)
        slot = i % 2
        x, y, c = _coords()

        def copies(i_, slot_):
            rows = out.at[pl.ds(pl.multiple_of((c * nb + i_) * br, 8), br), :]
            return (pltpu.make_async_copy(obuf.at[slot_], rows, local_sems.at[slot_]),
                    pltpu.make_async_remote_copy(src_ref=obuf.at[slot_], dst_ref=rows, send_sem=send_sems.at[slot_],
                                                 recv_sem=recv_sem, device_id=(x, y, 1 - c), device_id_type=MESH))

        @pl.when(i >= 2)
        def _():
            here, there = copies(i, slot)
            here.wait()
            there.wait_send()

        acc = own_ref[...].astype(F32) + r1_ref[...].astype(F32)
        obuf[slot] = (acc + r2_ref[...].astype(F32)) + r3_ref[...].astype(F32)
        here, there = copies(i, slot)
        here.start()
        there.start()

        @pl.when(i == nb - 1)
        def _():
            for s in range(min(2, nb)):
                here, there = copies(i, (i - s) % 2)
                here.wait()
                there.wait_send()
            theirs = out.at[pl.ds(pl.multiple_of((1 - c) * half, 8), half), :]
            pltpu.make_async_remote_copy(src_ref=theirs, dst_ref=theirs, send_sem=send_sems.at[0], recv_sem=recv_sem,
                                         device_id=(x, y, 1 - c), device_id_type=MESH).wait_recv()

    def slab(m):
        return pl.BlockSpec((None, br, C), lambda i, kc: (kc[0] ^ m, i, 0))

    return pl.pallas_call(
        body, name=name,
        grid_spec=pltpu.PrefetchScalarGridSpec(
            num_scalar_prefetch=1, grid=(nb,),
            in_specs=[slab(0), slab(1), slab(2), slab(3)] + [ANY] * len(extra),
            out_specs=ANY,
            scratch_shapes=[pltpu.VMEM((2, br, C), F32), pltpu.SemaphoreType.DMA((2,)), pltpu.SemaphoreType.DMA((2,)),
                            pltpu.SemaphoreType.DMA]),
        out_shape=jax.ShapeDtypeStruct((2 * half, C), F32), compiler_params=_params("arbitrary"),
    )(kc, sa, rb, rb, rb, *extra)


def _sum8(ga, name):
    _, R, C = ga.shape

    def body(g_ref, o_ref):
        acc = g_ref[0]
        for j in range(1, 8):
            acc = acc + g_ref[j]
        o_ref[...] = acc

    return pl.pallas_call(
        body, name=name, in_specs=[pl.BlockSpec((8, R, C), lambda: (0, 0, 0))],
        out_specs=pl.BlockSpec((R, C), lambda: (0, 0)), out_shape=jax.ShapeDtypeStruct((R, C), F32),
    )(ga)


ADA_COLS = 9 * D // NCHIP
ADA_BLK = 256


def _ada_mod(c_all, ada_w, ada_b, kidx):
    def body(k_ref, c_ref, w_ref, b_ref, o_ref):
        cv = c_ref[...]
        cs = cv * _sigmoid(cv)
        o_ref[...] = jnp.dot(cs, w_ref[...], precision=lax.Precision.HIGHEST,
                             preferred_element_type=F32) + b_ref[...]

    nblk = ADA_COLS // ADA_BLK
    return pl.pallas_call(
        body, name="ada_mod",
        grid_spec=pltpu.PrefetchScalarGridSpec(
            num_scalar_prefetch=1, grid=(nblk,),
            in_specs=[pl.BlockSpec((8, D), lambda j, k: (0, 0)),
                      pl.BlockSpec((D, ADA_BLK), lambda j, k: (0, j)),
                      pl.BlockSpec((1, ADA_BLK), lambda j, k: (0, k[0] * nblk + j))],
            out_specs=pl.BlockSpec((8, ADA_BLK), lambda j, k: (0, j))),
        out_shape=jax.ShapeDtypeStruct((8, ADA_COLS), F32),
        compiler_params=_params("parallel"),
    )(kidx, c_all, ada_w, ada_b)


def _ada_grad(c_all_t, dmod_all, kidx):
    def body(k_ref, ct_ref, dm_ref, o_ref):
        cv = ct_ref[...]
        cs = cv * _sigmoid(cv)
        acc = cs[:, 0:1] * dm_ref[0:1, :]
        for b in range(1, 8):
            acc = acc + cs[:, b:b + 1] * dm_ref[b:b + 1, :]
        o_ref[...] = acc

    nblk = ADA_COLS // ADA_BLK
    return pl.pallas_call(
        body, name="ada_grad",
        grid_spec=pltpu.PrefetchScalarGridSpec(
            num_scalar_prefetch=1, grid=(nblk,),
            in_specs=[pl.BlockSpec((D, 8), lambda j, k: (0, 0)),
                      pl.BlockSpec((8, ADA_BLK), lambda j, k: (0, k[0] * nblk + j))],
            out_specs=pl.BlockSpec((D, ADA_BLK), lambda j, k: (0, j))),
        out_shape=jax.ShapeDtypeStruct((D, ADA_COLS), F32),
        compiler_params=_params("parallel"),
    )(kidx, c_all_t, dmod_all)


BIG = ("ffn1_w_in", "ffn1_w_out", "mix_w_in", "hgrn_w_o", "conv_w_o", "mix_w_out", "ffn2_w_in", "ffn2_w_out")
ROW_SHARDED = ("ffn1_w_out", "hgrn_w_o", "conv_w_o", "mix_w_out", "ffn2_w_out")
GATHER_GROUPS = ((0, 1), (2,), (3, 4, 5), (6, 7))
GATHER_HALVED = (True, True, False, False)
GATHER_STARTS = ((0, 1), (2, 3))
PACK_LEN = {"ada_b": 9, "hgrn_lb": 2}
WEIGHTS = ("ada_w", "ada_b", "norm_ffn1", "ffn1_w_in", "ffn1_w_out", "norm_mix", "mix_w_in", "hgrn_lb", "hgrn_g",
           "hgrn_w_o", "conv_w", "conv_b", "conv_ln_g", "conv_ln_b", "conv_w_o", "mix_w_out", "norm_ffn2",
           "ffn2_w_in", "ffn2_w_out", "norm_final")
PACKED = ("ada_b", "norm_ffn1", "norm_mix", "hgrn_g", "conv_b", "conv_ln_g", "conv_ln_b", "norm_ffn2",
          "norm_final", "hgrn_lb")


def _step(w, m, v, x, c, tgt):
    xi, yi, ci = _coords()
    kidx = (2 * xi + yi).astype(jnp.int32).reshape(1)
    kc = jnp.stack([2 * xi + yi, ci]).astype(jnp.int32)
    me = 4 * xi + 2 * yi + ci

    cq = D // NCHIP
    first = jnp.zeros((40, cq), F32).at[0:CONV_K].set(w["conv_w"][0]).at[32:36].set(c.reshape(NCHIP, cq))
    first_all = _allgather8(first, "gather_c_conv_w")
    c_all = first_all[:, 32:36, :].reshape(8, D)
    mod_cols = _ada_mod(c_all, w["ada_w"][0], w["ada_b"], kidx)
    mod_all = _allgather8(mod_cols, "gather_mod")
    mod = lax.dynamic_slice(mod_all, (0, me, 0), (8, 1, ADA_COLS))[::2].reshape(9, D)
    small = {n: w[n].reshape(-1, D) for n in ("norm_ffn1", "norm_mix", "hgrn_lb", "hgrn_g", "conv_b", "conv_ln_g",
                                              "conv_ln_b", "norm_ffn2", "norm_final")}
    small["conv_w"] = jnp.concatenate([first_all[2 * j, 0:32, :] for j in range(NCHIP)], axis=1)

    lands, sends, recvs = [], [], []
    after = mod
    for part in GATHER_STARTS:
        tensors = [t for gi in part for t in GATHER_GROUPS[gi]]
        cast = [_cast_into_slot(w[BIG[t]][0], kc, "cast_" + BIG[t], after) for t in tensors]
        groups = [tuple(tensors.index(t) for t in GATHER_GROUPS[gi]) for gi in part]
        s, r, thru, after = _gather_start(cast, groups, [GATHER_HALVED[gi] for gi in part], [after],
                                          "gather_weights_start%d" % part[0])
        lands, sends, recvs = lands + list(thru), sends + list(s), recvs + list(r)
    started_all = after
    ready = {}

    def weight(name, after):
        t = BIG.index(name)
        if t not in ready:
            gi = [t in grp for grp in GATHER_GROUPS].index(True)
            grp = GATHER_GROUPS[gi]
            outs = _gather_wait([lands[j] for j in grp], GATHER_HALVED[gi], sends[gi], recvs[gi],
                                started_all if gi == 0 else after, "gather_weights_wait%d" % gi)
            if GATHER_HALVED[gi]:
                outs = _sibling_fill(outs, "gather_weights_fill%d" % gi)
            ready.update(zip(grp, outs))
        return ready[t].reshape(-1, D) if name in ROW_SHARDED else ready[t]

    grads, delta, new_m, new_v = {}, {}, {}, {}
    flight = []
    landed = []

    def settle(after):
        names, sa, rb, send, recv = flight.pop()
        sa, rb = _scatter_wait(sa, rb, send, recv, after, "rs_chip_wait_" + names[0])
        landed.append((names, sa, rb))

    def reduce(names, pairs, after=None):
        gs = [g.reshape(NCHIP, -1, g.shape[-1]) for g, _ in pairs]
        ra = [r.reshape(NCHIP, -1, r.shape[-1]) for _, r in pairs]
        sa = [_sum_own_half(g, r, kc, "rs_sum_pair_" + n) for g, r, n in zip(gs, ra, names)]
        if flight:
            settle(sa[0])
        send, recv, sa, rb, tok = _scatter_start(sa, "rs_chip_start_" + names[0], () if after is None else (after,))
        flight.append((names, sa, rb, send, recv))
        started.append(tok)
        return tok

    def adamw(n, after=None):
        shape = w[n].shape
        two = (shape[-2], shape[-1])
        out = _adamw(w[n].reshape(two), grads[n], m[n].reshape(two), v[n].reshape(two), "adamw_" + n, after,
                     copy_grad=n in BIG)
        g_ = out[3] if n in BIG else grads[n]
        grads[n], delta[n], new_m[n], new_v[n] = (a.reshape(shape) for a in (g_, out[0], out[1], out[2]))
        return out[1]

    def finish(after=None):
        names, sa, rb = landed.pop(0)
        full = [_sum_chips(s, r, kc, "rs_sum_chips_" + n, after) for s, r, n in zip(sa, rb, names)]
        grads.update(zip(names, full))
        return [adamw(n) for n in names][-1]

    started = []

    smalls = []

    def reduce_small(parts):
        blocks = _pack_rows(parts, PACK_ROWS, "pack_small_grads", slot=me.astype(jnp.int32).reshape(1))
        send, recv, blocks, tok = _allgather8_start(blocks, "gather_small_grads_start")

        def finish(after):
            packed_all = _allgather8_wait(blocks, send, recv, after, "gather_small_grads_wait")
            smalls.extend([packed_all, _sum8(packed_all, "sum_small_grads")])
            return smalls[1]

        return finish, tok

    dx = _local_step(x[0], tgt[0], mod, small, kc, weight, reduce, reduce_small)
    packed_all, gsum = smalls
    loss = (0.5 / D) * jnp.sum(gsum[PACK_AT["loss"]])
    dmod_all = packed_all[:, 0:9, :].reshape(8, 9 * D)
    grads["ada_w"] = _ada_grad(c_all.T, dmod_all, kidx)
    grads["conv_w"] = lax.dynamic_slice(gsum, (PACK_AT["conv_w"], kidx[0] * (D // NCHIP)), (CONV_K, D // NCHIP))

    tok = started[-1]
    adamw("ada_w", tok)
    adamw("conv_w")
    two = lambda a, n: a.reshape(PACK_LEN.get(n, 1), D)
    small_out = _adamw_rows([(two(w[n], n), two(m[n], n), two(v[n], n)) for n in PACKED], gsum,
                            [PACK_AT[n] for n in PACKED], tok, "adamw_small")
    for n, quad in zip(PACKED, small_out):
        grads[n], delta[n], new_m[n], new_v[n] = (a.reshape(w[n].shape) for a in quad)
    last = small_out[-1][3]
    while landed:
        last = finish(tok)
    settle(last)
    finish()

    outs = [loss, dx[None]]
    for d in (grads, delta, new_m, new_v):
        outs += [d[n] for n in WEIGHTS]
    return tuple(outs)


def kernel(x, c, ada_w, ada_b, norm_ffn1, ffn1_w_in, ffn1_w_out, norm_mix, mix_w_in, hgrn_lb, hgrn_g, hgrn_w_o, conv_w, conv_b, conv_ln_g, conv_ln_b, conv_w_o, mix_w_out, norm_ffn2, ffn2_w_in, ffn2_w_out, norm_final, loss_target, m_ada_w, m_ada_b, m_norm_ffn1, m_ffn1_w_in, m_ffn1_w_out, m_norm_mix, m_mix_w_in, m_hgrn_lb, m_hgrn_g, m_hgrn_w_o, m_conv_w, m_conv_b, m_conv_ln_g, m_conv_ln_b, m_conv_w_o, m_mix_w_out, m_norm_ffn2, m_ffn2_w_in, m_ffn2_w_out, m_norm_final, v_ada_w, v_ada_b, v_norm_ffn1, v_ffn1_w_in, v_ffn1_w_out, v_norm_mix, v_mix_w_in, v_hgrn_lb, v_hgrn_g, v_hgrn_w_o, v_conv_w, v_conv_b, v_conv_ln_g, v_conv_ln_b, v_conv_w_o, v_mix_w_out, v_norm_ffn2, v_ffn2_w_in, v_ffn2_w_out, v_norm_final):
    w = dict(ada_w=ada_w, ada_b=ada_b, norm_ffn1=norm_ffn1, ffn1_w_in=ffn1_w_in, ffn1_w_out=ffn1_w_out,
             norm_mix=norm_mix, mix_w_in=mix_w_in, hgrn_lb=hgrn_lb, hgrn_g=hgrn_g, hgrn_w_o=hgrn_w_o, conv_w=conv_w,
             conv_b=conv_b, conv_ln_g=conv_ln_g, conv_ln_b=conv_ln_b, conv_w_o=conv_w_o, mix_w_out=mix_w_out,
             norm_ffn2=norm_ffn2, ffn2_w_in=ffn2_w_in, ffn2_w_out=ffn2_w_out, norm_final=norm_final)
    m = dict(ada_w=m_ada_w, ada_b=m_ada_b, norm_ffn1=m_norm_ffn1, ffn1_w_in=m_ffn1_w_in, ffn1_w_out=m_ffn1_w_out,
             norm_mix=m_norm_mix, mix_w_in=m_mix_w_in, hgrn_lb=m_hgrn_lb, hgrn_g=m_hgrn_g, hgrn_w_o=m_hgrn_w_o,
             conv_w=m_conv_w, conv_b=m_conv_b, conv_ln_g=m_conv_ln_g, conv_ln_b=m_conv_ln_b, conv_w_o=m_conv_w_o,
             mix_w_out=m_mix_w_out, norm_ffn2=m_norm_ffn2, ffn2_w_in=m_ffn2_w_in, ffn2_w_out=m_ffn2_w_out,
             norm_final=m_norm_final)
    v = dict(ada_w=v_ada_w, ada_b=v_ada_b, norm_ffn1=v_norm_ffn1, ffn1_w_in=v_ffn1_w_in, ffn1_w_out=v_ffn1_w_out,
             norm_mix=v_norm_mix, mix_w_in=v_mix_w_in, hgrn_lb=v_hgrn_lb, hgrn_g=v_hgrn_g, hgrn_w_o=v_hgrn_w_o,
             conv_w=v_conv_w, conv_b=v_conv_b, conv_ln_g=v_conv_ln_g, conv_ln_b=v_conv_ln_b, conv_w_o=v_conv_w_o,
             mix_w_out=v_mix_w_out, norm_ffn2=v_norm_ffn2, ffn2_w_in=v_ffn2_w_in, ffn2_w_out=v_ffn2_w_out,
             norm_final=v_norm_final)
    return _step(w, m, v, x, c, loss_target)
```

```python
import jax
import jax.numpy as jnp
from jax import lax
from jax.experimental import pallas as pl
from jax.experimental.pallas import tpu as pltpu

F32 = jnp.float32
BF16 = jnp.bfloat16

D = 1024
DFF = 2816
NCHIP = 4
FSH = 2 * DFF // NCHIP
HEADS = 8
DK = 128
CHUNK = 64
CONV_K = 31
HALO = 32
EPS = 1e-6
TB = 256
CB = 2048
DW_TOKENS = 2048
VMEM_LIMIT = 56 * 1024 * 1024

ADAM_LR = 0.001
ADAM_B1 = 0.9
ADAM_B2 = 0.999
ADAM_EPS = 1e-08
ADAM_WD = 0.01
ADAM_STEP = 10

MESH = pl.DeviceIdType.MESH
ANY = pl.BlockSpec(memory_space=pl.ANY)


def _params(*sem):
    return pltpu.CompilerParams(dimension_semantics=sem, vmem_limit_bytes=VMEM_LIMIT)


def _sigmoid(x):
    return 0.5 * jnp.tanh(0.5 * x) + 0.5


def _dsilu(x, sg):
    return sg * (1.0 + x * (1.0 - sg))


def _nt(a, b):
    return lax.dot_general(a, b, (((1,), (1,)), ((), ())), preferred_element_type=F32)


def _tn(a, b):
    return lax.dot_general(a, b, (((0,), (0,)), ((), ())), preferred_element_type=F32)


def _nn(a, b):
    return jnp.dot(a, b, preferred_element_type=F32)


def _colsum(x):
    return jnp.sum(x, axis=0, keepdims=True)


def _rms_fwd(x, gn, sc, sh):
    r = lax.rsqrt(jnp.mean(x * x, axis=-1, keepdims=True) + EPS)
    n = x * r
    h = (n * gn) * (1.0 + sc) + sh
    return r, n, h


def _rms_bwd(dh, r, n, gn, sc, acc_ref):
    acc_ref[0:1, :] += _colsum(dh)
    acc_ref[1:2, :] += _colsum(dh * (n * gn))
    dng = dh * (1.0 + sc)
    acc_ref[3:4, :] += _colsum(dng * n)
    dn = dng * gn
    return r * (dn - n * jnp.mean(dn * n, axis=-1, keepdims=True))


def _loss_head(x, tgt, gf, acc_ref):
    r = lax.rsqrt(jnp.mean(x * x, axis=-1, keepdims=True) + EPS)
    n = x * r
    err = n * gf - tgt
    acc_ref[1:2, :] += _colsum(err * err)
    dy = err * (1.0 / D)
    acc_ref[0:1, :] += _colsum(dy * n)
    dn = dy * gf
    return r * (dn - n * jnp.mean(dn * n, axis=-1, keepdims=True))


def _ffn_fwd(x, vec, w_in, w_out, name, head=None):
    T = x.shape[0]
    nh = 0 if head is None else 2

    def body(x_ref, vec_ref, *rest):
        win_hbm, wout_hbm = rest[nh:nh + 2]
        xo_ref, h_ref, a_ref, b_ref, s_ref, f_ref = rest[nh + 2:nh + 8]
        win, wout = rest[-2:]

        @pl.when(pl.program_id(0) == 0)
        def _():
            pltpu.sync_copy(win_hbm, win)
            pltpu.sync_copy(wout_hbm, wout)
            if head is not None:
                rest[nh + 8][...] = jnp.zeros((8, D), F32)

        x = x_ref[...]
        sh, sc, gate, gn = vec_ref[0:1, :], vec_ref[1:2, :], vec_ref[2:3, :], vec_ref[3:4, :]
        _, _, h = _rms_fwd(x, gn, sc, sh)
        hb = h.astype(BF16)
        h_ref[...] = hb
        f = jnp.zeros((TB, D), F32)
        for j in range(2):
            cols = slice(j * FSH, (j + 1) * FSH)
            a = _nn(hb, win[j])
            b = _nn(hb, win[2 + j])
            s = (a * _sigmoid(a) * b).astype(BF16)
            a_ref[:, cols] = a.astype(BF16)
            b_ref[:, cols] = b.astype(BF16)
            s_ref[:, cols] = s
            f = f + _nn(s, wout[cols, :])
        xo = x + (0.5 * gate) * f
        f_ref[...] = f.astype(BF16)
        if head is None:
            xo_ref[...] = xo
        else:
            xo_ref[...] = _loss_head(xo, rest[0][...], rest[1][0:1, :], rest[nh + 8])

    row = lambda w: pl.BlockSpec((TB, w), lambda i: (i, 0))
    vec8 = pl.BlockSpec((8, D), lambda i: (0, 0))
    acc = [] if head is None else [jax.ShapeDtypeStruct((8, D), F32)]
    return pl.pallas_call(
        body, name=name, grid=(T // TB,),
        in_specs=[row(D), vec8] + ([] if head is None else [row(D), vec8]) + [ANY, ANY],
        out_specs=[row(D), row(D), row(DFF), row(DFF), row(DFF), row(D)] + [vec8] * len(acc),
        out_shape=[jax.ShapeDtypeStruct((T, D), F32), jax.ShapeDtypeStruct((T, D), BF16),
                   jax.ShapeDtypeStruct((T, DFF), BF16), jax.ShapeDtypeStruct((T, DFF), BF16),
                   jax.ShapeDtypeStruct((T, DFF), BF16), jax.ShapeDtypeStruct((T, D), BF16)] + acc,
        scratch_shapes=[pltpu.VMEM((NCHIP, D, FSH), BF16), pltpu.VMEM((DFF, D), BF16)],
        compiler_params=_params("arbitrary"),
    )(x, vec, *([] if head is None else list(head)), w_in, w_out)


def _ffn_bwd(dxo, x, vec, a, b, f, w_in, w_out, name):
    T = x.shape[0]

    def body(dxo_ref, x_ref, vec_ref, a_ref, b_ref, f_ref, win_hbm, wout_hbm,
             dx_ref, df_ref, dab_ref, acc_ref, win, wout):
        @pl.when(pl.program_id(0) == 0)
        def _():
            pltpu.sync_copy(win_hbm, win)
            pltpu.sync_copy(wout_hbm, wout)
            acc_ref[...] = jnp.zeros_like(acc_ref)

        dxo = dxo_ref[...]
        x = x_ref[...]
        sh, sc, gate, gn = vec_ref[0:1, :], vec_ref[1:2, :], vec_ref[2:3, :], vec_ref[3:4, :]
        r, n, _ = _rms_fwd(x, gn, sc, sh)
        acc_ref[2:3, :] += _colsum(0.5 * f_ref[...].astype(F32) * dxo)
        dfb = ((0.5 * gate) * dxo).astype(BF16)
        df_ref[...] = dfb
        dh = jnp.zeros((TB, D), F32)
        for j in range(2):
            cols = slice(j * FSH, (j + 1) * FSH)
            ds = _nt(dfb, wout[cols, :])
            av = a_ref[:, cols].astype(F32)
            bv = b_ref[:, cols].astype(F32)
            sg = _sigmoid(av)
            da = (ds * bv * _dsilu(av, sg)).astype(BF16)
            db = (ds * (av * sg)).astype(BF16)
            dab_ref[j] = da
            dab_ref[2 + j] = db
            dh = dh + _nt(da, win[j]) + _nt(db, win[2 + j])
        dx_ref[...] = dxo + _rms_bwd(dh, r, n, gn, sc, acc_ref)

    row = lambda w: pl.BlockSpec((TB, w), lambda i: (i, 0))
    vec8 = pl.BlockSpec((8, D), lambda i: (0, 0))
    return pl.pallas_call(
        body, name=name, grid=(T // TB,),
        in_specs=[row(D), row(D), vec8, row(DFF), row(DFF), row(D), ANY, ANY],
        out_specs=[row(D), pl.BlockSpec((None, TB, D), lambda i: (0, i, 0)),
                   pl.BlockSpec((NCHIP, TB, FSH), lambda i: (0, i, 0)), vec8],
        out_shape=[jax.ShapeDtypeStruct((T, D), F32), jax.ShapeDtypeStruct((1, T, D), BF16),
                   jax.ShapeDtypeStruct((NCHIP, T, FSH), BF16), jax.ShapeDtypeStruct((8, D), F32)],
        scratch_shapes=[pltpu.VMEM((NCHIP, D, FSH), BF16), pltpu.VMEM((DFF, D), BF16)],
        compiler_params=_params("arbitrary"),
    )(dxo, x, vec, a, b, f, w_in, w_out)


def _mm_tn(a, b3, hp, kc, shard_rows, name, into=None, slab=0, slabs=None, after=None):
    T, M = a.shape
    P, _, N = b3.shape
    tm = M if M <= 1408 else M // 2
    tk = min(T, DW_TOKENS if P * (M // tm) > 1 else DW_TOKENS // 2)
    nk = T // tk
    ni = M // tm
    slabs = P // hp if slabs is None else slabs
    half = shard_rows // 2
    extra = ([] if into is None else list(into)) + ([] if after is None else [after])

    def body(kc_ref, a_ref, b_ref, *rest):
        o_ref, ra_ref, hbuf, send_sems, recv_sem = rest[-5:]
        p, i, k = pl.program_id(0), pl.program_id(1), pl.program_id(2)
        x, y, c = _coords()
        step = p * ni + i
        slot = step % 2

        def send(p_, i_, slot_):
            dst = ra_ref.at[slab + p_ // hp, pl.ds(pl.multiple_of(i_ * (tm // 2), 8), tm // 2),
                            pl.ds(pl.multiple_of((p_ % hp) * N, LANES), N)]
            return pltpu.make_async_remote_copy(
                src_ref=hbuf.at[slot_], dst_ref=dst, send_sem=send_sems.at[slot_], recv_sem=recv_sem,
                device_id=(x, y, 1 - c), device_id_type=MESH)

        @pl.when(k == 0)
        def _():
            o_ref[...] = jnp.zeros_like(o_ref)

        o_ref[...] += _tn(a_ref[...], b_ref[...])

        @pl.when(k == nk - 1)
        def _():
            @pl.when(step >= 2)
            def _():
                send(p, i, slot).wait_send()

            for j in range(tm // shard_rows):
                start = pl.multiple_of(j * shard_rows + (1 - kc_ref[1]) * half, 8)
                hbuf[slot, j * half:(j + 1) * half, :] = o_ref[pl.ds(start, half), :].astype(BF16)
            send(p, i, slot).start()

        @pl.when((step == P * ni - 1) & (k == nk - 1))
        def _():
            for s in range(min(2, P * ni)):
                send(p, i, (step - s) % 2).wait_send()
            mine = ra_ref.at[slab:slab + P // hp]
            pltpu.make_async_remote_copy(src_ref=mine, dst_ref=mine, send_sem=send_sems.at[0], recv_sem=recv_sem,
                                         device_id=(x, y, 1 - c), device_id_type=MESH).wait_recv()

    return pl.pallas_call(
        body, name=name,
        grid_spec=pltpu.PrefetchScalarGridSpec(
            num_scalar_prefetch=1, grid=(P, ni, nk),
            in_specs=[pl.BlockSpec((tk, tm), lambda p, i, k, kc: (k, i)),
                      pl.BlockSpec((None, tk, N), lambda p, i, k, kc: (p, k, 0))] + [ANY] * len(extra),
            out_specs=[pl.BlockSpec((None, tm, N), lambda p, i, k, kc: (slab + p // hp, i, p % hp)), ANY],
            scratch_shapes=[pltpu.VMEM((2, tm // 2, N), BF16), pltpu.SemaphoreType.DMA((2,)),
                            pltpu.SemaphoreType.DMA]),
        out_shape=[jax.ShapeDtypeStruct((slabs, M, hp * N), F32), jax.ShapeDtypeStruct((slabs, M // 2, hp * N), BF16)],
        input_output_aliases={} if into is None else {3: 0, 4: 1},
        compiler_params=_params("arbitrary", "arbitrary", "arbitrary"),
    )(kc, a, b3, *extra)


def _mix_proj_fwd(x, vec, w_in):
    T = x.shape[0]

    def body(x_ref, vec_ref, w_hbm, h_ref, qr_ref, g_ref, k_ref, v_ref, og_ref, u_ref, ua_ref, ub_ref,
             sa_ref, sb_ref, w):
        @pl.when(pl.program_id(0) == 0)
        def _():
            pltpu.sync_copy(w_hbm, w)

        x = x_ref[...]
        sh, sc, gn, lb = vec_ref[0:1, :], vec_ref[1:2, :], vec_ref[3:4, :], vec_ref[4:5, :]
        _, _, h = _rms_fwd(x, gn, sc, sh)
        hb = h.astype(BF16)
        h_ref[...] = hb
        p = _nn(hb, w[0])
        qr_ref[...] = p[:, :D].astype(BF16)
        fg = lb + (1.0 - lb) * _sigmoid(p[:, D:])
        g_ref[...] = jnp.log(fg)
        k_ref[...] = (1.0 - fg).astype(BF16)
        p = _nn(hb, w[1])
        v_ref[...] = p[:, :D].astype(BF16)
        og_ref[...] = p[:, D:].astype(BF16)
        p = _nn(hb, w[2])
        ua, ub = p[:, :D], p[:, D:]
        u_ref[...] = ua * _sigmoid(ub)
        ua_ref[...] = ua.astype(BF16)
        ub_ref[...] = ub.astype(BF16)
        p = _nn(hb, w[3])
        sa_ref[...] = _sigmoid(p[:, :D]).astype(BF16)
        sb_ref[...] = _sigmoid(p[:, D:]).astype(BF16)

    row = pl.BlockSpec((TB, D), lambda i: (i, 0))
    bf = jax.ShapeDtypeStruct((T, D), BF16)
    f32 = jax.ShapeDtypeStruct((T, D), F32)
    return pl.pallas_call(
        body, name="mix_proj_fwd", grid=(T // TB,),
        in_specs=[row, pl.BlockSpec((8, D), lambda i: (0, 0)), ANY],
        out_specs=[row] * 11,
        out_shape=[bf, bf, f32, bf, bf, bf, f32, bf, bf, bf, bf],
        scratch_shapes=[pltpu.VMEM((NCHIP, D, 2 * D), BF16)],
        compiler_params=_params("arbitrary"),
    )(x, vec, w_in)


def _mix_proj_bwd(dxo, x, vec, dpa, dpb, dpc, w_in):
    T = x.shape[0]

    def body(dxo_ref, x_ref, vec_ref, dpa_ref, dpb_ref, dpc_ref, w_hbm, dx_ref, acc_ref, w):
        @pl.when(pl.program_id(0) == 0)
        def _():
            pltpu.sync_copy(w_hbm, w)
            acc_ref[...] = jnp.zeros_like(acc_ref)

        x = x_ref[...]
        sh, sc, gn = vec_ref[0:1, :], vec_ref[1:2, :], vec_ref[3:4, :]
        r, n, _ = _rms_fwd(x, gn, sc, sh)
        dh = jnp.zeros((TB, D), F32)
        for p in range(8):
            src = dpa_ref[p] if p < 4 else (dpb_ref[p - 4] if p < 6 else dpc_ref[p - 6])
            dh = dh + _nt(src, w[p // 2, :, (p % 2) * D:(p % 2 + 1) * D])
        dx_ref[...] = dxo_ref[...] + _rms_bwd(dh, r, n, gn, sc, acc_ref)

    row = pl.BlockSpec((TB, D), lambda i: (i, 0))
    vec8 = pl.BlockSpec((8, D), lambda i: (0, 0))
    stack = lambda k: pl.BlockSpec((k, TB, D), lambda i: (0, i, 0))
    return pl.pallas_call(
        body, name="mix_proj_bwd", grid=(T // TB,),
        in_specs=[row, row, vec8, stack(4), stack(2), stack(2), ANY],
        out_specs=[row, vec8],
        out_shape=[jax.ShapeDtypeStruct((T, D), F32), jax.ShapeDtypeStruct((8, D), F32)],
        scratch_shapes=[pltpu.VMEM((NCHIP, D, 2 * D), BF16)],
        compiler_params=_params("arbitrary"),
    )(dxo, x, vec, dpa, dpb, dpc, w_in)


def _tri(lower):
    r = lax.broadcasted_iota(jnp.int32, (CHUNK, CHUNK), 0)
    c = lax.broadcasted_iota(jnp.int32, (CHUNK, CHUNK), 1)
    return (c <= r) if lower else (c >= r)


def _cumsum_rows(mask, g):
    hi = g.astype(BF16)
    rest = g - hi.astype(F32)
    mid = rest.astype(BF16)
    low = (rest - mid.astype(F32)).astype(BF16)
    n = g.shape[1]
    p = _nn(mask.astype(BF16), jnp.concatenate([hi, mid, low], axis=1))
    return (p[:, 2 * n:] + p[:, n:2 * n]) + p[:, :n]


def _chunk_decay(low, g, nck):
    bs, mids, lasts = [], [], []
    for c in range(nck):
        gc = g[c * CHUNK:(c + 1) * CHUNK]
        bs.append(_cumsum_rows(low, gc))
        mids.append(_colsum(gc[0:CHUNK // 2]))
        lasts.append(_colsum(gc))
    spread = lambda rows: jnp.concatenate([jnp.broadcast_to(r, (CHUNK, DK)) for r in rows], axis=0)
    return jnp.concatenate(bs, axis=0), spread(mids), spread(lasts), lasts


def _hgrn_fwd(qr, g, k, v, og, vec):
    T = qr.shape[0]
    nck = CB // CHUNK

    def body(qr_ref, g_ref, k_ref, v_ref, og_ref, vec_ref, out_ref, o_ref, st_ref, state):
        @pl.when(pl.program_id(1) == 0)
        def _():
            state[...] = jnp.zeros_like(state)

        low = _tri(True)
        qv = qr_ref[...].astype(F32)
        q = qv * _sigmoid(qv) * (DK ** -0.5)
        kk = k_ref[...].astype(F32)
        vb = v_ref[...]
        b, mid, last, lasts = _chunk_decay(low, g_ref[...], nck)
        qt = (q * jnp.exp(b - mid)).astype(BF16)
        kt = (kk * jnp.exp(mid - b)).astype(BF16)
        qe = (q * jnp.exp(b)).astype(BF16)
        kd = (kk * jnp.exp(last - b)).astype(BF16)
        intra, grow = [], []
        for c in range(nck):
            r = slice(c * CHUNK, (c + 1) * CHUNK)
            att = jnp.where(low, _nt(qt[r], kt[r]), 0.0).astype(BF16)
            intra.append(_nn(att, vb[r]))
            grow.append(_tn(vb[r], kd[r]))
        st = state[...]
        inter = []
        for c in range(nck):
            stb = st.astype(BF16)
            st_ref[c] = stb
            inter.append(_nt(qe[c * CHUNK:(c + 1) * CHUNK], stb))
            st = st * jnp.exp(lasts[c]) + grow[c]
        state[...] = st
        o = jnp.concatenate(intra, axis=0) + jnp.concatenate(inter, axis=0)
        o_ref[...] = o
        ogv = og_ref[...].astype(F32)
        rms = lax.rsqrt(jnp.mean(o * o, axis=-1, keepdims=True) + EPS)
        out_ref[...] = (o * rms * vec_ref[5:6, :] * (ogv * _sigmoid(ogv))).astype(BF16)

    blk = pl.BlockSpec((CB, DK), lambda h, i: (i, h))
    return pl.pallas_call(
        body, name="hgrn_fwd", grid=(HEADS, T // CB),
        in_specs=[blk, blk, blk, blk, blk, pl.BlockSpec((8, DK), lambda h, i: (0, h))],
        out_specs=[blk, blk, pl.BlockSpec((None, nck, DK, DK), lambda h, i: (h, i, 0, 0))],
        out_shape=[jax.ShapeDtypeStruct((T, D), BF16), jax.ShapeDtypeStruct((T, D), F32),
                   jax.ShapeDtypeStruct((HEADS, T // CHUNK, DK, DK), BF16)],
        scratch_shapes=[pltpu.VMEM((DK, DK), F32)],
        compiler_params=_params("parallel", "arbitrary"),
    )(qr, g, k, v, og, vec)


def _hgrn_bwd(dout, og, qr, g, k, v, o, st, vec):
    T = qr.shape[0]
    nck = CB // CHUNK
    nb = T // CB

    def body(dout_ref, og_ref, qr_ref, g_ref, k_ref, v_ref, o_ref, st_ref, vec_ref,
             dp_ref, acc_ref, dstate):
        @pl.when(pl.program_id(1) == 0)
        def _():
            dstate[...] = jnp.zeros_like(dstate)
            acc_ref[...] = jnp.zeros_like(acc_ref)

        o = o_ref[...]
        ogv = og_ref[...].astype(F32)
        dout = dout_ref[...].astype(F32)
        hg = vec_ref[5:6, :]
        sgo = _sigmoid(ogv)
        rms = lax.rsqrt(jnp.mean(o * o, axis=-1, keepdims=True) + EPS)
        ohat = o * rms
        dp_ref[3] = (dout * (ohat * hg) * _dsilu(ogv, sgo)).astype(BF16)
        don = dout * (ogv * sgo)
        acc_ref[0:1, :] += _colsum(don * ohat)
        dohat = don * hg
        dob = (rms * (dohat - ohat * jnp.mean(dohat * ohat, axis=-1, keepdims=True))).astype(BF16)

        low = _tri(True)
        upp = _tri(False)
        lb = vec_ref[4:5, :]
        qv = qr_ref[...].astype(F32)
        sgq = _sigmoid(qv)
        q = qv * sgq * (DK ** -0.5)
        kk = k_ref[...].astype(F32)
        vb = v_ref[...]
        gv = g_ref[...]
        b, mid, last, lasts = _chunk_decay(low, gv, nck)
        eq = jnp.exp(b - mid)
        ek = jnp.exp(mid - b)
        eb = jnp.exp(b)
        ed = jnp.exp(last - b)
        qtb, ktb, qeb, kdb = ((t).astype(BF16) for t in (q * eq, kk * ek, q * eb, kk * ed))
        rows = [slice(c * CHUNK, (c + 1) * CHUNK) for c in range(nck)]

        dv1, dqt, dkt, dqe, grow = [], [], [], [], []
        for c, r in enumerate(rows):
            att = jnp.where(low, _nt(qtb[r], ktb[r]), 0.0).astype(BF16)
            datt = jnp.where(low, _nt(dob[r], vb[r]), 0.0).astype(BF16)
            dv1.append(_tn(att, dob[r]))
            dqt.append(_nn(datt, ktb[r]))
            dkt.append(_tn(datt, qtb[r]))
            dqe.append(_nn(dob[r], st_ref[c]))
            grow.append(_tn(dob[r], qeb[r]))
        ds = dstate[...]
        ds1b, dl_state = [None] * nck, [None] * nck
        for c in reversed(range(nck)):
            el = jnp.exp(lasts[c])
            ds1b[c] = ds.astype(BF16)
            dl_state[c] = el * _colsum(ds * st_ref[c].astype(F32))
            ds = ds * el + grow[c]
        dstate[...] = ds
        dkd = jnp.concatenate([_nn(vb[r], ds1b[c]) for c, r in enumerate(rows)], axis=0)
        dv = jnp.concatenate(dv1, axis=0) + jnp.concatenate([_nt(kdb[r], ds1b[c]) for c, r in enumerate(rows)], axis=0)
        dqt, dkt, dqe = (jnp.concatenate(t, axis=0) for t in (dqt, dkt, dqe))
        dq = dqt * eq + dqe * eb
        dk = dkt * ek + dkd * ed
        dkdkd = dkd * kdb.astype(F32)
        db = dqt * qtb.astype(F32) - dkt * ktb.astype(F32) + dqe * qeb.astype(F32) - dkdkd
        dg = jnp.concatenate([_cumsum_rows(upp, db[r]) + (_colsum(dkdkd[r]) + dl_state[c])
                              for c, r in enumerate(rows)], axis=0)
        fg = jnp.exp(gv)
        dfg = dg * jnp.exp(-gv) - dk
        one_m_sig = (1.0 - fg) * (1.0 / (1.0 - lb))
        dp_ref[0] = (dq * (DK ** -0.5) * _dsilu(qv, sgq)).astype(BF16)
        dp_ref[1] = (dfg * (fg - lb) * one_m_sig).astype(BF16)
        dp_ref[2] = dv.astype(BF16)
        dlb = _colsum(dfg * one_m_sig) * (lb * (1.0 - lb))
        acc_ref[1:2, :] += dlb
        acc_ref[2:3, :] -= dlb

    blk = pl.BlockSpec((CB, DK), lambda h, i: (nb - 1 - i, h))
    return pl.pallas_call(
        body, name="hgrn_bwd", grid=(HEADS, nb),
        in_specs=[blk, blk, blk, blk, blk, blk, blk,
                  pl.BlockSpec((None, nck, DK, DK), lambda h, i: (h, nb - 1 - i, 0, 0)),
                  pl.BlockSpec((8, DK), lambda h, i: (0, h))],
        out_specs=[pl.BlockSpec((4, CB, DK), lambda h, i: (0, nb - 1 - i, h)),
                   pl.BlockSpec((8, DK), lambda h, i: (0, h))],
        out_shape=[jax.ShapeDtypeStruct((4, T, D), BF16), jax.ShapeDtypeStruct((8, D), F32)],
        scratch_shapes=[pltpu.VMEM((DK, DK), F32)],
        compiler_params=_params("parallel", "arbitrary"),
    )(dout, og, qr, g, k, v, o, st, vec)


def _ln_fwd(uc, lg, lbias):
    mu = jnp.mean(uc, axis=-1, keepdims=True)
    xc = uc - mu
    rstd = lax.rsqrt(jnp.mean(xc * xc, axis=-1, keepdims=True) + EPS)
    z = xc * rstd
    return rstd, z, z * lg + lbias


LANES = 128
SUBLANES = 8
CONV_ROWS = 64


def _lane_tiles():
    return [slice(l * LANES, (l + 1) * LANES) for l in range(D // LANES)]


def _row_shifts(x):
    n = x.shape[0]
    return [x] + [pltpu.roll(x, n - r, axis=0) for r in range(1, SUBLANES)]


TAPS_PAST = tuple(HALO - (CONV_K - 1) + j for j in range(CONV_K))
TAPS_AHEAD = tuple(CONV_K - 1 - j for j in range(CONV_K))


def _tap_windows(shifted, starts, r0, rows):
    for r in range(SUBLANES):
        taps = [(j, s // SUBLANES) for j, s in enumerate(starts) if s % SUBLANES == r]
        if not taps:
            continue
        lo = min(a for _, a in taps)
        hi = max(a for _, a in taps)
        span = shifted[r][r0 + lo * SUBLANES:r0 + hi * SUBLANES + rows]
        for j, a in taps:
            yield j, span[(a - lo) * SUBLANES:(a - lo) * SUBLANES + rows]


def _conv_fwd(u, cw, cvec):
    T = u.shape[0]
    per = TB // HALO

    def body(u_ref, halo_ref, cw_ref, cvec_ref, us_ref, uc_ref, pad):
        i = pl.program_id(0)
        pad[0:HALO, :] = jnp.where(i > 0, halo_ref[...], 0.0)
        pad[HALO:, :] = u_ref[...]
        for lanes in _lane_tiles():
            shifted = _row_shifts(pad[:, lanes])
            taps = cw_ref[:, lanes]
            for r0 in range(0, TB, CONV_ROWS):
                acc = jnp.broadcast_to(cvec_ref[0:1, lanes], (CONV_ROWS, LANES))
                for j, window in _tap_windows(shifted, TAPS_PAST, r0, CONV_ROWS):
                    acc = acc + taps[j:j + 1] * window
                uc_ref[r0:r0 + CONV_ROWS, lanes] = acc
        _, _, ul = _ln_fwd(uc_ref[...], cvec_ref[1:2, :], cvec_ref[2:3, :])
        us_ref[...] = (ul * _sigmoid(ul)).astype(BF16)

    row = pl.BlockSpec((TB, D), lambda i: (i, 0))
    return pl.pallas_call(
        body, name="conv_fwd", grid=(T // TB,),
        in_specs=[row, pl.BlockSpec((HALO, D), lambda i: (jnp.maximum(i * per - 1, 0), 0)),
                  pl.BlockSpec((32, D), lambda i: (0, 0)), pl.BlockSpec((8, D), lambda i: (0, 0))],
        out_specs=[row, row],
        out_shape=[jax.ShapeDtypeStruct((T, D), BF16), jax.ShapeDtypeStruct((T, D), F32)],
        scratch_shapes=[pltpu.VMEM((TB + HALO, D), F32)],
        compiler_params=_params("parallel"),
    )(u, u, cw, cvec)


def _conv_bwd_taps(duc, u, ua, ub, cw):
    T = u.shape[0]
    per = TB // HALO
    nblk = T // TB

    def body(duc_ref, dnext_ref, u_ref, uprev_ref, ua_ref, ub_ref, cw_ref, dp_ref, dcw_ref, upad, dpad, dcw):
        i = pl.program_id(0)

        @pl.when(i == 0)
        def _():
            dcw[...] = jnp.zeros_like(dcw)

        upad[0:HALO, :] = jnp.where(i > 0, uprev_ref[...], 0.0)
        upad[HALO:, :] = u_ref[...]
        dpad[0:TB, :] = duc_ref[...]
        dpad[TB:, :] = jnp.where(i < nblk - 1, dnext_ref[...], 0.0)
        for lanes in _lane_tiles():
            ushift = _row_shifts(upad[:, lanes])
            dshift = _row_shifts(dpad[:, lanes])
            for r0 in range(0, TB, CONV_ROWS):
                rows = slice(r0, r0 + CONV_ROWS)
                duc = duc_ref[rows, lanes]
                for j, window in _tap_windows(ushift, TAPS_PAST, r0, CONV_ROWS):
                    prod = duc * window
                    dcw[j, :, lanes] += jnp.sum(prod.reshape(CONV_ROWS // SUBLANES, SUBLANES, LANES), axis=0)
                du = jnp.zeros((CONV_ROWS, LANES), F32)
                for j, window in _tap_windows(dshift, TAPS_AHEAD, r0, CONV_ROWS):
                    du = du + cw_ref[j:j + 1, lanes] * window
                ua = ua_ref[rows, lanes].astype(F32)
                sg = _sigmoid(ub_ref[rows, lanes].astype(F32))
                dp_ref[0, rows, lanes] = (du * sg).astype(BF16)
                dp_ref[1, rows, lanes] = (du * ua * sg * (1.0 - sg)).astype(BF16)

        @pl.when(i == nblk - 1)
        def _():
            dcw_ref[...] = jnp.sum(dcw[...], axis=1)

    row = pl.BlockSpec((TB, D), lambda i: (i, 0))
    return pl.pallas_call(
        body, name="conv_bwd_taps", grid=(nblk,),
        in_specs=[row, pl.BlockSpec((HALO, D), lambda i: (jnp.minimum((i + 1) * per, T // HALO - 1), 0)),
                  row, pl.BlockSpec((HALO, D), lambda i: (jnp.maximum(i * per - 1, 0), 0)),
                  row, row, pl.BlockSpec((32, D), lambda i: (0, 0))],
        out_specs=[pl.BlockSpec((2, TB, D), lambda i: (0, i, 0)), pl.BlockSpec((32, D), lambda i: (0, 0))],
        out_shape=[jax.ShapeDtypeStruct((2, T, D), BF16), jax.ShapeDtypeStruct((32, D), F32)],
        scratch_shapes=[pltpu.VMEM((TB + HALO, D), F32), pltpu.VMEM((TB + HALO, D), F32),
                        pltpu.VMEM((32, SUBLANES, D), F32)],
        compiler_params=_params("arbitrary"),
    )(duc, duc, u, u, ua, ub, cw)


def _merge_fwd(x, oa, us, sa, sb, vec, w_ho, w_co, w_mo):
    T = x.shape[0]

    def body(x_ref, oa_ref, us_ref, sa_ref, sb_ref, vec_ref, who_hbm, wco_hbm, wmo_hbm,
             xo_ref, ya_ref, yb_ref, mg_ref, mo_ref, who, wco, wmo):
        @pl.when(pl.program_id(0) == 0)
        def _():
            pltpu.sync_copy(who_hbm, who)
            pltpu.sync_copy(wco_hbm, wco)
            pltpu.sync_copy(wmo_hbm, wmo)

        ya = _nn(oa_ref[...], who[...])
        yb = _nn(us_ref[...], wco[...])
        mg = (sa_ref[...].astype(F32) * ya + sb_ref[...].astype(F32) * yb).astype(BF16)
        mo = _nn(mg, wmo[...])
        xo_ref[...] = x_ref[...] + vec_ref[2:3, :] * mo
        ya_ref[...] = ya.astype(BF16)
        yb_ref[...] = yb.astype(BF16)
        mg_ref[...] = mg
        mo_ref[...] = mo.astype(BF16)

    row = pl.BlockSpec((TB, D), lambda i: (i, 0))
    bf = jax.ShapeDtypeStruct((T, D), BF16)
    wv = pltpu.VMEM((D, D), BF16)
    return pl.pallas_call(
        body, name="merge_fwd", grid=(T // TB,),
        in_specs=[row, row, row, row, row, pl.BlockSpec((8, D), lambda i: (0, 0)), ANY, ANY, ANY],
        out_specs=[row] * 5,
        out_shape=[jax.ShapeDtypeStruct((T, D), F32), bf, bf, bf, bf],
        scratch_shapes=[wv, wv, wv],
        compiler_params=_params("arbitrary"),
    )(x, oa, us, sa, sb, vec, w_ho, w_co, w_mo)


def _merge_bwd(dxo, mo, ya, yb, sa, sb, uc, vec, cvec, w_ho, w_co, w_mo):
    T = dxo.shape[0]

    def body(dxo_ref, mo_ref, ya_ref, yb_ref, sa_ref, sb_ref, uc_ref, vec_ref, cvec_ref, who_hbm, wco_hbm, wmo_hbm,
             dmo_ref, dya_ref, dyb_ref, doa_ref, duc_ref, dp_ref, acc_ref, cacc_ref, who, wco, wmo):
        @pl.when(pl.program_id(0) == 0)
        def _():
            pltpu.sync_copy(who_hbm, who)
            pltpu.sync_copy(wco_hbm, wco)
            pltpu.sync_copy(wmo_hbm, wmo)
            acc_ref[...] = jnp.zeros_like(acc_ref)
            cacc_ref[...] = jnp.zeros_like(cacc_ref)

        dxo = dxo_ref[...]
        acc_ref[2:3, :] += _colsum(mo_ref[...].astype(F32) * dxo)
        dmo = (vec_ref[2:3, :] * dxo).astype(BF16)
        dmo_ref[...] = dmo
        dmg = _nt(dmo, wmo[...])
        sa = sa_ref[...].astype(F32)
        sb = sb_ref[...].astype(F32)
        dya = (sa * dmg).astype(BF16)
        dyb = (sb * dmg).astype(BF16)
        dya_ref[...] = dya
        dyb_ref[...] = dyb
        dp_ref[0] = (dmg * ya_ref[...].astype(F32) * sa * (1.0 - sa)).astype(BF16)
        dp_ref[1] = (dmg * yb_ref[...].astype(F32) * sb * (1.0 - sb)).astype(BF16)
        doa_ref[...] = _nt(dya, who[...]).astype(BF16)
        dus = _nt(dyb, wco[...])
        lg = cvec_ref[1:2, :]
        rstd, z, ul = _ln_fwd(uc_ref[...], lg, cvec_ref[2:3, :])
        dul = dus * _dsilu(ul, _sigmoid(ul))
        cacc_ref[1:2, :] += _colsum(dul * z)
        cacc_ref[2:3, :] += _colsum(dul)
        dz = dul * lg
        duc = rstd * (dz - jnp.mean(dz, axis=-1, keepdims=True) - z * jnp.mean(dz * z, axis=-1, keepdims=True))
        cacc_ref[0:1, :] += _colsum(duc)
        duc_ref[...] = duc

    row = pl.BlockSpec((TB, D), lambda i: (i, 0))
    one = pl.BlockSpec((None, TB, D), lambda i: (0, i, 0))
    vec8 = pl.BlockSpec((8, D), lambda i: (0, 0))
    bf = jax.ShapeDtypeStruct((T, D), BF16)
    bf1 = jax.ShapeDtypeStruct((1, T, D), BF16)
    acc = jax.ShapeDtypeStruct((8, D), F32)
    wv = pltpu.VMEM((D, D), BF16)
    return pl.pallas_call(
        body, name="merge_bwd", grid=(T // TB,),
        in_specs=[row, row, row, row, row, row, row, vec8, vec8, ANY, ANY, ANY],
        out_specs=[one, one, one, row, row, pl.BlockSpec((2, TB, D), lambda i: (0, i, 0)), vec8, vec8],
        out_shape=[bf1, bf1, bf1, bf, jax.ShapeDtypeStruct((T, D), F32), jax.ShapeDtypeStruct((2, T, D), BF16), acc, acc],
        scratch_shapes=[wv, wv, wv],
        compiler_params=_params("arbitrary"),
    )(dxo, mo, ya, yb, sa, sb, uc, vec, cvec, w_ho, w_co, w_mo)


def _pack_rows(parts, total, name, slot=None):
    def body(*refs):
        out = refs[-1]
        out[...] = jnp.zeros_like(out)
        for ref, (_, src, n, dst) in zip(refs[-1 - len(parts):-1], parts):
            out[dst:dst + n, :] = ref[src:src + n, :]

    arrs = [p[0] for p in parts]
    if slot is None:
        return pl.pallas_call(
            body, name=name, in_specs=[pl.BlockSpec(a.shape, lambda: (0, 0)) for a in arrs],
            out_specs=pl.BlockSpec((total, D), lambda: (0, 0)),
            out_shape=jax.ShapeDtypeStruct((total, D), F32),
        )(*arrs)
    return pl.pallas_call(
        body, name=name,
        grid_spec=pltpu.PrefetchScalarGridSpec(
            num_scalar_prefetch=1, grid=(1,),
            in_specs=[pl.BlockSpec(a.shape, lambda i, s: (0, 0)) for a in arrs],
            out_specs=pl.BlockSpec((None, total, D), lambda i, s: (s[0], 0, 0))),
        out_shape=jax.ShapeDtypeStruct((8, total, D), F32),
    )(slot, *arrs)


PACK_ROWS = 56
PACK_AT = {"ada_b": 0, "loss": 9, "norm_ffn1": 10, "norm_mix": 11, "hgrn_g": 12, "conv_b": 13, "conv_ln_g": 14,
           "conv_ln_b": 15, "norm_ffn2": 16, "norm_final": 17, "hgrn_lb": 18, "conv_w": 20}


def _local_step(x, tgt, mod, small, kc, weight, reduce, reduce_small):
    lb = jax.nn.sigmoid(small["hgrn_lb"][0:1] - small["hgrn_lb"][1:2])
    vec1 = _pack_rows([(mod, 0, 3, 0), (small["norm_ffn1"], 0, 1, 3)], 8, "pack_vec1")
    vec2 = _pack_rows([(mod, 3, 3, 0), (small["norm_mix"], 0, 1, 3), (lb, 0, 1, 4), (small["hgrn_g"], 0, 1, 5)],
                      8, "pack_vec2")
    vec3 = _pack_rows([(mod, 6, 3, 0), (small["norm_ffn2"], 0, 1, 3)], 8, "pack_vec3")
    cvec = _pack_rows([(small["conv_b"], 0, 1, 0), (small["conv_ln_g"], 0, 1, 1), (small["conv_ln_b"], 0, 1, 2)],
                      8, "pack_cvec")
    cw = small["conv_w"]
    gvec = _pack_rows([(small["norm_final"], 0, 1, 0)], 8, "pack_gvec")

    wg = {n: weight(n, vec1) for n in ("ffn1_w_in", "ffn1_w_out")}
    x1, h1, a1, b1, s1, f1 = _ffn_fwd(x, vec1, wg["ffn1_w_in"], wg["ffn1_w_out"], "ffn1_fwd")
    wg["mix_w_in"] = weight("mix_w_in", x1)
    h2, qr, g, k, v, og, u, ua, ub, sa, sb = _mix_proj_fwd(x1, vec2, wg["mix_w_in"])
    oa, o, st = _hgrn_fwd(qr, g, k, v, og, vec2)
    us, uc = _conv_fwd(u, cw, cvec)
    wg.update({n: weight(n, us) for n in ("hgrn_w_o", "conv_w_o", "mix_w_out")})
    x2, ya, yb, mg, mo = _merge_fwd(x1, oa, us, sa, sb, vec2, wg["hgrn_w_o"], wg["conv_w_o"], wg["mix_w_out"])
    wg.update({n: weight(n, x2) for n in ("ffn2_w_in", "ffn2_w_out")})
    dx3, h3, a3, b3, s3, f3, acc_head = _ffn_fwd(x2, vec3, wg["ffn2_w_in"], wg["ffn2_w_out"], "ffn2_fwd",
                                                 head=(tgt, gvec))

    dx2, df3, dab3, acc3 = _ffn_bwd(dx3, x2, vec3, a3, b3, f3, wg["ffn2_w_in"], wg["ffn2_w_out"], "ffn2_bwd")
    tok = reduce(("ffn2_w_out", "ffn2_w_in"), [_mm_tn(s3, df3, 1, kc, DFF // NCHIP, "ffn2_dwout"),
                                               _mm_tn(h3, dab3, 1, kc, D, "ffn2_dwin")])
    vec2b = vec2 + tok[0:1, 0:1]
    dmo, dya, dyb, doa, duc, dpc, acc_m, acc_c = _merge_bwd(dx2, mo, ya, yb, sa, sb, uc, vec2b, cvec,
                                                            wg["hgrn_w_o"], wg["conv_w_o"], wg["mix_w_out"])
    tok = reduce(("mix_w_out", "hgrn_w_o", "conv_w_o"),
                 [_mm_tn(mg, dmo, 1, kc, D // NCHIP, "mix_dwout"), _mm_tn(oa, dya, 1, kc, D // NCHIP, "hgrn_dwo"),
                  _mm_tn(us, dyb, 1, kc, D // NCHIP, "conv_dwo")])
    vec2c = vec2 + tok[0:1, 0:1]
    dpb, dcw = _conv_bwd_taps(duc, u, ua, ub, cw)
    dpa, acc_h = _hgrn_bwd(doa, og, qr, g, k, v, o, st, vec2c)
    dx1, acc2 = _mix_proj_bwd(dx2, x1, vec2c, dpa, dpb, dpc, wg["mix_w_in"])
    gmix = _mm_tn(h2, dpa, 2, kc, D, "mix_dwin_a", slabs=NCHIP)
    gmix = _mm_tn(h2, dpb, 2, kc, D, "mix_dwin_b", into=gmix, slab=2, slabs=NCHIP)
    gmix = _mm_tn(h2, dpc, 2, kc, D, "mix_dwin_c", into=gmix, slab=3, slabs=NCHIP)
    tok = reduce(("mix_w_in",), [gmix])
    vec1b = vec1 + tok[0:1, 0:1]
    dx0, df1, dab1, acc1 = _ffn_bwd(dx1, x, vec1b, a1, b1, f1, wg["ffn1_w_in"], wg["ffn1_w_out"], "ffn1_bwd")

    at = PACK_AT
    finish_small = reduce_small([
        (acc1, 0, 3, at["ada_b"]), (acc2, 0, 2, at["ada_b"] + 3), (acc_m, 2, 1, at["ada_b"] + 5),
        (acc3, 0, 3, at["ada_b"] + 6), (acc_head, 1, 1, at["loss"]), (acc1, 3, 1, at["norm_ffn1"]),
        (acc2, 3, 1, at["norm_mix"]), (acc_h, 0, 1, at["hgrn_g"]), (acc_c, 0, 3, at["conv_b"]),
        (acc3, 3, 1, at["norm_ffn2"]), (acc_head, 0, 1, at["norm_final"]), (acc_h, 1, 2, at["hgrn_lb"]),
        (dcw, 0, CONV_K, at["conv_w"])])
    finish_small, tok = finish_small
    last = [_mm_tn(s1, df1, 1, kc, DFF // NCHIP, "ffn1_dwout", after=tok),
            _mm_tn(h1, dab1, 1, kc, D, "ffn1_dwin", after=tok)]
    reduce(("ffn1_w_out", "ffn1_w_in"), last, finish_small(last[1][0]))
    return dx0


BLOCK_BYTES = 5 * 512 * 1024


def _row_block(rows, cols):
    for br in (512, 352, 256, 176, 128, 64, 32, 16, 8):
        if rows % br == 0 and br * cols * 4 <= BLOCK_BYTES:
            return br
    return rows


def _cast_into_slot(w, kc, name, after):
    R, C = w.shape
    br = _row_block(R, C)

    def body(kc_ref, w_ref, after_ref, o_ref):
        o_ref[...] = w_ref[...].astype(BF16)

    return pl.pallas_call(
        body, name=name,
        grid_spec=pltpu.PrefetchScalarGridSpec(
            num_scalar_prefetch=1, grid=(R // br,),
            in_specs=[pl.BlockSpec((br, C), lambda i, kc: (i, 0)), ANY],
            out_specs=pl.BlockSpec((None, br, C), lambda i, kc: (kc[0], i, 0))),
        out_shape=jax.ShapeDtypeStruct((NCHIP, R, C), BF16), compiler_params=_params("parallel"),
    )(kc, w, after)


def _adamw_math(w, g, m, v):
    nm = ADAM_B1 * m + (1.0 - ADAM_B1) * g
    nv = ADAM_B2 * v + (1.0 - ADAM_B2) * (g * g)
    m_hat = nm / (1.0 - ADAM_B1 ** ADAM_STEP)
    v_hat = nv / (1.0 - ADAM_B2 ** ADAM_STEP)
    return -ADAM_LR * (m_hat / (jnp.sqrt(v_hat) + ADAM_EPS) + ADAM_WD * w), nm, nv


def _adamw_rows(params, gsum, rows, after, name):
    n = len(params)

    def body(*refs):
        g_ref = refs[3 * n]
        outs = refs[3 * n + 2:]
        for i, r0 in enumerate(rows):
            w_ref, m_ref, v_ref = refs[3 * i:3 * i + 3]
            g = g_ref[r0:r0 + w_ref.shape[0], :]
            outs[4 * i][...] = g
            outs[4 * i + 1][...], outs[4 * i + 2][...], outs[4 * i + 3][...] = _adamw_math(
                w_ref[...], g, m_ref[...], v_ref[...])

    flat = [a for p in params for a in p]
    full = lambda a: pl.BlockSpec(a.shape, lambda: (0, 0))
    out = pl.pallas_call(
        body, name=name, in_specs=[full(a) for a in flat] + [full(gsum), ANY],
        out_specs=[full(p[0]) for p in params for _ in range(4)],
        out_shape=[jax.ShapeDtypeStruct(p[0].shape, F32) for p in params for _ in range(4)],
    )(*flat, gsum, after)
    return [out[4 * i:4 * i + 4] for i in range(n)]


def _adamw(w, g, m, v, name, after=None, copy_grad=False):
    R, C = w.shape
    br = _row_block(R, C)
    extra = [] if after is None else [after]
    nout = 4 if copy_grad else 3

    def body(w_ref, g_ref, m_ref, v_ref, *rest):
        d_ref, nm_ref, nv_ref = rest[-nout:][:3]
        gv = g_ref[...]
        if copy_grad:
            rest[-1][...] = gv
        d_ref[...], nm_ref[...], nv_ref[...] = _adamw_math(w_ref[...], gv, m_ref[...], v_ref[...])

    blk = pl.BlockSpec((br, C), lambda i: (i, 0))
    out = jax.ShapeDtypeStruct((R, C), F32)
    return pl.pallas_call(
        body, name=name, grid=(R // br,), in_specs=[blk] * 4 + [ANY] * len(extra), out_specs=[blk] * nout,
        out_shape=[out] * nout, compiler_params=_params("parallel"),
    )(w, g, m, v, *extra)


def _coords():
    return lax.axis_index("x"), lax.axis_index("y"), lax.axis_index("c")


def _flip(v, bit):
    return 1 - v if bit else v


def _allgather8(v, name):
    R, C = v.shape

    def body(v_ref, out_ref, send_sems, recv_sems, local_sem):
        x, y, c = _coords()
        me = 4 * x + 2 * y + c
        mine = pltpu.make_async_copy(v_ref, out_ref.at[me], local_sem)
        mine.start()

        def copy(m, block):
            peer = (_flip(x, m & 4), _flip(y, m & 2), _flip(c, m & 1))
            return pltpu.make_async_remote_copy(
                src_ref=v_ref, dst_ref=out_ref.at[block], send_sem=send_sems.at[m - 1],
                recv_sem=recv_sems.at[m - 1], device_id=peer, device_id_type=MESH)

        sends = [copy(m, me) for m in range(1, 8)]
        for cp in sends:
            cp.start()
        for m in range(1, 8):
            sender = 4 * _flip(x, m & 4) + 2 * _flip(y, m & 2) + _flip(c, m & 1)
            copy(m, sender).wait_recv()
        for cp in sends:
            cp.wait_send()
        mine.wait()

    vm = pl.BlockSpec(memory_space=pltpu.VMEM)
    return pl.pallas_call(
        body, name=name, in_specs=[vm], out_specs=vm,
        out_shape=jax.ShapeDtypeStruct((8, R, C), F32),
        scratch_shapes=[pltpu.SemaphoreType.DMA((7,)), pltpu.SemaphoreType.DMA((7,)), pltpu.SemaphoreType.DMA],
    )(v)


HBM = pl.BlockSpec(memory_space=pltpu.HBM)
SEM = pl.BlockSpec(memory_space=pltpu.SEMAPHORE)
EFFECT = pltpu.SideEffectType.DATAFLOW_SIDE_EFFECTING


def _peer8(x, y, c, m):
    px, py, pc = _flip(x, m & 4), _flip(y, m & 2), _flip(c, m & 1)
    return (px, py, pc), 4 * px + 2 * py + pc


def _allgather8_start(blocks, name):
    def body(b_ref, send, recv, thru, token):
        x, y, c = _coords()
        me = 4 * x + 2 * y + c
        for m in range(1, 8):
            peer, _ = _peer8(x, y, c, m)
            pltpu.make_async_remote_copy(src_ref=b_ref.at[me], dst_ref=b_ref.at[me], send_sem=send.at[m - 1],
                                         recv_sem=recv.at[m - 1], device_id=peer, device_id_type=MESH).start()
        token[...] = jnp.zeros_like(token)

    sem = pltpu.SemaphoreType.DMA((7,))
    return pl.pallas_call(
        body, name=name,
        out_shape=[sem, sem, pltpu.HBM(blocks.shape, blocks.dtype), jax.ShapeDtypeStruct((8, 128), F32)],
        in_specs=[HBM], out_specs=[SEM, SEM, HBM, pl.BlockSpec(memory_space=pltpu.VMEM)], input_output_aliases={0: 2},
        compiler_params=pltpu.CompilerParams(has_side_effects=EFFECT),
    )(pltpu.with_memory_space_constraint(blocks, pltpu.HBM))


def _allgather8_wait(blocks, send_sem, recv_sem, after, name):
    def body(b_ref, send, recv, after_ref, thru):
        x, y, c = _coords()
        me = 4 * x + 2 * y + c
        for m in range(1, 8):
            peer, sender = _peer8(x, y, c, m)
            cp = pltpu.make_async_remote_copy(src_ref=b_ref.at[me], dst_ref=b_ref.at[sender], send_sem=send.at[m - 1],
                                              recv_sem=recv.at[m - 1], device_id=peer, device_id_type=MESH)
            cp.wait_send()
            cp.wait_recv()

    return pl.pallas_call(
        body, name=name, out_shape=pltpu.HBM(blocks.shape, blocks.dtype),
        in_specs=[HBM, SEM, SEM, ANY], out_specs=HBM, input_output_aliases={0: 0},
        compiler_params=pltpu.CompilerParams(has_side_effects=EFFECT),
    )(blocks, send_sem, recv_sem, after)


def _chip_peer(x, y, m):
    px, py = _flip(x, m & 2), _flip(y, m & 1)
    return px, py, 2 * px + py


def _core_rows(land, c):
    half = land.shape[1] // 2
    return pl.ds(pl.multiple_of(c * half, 16), half)


def _gather_start(lands, groups, halved, after, name):
    n, ng, na = len(lands), len(groups), len(after)

    def body(*refs):
        ins = refs[:n]
        sends, recvs = refs[n + na:n + na + ng], refs[n + na + ng:n + na + 2 * ng]
        token = refs[n + na + 2 * ng + n]
        x, y, c = _coords()
        k = 2 * x + y
        for gi, grp in enumerate(groups):
            for j, t in enumerate(grp):
                mine = ins[t].at[k, _core_rows(ins[t], c), :] if halved[gi] else ins[t].at[k]
                for m in (1, 2, 3):
                    px, py, _ = _chip_peer(x, y, m)
                    pltpu.make_async_remote_copy(
                        src_ref=mine, dst_ref=mine, send_sem=sends[gi].at[3 * j + m - 1],
                        recv_sem=recvs[gi].at[3 * j + m - 1], device_id=(px, py, c), device_id_type=MESH).start()
        token[...] = jnp.zeros_like(token)

    sems = [pltpu.SemaphoreType.DMA((3 * len(g),)) for g in groups]
    out = pl.pallas_call(
        body, name=name,
        out_shape=sems + sems + [pltpu.HBM(a.shape, a.dtype) for a in lands] + [jax.ShapeDtypeStruct((8, 128), F32)],
        in_specs=[HBM] * n + [ANY] * na,
        out_specs=[SEM] * (2 * ng) + [HBM] * n + [pl.BlockSpec(memory_space=pltpu.VMEM)],
        input_output_aliases={t: 2 * ng + t for t in range(n)},
        compiler_params=pltpu.CompilerParams(has_side_effects=EFFECT),
    )(*[pltpu.with_memory_space_constraint(a, pltpu.HBM) for a in lands], *after)
    return out[:ng], out[ng:2 * ng], out[2 * ng:2 * ng + n], out[2 * ng + n]


def _gather_wait(lands, halved, send_sem, recv_sem, after, name):
    n = len(lands)

    def body(*refs):
        ins, send, recv = refs[:n], refs[n], refs[n + 1]
        x, y, c = _coords()
        k = 2 * x + y
        for j in range(n):
            rows = _core_rows(ins[j], c)
            for m in (1, 2, 3):
                px, py, pk = _chip_peer(x, y, m)
                cp = pltpu.make_async_remote_copy(
                    src_ref=ins[j].at[k, rows, :] if halved else ins[j].at[k],
                    dst_ref=ins[j].at[pk, rows, :] if halved else ins[j].at[pk], send_sem=send.at[3 * j + m - 1],
                    recv_sem=recv.at[3 * j + m - 1], device_id=(px, py, c), device_id_type=MESH)
                cp.wait_send()
                cp.wait_recv()

    return pl.pallas_call(
        body, name=name, out_shape=[pltpu.HBM(a.shape, a.dtype) for a in lands],
        in_specs=[HBM] * n + [SEM, SEM, ANY], out_specs=[HBM] * n,
        input_output_aliases={j: j for j in range(n)},
        compiler_params=pltpu.CompilerParams(has_side_effects=EFFECT),
    )(*lands, send_sem, recv_sem, after)


def _sibling_fill(lands, name):
    n = len(lands)

    def body(*refs):
        ins = refs[:n]
        send_sems, recv_sems = refs[2 * n:]
        x, y, c = _coords()
        sends, recvs = [], []
        for t in range(n):
            for m in (1, 2, 3):
                _, _, pk = _chip_peer(x, y, m)
                for rows, lst in ((_core_rows(ins[t], c), sends), (_core_rows(ins[t], 1 - c), recvs)):
                    lst.append(pltpu.make_async_remote_copy(
                        src_ref=ins[t].at[pk, rows, :], dst_ref=ins[t].at[pk, rows, :],
                        send_sem=send_sems.at[3 * t + m - 1], recv_sem=recv_sems.at[3 * t + m - 1],
                        device_id=(x, y, 1 - c), device_id_type=MESH))
        for cp in sends:
            cp.start()
        for cp in recvs:
            cp.wait_recv()
        for cp in sends:
            cp.wait_send()

    return pl.pallas_call(
        body, name=name, in_specs=[ANY] * n, out_specs=[ANY] * n,
        out_shape=[jax.ShapeDtypeStruct(a.shape, a.dtype) for a in lands],
        input_output_aliases={t: t for t in range(n)},
        scratch_shapes=[pltpu.SemaphoreType.DMA((3 * n,)), pltpu.SemaphoreType.DMA((3 * n,))],
    )(*lands)


def _scatter_start(srcs, name, after=()):
    n, na = len(srcs), len(after)

    def body(*refs):
        ins, lands = refs[:n], refs[n:2 * n]
        send, recv = refs[2 * n + na], refs[2 * n + na + 1]
        token = refs[2 * n + na + 2 + 2 * n]
        x, y, c = _coords()
        k = 2 * x + y
        for t in range(n):
            for m in (1, 2, 3):
                px, py, pk = _chip_peer(x, y, m)
                pltpu.make_async_remote_copy(
                    src_ref=ins[t].at[pk], dst_ref=lands[t].at[k], send_sem=send.at[3 * t + m - 1],
                    recv_sem=recv.at[3 * t + m - 1], device_id=(px, py, c), device_id_type=MESH).start()
        token[...] = jnp.zeros_like(token)

    sem = pltpu.SemaphoreType.DMA((3 * n,))
    hbm = [pltpu.HBM(a.shape, a.dtype) for a in srcs]
    operands = list(srcs) + [lax.empty(a.shape, a.dtype) for a in srcs]
    out = pl.pallas_call(
        body, name=name, out_shape=[sem, sem] + hbm + hbm + [jax.ShapeDtypeStruct((8, 128), F32)],
        in_specs=[HBM] * (2 * n) + [ANY] * na,
        out_specs=[SEM, SEM] + [HBM] * (2 * n) + [pl.BlockSpec(memory_space=pltpu.VMEM)],
        input_output_aliases={t: 2 + t for t in range(2 * n)},
        compiler_params=pltpu.CompilerParams(has_side_effects=EFFECT),
    )(*[pltpu.with_memory_space_constraint(a, pltpu.HBM) for a in operands], *after)
    return out[0], out[1], out[2:2 + n], out[2 + n:2 + 2 * n], out[2 + 2 * n]


def _scatter_wait(srcs, lands, send_sem, recv_sem, after, name):
    n = len(srcs)

    def body(*refs):
        ins, land = refs[:n], refs[n:2 * n]
        send, recv = refs[2 * n], refs[2 * n + 1]
        x, y, c = _coords()
        for t in range(n):
            for m in (1, 2, 3):
                px, py, pk = _chip_peer(x, y, m)
                cp = pltpu.make_async_remote_copy(
                    src_ref=ins[t].at[pk], dst_ref=land[t].at[pk], send_sem=send.at[3 * t + m - 1],
                    recv_sem=recv.at[3 * t + m - 1], device_id=(px, py, c), device_id_type=MESH)
                cp.wait_send()
                cp.wait_recv()

    hbm = [pltpu.HBM(a.shape, a.dtype) for a in srcs]
    out = pl.pallas_call(
        body, name=name, out_shape=hbm + hbm, in_specs=[HBM] * (2 * n) + [SEM, SEM, ANY], out_specs=[HBM] * (2 * n),
        input_output_aliases={t: t for t in range(2 * n)},
        compiler_params=pltpu.CompilerParams(has_side_effects=EFFECT),
    )(*srcs, *lands, send_sem, recv_sem, after)
    return out[:n], out[n:]


def _sum_own_half(g, ra, kc, name):
    _, R, C = g.shape
    half = R // 2
    br = _row_block(half, C)
    nb = half // br

    def body(kc_ref, g_ref, ra_ref, o_ref):
        o_ref[...] = (g_ref[...] + ra_ref[...].astype(F32)).astype(BF16)

    return pl.pallas_call(
        body, name=name,
        grid_spec=pltpu.PrefetchScalarGridSpec(
            num_scalar_prefetch=1, grid=(NCHIP, nb),
            in_specs=[pl.BlockSpec((None, br, C), lambda j, i, kc: (j, kc[1] * nb + i, 0)),
                      pl.BlockSpec((None, br, C), lambda j, i, kc: (j, i, 0))],
            out_specs=pl.BlockSpec((None, br, C), lambda j, i, kc: (j, i, 0))),
        out_shape=jax.ShapeDtypeStruct((NCHIP, half, C), BF16),
        compiler_params=_params("parallel", "parallel"),
    )(kc, g, ra)


def _sum_chips(sa, rb, kc, name, after=None):
    _, half, C = rb.shape
    br = _row_block(half, C)
    nb = half // br
    extra = [] if after is None else [after]

    def body(kc_ref, own_ref, r1_ref, r2_ref, r3_ref, *rest):
        out, obuf, local_sems, send_sems, recv_sem = rest[-5:]
        i = pl.program_id(0)
        slot = i % 2
        x, y, c = _coords()

        def copies(i_, slot_):
            rows = out.at[pl.ds(pl.multiple_of((c * nb + i_) * br, 8), br), :]
            return (pltpu.make_async_copy(obuf.at[slot_], rows, local_sems.at[slot_]),
                    pltpu.make_async_remote_copy(src_ref=obuf.at[slot_], dst_ref=rows, send_sem=send_sems.at[slot_],
                                                 recv_sem=recv_sem, device_id=(x, y, 1 - c), device_id_type=MESH))

        @pl.when(i >= 2)
        def _():
            here, there = copies(i, slot)
            here.wait()
            there.wait_send()

        acc = own_ref[...].astype(F32) + r1_ref[...].astype(F32)
        obuf[slot] = (acc + r2_ref[...].astype(F32)) + r3_ref[...].astype(F32)
        here, there = copies(i, slot)
        here.start()
        there.start()

        @pl.when(i == nb - 1)
        def _():
            for s in range(min(2, nb)):
                here, there = copies(i, (i - s) % 2)
                here.wait()
                there.wait_send()
            theirs = out.at[pl.ds(pl.multiple_of((1 - c) * half, 8), half), :]
            pltpu.make_async_remote_copy(src_ref=theirs, dst_ref=theirs, send_sem=send_sems.at[0], recv_sem=recv_sem,
                                         device_id=(x, y, 1 - c), device_id_type=MESH).wait_recv()

    def slab(m):
        return pl.BlockSpec((None, br, C), lambda i, kc: (kc[0] ^ m, i, 0))

    return pl.pallas_call(
        body, name=name,
        grid_spec=pltpu.PrefetchScalarGridSpec(
            num_scalar_prefetch=1, grid=(nb,),
            in_specs=[slab(0), slab(1), slab(2), slab(3)] + [ANY] * len(extra),
            out_specs=ANY,
            scratch_shapes=[pltpu.VMEM((2, br, C), F32), pltpu.SemaphoreType.DMA((2,)), pltpu.SemaphoreType.DMA((2,)),
                            pltpu.SemaphoreType.DMA]),
        out_shape=jax.ShapeDtypeStruct((2 * half, C), F32), compiler_params=_params("arbitrary"),
    )(kc, sa, rb, rb, rb, *extra)


def _sum8(ga, name):
    _, R, C = ga.shape

    def body(g_ref, o_ref):
        acc = g_ref[0]
        for j in range(1, 8):
            acc = acc + g_ref[j]
        o_ref[...] = acc

    return pl.pallas_call(
        body, name=name, in_specs=[pl.BlockSpec((8, R, C), lambda: (0, 0, 0))],
        out_specs=pl.BlockSpec((R, C), lambda: (0, 0)), out_shape=jax.ShapeDtypeStruct((R, C), F32),
    )(ga)


ADA_COLS = 9 * D // NCHIP
ADA_BLK = 256


def _ada_mod(c_all, ada_w, ada_b, kidx):
    def body(k_ref, c_ref, w_ref, b_ref, o_ref):
        cv = c_ref[...]
        cs = cv * _sigmoid(cv)
        o_ref[...] = jnp.dot(cs, w_ref[...], precision=lax.Precision.HIGHEST,
                             preferred_element_type=F32) + b_ref[...]

    nblk = ADA_COLS // ADA_BLK
    return pl.pallas_call(
        body, name="ada_mod",
        grid_spec=pltpu.PrefetchScalarGridSpec(
            num_scalar_prefetch=1, grid=(nblk,),
            in_specs=[pl.BlockSpec((8, D), lambda j, k: (0, 0)),
                      pl.BlockSpec((D, ADA_BLK), lambda j, k: (0, j)),
                      pl.BlockSpec((1, ADA_BLK), lambda j, k: (0, k[0] * nblk + j))],
            out_specs=pl.BlockSpec((8, ADA_BLK), lambda j, k: (0, j))),
        out_shape=jax.ShapeDtypeStruct((8, ADA_COLS), F32),
        compiler_params=_params("parallel"),
    )(kidx, c_all, ada_w, ada_b)


def _ada_grad(c_all_t, dmod_all, kidx):
    def body(k_ref, ct_ref, dm_ref, o_ref):
        cv = ct_ref[...]
        cs = cv * _sigmoid(cv)
        acc = cs[:, 0:1] * dm_ref[0:1, :]
        for b in range(1, 8):
            acc = acc + cs[:, b:b + 1] * dm_ref[b:b + 1, :]
        o_ref[...] = acc

    nblk = ADA_COLS // ADA_BLK
    return pl.pallas_call(
        body, name="ada_grad",
        grid_spec=pltpu.PrefetchScalarGridSpec(
            num_scalar_prefetch=1, grid=(nblk,),
            in_specs=[pl.BlockSpec((D, 8), lambda j, k: (0, 0)),
                      pl.BlockSpec((8, ADA_BLK), lambda j, k: (0, k[0] * nblk + j))],
            out_specs=pl.BlockSpec((D, ADA_BLK), lambda j, k: (0, j))),
        out_shape=jax.ShapeDtypeStruct((D, ADA_COLS), F32),
        compiler_params=_params("parallel"),
    )(kidx, c_all_t, dmod_all)


BIG = ("ffn1_w_in", "ffn1_w_out", "mix_w_in", "hgrn_w_o", "conv_w_o", "mix_w_out", "ffn2_w_in", "ffn2_w_out")
ROW_SHARDED = ("ffn1_w_out", "hgrn_w_o", "conv_w_o", "mix_w_out", "ffn2_w_out")
GATHER_GROUPS = ((0, 1), (2,), (3, 4, 5), (6, 7))
GATHER_HALVED = (True, True, False, False)
GATHER_STARTS = ((0, 1), (2, 3))
PACK_LEN = {"ada_b": 9, "hgrn_lb": 2}
WEIGHTS = ("ada_w", "ada_b", "norm_ffn1", "ffn1_w_in", "ffn1_w_out", "norm_mix", "mix_w_in", "hgrn_lb", "hgrn_g",
           "hgrn_w_o", "conv_w", "conv_b", "conv_ln_g", "conv_ln_b", "conv_w_o", "mix_w_out", "norm_ffn2",
           "ffn2_w_in", "ffn2_w_out", "norm_final")
PACKED = ("ada_b", "norm_ffn1", "norm_mix", "hgrn_g", "conv_b", "conv_ln_g", "conv_ln_b", "norm_ffn2",
          "norm_final", "hgrn_lb")


def _step(w, m, v, x, c, tgt):
    xi, yi, ci = _coords()
    kidx = (2 * xi + yi).astype(jnp.int32).reshape(1)
    kc = jnp.stack([2 * xi + yi, ci]).astype(jnp.int32)
    me = 4 * xi + 2 * yi + ci

    cq = D // NCHIP
    first = jnp.zeros((40, cq), F32).at[0:CONV_K].set(w["conv_w"][0]).at[32:36].set(c.reshape(NCHIP, cq))
    first_all = _allgather8(first, "gather_c_conv_w")
    c_all = first_all[:, 32:36, :].reshape(8, D)
    mod_cols = _ada_mod(c_all, w["ada_w"][0], w["ada_b"], kidx)
    mod_all = _allgather8(mod_cols, "gather_mod")
    mod = lax.dynamic_slice(mod_all, (0, me, 0), (8, 1, ADA_COLS))[::2].reshape(9, D)
    small = {n: w[n].reshape(-1, D) for n in ("norm_ffn1", "norm_mix", "hgrn_lb", "hgrn_g", "conv_b", "conv_ln_g",
                                              "conv_ln_b", "norm_ffn2", "norm_final")}
    small["conv_w"] = jnp.concatenate([first_all[2 * j, 0:32, :] for j in range(NCHIP)], axis=1)

    lands, sends, recvs = [], [], []
    after = mod
    for part in GATHER_STARTS:
        tensors = [t for gi in part for t in GATHER_GROUPS[gi]]
        cast = [_cast_into_slot(w[BIG[t]][0], kc, "cast_" + BIG[t], after) for t in tensors]
        groups = [tuple(tensors.index(t) for t in GATHER_GROUPS[gi]) for gi in part]
        s, r, thru, after = _gather_start(cast, groups, [GATHER_HALVED[gi] for gi in part], [after],
                                          "gather_weights_start%d" % part[0])
        lands, sends, recvs = lands + list(thru), sends + list(s), recvs + list(r)
    started_all = after
    ready = {}

    def weight(name, after):
        t = BIG.index(name)
        if t not in ready:
            gi = [t in grp for grp in GATHER_GROUPS].index(True)
            grp = GATHER_GROUPS[gi]
            outs = _gather_wait([lands[j] for j in grp], GATHER_HALVED[gi], sends[gi], recvs[gi],
                                started_all if gi == 0 else after, "gather_weights_wait%d" % gi)
            if GATHER_HALVED[gi]:
                outs = _sibling_fill(outs, "gather_weights_fill%d" % gi)
            ready.update(zip(grp, outs))
        return ready[t].reshape(-1, D) if name in ROW_SHARDED else ready[t]

    grads, delta, new_m, new_v = {}, {}, {}, {}
    flight = []
    landed = []

    def settle(after):
        names, sa, rb, send, recv = flight.pop()
        sa, rb = _scatter_wait(sa, rb, send, recv, after, "rs_chip_wait_" + names[0])
        landed.append((names, sa, rb))

    def reduce(names, pairs, after=None):
        gs = [g.reshape(NCHIP, -1, g.shape[-1]) for g, _ in pairs]
        ra = [r.reshape(NCHIP, -1, r.shape[-1]) for _, r in pairs]
        sa = [_sum_own_half(g, r, kc, "rs_sum_pair_" + n) for g, r, n in zip(gs, ra, names)]
        if flight:
            settle(sa[0])
        send, recv, sa, rb, tok = _scatter_start(sa, "rs_chip_start_" + names[0], () if after is None else (after,))
        flight.append((names, sa, rb, send, recv))
        started.append(tok)
        return tok

    def adamw(n, after=None):
        shape = w[n].shape
        two = (shape[-2], shape[-1])
        out = _adamw(w[n].reshape(two), grads[n], m[n].reshape(two), v[n].reshape(two), "adamw_" + n, after,
                     copy_grad=n in BIG)
        g_ = out[3] if n in BIG else grads[n]
        grads[n], delta[n], new_m[n], new_v[n] = (a.reshape(shape) for a in (g_, out[0], out[1], out[2]))
        return out[1]

    def finish(after=None):
        names, sa, rb = landed.pop(0)
        full = [_sum_chips(s, r, kc, "rs_sum_chips_" + n, after) for s, r, n in zip(sa, rb, names)]
        grads.update(zip(names, full))
        return [adamw(n) for n in names][-1]

    started = []

    smalls = []

    def reduce_small(parts):
        blocks = _pack_rows(parts, PACK_ROWS, "pack_small_grads", slot=me.astype(jnp.int32).reshape(1))
        send, recv, blocks, tok = _allgather8_start(blocks, "gather_small_grads_start")

        def finish(after):
            packed_all = _allgather8_wait(blocks, send, recv, after, "gather_small_grads_wait")
            smalls.extend([packed_all, _sum8(packed_all, "sum_small_grads")])
            return smalls[1]

        return finish, tok

    dx = _local_step(x[0], tgt[0], mod, small, kc, weight, reduce, reduce_small)
    packed_all, gsum = smalls
    loss = (0.5 / D) * jnp.sum(gsum[PACK_AT["loss"]])
    dmod_all = packed_all[:, 0:9, :].reshape(8, 9 * D)
    grads["ada_w"] = _ada_grad(c_all.T, dmod_all, kidx)
    grads["conv_w"] = lax.dynamic_slice(gsum, (PACK_AT["conv_w"], kidx[0] * (D // NCHIP)), (CONV_K, D // NCHIP))

    tok = started[-1]
    adamw("ada_w", tok)
    adamw("conv_w")
    two = lambda a, n: a.reshape(PACK_LEN.get(n, 1), D)
    small_out = _adamw_rows([(two(w[n], n), two(m[n], n), two(v[n], n)) for n in PACKED], gsum,
                            [PACK_AT[n] for n in PACKED], tok, "adamw_small")
    for n, quad in zip(PACKED, small_out):
        grads[n], delta[n], new_m[n], new_v[n] = (a.reshape(w[n].shape) for a in quad)
    last = small_out[-1][3]
    while landed:
        last = finish(tok)
    settle(last)
    finish()

    outs = [loss, dx[None]]
    for d in (grads, delta, new_m, new_v):
        outs += [d[n] for n in WEIGHTS]
    return tuple(outs)


def kernel(x, c, ada_w, ada_b, norm_ffn1, ffn1_w_in, ffn1_w_out, norm_mix, mix_w_in, hgrn_lb, hgrn_g, hgrn_w_o, conv_w, conv_b, conv_ln_g, conv_ln_b, conv_w_o, mix_w_out, norm_ffn2, ffn2_w_in, ffn2_w_out, norm_final, loss_target, m_ada_w, m_ada_b, m_norm_ffn1, m_ffn1_w_in, m_ffn1_w_out, m_norm_mix, m_mix_w_in, m_hgrn_lb, m_hgrn_g, m_hgrn_w_o, m_conv_w, m_conv_b, m_conv_ln_g, m_conv_ln_b, m_conv_w_o, m_mix_w_out, m_norm_ffn2, m_ffn2_w_in, m_ffn2_w_out, m_norm_final, v_ada_w, v_ada_b, v_norm_ffn1, v_ffn1_w_in, v_ffn1_w_out, v_norm_mix, v_mix_w_in, v_hgrn_lb, v_hgrn_g, v_hgrn_w_o, v_conv_w, v_conv_b, v_conv_ln_g, v_conv_ln_b, v_conv_w_o, v_mix_w_out, v_norm_ffn2, v_ffn2_w_in, v_ffn2_w_out, v_norm_final):
    w = dict(ada_w=ada_w, ada_b=ada_b, norm_ffn1=norm_ffn1, ffn1_w_in=ffn1_w_in, ffn1_w_out=ffn1_w_out,
             norm_mix=norm_mix, mix_w_in=mix_w_in, hgrn_lb=hgrn_lb, hgrn_g=hgrn_g, hgrn_w_o=hgrn_w_o, conv_w=conv_w,
             conv_b=conv_b, conv_ln_g=conv_ln_g, conv_ln_b=conv_ln_b, conv_w_o=conv_w_o, mix_w_out=mix_w_out,
             norm_ffn2=norm_ffn2, ffn2_w_in=ffn2_w_in, ffn2_w_out=ffn2_w_out, norm_final=norm_final)
    m = dict(ada_w=m_ada_w, ada_b=m_ada_b, norm_ffn1=m_norm_ffn1, ffn1_w_in=m_ffn1_w_in, ffn1_w_out=m_ffn1_w_out,
             norm_mix=m_norm_mix, mix_w_in=m_mix_w_in, hgrn_lb=m_hgrn_lb, hgrn_g=m_hgrn_g, hgrn_w_o=m_hgrn_w_o,
             conv_w=m_conv_w, conv_b=m_conv_b, conv_ln_g=m_conv_ln_g, conv_ln_b=m_conv_ln_b, conv_w_o=m_conv_w_o,
             mix_w_out=m_mix_w_out, norm_ffn2=m_norm_ffn2, ffn2_w_in=m_ffn2_w_in, ffn2_w_out=m_ffn2_w_out,
             norm_final=m_norm_final)
    v = dict(ada_w=v_ada_w, ada_b=v_ada_b, norm_ffn1=v_norm_ffn1, ffn1_w_in=v_ffn1_w_in, ffn1_w_out=v_ffn1_w_out,
             norm_mix=v_norm_mix, mix_w_in=v_mix_w_in, hgrn_lb=v_hgrn_lb, hgrn_g=v_hgrn_g, hgrn_w_o=v_hgrn_w_o,
             conv_w=v_conv_w, conv_b=v_conv_b, conv_ln_g=v_conv_ln_g, conv_ln_b=v_conv_ln_b, conv_w_o=v_conv_w_o,
             mix_w_out=v_mix_w_out, norm_ffn2=v_norm_ffn2, ffn2_w_in=v_ffn2_w_in, ffn2_w_out=v_ffn2_w_out,
             norm_final=v_norm_final)
    return _step(w, m, v, x, c, loss_target)
```

```python
import jax
import jax.numpy as jnp
from jax import lax
from jax.experimental import pallas as pl
from jax.experimental.pallas import tpu as pltpu

F32 = jnp.float32
BF16 = jnp.bfloat16

D = 1024
DFF = 2816
NCHIP = 4
FSH = 2 * DFF // NCHIP
HEADS = 8
DK = 128
CHUNK = 64
CONV_K = 31
HALO = 32
EPS = 1e-6
TB = 256
CB = 2048
DW_TOKENS = 2048
VMEM_LIMIT = 56 * 1024 * 1024

ADAM_LR = 0.001
ADAM_B1 = 0.9
ADAM_B2 = 0.999
ADAM_EPS = 1e-08
ADAM_WD = 0.01
ADAM_STEP = 10

MESH = pl.DeviceIdType.MESH
ANY = pl.BlockSpec(memory_space=pl.ANY)


def _params(*sem):
    return pltpu.CompilerParams(dimension_semantics=sem, vmem_limit_bytes=VMEM_LIMIT)


def _sigmoid(x):
    return 0.5 * jnp.tanh(0.5 * x) + 0.5


def _dsilu(x, sg):
    return sg * (1.0 + x * (1.0 - sg))


def _nt(a, b):
    return lax.dot_general(a, b, (((1,), (1,)), ((), ())), preferred_element_type=F32)


def _tn(a, b):
    return lax.dot_general(a, b, (((0,), (0,)), ((), ())), preferred_element_type=F32)


def _nn(a, b):
    return jnp.dot(a, b, preferred_element_type=F32)


def _colsum(x):
    return jnp.sum(x, axis=0, keepdims=True)


def _rms_fwd(x, gn, sc, sh):
    r = lax.rsqrt(jnp.mean(x * x, axis=-1, keepdims=True) + EPS)
    n = x * r
    h = (n * gn) * (1.0 + sc) + sh
    return r, n, h


def _rms_bwd(dh, r, n, gn, sc, acc_ref):
    acc_ref[0:1, :] += _colsum(dh)
    acc_ref[1:2, :] += _colsum(dh * (n * gn))
    dng = dh * (1.0 + sc)
    acc_ref[3:4, :] += _colsum(dng * n)
    dn = dng * gn
    return r * (dn - n * jnp.mean(dn * n, axis=-1, keepdims=True))


def _loss_head(x, tgt, gf, acc_ref):
    r = lax.rsqrt(jnp.mean(x * x, axis=-1, keepdims=True) + EPS)
    n = x * r
    err = n * gf - tgt
    acc_ref[1:2, :] += _colsum(err * err)
    dy = err * (1.0 / D)
    acc_ref[0:1, :] += _colsum(dy * n)
    dn = dy * gf
    return r * (dn - n * jnp.mean(dn * n, axis=-1, keepdims=True))


def _ffn_fwd(x, vec, w_in, w_out, name, head=None):
    T = x.shape[0]
    nh = 0 if head is None else 2

    def body(x_ref, vec_ref, *rest):
        win_hbm, wout_hbm = rest[nh:nh + 2]
        xo_ref, h_ref, a_ref, b_ref, s_ref, f_ref = rest[nh + 2:nh + 8]
        win, wout = rest[-2:]

        @pl.when(pl.program_id(0) == 0)
        def _():
            pltpu.sync_copy(win_hbm, win)
            pltpu.sync_copy(wout_hbm, wout)
            if head is not None:
                rest[nh + 8][...] = jnp.zeros((8, D), F32)

        x = x_ref[...]
        sh, sc, gate, gn = vec_ref[0:1, :], vec_ref[1:2, :], vec_ref[2:3, :], vec_ref[3:4, :]
        _, _, h = _rms_fwd(x, gn, sc, sh)
        hb = h.astype(BF16)
        h_ref[...] = hb
        f = jnp.zeros((TB, D), F32)
        for j in range(2):
            cols = slice(j * FSH, (j + 1) * FSH)
            a = _nn(hb, win[j])
            b = _nn(hb, win[2 + j])
            s = (a * _sigmoid(a) * b).astype(BF16)
            a_ref[:, cols] = a.astype(BF16)
            b_ref[:, cols] = b.astype(BF16)
            s_ref[:, cols] = s
            f = f + _nn(s, wout[cols, :])
        xo = x + (0.5 * gate) * f
        f_ref[...] = f.astype(BF16)
        if head is None:
            xo_ref[...] = xo
        else:
            xo_ref[...] = _loss_head(xo, rest[0][...], rest[1][0:1, :], rest[nh + 8])

    row = lambda w: pl.BlockSpec((TB, w), lambda i: (i, 0))
    vec8 = pl.BlockSpec((8, D), lambda i: (0, 0))
    acc = [] if head is None else [jax.ShapeDtypeStruct((8, D), F32)]
    return pl.pallas_call(
        body, name=name, grid=(T // TB,),
        in_specs=[row(D), vec8] + ([] if head is None else [row(D), vec8]) + [ANY, ANY],
        out_specs=[row(D), row(D), row(DFF), row(DFF), row(DFF), row(D)] + [vec8] * len(acc),
        out_shape=[jax.ShapeDtypeStruct((T, D), F32), jax.ShapeDtypeStruct((T, D), BF16),
                   jax.ShapeDtypeStruct((T, DFF), BF16), jax.ShapeDtypeStruct((T, DFF), BF16),
                   jax.ShapeDtypeStruct((T, DFF), BF16), jax.ShapeDtypeStruct((T, D), BF16)] + acc,
        scratch_shapes=[pltpu.VMEM((NCHIP, D, FSH), BF16), pltpu.VMEM((DFF, D), BF16)],
        compiler_params=_params("arbitrary"),
    )(x, vec, *([] if head is None else list(head)), w_in, w_out)


def _ffn_bwd(dxo, x, vec, a, b, f, w_in, w_out, name):
    T = x.shape[0]

    def body(dxo_ref, x_ref, vec_ref, a_ref, b_ref, f_ref, win_hbm, wout_hbm,
             dx_ref, df_ref, dab_ref, acc_ref, win, wout):
        @pl.when(pl.program_id(0) == 0)
        def _():
            pltpu.sync_copy(win_hbm, win)
            pltpu.sync_copy(wout_hbm, wout)
            acc_ref[...] = jnp.zeros_like(acc_ref)

        dxo = dxo_ref[...]
        x = x_ref[...]
        sh, sc, gate, gn = vec_ref[0:1, :], vec_ref[1:2, :], vec_ref[2:3, :], vec_ref[3:4, :]
        r, n, _ = _rms_fwd(x, gn, sc, sh)
        acc_ref[2:3, :] += _colsum(0.5 * f_ref[...].astype(F32) * dxo)
        dfb = ((0.5 * gate) * dxo).astype(BF16)
        df_ref[...] = dfb
        dh = jnp.zeros((TB, D), F32)
        for j in range(2):
            cols = slice(j * FSH, (j + 1) * FSH)
            ds = _nt(dfb, wout[cols, :])
            av = a_ref[:, cols].astype(F32)
            bv = b_ref[:, cols].astype(F32)
            sg = _sigmoid(av)
            da = (ds * bv * _dsilu(av, sg)).astype(BF16)
            db = (ds * (av * sg)).astype(BF16)
            dab_ref[j] = da
            dab_ref[2 + j] = db
            dh = dh + _nt(da, win[j]) + _nt(db, win[2 + j])
        dx_ref[...] = dxo + _rms_bwd(dh, r, n, gn, sc, acc_ref)

    row = lambda w: pl.BlockSpec((TB, w), lambda i: (i, 0))
    vec8 = pl.BlockSpec((8, D), lambda i: (0, 0))
    return pl.pallas_call(
        body, name=name, grid=(T // TB,),
        in_specs=[row(D), row(D), vec8, row(DFF), row(DFF), row(D), ANY, ANY],
        out_specs=[row(D), pl.BlockSpec((None, TB, D), lambda i: (0, i, 0)),
                   pl.BlockSpec((NCHIP, TB, FSH), lambda i: (0, i, 0)), vec8],
        out_shape=[jax.ShapeDtypeStruct((T, D), F32), jax.ShapeDtypeStruct((1, T, D), BF16),
                   jax.ShapeDtypeStruct((NCHIP, T, FSH), BF16), jax.ShapeDtypeStruct((8, D), F32)],
        scratch_shapes=[pltpu.VMEM((NCHIP, D, FSH), BF16), pltpu.VMEM((DFF, D), BF16)],
        compiler_params=_params("arbitrary"),
    )(dxo, x, vec, a, b, f, w_in, w_out)


def _mm_tn(a, b3, hp, kc, shard_rows, name, into=None, slab=0, slabs=None, after=None):
    T, M = a.shape
    P, _, N = b3.shape
    tm = M if M <= 1408 else M // 2
    tk = min(T, DW_TOKENS if P * (M // tm) > 1 else DW_TOKENS // 2)
    nk = T // tk
    ni = M // tm
    slabs = P // hp if slabs is None else slabs
    half = shard_rows // 2
    extra = ([] if into is None else list(into)) + ([] if after is None else [after])

    def body(kc_ref, a_ref, b_ref, *rest):
        o_ref, ra_ref, hbuf, send_sems, recv_sem = rest[-5:]
        p, i, k = pl.program_id(0), pl.program_id(1), pl.program_id(2)
        x, y, c = _coords()
        step = p * ni + i
        slot = step % 2

        def send(p_, i_, slot_):
            dst = ra_ref.at[slab + p_ // hp, pl.ds(pl.multiple_of(i_ * (tm // 2), 8), tm // 2),
                            pl.ds(pl.multiple_of((p_ % hp) * N, LANES), N)]
            return pltpu.make_async_remote_copy(
                src_ref=hbuf.at[slot_], dst_ref=dst, send_sem=send_sems.at[slot_], recv_sem=recv_sem,
                device_id=(x, y, 1 - c), device_id_type=MESH)

        @pl.when(k == 0)
        def _():
            o_ref[...] = jnp.zeros_like(o_ref)

        o_ref[...] += _tn(a_ref[...], b_ref[...])

        @pl.when(k == nk - 1)
        def _():
            @pl.when(step >= 2)
            def _():
                send(p, i, slot).wait_send()

            for j in range(tm // shard_rows):
                start = pl.multiple_of(j * shard_rows + (1 - kc_ref[1]) * half, 8)
                hbuf[slot, j * half:(j + 1) * half, :] = o_ref[pl.ds(start, half), :].astype(BF16)
            send(p, i, slot).start()

        @pl.when((step == P * ni - 1) & (k == nk - 1))
        def _():
            for s in range(min(2, P * ni)):
                send(p, i, (step - s) % 2).wait_send()
            mine = ra_ref.at[slab:slab + P // hp]
            pltpu.make_async_remote_copy(src_ref=mine, dst_ref=mine, send_sem=send_sems.at[0], recv_sem=recv_sem,
                                         device_id=(x, y, 1 - c), device_id_type=MESH).wait_recv()

    return pl.pallas_call(
        body, name=name,
        grid_spec=pltpu.PrefetchScalarGridSpec(
            num_scalar_prefetch=1, grid=(P, ni, nk),
            in_specs=[pl.BlockSpec((tk, tm), lambda p, i, k, kc: (k, i)),
                      pl.BlockSpec((None, tk, N), lambda p, i, k, kc: (p, k, 0))] + [ANY] * len(extra),
            out_specs=[pl.BlockSpec((None, tm, N), lambda p, i, k, kc: (slab + p // hp, i, p % hp)), ANY],
            scratch_shapes=[pltpu.VMEM((2, tm // 2, N), BF16), pltpu.SemaphoreType.DMA((2,)),
                            pltpu.SemaphoreType.DMA]),
        out_shape=[jax.ShapeDtypeStruct((slabs, M, hp * N), F32), jax.ShapeDtypeStruct((slabs, M // 2, hp * N), BF16)],
        input_output_aliases={} if into is None else {3: 0, 4: 1},
        compiler_params=_params("arbitrary", "arbitrary", "arbitrary"),
    )(kc, a, b3, *extra)


def _mix_proj_fwd(x, vec, w_in):
    T = x.shape[0]

    def body(x_ref, vec_ref, w_hbm, h_ref, qr_ref, g_ref, k_ref, v_ref, og_ref, u_ref, ua_ref, ub_ref,
             sa_ref, sb_ref, w):
        @pl.when(pl.program_id(0) == 0)
        def _():
            pltpu.sync_copy(w_hbm, w)

        x = x_ref[...]
        sh, sc, gn, lb = vec_ref[0:1, :], vec_ref[1:2, :], vec_ref[3:4, :], vec_ref[4:5, :]
        _, _, h = _rms_fwd(x, gn, sc, sh)
        hb = h.astype(BF16)
        h_ref[...] = hb
        p = _nn(hb, w[0])
        qr_ref[...] = p[:, :D].astype(BF16)
        fg = lb + (1.0 - lb) * _sigmoid(p[:, D:])
        g_ref[...] = jnp.log(fg)
        k_ref[...] = (1.0 - fg).astype(BF16)
        p = _nn(hb, w[1])
        v_ref[...] = p[:, :D].astype(BF16)
        og_ref[...] = p[:, D:].astype(BF16)
        p = _nn(hb, w[2])
        ua, ub = p[:, :D], p[:, D:]
        u_ref[...] = ua * _sigmoid(ub)
        ua_ref[...] = ua.astype(BF16)
        ub_ref[...] = ub.astype(BF16)
        p = _nn(hb, w[3])
        sa_ref[...] = _sigmoid(p[:, :D]).astype(BF16)
        sb_ref[...] = _sigmoid(p[:, D:]).astype(BF16)

    row = pl.BlockSpec((TB, D), lambda i: (i, 0))
    bf = jax.ShapeDtypeStruct((T, D), BF16)
    f32 = jax.ShapeDtypeStruct((T, D), F32)
    return pl.pallas_call(
        body, name="mix_proj_fwd", grid=(T // TB,),
        in_specs=[row, pl.BlockSpec((8, D), lambda i: (0, 0)), ANY],
        out_specs=[row] * 11,
        out_shape=[bf, bf, f32, bf, bf, bf, f32, bf, bf, bf, bf],
        scratch_shapes=[pltpu.VMEM((NCHIP, D, 2 * D), BF16)],
        compiler_params=_params("arbitrary"),
    )(x, vec, w_in)


def _mix_proj_bwd(dxo, x, vec, dpa, dpb, dpc, w_in):
    T = x.shape[0]

    def body(dxo_ref, x_ref, vec_ref, dpa_ref, dpb_ref, dpc_ref, w_hbm, dx_ref, acc_ref, w):
        @pl.when(pl.program_id(0) == 0)
        def _():
            pltpu.sync_copy(w_hbm, w)
            acc_ref[...] = jnp.zeros_like(acc_ref)

        x = x_ref[...]
        sh, sc, gn = vec_ref[0:1, :], vec_ref[1:2, :], vec_ref[3:4, :]
        r, n, _ = _rms_fwd(x, gn, sc, sh)
        dh = jnp.zeros((TB, D), F32)
        for p in range(8):
            src = dpa_ref[p] if p < 4 else (dpb_ref[p - 4] if p < 6 else dpc_ref[p - 6])
            dh = dh + _nt(src, w[p // 2, :, (p % 2) * D:(p % 2 + 1) * D])
        dx_ref[...] = dxo_ref[...] + _rms_bwd(dh, r, n, gn, sc, acc_ref)

    row = pl.BlockSpec((TB, D), lambda i: (i, 0))
    vec8 = pl.BlockSpec((8, D), lambda i: (0, 0))
    stack = lambda k: pl.BlockSpec((k, TB, D), lambda i: (0, i, 0))
    return pl.pallas_call(
        body, name="mix_proj_bwd", grid=(T // TB,),
        in_specs=[row, row, vec8, stack(4), stack(2), stack(2), ANY],
        out_specs=[row, vec8],
        out_shape=[jax.ShapeDtypeStruct((T, D), F32), jax.ShapeDtypeStruct((8, D), F32)],
        scratch_shapes=[pltpu.VMEM((NCHIP, D, 2 * D), BF16)],
        compiler_params=_params("arbitrary"),
    )(dxo, x, vec, dpa, dpb, dpc, w_in)


def _tri(lower):
    r = lax.broadcasted_iota(jnp.int32, (CHUNK, CHUNK), 0)
    c = lax.broadcasted_iota(jnp.int32, (CHUNK, CHUNK), 1)
    return (c <= r) if lower else (c >= r)


def _cumsum_rows(mask, g):
    hi = g.astype(BF16)
    rest = g - hi.astype(F32)
    mid = rest.astype(BF16)
    low = (rest - mid.astype(F32)).astype(BF16)
    n = g.shape[1]
    p = _nn(mask.astype(BF16), jnp.concatenate([hi, mid, low], axis=1))
    return (p[:, 2 * n:] + p[:, n:2 * n]) + p[:, :n]


def _chunk_decay(low, g, nck):
    bs, mids, lasts = [], [], []
    for c in range(nck):
        gc = g[c * CHUNK:(c + 1) * CHUNK]
        bs.append(_cumsum_rows(low, gc))
        mids.append(_colsum(gc[0:CHUNK // 2]))
        lasts.append(_colsum(gc))
    spread = lambda rows: jnp.concatenate([jnp.broadcast_to(r, (CHUNK, DK)) for r in rows], axis=0)
    return jnp.concatenate(bs, axis=0), spread(mids), spread(lasts), lasts


def _hgrn_fwd(qr, g, k, v, og, vec):
    T = qr.shape[0]
    nck = CB // CHUNK

    def body(qr_ref, g_ref, k_ref, v_ref, og_ref, vec_ref, out_ref, o_ref, st_ref, state):
        @pl.when(pl.program_id(1) == 0)
        def _():
            state[...] = jnp.zeros_like(state)

        low = _tri(True)
        qv = qr_ref[...].astype(F32)
        q = qv * _sigmoid(qv) * (DK ** -0.5)
        kk = k_ref[...].astype(F32)
        vb = v_ref[...]
        b, mid, last, lasts = _chunk_decay(low, g_ref[...], nck)
        qt = (q * jnp.exp(b - mid)).astype(BF16)
        kt = (kk * jnp.exp(mid - b)).astype(BF16)
        qe = (q * jnp.exp(b)).astype(BF16)
        kd = (kk * jnp.exp(last - b)).astype(BF16)
        intra, grow = [], []
        for c in range(nck):
            r = slice(c * CHUNK, (c + 1) * CHUNK)
            att = jnp.where(low, _nt(qt[r], kt[r]), 0.0).astype(BF16)
            intra.append(_nn(att, vb[r]))
            grow.append(_tn(vb[r], kd[r]))
        st = state[...]
        inter = []
        for c in range(nck):
            stb = st.astype(BF16)
            st_ref[c] = stb
            inter.append(_nt(qe[c * CHUNK:(c + 1) * CHUNK], stb))
            st = st * jnp.exp(lasts[c]) + grow[c]
        state[...] = st
        o = jnp.concatenate(intra, axis=0) + jnp.concatenate(inter, axis=0)
        o_ref[...] = o
        ogv = og_ref[...].astype(F32)
        rms = lax.rsqrt(jnp.mean(o * o, axis=-1, keepdims=True) + EPS)
        out_ref[...] = (o * rms * vec_ref[5:6, :] * (ogv * _sigmoid(ogv))).astype(BF16)

    blk = pl.BlockSpec((CB, DK), lambda h, i: (i, h))
    return pl.pallas_call(
        body, name="hgrn_fwd", grid=(HEADS, T // CB),
        in_specs=[blk, blk, blk, blk, blk, pl.BlockSpec((8, DK), lambda h, i: (0, h))],
        out_specs=[blk, blk, pl.BlockSpec((None, nck, DK, DK), lambda h, i: (h, i, 0, 0))],
        out_shape=[jax.ShapeDtypeStruct((T, D), BF16), jax.ShapeDtypeStruct((T, D), F32),
                   jax.ShapeDtypeStruct((HEADS, T // CHUNK, DK, DK), BF16)],
        scratch_shapes=[pltpu.VMEM((DK, DK), F32)],
        compiler_params=_params("parallel", "arbitrary"),
    )(qr, g, k, v, og, vec)


def _hgrn_bwd(dout, og, qr, g, k, v, o, st, vec):
    T = qr.shape[0]
    nck = CB // CHUNK
    nb = T // CB

    def body(dout_ref, og_ref, qr_ref, g_ref, k_ref, v_ref, o_ref, st_ref, vec_ref,
             dp_ref, acc_ref, dstate):
        @pl.when(pl.program_id(1) == 0)
        def _():
            dstate[...] = jnp.zeros_like(dstate)
            acc_ref[...] = jnp.zeros_like(acc_ref)

        o = o_ref[...]
        ogv = og_ref[...].astype(F32)
        dout = dout_ref[...].astype(F32)
        hg = vec_ref[5:6, :]
        sgo = _sigmoid(ogv)
        rms = lax.rsqrt(jnp.mean(o * o, axis=-1, keepdims=True) + EPS)
        ohat = o * rms
        dp_ref[3] = (dout * (ohat * hg) * _dsilu(ogv, sgo)).astype(BF16)
        don = dout * (ogv * sgo)
        acc_ref[0:1, :] += _colsum(don * ohat)
        dohat = don * hg
        dob = (rms * (dohat - ohat * jnp.mean(dohat * ohat, axis=-1, keepdims=True))).astype(BF16)

        low = _tri(True)
        upp = _tri(False)
        lb = vec_ref[4:5, :]
        qv = qr_ref[...].astype(F32)
        sgq = _sigmoid(qv)
        q = qv * sgq * (DK ** -0.5)
        kk = k_ref[...].astype(F32)
        vb = v_ref[...]
        gv = g_ref[...]
        b, mid, last, lasts = _chunk_decay(low, gv, nck)
        eq = jnp.exp(b - mid)
        ek = jnp.exp(mid - b)
        eb = jnp.exp(b)
        ed = jnp.exp(last - b)
        qtb, ktb, qeb, kdb = ((t).astype(BF16) for t in (q * eq, kk * ek, q * eb, kk * ed))
        rows = [slice(c * CHUNK, (c + 1) * CHUNK) for c in range(nck)]

        dv1, dqt, dkt, dqe, grow = [], [], [], [], []
        for c, r in enumerate(rows):
            att = jnp.where(low, _nt(qtb[r], ktb[r]), 0.0).astype(BF16)
            datt = jnp.where(low, _nt(dob[r], vb[r]), 0.0).astype(BF16)
            dv1.append(_tn(att, dob[r]))
            dqt.append(_nn(datt, ktb[r]))
            dkt.append(_tn(datt, qtb[r]))
            dqe.append(_nn(dob[r], st_ref[c]))
            grow.append(_tn(dob[r], qeb[r]))
        ds = dstate[...]
        ds1b, dl_state = [None] * nck, [None] * nck
        for c in reversed(range(nck)):
            el = jnp.exp(lasts[c])
            ds1b[c] = ds.astype(BF16)
            dl_state[c] = el * _colsum(ds * st_ref[c].astype(F32))
            ds = ds * el + grow[c]
        dstate[...] = ds
        dkd = jnp.concatenate([_nn(vb[r], ds1b[c]) for c, r in enumerate(rows)], axis=0)
        dv = jnp.concatenate(dv1, axis=0) + jnp.concatenate([_nt(kdb[r], ds1b[c]) for c, r in enumerate(rows)], axis=0)
        dqt, dkt, dqe = (jnp.concatenate(t, axis=0) for t in (dqt, dkt, dqe))
        dq = dqt * eq + dqe * eb
        dk = dkt * ek + dkd * ed
        dkdkd = dkd * kdb.astype(F32)
        db = dqt * qtb.astype(F32) - dkt * ktb.astype(F32) + dqe * qeb.astype(F32) - dkdkd
        dg = jnp.concatenate([_cumsum_rows(upp, db[r]) + (_colsum(dkdkd[r]) + dl_state[c])
                              for c, r in enumerate(rows)], axis=0)
        fg = jnp.exp(gv)
        dfg = dg * jnp.exp(-gv) - dk
        one_m_sig = (1.0 - fg) * (1.0 / (1.0 - lb))
        dp_ref[0] = (dq * (DK ** -0.5) * _dsilu(qv, sgq)).astype(BF16)
        dp_ref[1] = (dfg * (fg - lb) * one_m_sig).astype(BF16)
        dp_ref[2] = dv.astype(BF16)
        dlb = _colsum(dfg * one_m_sig) * (lb * (1.0 - lb))
        acc_ref[1:2, :] += dlb
        acc_ref[2:3, :] -= dlb

    blk = pl.BlockSpec((CB, DK), lambda h, i: (nb - 1 - i, h))
    return pl.pallas_call(
        body, name="hgrn_bwd", grid=(HEADS, nb),
        in_specs=[blk, blk, blk, blk, blk, blk, blk,
                  pl.BlockSpec((None, nck, DK, DK), lambda h, i: (h, nb - 1 - i, 0, 0)),
                  pl.BlockSpec((8, DK), lambda h, i: (0, h))],
        out_specs=[pl.BlockSpec((4, CB, DK), lambda h, i: (0, nb - 1 - i, h)),
                   pl.BlockSpec((8, DK), lambda h, i: (0, h))],
        out_shape=[jax.ShapeDtypeStruct((4, T, D), BF16), jax.ShapeDtypeStruct((8, D), F32)],
        scratch_shapes=[pltpu.VMEM((DK, DK), F32)],
        compiler_params=_params("parallel", "arbitrary"),
    )(dout, og, qr, g, k, v, o, st, vec)


def _ln_fwd(uc, lg, lbias):
    mu = jnp.mean(uc, axis=-1, keepdims=True)
    xc = uc - mu
    rstd = lax.rsqrt(jnp.mean(xc * xc, axis=-1, keepdims=True) + EPS)
    z = xc * rstd
    return rstd, z, z * lg + lbias


LANES = 128
SUBLANES = 8
CONV_ROWS = 64


def _lane_tiles():
    return [slice(l * LANES, (l + 1) * LANES) for l in range(D // LANES)]


def _row_shifts(x):
    n = x.shape[0]
    return [x] + [pltpu.roll(x, n - r, axis=0) for r in range(1, SUBLANES)]


TAPS_PAST = tuple(HALO - (CONV_K - 1) + j for j in range(CONV_K))
TAPS_AHEAD = tuple(CONV_K - 1 - j for j in range(CONV_K))


def _tap_windows(shifted, starts, r0, rows):
    for r in range(SUBLANES):
        taps = [(j, s // SUBLANES) for j, s in enumerate(starts) if s % SUBLANES == r]
        if not taps:
            continue
        lo = min(a for _, a in taps)
        hi = max(a for _, a in taps)
        span = shifted[r][r0 + lo * SUBLANES:r0 + hi * SUBLANES + rows]
        for j, a in taps:
            yield j, span[(a - lo) * SUBLANES:(a - lo) * SUBLANES + rows]


def _conv_fwd(u, cw, cvec):
    T = u.shape[0]
    per = TB // HALO

    def body(u_ref, halo_ref, cw_ref, cvec_ref, us_ref, uc_ref, pad):
        i = pl.program_id(0)
        pad[0:HALO, :] = jnp.where(i > 0, halo_ref[...], 0.0)
        pad[HALO:, :] = u_ref[...]
        for lanes in _lane_tiles():
            shifted = _row_shifts(pad[:, lanes])
            taps = cw_ref[:, lanes]
            for r0 in range(0, TB, CONV_ROWS):
                acc = jnp.broadcast_to(cvec_ref[0:1, lanes], (CONV_ROWS, LANES))
                for j, window in _tap_windows(shifted, TAPS_PAST, r0, CONV_ROWS):
                    acc = acc + taps[j:j + 1] * window
                uc_ref[r0:r0 + CONV_ROWS, lanes] = acc
        _, _, ul = _ln_fwd(uc_ref[...], cvec_ref[1:2, :], cvec_ref[2:3, :])
        us_ref[...] = (ul * _sigmoid(ul)).astype(BF16)

    row = pl.BlockSpec((TB, D), lambda i: (i, 0))
    return pl.pallas_call(
        body, name="conv_fwd", grid=(T // TB,),
        in_specs=[row, pl.BlockSpec((HALO, D), lambda i: (jnp.maximum(i * per - 1, 0), 0)),
                  pl.BlockSpec((32, D), lambda i: (0, 0)), pl.BlockSpec((8, D), lambda i: (0, 0))],
        out_specs=[row, row],
        out_shape=[jax.ShapeDtypeStruct((T, D), BF16), jax.ShapeDtypeStruct((T, D), F32)],
        scratch_shapes=[pltpu.VMEM((TB + HALO, D), F32)],
        compiler_params=_params("parallel"),
    )(u, u, cw, cvec)


def _conv_bwd_taps(duc, u, ua, ub, cw):
    T = u.shape[0]
    per = TB // HALO
    nblk = T // TB

    def body(duc_ref, dnext_ref, u_ref, uprev_ref, ua_ref, ub_ref, cw_ref, dp_ref, dcw_ref, upad, dpad, dcw):
        i = pl.program_id(0)

        @pl.when(i == 0)
        def _():
            dcw[...] = jnp.zeros_like(dcw)

        upad[0:HALO, :] = jnp.where(i > 0, uprev_ref[...], 0.0)
        upad[HALO:, :] = u_ref[...]
        dpad[0:TB, :] = duc_ref[...]
        dpad[TB:, :] = jnp.where(i < nblk - 1, dnext_ref[...], 0.0)
        for lanes in _lane_tiles():
            ushift = _row_shifts(upad[:, lanes])
            dshift = _row_shifts(dpad[:, lanes])
            for r0 in range(0, TB, CONV_ROWS):
                rows = slice(r0, r0 + CONV_ROWS)
                duc = duc_ref[rows, lanes]
                for j, window in _tap_windows(ushift, TAPS_PAST, r0, CONV_ROWS):
                    prod = duc * window
                    dcw[j, :, lanes] += jnp.sum(prod.reshape(CONV_ROWS // SUBLANES, SUBLANES, LANES), axis=0)
                du = jnp.zeros((CONV_ROWS, LANES), F32)
                for j, window in _tap_windows(dshift, TAPS_AHEAD, r0, CONV_ROWS):
                    du = du + cw_ref[j:j + 1, lanes] * window
                ua = ua_ref[rows, lanes].astype(F32)
                sg = _sigmoid(ub_ref[rows, lanes].astype(F32))
                dp_ref[0, rows, lanes] = (du * sg).astype(BF16)
                dp_ref[1, rows, lanes] = (du * ua * sg * (1.0 - sg)).astype(BF16)

        @pl.when(i == nblk - 1)
        def _():
            dcw_ref[...] = jnp.sum(dcw[...], axis=1)

    row = pl.BlockSpec((TB, D), lambda i: (i, 0))
    return pl.pallas_call(
        body, name="conv_bwd_taps", grid=(nblk,),
        in_specs=[row, pl.BlockSpec((HALO, D), lambda i: (jnp.minimum((i + 1) * per, T // HALO - 1), 0)),
                  row, pl.BlockSpec((HALO, D), lambda i: (jnp.maximum(i * per - 1, 0), 0)),
                  row, row, pl.BlockSpec((32, D), lambda i: (0, 0))],
        out_specs=[pl.BlockSpec((2, TB, D), lambda i: (0, i, 0)), pl.BlockSpec((32, D), lambda i: (0, 0))],
        out_shape=[jax.ShapeDtypeStruct((2, T, D), BF16), jax.ShapeDtypeStruct((32, D), F32)],
        scratch_shapes=[pltpu.VMEM((TB + HALO, D), F32), pltpu.VMEM((TB + HALO, D), F32),
                        pltpu.VMEM((32, SUBLANES, D), F32)],
        compiler_params=_params("arbitrary"),
    )(duc, duc, u, u, ua, ub, cw)


def _merge_fwd(x, oa, us, sa, sb, vec, w_ho, w_co, w_mo):
    T = x.shape[0]

    def body(x_ref, oa_ref, us_ref, sa_ref, sb_ref, vec_ref, who_hbm, wco_hbm, wmo_hbm,
             xo_ref, ya_ref, yb_ref, mg_ref, mo_ref, who, wco, wmo):
        @pl.when(pl.program_id(0) == 0)
        def _():
            pltpu.sync_copy(who_hbm, who)
            pltpu.sync_copy(wco_hbm, wco)
            pltpu.sync_copy(wmo_hbm, wmo)

        ya = _nn(oa_ref[...], who[...])
        yb = _nn(us_ref[...], wco[...])
        mg = (sa_ref[...].astype(F32) * ya + sb_ref[...].astype(F32) * yb).astype(BF16)
        mo = _nn(mg, wmo[...])
        xo_ref[...] = x_ref[...] + vec_ref[2:3, :] * mo
        ya_ref[...] = ya.astype(BF16)
        yb_ref[...] = yb.astype(BF16)
        mg_ref[...] = mg
        mo_ref[...] = mo.astype(BF16)

    row = pl.BlockSpec((TB, D), lambda i: (i, 0))
    bf = jax.ShapeDtypeStruct((T, D), BF16)
    wv = pltpu.VMEM((D, D), BF16)
    return pl.pallas_call(
        body, name="merge_fwd", grid=(T // TB,),
        in_specs=[row, row, row, row, row, pl.BlockSpec((8, D), lambda i: (0, 0)), ANY, ANY, ANY],
        out_specs=[row] * 5,
        out_shape=[jax.ShapeDtypeStruct((T, D), F32), bf, bf, bf, bf],
        scratch_shapes=[wv, wv, wv],
        compiler_params=_params("arbitrary"),
    )(x, oa, us, sa, sb, vec, w_ho, w_co, w_mo)


def _merge_bwd(dxo, mo, ya, yb, sa, sb, uc, vec, cvec, w_ho, w_co, w_mo):
    T = dxo.shape[0]

    def body(dxo_ref, mo_ref, ya_ref, yb_ref, sa_ref, sb_ref, uc_ref, vec_ref, cvec_ref, who_hbm, wco_hbm, wmo_hbm,
             dmo_ref, dya_ref, dyb_ref, doa_ref, duc_ref, dp_ref, acc_ref, cacc_ref, who, wco, wmo):
        @pl.when(pl.program_id(0) == 0)
        def _():
            pltpu.sync_copy(who_hbm, who)
            pltpu.sync_copy(wco_hbm, wco)
            pltpu.sync_copy(wmo_hbm, wmo)
            acc_ref[...] = jnp.zeros_like(acc_ref)
            cacc_ref[...] = jnp.zeros_like(cacc_ref)

        dxo = dxo_ref[...]
        acc_ref[2:3, :] += _colsum(mo_ref[...].astype(F32) * dxo)
        dmo = (vec_ref[2:3, :] * dxo).astype(BF16)
        dmo_ref[...] = dmo
        dmg = _nt(dmo, wmo[...])
        sa = sa_ref[...].astype(F32)
        sb = sb_ref[...].astype(F32)
        dya = (sa * dmg).astype(BF16)
        dyb = (sb * dmg).astype(BF16)
        dya_ref[...] = dya
        dyb_ref[...] = dyb
        dp_ref[0] = (dmg * ya_ref[...].astype(F32) * sa * (1.0 - sa)).astype(BF16)
        dp_ref[1] = (dmg * yb_ref[...].astype(F32) * sb * (1.0 - sb)).astype(BF16)
        doa_ref[...] = _nt(dya, who[...]).astype(BF16)
        dus = _nt(dyb, wco[...])
        lg = cvec_ref[1:2, :]
        rstd, z, ul = _ln_fwd(uc_ref[...], lg, cvec_ref[2:3, :])
        dul = dus * _dsilu(ul, _sigmoid(ul))
        cacc_ref[1:2, :] += _colsum(dul * z)
        cacc_ref[2:3, :] += _colsum(dul)
        dz = dul * lg
        duc = rstd * (dz - jnp.mean(dz, axis=-1, keepdims=True) - z * jnp.mean(dz * z, axis=-1, keepdims=True))
        cacc_ref[0:1, :] += _colsum(duc)
        duc_ref[...] = duc

    row = pl.BlockSpec((TB, D), lambda i: (i, 0))
    one = pl.BlockSpec((None, TB, D), lambda i: (0, i, 0))
    vec8 = pl.BlockSpec((8, D), lambda i: (0, 0))
    bf = jax.ShapeDtypeStruct((T, D), BF16)
    bf1 = jax.ShapeDtypeStruct((1, T, D), BF16)
    acc = jax.ShapeDtypeStruct((8, D), F32)
    wv = pltpu.VMEM((D, D), BF16)
    return pl.pallas_call(
        body, name="merge_bwd", grid=(T // TB,),
        in_specs=[row, row, row, row, row, row, row, vec8, vec8, ANY, ANY, ANY],
        out_specs=[one, one, one, row, row, pl.BlockSpec((2, TB, D), lambda i: (0, i, 0)), vec8, vec8],
        out_shape=[bf1, bf1, bf1, bf, jax.ShapeDtypeStruct((T, D), F32), jax.ShapeDtypeStruct((2, T, D), BF16), acc, acc],
        scratch_shapes=[wv, wv, wv],
        compiler_params=_params("arbitrary"),
    )(dxo, mo, ya, yb, sa, sb, uc, vec, cvec, w_ho, w_co, w_mo)


def _pack_rows(parts, total, name, slot=None):
    def body(*refs):
        out = refs[-1]
        out[...] = jnp.zeros_like(out)
        for ref, (_, src, n, dst) in zip(refs[-1 - len(parts):-1], parts):
            out[dst:dst + n, :] = ref[src:src + n, :]

    arrs = [p[0] for p in parts]
    if slot is None:
        return pl.pallas_call(
            body, name=name, in_specs=[pl.BlockSpec(a.shape, lambda: (0, 0)) for a in arrs],
            out_specs=pl.BlockSpec((total, D), lambda: (0, 0)),
            out_shape=jax.ShapeDtypeStruct((total, D), F32),
        )(*arrs)
    return pl.pallas_call(
        body, name=name,
        grid_spec=pltpu.PrefetchScalarGridSpec(
            num_scalar_prefetch=1, grid=(1,),
            in_specs=[pl.BlockSpec(a.shape, lambda i, s: (0, 0)) for a in arrs],
            out_specs=pl.BlockSpec((None, total, D), lambda i, s: (s[0], 0, 0))),
        out_shape=jax.ShapeDtypeStruct((8, total, D), F32),
    )(slot, *arrs)


PACK_ROWS = 56
PACK_AT = {"ada_b": 0, "loss": 9, "norm_ffn1": 10, "norm_mix": 11, "hgrn_g": 12, "conv_b": 13, "conv_ln_g": 14,
           "conv_ln_b": 15, "norm_ffn2": 16, "norm_final": 17, "hgrn_lb": 18, "conv_w": 20}


def _local_step(x, tgt, mod, small, kc, weight, reduce, reduce_small):
    lb = jax.nn.sigmoid(small["hgrn_lb"][0:1] - small["hgrn_lb"][1:2])
    vec1 = _pack_rows([(mod, 0, 3, 0), (small["norm_ffn1"], 0, 1, 3)], 8, "pack_vec1")
    vec2 = _pack_rows([(mod, 3, 3, 0), (small["norm_mix"], 0, 1, 3), (lb, 0, 1, 4), (small["hgrn_g"], 0, 1, 5)],
                      8, "pack_vec2")
    vec3 = _pack_rows([(mod, 6, 3, 0), (small["norm_ffn2"], 0, 1, 3)], 8, "pack_vec3")
    cvec = _pack_rows([(small["conv_b"], 0, 1, 0), (small["conv_ln_g"], 0, 1, 1), (small["conv_ln_b"], 0, 1, 2)],
                      8, "pack_cvec")
    cw = small["conv_w"]
    gvec = _pack_rows([(small["norm_final"], 0, 1, 0)], 8, "pack_gvec")

    wg = {n: weight(n, vec1) for n in ("ffn1_w_in", "ffn1_w_out")}
    x1, h1, a1, b1, s1, f1 = _ffn_fwd(x, vec1, wg["ffn1_w_in"], wg["ffn1_w_out"], "ffn1_fwd")
    wg["mix_w_in"] = weight("mix_w_in", x1)
    h2, qr, g, k, v, og, u, ua, ub, sa, sb = _mix_proj_fwd(x1, vec2, wg["mix_w_in"])
    oa, o, st = _hgrn_fwd(qr, g, k, v, og, vec2)
    us, uc = _conv_fwd(u, cw, cvec)
    wg.update({n: weight(n, us) for n in ("hgrn_w_o", "conv_w_o", "mix_w_out")})
    x2, ya, yb, mg, mo = _merge_fwd(x1, oa, us, sa, sb, vec2, wg["hgrn_w_o"], wg["conv_w_o"], wg["mix_w_out"])
    wg.update({n: weight(n, x2) for n in ("ffn2_w_in", "ffn2_w_out")})
    dx3, h3, a3, b3, s3, f3, acc_head = _ffn_fwd(x2, vec3, wg["ffn2_w_in"], wg["ffn2_w_out"], "ffn2_fwd",
                                                 head=(tgt, gvec))

    dx2, df3, dab3, acc3 = _ffn_bwd(dx3, x2, vec3, a3, b3, f3, wg["ffn2_w_in"], wg["ffn2_w_out"], "ffn2_bwd")
    tok = reduce(("ffn2_w_out", "ffn2_w_in"), [_mm_tn(s3, df3, 1, kc, DFF // NCHIP, "ffn2_dwout"),
                                               _mm_tn(h3, dab3, 1, kc, D, "ffn2_dwin")])
    vec2b = vec2 + tok[0:1, 0:1]
    dmo, dya, dyb, doa, duc, dpc, acc_m, acc_c = _merge_bwd(dx2, mo, ya, yb, sa, sb, uc, vec2b, cvec,
                                                            wg["hgrn_w_o"], wg["conv_w_o"], wg["mix_w_out"])
    tok = reduce(("mix_w_out", "hgrn_w_o", "conv_w_o"),
                 [_mm_tn(mg, dmo, 1, kc, D // NCHIP, "mix_dwout"), _mm_tn(oa, dya, 1, kc, D // NCHIP, "hgrn_dwo"),
                  _mm_tn(us, dyb, 1, kc, D // NCHIP, "conv_dwo")])
    vec2c = vec2 + tok[0:1, 0:1]
    dpb, dcw = _conv_bwd_taps(duc, u, ua, ub, cw)
    dpa, acc_h = _hgrn_bwd(doa, og, qr, g, k, v, o, st, vec2c)
    dx1, acc2 = _mix_proj_bwd(dx2, x1, vec2c, dpa, dpb, dpc, wg["mix_w_in"])
    gmix = _mm_tn(h2, dpa, 2, kc, D, "mix_dwin_a", slabs=NCHIP)
    gmix = _mm_tn(h2, dpb, 2, kc, D, "mix_dwin_b", into=gmix, slab=2, slabs=NCHIP)
    gmix = _mm_tn(h2, dpc, 2, kc, D, "mix_dwin_c", into=gmix, slab=3, slabs=NCHIP)
    tok = reduce(("mix_w_in",), [gmix])
    vec1b = vec1 + tok[0:1, 0:1]
    dx0, df1, dab1, acc1 = _ffn_bwd(dx1, x, vec1b, a1, b1, f1, wg["ffn1_w_in"], wg["ffn1_w_out"], "ffn1_bwd")

    at = PACK_AT
    finish_small = reduce_small([
        (acc1, 0, 3, at["ada_b"]), (acc2, 0, 2, at["ada_b"] + 3), (acc_m, 2, 1, at["ada_b"] + 5),
        (acc3, 0, 3, at["ada_b"] + 6), (acc_head, 1, 1, at["loss"]), (acc1, 3, 1, at["norm_ffn1"]),
        (acc2, 3, 1, at["norm_mix"]), (acc_h, 0, 1, at["hgrn_g"]), (acc_c, 0, 3, at["conv_b"]),
        (acc3, 3, 1, at["norm_ffn2"]), (acc_head, 0, 1, at["norm_final"]), (acc_h, 1, 2, at["hgrn_lb"]),
        (dcw, 0, CONV_K, at["conv_w"])])
    finish_small, tok = finish_small
    last = [_mm_tn(s1, df1, 1, kc, DFF // NCHIP, "ffn1_dwout", after=tok),
            _mm_tn(h1, dab1, 1, kc, D, "ffn1_dwin", after=tok)]
    reduce(("ffn1_w_out", "ffn1_w_in"), last, finish_small(last[1][0]))
    return dx0


BLOCK_BYTES = 5 * 512 * 1024


def _row_block(rows, cols):
    for br in (512, 352, 256, 176, 128, 64, 32, 16, 8):
        if rows % br == 0 and br * cols * 4 <= BLOCK_BYTES:
            return br
    return rows


def _cast_into_slot(w, kc, name, after):
    R, C = w.shape
    br = _row_block(R, C)

    def body(kc_ref, w_ref, after_ref, o_ref):
        o_ref[...] = w_ref[...].astype(BF16)

    return pl.pallas_call(
        body, name=name,
        grid_spec=pltpu.PrefetchScalarGridSpec(
            num_scalar_prefetch=1, grid=(R // br,),
            in_specs=[pl.BlockSpec((br, C), lambda i, kc: (i, 0)), ANY],
            out_specs=pl.BlockSpec((None, br, C), lambda i, kc: (kc[0], i, 0))),
        out_shape=jax.ShapeDtypeStruct((NCHIP, R, C), BF16), compiler_params=_params("parallel"),
    )(kc, w, after)


def _adamw_math(w, g, m, v):
    nm = ADAM_B1 * m + (1.0 - ADAM_B1) * g
    nv = ADAM_B2 * v + (1.0 - ADAM_B2) * (g * g)
    m_hat = nm / (1.0 - ADAM_B1 ** ADAM_STEP)
    v_hat = nv / (1.0 - ADAM_B2 ** ADAM_STEP)
    return -ADAM_LR * (m_hat / (jnp.sqrt(v_hat) + ADAM_EPS) + ADAM_WD * w), nm, nv


def _adamw_rows(params, gsum, rows, after, name):
    n = len(params)

    def body(*refs):
        g_ref = refs[3 * n]
        outs = refs[3 * n + 2:]
        for i, r0 in enumerate(rows):
            w_ref, m_ref, v_ref = refs[3 * i:3 * i + 3]
            g = g_ref[r0:r0 + w_ref.shape[0], :]
            outs[4 * i][...] = g
            outs[4 * i + 1][...], outs[4 * i + 2][...], outs[4 * i + 3][...] = _adamw_math(
                w_ref[...], g, m_ref[...], v_ref[...])

    flat = [a for p in params for a in p]
    full = lambda a: pl.BlockSpec(a.shape, lambda: (0, 0))
    out = pl.pallas_call(
        body, name=name, in_specs=[full(a) for a in flat] + [full(gsum), ANY],
        out_specs=[full(p[0]) for p in params for _ in range(4)],
        out_shape=[jax.ShapeDtypeStruct(p[0].shape, F32) for p in params for _ in range(4)],
    )(*flat, gsum, after)
    return [out[4 * i:4 * i + 4] for i in range(n)]


def _adamw(w, g, m, v, name, after=None, copy_grad=False):
    R, C = w.shape
    br = _row_block(R, C)
    extra = [] if after is None else [after]
    nout = 4 if copy_grad else 3

    def body(w_ref, g_ref, m_ref, v_ref, *rest):
        d_ref, nm_ref, nv_ref = rest[-nout:][:3]
        gv = g_ref[...]
        if copy_grad:
            rest[-1][...] = gv
        d_ref[...], nm_ref[...], nv_ref[...] = _adamw_math(w_ref[...], gv, m_ref[...], v_ref[...])

    blk = pl.BlockSpec((br, C), lambda i: (i, 0))
    out = jax.ShapeDtypeStruct((R, C), F32)
    return pl.pallas_call(
        body, name=name, grid=(R // br,), in_specs=[blk] * 4 + [ANY] * len(extra), out_specs=[blk] * nout,
        out_shape=[out] * nout, compiler_params=_params("parallel"),
    )(w, g, m, v, *extra)


def _coords():
    return lax.axis_index("x"), lax.axis_index("y"), lax.axis_index("c")


def _flip(v, bit):
    return 1 - v if bit else v


HBM = pl.BlockSpec(memory_space=pltpu.HBM)
SEM = pl.BlockSpec(memory_space=pltpu.SEMAPHORE)
EFFECT = pltpu.SideEffectType.DATAFLOW_SIDE_EFFECTING


def _peer8(x, y, c, m):
    px, py, pc = _flip(x, m & 4), _flip(y, m & 2), _flip(c, m & 1)
    return (px, py, pc), 4 * px + 2 * py + pc


def _allgather8_start(blocks, name):
    def body(b_ref, send, recv, thru, token):
        x, y, c = _coords()
        me = 4 * x + 2 * y + c
        for m in range(1, 8):
            peer, _ = _peer8(x, y, c, m)
            pltpu.make_async_remote_copy(src_ref=b_ref.at[me], dst_ref=b_ref.at[me], send_sem=send.at[m - 1],
                                         recv_sem=recv.at[m - 1], device_id=peer, device_id_type=MESH).start()
        token[...] = jnp.zeros_like(token)

    sem = pltpu.SemaphoreType.DMA((7,))
    return pl.pallas_call(
        body, name=name,
        out_shape=[sem, sem, pltpu.HBM(blocks.shape, blocks.dtype), jax.ShapeDtypeStruct((8, 128), F32)],
        in_specs=[HBM], out_specs=[SEM, SEM, HBM, pl.BlockSpec(memory_space=pltpu.VMEM)], input_output_aliases={0: 2},
        compiler_params=pltpu.CompilerParams(has_side_effects=EFFECT),
    )(pltpu.with_memory_space_constraint(blocks, pltpu.HBM))


def _allgather8_wait(blocks, send_sem, recv_sem, after, name):
    def body(b_ref, send, recv, after_ref, thru):
        x, y, c = _coords()
        me = 4 * x + 2 * y + c
        for m in range(1, 8):
            peer, sender = _peer8(x, y, c, m)
            cp = pltpu.make_async_remote_copy(src_ref=b_ref.at[me], dst_ref=b_ref.at[sender], send_sem=send.at[m - 1],
                                              recv_sem=recv.at[m - 1], device_id=peer, device_id_type=MESH)
            cp.wait_send()
            cp.wait_recv()

    return pl.pallas_call(
        body, name=name, out_shape=pltpu.HBM(blocks.shape, blocks.dtype),
        in_specs=[HBM, SEM, SEM, ANY], out_specs=HBM, input_output_aliases={0: 0},
        compiler_params=pltpu.CompilerParams(has_side_effects=EFFECT),
    )(blocks, send_sem, recv_sem, after)


def _chip_peer(x, y, m):
    px, py = _flip(x, m & 2), _flip(y, m & 1)
    return px, py, 2 * px + py


def _core_rows(land, c):
    half = land.shape[1] // 2
    return pl.ds(pl.multiple_of(c * half, 16), half)


def _gather_start(lands, groups, halved, after, name):
    n, ng, na = len(lands), len(groups), len(after)

    def body(*refs):
        ins = refs[:n]
        sends, recvs = refs[n + na:n + na + ng], refs[n + na + ng:n + na + 2 * ng]
        token = refs[n + na + 2 * ng + n]
        x, y, c = _coords()
        k = 2 * x + y
        for gi, grp in enumerate(groups):
            for j, t in enumerate(grp):
                mine = ins[t].at[k, _core_rows(ins[t], c), :] if halved[gi] else ins[t].at[k]
                for m in (1, 2, 3):
                    px, py, _ = _chip_peer(x, y, m)
                    pltpu.make_async_remote_copy(
                        src_ref=mine, dst_ref=mine, send_sem=sends[gi].at[3 * j + m - 1],
                        recv_sem=recvs[gi].at[3 * j + m - 1], device_id=(px, py, c), device_id_type=MESH).start()
        token[...] = jnp.zeros_like(token)

    sems = [pltpu.SemaphoreType.DMA((3 * len(g),)) for g in groups]
    out = pl.pallas_call(
        body, name=name,
        out_shape=sems + sems + [pltpu.HBM(a.shape, a.dtype) for a in lands] + [jax.ShapeDtypeStruct((8, 128), F32)],
        in_specs=[HBM] * n + [ANY] * na,
        out_specs=[SEM] * (2 * ng) + [HBM] * n + [pl.BlockSpec(memory_space=pltpu.VMEM)],
        input_output_aliases={t: 2 * ng + t for t in range(n)},
        compiler_params=pltpu.CompilerParams(has_side_effects=EFFECT),
    )(*[pltpu.with_memory_space_constraint(a, pltpu.HBM) for a in lands], *after)
    return out[:ng], out[ng:2 * ng], out[2 * ng:2 * ng + n], out[2 * ng + n]


def _gather_wait(lands, halved, send_sem, recv_sem, after, name):
    n = len(lands)

    def body(*refs):
        ins, send, recv = refs[:n], refs[n], refs[n + 1]
        x, y, c = _coords()
        k = 2 * x + y
        for j in range(n):
            rows = _core_rows(ins[j], c)
            for m in (1, 2, 3):
                px, py, pk = _chip_peer(x, y, m)
                cp = pltpu.make_async_remote_copy(
                    src_ref=ins[j].at[k, rows, :] if halved else ins[j].at[k],
                    dst_ref=ins[j].at[pk, rows, :] if halved else ins[j].at[pk], send_sem=send.at[3 * j + m - 1],
                    recv_sem=recv.at[3 * j + m - 1], device_id=(px, py, c), device_id_type=MESH)
                cp.wait_send()
                cp.wait_recv()

    return pl.pallas_call(
        body, name=name, out_shape=[pltpu.HBM(a.shape, a.dtype) for a in lands],
        in_specs=[HBM] * n + [SEM, SEM, ANY], out_specs=[HBM] * n,
        input_output_aliases={j: j for j in range(n)},
        compiler_params=pltpu.CompilerParams(has_side_effects=EFFECT),
    )(*lands, send_sem, recv_sem, after)


def _sibling_fill(lands, name):
    n = len(lands)

    def body(*refs):
        ins = refs[:n]
        send_sems, recv_sems = refs[2 * n:]
        x, y, c = _coords()
        sends, recvs = [], []
        for t in range(n):
            for m in (1, 2, 3):
                _, _, pk = _chip_peer(x, y, m)
                for rows, lst in ((_core_rows(ins[t], c), sends), (_core_rows(ins[t], 1 - c), recvs)):
                    lst.append(pltpu.make_async_remote_copy(
                        src_ref=ins[t].at[pk, rows, :], dst_ref=ins[t].at[pk, rows, :],
                        send_sem=send_sems.at[3 * t + m - 1], recv_sem=recv_sems.at[3 * t + m - 1],
                        device_id=(x, y, 1 - c), device_id_type=MESH))
        for cp in sends:
            cp.start()
        for cp in recvs:
            cp.wait_recv()
        for cp in sends:
            cp.wait_send()

    return pl.pallas_call(
        body, name=name, in_specs=[ANY] * n, out_specs=[ANY] * n,
        out_shape=[jax.ShapeDtypeStruct(a.shape, a.dtype) for a in lands],
        input_output_aliases={t: t for t in range(n)},
        scratch_shapes=[pltpu.SemaphoreType.DMA((3 * n,)), pltpu.SemaphoreType.DMA((3 * n,))],
    )(*lands)


def _scatter_start(srcs, name, after=()):
    n, na = len(srcs), len(after)

    def body(*refs):
        ins, lands = refs[:n], refs[n:2 * n]
        send, recv = refs[2 * n + na], refs[2 * n + na + 1]
        token = refs[2 * n + na + 2 + 2 * n]
        x, y, c = _coords()
        k = 2 * x + y
        for t in range(n):
            for m in (1, 2, 3):
                px, py, pk = _chip_peer(x, y, m)
                pltpu.make_async_remote_copy(
                    src_ref=ins[t].at[pk], dst_ref=lands[t].at[k], send_sem=send.at[3 * t + m - 1],
                    recv_sem=recv.at[3 * t + m - 1], device_id=(px, py, c), device_id_type=MESH).start()
        token[...] = jnp.zeros_like(token)

    sem = pltpu.SemaphoreType.DMA((3 * n,))
    hbm = [pltpu.HBM(a.shape, a.dtype) for a in srcs]
    operands = list(srcs) + [lax.empty(a.shape, a.dtype) for a in srcs]
    out = pl.pallas_call(
        body, name=name, out_shape=[sem, sem] + hbm + hbm + [jax.ShapeDtypeStruct((8, 128), F32)],
        in_specs=[HBM] * (2 * n) + [ANY] * na,
        out_specs=[SEM, SEM] + [HBM] * (2 * n) + [pl.BlockSpec(memory_space=pltpu.VMEM)],
        input_output_aliases={t: 2 + t for t in range(2 * n)},
        compiler_params=pltpu.CompilerParams(has_side_effects=EFFECT),
    )(*[pltpu.with_memory_space_constraint(a, pltpu.HBM) for a in operands], *after)
    return out[0], out[1], out[2:2 + n], out[2 + n:2 + 2 * n], out[2 + 2 * n]


def _scatter_wait(srcs, lands, send_sem, recv_sem, after, name):
    n = len(srcs)

    def body(*refs):
        ins, land = refs[:n], refs[n:2 * n]
        send, recv = refs[2 * n], refs[2 * n + 1]
        x, y, c = _coords()
        for t in range(n):
            for m in (1, 2, 3):
                px, py, pk = _chip_peer(x, y, m)
                cp = pltpu.make_async_remote_copy(
                    src_ref=ins[t].at[pk], dst_ref=land[t].at[pk], send_sem=send.at[3 * t + m - 1],
                    recv_sem=recv.at[3 * t + m - 1], device_id=(px, py, c), device_id_type=MESH)
                cp.wait_send()
                cp.wait_recv()

    hbm = [pltpu.HBM(a.shape, a.dtype) for a in srcs]
    out = pl.pallas_call(
        body, name=name, out_shape=hbm + hbm, in_specs=[HBM] * (2 * n) + [SEM, SEM, ANY], out_specs=[HBM] * (2 * n),
        input_output_aliases={t: t for t in range(2 * n)},
        compiler_params=pltpu.CompilerParams(has_side_effects=EFFECT),
    )(*srcs, *lands, send_sem, recv_sem, after)
    return out[:n], out[n:]


def _sum_own_half(g, ra, kc, name):
    _, R, C = g.shape
    half = R // 2
    br = _row_block(half, C)
    nb = half // br

    def body(kc_ref, g_ref, ra_ref, o_ref):
        o_ref[...] = (g_ref[...] + ra_ref[...].astype(F32)).astype(BF16)

    return pl.pallas_call(
        body, name=name,
        grid_spec=pltpu.PrefetchScalarGridSpec(
            num_scalar_prefetch=1, grid=(NCHIP, nb),
            in_specs=[pl.BlockSpec((None, br, C), lambda j, i, kc: (j, kc[1] * nb + i, 0)),
                      pl.BlockSpec((None, br, C), lambda j, i, kc: (j, i, 0))],
            out_specs=pl.BlockSpec((None, br, C), lambda j, i, kc: (j, i, 0))),
        out_shape=jax.ShapeDtypeStruct((NCHIP, half, C), BF16),
        compiler_params=_params("parallel", "parallel"),
    )(kc, g, ra)


def _sum_chips(sa, rb, kc, name, after=None):
    _, half, C = rb.shape
    br = _row_block(half, C)
    nb = half // br
    extra = [] if after is None else [after]

    def body(kc_ref, own_ref, r1_ref, r2_ref, r3_ref, *rest):
        out, obuf, local_sems, send_sems, recv_sem = rest[-5:]
        i = pl.program_id(0)
        slot = i % 2
        x, y, c = _coords()

        def copies(i_, slot_):
            rows = out.at[pl.ds(pl.multiple_of((c * nb + i_) * br, 8), br), :]
            return (pltpu.make_async_copy(obuf.at[slot_], rows, local_sems.at[slot_]),
                    pltpu.make_async_remote_copy(src_ref=obuf.at[slot_], dst_ref=rows, send_sem=send_sems.at[slot_],
                                                 recv_sem=recv_sem, device_id=(x, y, 1 - c), device_id_type=MESH))

        @pl.when(i >= 2)
        def _():
            here, there = copies(i, slot)
            here.wait()
            there.wait_send()

        acc = own_ref[...].astype(F32) + r1_ref[...].astype(F32)
        obuf[slot] = (acc + r2_ref[...].astype(F32)) + r3_ref[...].astype(F32)
        here, there = copies(i, slot)
        here.start()
        there.start()

        @pl.when(i == nb - 1)
        def _():
            for s in range(min(2, nb)):
                here, there = copies(i, (i - s) % 2)
                here.wait()
                there.wait_send()
            theirs = out.at[pl.ds(pl.multiple_of((1 - c) * half, 8), half), :]
            pltpu.make_async_remote_copy(src_ref=theirs, dst_ref=theirs, send_sem=send_sems.at[0], recv_sem=recv_sem,
                                         device_id=(x, y, 1 - c), device_id_type=MESH).wait_recv()

    def slab(m):
        return pl.BlockSpec((None, br, C), lambda i, kc: (kc[0] ^ m, i, 0))

    return pl.pallas_call(
        body, name=name,
        grid_spec=pltpu.PrefetchScalarGridSpec(
            num_scalar_prefetch=1, grid=(nb,),
            in_specs=[slab(0), slab(1), slab(2), slab(3)] + [ANY] * len(extra),
            out_specs=ANY,
            scratch_shapes=[pltpu.VMEM((2, br, C), F32), pltpu.SemaphoreType.DMA((2,)), pltpu.SemaphoreType.DMA((2,)),
                            pltpu.SemaphoreType.DMA]),
        out_shape=jax.ShapeDtypeStruct((2 * half, C), F32), compiler_params=_params("arbitrary"),
    )(kc, sa, rb, rb, rb, *extra)


def _sum8(ga, name):
    _, R, C = ga.shape

    def body(g_ref, o_ref):
        acc = g_ref[0]
        for j in range(1, 8):
            acc = acc + g_ref[j]
        o_ref[...] = acc

    return pl.pallas_call(
        body, name=name, in_specs=[pl.BlockSpec((8, R, C), lambda: (0, 0, 0))],
        out_specs=pl.BlockSpec((R, C), lambda: (0, 0)), out_shape=jax.ShapeDtypeStruct((R, C), F32),
    )(ga)


ADA_COLS = 9 * D // NCHIP
ADA_BLK = 256


def _ada_mod(c_all, ada_w, ada_b, kme):
    def body(k_ref, c_ref, w_ref, b_ref, o_ref):
        cv = c_ref[...]
        cs = cv * _sigmoid(cv)
        o_ref[...] = jnp.dot(cs, w_ref[...], precision=lax.Precision.HIGHEST,
                             preferred_element_type=F32) + b_ref[...]

    nblk = ADA_COLS // ADA_BLK
    return pl.pallas_call(
        body, name="ada_mod",
        grid_spec=pltpu.PrefetchScalarGridSpec(
            num_scalar_prefetch=1, grid=(nblk,),
            in_specs=[pl.BlockSpec((8, D), lambda j, k: (0, 0)),
                      pl.BlockSpec((D, ADA_BLK), lambda j, k: (0, j)),
                      pl.BlockSpec((1, ADA_BLK), lambda j, k: (0, k[0] * nblk + j))],
            out_specs=pl.BlockSpec((None, 8, ADA_BLK), lambda j, k: (k[1], 0, j))),
        out_shape=jax.ShapeDtypeStruct((8, 8, ADA_COLS), F32),
        compiler_params=_params("parallel"),
    )(kme, c_all, ada_w, ada_b)


def _ada_grad(c_all_t, dmod_all, kidx):
    def body(k_ref, ct_ref, dm_ref, o_ref):
        cv = ct_ref[...]
        cs = cv * _sigmoid(cv)
        acc = cs[:, 0:1] * dm_ref[0:1, :]
        for b in range(1, 8):
            acc = acc + cs[:, b:b + 1] * dm_ref[b:b + 1, :]
        o_ref[...] = acc

    nblk = ADA_COLS // ADA_BLK
    return pl.pallas_call(
        body, name="ada_grad",
        grid_spec=pltpu.PrefetchScalarGridSpec(
            num_scalar_prefetch=1, grid=(nblk,),
            in_specs=[pl.BlockSpec((D, 8), lambda j, k: (0, 0)),
                      pl.BlockSpec((8, ADA_BLK), lambda j, k: (0, k[0] * nblk + j))],
            out_specs=pl.BlockSpec((D, ADA_BLK), lambda j, k: (0, j))),
        out_shape=jax.ShapeDtypeStruct((D, ADA_COLS), F32),
        compiler_params=_params("parallel"),
    )(kidx, c_all_t, dmod_all)


BIG = ("ffn1_w_in", "ffn1_w_out", "mix_w_in", "hgrn_w_o", "conv_w_o", "mix_w_out", "ffn2_w_in", "ffn2_w_out")
ROW_SHARDED = ("ffn1_w_out", "hgrn_w_o", "conv_w_o", "mix_w_out", "ffn2_w_out")
GATHER_GROUPS = ((0, 1), (2,), (3, 4, 5), (6, 7))
GATHER_HALVED = (True, True, False, False)
GATHER_STARTS = ((0, 1), (2, 3))
PACK_LEN = {"ada_b": 9, "hgrn_lb": 2}
WEIGHTS = ("ada_w", "ada_b", "norm_ffn1", "ffn1_w_in", "ffn1_w_out", "norm_mix", "mix_w_in", "hgrn_lb", "hgrn_g",
           "hgrn_w_o", "conv_w", "conv_b", "conv_ln_g", "conv_ln_b", "conv_w_o", "mix_w_out", "norm_ffn2",
           "ffn2_w_in", "ffn2_w_out", "norm_final")
PACKED = ("ada_b", "norm_ffn1", "norm_mix", "hgrn_g", "conv_b", "conv_ln_g", "conv_ln_b", "norm_ffn2",
          "norm_final", "hgrn_lb")


def _step(w, m, v, x, c, tgt):
    xi, yi, ci = _coords()
    kidx = (2 * xi + yi).astype(jnp.int32).reshape(1)
    kc = jnp.stack([2 * xi + yi, ci]).astype(jnp.int32)
    me = 4 * xi + 2 * yi + ci

    cq = D // NCHIP
    me32 = me.astype(jnp.int32)
    first = jnp.zeros((40, cq), F32).at[0:CONV_K].set(w["conv_w"][0]).at[32:36].set(c.reshape(NCHIP, cq))
    first = lax.dynamic_update_slice(jnp.zeros((8, 40, cq), F32), first[None], (me32, 0, 0))
    send, recv, first, tok = _allgather8_start(first, "gather_c_conv_w_start")
    early = {0: _cast_into_slot(w[BIG[0]][0], kc, "cast_" + BIG[0], tok)}
    first_all = _allgather8_wait(first, send, recv, early[0], "gather_c_conv_w_wait")
    c_all = first_all[:, 32:36, :].reshape(8, D)
    mod_cols = _ada_mod(c_all, w["ada_w"][0], w["ada_b"], jnp.stack([kidx[0], me32]))
    send, recv, mod_cols, tok = _allgather8_start(mod_cols, "gather_mod_start")
    early.update({t: _cast_into_slot(w[BIG[t]][0], kc, "cast_" + BIG[t], tok) for t in (1, 2)})
    mod_all = _allgather8_wait(mod_cols, send, recv, early[2], "gather_mod_wait")
    mod = lax.dynamic_slice(mod_all, (0, me, 0), (8, 1, ADA_COLS))[::2].reshape(9, D)
    small = {n: w[n].reshape(-1, D) for n in ("norm_ffn1", "norm_mix", "hgrn_lb", "hgrn_g", "conv_b", "conv_ln_g",
                                              "conv_ln_b", "norm_ffn2", "norm_final")}
    small["conv_w"] = jnp.concatenate([first_all[2 * j, 0:32, :] for j in range(NCHIP)], axis=1)

    lands, sends, recvs = [], [], []
    after = mod
    for part in GATHER_STARTS:
        tensors = [t for gi in part for t in GATHER_GROUPS[gi]]
        cast = [early[t] if t in early else _cast_into_slot(w[BIG[t]][0], kc, "cast_" + BIG[t], after)
                for t in tensors]
        groups = [tuple(tensors.index(t) for t in GATHER_GROUPS[gi]) for gi in part]
        s, r, thru, after = _gather_start(cast, groups, [GATHER_HALVED[gi] for gi in part], [after],
                                          "gather_weights_start%d" % part[0])
        lands, sends, recvs = lands + list(thru), sends + list(s), recvs + list(r)
    started_all = after
    ready = {}

    def weight(name, after):
        t = BIG.index(name)
        if t not in ready:
            gi = [t in grp for grp in GATHER_GROUPS].index(True)
            grp = GATHER_GROUPS[gi]
            outs = _gather_wait([lands[j] for j in grp], GATHER_HALVED[gi], sends[gi], recvs[gi],
                                started_all if gi == 0 else after, "gather_weights_wait%d" % gi)
            if GATHER_HALVED[gi]:
                outs = _sibling_fill(outs, "gather_weights_fill%d" % gi)
            ready.update(zip(grp, outs))
        return ready[t].reshape(-1, D) if name in ROW_SHARDED else ready[t]

    grads, delta, new_m, new_v = {}, {}, {}, {}
    flight = []
    landed = []

    def settle(after):
        names, sa, rb, send, recv = flight.pop()
        sa, rb = _scatter_wait(sa, rb, send, recv, after, "rs_chip_wait_" + names[0])
        landed.append((names, sa, rb))

    def reduce(names, pairs, after=None):
        gs = [g.reshape(NCHIP, -1, g.shape[-1]) for g, _ in pairs]
        ra = [r.reshape(NCHIP, -1, r.shape[-1]) for _, r in pairs]
        sa = [_sum_own_half(g, r, kc, "rs_sum_pair_" + n) for g, r, n in zip(gs, ra, names)]
        if flight:
            settle(sa[0])
        send, recv, sa, rb, tok = _scatter_start(sa, "rs_chip_start_" + names[0], () if after is None else (after,))
        flight.append((names, sa, rb, send, recv))
        started.append(tok)
        return tok

    def adamw(n, after=None):
        shape = w[n].shape
        two = (shape[-2], shape[-1])
        out = _adamw(w[n].reshape(two), grads[n], m[n].reshape(two), v[n].reshape(two), "adamw_" + n, after,
                     copy_grad=n in BIG)
        g_ = out[3] if n in BIG else grads[n]
        grads[n], delta[n], new_m[n], new_v[n] = (a.reshape(shape) for a in (g_, out[0], out[1], out[2]))
        return out[1]

    def finish(after=None):
        names, sa, rb = landed.pop(0)
        full = [_sum_chips(s, r, kc, "rs_sum_chips_" + n, after) for s, r, n in zip(sa, rb, names)]
        grads.update(zip(names, full))
        return [adamw(n) for n in names][-1]

    started = []

    smalls = []

    def reduce_small(parts):
        blocks = _pack_rows(parts, PACK_ROWS, "pack_small_grads", slot=me.astype(jnp.int32).reshape(1))
        send, recv, blocks, tok = _allgather8_start(blocks, "gather_small_grads_start")

        def finish(after):
            packed_all = _allgather8_wait(blocks, send, recv, after, "gather_small_grads_wait")
            smalls.extend([packed_all, _sum8(packed_all, "sum_small_grads")])
            return smalls[1]

        return finish, tok

    dx = _local_step(x[0], tgt[0], mod, small, kc, weight, reduce, reduce_small)
    packed_all, gsum = smalls
    loss = (0.5 / D) * jnp.sum(gsum[PACK_AT["loss"]])
    dmod_all = packed_all[:, 0:9, :].reshape(8, 9 * D)
    grads["ada_w"] = _ada_grad(c_all.T, dmod_all, kidx)
    grads["conv_w"] = lax.dynamic_slice(gsum, (PACK_AT["conv_w"], kidx[0] * (D // NCHIP)), (CONV_K, D // NCHIP))

    tok = started[-1]
    adamw("ada_w", tok)
    adamw("conv_w")
    two = lambda a, n: a.reshape(PACK_LEN.get(n, 1), D)
    small_out = _adamw_rows([(two(w[n], n), two(m[n], n), two(v[n], n)) for n in PACKED], gsum,
                            [PACK_AT[n] for n in PACKED], tok, "adamw_small")
    for n, quad in zip(PACKED, small_out):
        grads[n], delta[n], new_m[n], new_v[n] = (a.reshape(w[n].shape) for a in quad)
    last = small_out[-1][3]
    while landed:
        last = finish(tok)
    settle(last)
    finish()

    outs = [loss, dx[None]]
    for d in (grads, delta, new_m, new_v):
        outs += [d[n] for n in WEIGHTS]
    return tuple(outs)


def kernel(x, c, ada_w, ada_b, norm_ffn1, ffn1_w_in, ffn1_w_out, norm_mix, mix_w_in, hgrn_lb, hgrn_g, hgrn_w_o, conv_w, conv_b, conv_ln_g, conv_ln_b, conv_w_o, mix_w_out, norm_ffn2, ffn2_w_in, ffn2_w_out, norm_final, loss_target, m_ada_w, m_ada_b, m_norm_ffn1, m_ffn1_w_in, m_ffn1_w_out, m_norm_mix, m_mix_w_in, m_hgrn_lb, m_hgrn_g, m_hgrn_w_o, m_conv_w, m_conv_b, m_conv_ln_g, m_conv_ln_b, m_conv_w_o, m_mix_w_out, m_norm_ffn2, m_ffn2_w_in, m_ffn2_w_out, m_norm_final, v_ada_w, v_ada_b, v_norm_ffn1, v_ffn1_w_in, v_ffn1_w_out, v_norm_mix, v_mix_w_in, v_hgrn_lb, v_hgrn_g, v_hgrn_w_o, v_conv_w, v_conv_b, v_conv_ln_g, v_conv_ln_b, v_conv_w_o, v_mix_w_out, v_norm_ffn2, v_ffn2_w_in, v_ffn2_w_out, v_norm_final):
    w = dict(ada_w=ada_w, ada_b=ada_b, norm_ffn1=norm_ffn1, ffn1_w_in=ffn1_w_in, ffn1_w_out=ffn1_w_out,
             norm_mix=norm_mix, mix_w_in=mix_w_in, hgrn_lb=hgrn_lb, hgrn_g=hgrn_g, hgrn_w_o=hgrn_w_o, conv_w=conv_w,
             conv_b=conv_b, conv_ln_g=conv_ln_g, conv_ln_b=conv_ln_b, conv_w_o=conv_w_o, mix_w_out=mix_w_out,
             norm_ffn2=norm_ffn2, ffn2_w_in=ffn2_w_in, ffn2_w_out=ffn2_w_out, norm_final=norm_final)
    m = dict(ada_w=m_ada_w, ada_b=m_ada_b, norm_ffn1=m_norm_ffn1, ffn1_w_in=m_ffn1_w_in, ffn1_w_out=m_ffn1_w_out,
             norm_mix=m_norm_mix, mix_w_in=m_mix_w_in, hgrn_lb=m_hgrn_lb, hgrn_g=m_hgrn_g, hgrn_w_o=m_hgrn_w_o,
             conv_w=m_conv_w, conv_b=m_conv_b, conv_ln_g=m_conv_ln_g, conv_ln_b=m_conv_ln_b, conv_w_o=m_conv_w_o,
             mix_w_out=m_mix_w_out, norm_ffn2=m_norm_ffn2, ffn2_w_in=m_ffn2_w_in, ffn2_w_out=m_ffn2_w_out,
             norm_final=m_norm_final)
    v = dict(ada_w=v_ada_w, ada_b=v_ada_b, norm_ffn1=v_norm_ffn1, ffn1_w_in=v_ffn1_w_in, ffn1_w_out=v_ffn1_w_out,
             norm_mix=v_norm_mix, mix_w_in=v_mix_w_in, hgrn_lb=v_hgrn_lb, hgrn_g=v_hgrn_g, hgrn_w_o=v_hgrn_w_o,
             conv_w=v_conv_w, conv_b=v_conv_b, conv_ln_g=v_conv_ln_g, conv_ln_b=v_conv_ln_b, conv_w_o=v_conv_w_o,
             mix_w_out=v_mix_w_out, norm_ffn2=v_norm_ffn2, ffn2_w_in=v_ffn2_w_in, ffn2_w_out=v_ffn2_w_out,
             norm_final=v_norm_final)
    return _step(w, m, v, x, c, loss_target)
```

```python
import jax
import jax.numpy as jnp
from jax import lax
from jax.experimental import pallas as pl
from jax.experimental.pallas import tpu as pltpu

F32 = jnp.float32
BF16 = jnp.bfloat16

D = 1024
DFF = 2816
NCHIP = 4
FSH = 2 * DFF // NCHIP
HEADS = 8
DK = 128
CHUNK = 64
CONV_K = 31
HALO = 32
EPS = 1e-6
TB = 256
CB = 2048
DW_TOKENS = 2048
VMEM_LIMIT = 56 * 1024 * 1024

ADAM_LR = 0.001
ADAM_B1 = 0.9
ADAM_B2 = 0.999
ADAM_EPS = 1e-08
ADAM_WD = 0.01
ADAM_STEP = 10

MESH = pl.DeviceIdType.MESH
ANY = pl.BlockSpec(memory_space=pl.ANY)


def _params(*sem):
    return pltpu.CompilerParams(dimension_semantics=sem, vmem_limit_bytes=VMEM_LIMIT)


def _sigmoid(x):
    return 0.5 * jnp.tanh(0.5 * x) + 0.5


def _dsilu(x, sg):
    return sg * (1.0 + x * (1.0 - sg))


def _nt(a, b):
    return lax.dot_general(a, b, (((1,), (1,)), ((), ())), preferred_element_type=F32)


def _tn(a, b):
    return lax.dot_general(a, b, (((0,), (0,)), ((), ())), preferred_element_type=F32)


def _nn(a, b):
    return jnp.dot(a, b, preferred_element_type=F32)


def _colsum(x):
    return jnp.sum(x, axis=0, keepdims=True)


def _rms_fwd(x, gn, sc, sh):
    r = lax.rsqrt(jnp.mean(x * x, axis=-1, keepdims=True) + EPS)
    n = x * r
    h = (n * gn) * (1.0 + sc) + sh
    return r, n, h


def _rms_bwd(dh, r, n, gn, sc, acc_ref):
    acc_ref[0:1, :] += _colsum(dh)
    acc_ref[1:2, :] += _colsum(dh * (n * gn))
    dng = dh * (1.0 + sc)
    acc_ref[3:4, :] += _colsum(dng * n)
    dn = dng * gn
    return r * (dn - n * jnp.mean(dn * n, axis=-1, keepdims=True))


def _loss_head(x, tgt, gf, acc_ref):
    r = lax.rsqrt(jnp.mean(x * x, axis=-1, keepdims=True) + EPS)
    n = x * r
    err = n * gf - tgt
    acc_ref[1:2, :] += _colsum(err * err)
    dy = err * (1.0 / D)
    acc_ref[0:1, :] += _colsum(dy * n)
    dn = dy * gf
    return r * (dn - n * jnp.mean(dn * n, axis=-1, keepdims=True))


def _ffn_fwd(x, vec, w_in, w_out, name, head=None):
    T = x.shape[0]
    nh = 0 if head is None else 2

    def body(x_ref, vec_ref, *rest):
        win_hbm, wout_hbm = rest[nh:nh + 2]
        xo_ref, h_ref, a_ref, b_ref, s_ref, f_ref = rest[nh + 2:nh + 8]
        win, wout = rest[-2:]

        @pl.when(pl.program_id(0) == 0)
        def _():
            pltpu.sync_copy(win_hbm, win)
            pltpu.sync_copy(wout_hbm, wout)
            if head is not None:
                rest[nh + 8][...] = jnp.zeros((8, D), F32)

        x = x_ref[...]
        sh, sc, gate, gn = vec_ref[0:1, :], vec_ref[1:2, :], vec_ref[2:3, :], vec_ref[3:4, :]
        _, _, h = _rms_fwd(x, gn, sc, sh)
        hb = h.astype(BF16)
        h_ref[...] = hb
        f = jnp.zeros((TB, D), F32)
        for j in range(2):
            cols = slice(j * FSH, (j + 1) * FSH)
            a = _nn(hb, win[j])
            b = _nn(hb, win[2 + j])
            s = (a * _sigmoid(a) * b).astype(BF16)
            a_ref[:, cols] = a.astype(BF16)
            b_ref[:, cols] = b.astype(BF16)
            s_ref[:, cols] = s
            f = f + _nn(s, wout[cols, :])
        xo = x + (0.5 * gate) * f
        f_ref[...] = f.astype(BF16)
        if head is None:
            xo_ref[...] = xo
        else:
            xo_ref[...] = _loss_head(xo, rest[0][...], rest[1][0:1, :], rest[nh + 8])

    row = lambda w: pl.BlockSpec((TB, w), lambda i: (i, 0))
    vec8 = pl.BlockSpec((8, D), lambda i: (0, 0))
    acc = [] if head is None else [jax.ShapeDtypeStruct((8, D), F32)]
    return pl.pallas_call(
        body, name=name, grid=(T // TB,),
        in_specs=[row(D), vec8] + ([] if head is None else [row(D), vec8]) + [ANY, ANY],
        out_specs=[row(D), row(D), row(DFF), row(DFF), row(DFF), row(D)] + [vec8] * len(acc),
        out_shape=[jax.ShapeDtypeStruct((T, D), F32), jax.ShapeDtypeStruct((T, D), BF16),
                   jax.ShapeDtypeStruct((T, DFF), BF16), jax.ShapeDtypeStruct((T, DFF), BF16),
                   jax.ShapeDtypeStruct((T, DFF), BF16), jax.ShapeDtypeStruct((T, D), BF16)] + acc,
        scratch_shapes=[pltpu.VMEM((NCHIP, D, FSH), BF16), pltpu.VMEM((DFF, D), BF16)],
        compiler_params=_params("arbitrary"),
    )(x, vec, *([] if head is None else list(head)), w_in, w_out)


def _ffn_bwd(dxo, x, vec, a, b, f, w_in, w_out, name):
    T = x.shape[0]

    def body(dxo_ref, x_ref, vec_ref, a_ref, b_ref, f_ref, win_hbm, wout_hbm,
             dx_ref, df_ref, dab_ref, acc_ref, win, wout):
        @pl.when(pl.program_id(0) == 0)
        def _():
            pltpu.sync_copy(win_hbm, win)
            pltpu.sync_copy(wout_hbm, wout)
            acc_ref[...] = jnp.zeros_like(acc_ref)

        dxo = dxo_ref[...]
        x = x_ref[...]
        sh, sc, gate, gn = vec_ref[0:1, :], vec_ref[1:2, :], vec_ref[2:3, :], vec_ref[3:4, :]
        r, n, _ = _rms_fwd(x, gn, sc, sh)
        acc_ref[2:3, :] += _colsum(0.5 * f_ref[...].astype(F32) * dxo)
        dfb = ((0.5 * gate) * dxo).astype(BF16)
        df_ref[...] = dfb
        dh = jnp.zeros((TB, D), F32)
        for j in range(2):
            cols = slice(j * FSH, (j + 1) * FSH)
            ds = _nt(dfb, wout[cols, :])
            av = a_ref[:, cols].astype(F32)
            bv = b_ref[:, cols].astype(F32)
            sg = _sigmoid(av)
            da = (ds * bv * _dsilu(av, sg)).astype(BF16)
            db = (ds * (av * sg)).astype(BF16)
            dab_ref[j] = da
            dab_ref[2 + j] = db
            dh = dh + _nt(da, win[j]) + _nt(db, win[2 + j])
        dx_ref[...] = dxo + _rms_bwd(dh, r, n, gn, sc, acc_ref)

    row = lambda w: pl.BlockSpec((TB, w), lambda i: (i, 0))
    vec8 = pl.BlockSpec((8, D), lambda i: (0, 0))
    return pl.pallas_call(
        body, name=name, grid=(T // TB,),
        in_specs=[row(D), row(D), vec8, row(DFF), row(DFF), row(D), ANY, ANY],
        out_specs=[row(D), pl.BlockSpec((None, TB, D), lambda i: (0, i, 0)),
                   pl.BlockSpec((NCHIP, TB, FSH), lambda i: (0, i, 0)), vec8],
        out_shape=[jax.ShapeDtypeStruct((T, D), F32), jax.ShapeDtypeStruct((1, T, D), BF16),
                   jax.ShapeDtypeStruct((NCHIP, T, FSH), BF16), jax.ShapeDtypeStruct((8, D), F32)],
        scratch_shapes=[pltpu.VMEM((NCHIP, D, FSH), BF16), pltpu.VMEM((DFF, D), BF16)],
        compiler_params=_params("arbitrary"),
    )(dxo, x, vec, a, b, f, w_in, w_out)


def _mm_tn(a, b3, hp, kc, shard_rows, name, into=None, slab=0, slabs=None, after=None):
    T, M = a.shape
    P, _, N = b3.shape
    tm = M if M <= 1408 else M // 2
    tk = min(T, DW_TOKENS if P * (M // tm) > 1 else DW_TOKENS // 2)
    nk = T // tk
    ni = M // tm
    slabs = P // hp if slabs is None else slabs
    half = shard_rows // 2
    extra = ([] if into is None else list(into)) + ([] if after is None else [after])

    def body(kc_ref, a_ref, b_ref, *rest):
        o_ref, ra_ref, hbuf, send_sems, recv_sem = rest[-5:]
        p, i, k = pl.program_id(0), pl.program_id(1), pl.program_id(2)
        x, y, c = _coords()
        step = p * ni + i
        slot = step % 2

        def send(p_, i_, slot_):
            dst = ra_ref.at[slab + p_ // hp, pl.ds(pl.multiple_of(i_ * (tm // 2), 8), tm // 2),
                            pl.ds(pl.multiple_of((p_ % hp) * N, LANES), N)]
            return pltpu.make_async_remote_copy(
                src_ref=hbuf.at[slot_], dst_ref=dst, send_sem=send_sems.at[slot_], recv_sem=recv_sem,
                device_id=(x, y, 1 - c), device_id_type=MESH)

        @pl.when(k == 0)
        def _():
            o_ref[...] = jnp.zeros_like(o_ref)

        o_ref[...] += _tn(a_ref[...], b_ref[...])

        @pl.when(k == nk - 1)
        def _():
            @pl.when(step >= 2)
            def _():
                send(p, i, slot).wait_send()

            for j in range(tm // shard_rows):
                start = pl.multiple_of(j * shard_rows + (1 - kc_ref[1]) * half, 8)
                hbuf[slot, j * half:(j + 1) * half, :] = o_ref[pl.ds(start, half), :].astype(BF16)
            send(p, i, slot).start()

        @pl.when((step == P * ni - 1) & (k == nk - 1))
        def _():
            for s in range(min(2, P * ni)):
                send(p, i, (step - s) % 2).wait_send()
            mine = ra_ref.at[slab:slab + P // hp]
            pltpu.make_async_remote_copy(src_ref=mine, dst_ref=mine, send_sem=send_sems.at[0], recv_sem=recv_sem,
                                         device_id=(x, y, 1 - c), device_id_type=MESH).wait_recv()

    return pl.pallas_call(
        body, name=name,
        grid_spec=pltpu.PrefetchScalarGridSpec(
            num_scalar_prefetch=1, grid=(P, ni, nk),
            in_specs=[pl.BlockSpec((tk, tm), lambda p, i, k, kc: (k, i)),
                      pl.BlockSpec((None, tk, N), lambda p, i, k, kc: (p, k, 0))] + [ANY] * len(extra),
            out_specs=[pl.BlockSpec((None, tm, N), lambda p, i, k, kc: (slab + p // hp, i, p % hp)), ANY],
            scratch_shapes=[pltpu.VMEM((2, tm // 2, N), BF16), pltpu.SemaphoreType.DMA((2,)),
                            pltpu.SemaphoreType.DMA]),
        out_shape=[jax.ShapeDtypeStruct((slabs, M, hp * N), F32), jax.ShapeDtypeStruct((slabs, M // 2, hp * N), BF16)],
        input_output_aliases={} if into is None else {3: 0, 4: 1},
        compiler_params=_params("arbitrary", "arbitrary", "arbitrary"),
    )(kc, a, b3, *extra)


def _mix_proj_fwd(x, vec, w_in):
    T = x.shape[0]

    def body(x_ref, vec_ref, w_hbm, h_ref, qr_ref, g_ref, k_ref, v_ref, og_ref, u_ref, ua_ref, ub_ref,
             sa_ref, sb_ref, w):
        @pl.when(pl.program_id(0) == 0)
        def _():
            pltpu.sync_copy(w_hbm, w)

        x = x_ref[...]
        sh, sc, gn, lb = vec_ref[0:1, :], vec_ref[1:2, :], vec_ref[3:4, :], vec_ref[4:5, :]
        _, _, h = _rms_fwd(x, gn, sc, sh)
        hb = h.astype(BF16)
        h_ref[...] = hb
        p = _nn(hb, w[0])
        qr_ref[...] = p[:, :D].astype(BF16)
        fg = lb + (1.0 - lb) * _sigmoid(p[:, D:])
        g_ref[...] = jnp.log(fg)
        k_ref[...] = (1.0 - fg).astype(BF16)
        p = _nn(hb, w[1])
        v_ref[...] = p[:, :D].astype(BF16)
        og_ref[...] = p[:, D:].astype(BF16)
        p = _nn(hb, w[2])
        ua, ub = p[:, :D], p[:, D:]
        u_ref[...] = ua * _sigmoid(ub)
        ua_ref[...] = ua.astype(BF16)
        ub_ref[...] = ub.astype(BF16)
        p = _nn(hb, w[3])
        sa_ref[...] = _sigmoid(p[:, :D]).astype(BF16)
        sb_ref[...] = _sigmoid(p[:, D:]).astype(BF16)

    row = pl.BlockSpec((TB, D), lambda i: (i, 0))
    bf = jax.ShapeDtypeStruct((T, D), BF16)
    f32 = jax.ShapeDtypeStruct((T, D), F32)
    return pl.pallas_call(
        body, name="mix_proj_fwd", grid=(T // TB,),
        in_specs=[row, pl.BlockSpec((8, D), lambda i: (0, 0)), ANY],
        out_specs=[row] * 11,
        out_shape=[bf, bf, f32, bf, bf, bf, f32, bf, bf, bf, bf],
        scratch_shapes=[pltpu.VMEM((NCHIP, D, 2 * D), BF16)],
        compiler_params=_params("arbitrary"),
    )(x, vec, w_in)


def _mix_proj_bwd(dxo, x, vec, dpa, dpb, dpc, w_in):
    T = x.shape[0]

    def body(dxo_ref, x_ref, vec_ref, dpa_ref, dpb_ref, dpc_ref, w_hbm, dx_ref, acc_ref, w):
        @pl.when(pl.program_id(0) == 0)
        def _():
            pltpu.sync_copy(w_hbm, w)
            acc_ref[...] = jnp.zeros_like(acc_ref)

        x = x_ref[...]
        sh, sc, gn = vec_ref[0:1, :], vec_ref[1:2, :], vec_ref[3:4, :]
        r, n, _ = _rms_fwd(x, gn, sc, sh)
        dh = jnp.zeros((TB, D), F32)
        for p in range(8):
            src = dpa_ref[p] if p < 4 else (dpb_ref[p - 4] if p < 6 else dpc_ref[p - 6])
            dh = dh + _nt(src, w[p // 2, :, (p % 2) * D:(p % 2 + 1) * D])
        dx_ref[...] = dxo_ref[...] + _rms_bwd(dh, r, n, gn, sc, acc_ref)

    row = pl.BlockSpec((TB, D), lambda i: (i, 0))
    vec8 = pl.BlockSpec((8, D), lambda i: (0, 0))
    stack = lambda k: pl.BlockSpec((k, TB, D), lambda i: (0, i, 0))
    return pl.pallas_call(
        body, name="mix_proj_bwd", grid=(T // TB,),
        in_specs=[row, row, vec8, stack(4), stack(2), stack(2), ANY],
        out_specs=[row, vec8],
        out_shape=[jax.ShapeDtypeStruct((T, D), F32), jax.ShapeDtypeStruct((8, D), F32)],
        scratch_shapes=[pltpu.VMEM((NCHIP, D, 2 * D), BF16)],
        compiler_params=_params("arbitrary"),
    )(dxo, x, vec, dpa, dpb, dpc, w_in)


def _tri(lower):
    r = lax.broadcasted_iota(jnp.int32, (CHUNK, CHUNK), 0)
    c = lax.broadcasted_iota(jnp.int32, (CHUNK, CHUNK), 1)
    return (c <= r) if lower else (c >= r)


def _cumsum_rows(mask, g):
    hi = g.astype(BF16)
    rest = g - hi.astype(F32)
    mid = rest.astype(BF16)
    low = (rest - mid.astype(F32)).astype(BF16)
    n = g.shape[1]
    p = _nn(mask.astype(BF16), jnp.concatenate([hi, mid, low], axis=1))
    return (p[:, 2 * n:] + p[:, n:2 * n]) + p[:, :n]


def _chunk_decay(low, g, nck):
    bs, mids, lasts = [], [], []
    for c in range(nck):
        gc = g[c * CHUNK:(c + 1) * CHUNK]
        bs.append(_cumsum_rows(low, gc))
        mids.append(_colsum(gc[0:CHUNK // 2]))
        lasts.append(_colsum(gc))
    spread = lambda rows: jnp.concatenate([jnp.broadcast_to(r, (CHUNK, DK)) for r in rows], axis=0)
    return jnp.concatenate(bs, axis=0), spread(mids), spread(lasts), lasts


def _hgrn_fwd(qr, g, k, v, og, vec):
    T = qr.shape[0]
    nck = CB // CHUNK

    def body(qr_ref, g_ref, k_ref, v_ref, og_ref, vec_ref, out_ref, o_ref, st_ref, state):
        @pl.when(pl.program_id(1) == 0)
        def _():
            state[...] = jnp.zeros_like(state)

        low = _tri(True)
        qv = qr_ref[...].astype(F32)
        q = qv * _sigmoid(qv) * (DK ** -0.5)
        kk = k_ref[...].astype(F32)
        vb = v_ref[...]
        b, mid, last, lasts = _chunk_decay(low, g_ref[...], nck)
        qt = (q * jnp.exp(b - mid)).astype(BF16)
        kt = (kk * jnp.exp(mid - b)).astype(BF16)
        qe = (q * jnp.exp(b)).astype(BF16)
        kd = (kk * jnp.exp(last - b)).astype(BF16)
        intra, grow = [], []
        for c in range(nck):
            r = slice(c * CHUNK, (c + 1) * CHUNK)
            att = jnp.where(low, _nt(qt[r], kt[r]), 0.0).astype(BF16)
            intra.append(_nn(att, vb[r]))
            grow.append(_tn(vb[r], kd[r]))
        st = state[...]
        inter = []
        for c in range(nck):
            stb = st.astype(BF16)
            st_ref[c] = stb
            inter.append(_nt(qe[c * CHUNK:(c + 1) * CHUNK], stb))
            st = st * jnp.exp(lasts[c]) + grow[c]
        state[...] = st
        o = jnp.concatenate(intra, axis=0) + jnp.concatenate(inter, axis=0)
        o_ref[...] = o
        ogv = og_ref[...].astype(F32)
        rms = lax.rsqrt(jnp.mean(o * o, axis=-1, keepdims=True) + EPS)
        out_ref[...] = (o * rms * vec_ref[5:6, :] * (ogv * _sigmoid(ogv))).astype(BF16)

    blk = pl.BlockSpec((CB, DK), lambda h, i: (i, h))
    return pl.pallas_call(
        body, name="hgrn_fwd", grid=(HEADS, T // CB),
        in_specs=[blk, blk, blk, blk, blk, pl.BlockSpec((8, DK), lambda h, i: (0, h))],
        out_specs=[blk, blk, pl.BlockSpec((None, nck, DK, DK), lambda h, i: (h, i, 0, 0))],
        out_shape=[jax.ShapeDtypeStruct((T, D), BF16), jax.ShapeDtypeStruct((T, D), F32),
                   jax.ShapeDtypeStruct((HEADS, T // CHUNK, DK, DK), BF16)],
        scratch_shapes=[pltpu.VMEM((DK, DK), F32)],
        compiler_params=_params("parallel", "arbitrary"),
    )(qr, g, k, v, og, vec)


def _hgrn_bwd(dout, og, qr, g, k, v, o, st, vec):
    T = qr.shape[0]
    nck = CB // CHUNK
    nb = T // CB

    def body(dout_ref, og_ref, qr_ref, g_ref, k_ref, v_ref, o_ref, st_ref, vec_ref,
             dp_ref, acc_ref, dstate):
        @pl.when(pl.program_id(1) == 0)
        def _():
            dstate[...] = jnp.zeros_like(dstate)
            acc_ref[...] = jnp.zeros_like(acc_ref)

        o = o_ref[...]
        ogv = og_ref[...].astype(F32)
        dout = dout_ref[...].astype(F32)
        hg = vec_ref[5:6, :]
        sgo = _sigmoid(ogv)
        rms = lax.rsqrt(jnp.mean(o * o, axis=-1, keepdims=True) + EPS)
        ohat = o * rms
        dp_ref[3] = (dout * (ohat * hg) * _dsilu(ogv, sgo)).astype(BF16)
        don = dout * (ogv * sgo)
        acc_ref[0:1, :] += _colsum(don * ohat)
        dohat = don * hg
        dob = (rms * (dohat - ohat * jnp.mean(dohat * ohat, axis=-1, keepdims=True))).astype(BF16)

        low = _tri(True)
        upp = _tri(False)
        lb = vec_ref[4:5, :]
        qv = qr_ref[...].astype(F32)
        sgq = _sigmoid(qv)
        q = qv * sgq * (DK ** -0.5)
        kk = k_ref[...].astype(F32)
        vb = v_ref[...]
        gv = g_ref[...]
        b, mid, last, lasts = _chunk_decay(low, gv, nck)
        eq = jnp.exp(b - mid)
        ek = jnp.exp(mid - b)
        eb = jnp.exp(b)
        ed = jnp.exp(last - b)
        qtb, ktb, qeb, kdb = ((t).astype(BF16) for t in (q * eq, kk * ek, q * eb, kk * ed))
        rows = [slice(c * CHUNK, (c + 1) * CHUNK) for c in range(nck)]

        dv1, dqt, dkt, dqe, grow = [], [], [], [], []
        for c, r in enumerate(rows):
            att = jnp.where(low, _nt(qtb[r], ktb[r]), 0.0).astype(BF16)
            datt = jnp.where(low, _nt(dob[r], vb[r]), 0.0).astype(BF16)
            dv1.append(_tn(att, dob[r]))
            dqt.append(_nn(datt, ktb[r]))
            dkt.append(_tn(datt, qtb[r]))
            dqe.append(_nn(dob[r], st_ref[c]))
            grow.append(_tn(dob[r], qeb[r]))
        ds = dstate[...]
        ds1b, dl_state = [None] * nck, [None] * nck
        for c in reversed(range(nck)):
            el = jnp.exp(lasts[c])
            ds1b[c] = ds.astype(BF16)
            dl_state[c] = el * _colsum(ds * st_ref[c].astype(F32))
            ds = ds * el + grow[c]
        dstate[...] = ds
        dkd = jnp.concatenate([_nn(vb[r], ds1b[c]) for c, r in enumerate(rows)], axis=0)
        dv = jnp.concatenate(dv1, axis=0) + jnp.concatenate([_nt(kdb[r], ds1b[c]) for c, r in enumerate(rows)], axis=0)
        dqt, dkt, dqe = (jnp.concatenate(t, axis=0) for t in (dqt, dkt, dqe))
        dq = dqt * eq + dqe * eb
        dk = dkt * ek + dkd * ed
        dkdkd = dkd * kdb.astype(F32)
        db = dqt * qtb.astype(F32) - dkt * ktb.astype(F32) + dqe * qeb.astype(F32) - dkdkd
        dg = jnp.concatenate([_cumsum_rows(upp, db[r]) + (_colsum(dkdkd[r]) + dl_state[c])
                              for c, r in enumerate(rows)], axis=0)
        fg = jnp.exp(gv)
        dfg = dg * jnp.exp(-gv) - dk
        one_m_sig = (1.0 - fg) * (1.0 / (1.0 - lb))
        dp_ref[0] = (dq * (DK ** -0.5) * _dsilu(qv, sgq)).astype(BF16)
        dp_ref[1] = (dfg * (fg - lb) * one_m_sig).astype(BF16)
        dp_ref[2] = dv.astype(BF16)
        dlb = _colsum(dfg * one_m_sig) * (lb * (1.0 - lb))
        acc_ref[1:2, :] += dlb
        acc_ref[2:3, :] -= dlb

    blk = pl.BlockSpec((CB, DK), lambda h, i: (nb - 1 - i, h))
    return pl.pallas_call(
        body, name="hgrn_bwd", grid=(HEADS, nb),
        in_specs=[blk, blk, blk, blk, blk, blk, blk,
                  pl.BlockSpec((None, nck, DK, DK), lambda h, i: (h, nb - 1 - i, 0, 0)),
                  pl.BlockSpec((8, DK), lambda h, i: (0, h))],
        out_specs=[pl.BlockSpec((4, CB, DK), lambda h, i: (0, nb - 1 - i, h)),
                   pl.BlockSpec((8, DK), lambda h, i: (0, h))],
        out_shape=[jax.ShapeDtypeStruct((4, T, D), BF16), jax.ShapeDtypeStruct((8, D), F32)],
        scratch_shapes=[pltpu.VMEM((DK, DK), F32)],
        compiler_params=_params("parallel", "arbitrary"),
    )(dout, og, qr, g, k, v, o, st, vec)


def _ln_fwd(uc, lg, lbias):
    mu = jnp.mean(uc, axis=-1, keepdims=True)
    xc = uc - mu
    rstd = lax.rsqrt(jnp.mean(xc * xc, axis=-1, keepdims=True) + EPS)
    z = xc * rstd
    return rstd, z, z * lg + lbias


LANES = 128
SUBLANES = 8
CONV_ROWS = 64


def _lane_tiles():
    return [slice(l * LANES, (l + 1) * LANES) for l in range(D // LANES)]


def _row_shifts(x):
    n = x.shape[0]
    return [x] + [pltpu.roll(x, n - r, axis=0) for r in range(1, SUBLANES)]


TAPS_PAST = tuple(HALO - (CONV_K - 1) + j for j in range(CONV_K))
TAPS_AHEAD = tuple(CONV_K - 1 - j for j in range(CONV_K))


def _tap_windows(shifted, starts, r0, rows):
    for r in range(SUBLANES):
        taps = [(j, s // SUBLANES) for j, s in enumerate(starts) if s % SUBLANES == r]
        if not taps:
            continue
        lo = min(a for _, a in taps)
        hi = max(a for _, a in taps)
        span = shifted[r][r0 + lo * SUBLANES:r0 + hi * SUBLANES + rows]
        for j, a in taps:
            yield j, span[(a - lo) * SUBLANES:(a - lo) * SUBLANES + rows]


def _conv_fwd(u, cw, cvec):
    T = u.shape[0]
    per = TB // HALO

    def body(u_ref, halo_ref, cw_ref, cvec_ref, us_ref, uc_ref, pad):
        i = pl.program_id(0)
        pad[0:HALO, :] = jnp.where(i > 0, halo_ref[...], 0.0)
        pad[HALO:, :] = u_ref[...]
        for lanes in _lane_tiles():
            shifted = _row_shifts(pad[:, lanes])
            taps = cw_ref[:, lanes]
            for r0 in range(0, TB, CONV_ROWS):
                acc = jnp.broadcast_to(cvec_ref[0:1, lanes], (CONV_ROWS, LANES))
                for j, window in _tap_windows(shifted, TAPS_PAST, r0, CONV_ROWS):
                    acc = acc + taps[j:j + 1] * window
                uc_ref[r0:r0 + CONV_ROWS, lanes] = acc
        _, _, ul = _ln_fwd(uc_ref[...], cvec_ref[1:2, :], cvec_ref[2:3, :])
        us_ref[...] = (ul * _sigmoid(ul)).astype(BF16)

    row = pl.BlockSpec((TB, D), lambda i: (i, 0))
    return pl.pallas_call(
        body, name="conv_fwd", grid=(T // TB,),
        in_specs=[row, pl.BlockSpec((HALO, D), lambda i: (jnp.maximum(i * per - 1, 0), 0)),
                  pl.BlockSpec((32, D), lambda i: (0, 0)), pl.BlockSpec((8, D), lambda i: (0, 0))],
        out_specs=[row, row],
        out_shape=[jax.ShapeDtypeStruct((T, D), BF16), jax.ShapeDtypeStruct((T, D), F32)],
        scratch_shapes=[pltpu.VMEM((TB + HALO, D), F32)],
        compiler_params=_params("parallel"),
    )(u, u, cw, cvec)


def _conv_bwd_taps(duc, u, ua, ub, cw):
    T = u.shape[0]
    per = TB // HALO
    nblk = T // TB

    def body(duc_ref, dnext_ref, u_ref, uprev_ref, ua_ref, ub_ref, cw_ref, dp_ref, dcw_ref, upad, dpad, dcw):
        i = pl.program_id(0)

        @pl.when(i == 0)
        def _():
            dcw[...] = jnp.zeros_like(dcw)

        upad[0:HALO, :] = jnp.where(i > 0, uprev_ref[...], 0.0)
        upad[HALO:, :] = u_ref[...]
        dpad[0:TB, :] = duc_ref[...]
        dpad[TB:, :] = jnp.where(i < nblk - 1, dnext_ref[...], 0.0)
        for lanes in _lane_tiles():
            ushift = _row_shifts(upad[:, lanes])
            dshift = _row_shifts(dpad[:, lanes])
            for r0 in range(0, TB, CONV_ROWS):
                rows = slice(r0, r0 + CONV_ROWS)
                duc = duc_ref[rows, lanes]
                for j, window in _tap_windows(ushift, TAPS_PAST, r0, CONV_ROWS):
                    prod = duc * window
                    dcw[j, :, lanes] += jnp.sum(prod.reshape(CONV_ROWS // SUBLANES, SUBLANES, LANES), axis=0)
                du = jnp.zeros((CONV_ROWS, LANES), F32)
                for j, window in _tap_windows(dshift, TAPS_AHEAD, r0, CONV_ROWS):
                    du = du + cw_ref[j:j + 1, lanes] * window
                ua = ua_ref[rows, lanes].astype(F32)
                sg = _sigmoid(ub_ref[rows, lanes].astype(F32))
                dp_ref[0, rows, lanes] = (du * sg).astype(BF16)
                dp_ref[1, rows, lanes] = (du * ua * sg * (1.0 - sg)).astype(BF16)

        @pl.when(i == nblk - 1)
        def _():
            dcw_ref[...] = jnp.sum(dcw[...], axis=1)

    row = pl.BlockSpec((TB, D), lambda i: (i, 0))
    return pl.pallas_call(
        body, name="conv_bwd_taps", grid=(nblk,),
        in_specs=[row, pl.BlockSpec((HALO, D), lambda i: (jnp.minimum((i + 1) * per, T // HALO - 1), 0)),
                  row, pl.BlockSpec((HALO, D), lambda i: (jnp.maximum(i * per - 1, 0), 0)),
                  row, row, pl.BlockSpec((32, D), lambda i: (0, 0))],
        out_specs=[pl.BlockSpec((2, TB, D), lambda i: (0, i, 0)), pl.BlockSpec((32, D), lambda i: (0, 0))],
        out_shape=[jax.ShapeDtypeStruct((2, T, D), BF16), jax.ShapeDtypeStruct((32, D), F32)],
        scratch_shapes=[pltpu.VMEM((TB + HALO, D), F32), pltpu.VMEM((TB + HALO, D), F32),
                        pltpu.VMEM((32, SUBLANES, D), F32)],
        compiler_params=_params("arbitrary"),
    )(duc, duc, u, u, ua, ub, cw)


def _merge_fwd(x, oa, us, sa, sb, vec, w_ho, w_co, w_mo):
    T = x.shape[0]

    def body(x_ref, oa_ref, us_ref, sa_ref, sb_ref, vec_ref, who_hbm, wco_hbm, wmo_hbm,
             xo_ref, ya_ref, yb_ref, mg_ref, mo_ref, who, wco, wmo):
        @pl.when(pl.program_id(0) == 0)
        def _():
            pltpu.sync_copy(who_hbm, who)
            pltpu.sync_copy(wco_hbm, wco)
            pltpu.sync_copy(wmo_hbm, wmo)

        ya = _nn(oa_ref[...], who[...])
        yb = _nn(us_ref[...], wco[...])
        mg = (sa_ref[...].astype(F32) * ya + sb_ref[...].astype(F32) * yb).astype(BF16)
        mo = _nn(mg, wmo[...])
        xo_ref[...] = x_ref[...] + vec_ref[2:3, :] * mo
        ya_ref[...] = ya.astype(BF16)
        yb_ref[...] = yb.astype(BF16)
        mg_ref[...] = mg
        mo_ref[...] = mo.astype(BF16)

    row = pl.BlockSpec((TB, D), lambda i: (i, 0))
    bf = jax.ShapeDtypeStruct((T, D), BF16)
    wv = pltpu.VMEM((D, D), BF16)
    return pl.pallas_call(
        body, name="merge_fwd", grid=(T // TB,),
        in_specs=[row, row, row, row, row, pl.BlockSpec((8, D), lambda i: (0, 0)), ANY, ANY, ANY],
        out_specs=[row] * 5,
        out_shape=[jax.ShapeDtypeStruct((T, D), F32), bf, bf, bf, bf],
        scratch_shapes=[wv, wv, wv],
        compiler_params=_params("arbitrary"),
    )(x, oa, us, sa, sb, vec, w_ho, w_co, w_mo)


def _merge_bwd(dxo, mo, ya, yb, sa, sb, uc, vec, cvec, w_ho, w_co, w_mo):
    T = dxo.shape[0]

    def body(dxo_ref, mo_ref, ya_ref, yb_ref, sa_ref, sb_ref, uc_ref, vec_ref, cvec_ref, who_hbm, wco_hbm, wmo_hbm,
             dmo_ref, dya_ref, dyb_ref, doa_ref, duc_ref, dp_ref, acc_ref, cacc_ref, who, wco, wmo):
        @pl.when(pl.program_id(0) == 0)
        def _():
            pltpu.sync_copy(who_hbm, who)
            pltpu.sync_copy(wco_hbm, wco)
            pltpu.sync_copy(wmo_hbm, wmo)
            acc_ref[...] = jnp.zeros_like(acc_ref)
            cacc_ref[...] = jnp.zeros_like(cacc_ref)

        dxo = dxo_ref[...]
        acc_ref[2:3, :] += _colsum(mo_ref[...].astype(F32) * dxo)
        dmo = (vec_ref[2:3, :] * dxo).astype(BF16)
        dmo_ref[...] = dmo
        dmg = _nt(dmo, wmo[...])
        sa = sa_ref[...].astype(F32)
        sb = sb_ref[...].astype(F32)
        dya = (sa * dmg).astype(BF16)
        dyb = (sb * dmg).astype(BF16)
        dya_ref[...] = dya
        dyb_ref[...] = dyb
        dp_ref[0] = (dmg * ya_ref[...].astype(F32) * sa * (1.0 - sa)).astype(BF16)
        dp_ref[1] = (dmg * yb_ref[...].astype(F32) * sb * (1.0 - sb)).astype(BF16)
        doa_ref[...] = _nt(dya, who[...]).astype(BF16)
        dus = _nt(dyb, wco[...])
        lg = cvec_ref[1:2, :]
        rstd, z, ul = _ln_fwd(uc_ref[...], lg, cvec_ref[2:3, :])
        dul = dus * _dsilu(ul, _sigmoid(ul))
        cacc_ref[1:2, :] += _colsum(dul * z)
        cacc_ref[2:3, :] += _colsum(dul)
        dz = dul * lg
        duc = rstd * (dz - jnp.mean(dz, axis=-1, keepdims=True) - z * jnp.mean(dz * z, axis=-1, keepdims=True))
        cacc_ref[0:1, :] += _colsum(duc)
        duc_ref[...] = duc

    row = pl.BlockSpec((TB, D), lambda i: (i, 0))
    one = pl.BlockSpec((None, TB, D), lambda i: (0, i, 0))
    vec8 = pl.BlockSpec((8, D), lambda i: (0, 0))
    bf = jax.ShapeDtypeStruct((T, D), BF16)
    bf1 = jax.ShapeDtypeStruct((1, T, D), BF16)
    acc = jax.ShapeDtypeStruct((8, D), F32)
    wv = pltpu.VMEM((D, D), BF16)
    return pl.pallas_call(
        body, name="merge_bwd", grid=(T // TB,),
        in_specs=[row, row, row, row, row, row, row, vec8, vec8, ANY, ANY, ANY],
        out_specs=[one, one, one, row, row, pl.BlockSpec((2, TB, D), lambda i: (0, i, 0)), vec8, vec8],
        out_shape=[bf1, bf1, bf1, bf, jax.ShapeDtypeStruct((T, D), F32), jax.ShapeDtypeStruct((2, T, D), BF16), acc, acc],
        scratch_shapes=[wv, wv, wv],
        compiler_params=_params("arbitrary"),
    )(dxo, mo, ya, yb, sa, sb, uc, vec, cvec, w_ho, w_co, w_mo)


def _pack_rows(parts, total, name, slot=None):
    def body(*refs):
        out = refs[-1]
        out[...] = jnp.zeros_like(out)
        for ref, (_, src, n, dst) in zip(refs[-1 - len(parts):-1], parts):
            out[dst:dst + n, :] = ref[src:src + n, :]

    arrs = [p[0] for p in parts]
    if slot is None:
        return pl.pallas_call(
            body, name=name, in_specs=[pl.BlockSpec(a.shape, lambda: (0, 0)) for a in arrs],
            out_specs=pl.BlockSpec((total, D), lambda: (0, 0)),
            out_shape=jax.ShapeDtypeStruct((total, D), F32),
        )(*arrs)
    return pl.pallas_call(
        body, name=name,
        grid_spec=pltpu.PrefetchScalarGridSpec(
            num_scalar_prefetch=1, grid=(1,),
            in_specs=[pl.BlockSpec(a.shape, lambda i, s: (0, 0)) for a in arrs],
            out_specs=pl.BlockSpec((None, total, D), lambda i, s: (s[0], 0, 0))),
        out_shape=jax.ShapeDtypeStruct((8, total, D), F32),
    )(slot, *arrs)


PACK_ROWS = 56
PACK_AT = {"ada_b": 0, "loss": 9, "norm_ffn1": 10, "norm_mix": 11, "hgrn_g": 12, "conv_b": 13, "conv_ln_g": 14,
           "conv_ln_b": 15, "norm_ffn2": 16, "norm_final": 17, "hgrn_lb": 18, "conv_w": 20}


def _local_step(x, tgt, mod, small, kc, weight, reduce, reduce_small):
    lb = jax.nn.sigmoid(small["hgrn_lb"][0:1] - small["hgrn_lb"][1:2])
    vec1 = _pack_rows([(mod, 0, 3, 0), (small["norm_ffn1"], 0, 1, 3)], 8, "pack_vec1")
    vec2 = _pack_rows([(mod, 3, 3, 0), (small["norm_mix"], 0, 1, 3), (lb, 0, 1, 4), (small["hgrn_g"], 0, 1, 5)],
                      8, "pack_vec2")
    vec3 = _pack_rows([(mod, 6, 3, 0), (small["norm_ffn2"], 0, 1, 3)], 8, "pack_vec3")
    cvec = _pack_rows([(small["conv_b"], 0, 1, 0), (small["conv_ln_g"], 0, 1, 1), (small["conv_ln_b"], 0, 1, 2)],
                      8, "pack_cvec")
    cw = small["conv_w"]
    gvec = _pack_rows([(small["norm_final"], 0, 1, 0)], 8, "pack_gvec")

    wg = {n: weight(n, vec1) for n in ("ffn1_w_in", "ffn1_w_out")}
    x1, h1, a1, b1, s1, f1 = _ffn_fwd(x, vec1, wg["ffn1_w_in"], wg["ffn1_w_out"], "ffn1_fwd")
    wg["mix_w_in"] = weight("mix_w_in", x1)
    h2, qr, g, k, v, og, u, ua, ub, sa, sb = _mix_proj_fwd(x1, vec2, wg["mix_w_in"])
    oa, o, st = _hgrn_fwd(qr, g, k, v, og, vec2)
    us, uc = _conv_fwd(u, cw, cvec)
    wg.update({n: weight(n, us) for n in ("hgrn_w_o", "conv_w_o", "mix_w_out")})
    x2, ya, yb, mg, mo = _merge_fwd(x1, oa, us, sa, sb, vec2, wg["hgrn_w_o"], wg["conv_w_o"], wg["mix_w_out"])
    wg.update({n: weight(n, x2) for n in ("ffn2_w_in", "ffn2_w_out")})
    dx3, h3, a3, b3, s3, f3, acc_head = _ffn_fwd(x2, vec3, wg["ffn2_w_in"], wg["ffn2_w_out"], "ffn2_fwd",
                                                 head=(tgt, gvec))

    dx2, df3, dab3, acc3 = _ffn_bwd(dx3, x2, vec3, a3, b3, f3, wg["ffn2_w_in"], wg["ffn2_w_out"], "ffn2_bwd")
    tok = reduce(("ffn2_w_out", "ffn2_w_in"), [_mm_tn(s3, df3, 1, kc, DFF // NCHIP, "ffn2_dwout"),
                                               _mm_tn(h3, dab3, 1, kc, D, "ffn2_dwin")])
    vec2b = vec2 + tok[0:1, 0:1]
    dmo, dya, dyb, doa, duc, dpc, acc_m, acc_c = _merge_bwd(dx2, mo, ya, yb, sa, sb, uc, vec2b, cvec,
                                                            wg["hgrn_w_o"], wg["conv_w_o"], wg["mix_w_out"])
    tok = reduce(("mix_w_out", "hgrn_w_o", "conv_w_o"),
                 [_mm_tn(mg, dmo, 1, kc, D // NCHIP, "mix_dwout"), _mm_tn(oa, dya, 1, kc, D // NCHIP, "hgrn_dwo"),
                  _mm_tn(us, dyb, 1, kc, D // NCHIP, "conv_dwo")])
    vec2c = vec2 + tok[0:1, 0:1]
    dpb, dcw = _conv_bwd_taps(duc, u, ua, ub, cw)
    dpa, acc_h = _hgrn_bwd(doa, og, qr, g, k, v, o, st, vec2c)
    dx1, acc2 = _mix_proj_bwd(dx2, x1, vec2c, dpa, dpb, dpc, wg["mix_w_in"])
    gmix = _mm_tn(h2, dpa, 2, kc, D, "mix_dwin_a", slabs=NCHIP)
    gmix = _mm_tn(h2, dpb, 2, kc, D, "mix_dwin_b", into=gmix, slab=2, slabs=NCHIP)
    gmix = _mm_tn(h2, dpc, 2, kc, D, "mix_dwin_c", into=gmix, slab=3, slabs=NCHIP)
    tok = reduce(("mix_w_in",), [gmix])
    vec1b = vec1 + tok[0:1, 0:1]
    dx0, df1, dab1, acc1 = _ffn_bwd(dx1, x, vec1b, a1, b1, f1, wg["ffn1_w_in"], wg["ffn1_w_out"], "ffn1_bwd")

    at = PACK_AT
    finish_small = reduce_small([
        (acc1, 0, 3, at["ada_b"]), (acc2, 0, 2, at["ada_b"] + 3), (acc_m, 2, 1, at["ada_b"] + 5),
        (acc3, 0, 3, at["ada_b"] + 6), (acc_head, 1, 1, at["loss"]), (acc1, 3, 1, at["norm_ffn1"]),
        (acc2, 3, 1, at["norm_mix"]), (acc_h, 0, 1, at["hgrn_g"]), (acc_c, 0, 3, at["conv_b"]),
        (acc3, 3, 1, at["norm_ffn2"]), (acc_head, 0, 1, at["norm_final"]), (acc_h, 1, 2, at["hgrn_lb"]),
        (dcw, 0, CONV_K, at["conv_w"])])
    finish_small, tok = finish_small
    last = [_mm_tn(s1, df1, 1, kc, DFF // NCHIP, "ffn1_dwout", after=tok),
            _mm_tn(h1, dab1, 1, kc, D, "ffn1_dwin", after=tok)]
    reduce(("ffn1_w_out", "ffn1_w_in"), last, finish_small(last[1][0]))
    return dx0


BLOCK_BYTES = 5 * 512 * 1024


def _row_block(rows, cols):
    for br in (512, 352, 256, 176, 128, 64, 32, 16, 8):
        if rows % br == 0 and br * cols * 4 <= BLOCK_BYTES:
            return br
    return rows


def _cast_into_slot(w, kc, name, after):
    R, C = w.shape
    br = _row_block(R, C)

    def body(kc_ref, w_ref, after_ref, o_ref):
        o_ref[...] = w_ref[...].astype(BF16)

    return pl.pallas_call(
        body, name=name,
        grid_spec=pltpu.PrefetchScalarGridSpec(
            num_scalar_prefetch=1, grid=(R // br,),
            in_specs=[pl.BlockSpec((br, C), lambda i, kc: (i, 0)), ANY],
            out_specs=pl.BlockSpec((None, br, C), lambda i, kc: (kc[0], i, 0))),
        out_shape=jax.ShapeDtypeStruct((NCHIP, R, C), BF16), compiler_params=_params("parallel"),
    )(kc, w, after)


def _adamw_math(w, g, m, v):
    nm = ADAM_B1 * m + (1.0 - ADAM_B1) * g
    nv = ADAM_B2 * v + (1.0 - ADAM_B2) * (g * g)
    m_hat = nm / (1.0 - ADAM_B1 ** ADAM_STEP)
    v_hat = nv / (1.0 - ADAM_B2 ** ADAM_STEP)
    return -ADAM_LR * (m_hat / (jnp.sqrt(v_hat) + ADAM_EPS) + ADAM_WD * w), nm, nv


def _adamw_rows(params, gsum, rows, after, name):
    n = len(params)

    def body(*refs):
        g_ref = refs[3 * n]
        outs = refs[3 * n + 2:]
        for i, r0 in enumerate(rows):
            w_ref, m_ref, v_ref = refs[3 * i:3 * i + 3]
            g = g_ref[r0:r0 + w_ref.shape[0], :]
            outs[4 * i][...] = g
            outs[4 * i + 1][...], outs[4 * i + 2][...], outs[4 * i + 3][...] = _adamw_math(
                w_ref[...], g, m_ref[...], v_ref[...])

    flat = [a for p in params for a in p]
    full = lambda a: pl.BlockSpec(a.shape, lambda: (0, 0))
    out = pl.pallas_call(
        body, name=name, in_specs=[full(a) for a in flat] + [full(gsum), ANY],
        out_specs=[full(p[0]) for p in params for _ in range(4)],
        out_shape=[jax.ShapeDtypeStruct(p[0].shape, F32) for p in params for _ in range(4)],
    )(*flat, gsum, after)
    return [out[4 * i:4 * i + 4] for i in range(n)]


def _adamw(w, g, m, v, name, after=None, copy_grad=False):
    R, C = w.shape
    br = _row_block(R, C)
    extra = [] if after is None else [after]
    nout = 4 if copy_grad else 3

    def body(w_ref, g_ref, m_ref, v_ref, *rest):
        d_ref, nm_ref, nv_ref = rest[-nout:][:3]
        gv = g_ref[...]
        if copy_grad:
            rest[-1][...] = gv
        d_ref[...], nm_ref[...], nv_ref[...] = _adamw_math(w_ref[...], gv, m_ref[...], v_ref[...])

    blk = pl.BlockSpec((br, C), lambda i: (i, 0))
    out = jax.ShapeDtypeStruct((R, C), F32)
    return pl.pallas_call(
        body, name=name, grid=(R // br,), in_specs=[blk] * 4 + [ANY] * len(extra), out_specs=[blk] * nout,
        out_shape=[out] * nout, compiler_params=_params("parallel"),
    )(w, g, m, v, *extra)


def _coords():
    return lax.axis_index("x"), lax.axis_index("y"), lax.axis_index("c")


def _flip(v, bit):
    return 1 - v if bit else v


HBM = pl.BlockSpec(memory_space=pltpu.HBM)
SEM = pl.BlockSpec(memory_space=pltpu.SEMAPHORE)
EFFECT = pltpu.SideEffectType.DATAFLOW_SIDE_EFFECTING


def _peer8(x, y, c, m):
    px, py, pc = _flip(x, m & 4), _flip(y, m & 2), _flip(c, m & 1)
    return (px, py, pc), 4 * px + 2 * py + pc


def _allgather8_start(blocks, name):
    def body(b_ref, send, recv, thru, token):
        x, y, c = _coords()
        me = 4 * x + 2 * y + c
        for m in range(1, 8):
            peer, _ = _peer8(x, y, c, m)
            pltpu.make_async_remote_copy(src_ref=b_ref.at[me], dst_ref=b_ref.at[me], send_sem=send.at[m - 1],
                                         recv_sem=recv.at[m - 1], device_id=peer, device_id_type=MESH).start()
        token[...] = jnp.zeros_like(token)

    sem = pltpu.SemaphoreType.DMA((7,))
    return pl.pallas_call(
        body, name=name,
        out_shape=[sem, sem, pltpu.HBM(blocks.shape, blocks.dtype), jax.ShapeDtypeStruct((8, 128), F32)],
        in_specs=[HBM], out_specs=[SEM, SEM, HBM, pl.BlockSpec(memory_space=pltpu.VMEM)], input_output_aliases={0: 2},
        compiler_params=pltpu.CompilerParams(has_side_effects=EFFECT),
    )(pltpu.with_memory_space_constraint(blocks, pltpu.HBM))


def _allgather8_wait(blocks, send_sem, recv_sem, after, name):
    def body(b_ref, send, recv, after_ref, thru):
        x, y, c = _coords()
        me = 4 * x + 2 * y + c
        for m in range(1, 8):
            peer, sender = _peer8(x, y, c, m)
            cp = pltpu.make_async_remote_copy(src_ref=b_ref.at[me], dst_ref=b_ref.at[sender], send_sem=send.at[m - 1],
                                              recv_sem=recv.at[m - 1], device_id=peer, device_id_type=MESH)
            cp.wait_send()
            cp.wait_recv()

    return pl.pallas_call(
        body, name=name, out_shape=pltpu.HBM(blocks.shape, blocks.dtype),
        in_specs=[HBM, SEM, SEM, ANY], out_specs=HBM, input_output_aliases={0: 0},
        compiler_params=pltpu.CompilerParams(has_side_effects=EFFECT),
    )(blocks, send_sem, recv_sem, after)


def _chip_peer(x, y, m):
    px, py = _flip(x, m & 2), _flip(y, m & 1)
    return px, py, 2 * px + py


def _core_rows(land, c):
    half = land.shape[1] // 2
    return pl.ds(pl.multiple_of(c * half, 16), half)


def _gather_start(lands, groups, halved, after, name):
    n, ng, na = len(lands), len(groups), len(after)

    def body(*refs):
        ins = refs[:n]
        sends, recvs = refs[n + na:n + na + ng], refs[n + na + ng:n + na + 2 * ng]
        token = refs[n + na + 2 * ng + n]
        x, y, c = _coords()
        k = 2 * x + y
        for gi, grp in enumerate(groups):
            for j, t in enumerate(grp):
                mine = ins[t].at[k, _core_rows(ins[t], c), :] if halved[gi] else ins[t].at[k]
                for m in (1, 2, 3):
                    px, py, _ = _chip_peer(x, y, m)
                    pltpu.make_async_remote_copy(
                        src_ref=mine, dst_ref=mine, send_sem=sends[gi].at[3 * j + m - 1],
                        recv_sem=recvs[gi].at[3 * j + m - 1], device_id=(px, py, c), device_id_type=MESH).start()
        token[...] = jnp.zeros_like(token)

    sems = [pltpu.SemaphoreType.DMA((3 * len(g),)) for g in groups]
    out = pl.pallas_call(
        body, name=name,
        out_shape=sems + sems + [pltpu.HBM(a.shape, a.dtype) for a in lands] + [jax.ShapeDtypeStruct((8, 128), F32)],
        in_specs=[HBM] * n + [ANY] * na,
        out_specs=[SEM] * (2 * ng) + [HBM] * n + [pl.BlockSpec(memory_space=pltpu.VMEM)],
        input_output_aliases={t: 2 * ng + t for t in range(n)},
        compiler_params=pltpu.CompilerParams(has_side_effects=EFFECT),
    )(*[pltpu.with_memory_space_constraint(a, pltpu.HBM) for a in lands], *after)
    return out[:ng], out[ng:2 * ng], out[2 * ng:2 * ng + n], out[2 * ng + n]


def _gather_wait(lands, halved, send_sem, recv_sem, after, name):
    n = len(lands)

    def body(*refs):
        ins, send, recv = refs[:n], refs[n], refs[n + 1]
        x, y, c = _coords()
        k = 2 * x + y
        for j in range(n):
            rows = _core_rows(ins[j], c)
            for m in (1, 2, 3):
                px, py, pk = _chip_peer(x, y, m)
                cp = pltpu.make_async_remote_copy(
                    src_ref=ins[j].at[k, rows, :] if halved else ins[j].at[k],
                    dst_ref=ins[j].at[pk, rows, :] if halved else ins[j].at[pk], send_sem=send.at[3 * j + m - 1],
                    recv_sem=recv.at[3 * j + m - 1], device_id=(px, py, c), device_id_type=MESH)
                cp.wait_send()
                cp.wait_recv()

    return pl.pallas_call(
        body, name=name, out_shape=[pltpu.HBM(a.shape, a.dtype) for a in lands],
        in_specs=[HBM] * n + [SEM, SEM, ANY], out_specs=[HBM] * n,
        input_output_aliases={j: j for j in range(n)},
        compiler_params=pltpu.CompilerParams(has_side_effects=EFFECT),
    )(*lands, send_sem, recv_sem, after)


def _sibling_fill(lands, name):
    n = len(lands)

    def body(*refs):
        ins = refs[:n]
        send_sems, recv_sems = refs[2 * n:]
        x, y, c = _coords()
        sends, recvs = [], []
        for t in range(n):
            for m in (1, 2, 3):
                _, _, pk = _chip_peer(x, y, m)
                for rows, lst in ((_core_rows(ins[t], c), sends), (_core_rows(ins[t], 1 - c), recvs)):
                    lst.append(pltpu.make_async_remote_copy(
                        src_ref=ins[t].at[pk, rows, :], dst_ref=ins[t].at[pk, rows, :],
                        send_sem=send_sems.at[3 * t + m - 1], recv_sem=recv_sems.at[3 * t + m - 1],
                        device_id=(x, y, 1 - c), device_id_type=MESH))
        for cp in sends:
            cp.start()
        for cp in recvs:
            cp.wait_recv()
        for cp in sends:
            cp.wait_send()

    return pl.pallas_call(
        body, name=name, in_specs=[ANY] * n, out_specs=[ANY] * n,
        out_shape=[jax.ShapeDtypeStruct(a.shape, a.dtype) for a in lands],
        input_output_aliases={t: t for t in range(n)},
        scratch_shapes=[pltpu.SemaphoreType.DMA((3 * n,)), pltpu.SemaphoreType.DMA((3 * n,))],
    )(*lands)


def _scatter_start(srcs, name, after=()):
    n, na = len(srcs), len(after)

    def body(*refs):
        ins, lands = refs[:n], refs[n:2 * n]
        send, recv = refs[2 * n + na], refs[2 * n + na + 1]
        token = refs[2 * n + na + 2 + 2 * n]
        x, y, c = _coords()
        k = 2 * x + y
        for t in range(n):
            for m in (1, 2, 3):
                px, py, pk = _chip_peer(x, y, m)
                pltpu.make_async_remote_copy(
                    src_ref=ins[t].at[pk], dst_ref=lands[t].at[k], send_sem=send.at[3 * t + m - 1],
                    recv_sem=recv.at[3 * t + m - 1], device_id=(px, py, c), device_id_type=MESH).start()
        token[...] = jnp.zeros_like(token)

    sem = pltpu.SemaphoreType.DMA((3 * n,))
    hbm = [pltpu.HBM(a.shape, a.dtype) for a in srcs]
    operands = list(srcs) + [lax.empty(a.shape, a.dtype) for a in srcs]
    out = pl.pallas_call(
        body, name=name, out_shape=[sem, sem] + hbm + hbm + [jax.ShapeDtypeStruct((8, 128), F32)],
        in_specs=[HBM] * (2 * n) + [ANY] * na,
        out_specs=[SEM, SEM] + [HBM] * (2 * n) + [pl.BlockSpec(memory_space=pltpu.VMEM)],
        input_output_aliases={t: 2 + t for t in range(2 * n)},
        compiler_params=pltpu.CompilerParams(has_side_effects=EFFECT),
    )(*[pltpu.with_memory_space_constraint(a, pltpu.HBM) for a in operands], *after)
    return out[0], out[1], out[2:2 + n], out[2 + n:2 + 2 * n], out[2 + 2 * n]


def _scatter_wait(srcs, lands, send_sem, recv_sem, after, name):
    n = len(srcs)

    def body(*refs):
        ins, land = refs[:n], refs[n:2 * n]
        send, recv = refs[2 * n], refs[2 * n + 1]
        x, y, c = _coords()
        for t in range(n):
            for m in (1, 2, 3):
                px, py, pk = _chip_peer(x, y, m)
                cp = pltpu.make_async_remote_copy(
                    src_ref=ins[t].at[pk], dst_ref=land[t].at[pk], send_sem=send.at[3 * t + m - 1],
                    recv_sem=recv.at[3 * t + m - 1], device_id=(px, py, c), device_id_type=MESH)
                cp.wait_send()
                cp.wait_recv()

    hbm = [pltpu.HBM(a.shape, a.dtype) for a in srcs]
    out = pl.pallas_call(
        body, name=name, out_shape=hbm + hbm, in_specs=[HBM] * (2 * n) + [SEM, SEM, ANY], out_specs=[HBM] * (2 * n),
        input_output_aliases={t: t for t in range(2 * n)},
        compiler_params=pltpu.CompilerParams(has_side_effects=EFFECT),
    )(*srcs, *lands, send_sem, recv_sem, after)
    return out[:n], out[n:]


def _sum_own_half(g, ra, kc, name):
    _, R, C = g.shape
    half = R // 2
    br = _row_block(half, C)
    nb = half // br

    def body(kc_ref, g_ref, ra_ref, o_ref):
        o_ref[...] = (g_ref[...] + ra_ref[...].astype(F32)).astype(BF16)

    return pl.pallas_call(
        body, name=name,
        grid_spec=pltpu.PrefetchScalarGridSpec(
            num_scalar_prefetch=1, grid=(NCHIP, nb),
            in_specs=[pl.BlockSpec((None, br, C), lambda j, i, kc: (j, kc[1] * nb + i, 0)),
                      pl.BlockSpec((None, br, C), lambda j, i, kc: (j, i, 0))],
            out_specs=pl.BlockSpec((None, br, C), lambda j, i, kc: (j, i, 0))),
        out_shape=jax.ShapeDtypeStruct((NCHIP, half, C), BF16),
        compiler_params=_params("parallel", "parallel"),
    )(kc, g, ra)


def _sum_chips(sa, rb, kc, name, after=None):
    _, half, C = rb.shape
    br = _row_block(half, C)
    nb = half // br
    extra = [] if after is None else [after]

    def body(kc_ref, own_ref, r1_ref, r2_ref, r3_ref, *rest):
        out, obuf, local_sems, send_sems, recv_sem = rest[-5:]
        i = pl.program_id(0)
        slot = i % 2
        x, y, c = _coords()

        def copies(i_, slot_):
            rows = out.at[pl.ds(pl.multiple_of((c * nb + i_) * br, 8), br), :]
            return (pltpu.make_async_copy(obuf.at[slot_], rows, local_sems.at[slot_]),
                    pltpu.make_async_remote_copy(src_ref=obuf.at[slot_], dst_ref=rows, send_sem=send_sems.at[slot_],
                                                 recv_sem=recv_sem, device_id=(x, y, 1 - c), device_id_type=MESH))

        @pl.when(i >= 2)
        def _():
            here, there = copies(i, slot)
            here.wait()
            there.wait_send()

        acc = own_ref[...].astype(F32) + r1_ref[...].astype(F32)
        obuf[slot] = (acc + r2_ref[...].astype(F32)) + r3_ref[...].astype(F32)
        here, there = copies(i, slot)
        here.start()
        there.start()

        @pl.when(i == nb - 1)
        def _():
            for s in range(min(2, nb)):
                here, there = copies(i, (i - s) % 2)
                here.wait()
                there.wait_send()
            theirs = out.at[pl.ds(pl.multiple_of((1 - c) * half, 8), half), :]
            pltpu.make_async_remote_copy(src_ref=theirs, dst_ref=theirs, send_sem=send_sems.at[0], recv_sem=recv_sem,
                                         device_id=(x, y, 1 - c), device_id_type=MESH).wait_recv()

    def slab(m):
        return pl.BlockSpec((None, br, C), lambda i, kc: (kc[0] ^ m, i, 0))

    return pl.pallas_call(
        body, name=name,
        grid_spec=pltpu.PrefetchScalarGridSpec(
            num_scalar_prefetch=1, grid=(nb,),
            in_specs=[slab(0), slab(1), slab(2), slab(3)] + [ANY] * len(extra),
            out_specs=ANY,
            scratch_shapes=[pltpu.VMEM((2, br, C), F32), pltpu.SemaphoreType.DMA((2,)), pltpu.SemaphoreType.DMA((2,)),
                            pltpu.SemaphoreType.DMA]),
        out_shape=jax.ShapeDtypeStruct((2 * half, C), F32), compiler_params=_params("arbitrary"),
    )(kc, sa, rb, rb, rb, *extra)


def _sum8(ga, name):
    _, R, C = ga.shape

    def body(g_ref, o_ref):
        acc = g_ref[0]
        for j in range(1, 8):
            acc = acc + g_ref[j]
        o_ref[...] = acc

    return pl.pallas_call(
        body, name=name, in_specs=[pl.BlockSpec((8, R, C), lambda: (0, 0, 0))],
        out_specs=pl.BlockSpec((R, C), lambda: (0, 0)), out_shape=jax.ShapeDtypeStruct((R, C), F32),
    )(ga)


ADA_COLS = 9 * D // NCHIP
ADA_BLK = 256


def _ada_mod(c_all, ada_w, ada_b, kme):
    def body(k_ref, c_ref, w_ref, b_ref, o_ref):
        cv = c_ref[...]
        cs = cv * _sigmoid(cv)
        o_ref[...] = jnp.dot(cs, w_ref[...], precision=lax.Precision.HIGHEST,
                             preferred_element_type=F32) + b_ref[...]

    nblk = ADA_COLS // ADA_BLK
    return pl.pallas_call(
        body, name="ada_mod",
        grid_spec=pltpu.PrefetchScalarGridSpec(
            num_scalar_prefetch=1, grid=(nblk,),
            in_specs=[pl.BlockSpec((8, D), lambda j, k: (0, 0)),
                      pl.BlockSpec((D, ADA_BLK), lambda j, k: (0, j)),
                      pl.BlockSpec((1, ADA_BLK), lambda j, k: (0, k[0] * nblk + j))],
            out_specs=pl.BlockSpec((None, 8, ADA_BLK), lambda j, k: (k[1], 0, j))),
        out_shape=jax.ShapeDtypeStruct((8, 8, ADA_COLS), F32),
        compiler_params=_params("parallel"),
    )(kme, c_all, ada_w, ada_b)


def _ada_grad(c_all_t, dmod_all, kidx):
    def body(k_ref, ct_ref, dm_ref, o_ref):
        cv = ct_ref[...]
        cs = cv * _sigmoid(cv)
        acc = cs[:, 0:1] * dm_ref[0:1, :]
        for b in range(1, 8):
            acc = acc + cs[:, b:b + 1] * dm_ref[b:b + 1, :]
        o_ref[...] = acc

    nblk = ADA_COLS // ADA_BLK
    return pl.pallas_call(
        body, name="ada_grad",
        grid_spec=pltpu.PrefetchScalarGridSpec(
            num_scalar_prefetch=1, grid=(nblk,),
            in_specs=[pl.BlockSpec((D, 8), lambda j, k: (0, 0)),
                      pl.BlockSpec((8, ADA_BLK), lambda j, k: (0, k[0] * nblk + j))],
            out_specs=pl.BlockSpec((D, ADA_BLK), lambda j, k: (0, j))),
        out_shape=jax.ShapeDtypeStruct((D, ADA_COLS), F32),
        compiler_params=_params("parallel"),
    )(kidx, c_all_t, dmod_all)


BIG = ("ffn1_w_in", "ffn1_w_out", "mix_w_in", "hgrn_w_o", "conv_w_o", "mix_w_out", "ffn2_w_in", "ffn2_w_out")
ROW_SHARDED = ("ffn1_w_out", "hgrn_w_o", "conv_w_o", "mix_w_out", "ffn2_w_out")
GATHER_GROUPS = ((0, 1), (2,), (3, 4, 5), (6, 7))
GATHER_HALVED = (True, True, False, False)
GATHER_STARTS = ((0,), (1,), (2, 3))
PACK_LEN = {"ada_b": 9, "hgrn_lb": 2}
WEIGHTS = ("ada_w", "ada_b", "norm_ffn1", "ffn1_w_in", "ffn1_w_out", "norm_mix", "mix_w_in", "hgrn_lb", "hgrn_g",
           "hgrn_w_o", "conv_w", "conv_b", "conv_ln_g", "conv_ln_b", "conv_w_o", "mix_w_out", "norm_ffn2",
           "ffn2_w_in", "ffn2_w_out", "norm_final")
PACKED = ("ada_b", "norm_ffn1", "norm_mix", "hgrn_g", "conv_b", "conv_ln_g", "conv_ln_b", "norm_ffn2",
          "norm_final", "hgrn_lb")


def _step(w, m, v, x, c, tgt):
    xi, yi, ci = _coords()
    kidx = (2 * xi + yi).astype(jnp.int32).reshape(1)
    kc = jnp.stack([2 * xi + yi, ci]).astype(jnp.int32)
    me = 4 * xi + 2 * yi + ci

    cq = D // NCHIP
    me32 = me.astype(jnp.int32)
    first = jnp.zeros((40, cq), F32).at[0:CONV_K].set(w["conv_w"][0]).at[32:36].set(c.reshape(NCHIP, cq))
    first = lax.dynamic_update_slice(jnp.zeros((8, 40, cq), F32), first[None], (me32, 0, 0))
    send, recv, first, tok = _allgather8_start(first, "gather_c_conv_w_start")
    early = {t: _cast_into_slot(w[BIG[t]][0], kc, "cast_" + BIG[t], tok) for t in GATHER_GROUPS[0]}
    first_all = _allgather8_wait(first, send, recv, early[GATHER_GROUPS[0][-1]], "gather_c_conv_w_wait")
    c_all = first_all[:, 32:36, :].reshape(8, D)
    mod_cols = _ada_mod(c_all, w["ada_w"][0], w["ada_b"], jnp.stack([kidx[0], me32]))
    send, recv, mod_cols, tok = _allgather8_start(mod_cols, "gather_mod_start")
    first_start = _gather_start([early[t] for t in GATHER_GROUPS[0]], [tuple(range(len(GATHER_GROUPS[0])))],
                                [GATHER_HALVED[0]], [tok], "gather_weights_start0")
    early.update({t: _cast_into_slot(w[BIG[t]][0], kc, "cast_" + BIG[t], first_start[3]) for t in GATHER_GROUPS[1]})
    mod_all = _allgather8_wait(mod_cols, send, recv, early[GATHER_GROUPS[1][-1]], "gather_mod_wait")
    mod = lax.dynamic_slice(mod_all, (0, me, 0), (8, 1, ADA_COLS))[::2].reshape(9, D)
    small = {n: w[n].reshape(-1, D) for n in ("norm_ffn1", "norm_mix", "hgrn_lb", "hgrn_g", "conv_b", "conv_ln_g",
                                              "conv_ln_b", "norm_ffn2", "norm_final")}
    small["conv_w"] = jnp.concatenate([first_all[2 * j, 0:32, :] for j in range(NCHIP)], axis=1)

    lands, sends, recvs = list(first_start[2]), list(first_start[0]), list(first_start[1])
    after = mod
    for part in GATHER_STARTS[1:]:
        tensors = [t for gi in part for t in GATHER_GROUPS[gi]]
        cast = [early[t] if t in early else _cast_into_slot(w[BIG[t]][0], kc, "cast_" + BIG[t], after)
                for t in tensors]
        groups = [tuple(tensors.index(t) for t in GATHER_GROUPS[gi]) for gi in part]
        s, r, thru, after = _gather_start(cast, groups, [GATHER_HALVED[gi] for gi in part], [after],
                                          "gather_weights_start%d" % part[0])
        lands, sends, recvs = lands + list(thru), sends + list(s), recvs + list(r)
    started_all = after
    ready = {}

    def weight(name, after):
        t = BIG.index(name)
        if t not in ready:
            gi = [t in grp for grp in GATHER_GROUPS].index(True)
            grp = GATHER_GROUPS[gi]
            outs = _gather_wait([lands[j] for j in grp], GATHER_HALVED[gi], sends[gi], recvs[gi],
                                started_all if gi == 0 else after, "gather_weights_wait%d" % gi)
            if GATHER_HALVED[gi]:
                outs = _sibling_fill(outs, "gather_weights_fill%d" % gi)
            ready.update(zip(grp, outs))
        return ready[t].reshape(-1, D) if name in ROW_SHARDED else ready[t]

    grads, delta, new_m, new_v = {}, {}, {}, {}
    flight = []
    landed = []

    def settle(after):
        names, sa, rb, send, recv = flight.pop()
        sa, rb = _scatter_wait(sa, rb, send, recv, after, "rs_chip_wait_" + names[0])
        landed.append((names, sa, rb))

    def reduce(names, pairs, after=None):
        gs = [g.reshape(NCHIP, -1, g.shape[-1]) for g, _ in pairs]
        ra = [r.reshape(NCHIP, -1, r.shape[-1]) for _, r in pairs]
        sa = [_sum_own_half(g, r, kc, "rs_sum_pair_" + n) for g, r, n in zip(gs, ra, names)]
        if flight:
            settle(sa[0])
        send, recv, sa, rb, tok = _scatter_start(sa, "rs_chip_start_" + names[0], () if after is None else (after,))
        flight.append((names, sa, rb, send, recv))
        started.append(tok)
        return tok

    def adamw(n, after=None):
        shape = w[n].shape
        two = (shape[-2], shape[-1])
        out = _adamw(w[n].reshape(two), grads[n], m[n].reshape(two), v[n].reshape(two), "adamw_" + n, after,
                     copy_grad=n in BIG)
        g_ = out[3] if n in BIG else grads[n]
        grads[n], delta[n], new_m[n], new_v[n] = (a.reshape(shape) for a in (g_, out[0], out[1], out[2]))
        return out[1]

    def finish(after=None):
        names, sa, rb = landed.pop(0)
        full = [_sum_chips(s, r, kc, "rs_sum_chips_" + n, after) for s, r, n in zip(sa, rb, names)]
        grads.update(zip(names, full))
        return [adamw(n) for n in names][-1]

    started = []

    smalls = []

    def reduce_small(parts):
        blocks = _pack_rows(parts, PACK_ROWS, "pack_small_grads", slot=me.astype(jnp.int32).reshape(1))
        send, recv, blocks, tok = _allgather8_start(blocks, "gather_small_grads_start")

        def finish(after):
            packed_all = _allgather8_wait(blocks, send, recv, after, "gather_small_grads_wait")
            smalls.extend([packed_all, _sum8(packed_all, "sum_small_grads")])
            return smalls[1]

        return finish, tok

    dx = _local_step(x[0], tgt[0], mod, small, kc, weight, reduce, reduce_small)
    packed_all, gsum = smalls
    loss = (0.5 / D) * jnp.sum(gsum[PACK_AT["loss"]])
    dmod_all = packed_all[:, 0:9, :].reshape(8, 9 * D)
    grads["ada_w"] = _ada_grad(c_all.T, dmod_all, kidx)
    grads["conv_w"] = lax.dynamic_slice(gsum, (PACK_AT["conv_w"], kidx[0] * (D // NCHIP)), (CONV_K, D // NCHIP))

    tok = started[-1]
    adamw("ada_w", tok)
    adamw("conv_w")
    two = lambda a, n: a.reshape(PACK_LEN.get(n, 1), D)
    small_out = _adamw_rows([(two(w[n], n), two(m[n], n), two(v[n], n)) for n in PACKED], gsum,
                            [PACK_AT[n] for n in PACKED], tok, "adamw_small")
    for n, quad in zip(PACKED, small_out):
        grads[n], delta[n], new_m[n], new_v[n] = (a.reshape(w[n].shape) for a in quad)
    last = small_out[-1][3]
    while landed:
        last = finish(tok)
    settle(last)
    finish()

    outs = [loss, dx[None]]
    for d in (grads, delta, new_m, new_v):
        outs += [d[n] for n in WEIGHTS]
    return tuple(outs)


def kernel(x, c, ada_w, ada_b, norm_ffn1, ffn1_w_in, ffn1_w_out, norm_mix, mix_w_in, hgrn_lb, hgrn_g, hgrn_w_o, conv_w, conv_b, conv_ln_g, conv_ln_b, conv_w_o, mix_w_out, norm_ffn2, ffn2_w_in, ffn2_w_out, norm_final, loss_target, m_ada_w, m_ada_b, m_norm_ffn1, m_ffn1_w_in, m_ffn1_w_out, m_norm_mix, m_mix_w_in, m_hgrn_lb, m_hgrn_g, m_hgrn_w_o, m_conv_w, m_conv_b, m_conv_ln_g, m_conv_ln_b, m_conv_w_o, m_mix_w_out, m_norm_ffn2, m_ffn2_w_in, m_ffn2_w_out, m_norm_final, v_ada_w, v_ada_b, v_norm_ffn1, v_ffn1_w_in, v_ffn1_w_out, v_norm_mix, v_mix_w_in, v_hgrn_lb, v_hgrn_g, v_hgrn_w_o, v_conv_w, v_conv_b, v_conv_ln_g, v_conv_ln_b, v_conv_w_o, v_mix_w_out, v_norm_ffn2, v_ffn2_w_in, v_ffn2_w_out, v_norm_final):
    w = dict(ada_w=ada_w, ada_b=ada_b, norm_ffn1=norm_ffn1, ffn1_w_in=ffn1_w_in, ffn1_w_out=ffn1_w_out,
             norm_mix=norm_mix, mix_w_in=mix_w_in, hgrn_lb=hgrn_lb, hgrn_g=hgrn_g, hgrn_w_o=hgrn_w_o, conv_w=conv_w,
             conv_b=conv_b, conv_ln_g=conv_ln_g, conv_ln_b=conv_ln_b, conv_w_o=conv_w_o, mix_w_out=mix_w_out,
             norm_ffn2=norm_ffn2, ffn2_w_in=ffn2_w_in, ffn2_w_out=ffn2_w_out, norm_final=norm_final)
    m = dict(ada_w=m_ada_w, ada_b=m_ada_b, norm_ffn1=m_norm_ffn1, ffn1_w_in=m_ffn1_w_in, ffn1_w_out=m_ffn1_w_out,
             norm_mix=m_norm_mix, mix_w_in=m_mix_w_in, hgrn_lb=m_hgrn_lb, hgrn_g=m_hgrn_g, hgrn_w_o=m_hgrn_w_o,
             conv_w=m_conv_w, conv_b=m_conv_b, conv_ln_g=m_conv_ln_g, conv_ln_b=m_conv_ln_b, conv_w_o=m_conv_w_o,
             mix_w_out=m_mix_w_out, norm_ffn2=m_norm_ffn2, ffn2_w_in=m_ffn2_w_in, ffn2_w_out=m_ffn2_w_out,
             norm_final=m_norm_final)
    v = dict(ada_w=v_ada_w, ada_b=v_ada_b, norm_ffn1=v_norm_ffn1, ffn1_w_in=v_ffn1_w_in, ffn1_w_out=v_ffn1_w_out,
             norm_mix=v_norm_mix, mix_w_in=v_mix_w_in, hgrn_lb=v_hgrn_lb, hgrn_g=v_hgrn_g, hgrn_w_o=v_hgrn_w_o,
             conv_w=v_conv_w, conv_b=v_conv_b, conv_ln_g=v_conv_ln_g, conv_ln_b=v_conv_ln_b, conv_w_o=v_conv_w_o,
             mix_w_out=v_mix_w_out, norm_ffn2=v_norm_ffn2, ffn2_w_in=v_ffn2_w_in, ffn2_w_out=v_ffn2_w_out,
             norm_final=v_norm_final)
    return _step(w, m, v, x, c, loss_target)
```

```python
import jax
import jax.numpy as jnp
from jax import lax
from jax.experimental import pallas as pl
from jax.experimental.pallas import tpu as pltpu

F32 = jnp.float32
BF16 = jnp.bfloat16

D = 1024
DFF = 2816
NCHIP = 4
FSH = 2 * DFF // NCHIP
HEADS = 8
DK = 128
CHUNK = 64
CONV_K = 31
HALO = 32
EPS = 1e-6
TB = 256
CB = 2048
DW_TOKENS = 2048
VMEM_LIMIT = 56 * 1024 * 1024

ADAM_LR = 0.001
ADAM_B1 = 0.9
ADAM_B2 = 0.999
ADAM_EPS = 1e-08
ADAM_WD = 0.01
ADAM_STEP = 10

MESH = pl.DeviceIdType.MESH
ANY = pl.BlockSpec(memory_space=pl.ANY)


def _params(*sem):
    return pltpu.CompilerParams(dimension_semantics=sem, vmem_limit_bytes=VMEM_LIMIT)


def _sigmoid(x):
    return 0.5 * jnp.tanh(0.5 * x) + 0.5


def _dsilu(x, sg):
    return sg * (1.0 + x * (1.0 - sg))


def _nt(a, b):
    return lax.dot_general(a, b, (((1,), (1,)), ((), ())), preferred_element_type=F32)


def _tn(a, b):
    return lax.dot_general(a, b, (((0,), (0,)), ((), ())), preferred_element_type=F32)


def _nn(a, b):
    return jnp.dot(a, b, preferred_element_type=F32)


def _colsum(x):
    return jnp.sum(x, axis=0, keepdims=True)


def _rms_fwd(x, gn, sc, sh):
    r = lax.rsqrt(jnp.mean(x * x, axis=-1, keepdims=True) + EPS)
    n = x * r
    h = (n * gn) * (1.0 + sc) + sh
    return r, n, h


def _rms_bwd(dh, r, n, gn, sc, acc_ref):
    acc_ref[0:1, :] += _colsum(dh)
    acc_ref[1:2, :] += _colsum(dh * (n * gn))
    dng = dh * (1.0 + sc)
    acc_ref[3:4, :] += _colsum(dng * n)
    dn = dng * gn
    return r * (dn - n * jnp.mean(dn * n, axis=-1, keepdims=True))


def _loss_head(x, tgt, gf, acc_ref):
    r = lax.rsqrt(jnp.mean(x * x, axis=-1, keepdims=True) + EPS)
    n = x * r
    err = n * gf - tgt
    acc_ref[1:2, :] += _colsum(err * err)
    dy = err * (1.0 / D)
    acc_ref[0:1, :] += _colsum(dy * n)
    dn = dy * gf
    return r * (dn - n * jnp.mean(dn * n, axis=-1, keepdims=True))


def _ffn_fwd(x, vec, w_in, w_out, name, head=None):
    T = x.shape[0]
    nh = 0 if head is None else 2

    def body(x_ref, vec_ref, *rest):
        win_hbm, wout_hbm = rest[nh:nh + 2]
        xo_ref, h_ref, a_ref, b_ref, s_ref, f_ref = rest[nh + 2:nh + 8]
        win, wout = rest[-2:]

        @pl.when(pl.program_id(0) == 0)
        def _():
            pltpu.sync_copy(win_hbm, win)
            pltpu.sync_copy(wout_hbm, wout)
            if head is not None:
                rest[nh + 8][...] = jnp.zeros((8, D), F32)

        x = x_ref[...]
        sh, sc, gate, gn = vec_ref[0:1, :], vec_ref[1:2, :], vec_ref[2:3, :], vec_ref[3:4, :]
        _, _, h = _rms_fwd(x, gn, sc, sh)
        hb = h.astype(BF16)
        h_ref[...] = hb
        f = jnp.zeros((TB, D), F32)
        for j in range(2):
            cols = slice(j * FSH, (j + 1) * FSH)
            a = _nn(hb, win[j])
            b = _nn(hb, win[2 + j])
            s = (a * _sigmoid(a) * b).astype(BF16)
            a_ref[:, cols] = a.astype(BF16)
            b_ref[:, cols] = b.astype(BF16)
            s_ref[:, cols] = s
            f = f + _nn(s, wout[cols, :])
        xo = x + (0.5 * gate) * f
        f_ref[...] = f.astype(BF16)
        if head is None:
            xo_ref[...] = xo
        else:
            xo_ref[...] = _loss_head(xo, rest[0][...], rest[1][0:1, :], rest[nh + 8])

    row = lambda w: pl.BlockSpec((TB, w), lambda i: (i, 0))
    vec8 = pl.BlockSpec((8, D), lambda i: (0, 0))
    acc = [] if head is None else [jax.ShapeDtypeStruct((8, D), F32)]
    return pl.pallas_call(
        body, name=name, grid=(T // TB,),
        in_specs=[row(D), vec8] + ([] if head is None else [row(D), vec8]) + [ANY, ANY],
        out_specs=[row(D), row(D), row(DFF), row(DFF), row(DFF), row(D)] + [vec8] * len(acc),
        out_shape=[jax.ShapeDtypeStruct((T, D), F32), jax.ShapeDtypeStruct((T, D), BF16),
                   jax.ShapeDtypeStruct((T, DFF), BF16), jax.ShapeDtypeStruct((T, DFF), BF16),
                   jax.ShapeDtypeStruct((T, DFF), BF16), jax.ShapeDtypeStruct((T, D), BF16)] + acc,
        scratch_shapes=[pltpu.VMEM((NCHIP, D, FSH), BF16), pltpu.VMEM((DFF, D), BF16)],
        compiler_params=_params("arbitrary"),
    )(x, vec, *([] if head is None else list(head)), w_in, w_out)


def _ffn_bwd(dxo, x, vec, a, b, f, w_in, w_out, name):
    T = x.shape[0]

    def body(dxo_ref, x_ref, vec_ref, a_ref, b_ref, f_ref, win_hbm, wout_hbm,
             dx_ref, df_ref, dab_ref, acc_ref, win, wout):
        @pl.when(pl.program_id(0) == 0)
        def _():
            pltpu.sync_copy(win_hbm, win)
            pltpu.sync_copy(wout_hbm, wout)
            acc_ref[...] = jnp.zeros_like(acc_ref)

        dxo = dxo_ref[...]
        x = x_ref[...]
        sh, sc, gate, gn = vec_ref[0:1, :], vec_ref[1:2, :], vec_ref[2:3, :], vec_ref[3:4, :]
        r, n, _ = _rms_fwd(x, gn, sc, sh)
        acc_ref[2:3, :] += _colsum(0.5 * f_ref[...].astype(F32) * dxo)
        dfb = ((0.5 * gate) * dxo).astype(BF16)
        df_ref[...] = dfb
        dh = jnp.zeros((TB, D), F32)
        for j in range(2):
            cols = slice(j * FSH, (j + 1) * FSH)
            ds = _nt(dfb, wout[cols, :])
            av = a_ref[:, cols].astype(F32)
            bv = b_ref[:, cols].astype(F32)
            sg = _sigmoid(av)
            da = (ds * bv * _dsilu(av, sg)).astype(BF16)
            db = (ds * (av * sg)).astype(BF16)
            dab_ref[j] = da
            dab_ref[2 + j] = db
            dh = dh + _nt(da, win[j]) + _nt(db, win[2 + j])
        dx_ref[...] = dxo + _rms_bwd(dh, r, n, gn, sc, acc_ref)

    row = lambda w: pl.BlockSpec((TB, w), lambda i: (i, 0))
    vec8 = pl.BlockSpec((8, D), lambda i: (0, 0))
    return pl.pallas_call(
        body, name=name, grid=(T // TB,),
        in_specs=[row(D), row(D), vec8, row(DFF), row(DFF), row(D), ANY, ANY],
        out_specs=[row(D), pl.BlockSpec((None, TB, D), lambda i: (0, i, 0)),
                   pl.BlockSpec((NCHIP, TB, FSH), lambda i: (0, i, 0)), vec8],
        out_shape=[jax.ShapeDtypeStruct((T, D), F32), jax.ShapeDtypeStruct((1, T, D), BF16),
                   jax.ShapeDtypeStruct((NCHIP, T, FSH), BF16), jax.ShapeDtypeStruct((8, D), F32)],
        scratch_shapes=[pltpu.VMEM((NCHIP, D, FSH), BF16), pltpu.VMEM((DFF, D), BF16)],
        compiler_params=_params("arbitrary"),
    )(dxo, x, vec, a, b, f, w_in, w_out)


def _mm_tn(a, b3, hp, kc, shard_rows, name, into=None, slab=0, slabs=None, after=None):
    T, M = a.shape
    P, _, N = b3.shape
    tm = M if M <= 1408 else M // 2
    tk = min(T, DW_TOKENS if P * (M // tm) > 1 else DW_TOKENS // 2)
    nk = T // tk
    ni = M // tm
    slabs = P // hp if slabs is None else slabs
    half = shard_rows // 2
    extra = ([] if into is None else list(into)) + ([] if after is None else [after])

    def body(kc_ref, a_ref, b_ref, *rest):
        o_ref, ra_ref, hbuf, send_sems, recv_sem = rest[-5:]
        p, i, k = pl.program_id(0), pl.program_id(1), pl.program_id(2)
        x, y, c = _coords()
        step = p * ni + i
        slot = step % 2

        def send(p_, i_, slot_):
            dst = ra_ref.at[slab + p_ // hp, pl.ds(pl.multiple_of(i_ * (tm // 2), 8), tm // 2),
                            pl.ds(pl.multiple_of((p_ % hp) * N, LANES), N)]
            return pltpu.make_async_remote_copy(
                src_ref=hbuf.at[slot_], dst_ref=dst, send_sem=send_sems.at[slot_], recv_sem=recv_sem,
                device_id=(x, y, 1 - c), device_id_type=MESH)

        @pl.when(k == 0)
        def _():
            o_ref[...] = jnp.zeros_like(o_ref)

        o_ref[...] += _tn(a_ref[...], b_ref[...])

        @pl.when(k == nk - 1)
        def _():
            @pl.when(step >= 2)
            def _():
                send(p, i, slot).wait_send()

            for j in range(tm // shard_rows):
                start = pl.multiple_of(j * shard_rows + (1 - kc_ref[1]) * half, 8)
                hbuf[slot, j * half:(j + 1) * half, :] = o_ref[pl.ds(start, half), :].astype(BF16)
            send(p, i, slot).start()

        @pl.when((step == P * ni - 1) & (k == nk - 1))
        def _():
            for s in range(min(2, P * ni)):
                send(p, i, (step - s) % 2).wait_send()
            mine = ra_ref.at[slab:slab + P // hp]
            pltpu.make_async_remote_copy(src_ref=mine, dst_ref=mine, send_sem=send_sems.at[0], recv_sem=recv_sem,
                                         device_id=(x, y, 1 - c), device_id_type=MESH).wait_recv()

    return pl.pallas_call(
        body, name=name,
        grid_spec=pltpu.PrefetchScalarGridSpec(
            num_scalar_prefetch=1, grid=(P, ni, nk),
            in_specs=[pl.BlockSpec((tk, tm), lambda p, i, k, kc: (k, i)),
                      pl.BlockSpec((None, tk, N), lambda p, i, k, kc: (p, k, 0))] + [ANY] * len(extra),
            out_specs=[pl.BlockSpec((None, tm, N), lambda p, i, k, kc: (slab + p // hp, i, p % hp)), ANY],
            scratch_shapes=[pltpu.VMEM((2, tm // 2, N), BF16), pltpu.SemaphoreType.DMA((2,)),
                            pltpu.SemaphoreType.DMA]),
        out_shape=[jax.ShapeDtypeStruct((slabs, M, hp * N), F32), jax.ShapeDtypeStruct((slabs, M // 2, hp * N), BF16)],
        input_output_aliases={} if into is None else {3: 0, 4: 1},
        compiler_params=_params("arbitrary", "arbitrary", "arbitrary"),
    )(kc, a, b3, *extra)


def _mix_proj_fwd(x, vec, w_in):
    T = x.shape[0]

    def body(x_ref, vec_ref, w_hbm, h_ref, qr_ref, g_ref, k_ref, v_ref, og_ref, u_ref, ua_ref, ub_ref,
             sa_ref, sb_ref, w):
        @pl.when(pl.program_id(0) == 0)
        def _():
            pltpu.sync_copy(w_hbm, w)

        x = x_ref[...]
        sh, sc, gn, lb = vec_ref[0:1, :], vec_ref[1:2, :], vec_ref[3:4, :], vec_ref[4:5, :]
        _, _, h = _rms_fwd(x, gn, sc, sh)
        hb = h.astype(BF16)
        h_ref[...] = hb
        p = _nn(hb, w[0])
        qr_ref[...] = p[:, :D].astype(BF16)
        fg = lb + (1.0 - lb) * _sigmoid(p[:, D:])
        g_ref[...] = jnp.log(fg)
        k_ref[...] = (1.0 - fg).astype(BF16)
        p = _nn(hb, w[1])
        v_ref[...] = p[:, :D].astype(BF16)
        og_ref[...] = p[:, D:].astype(BF16)
        p = _nn(hb, w[2])
        ua, ub = p[:, :D], p[:, D:]
        u_ref[...] = ua * _sigmoid(ub)
        ua_ref[...] = ua.astype(BF16)
        ub_ref[...] = ub.astype(BF16)
        p = _nn(hb, w[3])
        sa_ref[...] = _sigmoid(p[:, :D]).astype(BF16)
        sb_ref[...] = _sigmoid(p[:, D:]).astype(BF16)

    row = pl.BlockSpec((TB, D), lambda i: (i, 0))
    bf = jax.ShapeDtypeStruct((T, D), BF16)
    f32 = jax.ShapeDtypeStruct((T, D), F32)
    return pl.pallas_call(
        body, name="mix_proj_fwd", grid=(T // TB,),
        in_specs=[row, pl.BlockSpec((8, D), lambda i: (0, 0)), ANY],
        out_specs=[row] * 11,
        out_shape=[bf, bf, f32, bf, bf, bf, f32, bf, bf, bf, bf],
        scratch_shapes=[pltpu.VMEM((NCHIP, D, 2 * D), BF16)],
        compiler_params=_params("arbitrary"),
    )(x, vec, w_in)


def _mix_proj_bwd(dxo, x, vec, dpa, dpb, dpc, w_in):
    T = x.shape[0]

    def body(dxo_ref, x_ref, vec_ref, dpa_ref, dpb_ref, dpc_ref, w_hbm, dx_ref, acc_ref, w):
        @pl.when(pl.program_id(0) == 0)
        def _():
            pltpu.sync_copy(w_hbm, w)
            acc_ref[...] = jnp.zeros_like(acc_ref)

        x = x_ref[...]
        sh, sc, gn = vec_ref[0:1, :], vec_ref[1:2, :], vec_ref[3:4, :]
        r, n, _ = _rms_fwd(x, gn, sc, sh)
        dh = jnp.zeros((TB, D), F32)
        for p in range(8):
            src = dpa_ref[p] if p < 4 else (dpb_ref[p - 4] if p < 6 else dpc_ref[p - 6])
            dh = dh + _nt(src, w[p // 2, :, (p % 2) * D:(p % 2 + 1) * D])
        dx_ref[...] = dxo_ref[...] + _rms_bwd(dh, r, n, gn, sc, acc_ref)

    row = pl.BlockSpec((TB, D), lambda i: (i, 0))
    vec8 = pl.BlockSpec((8, D), lambda i: (0, 0))
    stack = lambda k: pl.BlockSpec((k, TB, D), lambda i: (0, i, 0))
    return pl.pallas_call(
        body, name="mix_proj_bwd", grid=(T // TB,),
        in_specs=[row, row, vec8, stack(4), stack(2), stack(2), ANY],
        out_specs=[row, vec8],
        out_shape=[jax.ShapeDtypeStruct((T, D), F32), jax.ShapeDtypeStruct((8, D), F32)],
        scratch_shapes=[pltpu.VMEM((NCHIP, D, 2 * D), BF16)],
        compiler_params=_params("arbitrary"),
    )(dxo, x, vec, dpa, dpb, dpc, w_in)


def _tri(lower):
    r = lax.broadcasted_iota(jnp.int32, (CHUNK, CHUNK), 0)
    c = lax.broadcasted_iota(jnp.int32, (CHUNK, CHUNK), 1)
    return (c <= r) if lower else (c >= r)


def _cumsum_rows(mask, g):
    hi = g.astype(BF16)
    rest = g - hi.astype(F32)
    mid = rest.astype(BF16)
    low = (rest - mid.astype(F32)).astype(BF16)
    n = g.shape[1]
    p = _nn(mask.astype(BF16), jnp.concatenate([hi, mid, low], axis=1))
    return (p[:, 2 * n:] + p[:, n:2 * n]) + p[:, :n]


def _chunk_decay(low, g, nck):
    bs, mids, lasts = [], [], []
    for c in range(nck):
        gc = g[c * CHUNK:(c + 1) * CHUNK]
        bs.append(_cumsum_rows(low, gc))
        mids.append(_colsum(gc[0:CHUNK // 2]))
        lasts.append(_colsum(gc))
    spread = lambda rows: jnp.concatenate([jnp.broadcast_to(r, (CHUNK, DK)) for r in rows], axis=0)
    return jnp.concatenate(bs, axis=0), spread(mids), spread(lasts), lasts


def _hgrn_fwd(qr, g, k, v, og, vec):
    T = qr.shape[0]
    nck = CB // CHUNK

    def body(qr_ref, g_ref, k_ref, v_ref, og_ref, vec_ref, out_ref, o_ref, st_ref, state):
        @pl.when(pl.program_id(1) == 0)
        def _():
            state[...] = jnp.zeros_like(state)

        low = _tri(True)
        qv = qr_ref[...].astype(F32)
        q = qv * _sigmoid(qv) * (DK ** -0.5)
        kk = k_ref[...].astype(F32)
        vb = v_ref[...]
        b, mid, last, lasts = _chunk_decay(low, g_ref[...], nck)
        qt = (q * jnp.exp(b - mid)).astype(BF16)
        kt = (kk * jnp.exp(mid - b)).astype(BF16)
        qe = (q * jnp.exp(b)).astype(BF16)
        kd = (kk * jnp.exp(last - b)).astype(BF16)
        intra, grow = [], []
        for c in range(nck):
            r = slice(c * CHUNK, (c + 1) * CHUNK)
            att = jnp.where(low, _nt(qt[r], kt[r]), 0.0).astype(BF16)
            intra.append(_nn(att, vb[r]))
            grow.append(_tn(vb[r], kd[r]))
        st = state[...]
        inter = []
        for c in range(nck):
            stb = st.astype(BF16)
            st_ref[c] = stb
            inter.append(_nt(qe[c * CHUNK:(c + 1) * CHUNK], stb))
            st = st * jnp.exp(lasts[c]) + grow[c]
        state[...] = st
        o = jnp.concatenate(intra, axis=0) + jnp.concatenate(inter, axis=0)
        o_ref[...] = o
        ogv = og_ref[...].astype(F32)
        rms = lax.rsqrt(jnp.mean(o * o, axis=-1, keepdims=True) + EPS)
        out_ref[...] = (o * rms * vec_ref[5:6, :] * (ogv * _sigmoid(ogv))).astype(BF16)

    blk = pl.BlockSpec((CB, DK), lambda h, i: (i, h))
    return pl.pallas_call(
        body, name="hgrn_fwd", grid=(HEADS, T // CB),
        in_specs=[blk, blk, blk, blk, blk, pl.BlockSpec((8, DK), lambda h, i: (0, h))],
        out_specs=[blk, blk, pl.BlockSpec((None, nck, DK, DK), lambda h, i: (h, i, 0, 0))],
        out_shape=[jax.ShapeDtypeStruct((T, D), BF16), jax.ShapeDtypeStruct((T, D), F32),
                   jax.ShapeDtypeStruct((HEADS, T // CHUNK, DK, DK), BF16)],
        scratch_shapes=[pltpu.VMEM((DK, DK), F32)],
        compiler_params=_params("parallel", "arbitrary"),
    )(qr, g, k, v, og, vec)


def _hgrn_bwd(dout, og, qr, g, k, v, o, st, vec):
    T = qr.shape[0]
    nck = CB // CHUNK
    nb = T // CB

    def body(dout_ref, og_ref, qr_ref, g_ref, k_ref, v_ref, o_ref, st_ref, vec_ref,
             dp_ref, acc_ref, dstate):
        @pl.when(pl.program_id(1) == 0)
        def _():
            dstate[...] = jnp.zeros_like(dstate)
            acc_ref[...] = jnp.zeros_like(acc_ref)

        o = o_ref[...]
        ogv = og_ref[...].astype(F32)
        dout = dout_ref[...].astype(F32)
        hg = vec_ref[5:6, :]
        sgo = _sigmoid(ogv)
        rms = lax.rsqrt(jnp.mean(o * o, axis=-1, keepdims=True) + EPS)
        ohat = o * rms
        dp_ref[3] = (dout * (ohat * hg) * _dsilu(ogv, sgo)).astype(BF16)
        don = dout * (ogv * sgo)
        acc_ref[0:1, :] += _colsum(don * ohat)
        dohat = don * hg
        dob = (rms * (dohat - ohat * jnp.mean(dohat * ohat, axis=-1, keepdims=True))).astype(BF16)

        low = _tri(True)
        upp = _tri(False)
        lb = vec_ref[4:5, :]
        qv = qr_ref[...].astype(F32)
        sgq = _sigmoid(qv)
        q = qv * sgq * (DK ** -0.5)
        kk = k_ref[...].astype(F32)
        vb = v_ref[...]
        gv = g_ref[...]
        b, mid, last, lasts = _chunk_decay(low, gv, nck)
        eq = jnp.exp(b - mid)
        ek = jnp.exp(mid - b)
        eb = jnp.exp(b)
        ed = jnp.exp(last - b)
        qtb, ktb, qeb, kdb = ((t).astype(BF16) for t in (q * eq, kk * ek, q * eb, kk * ed))
        rows = [slice(c * CHUNK, (c + 1) * CHUNK) for c in range(nck)]

        dv1, dqt, dkt, dqe, grow = [], [], [], [], []
        for c, r in enumerate(rows):
            att = jnp.where(low, _nt(qtb[r], ktb[r]), 0.0).astype(BF16)
            datt = jnp.where(low, _nt(dob[r], vb[r]), 0.0).astype(BF16)
            dv1.append(_tn(att, dob[r]))
            dqt.append(_nn(datt, ktb[r]))
            dkt.append(_tn(datt, qtb[r]))
            dqe.append(_nn(dob[r], st_ref[c]))
            grow.append(_tn(dob[r], qeb[r]))
        ds = dstate[...]
        ds1b, dl_state = [None] * nck, [None] * nck
        for c in reversed(range(nck)):
            el = jnp.exp(lasts[c])
            ds1b[c] = ds.astype(BF16)
            dl_state[c] = el * _colsum(ds * st_ref[c].astype(F32))
            ds = ds * el + grow[c]
        dstate[...] = ds
        dkd = jnp.concatenate([_nn(vb[r], ds1b[c]) for c, r in enumerate(rows)], axis=0)
        dv = jnp.concatenate(dv1, axis=0) + jnp.concatenate([_nt(kdb[r], ds1b[c]) for c, r in enumerate(rows)], axis=0)
        dqt, dkt, dqe = (jnp.concatenate(t, axis=0) for t in (dqt, dkt, dqe))
        dq = dqt * eq + dqe * eb
        dk = dkt * ek + dkd * ed
        dkdkd = dkd * kdb.astype(F32)
        db = dqt * qtb.astype(F32) - dkt * ktb.astype(F32) + dqe * qeb.astype(F32) - dkdkd
        dg = jnp.concatenate([_cumsum_rows(upp, db[r]) + (_colsum(dkdkd[r]) + dl_state[c])
                              for c, r in enumerate(rows)], axis=0)
        fg = jnp.exp(gv)
        dfg = dg * jnp.exp(-gv) - dk
        one_m_sig = (1.0 - fg) * (1.0 / (1.0 - lb))
        dp_ref[0] = (dq * (DK ** -0.5) * _dsilu(qv, sgq)).astype(BF16)
        dp_ref[1] = (dfg * (fg - lb) * one_m_sig).astype(BF16)
        dp_ref[2] = dv.astype(BF16)
        dlb = _colsum(dfg * one_m_sig) * (lb * (1.0 - lb))
        acc_ref[1:2, :] += dlb
        acc_ref[2:3, :] -= dlb

    blk = pl.BlockSpec((CB, DK), lambda h, i: (nb - 1 - i, h))
    return pl.pallas_call(
        body, name="hgrn_bwd", grid=(HEADS, nb),
        in_specs=[blk, blk, blk, blk, blk, blk, blk,
                  pl.BlockSpec((None, nck, DK, DK), lambda h, i: (h, nb - 1 - i, 0, 0)),
                  pl.BlockSpec((8, DK), lambda h, i: (0, h))],
        out_specs=[pl.BlockSpec((4, CB, DK), lambda h, i: (0, nb - 1 - i, h)),
                   pl.BlockSpec((8, DK), lambda h, i: (0, h))],
        out_shape=[jax.ShapeDtypeStruct((4, T, D), BF16), jax.ShapeDtypeStruct((8, D), F32)],
        scratch_shapes=[pltpu.VMEM((DK, DK), F32)],
        compiler_params=_params("parallel", "arbitrary"),
    )(dout, og, qr, g, k, v, o, st, vec)


def _ln_fwd(uc, lg, lbias):
    mu = jnp.mean(uc, axis=-1, keepdims=True)
    xc = uc - mu
    rstd = lax.rsqrt(jnp.mean(xc * xc, axis=-1, keepdims=True) + EPS)
    z = xc * rstd
    return rstd, z, z * lg + lbias


LANES = 128
SUBLANES = 8
CONV_ROWS = 64


def _lane_tiles():
    return [slice(l * LANES, (l + 1) * LANES) for l in range(D // LANES)]


def _row_shifts(x):
    n = x.shape[0]
    return [x] + [pltpu.roll(x, n - r, axis=0) for r in range(1, SUBLANES)]


TAPS_PAST = tuple(HALO - (CONV_K - 1) + j for j in range(CONV_K))
TAPS_AHEAD = tuple(CONV_K - 1 - j for j in range(CONV_K))


def _tap_windows(shifted, starts, r0, rows):
    for r in range(SUBLANES):
        taps = [(j, s // SUBLANES) for j, s in enumerate(starts) if s % SUBLANES == r]
        if not taps:
            continue
        lo = min(a for _, a in taps)
        hi = max(a for _, a in taps)
        span = shifted[r][r0 + lo * SUBLANES:r0 + hi * SUBLANES + rows]
        for j, a in taps:
            yield j, span[(a - lo) * SUBLANES:(a - lo) * SUBLANES + rows]


def _conv_fwd(u, cw, cvec):
    T = u.shape[0]
    per = TB // HALO

    def body(u_ref, halo_ref, cw_ref, cvec_ref, us_ref, uc_ref, pad):
        i = pl.program_id(0)
        pad[0:HALO, :] = jnp.where(i > 0, halo_ref[...], 0.0)
        pad[HALO:, :] = u_ref[...]
        for lanes in _lane_tiles():
            shifted = _row_shifts(pad[:, lanes])
            taps = cw_ref[:, lanes]
            for r0 in range(0, TB, CONV_ROWS):
                acc = jnp.broadcast_to(cvec_ref[0:1, lanes], (CONV_ROWS, LANES))
                for j, window in _tap_windows(shifted, TAPS_PAST, r0, CONV_ROWS):
                    acc = acc + taps[j:j + 1] * window
                uc_ref[r0:r0 + CONV_ROWS, lanes] = acc
        _, _, ul = _ln_fwd(uc_ref[...], cvec_ref[1:2, :], cvec_ref[2:3, :])
        us_ref[...] = (ul * _sigmoid(ul)).astype(BF16)

    row = pl.BlockSpec((TB, D), lambda i: (i, 0))
    return pl.pallas_call(
        body, name="conv_fwd", grid=(T // TB,),
        in_specs=[row, pl.BlockSpec((HALO, D), lambda i: (jnp.maximum(i * per - 1, 0), 0)),
                  pl.BlockSpec((32, D), lambda i: (0, 0)), pl.BlockSpec((8, D), lambda i: (0, 0))],
        out_specs=[row, row],
        out_shape=[jax.ShapeDtypeStruct((T, D), BF16), jax.ShapeDtypeStruct((T, D), F32)],
        scratch_shapes=[pltpu.VMEM((TB + HALO, D), F32)],
        compiler_params=_params("parallel"),
    )(u, u, cw, cvec)


def _conv_bwd_taps(duc, u, ua, ub, cw):
    T = u.shape[0]
    per = TB // HALO
    nblk = T // TB

    def body(duc_ref, dnext_ref, u_ref, uprev_ref, ua_ref, ub_ref, cw_ref, dp_ref, dcw_ref, upad, dpad, dcw):
        i = pl.program_id(0)

        @pl.when(i == 0)
        def _():
            dcw[...] = jnp.zeros_like(dcw)

        upad[0:HALO, :] = jnp.where(i > 0, uprev_ref[...], 0.0)
        upad[HALO:, :] = u_ref[...]
        dpad[0:TB, :] = duc_ref[...]
        dpad[TB:, :] = jnp.where(i < nblk - 1, dnext_ref[...], 0.0)
        for lanes in _lane_tiles():
            ushift = _row_shifts(upad[:, lanes])
            dshift = _row_shifts(dpad[:, lanes])
            for r0 in range(0, TB, CONV_ROWS):
                rows = slice(r0, r0 + CONV_ROWS)
                duc = duc_ref[rows, lanes]
                for j, window in _tap_windows(ushift, TAPS_PAST, r0, CONV_ROWS):
                    prod = duc * window
                    dcw[j, :, lanes] += jnp.sum(prod.reshape(CONV_ROWS // SUBLANES, SUBLANES, LANES), axis=0)
                du = jnp.zeros((CONV_ROWS, LANES), F32)
                for j, window in _tap_windows(dshift, TAPS_AHEAD, r0, CONV_ROWS):
                    du = du + cw_ref[j:j + 1, lanes] * window
                ua = ua_ref[rows, lanes].astype(F32)
                sg = _sigmoid(ub_ref[rows, lanes].astype(F32))
                dp_ref[0, rows, lanes] = (du * sg).astype(BF16)
                dp_ref[1, rows, lanes] = (du * ua * sg * (1.0 - sg)).astype(BF16)

        @pl.when(i == nblk - 1)
        def _():
            dcw_ref[...] = jnp.sum(dcw[...], axis=1)

    row = pl.BlockSpec((TB, D), lambda i: (i, 0))
    return pl.pallas_call(
        body, name="conv_bwd_taps", grid=(nblk,),
        in_specs=[row, pl.BlockSpec((HALO, D), lambda i: (jnp.minimum((i + 1) * per, T // HALO - 1), 0)),
                  row, pl.BlockSpec((HALO, D), lambda i: (jnp.maximum(i * per - 1, 0), 0)),
                  row, row, pl.BlockSpec((32, D), lambda i: (0, 0))],
        out_specs=[pl.BlockSpec((2, TB, D), lambda i: (0, i, 0)), pl.BlockSpec((32, D), lambda i: (0, 0))],
        out_shape=[jax.ShapeDtypeStruct((2, T, D), BF16), jax.ShapeDtypeStruct((32, D), F32)],
        scratch_shapes=[pltpu.VMEM((TB + HALO, D), F32), pltpu.VMEM((TB + HALO, D), F32),
                        pltpu.VMEM((32, SUBLANES, D), F32)],
        compiler_params=_params("arbitrary"),
    )(duc, duc, u, u, ua, ub, cw)


def _merge_fwd(x, oa, us, sa, sb, vec, w_ho, w_co, w_mo):
    T = x.shape[0]

    def body(x_ref, oa_ref, us_ref, sa_ref, sb_ref, vec_ref, who_hbm, wco_hbm, wmo_hbm,
             xo_ref, ya_ref, yb_ref, mg_ref, mo_ref, who, wco, wmo):
        @pl.when(pl.program_id(0) == 0)
        def _():
            pltpu.sync_copy(who_hbm, who)
            pltpu.sync_copy(wco_hbm, wco)
            pltpu.sync_copy(wmo_hbm, wmo)

        ya = _nn(oa_ref[...], who[...])
        yb = _nn(us_ref[...], wco[...])
        mg = (sa_ref[...].astype(F32) * ya + sb_ref[...].astype(F32) * yb).astype(BF16)
        mo = _nn(mg, wmo[...])
        xo_ref[...] = x_ref[...] + vec_ref[2:3, :] * mo
        ya_ref[...] = ya.astype(BF16)
        yb_ref[...] = yb.astype(BF16)
        mg_ref[...] = mg
        mo_ref[...] = mo.astype(BF16)

    row = pl.BlockSpec((TB, D), lambda i: (i, 0))
    bf = jax.ShapeDtypeStruct((T, D), BF16)
    wv = pltpu.VMEM((D, D), BF16)
    return pl.pallas_call(
        body, name="merge_fwd", grid=(T // TB,),
        in_specs=[row, row, row, row, row, pl.BlockSpec((8, D), lambda i: (0, 0)), ANY, ANY, ANY],
        out_specs=[row] * 5,
        out_shape=[jax.ShapeDtypeStruct((T, D), F32), bf, bf, bf, bf],
        scratch_shapes=[wv, wv, wv],
        compiler_params=_params("arbitrary"),
    )(x, oa, us, sa, sb, vec, w_ho, w_co, w_mo)


def _merge_bwd(dxo, mo, ya, yb, sa, sb, uc, vec, cvec, w_ho, w_co, w_mo):
    T = dxo.shape[0]

    def body(dxo_ref, mo_ref, ya_ref, yb_ref, sa_ref, sb_ref, uc_ref, vec_ref, cvec_ref, who_hbm, wco_hbm, wmo_hbm,
             dmo_ref, dya_ref, dyb_ref, doa_ref, duc_ref, dp_ref, acc_ref, cacc_ref, who, wco, wmo):
        @pl.when(pl.program_id(0) == 0)
        def _():
            pltpu.sync_copy(who_hbm, who)
            pltpu.sync_copy(wco_hbm, wco)
            pltpu.sync_copy(wmo_hbm, wmo)
            acc_ref[...] = jnp.zeros_like(acc_ref)
            cacc_ref[...] = jnp.zeros_like(cacc_ref)

        dxo = dxo_ref[...]
        acc_ref[2:3, :] += _colsum(mo_ref[...].astype(F32) * dxo)
        dmo = (vec_ref[2:3, :] * dxo).astype(BF16)
        dmo_ref[...] = dmo
        dmg = _nt(dmo, wmo[...])
        sa = sa_ref[...].astype(F32)
        sb = sb_ref[...].astype(F32)
        dya = (sa * dmg).astype(BF16)
        dyb = (sb * dmg).astype(BF16)
        dya_ref[...] = dya
        dyb_ref[...] = dyb
        dp_ref[0] = (dmg * ya_ref[...].astype(F32) * sa * (1.0 - sa)).astype(BF16)
        dp_ref[1] = (dmg * yb_ref[...].astype(F32) * sb * (1.0 - sb)).astype(BF16)
        doa_ref[...] = _nt(dya, who[...]).astype(BF16)
        dus = _nt(dyb, wco[...])
        lg = cvec_ref[1:2, :]
        rstd, z, ul = _ln_fwd(uc_ref[...], lg, cvec_ref[2:3, :])
        dul = dus * _dsilu(ul, _sigmoid(ul))
        cacc_ref[1:2, :] += _colsum(dul * z)
        cacc_ref[2:3, :] += _colsum(dul)
        dz = dul * lg
        duc = rstd * (dz - jnp.mean(dz, axis=-1, keepdims=True) - z * jnp.mean(dz * z, axis=-1, keepdims=True))
        cacc_ref[0:1, :] += _colsum(duc)
        duc_ref[...] = duc

    row = pl.BlockSpec((TB, D), lambda i: (i, 0))
    one = pl.BlockSpec((None, TB, D), lambda i: (0, i, 0))
    vec8 = pl.BlockSpec((8, D), lambda i: (0, 0))
    bf = jax.ShapeDtypeStruct((T, D), BF16)
    bf1 = jax.ShapeDtypeStruct((1, T, D), BF16)
    acc = jax.ShapeDtypeStruct((8, D), F32)
    wv = pltpu.VMEM((D, D), BF16)
    return pl.pallas_call(
        body, name="merge_bwd", grid=(T // TB,),
        in_specs=[row, row, row, row, row, row, row, vec8, vec8, ANY, ANY, ANY],
        out_specs=[one, one, one, row, row, pl.BlockSpec((2, TB, D), lambda i: (0, i, 0)), vec8, vec8],
        out_shape=[bf1, bf1, bf1, bf, jax.ShapeDtypeStruct((T, D), F32), jax.ShapeDtypeStruct((2, T, D), BF16), acc, acc],
        scratch_shapes=[wv, wv, wv],
        compiler_params=_params("arbitrary"),
    )(dxo, mo, ya, yb, sa, sb, uc, vec, cvec, w_ho, w_co, w_mo)


def _pack_rows(parts, total, name, slot=None):
    def body(*refs):
        out = refs[-1]
        out[...] = jnp.zeros_like(out)
        for ref, (_, src, n, dst) in zip(refs[-1 - len(parts):-1], parts):
            out[dst:dst + n, :] = ref[src:src + n, :]

    arrs = [p[0] for p in parts]
    if slot is None:
        return pl.pallas_call(
            body, name=name, in_specs=[pl.BlockSpec(a.shape, lambda: (0, 0)) for a in arrs],
            out_specs=pl.BlockSpec((total, D), lambda: (0, 0)),
            out_shape=jax.ShapeDtypeStruct((total, D), F32),
        )(*arrs)
    return pl.pallas_call(
        body, name=name,
        grid_spec=pltpu.PrefetchScalarGridSpec(
            num_scalar_prefetch=1, grid=(1,),
            in_specs=[pl.BlockSpec(a.shape, lambda i, s: (0, 0)) for a in arrs],
            out_specs=pl.BlockSpec((None, total, D), lambda i, s: (s[0], 0, 0))),
        out_shape=jax.ShapeDtypeStruct((8, total, D), F32),
    )(slot, *arrs)


PACK_ROWS = 56
PACK_AT = {"ada_b": 0, "loss": 9, "norm_ffn1": 10, "norm_mix": 11, "hgrn_g": 12, "conv_b": 13, "conv_ln_g": 14,
           "conv_ln_b": 15, "norm_ffn2": 16, "norm_final": 17, "hgrn_lb": 18, "conv_w": 20}


def _local_step(x, tgt, mod, small, kc, weight, reduce, reduce_small):
    lb = jax.nn.sigmoid(small["hgrn_lb"][0:1] - small["hgrn_lb"][1:2])
    vec1 = _pack_rows([(mod, 0, 3, 0), (small["norm_ffn1"], 0, 1, 3)], 8, "pack_vec1")
    vec2 = _pack_rows([(mod, 3, 3, 0), (small["norm_mix"], 0, 1, 3), (lb, 0, 1, 4), (small["hgrn_g"], 0, 1, 5)],
                      8, "pack_vec2")
    vec3 = _pack_rows([(mod, 6, 3, 0), (small["norm_ffn2"], 0, 1, 3)], 8, "pack_vec3")
    cvec = _pack_rows([(small["conv_b"], 0, 1, 0), (small["conv_ln_g"], 0, 1, 1), (small["conv_ln_b"], 0, 1, 2)],
                      8, "pack_cvec")
    cw = small["conv_w"]
    gvec = _pack_rows([(small["norm_final"], 0, 1, 0)], 8, "pack_gvec")

    wg = {n: weight(n, (vec1, vec2, vec3, cvec, gvec)) for n in ("ffn1_w_in", "ffn1_w_out")}
    x1, h1, a1, b1, s1, f1 = _ffn_fwd(x, vec1, wg["ffn1_w_in"], wg["ffn1_w_out"], "ffn1_fwd")
    wg["mix_w_in"] = weight("mix_w_in", x1)
    h2, qr, g, k, v, og, u, ua, ub, sa, sb = _mix_proj_fwd(x1, vec2, wg["mix_w_in"])
    oa, o, st = _hgrn_fwd(qr, g, k, v, og, vec2)
    us, uc = _conv_fwd(u, cw, cvec)
    wg.update({n: weight(n, us) for n in ("hgrn_w_o", "conv_w_o", "mix_w_out")})
    x2, ya, yb, mg, mo = _merge_fwd(x1, oa, us, sa, sb, vec2, wg["hgrn_w_o"], wg["conv_w_o"], wg["mix_w_out"])
    wg.update({n: weight(n, x2) for n in ("ffn2_w_in", "ffn2_w_out")})
    dx3, h3, a3, b3, s3, f3, acc_head = _ffn_fwd(x2, vec3, wg["ffn2_w_in"], wg["ffn2_w_out"], "ffn2_fwd",
                                                 head=(tgt, gvec))

    dx2, df3, dab3, acc3 = _ffn_bwd(dx3, x2, vec3, a3, b3, f3, wg["ffn2_w_in"], wg["ffn2_w_out"], "ffn2_bwd")
    tok = reduce(("ffn2_w_out", "ffn2_w_in"), [_mm_tn(s3, df3, 1, kc, DFF // NCHIP, "ffn2_dwout"),
                                               _mm_tn(h3, dab3, 1, kc, D, "ffn2_dwin")])
    vec2b = vec2 + tok[0:1, 0:1]
    dmo, dya, dyb, doa, duc, dpc, acc_m, acc_c = _merge_bwd(dx2, mo, ya, yb, sa, sb, uc, vec2b, cvec,
                                                            wg["hgrn_w_o"], wg["conv_w_o"], wg["mix_w_out"])
    tok = reduce(("mix_w_out", "hgrn_w_o", "conv_w_o"),
                 [_mm_tn(mg, dmo, 1, kc, D // NCHIP, "mix_dwout"), _mm_tn(oa, dya, 1, kc, D // NCHIP, "hgrn_dwo"),
                  _mm_tn(us, dyb, 1, kc, D // NCHIP, "conv_dwo")])
    vec2c = vec2 + tok[0:1, 0:1]
    dpb, dcw = _conv_bwd_taps(duc, u, ua, ub, cw)
    dpa, acc_h = _hgrn_bwd(doa, og, qr, g, k, v, o, st, vec2c)
    dx1, acc2 = _mix_proj_bwd(dx2, x1, vec2c, dpa, dpb, dpc, wg["mix_w_in"])
    gmix = _mm_tn(h2, dpa, 2, kc, D, "mix_dwin_a", slabs=NCHIP)
    gmix = _mm_tn(h2, dpb, 2, kc, D, "mix_dwin_b", into=gmix, slab=2, slabs=NCHIP)
    gmix = _mm_tn(h2, dpc, 2, kc, D, "mix_dwin_c", into=gmix, slab=3, slabs=NCHIP)
    tok = reduce(("mix_w_in",), [gmix])
    vec1b = vec1 + tok[0:1, 0:1]
    dx0, df1, dab1, acc1 = _ffn_bwd(dx1, x, vec1b, a1, b1, f1, wg["ffn1_w_in"], wg["ffn1_w_out"], "ffn1_bwd")

    at = PACK_AT
    finish_small = reduce_small([
        (acc1, 0, 3, at["ada_b"]), (acc2, 0, 2, at["ada_b"] + 3), (acc_m, 2, 1, at["ada_b"] + 5),
        (acc3, 0, 3, at["ada_b"] + 6), (acc_head, 1, 1, at["loss"]), (acc1, 3, 1, at["norm_ffn1"]),
        (acc2, 3, 1, at["norm_mix"]), (acc_h, 0, 1, at["hgrn_g"]), (acc_c, 0, 3, at["conv_b"]),
        (acc3, 3, 1, at["norm_ffn2"]), (acc_head, 0, 1, at["norm_final"]), (acc_h, 1, 2, at["hgrn_lb"]),
        (dcw, 0, CONV_K, at["conv_w"])])
    finish_small, tok = finish_small
    last = [_mm_tn(s1, df1, 1, kc, DFF // NCHIP, "ffn1_dwout", after=tok),
            _mm_tn(h1, dab1, 1, kc, D, "ffn1_dwin", after=tok)]
    reduce(("ffn1_w_out", "ffn1_w_in"), last, finish_small(last[1][0]))
    return dx0


BLOCK_BYTES = 5 * 512 * 1024


def _row_block(rows, cols):
    for br in (512, 352, 256, 176, 128, 64, 32, 16, 8):
        if rows % br == 0 and br * cols * 4 <= BLOCK_BYTES:
            return br
    return rows


def _cast_into_slot(w, kc, name, after):
    R, C = w.shape
    br = _row_block(R, C)

    def body(kc_ref, w_ref, after_ref, o_ref):
        o_ref[...] = w_ref[...].astype(BF16)

    return pl.pallas_call(
        body, name=name,
        grid_spec=pltpu.PrefetchScalarGridSpec(
            num_scalar_prefetch=1, grid=(R // br,),
            in_specs=[pl.BlockSpec((br, C), lambda i, kc: (i, 0)), ANY],
            out_specs=pl.BlockSpec((None, br, C), lambda i, kc: (kc[0], i, 0))),
        out_shape=jax.ShapeDtypeStruct((NCHIP, R, C), BF16), compiler_params=_params("parallel"),
    )(kc, w, after)


def _adamw_math(w, g, m, v):
    nm = ADAM_B1 * m + (1.0 - ADAM_B1) * g
    nv = ADAM_B2 * v + (1.0 - ADAM_B2) * (g * g)
    m_hat = nm / (1.0 - ADAM_B1 ** ADAM_STEP)
    v_hat = nv / (1.0 - ADAM_B2 ** ADAM_STEP)
    return -ADAM_LR * (m_hat / (jnp.sqrt(v_hat) + ADAM_EPS) + ADAM_WD * w), nm, nv


def _adamw_rows(params, gsum, rows, after, name):
    n = len(params)

    def body(*refs):
        g_ref = refs[3 * n]
        outs = refs[3 * n + 2:]
        for i, r0 in enumerate(rows):
            w_ref, m_ref, v_ref = refs[3 * i:3 * i + 3]
            g = g_ref[r0:r0 + w_ref.shape[0], :]
            outs[4 * i][...] = g
            outs[4 * i + 1][...], outs[4 * i + 2][...], outs[4 * i + 3][...] = _adamw_math(
                w_ref[...], g, m_ref[...], v_ref[...])

    flat = [a for p in params for a in p]
    full = lambda a: pl.BlockSpec(a.shape, lambda: (0, 0))
    out = pl.pallas_call(
        body, name=name, in_specs=[full(a) for a in flat] + [full(gsum), ANY],
        out_specs=[full(p[0]) for p in params for _ in range(4)],
        out_shape=[jax.ShapeDtypeStruct(p[0].shape, F32) for p in params for _ in range(4)],
    )(*flat, gsum, after)
    return [out[4 * i:4 * i + 4] for i in range(n)]


def _adamw(w, g, m, v, name, after=None, copy_grad=False):
    R, C = w.shape
    br = _row_block(R, C)
    extra = [] if after is None else [after]
    nout = 4 if copy_grad else 3

    def body(w_ref, g_ref, m_ref, v_ref, *rest):
        d_ref, nm_ref, nv_ref = rest[-nout:][:3]
        gv = g_ref[...]
        if copy_grad:
            rest[-1][...] = gv
        d_ref[...], nm_ref[...], nv_ref[...] = _adamw_math(w_ref[...], gv, m_ref[...], v_ref[...])

    blk = pl.BlockSpec((br, C), lambda i: (i, 0))
    out = jax.ShapeDtypeStruct((R, C), F32)
    return pl.pallas_call(
        body, name=name, grid=(R // br,), in_specs=[blk] * 4 + [ANY] * len(extra), out_specs=[blk] * nout,
        out_shape=[out] * nout, compiler_params=_params("parallel"),
    )(w, g, m, v, *extra)


def _coords():
    return lax.axis_index("x"), lax.axis_index("y"), lax.axis_index("c")


def _flip(v, bit):
    return 1 - v if bit else v


HBM = pl.BlockSpec(memory_space=pltpu.HBM)
SEM = pl.BlockSpec(memory_space=pltpu.SEMAPHORE)
EFFECT = pltpu.SideEffectType.DATAFLOW_SIDE_EFFECTING


def _peer8(x, y, c, m):
    px, py, pc = _flip(x, m & 4), _flip(y, m & 2), _flip(c, m & 1)
    return (px, py, pc), 4 * px + 2 * py + pc


def _allgather8_start(blocks, name):
    def body(b_ref, send, recv, thru, token):
        x, y, c = _coords()
        me = 4 * x + 2 * y + c
        for m in range(1, 8):
            peer, _ = _peer8(x, y, c, m)
            pltpu.make_async_remote_copy(src_ref=b_ref.at[me], dst_ref=b_ref.at[me], send_sem=send.at[m - 1],
                                         recv_sem=recv.at[m - 1], device_id=peer, device_id_type=MESH).start()
        token[...] = jnp.zeros_like(token)

    sem = pltpu.SemaphoreType.DMA((7,))
    return pl.pallas_call(
        body, name=name,
        out_shape=[sem, sem, pltpu.HBM(blocks.shape, blocks.dtype), jax.ShapeDtypeStruct((8, 128), F32)],
        in_specs=[HBM], out_specs=[SEM, SEM, HBM, pl.BlockSpec(memory_space=pltpu.VMEM)], input_output_aliases={0: 2},
        compiler_params=pltpu.CompilerParams(has_side_effects=EFFECT),
    )(pltpu.with_memory_space_constraint(blocks, pltpu.HBM))


def _allgather8_wait(blocks, send_sem, recv_sem, after, name):
    def body(b_ref, send, recv, after_ref, thru):
        x, y, c = _coords()
        me = 4 * x + 2 * y + c
        for m in range(1, 8):
            peer, sender = _peer8(x, y, c, m)
            cp = pltpu.make_async_remote_copy(src_ref=b_ref.at[me], dst_ref=b_ref.at[sender], send_sem=send.at[m - 1],
                                              recv_sem=recv.at[m - 1], device_id=peer, device_id_type=MESH)
            cp.wait_send()
            cp.wait_recv()

    return pl.pallas_call(
        body, name=name, out_shape=pltpu.HBM(blocks.shape, blocks.dtype),
        in_specs=[HBM, SEM, SEM, ANY], out_specs=HBM, input_output_aliases={0: 0},
        compiler_params=pltpu.CompilerParams(has_side_effects=EFFECT),
    )(blocks, send_sem, recv_sem, after)


def _chip_peer(x, y, m):
    px, py = _flip(x, m & 2), _flip(y, m & 1)
    return px, py, 2 * px + py


def _core_rows(land, c):
    half = land.shape[1] // 2
    return pl.ds(pl.multiple_of(c * half, 16), half)


def _gather_start(lands, groups, halved, after, name):
    n, ng, na = len(lands), len(groups), len(after)

    def body(*refs):
        ins = refs[:n]
        sends, recvs = refs[n + na:n + na + ng], refs[n + na + ng:n + na + 2 * ng]
        token = refs[n + na + 2 * ng + n]
        x, y, c = _coords()
        k = 2 * x + y
        for gi, grp in enumerate(groups):
            for j, t in enumerate(grp):
                mine = ins[t].at[k, _core_rows(ins[t], c), :] if halved[gi] else ins[t].at[k]
                for m in (1, 2, 3):
                    px, py, _ = _chip_peer(x, y, m)
                    pltpu.make_async_remote_copy(
                        src_ref=mine, dst_ref=mine, send_sem=sends[gi].at[3 * j + m - 1],
                        recv_sem=recvs[gi].at[3 * j + m - 1], device_id=(px, py, c), device_id_type=MESH).start()
        token[...] = jnp.zeros_like(token)

    sems = [pltpu.SemaphoreType.DMA((3 * len(g),)) for g in groups]
    out = pl.pallas_call(
        body, name=name,
        out_shape=sems + sems + [pltpu.HBM(a.shape, a.dtype) for a in lands] + [jax.ShapeDtypeStruct((8, 128), F32)],
        in_specs=[HBM] * n + [ANY] * na,
        out_specs=[SEM] * (2 * ng) + [HBM] * n + [pl.BlockSpec(memory_space=pltpu.VMEM)],
        input_output_aliases={t: 2 * ng + t for t in range(n)},
        compiler_params=pltpu.CompilerParams(has_side_effects=EFFECT),
    )(*[pltpu.with_memory_space_constraint(a, pltpu.HBM) for a in lands], *after)
    return out[:ng], out[ng:2 * ng], out[2 * ng:2 * ng + n], out[2 * ng + n]


def _gather_wait(lands, halved, send_sem, recv_sem, after, name):
    n = len(lands)

    def body(*refs):
        ins, send, recv = refs[:n], refs[n], refs[n + 1]
        x, y, c = _coords()
        k = 2 * x + y
        for j in range(n):
            rows = _core_rows(ins[j], c)
            for m in (1, 2, 3):
                px, py, pk = _chip_peer(x, y, m)
                cp = pltpu.make_async_remote_copy(
                    src_ref=ins[j].at[k, rows, :] if halved else ins[j].at[k],
                    dst_ref=ins[j].at[pk, rows, :] if halved else ins[j].at[pk], send_sem=send.at[3 * j + m - 1],
                    recv_sem=recv.at[3 * j + m - 1], device_id=(px, py, c), device_id_type=MESH)
                cp.wait_send()
                cp.wait_recv()

    return pl.pallas_call(
        body, name=name, out_shape=[pltpu.HBM(a.shape, a.dtype) for a in lands],
        in_specs=[HBM] * n + [SEM, SEM] + [ANY] * len(after), out_specs=[HBM] * n,
        input_output_aliases={j: j for j in range(n)},
        compiler_params=pltpu.CompilerParams(has_side_effects=EFFECT),
    )(*lands, send_sem, recv_sem, *after)


def _sibling_fill(lands, name):
    n = len(lands)

    def body(*refs):
        ins = refs[:n]
        send_sems, recv_sems = refs[2 * n:]
        x, y, c = _coords()
        sends, recvs = [], []
        for t in range(n):
            for m in (1, 2, 3):
                _, _, pk = _chip_peer(x, y, m)
                for rows, lst in ((_core_rows(ins[t], c), sends), (_core_rows(ins[t], 1 - c), recvs)):
                    lst.append(pltpu.make_async_remote_copy(
                        src_ref=ins[t].at[pk, rows, :], dst_ref=ins[t].at[pk, rows, :],
                        send_sem=send_sems.at[3 * t + m - 1], recv_sem=recv_sems.at[3 * t + m - 1],
                        device_id=(x, y, 1 - c), device_id_type=MESH))
        for cp in sends:
            cp.start()
        for cp in recvs:
            cp.wait_recv()
        for cp in sends:
            cp.wait_send()

    return pl.pallas_call(
        body, name=name, in_specs=[ANY] * n, out_specs=[ANY] * n,
        out_shape=[jax.ShapeDtypeStruct(a.shape, a.dtype) for a in lands],
        input_output_aliases={t: t for t in range(n)},
        scratch_shapes=[pltpu.SemaphoreType.DMA((3 * n,)), pltpu.SemaphoreType.DMA((3 * n,))],
    )(*lands)


def _scatter_start(srcs, name, after=()):
    n, na = len(srcs), len(after)

    def body(*refs):
        ins, lands = refs[:n], refs[n:2 * n]
        send, recv = refs[2 * n + na], refs[2 * n + na + 1]
        token = refs[2 * n + na + 2 + 2 * n]
        x, y, c = _coords()
        k = 2 * x + y
        for t in range(n):
            for m in (1, 2, 3):
                px, py, pk = _chip_peer(x, y, m)
                pltpu.make_async_remote_copy(
                    src_ref=ins[t].at[pk], dst_ref=lands[t].at[k], send_sem=send.at[3 * t + m - 1],
                    recv_sem=recv.at[3 * t + m - 1], device_id=(px, py, c), device_id_type=MESH).start()
        token[...] = jnp.zeros_like(token)

    sem = pltpu.SemaphoreType.DMA((3 * n,))
    hbm = [pltpu.HBM(a.shape, a.dtype) for a in srcs]
    operands = list(srcs) + [lax.empty(a.shape, a.dtype) for a in srcs]
    out = pl.pallas_call(
        body, name=name, out_shape=[sem, sem] + hbm + hbm + [jax.ShapeDtypeStruct((8, 128), F32)],
        in_specs=[HBM] * (2 * n) + [ANY] * na,
        out_specs=[SEM, SEM] + [HBM] * (2 * n) + [pl.BlockSpec(memory_space=pltpu.VMEM)],
        input_output_aliases={t: 2 + t for t in range(2 * n)},
        compiler_params=pltpu.CompilerParams(has_side_effects=EFFECT),
    )(*[pltpu.with_memory_space_constraint(a, pltpu.HBM) for a in operands], *after)
    return out[0], out[1], out[2:2 + n], out[2 + n:2 + 2 * n], out[2 + 2 * n]


def _scatter_wait(srcs, lands, send_sem, recv_sem, after, name):
    n = len(srcs)

    def body(*refs):
        ins, land = refs[:n], refs[n:2 * n]
        send, recv = refs[2 * n], refs[2 * n + 1]
        x, y, c = _coords()
        for t in range(n):
            for m in (1, 2, 3):
                px, py, pk = _chip_peer(x, y, m)
                cp = pltpu.make_async_remote_copy(
                    src_ref=ins[t].at[pk], dst_ref=land[t].at[pk], send_sem=send.at[3 * t + m - 1],
                    recv_sem=recv.at[3 * t + m - 1], device_id=(px, py, c), device_id_type=MESH)
                cp.wait_send()
                cp.wait_recv()

    hbm = [pltpu.HBM(a.shape, a.dtype) for a in srcs]
    out = pl.pallas_call(
        body, name=name, out_shape=hbm + hbm, in_specs=[HBM] * (2 * n) + [SEM, SEM, ANY], out_specs=[HBM] * (2 * n),
        input_output_aliases={t: t for t in range(2 * n)},
        compiler_params=pltpu.CompilerParams(has_side_effects=EFFECT),
    )(*srcs, *lands, send_sem, recv_sem, after)
    return out[:n], out[n:]


def _sum_own_half(g, ra, kc, name):
    _, R, C = g.shape
    half = R // 2
    br = _row_block(half, C)
    nb = half // br

    def body(kc_ref, g_ref, ra_ref, o_ref):
        o_ref[...] = (g_ref[...] + ra_ref[...].astype(F32)).astype(BF16)

    return pl.pallas_call(
        body, name=name,
        grid_spec=pltpu.PrefetchScalarGridSpec(
            num_scalar_prefetch=1, grid=(NCHIP, nb),
            in_specs=[pl.BlockSpec((None, br, C), lambda j, i, kc: (j, kc[1] * nb + i, 0)),
                      pl.BlockSpec((None, br, C), lambda j, i, kc: (j, i, 0))],
            out_specs=pl.BlockSpec((None, br, C), lambda j, i, kc: (j, i, 0))),
        out_shape=jax.ShapeDtypeStruct((NCHIP, half, C), BF16),
        compiler_params=_params("parallel", "parallel"),
    )(kc, g, ra)


def _sum_chips(sa, rb, kc, name, after=None):
    _, half, C = rb.shape
    br = _row_block(half, C)
    nb = half // br
    extra = [] if after is None else [after]

    def body(kc_ref, own_ref, r1_ref, r2_ref, r3_ref, *rest):
        out, obuf, local_sems, send_sems, recv_sem = rest[-5:]
        i = pl.program_id(0)
        slot = i % 2
        x, y, c = _coords()

        def copies(i_, slot_):
            rows = out.at[pl.ds(pl.multiple_of((c * nb + i_) * br, 8), br), :]
            return (pltpu.make_async_copy(obuf.at[slot_], rows, local_sems.at[slot_]),
                    pltpu.make_async_remote_copy(src_ref=obuf.at[slot_], dst_ref=rows, send_sem=send_sems.at[slot_],
                                                 recv_sem=recv_sem, device_id=(x, y, 1 - c), device_id_type=MESH))

        @pl.when(i >= 2)
        def _():
            here, there = copies(i, slot)
            here.wait()
            there.wait_send()

        acc = own_ref[...].astype(F32) + r1_ref[...].astype(F32)
        obuf[slot] = (acc + r2_ref[...].astype(F32)) + r3_ref[...].astype(F32)
        here, there = copies(i, slot)
        here.start()
        there.start()

        @pl.when(i == nb - 1)
        def _():
            for s in range(min(2, nb)):
                here, there = copies(i, (i - s) % 2)
                here.wait()
                there.wait_send()
            theirs = out.at[pl.ds(pl.multiple_of((1 - c) * half, 8), half), :]
            pltpu.make_async_remote_copy(src_ref=theirs, dst_ref=theirs, send_sem=send_sems.at[0], recv_sem=recv_sem,
                                         device_id=(x, y, 1 - c), device_id_type=MESH).wait_recv()

    def slab(m):
        return pl.BlockSpec((None, br, C), lambda i, kc: (kc[0] ^ m, i, 0))

    return pl.pallas_call(
        body, name=name,
        grid_spec=pltpu.PrefetchScalarGridSpec(
            num_scalar_prefetch=1, grid=(nb,),
            in_specs=[slab(0), slab(1), slab(2), slab(3)] + [ANY] * len(extra),
            out_specs=ANY,
            scratch_shapes=[pltpu.VMEM((2, br, C), F32), pltpu.SemaphoreType.DMA((2,)), pltpu.SemaphoreType.DMA((2,)),
                            pltpu.SemaphoreType.DMA]),
        out_shape=jax.ShapeDtypeStruct((2 * half, C), F32), compiler_params=_params("arbitrary"),
    )(kc, sa, rb, rb, rb, *extra)


def _sum8(ga, name):
    _, R, C = ga.shape

    def body(g_ref, o_ref):
        acc = g_ref[0]
        for j in range(1, 8):
            acc = acc + g_ref[j]
        o_ref[...] = acc

    return pl.pallas_call(
        body, name=name, in_specs=[pl.BlockSpec((8, R, C), lambda: (0, 0, 0))],
        out_specs=pl.BlockSpec((R, C), lambda: (0, 0)), out_shape=jax.ShapeDtypeStruct((R, C), F32),
    )(ga)


ADA_COLS = 9 * D // NCHIP
ADA_BLK = 256


def _ada_mod(c_all, ada_w, ada_b, kme):
    def body(k_ref, c_ref, w_ref, b_ref, o_ref):
        cv = c_ref[...]
        cs = cv * _sigmoid(cv)
        o_ref[...] = jnp.dot(cs, w_ref[...], precision=lax.Precision.HIGHEST,
                             preferred_element_type=F32) + b_ref[...]

    nblk = ADA_COLS // ADA_BLK
    return pl.pallas_call(
        body, name="ada_mod",
        grid_spec=pltpu.PrefetchScalarGridSpec(
            num_scalar_prefetch=1, grid=(nblk,),
            in_specs=[pl.BlockSpec((8, D), lambda j, k: (0, 0)),
                      pl.BlockSpec((D, ADA_BLK), lambda j, k: (0, j)),
                      pl.BlockSpec((1, ADA_BLK), lambda j, k: (0, k[0] * nblk + j))],
            out_specs=pl.BlockSpec((None, 8, ADA_BLK), lambda j, k: (k[1], 0, j))),
        out_shape=jax.ShapeDtypeStruct((8, 8, ADA_COLS), F32),
        compiler_params=_params("parallel"),
    )(kme, c_all, ada_w, ada_b)


def _ada_grad(c_all_t, dmod_all, kidx):
    def body(k_ref, ct_ref, dm_ref, o_ref):
        cv = ct_ref[...]
        cs = cv * _sigmoid(cv)
        acc = cs[:, 0:1] * dm_ref[0:1, :]
        for b in range(1, 8):
            acc = acc + cs[:, b:b + 1] * dm_ref[b:b + 1, :]
        o_ref[...] = acc

    nblk = ADA_COLS // ADA_BLK
    return pl.pallas_call(
        body, name="ada_grad",
        grid_spec=pltpu.PrefetchScalarGridSpec(
            num_scalar_prefetch=1, grid=(nblk,),
            in_specs=[pl.BlockSpec((D, 8), lambda j, k: (0, 0)),
                      pl.BlockSpec((8, ADA_BLK), lambda j, k: (0, k[0] * nblk + j))],
            out_specs=pl.BlockSpec((D, ADA_BLK), lambda j, k: (0, j))),
        out_shape=jax.ShapeDtypeStruct((D, ADA_COLS), F32),
        compiler_params=_params("parallel"),
    )(kidx, c_all_t, dmod_all)


BIG = ("ffn1_w_in", "ffn1_w_out", "mix_w_in", "hgrn_w_o", "conv_w_o", "mix_w_out", "ffn2_w_in", "ffn2_w_out")
ROW_SHARDED = ("ffn1_w_out", "hgrn_w_o", "conv_w_o", "mix_w_out", "ffn2_w_out")
GATHER_GROUPS = ((0, 1), (2,), (3, 4, 5), (6, 7))
GATHER_HALVED = (True, True, False, False)
GATHER_STARTS = ((0,), (1,), (2, 3))
PACK_LEN = {"ada_b": 9, "hgrn_lb": 2}
WEIGHTS = ("ada_w", "ada_b", "norm_ffn1", "ffn1_w_in", "ffn1_w_out", "norm_mix", "mix_w_in", "hgrn_lb", "hgrn_g",
           "hgrn_w_o", "conv_w", "conv_b", "conv_ln_g", "conv_ln_b", "conv_w_o", "mix_w_out", "norm_ffn2",
           "ffn2_w_in", "ffn2_w_out", "norm_final")
PACKED = ("ada_b", "norm_ffn1", "norm_mix", "hgrn_g", "conv_b", "conv_ln_g", "conv_ln_b", "norm_ffn2",
          "norm_final", "hgrn_lb")


def _step(w, m, v, x, c, tgt):
    xi, yi, ci = _coords()
    kidx = (2 * xi + yi).astype(jnp.int32).reshape(1)
    kc = jnp.stack([2 * xi + yi, ci]).astype(jnp.int32)
    me = 4 * xi + 2 * yi + ci

    cq = D // NCHIP
    me32 = me.astype(jnp.int32)
    first = jnp.zeros((40, cq), F32).at[0:CONV_K].set(w["conv_w"][0]).at[32:36].set(c.reshape(NCHIP, cq))
    first = lax.dynamic_update_slice(jnp.zeros((8, 40, cq), F32), first[None], (me32, 0, 0))
    send, recv, first, tok = _allgather8_start(first, "gather_c_conv_w_start")
    early = {t: _cast_into_slot(w[BIG[t]][0], kc, "cast_" + BIG[t], tok) for t in GATHER_GROUPS[0]}
    first_all = _allgather8_wait(first, send, recv, early[GATHER_GROUPS[0][-1]], "gather_c_conv_w_wait")
    c_all = first_all[:, 32:36, :].reshape(8, D)
    mod_cols = _ada_mod(c_all, w["ada_w"][0], w["ada_b"], jnp.stack([kidx[0], me32]))
    send, recv, mod_cols, tok = _allgather8_start(mod_cols, "gather_mod_start")
    first_start = _gather_start([early[t] for t in GATHER_GROUPS[0]], [tuple(range(len(GATHER_GROUPS[0])))],
                                [GATHER_HALVED[0]], [tok], "gather_weights_start0")
    early.update({t: _cast_into_slot(w[BIG[t]][0], kc, "cast_" + BIG[t], first_start[3]) for t in GATHER_GROUPS[1]})
    mod_all = _allgather8_wait(mod_cols, send, recv, early[GATHER_GROUPS[1][-1]], "gather_mod_wait")
    mod = lax.dynamic_slice(mod_all, (0, me, 0), (8, 1, ADA_COLS))[::2].reshape(9, D)
    small = {n: w[n].reshape(-1, D) for n in ("norm_ffn1", "norm_mix", "hgrn_lb", "hgrn_g", "conv_b", "conv_ln_g",
                                              "conv_ln_b", "norm_ffn2", "norm_final")}
    small["conv_w"] = jnp.concatenate([first_all[2 * j, 0:32, :] for j in range(NCHIP)], axis=1)

    lands, sends, recvs = list(first_start[2]), list(first_start[0]), list(first_start[1])
    after = mod
    for part in GATHER_STARTS[1:]:
        tensors = [t for gi in part for t in GATHER_GROUPS[gi]]
        cast = [early[t] if t in early else _cast_into_slot(w[BIG[t]][0], kc, "cast_" + BIG[t], after)
                for t in tensors]
        groups = [tuple(tensors.index(t) for t in GATHER_GROUPS[gi]) for gi in part]
        s, r, thru, after = _gather_start(cast, groups, [GATHER_HALVED[gi] for gi in part], [after],
                                          "gather_weights_start%d" % part[0])
        lands, sends, recvs = lands + list(thru), sends + list(s), recvs + list(r)
    started_all = after
    ready = {}

    def weight(name, after):
        t = BIG.index(name)
        if t not in ready:
            gi = [t in grp for grp in GATHER_GROUPS].index(True)
            grp = GATHER_GROUPS[gi]
            outs = _gather_wait([lands[j] for j in grp], GATHER_HALVED[gi], sends[gi], recvs[gi],
                                list(after) + [started_all] if gi == 0 else [after], "gather_weights_wait%d" % gi)
            if GATHER_HALVED[gi]:
                outs = _sibling_fill(outs, "gather_weights_fill%d" % gi)
            ready.update(zip(grp, outs))
        return ready[t].reshape(-1, D) if name in ROW_SHARDED else ready[t]

    grads, delta, new_m, new_v = {}, {}, {}, {}
    flight = []
    landed = []

    def settle(after):
        names, sa, rb, send, recv = flight.pop()
        sa, rb = _scatter_wait(sa, rb, send, recv, after, "rs_chip_wait_" + names[0])
        landed.append((names, sa, rb))

    def reduce(names, pairs, after=None):
        gs = [g.reshape(NCHIP, -1, g.shape[-1]) for g, _ in pairs]
        ra = [r.reshape(NCHIP, -1, r.shape[-1]) for _, r in pairs]
        sa = [_sum_own_half(g, r, kc, "rs_sum_pair_" + n) for g, r, n in zip(gs, ra, names)]
        if flight:
            settle(sa[0])
        send, recv, sa, rb, tok = _scatter_start(sa, "rs_chip_start_" + names[0], () if after is None else (after,))
        flight.append((names, sa, rb, send, recv))
        started.append(tok)
        return tok

    def adamw(n, after=None):
        shape = w[n].shape
        two = (shape[-2], shape[-1])
        out = _adamw(w[n].reshape(two), grads[n], m[n].reshape(two), v[n].reshape(two), "adamw_" + n, after,
                     copy_grad=n in BIG)
        g_ = out[3] if n in BIG else grads[n]
        grads[n], delta[n], new_m[n], new_v[n] = (a.reshape(shape) for a in (g_, out[0], out[1], out[2]))
        return out[1]

    def finish(after=None):
        names, sa, rb = landed.pop(0)
        full = [_sum_chips(s, r, kc, "rs_sum_chips_" + n, after) for s, r, n in zip(sa, rb, names)]
        grads.update(zip(names, full))
        return [adamw(n) for n in names][-1]

    started = []

    smalls = []

    def reduce_small(parts):
        blocks = _pack_rows(parts, PACK_ROWS, "pack_small_grads", slot=me.astype(jnp.int32).reshape(1))
        send, recv, blocks, tok = _allgather8_start(blocks, "gather_small_grads_start")

        def finish(after):
            packed_all = _allgather8_wait(blocks, send, recv, after, "gather_small_grads_wait")
            smalls.extend([packed_all, _sum8(packed_all, "sum_small_grads")])
            return smalls[1]

        return finish, tok

    dx = _local_step(x[0], tgt[0], mod, small, kc, weight, reduce, reduce_small)
    packed_all, gsum = smalls
    loss = (0.5 / D) * jnp.sum(gsum[PACK_AT["loss"]])
    dmod_all = packed_all[:, 0:9, :].reshape(8, 9 * D)
    grads["ada_w"] = _ada_grad(c_all.T, dmod_all, kidx)
    grads["conv_w"] = lax.dynamic_slice(gsum, (PACK_AT["conv_w"], kidx[0] * (D // NCHIP)), (CONV_K, D // NCHIP))

    tok = started[-1]
    adamw("ada_w", tok)
    adamw("conv_w")
    two = lambda a, n: a.reshape(PACK_LEN.get(n, 1), D)
    small_out = _adamw_rows([(two(w[n], n), two(m[n], n), two(v[n], n)) for n in PACKED], gsum,
                            [PACK_AT[n] for n in PACKED], tok, "adamw_small")
    for n, quad in zip(PACKED, small_out):
        grads[n], delta[n], new_m[n], new_v[n] = (a.reshape(w[n].shape) for a in quad)
    last = small_out[-1][3]
    while landed:
        last = finish(tok)
    settle(last)
    finish()

    outs = [loss, dx[None]]
    for d in (grads, delta, new_m, new_v):
        outs += [d[n] for n in WEIGHTS]
    return tuple(outs)


def kernel(x, c, ada_w, ada_b, norm_ffn1, ffn1_w_in, ffn1_w_out, norm_mix, mix_w_in, hgrn_lb, hgrn_g, hgrn_w_o, conv_w, conv_b, conv_ln_g, conv_ln_b, conv_w_o, mix_w_out, norm_ffn2, ffn2_w_in, ffn2_w_out, norm_final, loss_target, m_ada_w, m_ada_b, m_norm_ffn1, m_ffn1_w_in, m_ffn1_w_out, m_norm_mix, m_mix_w_in, m_hgrn_lb, m_hgrn_g, m_hgrn_w_o, m_conv_w, m_conv_b, m_conv_ln_g, m_conv_ln_b, m_conv_w_o, m_mix_w_out, m_norm_ffn2, m_ffn2_w_in, m_ffn2_w_out, m_norm_final, v_ada_w, v_ada_b, v_norm_ffn1, v_ffn1_w_in, v_ffn1_w_out, v_norm_mix, v_mix_w_in, v_hgrn_lb, v_hgrn_g, v_hgrn_w_o, v_conv_w, v_conv_b, v_conv_ln_g, v_conv_ln_b, v_conv_w_o, v_mix_w_out, v_norm_ffn2, v_ffn2_w_in, v_ffn2_w_out, v_norm_final):
    w = dict(ada_w=ada_w, ada_b=ada_b, norm_ffn1=norm_ffn1, ffn1_w_in=ffn1_w_in, ffn1_w_out=ffn1_w_out,
             norm_mix=norm_mix, mix_w_in=mix_w_in, hgrn_lb=hgrn_lb, hgrn_g=hgrn_g, hgrn_w_o=hgrn_w_o, conv_w=conv_w,
             conv_b=conv_b, conv_ln_g=conv_ln_g, conv_ln_b=conv_ln_b, conv_w_o=conv_w_o, mix_w_out=mix_w_out,
             norm_ffn2=norm_ffn2, ffn2_w_in=ffn2_w_in, ffn2_w_out=ffn2_w_out, norm_final=norm_final)
    m = dict(ada_w=m_ada_w, ada_b=m_ada_b, norm_ffn1=m_norm_ffn1, ffn1_w_in=m_ffn1_w_in, ffn1_w_out=m_ffn1_w_out,
             norm_mix=m_norm_mix, mix_w_in=m_mix_w_in, hgrn_lb=m_hgrn_lb, hgrn_g=m_hgrn_g, hgrn_w_o=m_hgrn_w_o,
             conv_w=m_conv_w, conv_b=m_conv_b, conv_ln_g=m_conv_ln_g, conv_ln_b=m_conv_ln_b, conv_w_o=m_conv_w_o,
             mix_w_out=m_mix_w_out, norm_ffn2=m_norm_ffn2, ffn2_w_in=m_ffn2_w_in, ffn2_w_out=m_ffn2_w_out,
             norm_final=m_norm_final)
    v = dict(ada_w=v_ada_w, ada_b=v_ada_b, norm_ffn1=v_norm_ffn1, ffn1_w_in=v_ffn1_w_in, ffn1_w_out=v_ffn1_w_out,
             norm_mix=v_norm_mix, mix_w_in=v_mix_w_in, hgrn_lb=v_hgrn_lb, hgrn_g=v_hgrn_g, hgrn_w_o=v_hgrn_w_o,
             conv_w=v_conv_w, conv_b=v_conv_b, conv_ln_g=v_conv_ln_g, conv_ln_b=v_conv_ln_b, conv_w_o=v_conv_w_o,
             mix_w_out=v_mix_w_out, norm_ffn2=v_norm_ffn2, ffn2_w_in=v_ffn2_w_in, ffn2_w_out=v_ffn2_w_out,
             norm_final=v_norm_final)
    return _step(w, m, v, x, c, loss_target)
```

```python
import jax
import jax.numpy as jnp
from jax import lax
from jax.experimental import pallas as pl
from jax.experimental.pallas import tpu as pltpu

F32 = jnp.float32
BF16 = jnp.bfloat16

D = 1024
DFF = 2816
NCHIP = 4
FSH = 2 * DFF // NCHIP
HEADS = 8
DK = 128
CHUNK = 64
CONV_K = 31
HALO = 32
EPS = 1e-6
TB = 256
CB = 2048
DW_TOKENS = 2048
VMEM_LIMIT = 56 * 1024 * 1024

ADAM_LR = 0.001
ADAM_B1 = 0.9
ADAM_B2 = 0.999
ADAM_EPS = 1e-08
ADAM_WD = 0.01
ADAM_STEP = 10

MESH = pl.DeviceIdType.MESH
ANY = pl.BlockSpec(memory_space=pl.ANY)


def _params(*sem):
    return pltpu.CompilerParams(dimension_semantics=sem, vmem_limit_bytes=VMEM_LIMIT)


def _sigmoid(x):
    return 0.5 * jnp.tanh(0.5 * x) + 0.5


def _dsilu(x, sg):
    return sg * (1.0 + x * (1.0 - sg))


def _nt(a, b):
    return lax.dot_general(a, b, (((1,), (1,)), ((), ())), preferred_element_type=F32)


def _tn(a, b):
    return lax.dot_general(a, b, (((0,), (0,)), ((), ())), preferred_element_type=F32)


def _nn(a, b):
    return jnp.dot(a, b, preferred_element_type=F32)


def _colsum(x):
    return jnp.sum(x, axis=0, keepdims=True)


def _rms_fwd(x, gn, sc, sh):
    r = lax.rsqrt(jnp.mean(x * x, axis=-1, keepdims=True) + EPS)
    n = x * r
    h = (n * gn) * (1.0 + sc) + sh
    return r, n, h


def _rms_bwd(dh, r, n, gn, sc, acc_ref):
    acc_ref[0:1, :] += _colsum(dh)
    acc_ref[1:2, :] += _colsum(dh * (n * gn))
    dng = dh * (1.0 + sc)
    acc_ref[3:4, :] += _colsum(dng * n)
    dn = dng * gn
    return r * (dn - n * jnp.mean(dn * n, axis=-1, keepdims=True))


def _loss_head(x, tgt, gf, acc_ref):
    r = lax.rsqrt(jnp.mean(x * x, axis=-1, keepdims=True) + EPS)
    n = x * r
    err = n * gf - tgt
    acc_ref[1:2, :] += _colsum(err * err)
    dy = err * (1.0 / D)
    acc_ref[0:1, :] += _colsum(dy * n)
    dn = dy * gf
    return r * (dn - n * jnp.mean(dn * n, axis=-1, keepdims=True))


def _ffn_fwd(x, vec, w_in, w_out, name, head=None):
    T = x.shape[0]
    nh = 0 if head is None else 2

    def body(x_ref, vec_ref, *rest):
        win_hbm, wout_hbm = rest[nh:nh + 2]
        xo_ref, h_ref, a_ref, b_ref, s_ref, f_ref = rest[nh + 2:nh + 8]
        win, wout, wsems = rest[-3:]
        load_in = pltpu.make_async_copy(win_hbm, win, wsems.at[0])
        load_out = pltpu.make_async_copy(wout_hbm, wout, wsems.at[1])

        @pl.when(pl.program_id(0) == 0)
        def _():
            load_in.start()
            load_out.start()
            load_in.wait()
            if head is not None:
                rest[nh + 8][...] = jnp.zeros((8, D), F32)

        x = x_ref[...]
        sh, sc, gate, gn = vec_ref[0:1, :], vec_ref[1:2, :], vec_ref[2:3, :], vec_ref[3:4, :]
        _, _, h = _rms_fwd(x, gn, sc, sh)
        hb = h.astype(BF16)
        h_ref[...] = hb
        f = jnp.zeros((TB, D), F32)
        for j in range(2):
            cols = slice(j * FSH, (j + 1) * FSH)
            a = _nn(hb, win[j])
            b = _nn(hb, win[2 + j])
            s = (a * _sigmoid(a) * b).astype(BF16)
            a_ref[:, cols] = a.astype(BF16)
            b_ref[:, cols] = b.astype(BF16)
            s_ref[:, cols] = s
            if j == 0:
                @pl.when(pl.program_id(0) == 0)
                def _():
                    load_out.wait()
            f = f + _nn(s, wout[cols, :])
        xo = x + (0.5 * gate) * f
        f_ref[...] = f.astype(BF16)
        if head is None:
            xo_ref[...] = xo
        else:
            xo_ref[...] = _loss_head(xo, rest[0][...], rest[1][0:1, :], rest[nh + 8])

    row = lambda w: pl.BlockSpec((TB, w), lambda i: (i, 0))
    vec8 = pl.BlockSpec((8, D), lambda i: (0, 0))
    acc = [] if head is None else [jax.ShapeDtypeStruct((8, D), F32)]
    return pl.pallas_call(
        body, name=name, grid=(T // TB,),
        in_specs=[row(D), vec8] + ([] if head is None else [row(D), vec8]) + [ANY, ANY],
        out_specs=[row(D), row(D), row(DFF), row(DFF), row(DFF), row(D)] + [vec8] * len(acc),
        out_shape=[jax.ShapeDtypeStruct((T, D), F32), jax.ShapeDtypeStruct((T, D), BF16),
                   jax.ShapeDtypeStruct((T, DFF), BF16), jax.ShapeDtypeStruct((T, DFF), BF16),
                   jax.ShapeDtypeStruct((T, DFF), BF16), jax.ShapeDtypeStruct((T, D), BF16)] + acc,
        scratch_shapes=[pltpu.VMEM((NCHIP, D, FSH), BF16), pltpu.VMEM((DFF, D), BF16),
                        pltpu.SemaphoreType.DMA((2,))],
        compiler_params=_params("arbitrary"),
    )(x, vec, *([] if head is None else list(head)), w_in, w_out)


def _ffn_bwd(dxo, x, vec, a, b, f, w_in, w_out, name):
    T = x.shape[0]

    def body(dxo_ref, x_ref, vec_ref, a_ref, b_ref, f_ref, win_hbm, wout_hbm,
             dx_ref, df_ref, dab_ref, acc_ref, win, wout):
        @pl.when(pl.program_id(0) == 0)
        def _():
            pltpu.sync_copy(win_hbm, win)
            pltpu.sync_copy(wout_hbm, wout)
            acc_ref[...] = jnp.zeros_like(acc_ref)

        dxo = dxo_ref[...]
        x = x_ref[...]
        sh, sc, gate, gn = vec_ref[0:1, :], vec_ref[1:2, :], vec_ref[2:3, :], vec_ref[3:4, :]
        r, n, _ = _rms_fwd(x, gn, sc, sh)
        acc_ref[2:3, :] += _colsum(0.5 * f_ref[...].astype(F32) * dxo)
        dfb = ((0.5 * gate) * dxo).astype(BF16)
        df_ref[...] = dfb
        dh = jnp.zeros((TB, D), F32)
        for j in range(2):
            cols = slice(j * FSH, (j + 1) * FSH)
            ds = _nt(dfb, wout[cols, :])
            av = a_ref[:, cols].astype(F32)
            bv = b_ref[:, cols].astype(F32)
            sg = _sigmoid(av)
            da = (ds * bv * _dsilu(av, sg)).astype(BF16)
            db = (ds * (av * sg)).astype(BF16)
            dab_ref[j] = da
            dab_ref[2 + j] = db
            dh = dh + _nt(da, win[j]) + _nt(db, win[2 + j])
        dx_ref[...] = dxo + _rms_bwd(dh, r, n, gn, sc, acc_ref)

    row = lambda w: pl.BlockSpec((TB, w), lambda i: (i, 0))
    vec8 = pl.BlockSpec((8, D), lambda i: (0, 0))
    return pl.pallas_call(
        body, name=name, grid=(T // TB,),
        in_specs=[row(D), row(D), vec8, row(DFF), row(DFF), row(D), ANY, ANY],
        out_specs=[row(D), pl.BlockSpec((None, TB, D), lambda i: (0, i, 0)),
                   pl.BlockSpec((NCHIP, TB, FSH), lambda i: (0, i, 0)), vec8],
        out_shape=[jax.ShapeDtypeStruct((T, D), F32), jax.ShapeDtypeStruct((1, T, D), BF16),
                   jax.ShapeDtypeStruct((NCHIP, T, FSH), BF16), jax.ShapeDtypeStruct((8, D), F32)],
        scratch_shapes=[pltpu.VMEM((NCHIP, D, FSH), BF16), pltpu.VMEM((DFF, D), BF16)],
        compiler_params=_params("arbitrary"),
    )(dxo, x, vec, a, b, f, w_in, w_out)


def _mm_tn(a, b3, hp, kc, shard_rows, name, into=None, slab=0, slabs=None, after=None):
    T, M = a.shape
    P, _, N = b3.shape
    tm = M if M <= 1408 else M // 2
    tk = min(T, DW_TOKENS if P * (M // tm) > 1 else DW_TOKENS // 2)
    nk = T // tk
    ni = M // tm
    slabs = P // hp if slabs is None else slabs
    half = shard_rows // 2
    extra = ([] if into is None else list(into)) + ([] if after is None else [after])

    def body(kc_ref, a_ref, b_ref, *rest):
        o_ref, ra_ref, hbuf, send_sems, recv_sem = rest[-5:]
        p, i, k = pl.program_id(0), pl.program_id(1), pl.program_id(2)
        x, y, c = _coords()
        step = p * ni + i
        slot = step % 2

        def send(p_, i_, slot_):
            dst = ra_ref.at[slab + p_ // hp, pl.ds(pl.multiple_of(i_ * (tm // 2), 8), tm // 2),
                            pl.ds(pl.multiple_of((p_ % hp) * N, LANES), N)]
            return pltpu.make_async_remote_copy(
                src_ref=hbuf.at[slot_], dst_ref=dst, send_sem=send_sems.at[slot_], recv_sem=recv_sem,
                device_id=(x, y, 1 - c), device_id_type=MESH)

        @pl.when(k == 0)
        def _():
            o_ref[...] = jnp.zeros_like(o_ref)

        o_ref[...] += _tn(a_ref[...], b_ref[...])

        @pl.when(k == nk - 1)
        def _():
            @pl.when(step >= 2)
            def _():
                send(p, i, slot).wait_send()

            for j in range(tm // shard_rows):
                start = pl.multiple_of(j * shard_rows + (1 - kc_ref[1]) * half, 8)
                hbuf[slot, j * half:(j + 1) * half, :] = o_ref[pl.ds(start, half), :].astype(BF16)
            send(p, i, slot).start()

        @pl.when((step == P * ni - 1) & (k == nk - 1))
        def _():
            for s in range(min(2, P * ni)):
                send(p, i, (step - s) % 2).wait_send()
            mine = ra_ref.at[slab:slab + P // hp]
            pltpu.make_async_remote_copy(src_ref=mine, dst_ref=mine, send_sem=send_sems.at[0], recv_sem=recv_sem,
                                         device_id=(x, y, 1 - c), device_id_type=MESH).wait_recv()

    return pl.pallas_call(
        body, name=name,
        grid_spec=pltpu.PrefetchScalarGridSpec(
            num_scalar_prefetch=1, grid=(P, ni, nk),
            in_specs=[pl.BlockSpec((tk, tm), lambda p, i, k, kc: (k, i)),
                      pl.BlockSpec((None, tk, N), lambda p, i, k, kc: (p, k, 0))] + [ANY] * len(extra),
            out_specs=[pl.BlockSpec((None, tm, N), lambda p, i, k, kc: (slab + p // hp, i, p % hp)), ANY],
            scratch_shapes=[pltpu.VMEM((2, tm // 2, N), BF16), pltpu.SemaphoreType.DMA((2,)),
                            pltpu.SemaphoreType.DMA]),
        out_shape=[jax.ShapeDtypeStruct((slabs, M, hp * N), F32), jax.ShapeDtypeStruct((slabs, M // 2, hp * N), BF16)],
        input_output_aliases={} if into is None else {3: 0, 4: 1},
        compiler_params=_params("arbitrary", "arbitrary", "arbitrary"),
    )(kc, a, b3, *extra)


def _mix_proj_fwd(x, vec, w_in):
    T = x.shape[0]

    def body(x_ref, vec_ref, w_hbm, h_ref, qr_ref, g_ref, k_ref, v_ref, og_ref, u_ref, ua_ref, ub_ref,
             sa_ref, sb_ref, w):
        @pl.when(pl.program_id(0) == 0)
        def _():
            pltpu.sync_copy(w_hbm, w)

        x = x_ref[...]
        sh, sc, gn, lb = vec_ref[0:1, :], vec_ref[1:2, :], vec_ref[3:4, :], vec_ref[4:5, :]
        _, _, h = _rms_fwd(x, gn, sc, sh)
        hb = h.astype(BF16)
        h_ref[...] = hb
        p = _nn(hb, w[0])
        qr_ref[...] = p[:, :D].astype(BF16)
        fg = lb + (1.0 - lb) * _sigmoid(p[:, D:])
        g_ref[...] = jnp.log(fg)
        k_ref[...] = (1.0 - fg).astype(BF16)
        p = _nn(hb, w[1])
        v_ref[...] = p[:, :D].astype(BF16)
        og_ref[...] = p[:, D:].astype(BF16)
        p = _nn(hb, w[2])
        ua, ub = p[:, :D], p[:, D:]
        u_ref[...] = ua * _sigmoid(ub)
        ua_ref[...] = ua.astype(BF16)
        ub_ref[...] = ub.astype(BF16)
        p = _nn(hb, w[3])
        sa_ref[...] = _sigmoid(p[:, :D]).astype(BF16)
        sb_ref[...] = _sigmoid(p[:, D:]).astype(BF16)

    row = pl.BlockSpec((TB, D), lambda i: (i, 0))
    bf = jax.ShapeDtypeStruct((T, D), BF16)
    f32 = jax.ShapeDtypeStruct((T, D), F32)
    return pl.pallas_call(
        body, name="mix_proj_fwd", grid=(T // TB,),
        in_specs=[row, pl.BlockSpec((8, D), lambda i: (0, 0)), ANY],
        out_specs=[row] * 11,
        out_shape=[bf, bf, f32, bf, bf, bf, f32, bf, bf, bf, bf],
        scratch_shapes=[pltpu.VMEM((NCHIP, D, 2 * D), BF16)],
        compiler_params=_params("arbitrary"),
    )(x, vec, w_in)


def _mix_proj_bwd(dxo, x, vec, dpa, dpb, dpc, w_in):
    T = x.shape[0]

    def body(dxo_ref, x_ref, vec_ref, dpa_ref, dpb_ref, dpc_ref, w_hbm, dx_ref, acc_ref, w):
        @pl.when(pl.program_id(0) == 0)
        def _():
            pltpu.sync_copy(w_hbm, w)
            acc_ref[...] = jnp.zeros_like(acc_ref)

        x = x_ref[...]
        sh, sc, gn = vec_ref[0:1, :], vec_ref[1:2, :], vec_ref[3:4, :]
        r, n, _ = _rms_fwd(x, gn, sc, sh)
        dh = jnp.zeros((TB, D), F32)
        for p in range(8):
            src = dpa_ref[p] if p < 4 else (dpb_ref[p - 4] if p < 6 else dpc_ref[p - 6])
            dh = dh + _nt(src, w[p // 2, :, (p % 2) * D:(p % 2 + 1) * D])
        dx_ref[...] = dxo_ref[...] + _rms_bwd(dh, r, n, gn, sc, acc_ref)

    row = pl.BlockSpec((TB, D), lambda i: (i, 0))
    vec8 = pl.BlockSpec((8, D), lambda i: (0, 0))
    stack = lambda k: pl.BlockSpec((k, TB, D), lambda i: (0, i, 0))
    return pl.pallas_call(
        body, name="mix_proj_bwd", grid=(T // TB,),
        in_specs=[row, row, vec8, stack(4), stack(2), stack(2), ANY],
        out_specs=[row, vec8],
        out_shape=[jax.ShapeDtypeStruct((T, D), F32), jax.ShapeDtypeStruct((8, D), F32)],
        scratch_shapes=[pltpu.VMEM((NCHIP, D, 2 * D), BF16)],
        compiler_params=_params("arbitrary"),
    )(dxo, x, vec, dpa, dpb, dpc, w_in)


def _tri(lower):
    r = lax.broadcasted_iota(jnp.int32, (CHUNK, CHUNK), 0)
    c = lax.broadcasted_iota(jnp.int32, (CHUNK, CHUNK), 1)
    return (c <= r) if lower else (c >= r)


def _cumsum_rows(mask, g):
    hi = g.astype(BF16)
    rest = g - hi.astype(F32)
    mid = rest.astype(BF16)
    low = (rest - mid.astype(F32)).astype(BF16)
    n = g.shape[1]
    p = _nn(mask.astype(BF16), jnp.concatenate([hi, mid, low], axis=1))
    return (p[:, 2 * n:] + p[:, n:2 * n]) + p[:, :n]


def _chunk_decay(low, g, nck):
    bs, mids, lasts = [], [], []
    for c in range(nck):
        gc = g[c * CHUNK:(c + 1) * CHUNK]
        bs.append(_cumsum_rows(low, gc))
        mids.append(_colsum(gc[0:CHUNK // 2]))
        lasts.append(_colsum(gc))
    spread = lambda rows: jnp.concatenate([jnp.broadcast_to(r, (CHUNK, DK)) for r in rows], axis=0)
    return jnp.concatenate(bs, axis=0), spread(mids), spread(lasts), lasts


def _hgrn_fwd(qr, g, k, v, og, vec):
    T = qr.shape[0]
    nck = CB // CHUNK

    def body(qr_ref, g_ref, k_ref, v_ref, og_ref, vec_ref, out_ref, o_ref, st_ref, state):
        @pl.when(pl.program_id(1) == 0)
        def _():
            state[...] = jnp.zeros_like(state)

        low = _tri(True)
        qv = qr_ref[...].astype(F32)
        q = qv * _sigmoid(qv) * (DK ** -0.5)
        kk = k_ref[...].astype(F32)
        vb = v_ref[...]
        b, mid, last, lasts = _chunk_decay(low, g_ref[...], nck)
        qt = (q * jnp.exp(b - mid)).astype(BF16)
        kt = (kk * jnp.exp(mid - b)).astype(BF16)
        qe = (q * jnp.exp(b)).astype(BF16)
        kd = (kk * jnp.exp(last - b)).astype(BF16)
        intra, grow = [], []
        for c in range(nck):
            r = slice(c * CHUNK, (c + 1) * CHUNK)
            att = jnp.where(low, _nt(qt[r], kt[r]), 0.0).astype(BF16)
            intra.append(_nn(att, vb[r]))
            grow.append(_tn(vb[r], kd[r]))
        st = state[...]
        inter = []
        for c in range(nck):
            stb = st.astype(BF16)
            st_ref[c] = stb
            inter.append(_nt(qe[c * CHUNK:(c + 1) * CHUNK], stb))
            st = st * jnp.exp(lasts[c]) + grow[c]
        state[...] = st
        o = jnp.concatenate(intra, axis=0) + jnp.concatenate(inter, axis=0)
        o_ref[...] = o
        ogv = og_ref[...].astype(F32)
        rms = lax.rsqrt(jnp.mean(o * o, axis=-1, keepdims=True) + EPS)
        out_ref[...] = (o * rms * vec_ref[5:6, :] * (ogv * _sigmoid(ogv))).astype(BF16)

    blk = pl.BlockSpec((CB, DK), lambda h, i: (i, h))
    return pl.pallas_call(
        body, name="hgrn_fwd", grid=(HEADS, T // CB),
        in_specs=[blk, blk, blk, blk, blk, pl.BlockSpec((8, DK), lambda h, i: (0, h))],
        out_specs=[blk, blk, pl.BlockSpec((None, nck, DK, DK), lambda h, i: (h, i, 0, 0))],
        out_shape=[jax.ShapeDtypeStruct((T, D), BF16), jax.ShapeDtypeStruct((T, D), F32),
                   jax.ShapeDtypeStruct((HEADS, T // CHUNK, DK, DK), BF16)],
        scratch_shapes=[pltpu.VMEM((DK, DK), F32)],
        compiler_params=_params("parallel", "arbitrary"),
    )(qr, g, k, v, og, vec)


def _hgrn_bwd(dout, og, qr, g, k, v, o, st, vec):
    T = qr.shape[0]
    nck = CB // CHUNK
    nb = T // CB

    def body(dout_ref, og_ref, qr_ref, g_ref, k_ref, v_ref, o_ref, st_ref, vec_ref,
             dp_ref, acc_ref, dstate):
        @pl.when(pl.program_id(1) == 0)
        def _():
            dstate[...] = jnp.zeros_like(dstate)
            acc_ref[...] = jnp.zeros_like(acc_ref)

        o = o_ref[...]
        ogv = og_ref[...].astype(F32)
        dout = dout_ref[...].astype(F32)
        hg = vec_ref[5:6, :]
        sgo = _sigmoid(ogv)
        rms = lax.rsqrt(jnp.mean(o * o, axis=-1, keepdims=True) + EPS)
        ohat = o * rms
        dp_ref[3] = (dout * (ohat * hg) * _dsilu(ogv, sgo)).astype(BF16)
        don = dout * (ogv * sgo)
        acc_ref[0:1, :] += _colsum(don * ohat)
        dohat = don * hg
        dob = (rms * (dohat - ohat * jnp.mean(dohat * ohat, axis=-1, keepdims=True))).astype(BF16)

        low = _tri(True)
        upp = _tri(False)
        lb = vec_ref[4:5, :]
        qv = qr_ref[...].astype(F32)
        sgq = _sigmoid(qv)
        q = qv * sgq * (DK ** -0.5)
        kk = k_ref[...].astype(F32)
        vb = v_ref[...]
        gv = g_ref[...]
        b, mid, last, lasts = _chunk_decay(low, gv, nck)
        eq = jnp.exp(b - mid)
        ek = jnp.exp(mid - b)
        eb = jnp.exp(b)
        ed = jnp.exp(last - b)
        qtb, ktb, qeb, kdb = ((t).astype(BF16) for t in (q * eq, kk * ek, q * eb, kk * ed))
        rows = [slice(c * CHUNK, (c + 1) * CHUNK) for c in range(nck)]

        dv1, dqt, dkt, dqe, grow = [], [], [], [], []
        for c, r in enumerate(rows):
            att = jnp.where(low, _nt(qtb[r], ktb[r]), 0.0).astype(BF16)
            datt = jnp.where(low, _nt(dob[r], vb[r]), 0.0).astype(BF16)
            dv1.append(_tn(att, dob[r]))
            dqt.append(_nn(datt, ktb[r]))
            dkt.append(_tn(datt, qtb[r]))
            dqe.append(_nn(dob[r], st_ref[c]))
            grow.append(_tn(dob[r], qeb[r]))
        ds = dstate[...]
        ds1b, dl_state = [None] * nck, [None] * nck
        for c in reversed(range(nck)):
            el = jnp.exp(lasts[c])
            ds1b[c] = ds.astype(BF16)
            dl_state[c] = el * _colsum(ds * st_ref[c].astype(F32))
            ds = ds * el + grow[c]
        dstate[...] = ds
        dkd = jnp.concatenate([_nn(vb[r], ds1b[c]) for c, r in enumerate(rows)], axis=0)
        dv = jnp.concatenate(dv1, axis=0) + jnp.concatenate([_nt(kdb[r], ds1b[c]) for c, r in enumerate(rows)], axis=0)
        dqt, dkt, dqe = (jnp.concatenate(t, axis=0) for t in (dqt, dkt, dqe))
        dq = dqt * eq + dqe * eb
        dk = dkt * ek + dkd * ed
        dkdkd = dkd * kdb.astype(F32)
        db = dqt * qtb.astype(F32) - dkt * ktb.astype(F32) + dqe * qeb.astype(F32) - dkdkd
        dg = jnp.concatenate([_cumsum_rows(upp, db[r]) + (_colsum(dkdkd[r]) + dl_state[c])
                              for c, r in enumerate(rows)], axis=0)
        fg = jnp.exp(gv)
        dfg = dg * jnp.exp(-gv) - dk
        one_m_sig = (1.0 - fg) * (1.0 / (1.0 - lb))
        dp_ref[0] = (dq * (DK ** -0.5) * _dsilu(qv, sgq)).astype(BF16)
        dp_ref[1] = (dfg * (fg - lb) * one_m_sig).astype(BF16)
        dp_ref[2] = dv.astype(BF16)
        dlb = _colsum(dfg * one_m_sig) * (lb * (1.0 - lb))
        acc_ref[1:2, :] += dlb
        acc_ref[2:3, :] -= dlb

    blk = pl.BlockSpec((CB, DK), lambda h, i: (nb - 1 - i, h))
    return pl.pallas_call(
        body, name="hgrn_bwd", grid=(HEADS, nb),
        in_specs=[blk, blk, blk, blk, blk, blk, blk,
                  pl.BlockSpec((None, nck, DK, DK), lambda h, i: (h, nb - 1 - i, 0, 0)),
                  pl.BlockSpec((8, DK), lambda h, i: (0, h))],
        out_specs=[pl.BlockSpec((4, CB, DK), lambda h, i: (0, nb - 1 - i, h)),
                   pl.BlockSpec((8, DK), lambda h, i: (0, h))],
        out_shape=[jax.ShapeDtypeStruct((4, T, D), BF16), jax.ShapeDtypeStruct((8, D), F32)],
        scratch_shapes=[pltpu.VMEM((DK, DK), F32)],
        compiler_params=_params("parallel", "arbitrary"),
    )(dout, og, qr, g, k, v, o, st, vec)


def _ln_fwd(uc, lg, lbias):
    mu = jnp.mean(uc, axis=-1, keepdims=True)
    xc = uc - mu
    rstd = lax.rsqrt(jnp.mean(xc * xc, axis=-1, keepdims=True) + EPS)
    z = xc * rstd
    return rstd, z, z * lg + lbias


LANES = 128
SUBLANES = 8
CONV_ROWS = 64


def _lane_tiles():
    return [slice(l * LANES, (l + 1) * LANES) for l in range(D // LANES)]


def _row_shifts(x):
    n = x.shape[0]
    return [x] + [pltpu.roll(x, n - r, axis=0) for r in range(1, SUBLANES)]


TAPS_PAST = tuple(HALO - (CONV_K - 1) + j for j in range(CONV_K))
TAPS_AHEAD = tuple(CONV_K - 1 - j for j in range(CONV_K))


def _tap_windows(shifted, starts, r0, rows):
    for r in range(SUBLANES):
        taps = [(j, s // SUBLANES) for j, s in enumerate(starts) if s % SUBLANES == r]
        if not taps:
            continue
        lo = min(a for _, a in taps)
        hi = max(a for _, a in taps)
        span = shifted[r][r0 + lo * SUBLANES:r0 + hi * SUBLANES + rows]
        for j, a in taps:
            yield j, span[(a - lo) * SUBLANES:(a - lo) * SUBLANES + rows]


def _conv_fwd(u, cw, cvec):
    T = u.shape[0]
    per = TB // HALO

    def body(u_ref, halo_ref, cw_ref, cvec_ref, us_ref, uc_ref, pad):
        i = pl.program_id(0)
        pad[0:HALO, :] = jnp.where(i > 0, halo_ref[...], 0.0)
        pad[HALO:, :] = u_ref[...]
        for lanes in _lane_tiles():
            shifted = _row_shifts(pad[:, lanes])
            taps = cw_ref[:, lanes]
            for r0 in range(0, TB, CONV_ROWS):
                acc = jnp.broadcast_to(cvec_ref[0:1, lanes], (CONV_ROWS, LANES))
                for j, window in _tap_windows(shifted, TAPS_PAST, r0, CONV_ROWS):
                    acc = acc + taps[j:j + 1] * window
                uc_ref[r0:r0 + CONV_ROWS, lanes] = acc
        _, _, ul = _ln_fwd(uc_ref[...], cvec_ref[1:2, :], cvec_ref[2:3, :])
        us_ref[...] = (ul * _sigmoid(ul)).astype(BF16)

    row = pl.BlockSpec((TB, D), lambda i: (i, 0))
    return pl.pallas_call(
        body, name="conv_fwd", grid=(T // TB,),
        in_specs=[row, pl.BlockSpec((HALO, D), lambda i: (jnp.maximum(i * per - 1, 0), 0)),
                  pl.BlockSpec((32, D), lambda i: (0, 0)), pl.BlockSpec((8, D), lambda i: (0, 0))],
        out_specs=[row, row],
        out_shape=[jax.ShapeDtypeStruct((T, D), BF16), jax.ShapeDtypeStruct((T, D), F32)],
        scratch_shapes=[pltpu.VMEM((TB + HALO, D), F32)],
        compiler_params=_params("parallel"),
    )(u, u, cw, cvec)


def _conv_bwd_taps(duc, u, ua, ub, cw):
    T = u.shape[0]
    per = TB // HALO
    nblk = T // TB

    def body(duc_ref, dnext_ref, u_ref, uprev_ref, ua_ref, ub_ref, cw_ref, dp_ref, dcw_ref, upad, dpad, dcw):
        i = pl.program_id(0)

        @pl.when(i == 0)
        def _():
            dcw[...] = jnp.zeros_like(dcw)

        upad[0:HALO, :] = jnp.where(i > 0, uprev_ref[...], 0.0)
        upad[HALO:, :] = u_ref[...]
        dpad[0:TB, :] = duc_ref[...]
        dpad[TB:, :] = jnp.where(i < nblk - 1, dnext_ref[...], 0.0)
        for lanes in _lane_tiles():
            ushift = _row_shifts(upad[:, lanes])
            dshift = _row_shifts(dpad[:, lanes])
            for r0 in range(0, TB, CONV_ROWS):
                rows = slice(r0, r0 + CONV_ROWS)
                duc = duc_ref[rows, lanes]
                for j, window in _tap_windows(ushift, TAPS_PAST, r0, CONV_ROWS):
                    prod = duc * window
                    dcw[j, :, lanes] += jnp.sum(prod.reshape(CONV_ROWS // SUBLANES, SUBLANES, LANES), axis=0)
                du = jnp.zeros((CONV_ROWS, LANES), F32)
                for j, window in _tap_windows(dshift, TAPS_AHEAD, r0, CONV_ROWS):
                    du = du + cw_ref[j:j + 1, lanes] * window
                ua = ua_ref[rows, lanes].astype(F32)
                sg = _sigmoid(ub_ref[rows, lanes].astype(F32))
                dp_ref[0, rows, lanes] = (du * sg).astype(BF16)
                dp_ref[1, rows, lanes] = (du * ua * sg * (1.0 - sg)).astype(BF16)

        @pl.when(i == nblk - 1)
        def _():
            dcw_ref[...] = jnp.sum(dcw[...], axis=1)

    row = pl.BlockSpec((TB, D), lambda i: (i, 0))
    return pl.pallas_call(
        body, name="conv_bwd_taps", grid=(nblk,),
        in_specs=[row, pl.BlockSpec((HALO, D), lambda i: (jnp.minimum((i + 1) * per, T // HALO - 1), 0)),
                  row, pl.BlockSpec((HALO, D), lambda i: (jnp.maximum(i * per - 1, 0), 0)),
                  row, row, pl.BlockSpec((32, D), lambda i: (0, 0))],
        out_specs=[pl.BlockSpec((2, TB, D), lambda i: (0, i, 0)), pl.BlockSpec((32, D), lambda i: (0, 0))],
        out_shape=[jax.ShapeDtypeStruct((2, T, D), BF16), jax.ShapeDtypeStruct((32, D), F32)],
        scratch_shapes=[pltpu.VMEM((TB + HALO, D), F32), pltpu.VMEM((TB + HALO, D), F32),
                        pltpu.VMEM((32, SUBLANES, D), F32)],
        compiler_params=_params("arbitrary"),
    )(duc, duc, u, u, ua, ub, cw)


def _merge_fwd(x, oa, us, sa, sb, vec, w_ho, w_co, w_mo):
    T = x.shape[0]

    def body(x_ref, oa_ref, us_ref, sa_ref, sb_ref, vec_ref, who_hbm, wco_hbm, wmo_hbm,
             xo_ref, ya_ref, yb_ref, mg_ref, mo_ref, who, wco, wmo):
        @pl.when(pl.program_id(0) == 0)
        def _():
            pltpu.sync_copy(who_hbm, who)
            pltpu.sync_copy(wco_hbm, wco)
            pltpu.sync_copy(wmo_hbm, wmo)

        ya = _nn(oa_ref[...], who[...])
        yb = _nn(us_ref[...], wco[...])
        mg = (sa_ref[...].astype(F32) * ya + sb_ref[...].astype(F32) * yb).astype(BF16)
        mo = _nn(mg, wmo[...])
        xo_ref[...] = x_ref[...] + vec_ref[2:3, :] * mo
        ya_ref[...] = ya.astype(BF16)
        yb_ref[...] = yb.astype(BF16)
        mg_ref[...] = mg
        mo_ref[...] = mo.astype(BF16)

    row = pl.BlockSpec((TB, D), lambda i: (i, 0))
    bf = jax.ShapeDtypeStruct((T, D), BF16)
    wv = pltpu.VMEM((D, D), BF16)
    return pl.pallas_call(
        body, name="merge_fwd", grid=(T // TB,),
        in_specs=[row, row, row, row, row, pl.BlockSpec((8, D), lambda i: (0, 0)), ANY, ANY, ANY],
        out_specs=[row] * 5,
        out_shape=[jax.ShapeDtypeStruct((T, D), F32), bf, bf, bf, bf],
        scratch_shapes=[wv, wv, wv],
        compiler_params=_params("arbitrary"),
    )(x, oa, us, sa, sb, vec, w_ho, w_co, w_mo)


def _merge_bwd(dxo, mo, ya, yb, sa, sb, uc, vec, cvec, w_ho, w_co, w_mo):
    T = dxo.shape[0]

    def body(dxo_ref, mo_ref, ya_ref, yb_ref, sa_ref, sb_ref, uc_ref, vec_ref, cvec_ref, who_hbm, wco_hbm, wmo_hbm,
             dmo_ref, dya_ref, dyb_ref, doa_ref, duc_ref, dp_ref, acc_ref, cacc_ref, who, wco, wmo):
        @pl.when(pl.program_id(0) == 0)
        def _():
            pltpu.sync_copy(who_hbm, who)
            pltpu.sync_copy(wco_hbm, wco)
            pltpu.sync_copy(wmo_hbm, wmo)
            acc_ref[...] = jnp.zeros_like(acc_ref)
            cacc_ref[...] = jnp.zeros_like(cacc_ref)

        dxo = dxo_ref[...]
        acc_ref[2:3, :] += _colsum(mo_ref[...].astype(F32) * dxo)
        dmo = (vec_ref[2:3, :] * dxo).astype(BF16)
        dmo_ref[...] = dmo
        dmg = _nt(dmo, wmo[...])
        sa = sa_ref[...].astype(F32)
        sb = sb_ref[...].astype(F32)
        dya = (sa * dmg).astype(BF16)
        dyb = (sb * dmg).astype(BF16)
        dya_ref[...] = dya
        dyb_ref[...] = dyb
        dp_ref[0] = (dmg * ya_ref[...].astype(F32) * sa * (1.0 - sa)).astype(BF16)
        dp_ref[1] = (dmg * yb_ref[...].astype(F32) * sb * (1.0 - sb)).astype(BF16)
        doa_ref[...] = _nt(dya, who[...]).astype(BF16)
        dus = _nt(dyb, wco[...])
        lg = cvec_ref[1:2, :]
        rstd, z, ul = _ln_fwd(uc_ref[...], lg, cvec_ref[2:3, :])
        dul = dus * _dsilu(ul, _sigmoid(ul))
        cacc_ref[1:2, :] += _colsum(dul * z)
        cacc_ref[2:3, :] += _colsum(dul)
        dz = dul * lg
        duc = rstd * (dz - jnp.mean(dz, axis=-1, keepdims=True) - z * jnp.mean(dz * z, axis=-1, keepdims=True))
        cacc_ref[0:1, :] += _colsum(duc)
        duc_ref[...] = duc

    row = pl.BlockSpec((TB, D), lambda i: (i, 0))
    one = pl.BlockSpec((None, TB, D), lambda i: (0, i, 0))
    vec8 = pl.BlockSpec((8, D), lambda i: (0, 0))
    bf = jax.ShapeDtypeStruct((T, D), BF16)
    bf1 = jax.ShapeDtypeStruct((1, T, D), BF16)
    acc = jax.ShapeDtypeStruct((8, D), F32)
    wv = pltpu.VMEM((D, D), BF16)
    return pl.pallas_call(
        body, name="merge_bwd", grid=(T // TB,),
        in_specs=[row, row, row, row, row, row, row, vec8, vec8, ANY, ANY, ANY],
        out_specs=[one, one, one, row, row, pl.BlockSpec((2, TB, D), lambda i: (0, i, 0)), vec8, vec8],
        out_shape=[bf1, bf1, bf1, bf, jax.ShapeDtypeStruct((T, D), F32), jax.ShapeDtypeStruct((2, T, D), BF16), acc, acc],
        scratch_shapes=[wv, wv, wv],
        compiler_params=_params("arbitrary"),
    )(dxo, mo, ya, yb, sa, sb, uc, vec, cvec, w_ho, w_co, w_mo)


def _pack_rows(parts, total, name, slot=None):
    def body(*refs):
        out = refs[-1]
        out[...] = jnp.zeros_like(out)
        for ref, (_, src, n, dst) in zip(refs[-1 - len(parts):-1], parts):
            out[dst:dst + n, :] = ref[src:src + n, :]

    arrs = [p[0] for p in parts]
    if slot is None:
        return pl.pallas_call(
            body, name=name, in_specs=[pl.BlockSpec(a.shape, lambda: (0, 0)) for a in arrs],
            out_specs=pl.BlockSpec((total, D), lambda: (0, 0)),
            out_shape=jax.ShapeDtypeStruct((total, D), F32),
        )(*arrs)
    return pl.pallas_call(
        body, name=name,
        grid_spec=pltpu.PrefetchScalarGridSpec(
            num_scalar_prefetch=1, grid=(1,),
            in_specs=[pl.BlockSpec(a.shape, lambda i, s: (0, 0)) for a in arrs],
            out_specs=pl.BlockSpec((None, total, D), lambda i, s: (s[0], 0, 0))),
        out_shape=jax.ShapeDtypeStruct((8, total, D), F32),
    )(slot, *arrs)


PACK_ROWS = 56
PACK_AT = {"ada_b": 0, "loss": 9, "norm_ffn1": 10, "norm_mix": 11, "hgrn_g": 12, "conv_b": 13, "conv_ln_g": 14,
           "conv_ln_b": 15, "norm_ffn2": 16, "norm_final": 17, "hgrn_lb": 18, "conv_w": 20}


def _local_step(x, tgt, mod, small, kc, weight, reduce, reduce_small):
    lb = jax.nn.sigmoid(small["hgrn_lb"][0:1] - small["hgrn_lb"][1:2])
    vec1 = _pack_rows([(mod, 0, 3, 0), (small["norm_ffn1"], 0, 1, 3)], 8, "pack_vec1")
    vec2 = _pack_rows([(mod, 3, 3, 0), (small["norm_mix"], 0, 1, 3), (lb, 0, 1, 4), (small["hgrn_g"], 0, 1, 5)],
                      8, "pack_vec2")
    vec3 = _pack_rows([(mod, 6, 3, 0), (small["norm_ffn2"], 0, 1, 3)], 8, "pack_vec3")
    cvec = _pack_rows([(small["conv_b"], 0, 1, 0), (small["conv_ln_g"], 0, 1, 1), (small["conv_ln_b"], 0, 1, 2)],
                      8, "pack_cvec")
    cw = small["conv_w"]
    gvec = _pack_rows([(small["norm_final"], 0, 1, 0)], 8, "pack_gvec")

    wg = {n: weight(n, (vec1, vec2, vec3, cvec, gvec)) for n in ("ffn1_w_in", "ffn1_w_out")}
    x1, h1, a1, b1, s1, f1 = _ffn_fwd(x, vec1, wg["ffn1_w_in"], wg["ffn1_w_out"], "ffn1_fwd")
    wg["mix_w_in"] = weight("mix_w_in", x1)
    h2, qr, g, k, v, og, u, ua, ub, sa, sb = _mix_proj_fwd(x1, vec2, wg["mix_w_in"])
    oa, o, st = _hgrn_fwd(qr, g, k, v, og, vec2)
    us, uc = _conv_fwd(u, cw, cvec)
    wg.update({n: weight(n, us) for n in ("hgrn_w_o", "conv_w_o", "mix_w_out")})
    x2, ya, yb, mg, mo = _merge_fwd(x1, oa, us, sa, sb, vec2, wg["hgrn_w_o"], wg["conv_w_o"], wg["mix_w_out"])
    wg.update({n: weight(n, x2) for n in ("ffn2_w_in", "ffn2_w_out")})
    dx3, h3, a3, b3, s3, f3, acc_head = _ffn_fwd(x2, vec3, wg["ffn2_w_in"], wg["ffn2_w_out"], "ffn2_fwd",
                                                 head=(tgt, gvec))

    dx2, df3, dab3, acc3 = _ffn_bwd(dx3, x2, vec3, a3, b3, f3, wg["ffn2_w_in"], wg["ffn2_w_out"], "ffn2_bwd")
    tok = reduce(("ffn2_w_out", "ffn2_w_in"), [_mm_tn(s3, df3, 1, kc, DFF // NCHIP, "ffn2_dwout"),
                                               _mm_tn(h3, dab3, 1, kc, D, "ffn2_dwin")])
    vec2b = vec2 + tok[0:1, 0:1]
    dmo, dya, dyb, doa, duc, dpc, acc_m, acc_c = _merge_bwd(dx2, mo, ya, yb, sa, sb, uc, vec2b, cvec,
                                                            wg["hgrn_w_o"], wg["conv_w_o"], wg["mix_w_out"])
    tok = reduce(("mix_w_out", "hgrn_w_o", "conv_w_o"),
                 [_mm_tn(mg, dmo, 1, kc, D // NCHIP, "mix_dwout"), _mm_tn(oa, dya, 1, kc, D // NCHIP, "hgrn_dwo"),
                  _mm_tn(us, dyb, 1, kc, D // NCHIP, "conv_dwo")])
    vec2c = vec2 + tok[0:1, 0:1]
    dpb, dcw = _conv_bwd_taps(duc, u, ua, ub, cw)
    dpa, acc_h = _hgrn_bwd(doa, og, qr, g, k, v, o, st, vec2c)
    dx1, acc2 = _mix_proj_bwd(dx2, x1, vec2c, dpa, dpb, dpc, wg["mix_w_in"])
    gmix = _mm_tn(h2, dpa, 2, kc, D, "mix_dwin_a", slabs=NCHIP)
    gmix = _mm_tn(h2, dpb, 2, kc, D, "mix_dwin_b", into=gmix, slab=2, slabs=NCHIP)
    gmix = _mm_tn(h2, dpc, 2, kc, D, "mix_dwin_c", into=gmix, slab=3, slabs=NCHIP)
    tok = reduce(("mix_w_in",), [gmix])
    vec1b = vec1 + tok[0:1, 0:1]
    dx0, df1, dab1, acc1 = _ffn_bwd(dx1, x, vec1b, a1, b1, f1, wg["ffn1_w_in"], wg["ffn1_w_out"], "ffn1_bwd")

    at = PACK_AT
    finish_small = reduce_small([
        (acc1, 0, 3, at["ada_b"]), (acc2, 0, 2, at["ada_b"] + 3), (acc_m, 2, 1, at["ada_b"] + 5),
        (acc3, 0, 3, at["ada_b"] + 6), (acc_head, 1, 1, at["loss"]), (acc1, 3, 1, at["norm_ffn1"]),
        (acc2, 3, 1, at["norm_mix"]), (acc_h, 0, 1, at["hgrn_g"]), (acc_c, 0, 3, at["conv_b"]),
        (acc3, 3, 1, at["norm_ffn2"]), (acc_head, 0, 1, at["norm_final"]), (acc_h, 1, 2, at["hgrn_lb"]),
        (dcw, 0, CONV_K, at["conv_w"])])
    finish_small, tok = finish_small
    last = [_mm_tn(s1, df1, 1, kc, DFF // NCHIP, "ffn1_dwout", after=tok),
            _mm_tn(h1, dab1, 1, kc, D, "ffn1_dwin", after=tok)]
    reduce(("ffn1_w_out", "ffn1_w_in"), last, finish_small(last[1][0]))
    return dx0


BLOCK_BYTES = 5 * 512 * 1024


def _row_block(rows, cols):
    for br in (512, 352, 256, 176, 128, 64, 32, 16, 8):
        if rows % br == 0 and br * cols * 4 <= BLOCK_BYTES:
            return br
    return rows


def _cast_into_slot(w, kc, name, after):
    R, C = w.shape
    br = _row_block(R, C)

    def body(kc_ref, w_ref, after_ref, o_ref):
        o_ref[...] = w_ref[...].astype(BF16)

    return pl.pallas_call(
        body, name=name,
        grid_spec=pltpu.PrefetchScalarGridSpec(
            num_scalar_prefetch=1, grid=(R // br,),
            in_specs=[pl.BlockSpec((br, C), lambda i, kc: (i, 0)), ANY],
            out_specs=pl.BlockSpec((None, br, C), lambda i, kc: (kc[0], i, 0))),
        out_shape=jax.ShapeDtypeStruct((NCHIP, R, C), BF16), compiler_params=_params("parallel"),
    )(kc, w, after)


def _adamw_math(w, g, m, v):
    nm = ADAM_B1 * m + (1.0 - ADAM_B1) * g
    nv = ADAM_B2 * v + (1.0 - ADAM_B2) * (g * g)
    m_hat = nm / (1.0 - ADAM_B1 ** ADAM_STEP)
    v_hat = nv / (1.0 - ADAM_B2 ** ADAM_STEP)
    return -ADAM_LR * (m_hat / (jnp.sqrt(v_hat) + ADAM_EPS) + ADAM_WD * w), nm, nv


def _adamw_rows(params, gsum, rows, after, name):
    n = len(params)

    def body(*refs):
        g_ref = refs[3 * n]
        outs = refs[3 * n + 2:]
        for i, r0 in enumerate(rows):
            w_ref, m_ref, v_ref = refs[3 * i:3 * i + 3]
            g = g_ref[r0:r0 + w_ref.shape[0], :]
            outs[4 * i][...] = g
            outs[4 * i + 1][...], outs[4 * i + 2][...], outs[4 * i + 3][...] = _adamw_math(
                w_ref[...], g, m_ref[...], v_ref[...])

    flat = [a for p in params for a in p]
    full = lambda a: pl.BlockSpec(a.shape, lambda: (0, 0))
    out = pl.pallas_call(
        body, name=name, in_specs=[full(a) for a in flat] + [full(gsum), ANY],
        out_specs=[full(p[0]) for p in params for _ in range(4)],
        out_shape=[jax.ShapeDtypeStruct(p[0].shape, F32) for p in params for _ in range(4)],
    )(*flat, gsum, after)
    return [out[4 * i:4 * i + 4] for i in range(n)]


def _adamw(w, g, m, v, name, after=None, copy_grad=False):
    R, C = w.shape
    br = _row_block(R, C)
    extra = [] if after is None else [after]
    nout = 4 if copy_grad else 3

    def body(w_ref, g_ref, m_ref, v_ref, *rest):
        d_ref, nm_ref, nv_ref = rest[-nout:][:3]
        gv = g_ref[...]
        if copy_grad:
            rest[-1][...] = gv
        d_ref[...], nm_ref[...], nv_ref[...] = _adamw_math(w_ref[...], gv, m_ref[...], v_ref[...])

    blk = pl.BlockSpec((br, C), lambda i: (i, 0))
    out = jax.ShapeDtypeStruct((R, C), F32)
    return pl.pallas_call(
        body, name=name, grid=(R // br,), in_specs=[blk] * 4 + [ANY] * len(extra), out_specs=[blk] * nout,
        out_shape=[out] * nout, compiler_params=_params("parallel"),
    )(w, g, m, v, *extra)


def _coords():
    return lax.axis_index("x"), lax.axis_index("y"), lax.axis_index("c")


def _flip(v, bit):
    return 1 - v if bit else v


HBM = pl.BlockSpec(memory_space=pltpu.HBM)
SEM = pl.BlockSpec(memory_space=pltpu.SEMAPHORE)
EFFECT = pltpu.SideEffectType.DATAFLOW_SIDE_EFFECTING


def _peer8(x, y, c, m):
    px, py, pc = _flip(x, m & 4), _flip(y, m & 2), _flip(c, m & 1)
    return (px, py, pc), 4 * px + 2 * py + pc


def _allgather8_start(blocks, name):
    def body(b_ref, send, recv, thru, token):
        x, y, c = _coords()
        me = 4 * x + 2 * y + c
        for m in range(1, 8):
            peer, _ = _peer8(x, y, c, m)
            pltpu.make_async_remote_copy(src_ref=b_ref.at[me], dst_ref=b_ref.at[me], send_sem=send.at[m - 1],
                                         recv_sem=recv.at[m - 1], device_id=peer, device_id_type=MESH).start()
        token[...] = jnp.zeros_like(token)

    sem = pltpu.SemaphoreType.DMA((7,))
    return pl.pallas_call(
        body, name=name,
        out_shape=[sem, sem, pltpu.HBM(blocks.shape, blocks.dtype), jax.ShapeDtypeStruct((8, 128), F32)],
        in_specs=[HBM], out_specs=[SEM, SEM, HBM, pl.BlockSpec(memory_space=pltpu.VMEM)], input_output_aliases={0: 2},
        compiler_params=pltpu.CompilerParams(has_side_effects=EFFECT),
    )(pltpu.with_memory_space_constraint(blocks, pltpu.HBM))


def _allgather8_wait(blocks, send_sem, recv_sem, after, name):
    def body(b_ref, send, recv, after_ref, thru):
        x, y, c = _coords()
        me = 4 * x + 2 * y + c
        for m in range(1, 8):
            peer, sender = _peer8(x, y, c, m)
            cp = pltpu.make_async_remote_copy(src_ref=b_ref.at[me], dst_ref=b_ref.at[sender], send_sem=send.at[m - 1],
                                              recv_sem=recv.at[m - 1], device_id=peer, device_id_type=MESH)
            cp.wait_send()
            cp.wait_recv()

    return pl.pallas_call(
        body, name=name, out_shape=pltpu.HBM(blocks.shape, blocks.dtype),
        in_specs=[HBM, SEM, SEM, ANY], out_specs=HBM, input_output_aliases={0: 0},
        compiler_params=pltpu.CompilerParams(has_side_effects=EFFECT),
    )(blocks, send_sem, recv_sem, after)


def _chip_peer(x, y, m):
    px, py = _flip(x, m & 2), _flip(y, m & 1)
    return px, py, 2 * px + py


def _core_rows(land, c):
    half = land.shape[1] // 2
    return pl.ds(pl.multiple_of(c * half, 16), half)


def _gather_start(lands, groups, halved, after, name):
    n, ng, na = len(lands), len(groups), len(after)

    def body(*refs):
        ins = refs[:n]
        sends, recvs = refs[n + na:n + na + ng], refs[n + na + ng:n + na + 2 * ng]
        token = refs[n + na + 2 * ng + n]
        x, y, c = _coords()
        k = 2 * x + y
        for gi, grp in enumerate(groups):
            for j, t in enumerate(grp):
                mine = ins[t].at[k, _core_rows(ins[t], c), :] if halved[gi] else ins[t].at[k]
                for m in (1, 2, 3):
                    px, py, _ = _chip_peer(x, y, m)
                    pltpu.make_async_remote_copy(
                        src_ref=mine, dst_ref=mine, send_sem=sends[gi].at[3 * j + m - 1],
                        recv_sem=recvs[gi].at[3 * j + m - 1], device_id=(px, py, c), device_id_type=MESH).start()
        token[...] = jnp.zeros_like(token)

    sems = [pltpu.SemaphoreType.DMA((3 * len(g),)) for g in groups]
    out = pl.pallas_call(
        body, name=name,
        out_shape=sems + sems + [pltpu.HBM(a.shape, a.dtype) for a in lands] + [jax.ShapeDtypeStruct((8, 128), F32)],
        in_specs=[HBM] * n + [ANY] * na,
        out_specs=[SEM] * (2 * ng) + [HBM] * n + [pl.BlockSpec(memory_space=pltpu.VMEM)],
        input_output_aliases={t: 2 * ng + t for t in range(n)},
        compiler_params=pltpu.CompilerParams(has_side_effects=EFFECT),
    )(*[pltpu.with_memory_space_constraint(a, pltpu.HBM) for a in lands], *after)
    return out[:ng], out[ng:2 * ng], out[2 * ng:2 * ng + n], out[2 * ng + n]


def _gather_wait(lands, halved, send_sem, recv_sem, after, name):
    n = len(lands)

    def body(*refs):
        ins, send, recv = refs[:n], refs[n], refs[n + 1]
        x, y, c = _coords()
        k = 2 * x + y
        for j in range(n):
            rows = _core_rows(ins[j], c)
            for m in (1, 2, 3):
                px, py, pk = _chip_peer(x, y, m)
                cp = pltpu.make_async_remote_copy(
                    src_ref=ins[j].at[k, rows, :] if halved else ins[j].at[k],
                    dst_ref=ins[j].at[pk, rows, :] if halved else ins[j].at[pk], send_sem=send.at[3 * j + m - 1],
                    recv_sem=recv.at[3 * j + m - 1], device_id=(px, py, c), device_id_type=MESH)
                cp.wait_send()
                cp.wait_recv()

    return pl.pallas_call(
        body, name=name, out_shape=[pltpu.HBM(a.shape, a.dtype) for a in lands],
        in_specs=[HBM] * n + [SEM, SEM] + [ANY] * len(after), out_specs=[HBM] * n,
        input_output_aliases={j: j for j in range(n)},
        compiler_params=pltpu.CompilerParams(has_side_effects=EFFECT),
    )(*lands, send_sem, recv_sem, *after)


def _sibling_fill(lands, name):
    n = len(lands)

    def body(*refs):
        ins = refs[:n]
        send_sems, recv_sems = refs[2 * n:]
        x, y, c = _coords()
        sends, recvs = [], []
        for t in range(n):
            for m in (1, 2, 3):
                _, _, pk = _chip_peer(x, y, m)
                for rows, lst in ((_core_rows(ins[t], c), sends), (_core_rows(ins[t], 1 - c), recvs)):
                    lst.append(pltpu.make_async_remote_copy(
                        src_ref=ins[t].at[pk, rows, :], dst_ref=ins[t].at[pk, rows, :],
                        send_sem=send_sems.at[3 * t + m - 1], recv_sem=recv_sems.at[3 * t + m - 1],
                        device_id=(x, y, 1 - c), device_id_type=MESH))
        for cp in sends:
            cp.start()
        for cp in recvs:
            cp.wait_recv()
        for cp in sends:
            cp.wait_send()

    return pl.pallas_call(
        body, name=name, in_specs=[ANY] * n, out_specs=[ANY] * n,
        out_shape=[jax.ShapeDtypeStruct(a.shape, a.dtype) for a in lands],
        input_output_aliases={t: t for t in range(n)},
        scratch_shapes=[pltpu.SemaphoreType.DMA((3 * n,)), pltpu.SemaphoreType.DMA((3 * n,))],
    )(*lands)


def _scatter_start(srcs, name, after=()):
    n, na = len(srcs), len(after)

    def body(*refs):
        ins, lands = refs[:n], refs[n:2 * n]
        send, recv = refs[2 * n + na], refs[2 * n + na + 1]
        token = refs[2 * n + na + 2 + 2 * n]
        x, y, c = _coords()
        k = 2 * x + y
        for t in range(n):
            for m in (1, 2, 3):
                px, py, pk = _chip_peer(x, y, m)
                pltpu.make_async_remote_copy(
                    src_ref=ins[t].at[pk], dst_ref=lands[t].at[k], send_sem=send.at[3 * t + m - 1],
                    recv_sem=recv.at[3 * t + m - 1], device_id=(px, py, c), device_id_type=MESH).start()
        token[...] = jnp.zeros_like(token)

    sem = pltpu.SemaphoreType.DMA((3 * n,))
    hbm = [pltpu.HBM(a.shape, a.dtype) for a in srcs]
    operands = list(srcs) + [lax.empty(a.shape, a.dtype) for a in srcs]
    out = pl.pallas_call(
        body, name=name, out_shape=[sem, sem] + hbm + hbm + [jax.ShapeDtypeStruct((8, 128), F32)],
        in_specs=[HBM] * (2 * n) + [ANY] * na,
        out_specs=[SEM, SEM] + [HBM] * (2 * n) + [pl.BlockSpec(memory_space=pltpu.VMEM)],
        input_output_aliases={t: 2 + t for t in range(2 * n)},
        compiler_params=pltpu.CompilerParams(has_side_effects=EFFECT),
    )(*[pltpu.with_memory_space_constraint(a, pltpu.HBM) for a in operands], *after)
    return out[0], out[1], out[2:2 + n], out[2 + n:2 + 2 * n], out[2 + 2 * n]


def _scatter_wait(srcs, lands, send_sem, recv_sem, after, name):
    n = len(srcs)

    def body(*refs):
        ins, land = refs[:n], refs[n:2 * n]
        send, recv = refs[2 * n], refs[2 * n + 1]
        x, y, c = _coords()
        for t in range(n):
            for m in (1, 2, 3):
                px, py, pk = _chip_peer(x, y, m)
                cp = pltpu.make_async_remote_copy(
                    src_ref=ins[t].at[pk], dst_ref=land[t].at[pk], send_sem=send.at[3 * t + m - 1],
                    recv_sem=recv.at[3 * t + m - 1], device_id=(px, py, c), device_id_type=MESH)
                cp.wait_send()
                cp.wait_recv()

    hbm = [pltpu.HBM(a.shape, a.dtype) for a in srcs]
    out = pl.pallas_call(
        body, name=name, out_shape=hbm + hbm, in_specs=[HBM] * (2 * n) + [SEM, SEM, ANY], out_specs=[HBM] * (2 * n),
        input_output_aliases={t: t for t in range(2 * n)},
        compiler_params=pltpu.CompilerParams(has_side_effects=EFFECT),
    )(*srcs, *lands, send_sem, recv_sem, after)
    return out[:n], out[n:]


def _sum_own_half(g, ra, kc, name):
    _, R, C = g.shape
    half = R // 2
    br = _row_block(half, C)
    nb = half // br

    def body(kc_ref, g_ref, ra_ref, o_ref):
        o_ref[...] = (g_ref[...] + ra_ref[...].astype(F32)).astype(BF16)

    return pl.pallas_call(
        body, name=name,
        grid_spec=pltpu.PrefetchScalarGridSpec(
            num_scalar_prefetch=1, grid=(NCHIP, nb),
            in_specs=[pl.BlockSpec((None, br, C), lambda j, i, kc: (j, kc[1] * nb + i, 0)),
                      pl.BlockSpec((None, br, C), lambda j, i, kc: (j, i, 0))],
            out_specs=pl.BlockSpec((None, br, C), lambda j, i, kc: (j, i, 0))),
        out_shape=jax.ShapeDtypeStruct((NCHIP, half, C), BF16),
        compiler_params=_params("parallel", "parallel"),
    )(kc, g, ra)


def _sum_chips(sa, rb, kc, name, after=None):
    _, half, C = rb.shape
    br = _row_block(half, C)
    nb = half // br
    extra = [] if after is None else [after]

    def body(kc_ref, own_ref, r1_ref, r2_ref, r3_ref, *rest):
        out, obuf, local_sems, send_sems, recv_sem = rest[-5:]
        i = pl.program_id(0)
        slot = i % 2
        x, y, c = _coords()

        def copies(i_, slot_):
            rows = out.at[pl.ds(pl.multiple_of((c * nb + i_) * br, 8), br), :]
            return (pltpu.make_async_copy(obuf.at[slot_], rows, local_sems.at[slot_]),
                    pltpu.make_async_remote_copy(src_ref=obuf.at[slot_], dst_ref=rows, send_sem=send_sems.at[slot_],
                                                 recv_sem=recv_sem, device_id=(x, y, 1 - c), device_id_type=MESH))

        @pl.when(i >= 2)
        def _():
            here, there = copies(i, slot)
            here.wait()
            there.wait_send()

        acc = own_ref[...].astype(F32) + r1_ref[...].astype(F32)
        obuf[slot] = (acc + r2_ref[...].astype(F32)) + r3_ref[...].astype(F32)
        here, there = copies(i, slot)
        here.start()
        there.start()

        @pl.when(i == nb - 1)
        def _():
            for s in range(min(2, nb)):
                here, there = copies(i, (i - s) % 2)
                here.wait()
                there.wait_send()
            theirs = out.at[pl.ds(pl.multiple_of((1 - c) * half, 8), half), :]
            pltpu.make_async_remote_copy(src_ref=theirs, dst_ref=theirs, send_sem=send_sems.at[0], recv_sem=recv_sem,
                                         device_id=(x, y, 1 - c), device_id_type=MESH).wait_recv()

    def slab(m):
        return pl.BlockSpec((None, br, C), lambda i, kc: (kc[0] ^ m, i, 0))

    return pl.pallas_call(
        body, name=name,
        grid_spec=pltpu.PrefetchScalarGridSpec(
            num_scalar_prefetch=1, grid=(nb,),
            in_specs=[slab(0), slab(1), slab(2), slab(3)] + [ANY] * len(extra),
            out_specs=ANY,
            scratch_shapes=[pltpu.VMEM((2, br, C), F32), pltpu.SemaphoreType.DMA((2,)), pltpu.SemaphoreType.DMA((2,)),
                            pltpu.SemaphoreType.DMA]),
        out_shape=jax.ShapeDtypeStruct((2 * half, C), F32), compiler_params=_params("arbitrary"),
    )(kc, sa, rb, rb, rb, *extra)


def _sum8(ga, name):
    _, R, C = ga.shape

    def body(g_ref, o_ref):
        acc = g_ref[0]
        for j in range(1, 8):
            acc = acc + g_ref[j]
        o_ref[...] = acc

    return pl.pallas_call(
        body, name=name, in_specs=[pl.BlockSpec((8, R, C), lambda: (0, 0, 0))],
        out_specs=pl.BlockSpec((R, C), lambda: (0, 0)), out_shape=jax.ShapeDtypeStruct((R, C), F32),
    )(ga)


ADA_COLS = 9 * D // NCHIP
ADA_BLK = 256


def _ada_mod(c_all, ada_w, ada_b, kme):
    def body(k_ref, c_ref, w_ref, b_ref, o_ref):
        cv = c_ref[...]
        cs = cv * _sigmoid(cv)
        o_ref[...] = jnp.dot(cs, w_ref[...], precision=lax.Precision.HIGHEST,
                             preferred_element_type=F32) + b_ref[...]

    nblk = ADA_COLS // ADA_BLK
    return pl.pallas_call(
        body, name="ada_mod",
        grid_spec=pltpu.PrefetchScalarGridSpec(
            num_scalar_prefetch=1, grid=(nblk,),
            in_specs=[pl.BlockSpec((8, D), lambda j, k: (0, 0)),
                      pl.BlockSpec((D, ADA_BLK), lambda j, k: (0, j)),
                      pl.BlockSpec((1, ADA_BLK), lambda j, k: (0, k[0] * nblk + j))],
            out_specs=pl.BlockSpec((None, 8, ADA_BLK), lambda j, k: (k[1], 0, j))),
        out_shape=jax.ShapeDtypeStruct((8, 8, ADA_COLS), F32),
        compiler_params=_params("parallel"),
    )(kme, c_all, ada_w, ada_b)


def _ada_grad(c_all_t, dmod_all, kidx):
    def body(k_ref, ct_ref, dm_ref, o_ref):
        cv = ct_ref[...]
        cs = cv * _sigmoid(cv)
        acc = cs[:, 0:1] * dm_ref[0:1, :]
        for b in range(1, 8):
            acc = acc + cs[:, b:b + 1] * dm_ref[b:b + 1, :]
        o_ref[...] = acc

    nblk = ADA_COLS // ADA_BLK
    return pl.pallas_call(
        body, name="ada_grad",
        grid_spec=pltpu.PrefetchScalarGridSpec(
            num_scalar_prefetch=1, grid=(nblk,),
            in_specs=[pl.BlockSpec((D, 8), lambda j, k: (0, 0)),
                      pl.BlockSpec((8, ADA_BLK), lambda j, k: (0, k[0] * nblk + j))],
            out_specs=pl.BlockSpec((D, ADA_BLK), lambda j, k: (0, j))),
        out_shape=jax.ShapeDtypeStruct((D, ADA_COLS), F32),
        compiler_params=_params("parallel"),
    )(kidx, c_all_t, dmod_all)


BIG = ("ffn1_w_in", "ffn1_w_out", "mix_w_in", "hgrn_w_o", "conv_w_o", "mix_w_out", "ffn2_w_in", "ffn2_w_out")
ROW_SHARDED = ("ffn1_w_out", "hgrn_w_o", "conv_w_o", "mix_w_out", "ffn2_w_out")
GATHER_GROUPS = ((0, 1), (2,), (3, 4, 5), (6, 7))
GATHER_HALVED = (True, True, False, False)
GATHER_STARTS = ((0,), (1,), (2, 3))
PACK_LEN = {"ada_b": 9, "hgrn_lb": 2}
WEIGHTS = ("ada_w", "ada_b", "norm_ffn1", "ffn1_w_in", "ffn1_w_out", "norm_mix", "mix_w_in", "hgrn_lb", "hgrn_g",
           "hgrn_w_o", "conv_w", "conv_b", "conv_ln_g", "conv_ln_b", "conv_w_o", "mix_w_out", "norm_ffn2",
           "ffn2_w_in", "ffn2_w_out", "norm_final")
PACKED = ("ada_b", "norm_ffn1", "norm_mix", "hgrn_g", "conv_b", "conv_ln_g", "conv_ln_b", "norm_ffn2",
          "norm_final", "hgrn_lb")


def _step(w, m, v, x, c, tgt):
    xi, yi, ci = _coords()
    kidx = (2 * xi + yi).astype(jnp.int32).reshape(1)
    kc = jnp.stack([2 * xi + yi, ci]).astype(jnp.int32)
    me = 4 * xi + 2 * yi + ci

    cq = D // NCHIP
    me32 = me.astype(jnp.int32)
    first = jnp.zeros((40, cq), F32).at[0:CONV_K].set(w["conv_w"][0]).at[32:36].set(c.reshape(NCHIP, cq))
    first = lax.dynamic_update_slice(jnp.zeros((8, 40, cq), F32), first[None], (me32, 0, 0))
    send, recv, first, tok = _allgather8_start(first, "gather_c_conv_w_start")
    early = {t: _cast_into_slot(w[BIG[t]][0], kc, "cast_" + BIG[t], tok) for t in GATHER_GROUPS[0]}
    first_all = _allgather8_wait(first, send, recv, early[GATHER_GROUPS[0][-1]], "gather_c_conv_w_wait")
    c_all = first_all[:, 32:36, :].reshape(8, D)
    mod_cols = _ada_mod(c_all, w["ada_w"][0], w["ada_b"], jnp.stack([kidx[0], me32]))
    send, recv, mod_cols, tok = _allgather8_start(mod_cols, "gather_mod_start")
    first_start = _gather_start([early[t] for t in GATHER_GROUPS[0]], [tuple(range(len(GATHER_GROUPS[0])))],
                                [GATHER_HALVED[0]], [tok], "gather_weights_start0")
    early.update({t: _cast_into_slot(w[BIG[t]][0], kc, "cast_" + BIG[t], first_start[3]) for t in GATHER_GROUPS[1]})
    mod_all = _allgather8_wait(mod_cols, send, recv, early[GATHER_GROUPS[1][-1]], "gather_mod_wait")
    mod = lax.dynamic_slice(mod_all, (0, me, 0), (8, 1, ADA_COLS))[::2].reshape(9, D)
    small = {n: w[n].reshape(-1, D) for n in ("norm_ffn1", "norm_mix", "hgrn_lb", "hgrn_g", "conv_b", "conv_ln_g",
                                              "conv_ln_b", "norm_ffn2", "norm_final")}
    small["conv_w"] = jnp.concatenate([first_all[2 * j, 0:32, :] for j in range(NCHIP)], axis=1)

    lands, sends, recvs = list(first_start[2]), list(first_start[0]), list(first_start[1])
    after = mod
    for part in GATHER_STARTS[1:]:
        tensors = [t for gi in part for t in GATHER_GROUPS[gi]]
        cast = [early[t] if t in early else _cast_into_slot(w[BIG[t]][0], kc, "cast_" + BIG[t], after)
                for t in tensors]
        groups = [tuple(tensors.index(t) for t in GATHER_GROUPS[gi]) for gi in part]
        s, r, thru, after = _gather_start(cast, groups, [GATHER_HALVED[gi] for gi in part], [after],
                                          "gather_weights_start%d" % part[0])
        lands, sends, recvs = lands + list(thru), sends + list(s), recvs + list(r)
    started_all = after
    ready = {}

    def weight(name, after):
        t = BIG.index(name)
        if t not in ready:
            gi = [t in grp for grp in GATHER_GROUPS].index(True)
            grp = GATHER_GROUPS[gi]
            outs = _gather_wait([lands[j] for j in grp], GATHER_HALVED[gi], sends[gi], recvs[gi],
                                list(after) + [started_all] if gi == 0 else [after], "gather_weights_wait%d" % gi)
            if GATHER_HALVED[gi]:
                outs = _sibling_fill(outs, "gather_weights_fill%d" % gi)
            ready.update(zip(grp, outs))
        return ready[t].reshape(-1, D) if name in ROW_SHARDED else ready[t]

    grads, delta, new_m, new_v = {}, {}, {}, {}
    flight = []
    landed = []

    def settle(after):
        names, sa, rb, send, recv = flight.pop()
        sa, rb = _scatter_wait(sa, rb, send, recv, after, "rs_chip_wait_" + names[0])
        landed.append((names, sa, rb))

    def reduce(names, pairs, after=None):
        gs = [g.reshape(NCHIP, -1, g.shape[-1]) for g, _ in pairs]
        ra = [r.reshape(NCHIP, -1, r.shape[-1]) for _, r in pairs]
        sa = [_sum_own_half(g, r, kc, "rs_sum_pair_" + n) for g, r, n in zip(gs, ra, names)]
        if flight:
            settle(sa[0])
        send, recv, sa, rb, tok = _scatter_start(sa, "rs_chip_start_" + names[0], () if after is None else (after,))
        flight.append((names, sa, rb, send, recv))
        started.append(tok)
        return tok

    def adamw(n, after=None):
        shape = w[n].shape
        two = (shape[-2], shape[-1])
        out = _adamw(w[n].reshape(two), grads[n], m[n].reshape(two), v[n].reshape(two), "adamw_" + n, after,
                     copy_grad=n in BIG)
        g_ = out[3] if n in BIG else grads[n]
        grads[n], delta[n], new_m[n], new_v[n] = (a.reshape(shape) for a in (g_, out[0], out[1], out[2]))
        return out[1]

    def finish(after=None):
        names, sa, rb = landed.pop(0)
        full = [_sum_chips(s, r, kc, "rs_sum_chips_" + n, after) for s, r, n in zip(sa, rb, names)]
        grads.update(zip(names, full))
        return [adamw(n) for n in names][-1]

    started = []

    smalls = []

    def reduce_small(parts):
        blocks = _pack_rows(parts, PACK_ROWS, "pack_small_grads", slot=me.astype(jnp.int32).reshape(1))
        send, recv, blocks, tok = _allgather8_start(blocks, "gather_small_grads_start")

        def finish(after):
            packed_all = _allgather8_wait(blocks, send, recv, after, "gather_small_grads_wait")
            smalls.extend([packed_all, _sum8(packed_all, "sum_small_grads")])
            return smalls[1]

        return finish, tok

    dx = _local_step(x[0], tgt[0], mod, small, kc, weight, reduce, reduce_small)
    packed_all, gsum = smalls
    loss = (0.5 / D) * jnp.sum(gsum[PACK_AT["loss"]])
    dmod_all = packed_all[:, 0:9, :].reshape(8, 9 * D)
    grads["ada_w"] = _ada_grad(c_all.T, dmod_all, kidx)
    grads["conv_w"] = lax.dynamic_slice(gsum, (PACK_AT["conv_w"], kidx[0] * (D // NCHIP)), (CONV_K, D // NCHIP))

    tok = started[-1]
    adamw("ada_w", tok)
    adamw("conv_w")
    two = lambda a, n: a.reshape(PACK_LEN.get(n, 1), D)
    small_out = _adamw_rows([(two(w[n], n), two(m[n], n), two(v[n], n)) for n in PACKED], gsum,
                            [PACK_AT[n] for n in PACKED], tok, "adamw_small")
    for n, quad in zip(PACKED, small_out):
        grads[n], delta[n], new_m[n], new_v[n] = (a.reshape(w[n].shape) for a in quad)
    last = small_out[-1][3]
    while landed:
        last = finish(tok)
    settle(last)
    finish()

    outs = [loss, dx[None]]
    for d in (grads, delta, new_m, new_v):
        outs += [d[n] for n in WEIGHTS]
    return tuple(outs)


def kernel(x, c, ada_w, ada_b, norm_ffn1, ffn1_w_in, ffn1_w_out, norm_mix, mix_w_in, hgrn_lb, hgrn_g, hgrn_w_o, conv_w, conv_b, conv_ln_g, conv_ln_b, conv_w_o, mix_w_out, norm_ffn2, ffn2_w_in, ffn2_w_out, norm_final, loss_target, m_ada_w, m_ada_b, m_norm_ffn1, m_ffn1_w_in, m_ffn1_w_out, m_norm_mix, m_mix_w_in, m_hgrn_lb, m_hgrn_g, m_hgrn_w_o, m_conv_w, m_conv_b, m_conv_ln_g, m_conv_ln_b, m_conv_w_o, m_mix_w_out, m_norm_ffn2, m_ffn2_w_in, m_ffn2_w_out, m_norm_final, v_ada_w, v_ada_b, v_norm_ffn1, v_ffn1_w_in, v_ffn1_w_out, v_norm_mix, v_mix_w_in, v_hgrn_lb, v_hgrn_g, v_hgrn_w_o, v_conv_w, v_conv_b, v_conv_ln_g, v_conv_ln_b, v_conv_w_o, v_mix_w_out, v_norm_ffn2, v_ffn2_w_in, v_ffn2_w_out, v_norm_final):
    w = dict(ada_w=ada_w, ada_b=ada_b, norm_ffn1=norm_ffn1, ffn1_w_in=ffn1_w_in, ffn1_w_out=ffn1_w_out,
             norm_mix=norm_mix, mix_w_in=mix_w_in, hgrn_lb=hgrn_lb, hgrn_g=hgrn_g, hgrn_w_o=hgrn_w_o, conv_w=conv_w,
             conv_b=conv_b, conv_ln_g=conv_ln_g, conv_ln_b=conv_ln_b, conv_w_o=conv_w_o, mix_w_out=mix_w_out,
             norm_ffn2=norm_ffn2, ffn2_w_in=ffn2_w_in, ffn2_w_out=ffn2_w_out, norm_final=norm_final)
    m = dict(ada_w=m_ada_w, ada_b=m_ada_b, norm_ffn1=m_norm_ffn1, ffn1_w_in=m_ffn1_w_in, ffn1_w_out=m_ffn1_w_out,
             norm_mix=m_norm_mix, mix_w_in=m_mix_w_in, hgrn_lb=m_hgrn_lb, hgrn_g=m_hgrn_g, hgrn_w_o=m_hgrn_w_o,
             conv_w=m_conv_w, conv_b=m_conv_b, conv_ln_g=m_conv_ln_g, conv_ln_b=m_conv_ln_b, conv_w_o=m_conv_w_o,
             mix_w_out=m_mix_w_out, norm_ffn2=m_norm_ffn2, ffn2_w_in=m_ffn2_w_in, ffn2_w_out=m_ffn2_w_out,
             norm_final=m_norm_final)
    v = dict(ada_w=v_ada_w, ada_b=v_ada_b, norm_ffn1=v_norm_ffn1, ffn1_w_in=v_ffn1_w_in, ffn1_w_out=v_ffn1_w_out,
             norm_mix=v_norm_mix, mix_w_in=v_mix_w_in, hgrn_lb=v_hgrn_lb, hgrn_g=v_hgrn_g, hgrn_w_o=v_hgrn_w_o,
             conv_w=v_conv_w, conv_b=v_conv_b, conv_ln_g=v_conv_ln_g, conv_ln_b=v_conv_ln_b, conv_w_o=v_conv_w_o,
             mix_w_out=v_mix_w_out, norm_ffn2=v_norm_ffn2, ffn2_w_in=v_ffn2_w_in, ffn2_w_out=v_ffn2_w_out,
             norm_final=v_norm_final)
    return _step(w, m, v, x, c, loss_target)
```
